```python
import math
import jax, jax.numpy as jnp
from jax import lax
import numpy as np

D_MODEL = 1024
BATCH = 8
SEQ = 2048
DEPTH = 1

D_SSM = 512
SSM_GROUP_WIDTH = 16
SSM_GROUPS = D_SSM // SSM_GROUP_WIDTH
SSM_STATE = 64
DT_MIN = 0.001
DT_MAX = 0.1
D_CONV = 512
CONV_WIDTH = 31
D_IN = D_SSM + 2 * D_CONV + 2 * D_MODEL
N_GROUPS_MOE = 4
EXPERTS_PER_GROUP = 8
N_EXPERTS = N_GROUPS_MOE * EXPERTS_PER_GROUP
TOPK_IN_GROUP = 2
D_EXPERT = 512
ROW_BLOCK = 128
D_PLE = 256
EPS = 1e-6

kernel_name = "hybrid_s5_conformer_hiermoe_block"


def rmsnorm(x, g):
    x32 = x.astype(jnp.float32)
    y = x32 * lax.rsqrt(jnp.mean(x32 * x32, axis=-1, keepdims=True) + EPS)
    return (y * g.astype(jnp.float32)).astype(x.dtype)


def layernorm(x, g, b):
    x32 = x.astype(jnp.float32)
    mu = jnp.mean(x32, axis=-1, keepdims=True)
    var = jnp.mean(jnp.square(x32 - mu), axis=-1, keepdims=True)
    y = (x32 - mu) * lax.rsqrt(var + EPS)
    return (y * g.astype(jnp.float32) + b.astype(jnp.float32)).astype(x.dtype)


def _complex_affine_combine(earlier, later):
    a_re, a_im, b_re, b_im = earlier
    c_re, c_im, d_re, d_im = later
    n_a_re = c_re * a_re - c_im * a_im
    n_a_im = c_re * a_im + c_im * a_re
    n_b_re = c_re * b_re - c_im * b_im + d_re
    n_b_im = c_re * b_im + c_im * b_re + d_im
    return (n_a_re, n_a_im, n_b_re, n_b_im)


def s5_ssm(u, a_re, a_im, log_dt, b_re, b_im, c_re, c_im, d):
    bsz, seq, _ = u.shape
    u32 = u.astype(jnp.float32).reshape(bsz, seq, SSM_GROUPS, SSM_GROUP_WIDTH)
    ar = a_re.astype(jnp.float32)
    ai = a_im.astype(jnp.float32)
    dt = jnp.exp(log_dt.astype(jnp.float32))[:, None]
    mag = jnp.exp(ar * dt)
    lam_re = mag * jnp.cos(ai * dt)
    lam_im = mag * jnp.sin(ai * dt)
    den = ar * ar + ai * ai
    nr = lam_re - 1.0
    ni = lam_im
    z_re = (nr * ar + ni * ai) / den
    z_im = (ni * ar - nr * ai) / den
    br = b_re.astype(jnp.float32)
    bi = b_im.astype(jnp.float32)
    bbar_re = z_re[..., None] * br - z_im[..., None] * bi
    bbar_im = z_re[..., None] * bi + z_im[..., None] * br
    bu_re = jnp.einsum('blgc,gnc->blgn', u32, bbar_re)
    bu_im = jnp.einsum('blgc,gnc->blgn', u32, bbar_im)
    lam_re_b = jnp.broadcast_to(lam_re, bu_re.shape)
    lam_im_b = jnp.broadcast_to(lam_im, bu_im.shape)
    _, _, s_re, s_im = lax.associative_scan(
        _complex_affine_combine, (lam_re_b, lam_im_b, bu_re, bu_im), axis=1)
    y = (jnp.einsum('blgn,gcn->blgc', s_re, c_re.astype(jnp.float32))
         - jnp.einsum('blgn,gcn->blgc', s_im, c_im.astype(jnp.float32)))
    y = y.reshape(bsz, seq, D_SSM) + d.astype(jnp.float32) * u32.reshape(bsz, seq, D_SSM)
    return y.astype(u.dtype)


def conformer_conv(v, dw, dw_b, ln_g, ln_b, w_pw_out):
    val, gate = jnp.split(v, 2, axis=-1)
    z = val * jax.nn.sigmoid(gate)
    z = lax.conv_general_dilated(
        z, dw[:, None, :].astype(z.dtype), window_strides=(1,),
        padding=[(CONV_WIDTH - 1, 0)],
        dimension_numbers=('NWC', 'WIO', 'NWC'),
        feature_group_count=D_CONV) + dw_b
    z = layernorm(z, ln_g, ln_b)
    z = jax.nn.silu(z)
    return z @ w_pw_out


def hierarchical_moe(h, w_rg, b_rg, w_re, b_re, w_gate, w_up, w_down):
    bsz, seq, dm = h.shape
    n_tok = bsz * seq
    ht = h.reshape(n_tok, dm)
    g_logits = (ht @ w_rg + b_rg).astype(jnp.float32)
    g_prob = jax.nn.softmax(g_logits, axis=-1)
    g_sel = jnp.argmax(g_logits, axis=-1).astype(jnp.int32)
    p_g = jnp.take_along_axis(g_prob, g_sel[:, None], axis=1)[:, 0]
    e_logits = (ht @ w_re + b_re).astype(jnp.float32).reshape(n_tok, N_GROUPS_MOE, EXPERTS_PER_GROUP)
    e_sel_logits = jnp.take_along_axis(e_logits, g_sel[:, None, None], axis=1)[:, 0]
    top_v, top_j = lax.top_k(e_sel_logits, TOPK_IN_GROUP)
    wts = jax.nn.softmax(top_v, axis=-1) * p_g[:, None]
    eid = (g_sel[:, None] * EXPERTS_PER_GROUP + top_j).reshape(-1).astype(jnp.int32)
    tok = jnp.repeat(jnp.arange(n_tok, dtype=jnp.int32), TOPK_IN_GROUP)
    wflat = wts.reshape(-1)
    n_assign = n_tok * TOPK_IN_GROUP

    order = jnp.argsort(eid)
    e_sorted = eid[order]
    counts = jnp.bincount(eid, length=N_EXPERTS).astype(jnp.int32)
    starts = jnp.cumsum(counts) - counts
    pcounts = (counts + ROW_BLOCK - 1) // ROW_BLOCK * ROW_BLOCK
    pends = jnp.cumsum(pcounts)
    pstarts = pends - pcounts
    dest = pstarts[e_sorted] + jnp.arange(n_assign, dtype=jnp.int32) - starts[e_sorted]
    n_blocks = (n_assign + N_EXPERTS * (ROW_BLOCK - 1) + ROW_BLOCK - 1) // ROW_BLOCK
    n_rows = n_blocks * ROW_BLOCK
    row_tok = jnp.full((n_rows,), n_tok, jnp.int32).at[dest].set(tok[order])
    row_w = jnp.zeros((n_rows,), jnp.float32).at[dest].set(wflat[order])
    blk_exp = jnp.minimum(
        jnp.searchsorted(pends, jnp.arange(n_blocks, dtype=jnp.int32) * ROW_BLOCK, side='right'),
        N_EXPERTS - 1).astype(jnp.int32)
    x_pad = jnp.concatenate([ht, jnp.zeros((1, dm), ht.dtype)], axis=0)
    xr = x_pad[row_tok].reshape(n_blocks, ROW_BLOCK, dm)

    def expert_block(args):
        xb, e = args
        return (jax.nn.silu(xb @ w_gate[e]) * (xb @ w_up[e])) @ w_down[e]

    yr = lax.map(expert_block, (xr, blk_exp)).reshape(n_rows, dm)
    out = jnp.zeros((n_tok, dm), h.dtype).at[row_tok].add(
        yr * row_w[:, None].astype(h.dtype), mode='drop')
    return out.reshape(bsz, seq, dm)


def setup_inputs(seed: int = 0) -> dict:
    key = jax.random.key(seed)
    ks = jax.random.split(key, 40)
    f32 = jnp.float32
    L, D = DEPTH, D_MODEL
    G, N, C = SSM_GROUPS, SSM_STATE, SSM_GROUP_WIDTH

    def nrm(k, shape, scale):
        return jax.random.normal(k, shape, f32) * scale

    def gain(k, shape):
        return 1.0 + 0.02 * jax.random.normal(k, shape, f32)

    a_re = -0.5 + 0.01 * jax.random.normal(ks[3], (L, G, N), f32)
    a_im = (math.pi * jnp.arange(N, dtype=f32))[None, None, :] + 0.01 * jax.random.normal(ks[4], (L, G, N), f32)
    log_dt = jax.random.uniform(ks[5], (L, G), f32, math.log(DT_MIN), math.log(DT_MAX))
    return {
        "x": jax.random.normal(ks[0], (BATCH, SEQ, D), f32),
        "p": jax.random.normal(ks[1], (DEPTH, BATCH, SEQ, D_PLE), f32),
        "g_mix": gain(ks[2], (L, D)),
        "w_in": nrm(ks[6], (L, D, D_IN), D ** -0.5),
        "b_gate": nrm(ks[7], (L, 2 * D), 0.02),
        "ssm_a_re": a_re,
        "ssm_a_im": a_im,
        "ssm_log_dt": log_dt,
        "ssm_b_re": nrm(ks[8], (L, G, N, C), (2 * C) ** -0.5),
        "ssm_b_im": nrm(ks[9], (L, G, N, C), (2 * C) ** -0.5),
        "ssm_c_re": nrm(ks[10], (L, G, C, N), N ** -0.5),
        "ssm_c_im": nrm(ks[11], (L, G, C, N), N ** -0.5),
        "ssm_d": nrm(ks[12], (L, D_SSM), 1.0),
        "w_glu": nrm(ks[13], (L, D_SSM, 2 * D), D_SSM ** -0.5),
        "conv_dw": nrm(ks[14], (L, CONV_WIDTH, D_CONV), CONV_WIDTH ** -0.5),
        "conv_dw_b": nrm(ks[15], (L, D_CONV), 0.02),
        "conv_ln_g": gain(ks[16], (L, D_CONV)),
        "conv_ln_b": nrm(ks[17], (L, D_CONV), 0.02),
        "w_conv_out": nrm(ks[18], (L, D_CONV, D), D_CONV ** -0.5),
        "w_out": nrm(ks[19], (L, D, D), D ** -0.5),
        "g_moe": gain(ks[20], (L, D)),
        "w_router_group": nrm(ks[21], (L, D, N_GROUPS_MOE), D ** -0.5),
        "b_router_group": nrm(ks[22], (L, N_GROUPS_MOE), 0.01),
        "w_router_expert": nrm(ks[23], (L, D, N_EXPERTS), D ** -0.5),
        "b_router_expert": nrm(ks[24], (L, N_EXPERTS), 0.01),
        "w_exp_gate": nrm(ks[25], (L, N_EXPERTS, D, D_EXPERT), D ** -0.5),
        "w_exp_up": nrm(ks[26], (L, N_EXPERTS, D, D_EXPERT), D ** -0.5),
        "w_exp_down": nrm(ks[27], (L, N_EXPERTS, D_EXPERT, D), D_EXPERT ** -0.5),
        "g_ple": gain(ks[28], (L, D)),
        "w_ple_gate": nrm(ks[29], (L, D, D), D ** -0.5),
        "w_ple": nrm(ks[30], (L, D_PLE, D), D_PLE ** -0.5),
        "g_final": gain(ks[31], (D,)),
    }


def reference(x, p, g_mix, w_in, b_gate, ssm_a_re, ssm_a_im, ssm_log_dt, ssm_b_re, ssm_b_im,
              ssm_c_re, ssm_c_im, ssm_d, w_glu, conv_dw, conv_dw_b, conv_ln_g, conv_ln_b,
              w_conv_out, w_out, g_moe, w_router_group, b_router_group, w_router_expert,
              b_router_expert, w_exp_gate, w_exp_up, w_exp_down, g_ple, w_ple_gate, w_ple,
              g_final):
    for i in range(DEPTH):
        h = rmsnorm(x, g_mix[i])
        proj = h @ w_in[i]
        u_ssm = proj[..., :D_SSM]
        v_conv = proj[..., D_SSM:D_SSM + 2 * D_CONV]
        gates = proj[..., D_SSM + 2 * D_CONV:] + b_gate[i]
        gate_ssm, gate_conv = jnp.split(gates, 2, axis=-1)

        y = s5_ssm(u_ssm, ssm_a_re[i], ssm_a_im[i], ssm_log_dt[i], ssm_b_re[i], ssm_b_im[i],
                   ssm_c_re[i], ssm_c_im[i], ssm_d[i])
        z_val, z_gate = jnp.split(jax.nn.gelu(y) @ w_glu[i], 2, axis=-1)
        y_ssm = z_val * jax.nn.sigmoid(z_gate)

        y_conv = conformer_conv(v_conv, conv_dw[i], conv_dw_b[i], conv_ln_g[i], conv_ln_b[i],
                                w_conv_out[i])

        merged = jax.nn.sigmoid(gate_ssm) * y_ssm + jax.nn.sigmoid(gate_conv) * y_conv
        x = x + merged @ w_out[i]

        x = x + hierarchical_moe(rmsnorm(x, g_moe[i]), w_router_group[i], b_router_group[i],
                                 w_router_expert[i], b_router_expert[i], w_exp_gate[i],
                                 w_exp_up[i], w_exp_down[i])

        ple_gate = jax.nn.sigmoid(rmsnorm(x, g_ple[i]) @ w_ple_gate[i])
        x = x + ple_gate * (p[i] @ w_ple[i])
    return rmsnorm(x, g_final)
```

```python
import functools
import math

import jax
import jax.numpy as jnp
from jax import lax
from jax.experimental import pallas as pl
from jax.experimental.pallas import tpu as pltpu

F32 = jnp.float32
BF16 = jnp.bfloat16
U32 = jnp.uint32
I32 = jnp.int32

D_MODEL = 1024
BATCH = 8
SEQ = 2048
N_TOK = BATCH * SEQ
D_SSM = 512
SSM_GROUP_WIDTH = 16
SSM_GROUPS = 32
SSM_STATE = 64
D_CONV = 512
CONV_WIDTH = 31
D_IN = D_SSM + 2 * D_CONV + 2 * D_MODEL
N_GROUPS_MOE = 4
EXPERTS_PER_GROUP = 8
N_EXPERTS = 32
D_EXPERT = 512
D_PLE = 256
EPS = 1e-6

SUBLANES = 8
LANES = 128
assert BATCH == SUBLANES

TM = 512
SB = 256
NSB = TM // SB
Q = 2
N_SLAB = D_SSM // LANES
GROUPS_PER_SLAB = SSM_GROUPS // N_SLAB
ROWS_Z = TM // Q
STATE_LANES = 2 * GROUPS_PER_SLAB * SSM_STATE
HALO = (CONV_WIDTH - 1) * BATCH

LANE_GRP0 = 0
LANE_EXP0 = 32
REC_EID0, REC_EID1, REC_W0, REC_W1, REC_RANK0, REC_RANK1 = 0, 1, 2, 3, 4, 5

BM = 256
N_BLK = (2 * N_TOK + N_EXPERTS * (BM - 1) + BM - 1) // BM
N_ROWS = N_BLK * BM
TD = 512
HALF = D_MODEL // 2
ROW_TILE = (HALF // LANES, LANES)

VMEM_LIMIT = 56 * 1024 * 1024


def _const_spec(shape):
    n = len(shape)
    return pl.BlockSpec(shape, lambda *_: (0,) * n, pipeline_mode=pl.Buffered(1))


def _rms(x, g):
    ms = jnp.mean(x * x, axis=-1, keepdims=True)
    return x * lax.rsqrt(ms + EPS) * g


def _pack_bf16_pair(lo, hi):
    ulo = lax.bitcast_convert_type(lo.astype(BF16).astype(F32), U32)
    uhi = lax.bitcast_convert_type(hi.astype(BF16).astype(F32), U32)
    return (ulo >> 16) | (uhi & jnp.uint32(0xFFFF0000))


def _unpack_bf16_pair(w):
    lo = lax.bitcast_convert_type(w << 16, F32)
    hi = lax.bitcast_convert_type(w & jnp.uint32(0xFFFF0000), F32)
    return lo, hi


def _mixer_kernel(x_ref, gmix_ref, win_ref, bgate_ref, mp_ref, r_ref, are_ref, aim_ref, d_ref,
                  wglu_ref, dw_ref, dwb_ref, lng_ref, lnb_ref, wco_ref, wout_ref, gmoe_ref,
                  wr1_ref, wr2_ref, br_ref,
                  x1_ref, h2p_ref, rec_ref, cnt_ref,
                  h_scr, u_scr, y_scr, yi_scr, xs_scr, z_scr, s_scr, cnt_scr):
    step = pl.program_id(0)

    @pl.when(step == 0)
    def _init():
        z_scr[0:HALO, :] = jnp.zeros((HALO, D_CONV), F32)
        s_scr[...] = jnp.zeros(s_scr.shape, F32)
        cnt_scr[...] = jnp.zeros(cnt_scr.shape, F32)

    def phase_a(r, carry):
        rows = pl.ds(pl.multiple_of(r * SB, SB), SB)
        h = _rms(x_ref[rows, :], gmix_ref[...]).astype(BF16)
        h_scr[rows, :] = h
        u = jnp.dot(h, win_ref[:, 0:D_SSM], preferred_element_type=F32)
        u_scr[pl.ds(r * (SB // (Q * SUBLANES)), SB // (Q * SUBLANES))] = u.reshape(
            SB // (Q * SUBLANES), Q, SUBLANES, D_SSM)
        return carry

    lax.fori_loop(0, NSB, phase_a, 0)

    for s in range(N_SLAB):
        lanes = slice(s * LANES, (s + 1) * LANES)
        z = jnp.concatenate(
            [u_scr[:, i, :, lanes].reshape(ROWS_Z, LANES) for i in range(Q)], axis=1).astype(BF16)
        xp = jnp.dot(z, mp_ref[s], preferred_element_type=F32)
        yi_scr[s] = xp[:, 0:Q * LANES]
        xs_scr[s] = xp[:, Q * LANES:]

    half = STATE_LANES // 2
    for s in range(N_SLAB):
        a_re = jnp.broadcast_to(are_ref[s:s + 1, :], (SUBLANES, half))
        a_im = jnp.broadcast_to(aim_ref[s:s + 1, :], (SUBLANES, half))

        def scan_body(k, carry, s=s, a_re=a_re, a_im=a_im):
            s_re, s_im = carry
            rows = pl.ds(pl.multiple_of(k * SUBLANES, SUBLANES), SUBLANES)
            x_re = xs_scr[s, rows, 0:half]
            x_im = xs_scr[s, rows, half:]
            xs_scr[s, rows, 0:half] = s_re
            xs_scr[s, rows, half:] = s_im
            n_re = a_re * s_re - a_im * s_im + x_re
            n_im = a_re * s_im + a_im * s_re + x_im
            return n_re, n_im

        s_re, s_im = lax.fori_loop(0, ROWS_Z // SUBLANES, scan_body,
                                   (s_scr[s, :, 0:half], s_scr[s, :, half:]), unroll=4)
        s_scr[s, :, 0:half] = s_re
        s_scr[s, :, half:] = s_im

    for s in range(N_SLAB):
        lanes = slice(s * LANES, (s + 1) * LANES)
        y_tot = yi_scr[s] + jnp.dot(xs_scr[s].astype(BF16), r_ref[s], preferred_element_type=F32)
        for j in range(Q):
            y_scr[:, j, :, lanes] = y_tot[:, j * LANES:(j + 1) * LANES].reshape(
                ROWS_Z // SUBLANES, SUBLANES, LANES)

    lane = lax.broadcasted_iota(I32, (1, LANES), 1).astype(F32)
    grp_mask = lane < float(N_GROUPS_MOE)
    exp_lane = (lane >= float(LANE_EXP0)) & (lane < float(LANE_EXP0 + N_EXPERTS))
    lane_grp = jnp.floor((lane - float(LANE_EXP0)) * (1.0 / EXPERTS_PER_GROUP))
    tri = (lax.broadcasted_iota(I32, (SB, SB), 0) > lax.broadcasted_iota(I32, (SB, SB), 1)).astype(BF16)
    neg_inf = float("-inf")
    big = float(4 * LANES)

    def phase_c(r, carry):
        rows = pl.ds(pl.multiple_of(r * SB, SB), SB)
        crow = pl.ds(r * (SB // (Q * SUBLANES)), SB // (Q * SUBLANES))
        h = h_scr[rows, :]

        v = jnp.dot(h, win_ref[:, D_SSM:D_SSM + 2 * D_CONV], preferred_element_type=F32)
        zc = v[:, 0:D_CONV] * jax.nn.sigmoid(v[:, D_CONV:])
        z_scr[pl.ds(pl.multiple_of(HALO + r * SB, SUBLANES), SB), :] = zc
        acc = jnp.broadcast_to(dwb_ref[...], (SB, D_CONV))
        for j in range(CONV_WIDTH):
            acc = acc + dw_ref[j:j + 1, :] * z_scr[pl.ds(pl.multiple_of(r * SB + j * BATCH, SUBLANES), SB), :]
        mu = jnp.mean(acc, axis=-1, keepdims=True)
        cen = acc - mu
        var = jnp.mean(cen * cen, axis=-1, keepdims=True)
        ln = cen * lax.rsqrt(var + EPS) * lng_ref[...] + lnb_ref[...]
        y_conv = jnp.dot(jax.nn.silu(ln).astype(BF16), wco_ref[...], preferred_element_type=F32)

        g0 = D_SSM + 2 * D_CONV
        gate_ssm = jnp.dot(h, win_ref[:, g0:g0 + D_MODEL], preferred_element_type=F32) \
            + bgate_ref[:, 0:D_MODEL]
        gate_conv = jnp.dot(h, win_ref[:, g0 + D_MODEL:], preferred_element_type=F32) \
            + bgate_ref[:, D_MODEL:]

        u = u_scr[crow].reshape(SB, D_SSM)
        y = y_scr[crow].reshape(SB, D_SSM) + d_ref[...] * u
        zz = jnp.dot(jax.nn.gelu(y).astype(BF16), wglu_ref[...], preferred_element_type=F32)
        y_ssm = zz[:, 0:D_MODEL] * jax.nn.sigmoid(zz[:, D_MODEL:])

        merged = jax.nn.sigmoid(gate_ssm) * y_ssm + jax.nn.sigmoid(gate_conv) * y_conv
        x1 = x_ref[rows, :] + jnp.dot(merged.astype(BF16), wout_ref[...], preferred_element_type=F32)
        x1_ref[rows, :] = x1

        h2 = _rms(x1, gmoe_ref[...])
        h2p_ref[rows] = _pack_bf16_pair(h2[:, 0:HALF], h2[:, HALF:]).reshape((SB,) + ROW_TILE)

        h2_hi = h2.astype(BF16)
        h2_lo = (h2 - h2_hi.astype(F32)).astype(BF16)
        l1 = jnp.dot(h2_hi, wr1_ref[...], preferred_element_type=F32)
        l2 = jnp.dot(h2_lo, wr2_ref[...], preferred_element_type=F32)
        logits = l1[:, 0:LANES] + l1[:, LANES:] + l2 + br_ref[...]

        lg = jnp.where(grp_mask, logits, neg_inf)
        g_max = jnp.max(lg, axis=-1, keepdims=True)
        g_sel = jnp.min(jnp.where(lg == g_max, lane, big), axis=-1, keepdims=True)
        p_g = 1.0 / jnp.sum(jnp.where(grp_mask, jnp.exp(logits - g_max), 0.0), axis=-1, keepdims=True)
        le = jnp.where(exp_lane & (lane_grp == g_sel), logits, neg_inf)
        m1 = jnp.max(le, axis=-1, keepdims=True)
        i1 = jnp.min(jnp.where(le == m1, lane, big), axis=-1, keepdims=True)
        le2 = jnp.where(lane == i1, neg_inf, le)
        m2 = jnp.max(le2, axis=-1, keepdims=True)
        i2 = jnp.min(jnp.where(le2 == m2, lane, big), axis=-1, keepdims=True)
        e2 = jnp.exp(m2 - m1)
        den = 1.0 + e2
        w_a = (1.0 / den) * p_g
        w_b = (e2 / den) * p_g

        sel1 = lane == i1
        sel2 = lane == i2
        onehot = jnp.where(sel1 | sel2, 1.0, 0.0)
        prefix = jnp.dot(tri, onehot.astype(BF16), preferred_element_type=F32) + cnt_scr[...]
        rank_a = jnp.sum(jnp.where(sel1, prefix, 0.0), axis=-1, keepdims=True)
        rank_b = jnp.sum(jnp.where(sel2, prefix, 0.0), axis=-1, keepdims=True)
        cnt_scr[...] = cnt_scr[...] + jnp.sum(onehot, axis=0, keepdims=True)

        rec = jnp.where(lane == float(REC_EID0), i1 - float(LANE_EXP0), 0.0)
        rec = jnp.where(lane == float(REC_EID1), i2 - float(LANE_EXP0), rec)
        rec = jnp.where(lane == float(REC_W0), w_a, rec)
        rec = jnp.where(lane == float(REC_W1), w_b, rec)
        rec = jnp.where(lane == float(REC_RANK0), rank_a, rec)
        rec = jnp.where(lane == float(REC_RANK1), rank_b, rec)
        rec_ref[rows, :] = rec
        return carry

    lax.fori_loop(0, NSB, phase_c, 0)

    z_scr[0:HALO, :] = z_scr[TM:TM + HALO, :]
    cnt_ref[...] = cnt_scr[...]


def _mixer(xt, gmix, win, bgate, mp, rmat, a_re, a_im, dvec, wglu, dw, dwb, lng, lnb, wco, wout,
           gmoe, wr1, wr2, br):
    grid = (N_TOK // TM,)
    row_spec = lambda w: pl.BlockSpec((TM, w), lambda i: (i, 0))
    in_specs = [
        row_spec(D_MODEL),
        _const_spec((1, D_MODEL)),
        _const_spec((D_MODEL, D_IN)),
        _const_spec((1, 2 * D_MODEL)),
        _const_spec(mp.shape),
        _const_spec(rmat.shape),
        _const_spec(a_re.shape),
        _const_spec(a_im.shape),
        _const_spec((1, D_SSM)),
        _const_spec((D_SSM, 2 * D_MODEL)),
        _const_spec((CONV_WIDTH, D_CONV)),
        _const_spec((1, D_CONV)),
        _const_spec((1, D_CONV)),
        _const_spec((1, D_CONV)),
        _const_spec((D_CONV, D_MODEL)),
        _const_spec((D_MODEL, D_MODEL)),
        _const_spec((1, D_MODEL)),
        _const_spec((D_MODEL, 2 * LANES)),
        _const_spec((D_MODEL, LANES)),
        _const_spec((1, LANES)),
    ]
    out_specs = [
        row_spec(D_MODEL),
        pl.BlockSpec((TM,) + ROW_TILE, lambda i: (i, 0, 0)),
        row_spec(LANES),
        pl.BlockSpec((1, LANES), lambda i: (0, 0)),
    ]
    out_shape = [
        jax.ShapeDtypeStruct((N_TOK, D_MODEL), F32),
        jax.ShapeDtypeStruct((N_TOK,) + ROW_TILE, U32),
        jax.ShapeDtypeStruct((N_TOK, LANES), F32),
        jax.ShapeDtypeStruct((1, LANES), F32),
    ]
    scratch = [
        pltpu.VMEM((TM, D_MODEL), BF16),
        pltpu.VMEM((ROWS_Z // SUBLANES, Q, SUBLANES, D_SSM), F32),
        pltpu.VMEM((ROWS_Z // SUBLANES, Q, SUBLANES, D_SSM), F32),
        pltpu.VMEM((N_SLAB, ROWS_Z, Q * LANES), F32),
        pltpu.VMEM((N_SLAB, ROWS_Z, STATE_LANES), F32),
        pltpu.VMEM((HALO + TM, D_CONV), F32),
        pltpu.VMEM((N_SLAB, SUBLANES, STATE_LANES), F32),
        pltpu.VMEM((1, LANES), F32),
    ]
    return pl.pallas_call(
        _mixer_kernel,
        grid=grid,
        in_specs=in_specs,
        out_specs=out_specs,
        out_shape=out_shape,
        scratch_shapes=scratch,
        compiler_params=pltpu.CompilerParams(
            dimension_semantics=("arbitrary",), vmem_limit_bytes=VMEM_LIMIT),
        name="mixer",
    )(xt, gmix, win, bgate, mp, rmat, a_re, a_im, dvec, wglu, dw, dwb, lng, lnb, wco, wout,
      gmoe, wr1, wr2, br)


def _cmul(a, b):
    return a[0] * b[0] - a[1] * b[1], a[0] * b[1] + a[1] * b[0]


def _ssm_matrices(a_re, a_im, log_dt, b_re, b_im, c_re, c_im):
    hp = lax.Precision.HIGHEST
    dt = jnp.exp(log_dt)[:, None]
    mag = jnp.exp(a_re * dt)
    lam = (mag * jnp.cos(a_im * dt), mag * jnp.sin(a_im * dt))
    den = a_re * a_re + a_im * a_im
    nr = lam[0] - 1.0
    ni = lam[1]
    z_re = (nr * a_re + ni * a_im) / den
    z_im = (ni * a_re - nr * a_im) / den
    bbar = (z_re[..., None] * b_re - z_im[..., None] * b_im,
            z_re[..., None] * b_im + z_im[..., None] * b_re)
    pw = [(jnp.ones_like(lam[0]), jnp.zeros_like(lam[0])), lam]
    for _ in range(2, Q + 1):
        pw.append(_cmul(pw[-1], lam))
    e = [(c_re * p[0][:, None, :] - c_im * p[1][:, None, :],
          c_re * p[1][:, None, :] + c_im * p[0][:, None, :]) for p in pw]
    k = [jnp.einsum('gcn,gnd->gcd', e[m][0], bbar[0], precision=hp)
         - jnp.einsum('gcn,gnd->gcd', e[m][1], bbar[1], precision=hp) for m in range(Q)]
    eye = jnp.eye(GROUPS_PER_SLAB, dtype=F32)
    split = lambda t: t.reshape((N_SLAB, GROUPS_PER_SLAB) + t.shape[1:])
    zero_k = jnp.zeros_like(k[0])
    kb = jnp.stack([jnp.stack([split(jnp.swapaxes(k[j - i] if j >= i else zero_k, 1, 2))
                               for j in range(Q)]) for i in range(Q)])
    m_mat = jnp.einsum('ijsgdc,gh->sigdjhc', kb, eye).reshape(N_SLAB, Q * LANES, Q * LANES)
    f = [_cmul((pw[Q - 1 - i][0][..., None], pw[Q - 1 - i][1][..., None]), bbar) for i in range(Q)]
    p_parts = []
    for part in range(2):
        fs = jnp.stack([split(f[i][part]) for i in range(Q)])
        p_parts.append(jnp.einsum('isgnd,gh->sigdhn', fs, eye).reshape(N_SLAB, Q * LANES, STATE_LANES // 2))
    p_mat = jnp.concatenate(p_parts, axis=-1)
    r_parts = []
    for part, sign in ((0, 1.0), (1, -1.0)):
        es = jnp.stack([split(e[j + 1][part]) for j in range(Q)])
        r_parts.append(sign * jnp.einsum('jsgcn,gh->shnjgc', es, eye).reshape(
            N_SLAB, STATE_LANES // 2, Q * LANES))
    r_mat = jnp.concatenate(r_parts, axis=1)
    mp = jnp.concatenate([m_mat, p_mat], axis=-1).astype(BF16)
    a_q = pw[Q]
    return (mp, r_mat.astype(BF16),
            a_q[0].reshape(N_SLAB, STATE_LANES // 2), a_q[1].reshape(N_SLAB, STATE_LANES // 2))


def _router_weights(w_rg, b_rg, w_re, b_re):
    w = jnp.zeros((D_MODEL, LANES), F32)
    w = w.at[:, LANE_GRP0:LANE_GRP0 + N_GROUPS_MOE].set(w_rg)
    w = w.at[:, LANE_EXP0:LANE_EXP0 + N_EXPERTS].set(w_re)
    b = jnp.zeros((1, LANES), F32)
    b = b.at[0, LANE_GRP0:LANE_GRP0 + N_GROUPS_MOE].set(b_rg)
    b = b.at[0, LANE_EXP0:LANE_EXP0 + N_EXPERTS].set(b_re)
    w_hi = w.astype(BF16)
    w_lo = (w - w_hi.astype(F32)).astype(BF16)
    return jnp.concatenate([w_hi, w_lo], axis=1), w_hi, b


def _dispatch_kernel(zstart_ref, zlen_ref, nused_ref, dest_ref, h_ref, xs_ref, zero_scr, sem):
    step = pl.program_id(0)

    @pl.when(step == 0)
    def _zero_padding():
        zero_scr[...] = jnp.zeros(zero_scr.shape, U32)

        def pad_pieces(e, act):
            n = zlen_ref[e]
            piece = BM // 2
            while piece >= 1:
                pos = zstart_ref[e] + (n & ~(2 * piece - 1))
                copy = pltpu.make_async_copy(
                    zero_scr.at[pl.ds(0, piece)], xs_ref.at[pl.ds(pos, piece)], sem.at[0])
                pl.when((n & piece) != 0)(functools.partial(act, copy))
                piece //= 2

        def tail_copy(b):
            return pltpu.make_async_copy(zero_scr, xs_ref.at[pl.ds(b * BM, BM)], sem.at[0])

        def start_all(e, c):
            pad_pieces(e, lambda copy: copy.start())
            return c

        def wait_all(e, c):
            pad_pieces(e, lambda copy: copy.wait())
            return c

        lax.fori_loop(0, N_EXPERTS, start_all, 0)
        lax.fori_loop(nused_ref[0], N_BLK, lambda b, c: (tail_copy(b).start(), c)[1], 0)
        lax.fori_loop(0, N_EXPERTS, wait_all, 0)
        lax.fori_loop(nused_ref[0], N_BLK, lambda b, c: (tail_copy(b).wait(), c)[1], 0)

    def row_copy(r, j):
        return pltpu.make_async_copy(h_ref.at[r], xs_ref.at[dest_ref[0, 0, 2 * r + j]], sem.at[0])

    def issue(r, carry):
        row_copy(r, 0).start()
        row_copy(r, 1).start()
        return carry

    lax.fori_loop(0, TD, issue, 0, unroll=8)

    def drain(r, carry):
        row_copy(r, 0).wait()
        row_copy(r, 1).wait()
        return carry

    lax.fori_loop(0, TD, drain, 0, unroll=8)


def _dispatch(zstart, zlen, nused, dest, h2p):
    grid_spec = pltpu.PrefetchScalarGridSpec(
        num_scalar_prefetch=3,
        grid=(N_TOK // TD,),
        in_specs=[
            pl.BlockSpec((1, 1, 2 * TD), lambda i, *_: (i, 0, 0), memory_space=pltpu.SMEM),
            pl.BlockSpec((TD,) + ROW_TILE, lambda i, *_: (i, 0, 0)),
        ],
        out_specs=pl.BlockSpec(memory_space=pl.ANY),
        scratch_shapes=[pltpu.VMEM((BM,) + ROW_TILE, U32), pltpu.SemaphoreType.DMA((1,))],
    )
    return pl.pallas_call(
        _dispatch_kernel,
        grid_spec=grid_spec,
        out_shape=jax.ShapeDtypeStruct((N_ROWS,) + ROW_TILE, U32),
        compiler_params=pltpu.CompilerParams(dimension_semantics=("arbitrary",)),
        name="dispatch",
    )(zstart, zlen, nused, dest.reshape(N_TOK // TD, 1, 2 * TD), h2p)


def _expert_kernel(bexp_ref, nused_ref, xs_ref, wg_ref, wu_ref, wd_ref, ys_ref, wg_scr, wu_scr, wd_scr):
    i = pl.program_id(0)
    prev = bexp_ref[jnp.maximum(i - 1, 0)]
    fresh = (i == 0) | (bexp_ref[i] != prev)

    @pl.when(fresh)
    def _cast_weights():
        wg_scr[...] = wg_ref[0].astype(BF16)
        wu_scr[...] = wu_ref[0].astype(BF16)
        wd_scr[...] = wd_ref[0].astype(BF16)

    @pl.when(i < nused_ref[0])
    def _compute():
        lo, hi = _unpack_bf16_pair(xs_ref[...].reshape(BM, HALF))
        lo = lo.astype(BF16)
        hi = hi.astype(BF16)
        g = jnp.dot(lo, wg_scr[0:HALF, :], preferred_element_type=F32) \
            + jnp.dot(hi, wg_scr[HALF:, :], preferred_element_type=F32)
        u = jnp.dot(lo, wu_scr[0:HALF, :], preferred_element_type=F32) \
            + jnp.dot(hi, wu_scr[HALF:, :], preferred_element_type=F32)
        a = (jax.nn.silu(g) * u).astype(BF16)
        o = jnp.dot(a, wd_scr[...], preferred_element_type=F32)
        ys_ref[...] = _pack_bf16_pair(o[:, 0:HALF], o[:, HALF:]).reshape((BM,) + ROW_TILE)

    @pl.when(i >= nused_ref[0])
    def _unused():
        ys_ref[...] = jnp.zeros(ys_ref.shape, U32)


def _experts(bexp, nused, xs, wg, wu, wd):
    grid_spec = pltpu.PrefetchScalarGridSpec(
        num_scalar_prefetch=2,
        grid=(N_BLK,),
        in_specs=[
            pl.BlockSpec((BM,) + ROW_TILE, lambda i, be, nu: (jnp.minimum(i, nu[0] - 1), 0, 0)),
            pl.BlockSpec((1, D_MODEL, D_EXPERT), lambda i, be, nu: (be[i], 0, 0)),
            pl.BlockSpec((1, D_MODEL, D_EXPERT), lambda i, be, nu: (be[i], 0, 0)),
            pl.BlockSpec((1, D_EXPERT, D_MODEL), lambda i, be, nu: (be[i], 0, 0)),
        ],
        out_specs=pl.BlockSpec((BM,) + ROW_TILE, lambda i, be, nu: (i, 0, 0)),
        scratch_shapes=[
            pltpu.VMEM((D_MODEL, D_EXPERT), BF16),
            pltpu.VMEM((D_MODEL, D_EXPERT), BF16),
            pltpu.VMEM((D_EXPERT, D_MODEL), BF16),
        ],
    )
    return pl.pallas_call(
        _expert_kernel,
        grid_spec=grid_spec,
        out_shape=jax.ShapeDtypeStruct((N_ROWS,) + ROW_TILE, U32),
        compiler_params=pltpu.CompilerParams(
            dimension_semantics=("arbitrary",), vmem_limit_bytes=VMEM_LIMIT),
        name="experts",
    )(bexp, nused, xs, wg, wu, wd)


def _combine_kernel(dest_ref, x1_ref, rec_ref, p_ref, gple_ref, wpg_ref, wple_ref, gfin_ref, ys_ref,
                    out_ref, gat_scr, sem):
    def row_copy(r, j):
        return pltpu.make_async_copy(ys_ref.at[dest_ref[0, 0, 2 * r + j]], gat_scr.at[j, r], sem.at[0])

    def issue(r, carry):
        row_copy(r, 0).start()
        row_copy(r, 1).start()
        return carry

    lax.fori_loop(0, TD, issue, 0, unroll=8)

    ple = jnp.dot(p_ref[...].astype(BF16), wple_ref[...], preferred_element_type=F32)

    def drain(r, carry):
        row_copy(r, 0).wait()
        row_copy(r, 1).wait()
        return carry

    lax.fori_loop(0, TD, drain, 0, unroll=8)

    rec = rec_ref[...]
    w0 = rec[:, REC_W0:REC_W0 + 1]
    w1 = rec[:, REC_W1:REC_W1 + 1]
    lo0, hi0 = _unpack_bf16_pair(gat_scr[0].reshape(TD, HALF))
    lo1, hi1 = _unpack_bf16_pair(gat_scr[1].reshape(TD, HALF))
    moe = jnp.concatenate([lo0 * w0 + lo1 * w1, hi0 * w0 + hi1 * w1], axis=1)
    x2 = x1_ref[...] + moe
    gate = jax.nn.sigmoid(jnp.dot(_rms(x2, gple_ref[...]).astype(BF16), wpg_ref[...],
                                  preferred_element_type=F32))
    x3 = x2 + gate * ple
    out_ref[...] = _rms(x3, gfin_ref[...])


def _combine(dest, x1, rec, pt, gple, wpg, wple, gfin, ys):
    row_spec = lambda w: pl.BlockSpec((TD, w), lambda i: (i, 0))
    return pl.pallas_call(
        _combine_kernel,
        grid=(N_TOK // TD,),
        in_specs=[
            pl.BlockSpec((1, 1, 2 * TD), lambda i: (i, 0, 0), memory_space=pltpu.SMEM),
            row_spec(D_MODEL),
            row_spec(LANES),
            row_spec(D_PLE),
            _const_spec((1, D_MODEL)),
            _const_spec((D_MODEL, D_MODEL)),
            _const_spec((D_PLE, D_MODEL)),
            _const_spec((1, D_MODEL)),
            pl.BlockSpec(memory_space=pl.ANY),
        ],
        out_specs=row_spec(D_MODEL),
        out_shape=jax.ShapeDtypeStruct((N_TOK, D_MODEL), F32),
        scratch_shapes=[pltpu.VMEM((2, TD) + ROW_TILE, U32), pltpu.SemaphoreType.DMA((1,))],
        compiler_params=pltpu.CompilerParams(
            dimension_semantics=("arbitrary",), vmem_limit_bytes=VMEM_LIMIT),
        name="combine",
    )(dest.reshape(N_TOK // TD, 1, 2 * TD), x1, rec, pt, gple, wpg, wple, gfin, ys)


def kernel(x, p, g_mix, w_in, b_gate, ssm_a_re, ssm_a_im, ssm_log_dt, ssm_b_re, ssm_b_im, ssm_c_re,
           ssm_c_im, ssm_d, w_glu, conv_dw, conv_dw_b, conv_ln_g, conv_ln_b, w_conv_out, w_out, g_moe,
           w_router_group, b_router_group, w_router_expert, b_router_expert, w_exp_gate, w_exp_up,
           w_exp_down, g_ple, w_ple_gate, w_ple, g_final):
    assert x.shape == (BATCH, SEQ, D_MODEL) and p.shape == (1, BATCH, SEQ, D_PLE)
    row = lambda v: v.reshape(1, -1)
    xt = jnp.transpose(x, (1, 0, 2)).reshape(N_TOK, D_MODEL)
    pt = jnp.transpose(p[0], (1, 0, 2)).reshape(N_TOK, D_PLE)

    mp, rmat, a_re, a_im = _ssm_matrices(ssm_a_re[0], ssm_a_im[0], ssm_log_dt[0], ssm_b_re[0],
                                         ssm_b_im[0], ssm_c_re[0], ssm_c_im[0])
    wr1, wr2, br = _router_weights(w_router_group[0], b_router_group[0], w_router_expert[0],
                                   b_router_expert[0])
    x1, h2p, rec, cnt = _mixer(
        xt, row(g_mix[0]), w_in[0].astype(BF16), row(b_gate[0]), mp, rmat, a_re, a_im, row(ssm_d[0]),
        w_glu[0].astype(BF16), conv_dw[0], row(conv_dw_b[0]), row(conv_ln_g[0]), row(conv_ln_b[0]),
        w_conv_out[0].astype(BF16), w_out[0].astype(BF16), row(g_moe[0]), wr1, wr2, br)

    counts = cnt[0, LANE_EXP0:LANE_EXP0 + N_EXPERTS].astype(I32)
    pcounts = (counts + BM - 1) // BM * BM
    pends = jnp.cumsum(pcounts)
    pstarts = pends - pcounts
    eid = rec[:, REC_EID0:REC_EID1 + 1].astype(I32)
    rank = rec[:, REC_RANK0:REC_RANK1 + 1].astype(I32)
    dest = pstarts[eid] + rank
    nused = (pends[-1] // BM).astype(I32)
    blk = jnp.arange(N_BLK, dtype=I32)
    bexp = jnp.minimum(jnp.searchsorted(pends, jnp.minimum(blk, nused - 1) * BM, side='right'),
                       N_EXPERTS - 1).astype(I32)
    zstart = (pstarts + counts).astype(I32)
    zlen = (pcounts - counts).astype(I32)

    nused = nused.reshape(1)
    xs = _dispatch(zstart, zlen, nused, dest, h2p)
    ys = _experts(bexp, nused, xs, w_exp_gate[0], w_exp_up[0], w_exp_down[0])
    out_t = _combine(dest, x1, rec, pt, row(g_ple[0]), w_ple_gate[0].astype(BF16),
                     w_ple[0].astype(BF16), row(g_final), ys)
    return jnp.transpose(out_t.reshape(SEQ, BATCH, D_MODEL), (1, 0, 2))
```

```python
import functools

import jax
import jax.numpy as jnp
from jax import lax
from jax.experimental import pallas as pl
from jax.experimental.pallas import tpu as pltpu

F32 = jnp.float32
BF16 = jnp.bfloat16
U32 = jnp.uint32
I32 = jnp.int32

D_MODEL = 1024
BATCH = 8
SEQ = 2048
N_TOK = BATCH * SEQ
D_SSM = 512
SSM_GROUP_WIDTH = 16
SSM_GROUPS = 32
SSM_STATE = 64
D_CONV = 512
CONV_WIDTH = 31
D_IN = D_SSM + 2 * D_CONV + 2 * D_MODEL
N_GROUPS_MOE = 4
EXPERTS_PER_GROUP = 8
N_EXPERTS = 32
D_EXPERT = 512
D_PLE = 256
EPS = 1e-6

SUBLANES = 8
LANES = 128
assert BATCH == SUBLANES

TT = 64
TM = TT * BATCH
N_STEP = SEQ // TT
SB = 256
NSB = TM // SB
BPS = SB // TT
Q = 2
N_SLAB = D_SSM // LANES
GROUPS_PER_SLAB = SSM_GROUPS // N_SLAB
ROWS_Z = TM // Q
STATE_LANES = 2 * GROUPS_PER_SLAB * SSM_STATE
HALO = (CONV_WIDTH - 1) * BATCH
CHUNK_ROWS = SB // (Q * SUBLANES)

LANE_GRP0 = 0
LANE_EXP0 = 32
REC_EID0, REC_EID1, REC_W0, REC_W1, REC_RANK0, REC_RANK1 = 0, 1, 2, 3, 4, 5
REC_ROWS = 8

BM = 256
N_BLK = (2 * N_TOK + N_EXPERTS * (BM - 1) + BM - 1) // BM
N_ROWS = N_BLK * BM
HALF = D_MODEL // 2
ROW_TILE = (HALF // LANES, LANES)

VMEM_LIMIT = 56 * 1024 * 1024


def _const_spec(shape):
    n = len(shape)
    return pl.BlockSpec(shape, lambda *_: (0,) * n, pipeline_mode=pl.Buffered(1))


def _rms(x, g):
    ms = jnp.mean(x * x, axis=-1, keepdims=True)
    return x * lax.rsqrt(ms + EPS) * g


def _pack_bf16_pair(lo, hi):
    ulo = lax.bitcast_convert_type(lo.astype(BF16).astype(F32), U32)
    uhi = lax.bitcast_convert_type(hi.astype(BF16).astype(F32), U32)
    return (ulo >> 16) | (uhi & jnp.uint32(0xFFFF0000))


def _unpack_bf16_pair(w):
    lo = lax.bitcast_convert_type(w << 16, F32)
    hi = lax.bitcast_convert_type(w & jnp.uint32(0xFFFF0000), F32)
    return lo, hi


def _mixer_kernel(x_ref, gmix_ref, win_ref, bgate_ref, perm_ref, permt_ref, mp_ref, r_ref, are_ref,
                  aim_ref, d_ref, wglu_ref, dw_ref, dwb_ref, lng_ref, lnb_ref, wco_ref, wout_ref,
                  gmoe_ref, wr1_ref, wr2_ref, br_ref,
                  x1_ref, h2p_ref, rec_ref, rect_ref, cnt_ref,
                  hb_scr, ht_scr, u_scr, y_scr, yi_scr, xs_scr, z_scr, act_scr, actb_scr, rect_scr,
                  s_scr, cnt_scr):
    step = pl.program_id(0)

    @pl.when(step == 0)
    def _init():
        z_scr[0:HALO, :] = jnp.zeros((HALO, D_CONV), F32)
        s_scr[...] = jnp.zeros(s_scr.shape, F32)
        cnt_scr[...] = jnp.zeros(cnt_scr.shape, F32)

    def sub_rows(r):
        return pl.ds(pl.multiple_of(r * SB, SB), SB)

    def phase_a(r, carry):
        xb = x_ref[pl.ds(r * BPS, BPS)].reshape(SB, D_MODEL)
        hb_scr[sub_rows(r), :] = _rms(xb, gmix_ref[...]).astype(BF16)
        return carry

    lax.fori_loop(0, NSB, phase_a, 0)

    ht_scr[...] = jnp.dot(perm_ref[...], hb_scr[...], preferred_element_type=F32).astype(BF16)

    def phase_a3(r, carry):
        h = ht_scr[sub_rows(r), :]
        u = jnp.dot(h, win_ref[:, 0:D_SSM], preferred_element_type=F32)
        u_scr[pl.ds(r * CHUNK_ROWS, CHUNK_ROWS)] = u.reshape(CHUNK_ROWS, Q, SUBLANES, D_SSM)
        v = jnp.dot(h, win_ref[:, D_SSM:D_SSM + 2 * D_CONV], preferred_element_type=F32)
        zc = v[:, 0:D_CONV] * jax.nn.sigmoid(v[:, D_CONV:])
        z_scr[pl.ds(pl.multiple_of(HALO + r * SB, SUBLANES), SB), :] = zc
        return carry

    lax.fori_loop(0, NSB, phase_a3, 0)

    for s in range(N_SLAB):
        lanes = slice(s * LANES, (s + 1) * LANES)
        z = jnp.concatenate(
            [u_scr[:, i, :, lanes].reshape(ROWS_Z, LANES) for i in range(Q)], axis=1).astype(BF16)
        xp = jnp.dot(z, mp_ref[s], preferred_element_type=F32)
        yi_scr[s] = xp[:, 0:Q * LANES]
        xs_scr[s] = xp[:, Q * LANES:]

    half = STATE_LANES // 2
    for s in range(N_SLAB):
        a_re = jnp.broadcast_to(are_ref[s:s + 1, :], (SUBLANES, half))
        a_im = jnp.broadcast_to(aim_ref[s:s + 1, :], (SUBLANES, half))

        def scan_body(k, carry, s=s, a_re=a_re, a_im=a_im):
            s_re, s_im = carry
            rows = pl.ds(pl.multiple_of(k * SUBLANES, SUBLANES), SUBLANES)
            x_re = xs_scr[s, rows, 0:half]
            x_im = xs_scr[s, rows, half:]
            xs_scr[s, rows, 0:half] = s_re
            xs_scr[s, rows, half:] = s_im
            n_re = a_re * s_re - a_im * s_im + x_re
            n_im = a_re * s_im + a_im * s_re + x_im
            return n_re, n_im

        s_re, s_im = lax.fori_loop(0, ROWS_Z // SUBLANES, scan_body,
                                   (s_scr[s, :, 0:half], s_scr[s, :, half:]), unroll=4)
        s_scr[s, :, 0:half] = s_re
        s_scr[s, :, half:] = s_im

    for s in range(N_SLAB):
        lanes = slice(s * LANES, (s + 1) * LANES)
        y_tot = yi_scr[s] + jnp.dot(xs_scr[s].astype(BF16), r_ref[s], preferred_element_type=F32)
        for j in range(Q):
            y_scr[:, j, :, lanes] = y_tot[:, j * LANES:(j + 1) * LANES].reshape(
                ROWS_Z // SUBLANES, SUBLANES, LANES)

    def phase_c1(r, carry):
        rows = sub_rows(r)
        crow = pl.ds(r * CHUNK_ROWS, CHUNK_ROWS)
        y = y_scr[crow].reshape(SB, D_SSM) + d_ref[...] * u_scr[crow].reshape(SB, D_SSM)
        act_scr[rows, 0:D_SSM] = jax.nn.gelu(y).astype(BF16)
        acc = jnp.broadcast_to(dwb_ref[...], (SB, D_CONV))
        for j in range(CONV_WIDTH):
            acc = acc + dw_ref[j:j + 1, :] * z_scr[pl.ds(pl.multiple_of(r * SB + j * BATCH, SUBLANES), SB), :]
        mu = jnp.mean(acc, axis=-1, keepdims=True)
        cen = acc - mu
        var = jnp.mean(cen * cen, axis=-1, keepdims=True)
        ln = cen * lax.rsqrt(var + EPS) * lng_ref[...] + lnb_ref[...]
        act_scr[rows, D_SSM:] = jax.nn.silu(ln).astype(BF16)
        return carry

    lax.fori_loop(0, NSB, phase_c1, 0)
    z_scr[0:HALO, :] = z_scr[TM:TM + HALO, :]

    actb_scr[...] = jnp.dot(permt_ref[...], act_scr[...], preferred_element_type=F32).astype(BF16)

    lane = lax.broadcasted_iota(I32, (1, LANES), 1).astype(F32)
    grp_mask = lane < float(N_GROUPS_MOE)
    exp_lane = (lane >= float(LANE_EXP0)) & (lane < float(LANE_EXP0 + N_EXPERTS))
    lane_grp = jnp.floor((lane - float(LANE_EXP0)) * (1.0 / EXPERTS_PER_GROUP))
    tri = (lax.broadcasted_iota(I32, (SB, SB), 0) > lax.broadcasted_iota(I32, (SB, SB), 1)).astype(BF16)
    neg_inf = float("-inf")
    big = float(4 * LANES)

    def phase_c3(r, carry):
        rows = sub_rows(r)
        h = hb_scr[rows, :]
        g0 = D_SSM + 2 * D_CONV
        gate_ssm = jnp.dot(h, win_ref[:, g0:g0 + D_MODEL], preferred_element_type=F32) \
            + bgate_ref[:, 0:D_MODEL]
        gate_conv = jnp.dot(h, win_ref[:, g0 + D_MODEL:], preferred_element_type=F32) \
            + bgate_ref[:, D_MODEL:]
        zz = jnp.dot(actb_scr[rows, 0:D_SSM], wglu_ref[...], preferred_element_type=F32)
        y_ssm = zz[:, 0:D_MODEL] * jax.nn.sigmoid(zz[:, D_MODEL:])
        y_conv = jnp.dot(actb_scr[rows, D_SSM:], wco_ref[...], preferred_element_type=F32)

        merged = jax.nn.sigmoid(gate_ssm) * y_ssm + jax.nn.sigmoid(gate_conv) * y_conv
        xb = x_ref[pl.ds(r * BPS, BPS)].reshape(SB, D_MODEL)
        x1 = xb + jnp.dot(merged.astype(BF16), wout_ref[...], preferred_element_type=F32)
        x1_ref[pl.ds(r * BPS, BPS)] = x1.reshape(BPS, TT, D_MODEL)

        h2 = _rms(x1, gmoe_ref[...])
        h2p_ref[rows] = _pack_bf16_pair(h2[:, 0:HALF], h2[:, HALF:]).reshape((SB,) + ROW_TILE)

        h2_hi = h2.astype(BF16)
        h2_lo = (h2 - h2_hi.astype(F32)).astype(BF16)
        l1 = jnp.dot(h2_hi, wr1_ref[...], preferred_element_type=F32)
        l2 = jnp.dot(h2_lo, wr2_ref[...], preferred_element_type=F32)
        logits = l1[:, 0:LANES] + l1[:, LANES:] + l2 + br_ref[...]

        lg = jnp.where(grp_mask, logits, neg_inf)
        g_max = jnp.max(lg, axis=-1, keepdims=True)
        g_sel = jnp.min(jnp.where(lg == g_max, lane, big), axis=-1, keepdims=True)
        p_g = 1.0 / jnp.sum(jnp.where(grp_mask, jnp.exp(logits - g_max), 0.0), axis=-1, keepdims=True)
        le = jnp.where(exp_lane & (lane_grp == g_sel), logits, neg_inf)
        m1 = jnp.max(le, axis=-1, keepdims=True)
        i1 = jnp.min(jnp.where(le == m1, lane, big), axis=-1, keepdims=True)
        le2 = jnp.where(lane == i1, neg_inf, le)
        m2 = jnp.max(le2, axis=-1, keepdims=True)
        i2 = jnp.min(jnp.where(le2 == m2, lane, big), axis=-1, keepdims=True)
        e2 = jnp.exp(m2 - m1)
        den = 1.0 + e2
        w_a = (1.0 / den) * p_g
        w_b = (e2 / den) * p_g

        sel1 = lane == i1
        sel2 = lane == i2
        onehot = jnp.where(sel1 | sel2, 1.0, 0.0)
        prefix = jnp.dot(tri, onehot.astype(BF16), preferred_element_type=F32) + cnt_scr[...]
        rank_a = jnp.sum(jnp.where(sel1, prefix, 0.0), axis=-1, keepdims=True)
        rank_b = jnp.sum(jnp.where(sel2, prefix, 0.0), axis=-1, keepdims=True)
        cnt_scr[...] = cnt_scr[...] + jnp.sum(onehot, axis=0, keepdims=True)

        rec = jnp.where(lane == float(REC_EID0), i1 - float(LANE_EXP0), 0.0)
        rec = jnp.where(lane == float(REC_EID1), i2 - float(LANE_EXP0), rec)
        rec = jnp.where(lane == float(REC_W0), w_a, rec)
        rec = jnp.where(lane == float(REC_W1), w_b, rec)
        rec = jnp.where(lane == float(REC_RANK0), rank_a, rec)
        rec = jnp.where(lane == float(REC_RANK1), rank_b, rec)
        rec_ref[rows, :] = rec
        rect_scr[r] = jnp.transpose(rec)[0:REC_ROWS, :]
        return carry

    lax.fori_loop(0, NSB, phase_c3, 0)

    for r in range(NSB):
        rect_ref[:, r * SB:(r + 1) * SB] = rect_scr[r]
    cnt_ref[...] = cnt_scr[...]


def _mixer(x, gmix, win, bgate, perm, permt, mp, rmat, a_re, a_im, dvec, wglu, dw, dwb, lng, lnb, wco,
           wout, gmoe, wr1, wr2, br):
    seq_spec = pl.BlockSpec((BATCH, TT, D_MODEL), lambda i: (0, i, 0))
    in_specs = [
        seq_spec,
        _const_spec((1, D_MODEL)),
        _const_spec((D_MODEL, D_IN)),
        _const_spec((1, 2 * D_MODEL)),
        _const_spec((TM, TM)),
        _const_spec((TM, TM)),
        _const_spec(mp.shape),
        _const_spec(rmat.shape),
        _const_spec(a_re.shape),
        _const_spec(a_im.shape),
        _const_spec((1, D_SSM)),
        _const_spec((D_SSM, 2 * D_MODEL)),
        _const_spec((CONV_WIDTH, D_CONV)),
        _const_spec((1, D_CONV)),
        _const_spec((1, D_CONV)),
        _const_spec((1, D_CONV)),
        _const_spec((D_CONV, D_MODEL)),
        _const_spec((D_MODEL, D_MODEL)),
        _const_spec((1, D_MODEL)),
        _const_spec((D_MODEL, 2 * LANES)),
        _const_spec((D_MODEL, LANES)),
        _const_spec((1, LANES)),
    ]
    out_specs = [
        seq_spec,
        pl.BlockSpec((TM,) + ROW_TILE, lambda i: (i, 0, 0)),
        pl.BlockSpec((TM, LANES), lambda i: (i, 0)),
        pl.BlockSpec((REC_ROWS, TM), lambda i: (0, i)),
        pl.BlockSpec((1, LANES), lambda i: (0, 0)),
    ]
    out_shape = [
        jax.ShapeDtypeStruct((BATCH, SEQ, D_MODEL), F32),
        jax.ShapeDtypeStruct((N_TOK,) + ROW_TILE, U32),
        jax.ShapeDtypeStruct((N_TOK, LANES), F32),
        jax.ShapeDtypeStruct((REC_ROWS, N_TOK), F32),
        jax.ShapeDtypeStruct((1, LANES), F32),
    ]
    chunk_shape = (ROWS_Z // SUBLANES, Q, SUBLANES, D_SSM)
    scratch = [
        pltpu.VMEM((TM, D_MODEL), BF16),
        pltpu.VMEM((TM, D_MODEL), BF16),
        pltpu.VMEM(chunk_shape, F32),
        pltpu.VMEM(chunk_shape, F32),
        pltpu.VMEM((N_SLAB, ROWS_Z, Q * LANES), F32),
        pltpu.VMEM((N_SLAB, ROWS_Z, STATE_LANES), F32),
        pltpu.VMEM((HALO + TM, D_CONV), F32),
        pltpu.VMEM((TM, D_SSM + D_CONV), BF16),
        pltpu.VMEM((TM, D_SSM + D_CONV), BF16),
        pltpu.VMEM((NSB, REC_ROWS, SB), F32),
        pltpu.VMEM((N_SLAB, SUBLANES, STATE_LANES), F32),
        pltpu.VMEM((1, LANES), F32),
    ]
    return pl.pallas_call(
        _mixer_kernel,
        grid=(N_STEP,),
        in_specs=in_specs,
        out_specs=out_specs,
        out_shape=out_shape,
        scratch_shapes=scratch,
        compiler_params=pltpu.CompilerParams(
            dimension_semantics=("arbitrary",), vmem_limit_bytes=VMEM_LIMIT),
        name="mixer",
    )(x, gmix, win, bgate, perm, permt, mp, rmat, a_re, a_im, dvec, wglu, dw, dwb, lng, lnb, wco, wout,
      gmoe, wr1, wr2, br)


def _cmul(a, b):
    return a[0] * b[0] - a[1] * b[1], a[0] * b[1] + a[1] * b[0]


def _ssm_matrices(a_re, a_im, log_dt, b_re, b_im, c_re, c_im):
    hp = lax.Precision.HIGHEST
    dt = jnp.exp(log_dt)[:, None]
    mag = jnp.exp(a_re * dt)
    lam = (mag * jnp.cos(a_im * dt), mag * jnp.sin(a_im * dt))
    den = a_re * a_re + a_im * a_im
    nr = lam[0] - 1.0
    ni = lam[1]
    z_re = (nr * a_re + ni * a_im) / den
    z_im = (ni * a_re - nr * a_im) / den
    bbar = (z_re[..., None] * b_re - z_im[..., None] * b_im,
            z_re[..., None] * b_im + z_im[..., None] * b_re)
    pw = [(jnp.ones_like(lam[0]), jnp.zeros_like(lam[0])), lam]
    for _ in range(2, Q + 1):
        pw.append(_cmul(pw[-1], lam))
    e = [(c_re * p[0][:, None, :] - c_im * p[1][:, None, :],
          c_re * p[1][:, None, :] + c_im * p[0][:, None, :]) for p in pw]
    k = [jnp.einsum('gcn,gnd->gcd', e[m][0], bbar[0], precision=hp)
         - jnp.einsum('gcn,gnd->gcd', e[m][1], bbar[1], precision=hp) for m in range(Q)]
    eye = jnp.eye(GROUPS_PER_SLAB, dtype=F32)
    split = lambda t: t.reshape((N_SLAB, GROUPS_PER_SLAB) + t.shape[1:])
    zero_k = jnp.zeros_like(k[0])
    kb = jnp.stack([jnp.stack([split(jnp.swapaxes(k[j - i] if j >= i else zero_k, 1, 2))
                               for j in range(Q)]) for i in range(Q)])
    m_mat = jnp.einsum('ijsgdc,gh->sigdjhc', kb, eye).reshape(N_SLAB, Q * LANES, Q * LANES)
    f = [_cmul((pw[Q - 1 - i][0][..., None], pw[Q - 1 - i][1][..., None]), bbar) for i in range(Q)]
    p_parts = []
    for part in range(2):
        fs = jnp.stack([split(f[i][part]) for i in range(Q)])
        p_parts.append(jnp.einsum('isgnd,gh->sigdhn', fs, eye).reshape(N_SLAB, Q * LANES, STATE_LANES // 2))
    p_mat = jnp.concatenate(p_parts, axis=-1)
    r_parts = []
    for part, sign in ((0, 1.0), (1, -1.0)):
        es = jnp.stack([split(e[j + 1][part]) for j in range(Q)])
        r_parts.append(sign * jnp.einsum('jsgcn,gh->shnjgc', es, eye).reshape(
            N_SLAB, STATE_LANES // 2, Q * LANES))
    r_mat = jnp.concatenate(r_parts, axis=1)
    mp = jnp.concatenate([m_mat, p_mat], axis=-1).astype(BF16)
    a_q = pw[Q]
    return (mp, r_mat.astype(BF16),
            a_q[0].reshape(N_SLAB, STATE_LANES // 2), a_q[1].reshape(N_SLAB, STATE_LANES // 2))


def _router_weights(w_rg, b_rg, w_re, b_re):
    pad_g = LANE_EXP0 - LANE_GRP0 - N_GROUPS_MOE
    pad_e = LANES - LANE_EXP0 - N_EXPERTS
    w = jnp.concatenate([w_rg, jnp.zeros((D_MODEL, pad_g), F32), w_re, jnp.zeros((D_MODEL, pad_e), F32)], axis=1)
    b = jnp.concatenate([b_rg, jnp.zeros((pad_g,), F32), b_re, jnp.zeros((pad_e,), F32)]).reshape(1, LANES)
    w_hi = w.astype(BF16)
    w_lo = (w - w_hi.astype(F32)).astype(BF16)
    return jnp.concatenate([w_hi, w_lo], axis=1), w_hi, b


def _time_major_permutation():
    tm = jnp.arange(TM, dtype=I32)
    src = (tm % BATCH) * TT + tm // BATCH
    perm = (src[:, None] == jnp.arange(TM, dtype=I32)[None, :]).astype(BF16)
    return perm, perm.T


def _dispatch_kernel(zstart_ref, zlen_ref, nused_ref, dest_ref, h_ref, xs_ref, zero_scr, sem):
    step = pl.program_id(0)

    @pl.when(step == 0)
    def _zero_padding():
        zero_scr[...] = jnp.zeros(zero_scr.shape, U32)

        def pad_pieces(e, act):
            n = zlen_ref[e]
            piece = BM // 2
            while piece >= 1:
                pos = zstart_ref[e] + (n & ~(2 * piece - 1))
                copy = pltpu.make_async_copy(
                    zero_scr.at[pl.ds(0, piece)], xs_ref.at[pl.ds(pos, piece)], sem.at[0])
                pl.when((n & piece) != 0)(functools.partial(act, copy))
                piece //= 2

        def tail_copy(b):
            return pltpu.make_async_copy(zero_scr, xs_ref.at[pl.ds(b * BM, BM)], sem.at[0])

        def start_all(e, c):
            pad_pieces(e, lambda copy: copy.start())
            return c

        def wait_all(e, c):
            pad_pieces(e, lambda copy: copy.wait())
            return c

        lax.fori_loop(0, N_EXPERTS, start_all, 0)
        lax.fori_loop(nused_ref[0], N_BLK, lambda b, c: (tail_copy(b).start(), c)[1], 0)
        lax.fori_loop(0, N_EXPERTS, wait_all, 0)
        lax.fori_loop(nused_ref[0], N_BLK, lambda b, c: (tail_copy(b).wait(), c)[1], 0)

    def row_copy(r, j):
        return pltpu.make_async_copy(h_ref.at[r], xs_ref.at[dest_ref[j, 0, 0, r]], sem.at[0])

    def issue(r, carry):
        row_copy(r, 0).start()
        row_copy(r, 1).start()
        return carry

    lax.fori_loop(0, TM, issue, 0, unroll=8)

    def drain(r, carry):
        row_copy(r, 0).wait()
        row_copy(r, 1).wait()
        return carry

    lax.fori_loop(0, TM, drain, 0, unroll=8)


def _dest_spec():
    return pl.BlockSpec((2, 1, 1, TM), lambda i, *_: (0, i, 0, 0), memory_space=pltpu.SMEM)


def _dispatch(zstart, zlen, nused, dest, h2p):
    grid_spec = pltpu.PrefetchScalarGridSpec(
        num_scalar_prefetch=3,
        grid=(N_STEP,),
        in_specs=[
            _dest_spec(),
            pl.BlockSpec((TM,) + ROW_TILE, lambda i, *_: (i, 0, 0)),
        ],
        out_specs=pl.BlockSpec(memory_space=pl.ANY),
        scratch_shapes=[pltpu.VMEM((BM,) + ROW_TILE, U32), pltpu.SemaphoreType.DMA((1,))],
    )
    return pl.pallas_call(
        _dispatch_kernel,
        grid_spec=grid_spec,
        out_shape=jax.ShapeDtypeStruct((N_ROWS,) + ROW_TILE, U32),
        compiler_params=pltpu.CompilerParams(dimension_semantics=("arbitrary",)),
        name="dispatch",
    )(zstart, zlen, nused, dest, h2p)


def _expert_kernel(bexp_ref, nused_ref, xs_ref, wg_ref, wu_ref, wd_ref, ys_ref, wg_scr, wu_scr, wd_scr):
    i = pl.program_id(0)
    prev = bexp_ref[jnp.maximum(i - 1, 0)]
    fresh = (i == 0) | (bexp_ref[i] != prev)

    @pl.when(fresh)
    def _cast_weights():
        wg_scr[...] = wg_ref[0].astype(BF16)
        wu_scr[...] = wu_ref[0].astype(BF16)
        wd_scr[...] = wd_ref[0].astype(BF16)

    @pl.when(i < nused_ref[0])
    def _compute():
        lo, hi = _unpack_bf16_pair(xs_ref[...].reshape(BM, HALF))
        lo = lo.astype(BF16)
        hi = hi.astype(BF16)
        g = jnp.dot(lo, wg_scr[0:HALF, :], preferred_element_type=F32) \
            + jnp.dot(hi, wg_scr[HALF:, :], preferred_element_type=F32)
        u = jnp.dot(lo, wu_scr[0:HALF, :], preferred_element_type=F32) \
            + jnp.dot(hi, wu_scr[HALF:, :], preferred_element_type=F32)
        a = (jax.nn.silu(g) * u).astype(BF16)
        o = jnp.dot(a, wd_scr[...], preferred_element_type=F32)
        ys_ref[...] = _pack_bf16_pair(o[:, 0:HALF], o[:, HALF:]).reshape((BM,) + ROW_TILE)

    @pl.when(i >= nused_ref[0])
    def _unused():
        ys_ref[...] = jnp.zeros(ys_ref.shape, U32)


def _experts(bexp, nused, xs, wg, wu, wd):
    grid_spec = pltpu.PrefetchScalarGridSpec(
        num_scalar_prefetch=2,
        grid=(N_BLK,),
        in_specs=[
            pl.BlockSpec((BM,) + ROW_TILE, lambda i, be, nu: (jnp.minimum(i, nu[0] - 1), 0, 0)),
            pl.BlockSpec((1, D_MODEL, D_EXPERT), lambda i, be, nu: (be[i], 0, 0)),
            pl.BlockSpec((1, D_MODEL, D_EXPERT), lambda i, be, nu: (be[i], 0, 0)),
            pl.BlockSpec((1, D_EXPERT, D_MODEL), lambda i, be, nu: (be[i], 0, 0)),
        ],
        out_specs=pl.BlockSpec((BM,) + ROW_TILE, lambda i, be, nu: (i, 0, 0)),
        scratch_shapes=[
            pltpu.VMEM((D_MODEL, D_EXPERT), BF16),
            pltpu.VMEM((D_MODEL, D_EXPERT), BF16),
            pltpu.VMEM((D_EXPERT, D_MODEL), BF16),
        ],
    )
    return pl.pallas_call(
        _expert_kernel,
        grid_spec=grid_spec,
        out_shape=jax.ShapeDtypeStruct((N_ROWS,) + ROW_TILE, U32),
        compiler_params=pltpu.CompilerParams(
            dimension_semantics=("arbitrary",), vmem_limit_bytes=VMEM_LIMIT),
        name="experts",
    )(bexp, nused, xs, wg, wu, wd)


def _combine_kernel(dest_ref, x1_ref, rec_ref, p_ref, gple_ref, wpg_ref, wple_ref, gfin_ref, ys_ref,
                    out_ref, gat_scr, sem):
    def row_copy(r, j):
        return pltpu.make_async_copy(ys_ref.at[dest_ref[j, 0, 0, r]], gat_scr.at[j, r], sem.at[0])

    def issue(r, carry):
        row_copy(r, 0).start()
        row_copy(r, 1).start()
        return carry

    lax.fori_loop(0, TM, issue, 0, unroll=8)

    ple = jnp.dot(p_ref[0].reshape(TM, D_PLE).astype(BF16), wple_ref[...], preferred_element_type=F32)

    def drain(r, carry):
        row_copy(r, 0).wait()
        row_copy(r, 1).wait()
        return carry

    lax.fori_loop(0, TM, drain, 0, unroll=8)

    rec = rec_ref[...]
    w0 = rec[:, REC_W0:REC_W0 + 1]
    w1 = rec[:, REC_W1:REC_W1 + 1]
    lo0, hi0 = _unpack_bf16_pair(gat_scr[0].reshape(TM, HALF))
    lo1, hi1 = _unpack_bf16_pair(gat_scr[1].reshape(TM, HALF))
    moe = jnp.concatenate([lo0 * w0 + lo1 * w1, hi0 * w0 + hi1 * w1], axis=1)
    x2 = x1_ref[...].reshape(TM, D_MODEL) + moe
    gate = jax.nn.sigmoid(jnp.dot(_rms(x2, gple_ref[...]).astype(BF16), wpg_ref[...],
                                  preferred_element_type=F32))
    x3 = x2 + gate * ple
    out_ref[...] = _rms(x3, gfin_ref[...]).reshape(BATCH, TT, D_MODEL)


def _combine(dest, x1, rec, p, gple, wpg, wple, gfin, ys):
    seq_spec = pl.BlockSpec((BATCH, TT, D_MODEL), lambda i: (0, i, 0))
    return pl.pallas_call(
        _combine_kernel,
        grid=(N_STEP,),
        in_specs=[
            _dest_spec(),
            seq_spec,
            pl.BlockSpec((TM, LANES), lambda i: (i, 0)),
            pl.BlockSpec((1, BATCH, TT, D_PLE), lambda i: (0, 0, i, 0)),
            _const_spec((1, D_MODEL)),
            _const_spec((D_MODEL, D_MODEL)),
            _const_spec((D_PLE, D_MODEL)),
            _const_spec((1, D_MODEL)),
            pl.BlockSpec(memory_space=pl.ANY),
        ],
        out_specs=seq_spec,
        out_shape=jax.ShapeDtypeStruct((BATCH, SEQ, D_MODEL), F32),
        scratch_shapes=[pltpu.VMEM((2, TM) + ROW_TILE, U32), pltpu.SemaphoreType.DMA((1,))],
        compiler_params=pltpu.CompilerParams(
            dimension_semantics=("arbitrary",), vmem_limit_bytes=VMEM_LIMIT),
        name="combine",
    )(dest, x1, rec, p, gple, wpg, wple, gfin, ys)


def kernel(x, p, g_mix, w_in, b_gate, ssm_a_re, ssm_a_im, ssm_log_dt, ssm_b_re, ssm_b_im, ssm_c_re,
           ssm_c_im, ssm_d, w_glu, conv_dw, conv_dw_b, conv_ln_g, conv_ln_b, w_conv_out, w_out, g_moe,
           w_router_group, b_router_group, w_router_expert, b_router_expert, w_exp_gate, w_exp_up,
           w_exp_down, g_ple, w_ple_gate, w_ple, g_final):
    assert x.shape == (BATCH, SEQ, D_MODEL) and p.shape == (1, BATCH, SEQ, D_PLE)
    row = lambda v: v.reshape(1, -1)

    mp, rmat, a_re, a_im = _ssm_matrices(ssm_a_re[0], ssm_a_im[0], ssm_log_dt[0], ssm_b_re[0],
                                         ssm_b_im[0], ssm_c_re[0], ssm_c_im[0])
    wr1, wr2, br = _router_weights(w_router_group[0], b_router_group[0], w_router_expert[0],
                                   b_router_expert[0])
    perm, permt = _time_major_permutation()
    x1, h2p, rec, rect, cnt = _mixer(
        x, row(g_mix[0]), w_in[0].astype(BF16), row(b_gate[0]), perm, permt, mp, rmat, a_re, a_im,
        row(ssm_d[0]), w_glu[0].astype(BF16), conv_dw[0], row(conv_dw_b[0]), row(conv_ln_g[0]),
        row(conv_ln_b[0]), w_conv_out[0].astype(BF16), w_out[0].astype(BF16), row(g_moe[0]), wr1, wr2, br)

    counts = cnt[0, LANE_EXP0:LANE_EXP0 + N_EXPERTS].astype(I32)
    pcounts = (counts + BM - 1) // BM * BM
    pends = jnp.cumsum(pcounts)
    pstarts = pends - pcounts
    eid = rect[REC_EID0:REC_EID1 + 1].astype(I32)
    rank = rect[REC_RANK0:REC_RANK1 + 1].astype(I32)
    dest = (jnp.sum(jnp.where(eid[..., None] == jnp.arange(N_EXPERTS, dtype=I32), pstarts, 0), axis=-1)
            + rank).reshape(2, N_STEP, 1, TM)
    nused = (pends[-1] // BM).astype(I32)
    blk = jnp.minimum(jnp.arange(N_BLK, dtype=I32), nused - 1) * BM
    bexp = jnp.minimum(jnp.sum((pends[None, :] <= blk[:, None]).astype(I32), axis=1), N_EXPERTS - 1)
    zstart = (pstarts + counts).astype(I32)
    zlen = (pcounts - counts).astype(I32)

    nused = nused.reshape(1)
    xs = _dispatch(zstart, zlen, nused, dest, h2p)
    ys = _experts(bexp, nused, xs, w_exp_gate[0], w_exp_up[0], w_exp_down[0])
    return _combine(dest, x1, rec, p, row(g_ple[0]), w_ple_gate[0].astype(BF16),
                    w_ple[0].astype(BF16), row(g_final), ys)
```

```python
import functools

import jax
import jax.numpy as jnp
from jax import lax
from jax.experimental import pallas as pl
from jax.experimental.pallas import tpu as pltpu

F32 = jnp.float32
BF16 = jnp.bfloat16
U32 = jnp.uint32
I32 = jnp.int32

D_MODEL = 1024
BATCH = 8
SEQ = 2048
N_TOK = BATCH * SEQ
D_SSM = 512
SSM_GROUP_WIDTH = 16
SSM_GROUPS = 32
SSM_STATE = 64
D_CONV = 512
CONV_WIDTH = 31
D_IN = D_SSM + 2 * D_CONV + 2 * D_MODEL
N_GROUPS_MOE = 4
EXPERTS_PER_GROUP = 8
N_EXPERTS = 32
D_EXPERT = 512
D_PLE = 256
EPS = 1e-6

SUBLANES = 8
LANES = 128
assert BATCH == SUBLANES

TT = 64
TM = TT * BATCH
N_STEP = SEQ // TT
SB = 256
NSB = TM // SB
N_SUB = N_TOK // SB
N_RUN = N_SUB * N_EXPERTS
BPS = SB // TT
Q = 2
N_SLAB = D_SSM // LANES
GROUPS_PER_SLAB = SSM_GROUPS // N_SLAB
ROWS_Z = TM // Q
STATE_LANES = 2 * GROUPS_PER_SLAB * SSM_STATE
HALO = (CONV_WIDTH - 1) * BATCH
CHUNK_ROWS = SB // (Q * SUBLANES)
CONV_ROWS = 64

LANE_GRP0 = 0
LANE_EXP0 = 32
REC_EID0, REC_EID1, REC_W0, REC_W1, REC_POS0, REC_POS1 = 0, 1, 2, 3, 4, 5

BM = 256
N_BLK = (2 * N_TOK + N_EXPERTS * (BM - 1) + BM - 1) // BM
N_ROWS = N_BLK * BM
HALF = D_MODEL // 2
ROW_TILE = (HALF // LANES, LANES)

VMEM_LIMIT = 56 * 1024 * 1024


def _const_spec(shape):
    n = len(shape)
    return pl.BlockSpec(shape, lambda *_: (0,) * n, pipeline_mode=pl.Buffered(1))


def _rms(x, g):
    ms = jnp.mean(x * x, axis=-1, keepdims=True)
    return x * lax.rsqrt(ms + EPS) * g


def _pack_bf16_pair(lo, hi):
    ulo = lax.bitcast_convert_type(lo.astype(BF16).astype(F32), U32)
    uhi = lax.bitcast_convert_type(hi.astype(BF16).astype(F32), U32)
    return (ulo >> 16) | (uhi & jnp.uint32(0xFFFF0000))


def _unpack_bf16_pair(w):
    lo = lax.bitcast_convert_type(w << 16, F32)
    hi = lax.bitcast_convert_type(w & jnp.uint32(0xFFFF0000), F32)
    return lo, hi


def _mixer_kernel(x_ref, gmix_ref, win_ref, bgate_ref, perm_ref, permt_ref, mp_ref, r_ref, are_ref,
                  aim_ref, d_ref, wglu_ref, dw_ref, dwb_ref, lng_ref, lnb_ref, wco_ref, wout_ref,
                  gmoe_ref, wr1_ref, wr2_ref, br_ref,
                  x1_ref, xst_ref, rec_ref, cblk_ref, cnt_ref,
                  hb_scr, ht_scr, u_scr, y_scr, yi_scr, xs_scr, z_scr, conv_scr, act_scr, actb_scr,
                  s_scr, cnt_scr):
    step = pl.program_id(0)

    @pl.when(step == 0)
    def _init():
        z_scr[:, 0:HALO, :] = jnp.zeros((D_CONV // LANES, HALO, LANES), F32)
        s_scr[...] = jnp.zeros(s_scr.shape, F32)
        cnt_scr[...] = jnp.zeros(cnt_scr.shape, F32)

    def sub_rows(r):
        return pl.ds(pl.multiple_of(r * SB, SB), SB)

    def phase_a(r, carry):
        xb = x_ref[pl.ds(r * BPS, BPS)].reshape(SB, D_MODEL)
        hb_scr[sub_rows(r), :] = _rms(xb, gmix_ref[...]).astype(BF16)
        return carry

    lax.fori_loop(0, NSB, phase_a, 0)

    ht_scr[...] = jnp.dot(perm_ref[...], hb_scr[...], preferred_element_type=F32).astype(BF16)

    def phase_a3(r, carry):
        h = ht_scr[sub_rows(r), :]
        u = jnp.dot(h, win_ref[:, 0:D_SSM], preferred_element_type=F32)
        u_scr[pl.ds(r * CHUNK_ROWS, CHUNK_ROWS)] = u.reshape(CHUNK_ROWS, Q, SUBLANES, D_SSM)
        v = jnp.dot(h, win_ref[:, D_SSM:D_SSM + 2 * D_CONV], preferred_element_type=F32)
        zc = v[:, 0:D_CONV] * jax.nn.sigmoid(v[:, D_CONV:])
        for lc in range(D_CONV // LANES):
            z_scr[lc, pl.ds(pl.multiple_of(HALO + r * SB, SUBLANES), SB), :] = zc[:, lc * LANES:(lc + 1) * LANES]
        return carry

    lax.fori_loop(0, NSB, phase_a3, 0)

    for s in range(N_SLAB):
        lanes = slice(s * LANES, (s + 1) * LANES)
        z = jnp.concatenate(
            [u_scr[:, i, :, lanes].reshape(ROWS_Z, LANES) for i in range(Q)], axis=1).astype(BF16)
        xp = jnp.dot(z, mp_ref[s], preferred_element_type=F32)
        yi_scr[s] = xp[:, 0:Q * LANES]
        xs_scr[s] = xp[:, Q * LANES:]

    half = STATE_LANES // 2
    for s in range(N_SLAB):
        a_re = jnp.broadcast_to(are_ref[s:s + 1, :], (SUBLANES, half))
        a_im = jnp.broadcast_to(aim_ref[s:s + 1, :], (SUBLANES, half))

        def scan_body(k, carry, s=s, a_re=a_re, a_im=a_im):
            s_re, s_im = carry
            rows = pl.ds(pl.multiple_of(k * SUBLANES, SUBLANES), SUBLANES)
            x_re = xs_scr[s, rows, 0:half]
            x_im = xs_scr[s, rows, half:]
            xs_scr[s, rows, 0:half] = s_re
            xs_scr[s, rows, half:] = s_im
            n_re = a_re * s_re - a_im * s_im + x_re
            n_im = a_re * s_im + a_im * s_re + x_im
            return n_re, n_im

        s_re, s_im = lax.fori_loop(0, ROWS_Z // SUBLANES, scan_body,
                                   (s_scr[s, :, 0:half], s_scr[s, :, half:]), unroll=4)
        s_scr[s, :, 0:half] = s_re
        s_scr[s, :, half:] = s_im

    for s in range(N_SLAB):
        lanes = slice(s * LANES, (s + 1) * LANES)
        y_tot = yi_scr[s] + jnp.dot(xs_scr[s].astype(BF16), r_ref[s], preferred_element_type=F32)
        for j in range(Q):
            y_scr[:, j, :, lanes] = y_tot[:, j * LANES:(j + 1) * LANES].reshape(
                ROWS_Z // SUBLANES, SUBLANES, LANES)

    def phase_c1(r, carry):
        rows = sub_rows(r)
        crow = pl.ds(r * CHUNK_ROWS, CHUNK_ROWS)
        y = y_scr[crow].reshape(SB, D_SSM) + d_ref[...] * u_scr[crow].reshape(SB, D_SSM)
        act_scr[rows, 0:D_SSM] = jax.nn.gelu(y).astype(BF16)
        for lc in range(D_CONV // LANES):
            lanes = slice(lc * LANES, (lc + 1) * LANES)

            def conv_piece(rc, c, lc=lc, lanes=lanes):
                r0 = r * SB + rc * CONV_ROWS
                piece = jnp.broadcast_to(dwb_ref[:, lanes], (CONV_ROWS, LANES))
                for j in range(CONV_WIDTH):
                    zrows = pl.ds(pl.multiple_of(r0 + j * BATCH, SUBLANES), CONV_ROWS)
                    piece = piece + dw_ref[j:j + 1, lanes] * z_scr[lc, zrows, :]
                conv_scr[pl.ds(pl.multiple_of(rc * CONV_ROWS, CONV_ROWS), CONV_ROWS), lanes] = piece
                return c

            lax.fori_loop(0, SB // CONV_ROWS, conv_piece, 0)
        acc = conv_scr[...]
        mu = jnp.mean(acc, axis=-1, keepdims=True)
        cen = acc - mu
        var = jnp.mean(cen * cen, axis=-1, keepdims=True)
        ln = cen * lax.rsqrt(var + EPS) * lng_ref[...] + lnb_ref[...]
        act_scr[rows, D_SSM:] = jax.nn.silu(ln).astype(BF16)
        return carry

    lax.fori_loop(0, NSB, phase_c1, 0)
    z_scr[:, 0:HALO, :] = z_scr[:, TM:TM + HALO, :]

    actb_scr[...] = jnp.dot(permt_ref[...], act_scr[...], preferred_element_type=F32).astype(BF16)

    lane = lax.broadcasted_iota(I32, (1, LANES), 1).astype(F32)
    grp_mask = lane < float(N_GROUPS_MOE)
    exp_lane = (lane >= float(LANE_EXP0)) & (lane < float(LANE_EXP0 + N_EXPERTS))
    lane_grp = jnp.floor((lane - float(LANE_EXP0)) * (1.0 / EXPERTS_PER_GROUP))
    tri = (lax.broadcasted_iota(I32, (SB, SB), 0) > lax.broadcasted_iota(I32, (SB, SB), 1)).astype(BF16)
    lower_lanes = (lax.broadcasted_iota(I32, (LANES, LANES), 0)
                   < lax.broadcasted_iota(I32, (LANES, LANES), 1)).astype(BF16)
    sorted_row = lax.broadcasted_iota(I32, (2 * SB, SB), 0).astype(F32)
    neg_inf = float("-inf")
    big = float(4 * LANES)

    def phase_c3(r, carry):
        rows = sub_rows(r)
        h = hb_scr[rows, :]
        g0 = D_SSM + 2 * D_CONV
        gate_ssm = jnp.dot(h, win_ref[:, g0:g0 + D_MODEL], preferred_element_type=F32) \
            + bgate_ref[:, 0:D_MODEL]
        gate_conv = jnp.dot(h, win_ref[:, g0 + D_MODEL:], preferred_element_type=F32) \
            + bgate_ref[:, D_MODEL:]
        zz = jnp.dot(actb_scr[rows, 0:D_SSM], wglu_ref[...], preferred_element_type=F32)
        y_ssm = zz[:, 0:D_MODEL] * jax.nn.sigmoid(zz[:, D_MODEL:])
        y_conv = jnp.dot(actb_scr[rows, D_SSM:], wco_ref[...], preferred_element_type=F32)

        merged = jax.nn.sigmoid(gate_ssm) * y_ssm + jax.nn.sigmoid(gate_conv) * y_conv
        xb = x_ref[pl.ds(r * BPS, BPS)].reshape(SB, D_MODEL)
        x1 = xb + jnp.dot(merged.astype(BF16), wout_ref[...], preferred_element_type=F32)
        x1_ref[pl.ds(r * BPS, BPS)] = x1.reshape(BPS, TT, D_MODEL)

        h2 = _rms(x1, gmoe_ref[...])

        h2_hi = h2.astype(BF16)
        h2_lo = (h2 - h2_hi.astype(F32)).astype(BF16)
        l1 = jnp.dot(h2_hi, wr1_ref[...], preferred_element_type=F32)
        l2 = jnp.dot(h2_lo, wr2_ref[...], preferred_element_type=F32)
        logits = l1[:, 0:LANES] + l1[:, LANES:] + l2 + br_ref[...]

        lg = jnp.where(grp_mask, logits, neg_inf)
        g_max = jnp.max(lg, axis=-1, keepdims=True)
        g_sel = jnp.min(jnp.where(lg == g_max, lane, big), axis=-1, keepdims=True)
        p_g = 1.0 / jnp.sum(jnp.where(grp_mask, jnp.exp(logits - g_max), 0.0), axis=-1, keepdims=True)
        le = jnp.where(exp_lane & (lane_grp == g_sel), logits, neg_inf)
        m1 = jnp.max(le, axis=-1, keepdims=True)
        i1 = jnp.min(jnp.where(le == m1, lane, big), axis=-1, keepdims=True)
        le2 = jnp.where(lane == i1, neg_inf, le)
        m2 = jnp.max(le2, axis=-1, keepdims=True)
        i2 = jnp.min(jnp.where(le2 == m2, lane, big), axis=-1, keepdims=True)
        e2 = jnp.exp(m2 - m1)
        den = 1.0 + e2
        w_a = (1.0 / den) * p_g
        w_b = (e2 / den) * p_g

        sel1 = lane == i1
        sel2 = lane == i2
        onehot = jnp.where(sel1 | sel2, 1.0, 0.0)
        colsum = jnp.sum(onehot, axis=0, keepdims=True)
        base = jnp.dot(jnp.broadcast_to(colsum, (SUBLANES, LANES)).astype(BF16), lower_lanes,
                       preferred_element_type=F32)[0:1, :]
        pos_l = jnp.dot(tri, onehot.astype(BF16), preferred_element_type=F32) + base
        pos_a = jnp.sum(jnp.where(sel1, pos_l, 0.0), axis=-1, keepdims=True)
        pos_b = jnp.sum(jnp.where(sel2, pos_l, 0.0), axis=-1, keepdims=True)
        cnt_scr[...] = cnt_scr[...] + colsum
        cblk_ref[r] = colsum

        rec = jnp.where(lane == float(REC_EID0), i1 - float(LANE_EXP0), 0.0)
        rec = jnp.where(lane == float(REC_EID1), i2 - float(LANE_EXP0), rec)
        rec = jnp.where(lane == float(REC_W0), w_a, rec)
        rec = jnp.where(lane == float(REC_W1), w_b, rec)
        rec = jnp.where(lane == float(REC_POS0), pos_a, rec)
        rec = jnp.where(lane == float(REC_POS1), pos_b, rec)
        rec_ref[rows, :] = rec

        rect = jnp.transpose(rec)
        sort_mat = ((sorted_row == rect[REC_POS0:REC_POS0 + 1, :])
                    | (sorted_row == rect[REC_POS1:REC_POS1 + 1, :])).astype(BF16)
        xsorted = jnp.dot(sort_mat, h2_hi, preferred_element_type=F32)
        xst_ref[pl.ds(pl.multiple_of(r * 2 * SB, 2 * SB), 2 * SB)] = _pack_bf16_pair(
            xsorted[:, 0:HALF], xsorted[:, HALF:]).reshape((2 * SB,) + ROW_TILE)
        return carry

    lax.fori_loop(0, NSB, phase_c3, 0)
    cnt_ref[...] = cnt_scr[...]


def _mixer(x, gmix, win, bgate, perm, permt, mp, rmat, a_re, a_im, dvec, wglu, dw, dwb, lng, lnb, wco,
           wout, gmoe, wr1, wr2, br):
    seq_spec = pl.BlockSpec((BATCH, TT, D_MODEL), lambda i: (0, i, 0))
    in_specs = [
        seq_spec,
        _const_spec((1, D_MODEL)),
        _const_spec((D_MODEL, D_IN)),
        _const_spec((1, 2 * D_MODEL)),
        _const_spec((TM, TM)),
        _const_spec((TM, TM)),
        _const_spec(mp.shape),
        _const_spec(rmat.shape),
        _const_spec(a_re.shape),
        _const_spec(a_im.shape),
        _const_spec((1, D_SSM)),
        _const_spec((D_SSM, 2 * D_MODEL)),
        _const_spec((CONV_WIDTH, D_CONV)),
        _const_spec((1, D_CONV)),
        _const_spec((1, D_CONV)),
        _const_spec((1, D_CONV)),
        _const_spec((D_CONV, D_MODEL)),
        _const_spec((D_MODEL, D_MODEL)),
        _const_spec((1, D_MODEL)),
        _const_spec((D_MODEL, 2 * LANES)),
        _const_spec((D_MODEL, LANES)),
        _const_spec((1, LANES)),
    ]
    out_specs = [
        seq_spec,
        pl.BlockSpec((2 * TM,) + ROW_TILE, lambda i: (i, 0, 0)),
        pl.BlockSpec((TM, LANES), lambda i: (i, 0)),
        pl.BlockSpec((NSB, 1, LANES), lambda i: (i, 0, 0)),
        pl.BlockSpec((1, LANES), lambda i: (0, 0)),
    ]
    out_shape = [
        jax.ShapeDtypeStruct((BATCH, SEQ, D_MODEL), F32),
        jax.ShapeDtypeStruct((2 * N_TOK,) + ROW_TILE, U32),
        jax.ShapeDtypeStruct((N_TOK, LANES), F32),
        jax.ShapeDtypeStruct((N_SUB, 1, LANES), F32),
        jax.ShapeDtypeStruct((1, LANES), F32),
    ]
    chunk_shape = (ROWS_Z // SUBLANES, Q, SUBLANES, D_SSM)
    scratch = [
        pltpu.VMEM((TM, D_MODEL), BF16),
        pltpu.VMEM((TM, D_MODEL), BF16),
        pltpu.VMEM(chunk_shape, F32),
        pltpu.VMEM(chunk_shape, F32),
        pltpu.VMEM((N_SLAB, ROWS_Z, Q * LANES), F32),
        pltpu.VMEM((N_SLAB, ROWS_Z, STATE_LANES), F32),
        pltpu.VMEM((D_CONV // LANES, HALO + TM, LANES), F32),
        pltpu.VMEM((SB, D_CONV), F32),
        pltpu.VMEM((TM, D_SSM + D_CONV), BF16),
        pltpu.VMEM((TM, D_SSM + D_CONV), BF16),
        pltpu.VMEM((N_SLAB, SUBLANES, STATE_LANES), F32),
        pltpu.VMEM((1, LANES), F32),
    ]
    return pl.pallas_call(
        _mixer_kernel,
        grid=(N_STEP,),
        in_specs=in_specs,
        out_specs=out_specs,
        out_shape=out_shape,
        scratch_shapes=scratch,
        compiler_params=pltpu.CompilerParams(
            dimension_semantics=("arbitrary",), vmem_limit_bytes=VMEM_LIMIT),
        name="mixer",
    )(x, gmix, win, bgate, perm, permt, mp, rmat, a_re, a_im, dvec, wglu, dw, dwb, lng, lnb, wco, wout,
      gmoe, wr1, wr2, br)


def _cmul(a, b):
    return a[0] * b[0] - a[1] * b[1], a[0] * b[1] + a[1] * b[0]


def _ssm_matrices(a_re, a_im, log_dt, b_re, b_im, c_re, c_im):
    hp = lax.Precision.HIGHEST
    dt = jnp.exp(log_dt)[:, None]
    mag = jnp.exp(a_re * dt)
    lam = (mag * jnp.cos(a_im * dt), mag * jnp.sin(a_im * dt))
    den = a_re * a_re + a_im * a_im
    nr = lam[0] - 1.0
    ni = lam[1]
    z_re = (nr * a_re + ni * a_im) / den
    z_im = (ni * a_re - nr * a_im) / den
    bbar = (z_re[..., None] * b_re - z_im[..., None] * b_im,
            z_re[..., None] * b_im + z_im[..., None] * b_re)
    pw = [(jnp.ones_like(lam[0]), jnp.zeros_like(lam[0])), lam]
    for _ in range(2, Q + 1):
        pw.append(_cmul(pw[-1], lam))
    e = [(c_re * p[0][:, None, :] - c_im * p[1][:, None, :],
          c_re * p[1][:, None, :] + c_im * p[0][:, None, :]) for p in pw]
    k = [jnp.einsum('gcn,gnd->gcd', e[m][0], bbar[0], precision=hp)
         - jnp.einsum('gcn,gnd->gcd', e[m][1], bbar[1], precision=hp) for m in range(Q)]
    eye = jnp.eye(GROUPS_PER_SLAB, dtype=F32)
    split = lambda t: t.reshape((N_SLAB, GROUPS_PER_SLAB) + t.shape[1:])
    zero_k = jnp.zeros_like(k[0])
    kb = jnp.stack([jnp.stack([split(jnp.swapaxes(k[j - i] if j >= i else zero_k, 1, 2))
                               for j in range(Q)]) for i in range(Q)])
    m_mat = jnp.einsum('ijsgdc,gh->sigdjhc', kb, eye).reshape(N_SLAB, Q * LANES, Q * LANES)
    f = [_cmul((pw[Q - 1 - i][0][..., None], pw[Q - 1 - i][1][..., None]), bbar) for i in range(Q)]
    p_parts = []
    for part in range(2):
        fs = jnp.stack([split(f[i][part]) for i in range(Q)])
        p_parts.append(jnp.einsum('isgnd,gh->sigdhn', fs, eye).reshape(N_SLAB, Q * LANES, STATE_LANES // 2))
    p_mat = jnp.concatenate(p_parts, axis=-1)
    r_parts = []
    for part, sign in ((0, 1.0), (1, -1.0)):
        es = jnp.stack([split(e[j + 1][part]) for j in range(Q)])
        r_parts.append(sign * jnp.einsum('jsgcn,gh->shnjgc', es, eye).reshape(
            N_SLAB, STATE_LANES // 2, Q * LANES))
    r_mat = jnp.concatenate(r_parts, axis=1)
    mp = jnp.concatenate([m_mat, p_mat], axis=-1).astype(BF16)
    a_q = pw[Q]
    return (mp, r_mat.astype(BF16),
            a_q[0].reshape(N_SLAB, STATE_LANES // 2), a_q[1].reshape(N_SLAB, STATE_LANES // 2))


def _router_weights(w_rg, b_rg, w_re, b_re):
    pad_g = LANE_EXP0 - LANE_GRP0 - N_GROUPS_MOE
    pad_e = LANES - LANE_EXP0 - N_EXPERTS
    w = jnp.concatenate([w_rg, jnp.zeros((D_MODEL, pad_g), F32), w_re, jnp.zeros((D_MODEL, pad_e), F32)], axis=1)
    b = jnp.concatenate([b_rg, jnp.zeros((pad_g,), F32), b_re, jnp.zeros((pad_e,), F32)]).reshape(1, LANES)
    w_hi = w.astype(BF16)
    w_lo = (w - w_hi.astype(F32)).astype(BF16)
    return jnp.concatenate([w_hi, w_lo], axis=1), w_hi, b


def _time_major_permutation():
    tm = jnp.arange(TM, dtype=I32)
    src = (tm % BATCH) * TT + tm // BATCH
    perm = (src[:, None] == jnp.arange(TM, dtype=I32)[None, :]).astype(BF16)
    return perm, perm.T


def _for_pieces(n, max_piece, act):
    piece = max_piece
    while piece >= 1:
        pl.when((n & piece) != 0)(functools.partial(act, n & ~(2 * piece - 1), piece))
        piece //= 2


def _dispatch_kernel(src_ref, len_ref, dst_ref, zstart_ref, zlen_ref, nused_ref, xst_ref, xs_ref,
                     zero_scr, sem):
    sub = pl.program_id(0)

    def pad_piece(e, off, piece):
        return pltpu.make_async_copy(
            zero_scr.at[pl.ds(0, piece)], xs_ref.at[pl.ds(zstart_ref[e] + off, piece)], sem.at[0])

    def tail_copy(b):
        return pltpu.make_async_copy(zero_scr, xs_ref.at[pl.ds(b * BM, BM)], sem.at[0])

    def run_piece(k, off, piece):
        return pltpu.make_async_copy(
            xst_ref.at[pl.ds(src_ref[k] + off, piece)], xs_ref.at[pl.ds(dst_ref[k] + off, piece)], sem.at[1])

    def over_zero_fill(method):
        def body(e, c):
            _for_pieces(zlen_ref[e], BM // 2, lambda off, piece: getattr(pad_piece(e, off, piece), method)())
            return c
        lax.fori_loop(0, N_EXPERTS, body, 0)
        lax.fori_loop(nused_ref[0], N_BLK, lambda b, c: (getattr(tail_copy(b), method)(), c)[1], 0)

    def over_runs(s, method):
        def body(k, c):
            _for_pieces(len_ref[k], SB, lambda off, piece: getattr(run_piece(k, off, piece), method)())
            return c
        lax.fori_loop(s * N_EXPERTS, (s + 1) * N_EXPERTS, body, 0)

    @pl.when(sub == 0)
    def _zero_fill():
        zero_scr[...] = jnp.zeros(zero_scr.shape, U32)
        over_zero_fill("start")

    over_runs(sub, "start")

    @pl.when(sub > 0)
    def _previous():
        over_runs(sub - 1, "wait")

    @pl.when(sub == N_SUB - 1)
    def _drain():
        over_runs(sub, "wait")
        over_zero_fill("wait")


def _dispatch(src, length, dst, zstart, zlen, nused, xst):
    grid_spec = pltpu.PrefetchScalarGridSpec(
        num_scalar_prefetch=6,
        grid=(N_SUB,),
        in_specs=[pl.BlockSpec(memory_space=pl.ANY)],
        out_specs=pl.BlockSpec(memory_space=pl.ANY),
        scratch_shapes=[pltpu.VMEM((BM,) + ROW_TILE, U32), pltpu.SemaphoreType.DMA((2,))],
    )
    return pl.pallas_call(
        _dispatch_kernel,
        grid_spec=grid_spec,
        out_shape=jax.ShapeDtypeStruct((N_ROWS,) + ROW_TILE, U32),
        compiler_params=pltpu.CompilerParams(dimension_semantics=("arbitrary",)),
        name="dispatch",
    )(src, length, dst, zstart, zlen, nused, xst)


def _expert_kernel(bexp_ref, nused_ref, xs_ref, wg_ref, wu_ref, wd_ref, ys_ref, wg_scr, wu_scr, wd_scr):
    i = pl.program_id(0)
    prev = bexp_ref[jnp.maximum(i - 1, 0)]
    fresh = (i == 0) | (bexp_ref[i] != prev)

    @pl.when(fresh)
    def _cast_weights():
        wg_scr[...] = wg_ref[0].astype(BF16)
        wu_scr[...] = wu_ref[0].astype(BF16)
        wd_scr[...] = wd_ref[0].astype(BF16)

    @pl.when(i < nused_ref[0])
    def _compute():
        lo, hi = _unpack_bf16_pair(xs_ref[...].reshape(BM, HALF))
        lo = lo.astype(BF16)
        hi = hi.astype(BF16)
        g = jnp.dot(lo, wg_scr[0:HALF, :], preferred_element_type=F32) \
            + jnp.dot(hi, wg_scr[HALF:, :], preferred_element_type=F32)
        u = jnp.dot(lo, wu_scr[0:HALF, :], preferred_element_type=F32) \
            + jnp.dot(hi, wu_scr[HALF:, :], preferred_element_type=F32)
        a = (jax.nn.silu(g) * u).astype(BF16)
        o = jnp.dot(a, wd_scr[...], preferred_element_type=F32)
        ys_ref[...] = _pack_bf16_pair(o[:, 0:HALF], o[:, HALF:]).reshape((BM,) + ROW_TILE)

    @pl.when(i >= nused_ref[0])
    def _unused():
        ys_ref[...] = jnp.zeros(ys_ref.shape, U32)


def _experts(bexp, nused, xs, wg, wu, wd):
    grid_spec = pltpu.PrefetchScalarGridSpec(
        num_scalar_prefetch=2,
        grid=(N_BLK,),
        in_specs=[
            pl.BlockSpec((BM,) + ROW_TILE, lambda i, be, nu: (jnp.minimum(i, nu[0] - 1), 0, 0)),
            pl.BlockSpec((1, D_MODEL, D_EXPERT), lambda i, be, nu: (be[i], 0, 0)),
            pl.BlockSpec((1, D_MODEL, D_EXPERT), lambda i, be, nu: (be[i], 0, 0)),
            pl.BlockSpec((1, D_EXPERT, D_MODEL), lambda i, be, nu: (be[i], 0, 0)),
        ],
        out_specs=pl.BlockSpec((BM,) + ROW_TILE, lambda i, be, nu: (i, 0, 0)),
        scratch_shapes=[
            pltpu.VMEM((D_MODEL, D_EXPERT), BF16),
            pltpu.VMEM((D_MODEL, D_EXPERT), BF16),
            pltpu.VMEM((D_EXPERT, D_MODEL), BF16),
        ],
    )
    return pl.pallas_call(
        _expert_kernel,
        grid_spec=grid_spec,
        out_shape=jax.ShapeDtypeStruct((N_ROWS,) + ROW_TILE, U32),
        compiler_params=pltpu.CompilerParams(
            dimension_semantics=("arbitrary",), vmem_limit_bytes=VMEM_LIMIT),
        name="experts",
    )(bexp, nused, xs, wg, wu, wd)


def _combine_kernel(src_ref, len_ref, dst_ref, x1_ref, rec_ref, p_ref, gple_ref, wpg_ref, wple_ref,
                    gfin_ref, ys_ref, out_ref, gat_scr, sem):
    i = pl.program_id(0)
    slot = lax.rem(i, 2)

    def fetch(step, buf, method):
        def run_piece(k, off, piece):
            loc = (src_ref[k] & (2 * TM - 1)) + off
            return pltpu.make_async_copy(
                ys_ref.at[pl.ds(dst_ref[k] + off, piece)], gat_scr.at[buf, pl.ds(loc, piece)], sem.at[buf])

        def body(k, c):
            _for_pieces(len_ref[k], SB, lambda off, piece: getattr(run_piece(k, off, piece), method)())
            return c

        lax.fori_loop(step * NSB * N_EXPERTS, (step + 1) * NSB * N_EXPERTS, body, 0)

    @pl.when(i == 0)
    def _first():
        fetch(0, 0, "start")

    @pl.when(i + 1 < N_STEP)
    def _next():
        fetch(i + 1, 1 - slot, "start")

    ple = jnp.dot(p_ref[0].reshape(TM, D_PLE).astype(BF16), wple_ref[...], preferred_element_type=F32)
    fetch(i, slot, "wait")

    rec = rec_ref[...]
    q = lax.broadcasted_iota(I32, (1, 2 * SB), 1).astype(F32)
    moe_parts = []
    for sb in range(NSB):
        rsb = rec[sb * SB:(sb + 1) * SB]
        pick = jnp.concatenate([(q == rsb[:, REC_POS0:REC_POS0 + 1]), (q == rsb[:, REC_POS1:REC_POS1 + 1])],
                               axis=0).astype(BF16)
        lo, hi = _unpack_bf16_pair(gat_scr[slot, pl.ds(sb * 2 * SB, 2 * SB)].reshape(2 * SB, HALF))
        y_lo = jnp.dot(pick, lo.astype(BF16), preferred_element_type=F32)
        y_hi = jnp.dot(pick, hi.astype(BF16), preferred_element_type=F32)
        w0 = rsb[:, REC_W0:REC_W0 + 1]
        w1 = rsb[:, REC_W1:REC_W1 + 1]
        moe_parts.append(jnp.concatenate(
            [y_lo[0:SB] * w0 + y_lo[SB:] * w1, y_hi[0:SB] * w0 + y_hi[SB:] * w1], axis=1))
    moe = jnp.concatenate(moe_parts, axis=0)
    x2 = x1_ref[...].reshape(TM, D_MODEL) + moe
    gate = jax.nn.sigmoid(jnp.dot(_rms(x2, gple_ref[...]).astype(BF16), wpg_ref[...],
                                  preferred_element_type=F32))
    x3 = x2 + gate * ple
    out_ref[...] = _rms(x3, gfin_ref[...]).reshape(BATCH, TT, D_MODEL)


def _combine(src, length, dst, x1, rec, p, gple, wpg, wple, gfin, ys):
    seq_spec = pl.BlockSpec((BATCH, TT, D_MODEL), lambda i, *_: (0, i, 0))
    const = lambda shape: pl.BlockSpec(shape, lambda i, *_: (0,) * len(shape), pipeline_mode=pl.Buffered(1))
    grid_spec = pltpu.PrefetchScalarGridSpec(
        num_scalar_prefetch=3,
        grid=(N_STEP,),
        in_specs=[
            seq_spec,
            pl.BlockSpec((TM, LANES), lambda i, *_: (i, 0)),
            pl.BlockSpec((1, BATCH, TT, D_PLE), lambda i, *_: (0, 0, i, 0)),
            const((1, D_MODEL)),
            const((D_MODEL, D_MODEL)),
            const((D_PLE, D_MODEL)),
            const((1, D_MODEL)),
            pl.BlockSpec(memory_space=pl.ANY),
        ],
        out_specs=seq_spec,
        scratch_shapes=[pltpu.VMEM((2, 2 * TM) + ROW_TILE, U32), pltpu.SemaphoreType.DMA((2,))],
    )
    return pl.pallas_call(
        _combine_kernel,
        grid_spec=grid_spec,
        out_shape=jax.ShapeDtypeStruct((BATCH, SEQ, D_MODEL), F32),
        compiler_params=pltpu.CompilerParams(
            dimension_semantics=("arbitrary",), vmem_limit_bytes=VMEM_LIMIT),
        name="combine",
    )(src, length, dst, x1, rec, p, gple, wpg, wple, gfin, ys)


def kernel(x, p, g_mix, w_in, b_gate, ssm_a_re, ssm_a_im, ssm_log_dt, ssm_b_re, ssm_b_im, ssm_c_re,
           ssm_c_im, ssm_d, w_glu, conv_dw, conv_dw_b, conv_ln_g, conv_ln_b, w_conv_out, w_out, g_moe,
           w_router_group, b_router_group, w_router_expert, b_router_expert, w_exp_gate, w_exp_up,
           w_exp_down, g_ple, w_ple_gate, w_ple, g_final):
    assert x.shape == (BATCH, SEQ, D_MODEL) and p.shape == (1, BATCH, SEQ, D_PLE)
    row = lambda v: v.reshape(1, -1)

    mp, rmat, a_re, a_im = _ssm_matrices(ssm_a_re[0], ssm_a_im[0], ssm_log_dt[0], ssm_b_re[0],
                                         ssm_b_im[0], ssm_c_re[0], ssm_c_im[0])
    wr1, wr2, br = _router_weights(w_router_group[0], b_router_group[0], w_router_expert[0],
                                   b_router_expert[0])
    perm, permt = _time_major_permutation()
    x1, xst, rec, cblk, cnt = _mixer(
        x, row(g_mix[0]), w_in[0].astype(BF16), row(b_gate[0]), perm, permt, mp, rmat, a_re, a_im,
        row(ssm_d[0]), w_glu[0].astype(BF16), conv_dw[0], row(conv_dw_b[0]), row(conv_ln_g[0]),
        row(conv_ln_b[0]), w_conv_out[0].astype(BF16), w_out[0].astype(BF16), row(g_moe[0]), wr1, wr2, br)

    counts = cnt[0, LANE_EXP0:LANE_EXP0 + N_EXPERTS].astype(I32)
    pcounts = (counts + BM - 1) // BM * BM
    pends = jnp.cumsum(pcounts)
    pstarts = pends - pcounts
    run_len = cblk[:, 0, LANE_EXP0:LANE_EXP0 + N_EXPERTS].astype(I32)
    run_src = (jnp.cumsum(run_len, axis=1) - run_len
               + jnp.arange(N_SUB, dtype=I32)[:, None] * (2 * SB))
    run_dst = pstarts[None, :] + jnp.cumsum(run_len, axis=0) - run_len
    run_len, run_src, run_dst = (t.reshape(N_RUN) for t in (run_len, run_src, run_dst))
    nused = (pends[-1] // BM).astype(I32)
    blk = jnp.minimum(jnp.arange(N_BLK, dtype=I32), nused - 1) * BM
    bexp = jnp.minimum(jnp.sum((pends[None, :] <= blk[:, None]).astype(I32), axis=1), N_EXPERTS - 1)
    zstart = (pstarts + counts).astype(I32)
    zlen = (pcounts - counts).astype(I32)

    nused = nused.reshape(1)
    xs = _dispatch(run_src, run_len, run_dst, zstart, zlen, nused, xst)
    ys = _experts(bexp, nused, xs, w_exp_gate[0], w_exp_up[0], w_exp_down[0])
    return _combine(run_src, run_len, run_dst, x1, rec, p, row(g_ple[0]), w_ple_gate[0].astype(BF16),
                    w_ple[0].astype(BF16), row(g_final), ys)
```

```python
import jax
import jax.numpy as jnp
from jax import lax
from jax.experimental import pallas as pl
from jax.experimental.pallas import tpu as pltpu
from jax.experimental.pallas import tpu_sc as plsc

F32 = jnp.float32
BF16 = jnp.bfloat16
U32 = jnp.uint32
I32 = jnp.int32

D_MODEL = 1024
BATCH = 8
SEQ = 2048
N_TOK = BATCH * SEQ
D_SSM = 512
SSM_GROUP_WIDTH = 16
SSM_GROUPS = 32
SSM_STATE = 64
D_CONV = 512
CONV_WIDTH = 31
D_IN = D_SSM + 2 * D_CONV + 2 * D_MODEL
N_GROUPS_MOE = 4
EXPERTS_PER_GROUP = 8
N_EXPERTS = 32
TOPK = 2
D_EXPERT = 512
D_PLE = 256
EPS = 1e-6

SUBLANES = 8
LANES = 128
assert BATCH == SUBLANES

TT = 64
TM = TT * BATCH
N_STEP = SEQ // TT
SB = 256
NSB = TM // SB
BPS = SB // TT
Q = 2
N_SLAB = D_SSM // LANES
GROUPS_PER_SLAB = SSM_GROUPS // N_SLAB
ROWS_Z = TM // Q
STATE_LANES = 2 * GROUPS_PER_SLAB * SSM_STATE
HALO = (CONV_WIDTH - 1) * BATCH
CHUNK_ROWS = SB // (Q * SUBLANES)
CONV_ROWS = 64
N_LC = D_CONV // LANES

LANE_GRP0 = 0
LANE_EXP0 = 32
REC_EID0, REC_EID1, REC_W0, REC_W1, REC_RANK0, REC_RANK1 = 0, 1, 2, 3, 4, 5
REC_ROWS = 8

BM = 256
N_BLK = (TOPK * N_TOK + N_EXPERTS * (BM - 1) + BM - 1) // BM
N_ROWS = N_BLK * BM
HALF = D_MODEL // 2
ROW_TILE = (HALF // LANES, LANES)
SC_WINDOW = 64

VMEM_LIMIT = 56 * 1024 * 1024


def _const_spec(shape):
    n = len(shape)
    return pl.BlockSpec(shape, lambda *_: (0,) * n, pipeline_mode=pl.Buffered(1))


def _rms(x, g):
    ms = jnp.mean(x * x, axis=-1, keepdims=True)
    return x * lax.rsqrt(ms + EPS) * g


def _pack_bf16_pair(lo, hi):
    ulo = lax.bitcast_convert_type(lo.astype(BF16).astype(F32), U32)
    uhi = lax.bitcast_convert_type(hi.astype(BF16).astype(F32), U32)
    return (ulo >> 16) | (uhi & jnp.uint32(0xFFFF0000))


def _unpack_bf16_pair(w):
    lo = lax.bitcast_convert_type(w << 16, F32)
    hi = lax.bitcast_convert_type(w & jnp.uint32(0xFFFF0000), F32)
    return lo, hi


def _mixer_kernel(x_ref, gmix_ref, win_ref, bgate_ref, perm_ref, permt_ref, mp_ref, r_ref, are_ref,
                  aim_ref, d_ref, wglu_ref, dw_ref, dwb_ref, lng_ref, lnb_ref, wco_ref, wout_ref,
                  gmoe_ref, wr1_ref, wr2_ref, br_ref,
                  x1_ref, h2p_ref, rec_ref, rect_ref, cnt_ref,
                  hb_scr, ht_scr, u_scr, y_scr, yi_scr, xs_scr, z_scr, conv_scr, act_scr, actb_scr,
                  rect_scr, s_scr, cnt_scr):
    step = pl.program_id(0)

    @pl.when(step == 0)
    def _init():
        z_scr[:, 0:HALO, :] = jnp.zeros((N_LC, HALO, LANES), F32)
        s_scr[...] = jnp.zeros(s_scr.shape, F32)
        cnt_scr[...] = jnp.zeros(cnt_scr.shape, F32)

    def sub_rows(r):
        return pl.ds(pl.multiple_of(r * SB, SB), SB)

    def phase_a(r, carry):
        xb = x_ref[pl.ds(r * BPS, BPS)].reshape(SB, D_MODEL)
        hb_scr[sub_rows(r), :] = _rms(xb, gmix_ref[...]).astype(BF16)
        return carry

    lax.fori_loop(0, NSB, phase_a, 0)

    ht_scr[...] = jnp.dot(perm_ref[...], hb_scr[...], preferred_element_type=F32).astype(BF16)

    def phase_a3(r, carry):
        h = ht_scr[sub_rows(r), :]
        u = jnp.dot(h, win_ref[:, 0:D_SSM], preferred_element_type=F32)
        u_scr[pl.ds(r * CHUNK_ROWS, CHUNK_ROWS)] = u.reshape(CHUNK_ROWS, Q, SUBLANES, D_SSM)
        v = jnp.dot(h, win_ref[:, D_SSM:D_SSM + 2 * D_CONV], preferred_element_type=F32)
        zc = v[:, 0:D_CONV] * jax.nn.sigmoid(v[:, D_CONV:])
        for lc in range(N_LC):
            z_scr[lc, pl.ds(pl.multiple_of(HALO + r * SB, SUBLANES), SB), :] = zc[:, lc * LANES:(lc + 1) * LANES]
        return carry

    lax.fori_loop(0, NSB, phase_a3, 0)

    for s in range(N_SLAB):
        lanes = slice(s * LANES, (s + 1) * LANES)
        z = jnp.concatenate(
            [u_scr[:, i, :, lanes].reshape(ROWS_Z, LANES) for i in range(Q)], axis=1).astype(BF16)
        xp = jnp.dot(z, mp_ref[s], preferred_element_type=F32)
        yi_scr[s] = xp[:, 0:Q * LANES]
        xs_scr[s] = xp[:, Q * LANES:]

    half = STATE_LANES // 2
    for s in range(N_SLAB):
        a_re = jnp.broadcast_to(are_ref[s:s + 1, :], (SUBLANES, half))
        a_im = jnp.broadcast_to(aim_ref[s:s + 1, :], (SUBLANES, half))

        def scan_body(k, carry, s=s, a_re=a_re, a_im=a_im):
            s_re, s_im = carry
            rows = pl.ds(pl.multiple_of(k * SUBLANES, SUBLANES), SUBLANES)
            x_re = xs_scr[s, rows, 0:half]
            x_im = xs_scr[s, rows, half:]
            xs_scr[s, rows, 0:half] = s_re
            xs_scr[s, rows, half:] = s_im
            n_re = a_re * s_re - a_im * s_im + x_re
            n_im = a_re * s_im + a_im * s_re + x_im
            return n_re, n_im

        s_re, s_im = lax.fori_loop(0, ROWS_Z // SUBLANES, scan_body,
                                   (s_scr[s, :, 0:half], s_scr[s, :, half:]), unroll=4)
        s_scr[s, :, 0:half] = s_re
        s_scr[s, :, half:] = s_im

    for s in range(N_SLAB):
        lanes = slice(s * LANES, (s + 1) * LANES)
        y_tot = yi_scr[s] + jnp.dot(xs_scr[s].astype(BF16), r_ref[s], preferred_element_type=F32)
        for j in range(Q):
            y_scr[:, j, :, lanes] = y_tot[:, j * LANES:(j + 1) * LANES].reshape(
                ROWS_Z // SUBLANES, SUBLANES, LANES)

    def phase_c1(r, carry):
        rows = sub_rows(r)
        crow = pl.ds(r * CHUNK_ROWS, CHUNK_ROWS)
        y = y_scr[crow].reshape(SB, D_SSM) + d_ref[...] * u_scr[crow].reshape(SB, D_SSM)
        act_scr[rows, 0:D_SSM] = jax.nn.gelu(y).astype(BF16)
        for lc in range(N_LC):
            lanes = slice(lc * LANES, (lc + 1) * LANES)

            def conv_piece(rc, c, lc=lc, lanes=lanes):
                r0 = r * SB + rc * CONV_ROWS
                piece = jnp.broadcast_to(dwb_ref[:, lanes], (CONV_ROWS, LANES))
                for j in range(CONV_WIDTH):
                    zrows = pl.ds(pl.multiple_of(r0 + j * BATCH, SUBLANES), CONV_ROWS)
                    piece = piece + dw_ref[j:j + 1, lanes] * z_scr[lc, zrows, :]
                conv_scr[pl.ds(pl.multiple_of(rc * CONV_ROWS, CONV_ROWS), CONV_ROWS), lanes] = piece
                return c

            lax.fori_loop(0, SB // CONV_ROWS, conv_piece, 0)
        acc = conv_scr[...]
        mu = jnp.mean(acc, axis=-1, keepdims=True)
        cen = acc - mu
        var = jnp.mean(cen * cen, axis=-1, keepdims=True)
        ln = cen * lax.rsqrt(var + EPS) * lng_ref[...] + lnb_ref[...]
        act_scr[rows, D_SSM:] = jax.nn.silu(ln).astype(BF16)
        return carry

    lax.fori_loop(0, NSB, phase_c1, 0)
    z_scr[:, 0:HALO, :] = z_scr[:, TM:TM + HALO, :]

    actb_scr[...] = jnp.dot(permt_ref[...], act_scr[...], preferred_element_type=F32).astype(BF16)

    lane = lax.broadcasted_iota(I32, (1, LANES), 1).astype(F32)
    grp_mask = lane < float(N_GROUPS_MOE)
    exp_lane = (lane >= float(LANE_EXP0)) & (lane < float(LANE_EXP0 + N_EXPERTS))
    lane_grp = jnp.floor((lane - float(LANE_EXP0)) * (1.0 / EXPERTS_PER_GROUP))
    tri = (lax.broadcasted_iota(I32, (SB, SB), 0) > lax.broadcasted_iota(I32, (SB, SB), 1)).astype(BF16)
    neg_inf = float("-inf")
    big = float(4 * LANES)

    def phase_c3(r, carry):
        rows = sub_rows(r)
        h = hb_scr[rows, :]
        g0 = D_SSM + 2 * D_CONV
        gate_ssm = jnp.dot(h, win_ref[:, g0:g0 + D_MODEL], preferred_element_type=F32) \
            + bgate_ref[:, 0:D_MODEL]
        gate_conv = jnp.dot(h, win_ref[:, g0 + D_MODEL:], preferred_element_type=F32) \
            + bgate_ref[:, D_MODEL:]
        zz = jnp.dot(actb_scr[rows, 0:D_SSM], wglu_ref[...], preferred_element_type=F32)
        y_ssm = zz[:, 0:D_MODEL] * jax.nn.sigmoid(zz[:, D_MODEL:])
        y_conv = jnp.dot(actb_scr[rows, D_SSM:], wco_ref[...], preferred_element_type=F32)

        merged = jax.nn.sigmoid(gate_ssm) * y_ssm + jax.nn.sigmoid(gate_conv) * y_conv
        xb = x_ref[pl.ds(r * BPS, BPS)].reshape(SB, D_MODEL)
        x1 = xb + jnp.dot(merged.astype(BF16), wout_ref[...], preferred_element_type=F32)
        x1_ref[pl.ds(r * BPS, BPS)] = x1.reshape(BPS, TT, D_MODEL)

        h2 = _rms(x1, gmoe_ref[...])
        h2p_ref[rows] = _pack_bf16_pair(h2[:, 0:HALF], h2[:, HALF:]).reshape((SB,) + ROW_TILE)

        h2_hi = h2.astype(BF16)
        h2_lo = (h2 - h2_hi.astype(F32)).astype(BF16)
        l1 = jnp.dot(h2_hi, wr1_ref[...], preferred_element_type=F32)
        l2 = jnp.dot(h2_lo, wr2_ref[...], preferred_element_type=F32)
        logits = l1[:, 0:LANES] + l1[:, LANES:] + l2 + br_ref[...]

        lg = jnp.where(grp_mask, logits, neg_inf)
        g_max = jnp.max(lg, axis=-1, keepdims=True)
        g_sel = jnp.min(jnp.where(lg == g_max, lane, big), axis=-1, keepdims=True)
        p_g = 1.0 / jnp.sum(jnp.where(grp_mask, jnp.exp(logits - g_max), 0.0), axis=-1, keepdims=True)
        le = jnp.where(exp_lane & (lane_grp == g_sel), logits, neg_inf)
        m1 = jnp.max(le, axis=-1, keepdims=True)
        i1 = jnp.min(jnp.where(le == m1, lane, big), axis=-1, keepdims=True)
        le2 = jnp.where(lane == i1, neg_inf, le)
        m2 = jnp.max(le2, axis=-1, keepdims=True)
        i2 = jnp.min(jnp.where(le2 == m2, lane, big), axis=-1, keepdims=True)
        e2 = jnp.exp(m2 - m1)
        den = 1.0 + e2
        w_a = (1.0 / den) * p_g
        w_b = (e2 / den) * p_g

        sel1 = lane == i1
        sel2 = lane == i2
        onehot = jnp.where(sel1 | sel2, 1.0, 0.0)
        prefix = jnp.dot(tri, onehot.astype(BF16), preferred_element_type=F32) + cnt_scr[...]
        rank_a = jnp.sum(jnp.where(sel1, prefix, 0.0), axis=-1, keepdims=True)
        rank_b = jnp.sum(jnp.where(sel2, prefix, 0.0), axis=-1, keepdims=True)
        cnt_scr[...] = cnt_scr[...] + jnp.sum(onehot, axis=0, keepdims=True)

        rec = jnp.where(lane == float(REC_EID0), i1 - float(LANE_EXP0), 0.0)
        rec = jnp.where(lane == float(REC_EID1), i2 - float(LANE_EXP0), rec)
        rec = jnp.where(lane == float(REC_W0), w_a, rec)
        rec = jnp.where(lane == float(REC_W1), w_b, rec)
        rec = jnp.where(lane == float(REC_RANK0), rank_a, rec)
        rec = jnp.where(lane == float(REC_RANK1), rank_b, rec)
        rec_ref[rows, :] = rec
        rect_scr[r] = jnp.transpose(rec)[0:REC_ROWS, :]
        return carry

    lax.fori_loop(0, NSB, phase_c3, 0)

    for r in range(NSB):
        rect_ref[:, r * SB:(r + 1) * SB] = rect_scr[r]
    cnt_ref[...] = cnt_scr[...]


def _mixer(x, gmix, win, bgate, perm, permt, mp, rmat, a_re, a_im, dvec, wglu, dw, dwb, lng, lnb, wco,
           wout, gmoe, wr1, wr2, br):
    seq_spec = pl.BlockSpec((BATCH, TT, D_MODEL), lambda i: (0, i, 0))
    in_specs = [
        seq_spec,
        _const_spec((1, D_MODEL)),
        _const_spec((D_MODEL, D_IN)),
        _const_spec((1, 2 * D_MODEL)),
        _const_spec((TM, TM)),
        _const_spec((TM, TM)),
        _const_spec(mp.shape),
        _const_spec(rmat.shape),
        _const_spec(a_re.shape),
        _const_spec(a_im.shape),
        _const_spec((1, D_SSM)),
        _const_spec((D_SSM, 2 * D_MODEL)),
        _const_spec((CONV_WIDTH, D_CONV)),
        _const_spec((1, D_CONV)),
        _const_spec((1, D_CONV)),
        _const_spec((1, D_CONV)),
        _const_spec((D_CONV, D_MODEL)),
        _const_spec((D_MODEL, D_MODEL)),
        _const_spec((1, D_MODEL)),
        _const_spec((D_MODEL, 2 * LANES)),
        _const_spec((D_MODEL, LANES)),
        _const_spec((1, LANES)),
    ]
    out_specs = [
        seq_spec,
        pl.BlockSpec((TM,) + ROW_TILE, lambda i: (i, 0, 0)),
        pl.BlockSpec((TM, LANES), lambda i: (i, 0)),
        pl.BlockSpec((REC_ROWS, TM), lambda i: (0, i)),
        pl.BlockSpec((1, LANES), lambda i: (0, 0)),
    ]
    out_shape = [
        jax.ShapeDtypeStruct((BATCH, SEQ, D_MODEL), F32),
        jax.ShapeDtypeStruct((N_TOK,) + ROW_TILE, U32),
        jax.ShapeDtypeStruct((N_TOK, LANES), F32),
        jax.ShapeDtypeStruct((REC_ROWS, N_TOK), F32),
        jax.ShapeDtypeStruct((1, LANES), F32),
    ]
    chunk_shape = (ROWS_Z // SUBLANES, Q, SUBLANES, D_SSM)
    scratch = [
        pltpu.VMEM((TM, D_MODEL), BF16),
        pltpu.VMEM((TM, D_MODEL), BF16),
        pltpu.VMEM(chunk_shape, F32),
        pltpu.VMEM(chunk_shape, F32),
        pltpu.VMEM((N_SLAB, ROWS_Z, Q * LANES), F32),
        pltpu.VMEM((N_SLAB, ROWS_Z, STATE_LANES), F32),
        pltpu.VMEM((N_LC, HALO + TM, LANES), F32),
        pltpu.VMEM((SB, D_CONV), F32),
        pltpu.VMEM((TM, D_SSM + D_CONV), BF16),
        pltpu.VMEM((TM, D_SSM + D_CONV), BF16),
        pltpu.VMEM((NSB, REC_ROWS, SB), F32),
        pltpu.VMEM((N_SLAB, SUBLANES, STATE_LANES), F32),
        pltpu.VMEM((1, LANES), F32),
    ]
    return pl.pallas_call(
        _mixer_kernel,
        grid=(N_STEP,),
        in_specs=in_specs,
        out_specs=out_specs,
        out_shape=out_shape,
        scratch_shapes=scratch,
        compiler_params=pltpu.CompilerParams(
            dimension_semantics=("arbitrary",), vmem_limit_bytes=VMEM_LIMIT),
        name="mixer",
    )(x, gmix, win, bgate, perm, permt, mp, rmat, a_re, a_im, dvec, wglu, dw, dwb, lng, lnb, wco, wout,
      gmoe, wr1, wr2, br)


def _cmul(a, b):
    return a[0] * b[0] - a[1] * b[1], a[0] * b[1] + a[1] * b[0]


def _ssm_matrices(a_re, a_im, log_dt, b_re, b_im, c_re, c_im):
    hp = lax.Precision.HIGHEST
    dt = jnp.exp(log_dt)[:, None]
    mag = jnp.exp(a_re * dt)
    lam = (mag * jnp.cos(a_im * dt), mag * jnp.sin(a_im * dt))
    den = a_re * a_re + a_im * a_im
    nr = lam[0] - 1.0
    ni = lam[1]
    z_re = (nr * a_re + ni * a_im) / den
    z_im = (ni * a_re - nr * a_im) / den
    bbar = (z_re[..., None] * b_re - z_im[..., None] * b_im,
            z_re[..., None] * b_im + z_im[..., None] * b_re)
    pw = [(jnp.ones_like(lam[0]), jnp.zeros_like(lam[0])), lam]
    for _ in range(2, Q + 1):
        pw.append(_cmul(pw[-1], lam))
    e = [(c_re * p[0][:, None, :] - c_im * p[1][:, None, :],
          c_re * p[1][:, None, :] + c_im * p[0][:, None, :]) for p in pw]
    k = [jnp.einsum('gcn,gnd->gcd', e[m][0], bbar[0], precision=hp)
         - jnp.einsum('gcn,gnd->gcd', e[m][1], bbar[1], precision=hp) for m in range(Q)]
    eye = jnp.eye(GROUPS_PER_SLAB, dtype=F32)
    split = lambda t: t.reshape((N_SLAB, GROUPS_PER_SLAB) + t.shape[1:])
    zero_k = jnp.zeros_like(k[0])
    kb = jnp.stack([jnp.stack([split(jnp.swapaxes(k[j - i] if j >= i else zero_k, 1, 2))
                               for j in range(Q)]) for i in range(Q)])
    m_mat = jnp.einsum('ijsgdc,gh->sigdjhc', kb, eye).reshape(N_SLAB, Q * LANES, Q * LANES)
    f = [_cmul((pw[Q - 1 - i][0][..., None], pw[Q - 1 - i][1][..., None]), bbar) for i in range(Q)]
    p_parts = []
    for part in range(2):
        fs = jnp.stack([split(f[i][part]) for i in range(Q)])
        p_parts.append(jnp.einsum('isgnd,gh->sigdhn', fs, eye).reshape(N_SLAB, Q * LANES, STATE_LANES // 2))
    p_mat = jnp.concatenate(p_parts, axis=-1)
    r_parts = []
    for part, sign in ((0, 1.0), (1, -1.0)):
        es = jnp.stack([split(e[j + 1][part]) for j in range(Q)])
        r_parts.append(sign * jnp.einsum('jsgcn,gh->shnjgc', es, eye).reshape(
            N_SLAB, STATE_LANES // 2, Q * LANES))
    r_mat = jnp.concatenate(r_parts, axis=1)
    mp = jnp.concatenate([m_mat, p_mat], axis=-1).astype(BF16)
    a_q = pw[Q]
    return (mp, r_mat.astype(BF16),
            a_q[0].reshape(N_SLAB, STATE_LANES // 2), a_q[1].reshape(N_SLAB, STATE_LANES // 2))


def _router_weights(w_rg, b_rg, w_re, b_re):
    pad_g = LANE_EXP0 - LANE_GRP0 - N_GROUPS_MOE
    pad_e = LANES - LANE_EXP0 - N_EXPERTS
    w = jnp.concatenate([w_rg, jnp.zeros((D_MODEL, pad_g), F32), w_re, jnp.zeros((D_MODEL, pad_e), F32)], axis=1)
    b = jnp.concatenate([b_rg, jnp.zeros((pad_g,), F32), b_re, jnp.zeros((pad_e,), F32)]).reshape(1, LANES)
    w_hi = w.astype(BF16)
    w_lo = (w - w_hi.astype(F32)).astype(BF16)
    return jnp.concatenate([w_hi, w_lo], axis=1), w_hi, b


def _time_major_permutation():
    tm = jnp.arange(TM, dtype=I32)
    src = (tm % BATCH) * TT + tm // BATCH
    perm = (src[:, None] == jnp.arange(TM, dtype=I32)[None, :]).astype(BF16)
    return perm, perm.T


def _sc_mesh():
    return plsc.VectorSubcoreMesh(core_axis_name="core", subcore_axis_name="subcore")


def _sc_worker(mesh):
    return lax.axis_index("core") * mesh.num_subcores + lax.axis_index("subcore")


def _dispatch(h2p, dest):
    mesh = _sc_mesh()
    n_win = N_TOK // SC_WINDOW
    per_worker = n_win // (mesh.num_cores * mesh.num_subcores)
    assert per_worker * mesh.num_cores * mesh.num_subcores == n_win

    @pl.kernel(out_type=jax.ShapeDtypeStruct((N_ROWS,) + ROW_TILE, U32), mesh=mesh,
               scratch_types=[pltpu.VMEM((SC_WINDOW,), I32), pltpu.VMEM((SC_WINDOW,) + ROW_TILE, U32)])
    def scatter_rows(h_hbm, dest_hbm, xs_hbm, idx_v, rows_v):
        first = _sc_worker(mesh) * per_worker

        @pl.loop(0, per_worker)
        def _(w):
            win = first + w
            pltpu.sync_copy(h_hbm.at[pl.ds(win * SC_WINDOW, SC_WINDOW)], rows_v)
            for j in range(TOPK):
                pltpu.sync_copy(dest_hbm.at[j, win], idx_v)
                pltpu.sync_copy(rows_v, xs_hbm.at[idx_v])

    return scatter_rows(h2p, dest)


def _collect(ys, dest):
    mesh = _sc_mesh()
    n_win = TOPK * N_TOK // SC_WINDOW
    per_worker = n_win // (mesh.num_cores * mesh.num_subcores)
    assert per_worker * mesh.num_cores * mesh.num_subcores == n_win

    @pl.kernel(out_type=jax.ShapeDtypeStruct((TOPK * N_TOK,) + ROW_TILE, U32), mesh=mesh,
               scratch_types=[pltpu.VMEM((SC_WINDOW,), I32), pltpu.VMEM((SC_WINDOW,) + ROW_TILE, U32)])
    def gather_rows(ys_hbm, dest_hbm, yg_hbm, idx_v, rows_v):
        first = _sc_worker(mesh) * per_worker

        @pl.loop(0, per_worker)
        def _(w):
            win = first + w
            pltpu.sync_copy(dest_hbm.at[win], idx_v)
            pltpu.sync_copy(ys_hbm.at[idx_v], rows_v)
            pltpu.sync_copy(rows_v, yg_hbm.at[pl.ds(win * SC_WINDOW, SC_WINDOW)])

    return gather_rows(ys, dest.reshape(n_win, SC_WINDOW)).reshape((TOPK, N_TOK) + ROW_TILE)


def _expert_kernel(bexp_ref, nvalid_ref, nused_ref, xs_ref, wg_ref, wu_ref, wd_ref, ys_ref,
                   wg_scr, wu_scr, wd_scr):
    i = pl.program_id(0)
    prev = bexp_ref[jnp.maximum(i - 1, 0)]
    fresh = (i == 0) | (bexp_ref[i] != prev)

    @pl.when(fresh)
    def _cast_weights():
        wg_scr[...] = wg_ref[0].astype(BF16)
        wu_scr[...] = wu_ref[0].astype(BF16)
        wd_scr[...] = wd_ref[0].astype(BF16)

    @pl.when(i < nused_ref[0])
    def _compute():
        valid = lax.broadcasted_iota(I32, (BM, 1), 0) < nvalid_ref[i]
        lo, hi = _unpack_bf16_pair(jnp.where(valid, xs_ref[...].reshape(BM, HALF), jnp.uint32(0)))
        lo = lo.astype(BF16)
        hi = hi.astype(BF16)
        g = jnp.dot(lo, wg_scr[0:HALF, :], preferred_element_type=F32) \
            + jnp.dot(hi, wg_scr[HALF:, :], preferred_element_type=F32)
        u = jnp.dot(lo, wu_scr[0:HALF, :], preferred_element_type=F32) \
            + jnp.dot(hi, wu_scr[HALF:, :], preferred_element_type=F32)
        a = (jax.nn.silu(g) * u).astype(BF16)
        o = jnp.dot(a, wd_scr[...], preferred_element_type=F32)
        ys_ref[...] = _pack_bf16_pair(o[:, 0:HALF], o[:, HALF:]).reshape((BM,) + ROW_TILE)

    @pl.when(i >= nused_ref[0])
    def _unused():
        ys_ref[...] = jnp.zeros(ys_ref.shape, U32)


def _experts(bexp, nvalid, nused, xs, wg, wu, wd):
    grid_spec = pltpu.PrefetchScalarGridSpec(
        num_scalar_prefetch=3,
        grid=(N_BLK,),
        in_specs=[
            pl.BlockSpec((BM,) + ROW_TILE, lambda i, be, nv, nu: (jnp.minimum(i, nu[0] - 1), 0, 0)),
            pl.BlockSpec((1, D_MODEL, D_EXPERT), lambda i, be, nv, nu: (be[i], 0, 0)),
            pl.BlockSpec((1, D_MODEL, D_EXPERT), lambda i, be, nv, nu: (be[i], 0, 0)),
            pl.BlockSpec((1, D_EXPERT, D_MODEL), lambda i, be, nv, nu: (be[i], 0, 0)),
        ],
        out_specs=pl.BlockSpec((BM,) + ROW_TILE, lambda i, be, nv, nu: (i, 0, 0)),
        scratch_shapes=[
            pltpu.VMEM((D_MODEL, D_EXPERT), BF16),
            pltpu.VMEM((D_MODEL, D_EXPERT), BF16),
            pltpu.VMEM((D_EXPERT, D_MODEL), BF16),
        ],
    )
    return pl.pallas_call(
        _expert_kernel,
        grid_spec=grid_spec,
        out_shape=jax.ShapeDtypeStruct((N_ROWS,) + ROW_TILE, U32),
        compiler_params=pltpu.CompilerParams(
            dimension_semantics=("arbitrary",), vmem_limit_bytes=VMEM_LIMIT),
        name="experts",
    )(bexp, nvalid, nused, xs, wg, wu, wd)


def _combine_kernel(x1_ref, rec_ref, yg_ref, p_ref, gple_ref, wpg_ref, wple_ref, gfin_ref, out_ref):
    ple = jnp.dot(p_ref[0].reshape(TM, D_PLE).astype(BF16), wple_ref[...], preferred_element_type=F32)
    rec = rec_ref[...]
    w0 = rec[:, REC_W0:REC_W0 + 1]
    w1 = rec[:, REC_W1:REC_W1 + 1]
    lo0, hi0 = _unpack_bf16_pair(yg_ref[0].reshape(TM, HALF))
    lo1, hi1 = _unpack_bf16_pair(yg_ref[1].reshape(TM, HALF))
    moe = jnp.concatenate([lo0 * w0 + lo1 * w1, hi0 * w0 + hi1 * w1], axis=1)
    x2 = x1_ref[...].reshape(TM, D_MODEL) + moe
    gate = jax.nn.sigmoid(jnp.dot(_rms(x2, gple_ref[...]).astype(BF16), wpg_ref[...],
                                  preferred_element_type=F32))
    x3 = x2 + gate * ple
    out_ref[...] = _rms(x3, gfin_ref[...]).reshape(BATCH, TT, D_MODEL)


def _combine(x1, rec, yg, p, gple, wpg, wple, gfin):
    seq_spec = pl.BlockSpec((BATCH, TT, D_MODEL), lambda i: (0, i, 0))
    return pl.pallas_call(
        _combine_kernel,
        grid=(N_STEP,),
        in_specs=[
            seq_spec,
            pl.BlockSpec((TM, LANES), lambda i: (i, 0)),
            pl.BlockSpec((TOPK, TM) + ROW_TILE, lambda i: (0, i, 0, 0)),
            pl.BlockSpec((1, BATCH, TT, D_PLE), lambda i: (0, 0, i, 0)),
            _const_spec((1, D_MODEL)),
            _const_spec((D_MODEL, D_MODEL)),
            _const_spec((D_PLE, D_MODEL)),
            _const_spec((1, D_MODEL)),
        ],
        out_specs=seq_spec,
        out_shape=jax.ShapeDtypeStruct((BATCH, SEQ, D_MODEL), F32),
        compiler_params=pltpu.CompilerParams(
            dimension_semantics=("arbitrary",), vmem_limit_bytes=VMEM_LIMIT),
        name="combine",
    )(x1, rec, yg, p, gple, wpg, wple, gfin)


def kernel(x, p, g_mix, w_in, b_gate, ssm_a_re, ssm_a_im, ssm_log_dt, ssm_b_re, ssm_b_im, ssm_c_re,
           ssm_c_im, ssm_d, w_glu, conv_dw, conv_dw_b, conv_ln_g, conv_ln_b, w_conv_out, w_out, g_moe,
           w_router_group, b_router_group, w_router_expert, b_router_expert, w_exp_gate, w_exp_up,
           w_exp_down, g_ple, w_ple_gate, w_ple, g_final):
    assert x.shape == (BATCH, SEQ, D_MODEL) and p.shape == (1, BATCH, SEQ, D_PLE)
    row = lambda v: v.reshape(1, -1)

    mp, rmat, a_re, a_im = _ssm_matrices(ssm_a_re[0], ssm_a_im[0], ssm_log_dt[0], ssm_b_re[0],
                                         ssm_b_im[0], ssm_c_re[0], ssm_c_im[0])
    wr1, wr2, br = _router_weights(w_router_group[0], b_router_group[0], w_router_expert[0],
                                   b_router_expert[0])
    perm, permt = _time_major_permutation()
    x1, h2p, rec, rect, cnt = _mixer(
        x, row(g_mix[0]), w_in[0].astype(BF16), row(b_gate[0]), perm, permt, mp, rmat, a_re, a_im,
        row(ssm_d[0]), w_glu[0].astype(BF16), conv_dw[0], row(conv_dw_b[0]), row(conv_ln_g[0]),
        row(conv_ln_b[0]), w_conv_out[0].astype(BF16), w_out[0].astype(BF16), row(g_moe[0]), wr1, wr2, br)

    counts = cnt[0, LANE_EXP0:LANE_EXP0 + N_EXPERTS].astype(I32)
    pcounts = (counts + BM - 1) // BM * BM
    pends = jnp.cumsum(pcounts)
    pstarts = pends - pcounts
    eid = rect[REC_EID0:REC_EID1 + 1].astype(I32)
    rank = rect[REC_RANK0:REC_RANK1 + 1].astype(I32)
    dest = (jnp.sum(jnp.where(eid[..., None] == jnp.arange(N_EXPERTS, dtype=I32), pstarts, 0), axis=-1)
            + rank).reshape(TOPK, N_TOK // SC_WINDOW, SC_WINDOW)
    nused = (pends[-1] // BM).astype(I32)
    blk = jnp.minimum(jnp.arange(N_BLK, dtype=I32), nused - 1) * BM
    bexp = jnp.minimum(jnp.sum((pends[None, :] <= blk[:, None]).astype(I32), axis=1), N_EXPERTS - 1)
    nvalid = jnp.clip(pstarts[bexp] + counts[bexp] - blk, 0, BM).astype(I32)

    xs = _dispatch(h2p, dest)
    ys = _experts(bexp, nvalid, nused.reshape(1), xs, w_exp_gate[0], w_exp_up[0], w_exp_down[0])
    yg = _collect(ys, dest)
    return _combine(x1, rec, yg, p, row(g_ple[0]), w_ple_gate[0].astype(BF16), w_ple[0].astype(BF16),
                    row(g_final))
```

```python
import jax
import jax.numpy as jnp
from jax import lax
from jax.experimental import pallas as pl
from jax.experimental.pallas import tpu as pltpu
from jax.experimental.pallas import tpu_sc as plsc

F32 = jnp.float32
BF16 = jnp.bfloat16
U32 = jnp.uint32
I32 = jnp.int32

D_MODEL = 1024
BATCH = 8
SEQ = 2048
N_TOK = BATCH * SEQ
D_SSM = 512
SSM_GROUP_WIDTH = 16
SSM_GROUPS = 32
SSM_STATE = 64
D_CONV = 512
CONV_WIDTH = 31
D_IN = D_SSM + 2 * D_CONV + 2 * D_MODEL
N_GROUPS_MOE = 4
EXPERTS_PER_GROUP = 8
N_EXPERTS = 32
TOPK = 2
D_EXPERT = 512
D_PLE = 256
EPS = 1e-6

SUBLANES = 8
LANES = 128
assert BATCH == SUBLANES

TT = 64
TM = TT * BATCH
N_STEP = SEQ // TT
SB = 256
NSB = TM // SB
BPS = SB // TT
Q = 2
N_SLAB = D_SSM // LANES
GROUPS_PER_SLAB = SSM_GROUPS // N_SLAB
ROWS_Z = TM // Q
STATE_LANES = 2 * GROUPS_PER_SLAB * SSM_STATE
HALO = (CONV_WIDTH - 1) * BATCH
CHUNK_ROWS = SB // (Q * SUBLANES)
CONV_ROWS = 64
N_LC = D_CONV // LANES

LANE_GRP0 = 0
LANE_EXP0 = 32
REC_EID0, REC_EID1, REC_W0, REC_W1, REC_RANK0, REC_RANK1 = 0, 1, 2, 3, 4, 5
REC_ROWS = 8

BM = 256
N_BLK = (TOPK * N_TOK + N_EXPERTS * (BM - 1) + BM - 1) // BM
N_ROWS = N_BLK * BM
HALF = D_MODEL // 2
ROW_TILE = (HALF // LANES, LANES)
SC_WINDOW = 64

VMEM_LIMIT = 56 * 1024 * 1024


def _const_spec(shape):
    n = len(shape)
    return pl.BlockSpec(shape, lambda *_: (0,) * n, pipeline_mode=pl.Buffered(1))


def _rms(x, g):
    ms = jnp.mean(x * x, axis=-1, keepdims=True)
    return x * lax.rsqrt(ms + EPS) * g


def _pack_bf16_pair(lo, hi):
    ulo = lax.bitcast_convert_type(lo.astype(BF16).astype(F32), U32)
    uhi = lax.bitcast_convert_type(hi.astype(BF16).astype(F32), U32)
    return (ulo >> 16) | (uhi & jnp.uint32(0xFFFF0000))


def _unpack_bf16_pair(w):
    lo = lax.bitcast_convert_type(w << 16, F32)
    hi = lax.bitcast_convert_type(w & jnp.uint32(0xFFFF0000), F32)
    return lo, hi


def _mixer_kernel(x_ref, gmix_ref, win_ref, bgate_ref, perm_ref, permt_ref, mp_ref, r_ref, are_ref,
                  aim_ref, d_ref, wglu_ref, dw_ref, dwb_ref, lng_ref, lnb_ref, wco_ref, wout_ref,
                  gmoe_ref, wr1_ref, wr2_ref, br_ref,
                  x1_ref, h2p_ref, rec_ref, rect_ref, cnt_ref,
                  hb_scr, ht_scr, u_scr, y_scr, yi_scr, xs_scr, z_scr, conv_scr, act_scr, actb_scr,
                  rect_scr, s_scr, cnt_scr):
    step = pl.program_id(0)

    @pl.when(step == 0)
    def _init():
        z_scr[:, 0:HALO, :] = jnp.zeros((N_LC, HALO, LANES), F32)
        s_scr[...] = jnp.zeros(s_scr.shape, F32)
        cnt_scr[...] = jnp.zeros(cnt_scr.shape, F32)

    def sub_rows(r):
        return pl.ds(pl.multiple_of(r * SB, SB), SB)

    def phase_a(r, carry):
        xb = x_ref[pl.ds(r * BPS, BPS)].reshape(SB, D_MODEL)
        hb_scr[sub_rows(r), :] = _rms(xb, gmix_ref[...]).astype(BF16)
        return carry

    lax.fori_loop(0, NSB, phase_a, 0)

    ht_scr[...] = jnp.dot(perm_ref[...], hb_scr[...], preferred_element_type=F32).astype(BF16)

    def phase_a3(r, carry):
        h = ht_scr[sub_rows(r), :]
        u = jnp.dot(h, win_ref[:, 0:D_SSM], preferred_element_type=F32)
        u_scr[pl.ds(r * CHUNK_ROWS, CHUNK_ROWS)] = u.reshape(CHUNK_ROWS, Q, SUBLANES, D_SSM)
        v = jnp.dot(h, win_ref[:, D_SSM:D_SSM + 2 * D_CONV], preferred_element_type=F32)
        zc = v[:, 0:D_CONV] * jax.nn.sigmoid(v[:, D_CONV:])
        for lc in range(N_LC):
            z_scr[lc, pl.ds(pl.multiple_of(HALO + r * SB, SUBLANES), SB), :] = zc[:, lc * LANES:(lc + 1) * LANES]
        return carry

    lax.fori_loop(0, NSB, phase_a3, 0)

    for s in range(N_SLAB):
        lanes = slice(s * LANES, (s + 1) * LANES)
        z = jnp.concatenate(
            [u_scr[:, i, :, lanes].reshape(ROWS_Z, LANES) for i in range(Q)], axis=1).astype(BF16)
        xp = jnp.dot(z, mp_ref[s], preferred_element_type=F32)
        yi_scr[s] = xp[:, 0:Q * LANES]
        xs_scr[s] = xp[:, Q * LANES:]

    half = STATE_LANES // 2
    for s in range(N_SLAB):
        a_re = jnp.broadcast_to(are_ref[s:s + 1, :], (SUBLANES, half))
        a_im = jnp.broadcast_to(aim_ref[s:s + 1, :], (SUBLANES, half))

        def scan_body(k, carry, s=s, a_re=a_re, a_im=a_im):
            s_re, s_im = carry
            rows = pl.ds(pl.multiple_of(k * SUBLANES, SUBLANES), SUBLANES)
            x_re = xs_scr[s, rows, 0:half]
            x_im = xs_scr[s, rows, half:]
            xs_scr[s, rows, 0:half] = s_re
            xs_scr[s, rows, half:] = s_im
            n_re = a_re * s_re - a_im * s_im + x_re
            n_im = a_re * s_im + a_im * s_re + x_im
            return n_re, n_im

        s_re, s_im = lax.fori_loop(0, ROWS_Z // SUBLANES, scan_body,
                                   (s_scr[s, :, 0:half], s_scr[s, :, half:]), unroll=4)
        s_scr[s, :, 0:half] = s_re
        s_scr[s, :, half:] = s_im

    for s in range(N_SLAB):
        lanes = slice(s * LANES, (s + 1) * LANES)
        y_tot = yi_scr[s] + jnp.dot(xs_scr[s].astype(BF16), r_ref[s], preferred_element_type=F32)
        for j in range(Q):
            y_scr[:, j, :, lanes] = y_tot[:, j * LANES:(j + 1) * LANES].reshape(
                ROWS_Z // SUBLANES, SUBLANES, LANES)

    def phase_c1(r, carry):
        rows = sub_rows(r)
        crow = pl.ds(r * CHUNK_ROWS, CHUNK_ROWS)
        y = y_scr[crow].reshape(SB, D_SSM) + d_ref[...] * u_scr[crow].reshape(SB, D_SSM)
        act_scr[rows, 0:D_SSM] = jax.nn.gelu(y).astype(BF16)
        for lc in range(N_LC):
            lanes = slice(lc * LANES, (lc + 1) * LANES)

            def conv_piece(rc, c, lc=lc, lanes=lanes):
                r0 = r * SB + rc * CONV_ROWS
                piece = jnp.broadcast_to(dwb_ref[:, lanes], (CONV_ROWS, LANES))
                for j in range(CONV_WIDTH):
                    zrows = pl.ds(pl.multiple_of(r0 + j * BATCH, SUBLANES), CONV_ROWS)
                    piece = piece + dw_ref[j:j + 1, lanes] * z_scr[lc, zrows, :]
                conv_scr[pl.ds(pl.multiple_of(rc * CONV_ROWS, CONV_ROWS), CONV_ROWS), lanes] = piece
                return c

            lax.fori_loop(0, SB // CONV_ROWS, conv_piece, 0)
        acc = conv_scr[...]
        mu = jnp.mean(acc, axis=-1, keepdims=True)
        cen = acc - mu
        var = jnp.mean(cen * cen, axis=-1, keepdims=True)
        ln = cen * lax.rsqrt(var + EPS) * lng_ref[...] + lnb_ref[...]
        act_scr[rows, D_SSM:] = jax.nn.silu(ln).astype(BF16)
        return carry

    lax.fori_loop(0, NSB, phase_c1, 0)
    z_scr[:, 0:HALO, :] = z_scr[:, TM:TM + HALO, :]

    actb_scr[...] = jnp.dot(permt_ref[...], act_scr[...], preferred_element_type=F32).astype(BF16)

    lane = lax.broadcasted_iota(I32, (1, LANES), 1).astype(F32)
    grp_mask = lane < float(N_GROUPS_MOE)
    exp_lane = (lane >= float(LANE_EXP0)) & (lane < float(LANE_EXP0 + N_EXPERTS))
    lane_grp = jnp.floor((lane - float(LANE_EXP0)) * (1.0 / EXPERTS_PER_GROUP))
    tri = (lax.broadcasted_iota(I32, (SB, SB), 0) > lax.broadcasted_iota(I32, (SB, SB), 1)).astype(BF16)
    neg_inf = float("-inf")
    big = float(4 * LANES)

    def phase_c3(r, carry):
        rows = sub_rows(r)
        h = hb_scr[rows, :]
        g0 = D_SSM + 2 * D_CONV
        gate_ssm = jnp.dot(h, win_ref[:, g0:g0 + D_MODEL], preferred_element_type=F32) \
            + bgate_ref[:, 0:D_MODEL]
        gate_conv = jnp.dot(h, win_ref[:, g0 + D_MODEL:], preferred_element_type=F32) \
            + bgate_ref[:, D_MODEL:]
        zz = jnp.dot(actb_scr[rows, 0:D_SSM], wglu_ref[...], preferred_element_type=F32)
        y_ssm = zz[:, 0:D_MODEL] * jax.nn.sigmoid(zz[:, D_MODEL:])
        y_conv = jnp.dot(actb_scr[rows, D_SSM:], wco_ref[...], preferred_element_type=F32)

        merged = jax.nn.sigmoid(gate_ssm) * y_ssm + jax.nn.sigmoid(gate_conv) * y_conv
        xb = x_ref[pl.ds(r * BPS, BPS)].reshape(SB, D_MODEL)
        x1 = xb + jnp.dot(merged.astype(BF16), wout_ref[...], preferred_element_type=F32)
        x1_ref[pl.ds(r * BPS, BPS)] = x1.reshape(BPS, TT, D_MODEL)

        h2 = _rms(x1, gmoe_ref[...])
        h2p_ref[rows] = _pack_bf16_pair(h2[:, 0:HALF], h2[:, HALF:]).reshape((SB,) + ROW_TILE)

        h2_hi = h2.astype(BF16)
        h2_lo = (h2 - h2_hi.astype(F32)).astype(BF16)
        l1 = jnp.dot(h2_hi, wr1_ref[...], preferred_element_type=F32)
        l2 = jnp.dot(h2_lo, wr2_ref[...], preferred_element_type=F32)
        logits = l1[:, 0:LANES] + l1[:, LANES:] + l2 + br_ref[...]

        lg = jnp.where(grp_mask, logits, neg_inf)
        g_max = jnp.max(lg, axis=-1, keepdims=True)
        g_sel = jnp.min(jnp.where(lg == g_max, lane, big), axis=-1, keepdims=True)
        p_g = 1.0 / jnp.sum(jnp.where(grp_mask, jnp.exp(logits - g_max), 0.0), axis=-1, keepdims=True)
        le = jnp.where(exp_lane & (lane_grp == g_sel), logits, neg_inf)
        m1 = jnp.max(le, axis=-1, keepdims=True)
        i1 = jnp.min(jnp.where(le == m1, lane, big), axis=-1, keepdims=True)
        le2 = jnp.where(lane == i1, neg_inf, le)
        m2 = jnp.max(le2, axis=-1, keepdims=True)
        i2 = jnp.min(jnp.where(le2 == m2, lane, big), axis=-1, keepdims=True)
        e2 = jnp.exp(m2 - m1)
        den = 1.0 + e2
        w_a = (1.0 / den) * p_g
        w_b = (e2 / den) * p_g

        sel1 = lane == i1
        sel2 = lane == i2
        onehot = jnp.where(sel1 | sel2, 1.0, 0.0)
        prefix = jnp.dot(tri, onehot.astype(BF16), preferred_element_type=F32) + cnt_scr[...]
        rank_a = jnp.sum(jnp.where(sel1, prefix, 0.0), axis=-1, keepdims=True)
        rank_b = jnp.sum(jnp.where(sel2, prefix, 0.0), axis=-1, keepdims=True)
        cnt_scr[...] = cnt_scr[...] + jnp.sum(onehot, axis=0, keepdims=True)

        rec = jnp.where(lane == float(REC_EID0), i1 - float(LANE_EXP0), 0.0)
        rec = jnp.where(lane == float(REC_EID1), i2 - float(LANE_EXP0), rec)
        rec = jnp.where(lane == float(REC_W0), w_a, rec)
        rec = jnp.where(lane == float(REC_W1), w_b, rec)
        rec = jnp.where(lane == float(REC_RANK0), rank_a, rec)
        rec = jnp.where(lane == float(REC_RANK1), rank_b, rec)
        rec_ref[rows, :] = rec
        rect_scr[r] = jnp.transpose(rec)[0:REC_ROWS, :]
        return carry

    lax.fori_loop(0, NSB, phase_c3, 0)

    for r in range(NSB):
        rect_ref[:, r * SB:(r + 1) * SB] = rect_scr[r]
    cnt_ref[...] = cnt_scr[...]


def _mixer(x, gmix, win, bgate, perm, permt, mp, rmat, a_re, a_im, dvec, wglu, dw, dwb, lng, lnb, wco,
           wout, gmoe, wr1, wr2, br):
    seq_spec = pl.BlockSpec((BATCH, TT, D_MODEL), lambda i: (0, i, 0))
    in_specs = [
        seq_spec,
        _const_spec((1, D_MODEL)),
        _const_spec((D_MODEL, D_IN)),
        _const_spec((1, 2 * D_MODEL)),
        _const_spec((TM, TM)),
        _const_spec((TM, TM)),
        _const_spec(mp.shape),
        _const_spec(rmat.shape),
        _const_spec(a_re.shape),
        _const_spec(a_im.shape),
        _const_spec((1, D_SSM)),
        _const_spec((D_SSM, 2 * D_MODEL)),
        _const_spec((CONV_WIDTH, D_CONV)),
        _const_spec((1, D_CONV)),
        _const_spec((1, D_CONV)),
        _const_spec((1, D_CONV)),
        _const_spec((D_CONV, D_MODEL)),
        _const_spec((D_MODEL, D_MODEL)),
        _const_spec((1, D_MODEL)),
        _const_spec((D_MODEL, 2 * LANES)),
        _const_spec((D_MODEL, LANES)),
        _const_spec((1, LANES)),
    ]
    out_specs = [
        seq_spec,
        pl.BlockSpec((TM,) + ROW_TILE, lambda i: (i, 0, 0)),
        pl.BlockSpec((TM, LANES), lambda i: (i, 0)),
        pl.BlockSpec((REC_ROWS, TM), lambda i: (0, i)),
        pl.BlockSpec((1, LANES), lambda i: (0, 0)),
    ]
    out_shape = [
        jax.ShapeDtypeStruct((BATCH, SEQ, D_MODEL), F32),
        jax.ShapeDtypeStruct((N_TOK,) + ROW_TILE, U32),
        jax.ShapeDtypeStruct((N_TOK, LANES), F32),
        jax.ShapeDtypeStruct((REC_ROWS, N_TOK), F32),
        jax.ShapeDtypeStruct((1, LANES), F32),
    ]
    chunk_shape = (ROWS_Z // SUBLANES, Q, SUBLANES, D_SSM)
    scratch = [
        pltpu.VMEM((TM, D_MODEL), BF16),
        pltpu.VMEM((TM, D_MODEL), BF16),
        pltpu.VMEM(chunk_shape, F32),
        pltpu.VMEM(chunk_shape, F32),
        pltpu.VMEM((N_SLAB, ROWS_Z, Q * LANES), F32),
        pltpu.VMEM((N_SLAB, ROWS_Z, STATE_LANES), F32),
        pltpu.VMEM((N_LC, HALO + TM, LANES), F32),
        pltpu.VMEM((SB, D_CONV), F32),
        pltpu.VMEM((TM, D_SSM + D_CONV), BF16),
        pltpu.VMEM((TM, D_SSM + D_CONV), BF16),
        pltpu.VMEM((NSB, REC_ROWS, SB), F32),
        pltpu.VMEM((N_SLAB, SUBLANES, STATE_LANES), F32),
        pltpu.VMEM((1, LANES), F32),
    ]
    return pl.pallas_call(
        _mixer_kernel,
        grid=(N_STEP,),
        in_specs=in_specs,
        out_specs=out_specs,
        out_shape=out_shape,
        scratch_shapes=scratch,
        compiler_params=pltpu.CompilerParams(
            dimension_semantics=("arbitrary",), vmem_limit_bytes=VMEM_LIMIT),
        name="mixer",
    )(x, gmix, win, bgate, perm, permt, mp, rmat, a_re, a_im, dvec, wglu, dw, dwb, lng, lnb, wco, wout,
      gmoe, wr1, wr2, br)


def _cmul(a, b):
    return a[0] * b[0] - a[1] * b[1], a[0] * b[1] + a[1] * b[0]


def _ssm_matrices(a_re, a_im, log_dt, b_re, b_im, c_re, c_im):
    hp = lax.Precision.HIGHEST
    dt = jnp.exp(log_dt)[:, None]
    mag = jnp.exp(a_re * dt)
    lam = (mag * jnp.cos(a_im * dt), mag * jnp.sin(a_im * dt))
    den = a_re * a_re + a_im * a_im
    nr = lam[0] - 1.0
    ni = lam[1]
    z_re = (nr * a_re + ni * a_im) / den
    z_im = (ni * a_re - nr * a_im) / den
    bbar = (z_re[..., None] * b_re - z_im[..., None] * b_im,
            z_re[..., None] * b_im + z_im[..., None] * b_re)
    pw = [(jnp.ones_like(lam[0]), jnp.zeros_like(lam[0])), lam]
    for _ in range(2, Q + 1):
        pw.append(_cmul(pw[-1], lam))
    e = [(c_re * p[0][:, None, :] - c_im * p[1][:, None, :],
          c_re * p[1][:, None, :] + c_im * p[0][:, None, :]) for p in pw]
    k = [jnp.einsum('gcn,gnd->gcd', e[m][0], bbar[0], precision=hp)
         - jnp.einsum('gcn,gnd->gcd', e[m][1], bbar[1], precision=hp) for m in range(Q)]
    eye = jnp.eye(GROUPS_PER_SLAB, dtype=F32)
    split = lambda t: t.reshape((N_SLAB, GROUPS_PER_SLAB) + t.shape[1:])
    zero_k = jnp.zeros_like(k[0])
    kb = jnp.stack([jnp.stack([split(jnp.swapaxes(k[j - i] if j >= i else zero_k, 1, 2))
                               for j in range(Q)]) for i in range(Q)])
    m_mat = jnp.einsum('ijsgdc,gh->sigdjhc', kb, eye).reshape(N_SLAB, Q * LANES, Q * LANES)
    f = [_cmul((pw[Q - 1 - i][0][..., None], pw[Q - 1 - i][1][..., None]), bbar) for i in range(Q)]
    p_parts = []
    for part in range(2):
        fs = jnp.stack([split(f[i][part]) for i in range(Q)])
        p_parts.append(jnp.einsum('isgnd,gh->sigdhn', fs, eye).reshape(N_SLAB, Q * LANES, STATE_LANES // 2))
    p_mat = jnp.concatenate(p_parts, axis=-1)
    r_parts = []
    for part, sign in ((0, 1.0), (1, -1.0)):
        es = jnp.stack([split(e[j + 1][part]) for j in range(Q)])
        r_parts.append(sign * jnp.einsum('jsgcn,gh->shnjgc', es, eye).reshape(
            N_SLAB, STATE_LANES // 2, Q * LANES))
    r_mat = jnp.concatenate(r_parts, axis=1)
    mp = jnp.concatenate([m_mat, p_mat], axis=-1).astype(BF16)
    a_q = pw[Q]
    return (mp, r_mat.astype(BF16),
            a_q[0].reshape(N_SLAB, STATE_LANES // 2), a_q[1].reshape(N_SLAB, STATE_LANES // 2))


def _router_weights(w_rg, b_rg, w_re, b_re):
    pad_g = LANE_EXP0 - LANE_GRP0 - N_GROUPS_MOE
    pad_e = LANES - LANE_EXP0 - N_EXPERTS
    w = jnp.concatenate([w_rg, jnp.zeros((D_MODEL, pad_g), F32), w_re, jnp.zeros((D_MODEL, pad_e), F32)], axis=1)
    b = jnp.concatenate([b_rg, jnp.zeros((pad_g,), F32), b_re, jnp.zeros((pad_e,), F32)]).reshape(1, LANES)
    w_hi = w.astype(BF16)
    w_lo = (w - w_hi.astype(F32)).astype(BF16)
    return jnp.concatenate([w_hi, w_lo], axis=1), w_hi, b


def _time_major_permutation():
    tm = jnp.arange(TM, dtype=I32)
    src = (tm % BATCH) * TT + tm // BATCH
    perm = (src[:, None] == jnp.arange(TM, dtype=I32)[None, :]).astype(BF16)
    return perm, perm.T


def _sc_mesh():
    return plsc.VectorSubcoreMesh(core_axis_name="core", subcore_axis_name="subcore")


def _sc_worker(mesh):
    return lax.axis_index("core") * mesh.num_subcores + lax.axis_index("subcore")


def _dispatch(h2p, dest):
    mesh = _sc_mesh()
    n_win = N_TOK // SC_WINDOW
    per_worker = n_win // (mesh.num_cores * mesh.num_subcores)
    assert per_worker * mesh.num_cores * mesh.num_subcores == n_win

    @pl.kernel(out_type=jax.ShapeDtypeStruct((N_ROWS,) + ROW_TILE, U32), mesh=mesh,
               scratch_types=[pltpu.VMEM((SC_WINDOW,), I32), pltpu.VMEM((SC_WINDOW,) + ROW_TILE, U32)])
    def scatter_rows(h_hbm, dest_hbm, xs_hbm, idx_v, rows_v):
        first = _sc_worker(mesh) * per_worker

        @pl.loop(0, per_worker)
        def _(w):
            win = first + w
            pltpu.sync_copy(h_hbm.at[pl.ds(win * SC_WINDOW, SC_WINDOW)], rows_v)
            for j in range(TOPK):
                pltpu.sync_copy(dest_hbm.at[j, win], idx_v)
                pltpu.sync_copy(rows_v, xs_hbm.at[idx_v])

    return scatter_rows(h2p, dest)


def _collect(ys, dest):
    mesh = _sc_mesh()
    n_win = TOPK * N_TOK // SC_WINDOW
    per_worker = n_win // (mesh.num_cores * mesh.num_subcores)
    assert per_worker * mesh.num_cores * mesh.num_subcores == n_win

    @pl.kernel(out_type=jax.ShapeDtypeStruct((TOPK * N_TOK,) + ROW_TILE, U32), mesh=mesh,
               scratch_types=[pltpu.VMEM((SC_WINDOW,), I32), pltpu.VMEM((SC_WINDOW,) + ROW_TILE, U32)])
    def gather_rows(ys_hbm, dest_hbm, yg_hbm, idx_v, rows_v):
        first = _sc_worker(mesh) * per_worker

        @pl.loop(0, per_worker)
        def _(w):
            win = first + w
            pltpu.sync_copy(dest_hbm.at[win], idx_v)
            pltpu.sync_copy(ys_hbm.at[idx_v], rows_v)
            pltpu.sync_copy(rows_v, yg_hbm.at[pl.ds(win * SC_WINDOW, SC_WINDOW)])

    return gather_rows(ys, dest.reshape(n_win, SC_WINDOW)).reshape((TOPK, N_TOK) + ROW_TILE)


def _expert_kernel(first_ref, nblk_ref, nvalid_ref, nused_ref, xs_hbm, wg_ref, wu_ref, wd_ref, ys_hbm,
                   wg_scr, wu_scr, wd_scr, x_buf, y_buf, in_sem, out_sem):
    e = pl.program_id(0)
    nused = nused_ref[0]

    def in_copy(g, slot):
        return pltpu.make_async_copy(xs_hbm.at[pl.ds(g * BM, BM)], x_buf.at[slot], in_sem.at[slot])

    def out_copy(g, slot):
        return pltpu.make_async_copy(y_buf.at[slot], ys_hbm.at[pl.ds(g * BM, BM)], out_sem.at[slot])

    @pl.when(e == 0)
    def _first():
        in_copy(0, 0).start()

    wg_scr[...] = wg_ref[0].astype(BF16)
    wu_scr[...] = wu_ref[0].astype(BF16)
    wd_scr[...] = wd_ref[0].astype(BF16)

    def block(b, carry):
        g = first_ref[e] + b
        slot = lax.rem(g, 2)
        in_copy(g, slot).wait()

        @pl.when(g + 1 < nused)
        def _prefetch():
            in_copy(g + 1, 1 - slot).start()

        @pl.when(g >= 2)
        def _slot_free():
            out_copy(g - 2, slot).wait()

        valid = lax.broadcasted_iota(I32, (BM, 1), 0) < nvalid_ref[g]
        lo, hi = _unpack_bf16_pair(jnp.where(valid, x_buf[slot].reshape(BM, HALF), jnp.uint32(0)))
        lo = lo.astype(BF16)
        hi = hi.astype(BF16)
        gate = jnp.dot(lo, wg_scr[0:HALF, :], preferred_element_type=F32) \
            + jnp.dot(hi, wg_scr[HALF:, :], preferred_element_type=F32)
        up = jnp.dot(lo, wu_scr[0:HALF, :], preferred_element_type=F32) \
            + jnp.dot(hi, wu_scr[HALF:, :], preferred_element_type=F32)
        act = (jax.nn.silu(gate) * up).astype(BF16)
        o = jnp.dot(act, wd_scr[...], preferred_element_type=F32)
        y_buf[slot] = _pack_bf16_pair(o[:, 0:HALF], o[:, HALF:]).reshape((BM,) + ROW_TILE)
        out_copy(g, slot).start()
        return carry

    lax.fori_loop(0, nblk_ref[e], block, 0)

    @pl.when(e == N_EXPERTS - 1)
    def _drain():
        out_copy(nused - 2, lax.rem(nused, 2)).wait()
        out_copy(nused - 1, 1 - lax.rem(nused, 2)).wait()


def _experts(first, nblk, nvalid, nused, xs, wg, wu, wd):
    grid_spec = pltpu.PrefetchScalarGridSpec(
        num_scalar_prefetch=4,
        grid=(N_EXPERTS,),
        in_specs=[
            pl.BlockSpec(memory_space=pl.ANY),
            pl.BlockSpec((1, D_MODEL, D_EXPERT), lambda e, *_: (e, 0, 0)),
            pl.BlockSpec((1, D_MODEL, D_EXPERT), lambda e, *_: (e, 0, 0)),
            pl.BlockSpec((1, D_EXPERT, D_MODEL), lambda e, *_: (e, 0, 0)),
        ],
        out_specs=pl.BlockSpec(memory_space=pl.ANY),
        scratch_shapes=[
            pltpu.VMEM((D_MODEL, D_EXPERT), BF16),
            pltpu.VMEM((D_MODEL, D_EXPERT), BF16),
            pltpu.VMEM((D_EXPERT, D_MODEL), BF16),
            pltpu.VMEM((2, BM) + ROW_TILE, U32),
            pltpu.VMEM((2, BM) + ROW_TILE, U32),
            pltpu.SemaphoreType.DMA((2,)),
            pltpu.SemaphoreType.DMA((2,)),
        ],
    )
    return pl.pallas_call(
        _expert_kernel,
        grid_spec=grid_spec,
        out_shape=jax.ShapeDtypeStruct((N_ROWS,) + ROW_TILE, U32),
        compiler_params=pltpu.CompilerParams(
            dimension_semantics=("arbitrary",), vmem_limit_bytes=VMEM_LIMIT),
        name="experts",
    )(first, nblk, nvalid, nused, xs, wg, wu, wd)


def _combine_kernel(x1_ref, rec_ref, yg_ref, p_ref, gple_ref, wpg_ref, wple_ref, gfin_ref, out_ref):
    ple = jnp.dot(p_ref[0].reshape(TM, D_PLE).astype(BF16), wple_ref[...], preferred_element_type=F32)
    rec = rec_ref[...]
    w0 = rec[:, REC_W0:REC_W0 + 1]
    w1 = rec[:, REC_W1:REC_W1 + 1]
    lo0, hi0 = _unpack_bf16_pair(yg_ref[0].reshape(TM, HALF))
    lo1, hi1 = _unpack_bf16_pair(yg_ref[1].reshape(TM, HALF))
    moe = jnp.concatenate([lo0 * w0 + lo1 * w1, hi0 * w0 + hi1 * w1], axis=1)
    x2 = x1_ref[...].reshape(TM, D_MODEL) + moe
    gate = jax.nn.sigmoid(jnp.dot(_rms(x2, gple_ref[...]).astype(BF16), wpg_ref[...],
                                  preferred_element_type=F32))
    x3 = x2 + gate * ple
    out_ref[...] = _rms(x3, gfin_ref[...]).reshape(BATCH, TT, D_MODEL)


def _combine(x1, rec, yg, p, gple, wpg, wple, gfin):
    seq_spec = pl.BlockSpec((BATCH, TT, D_MODEL), lambda i: (0, i, 0))
    return pl.pallas_call(
        _combine_kernel,
        grid=(N_STEP,),
        in_specs=[
            seq_spec,
            pl.BlockSpec((TM, LANES), lambda i: (i, 0)),
            pl.BlockSpec((TOPK, TM) + ROW_TILE, lambda i: (0, i, 0, 0)),
            pl.BlockSpec((1, BATCH, TT, D_PLE), lambda i: (0, 0, i, 0)),
            _const_spec((1, D_MODEL)),
            _const_spec((D_MODEL, D_MODEL)),
            _const_spec((D_PLE, D_MODEL)),
            _const_spec((1, D_MODEL)),
        ],
        out_specs=seq_spec,
        out_shape=jax.ShapeDtypeStruct((BATCH, SEQ, D_MODEL), F32),
        compiler_params=pltpu.CompilerParams(
            dimension_semantics=("arbitrary",), vmem_limit_bytes=VMEM_LIMIT),
        name="combine",
    )(x1, rec, yg, p, gple, wpg, wple, gfin)


def kernel(x, p, g_mix, w_in, b_gate, ssm_a_re, ssm_a_im, ssm_log_dt, ssm_b_re, ssm_b_im, ssm_c_re,
           ssm_c_im, ssm_d, w_glu, conv_dw, conv_dw_b, conv_ln_g, conv_ln_b, w_conv_out, w_out, g_moe,
           w_router_group, b_router_group, w_router_expert, b_router_expert, w_exp_gate, w_exp_up,
           w_exp_down, g_ple, w_ple_gate, w_ple, g_final):
    assert x.shape == (BATCH, SEQ, D_MODEL) and p.shape == (1, BATCH, SEQ, D_PLE)
    row = lambda v: v.reshape(1, -1)

    mp, rmat, a_re, a_im = _ssm_matrices(ssm_a_re[0], ssm_a_im[0], ssm_log_dt[0], ssm_b_re[0],
                                         ssm_b_im[0], ssm_c_re[0], ssm_c_im[0])
    wr1, wr2, br = _router_weights(w_router_group[0], b_router_group[0], w_router_expert[0],
                                   b_router_expert[0])
    perm, permt = _time_major_permutation()
    x1, h2p, rec, rect, cnt = _mixer(
        x, row(g_mix[0]), w_in[0].astype(BF16), row(b_gate[0]), perm, permt, mp, rmat, a_re, a_im,
        row(ssm_d[0]), w_glu[0].astype(BF16), conv_dw[0], row(conv_dw_b[0]), row(conv_ln_g[0]),
        row(conv_ln_b[0]), w_conv_out[0].astype(BF16), w_out[0].astype(BF16), row(g_moe[0]), wr1, wr2, br)

    counts = cnt[0, LANE_EXP0:LANE_EXP0 + N_EXPERTS].astype(I32)
    pcounts = (counts + BM - 1) // BM * BM
    pends = jnp.cumsum(pcounts)
    pstarts = pends - pcounts
    eid = rect[REC_EID0:REC_EID1 + 1].astype(I32)
    rank = rect[REC_RANK0:REC_RANK1 + 1].astype(I32)
    dest = (jnp.sum(jnp.where(eid[..., None] == jnp.arange(N_EXPERTS, dtype=I32), pstarts, 0), axis=-1)
            + rank).reshape(TOPK, N_TOK // SC_WINDOW, SC_WINDOW)
    nused = (pends[-1] // BM).astype(I32)
    blk = jnp.arange(N_BLK, dtype=I32)[:, None] * BM
    in_expert = (pstarts[None, :] <= blk) & (blk < pends[None, :])
    nvalid = jnp.clip(jnp.sum(jnp.where(in_expert, (pstarts + counts)[None, :] - blk, 0), axis=1), 0, BM)

    xs = _dispatch(h2p, dest)
    ys = _experts(pstarts // BM, pcounts // BM, nvalid.astype(I32), nused.reshape(1), xs,
                  w_exp_gate[0], w_exp_up[0], w_exp_down[0])
    yg = _collect(ys, dest)
    return _combine(x1, rec, yg, p, row(g_ple[0]), w_ple_gate[0].astype(BF16), w_ple[0].astype(BF16),
                    row(g_final))
```

```python
import jax
import jax.numpy as jnp
from jax import lax
from jax.experimental import pallas as pl
from jax.experimental.pallas import tpu as pltpu
from jax.experimental.pallas import tpu_sc as plsc

F32 = jnp.float32
BF16 = jnp.bfloat16
U32 = jnp.uint32
I32 = jnp.int32

D_MODEL = 1024
BATCH = 8
SEQ = 2048
N_TOK = BATCH * SEQ
D_SSM = 512
SSM_GROUP_WIDTH = 16
SSM_GROUPS = 32
SSM_STATE = 64
D_CONV = 512
CONV_WIDTH = 31
D_IN = D_SSM + 2 * D_CONV + 2 * D_MODEL
N_GROUPS_MOE = 4
EXPERTS_PER_GROUP = 8
N_EXPERTS = 32
TOPK = 2
D_EXPERT = 512
D_PLE = 256
EPS = 1e-6

SUBLANES = 8
LANES = 128
assert BATCH == SUBLANES

TT = 64
TM = TT * BATCH
N_STEP = SEQ // TT
SB = 256
NSB = TM // SB
BPS = SB // TT
Q = 2
N_SLAB = D_SSM // LANES
GROUPS_PER_SLAB = SSM_GROUPS // N_SLAB
ROWS_Z = TM // Q
STATE_LANES = 2 * GROUPS_PER_SLAB * SSM_STATE
HALO = (CONV_WIDTH - 1) * BATCH
CHUNK_ROWS = SB // (Q * SUBLANES)
CONV_ROWS = 64
N_LC = D_CONV // LANES

LANE_GRP0 = 0
LANE_EXP0 = 32
REC_EID0, REC_EID1, REC_W0, REC_W1, REC_RANK0, REC_RANK1 = 0, 1, 2, 3, 4, 5
REC_ROWS = 8

BM = 256
N_BLK = (TOPK * N_TOK + N_EXPERTS * (BM - 1) + BM - 1) // BM
N_ROWS = N_BLK * BM
HALF = D_MODEL // 2
ROW_TILE = (HALF // LANES, LANES)
SC_WINDOW = 64
IN_AHEAD = 3
IN_SLOTS = IN_AHEAD + 1

VMEM_LIMIT = 56 * 1024 * 1024


def _const_spec(shape):
    n = len(shape)
    return pl.BlockSpec(shape, lambda *_: (0,) * n, pipeline_mode=pl.Buffered(1))


def _rms(x, g):
    ms = jnp.mean(x * x, axis=-1, keepdims=True)
    return x * lax.rsqrt(ms + EPS) * g


def _pack_bf16_pair(lo, hi):
    ulo = lax.bitcast_convert_type(lo.astype(BF16).astype(F32), U32)
    uhi = lax.bitcast_convert_type(hi.astype(BF16).astype(F32), U32)
    return (ulo >> 16) | (uhi & jnp.uint32(0xFFFF0000))


def _unpack_bf16_pair(w):
    lo = lax.bitcast_convert_type(w << 16, F32)
    hi = lax.bitcast_convert_type(w & jnp.uint32(0xFFFF0000), F32)
    return lo, hi


def _mixer_kernel(x_ref, gmix_ref, win_ref, bgate_ref, perm_ref, permt_ref, mp_ref, r_ref, are_ref,
                  aim_ref, d_ref, wglu_ref, dw_ref, dwb_ref, lng_ref, lnb_ref, wco_ref, wout_ref,
                  gmoe_ref, wr1_ref, wr2_ref, br_ref,
                  x1_ref, h2p_ref, rec_ref, rect_ref, cnt_ref,
                  hb_scr, ht_scr, u_scr, y_scr, yi_scr, xs_scr, z_scr, conv_scr, act_scr, actb_scr,
                  rect_scr, s_scr, cnt_scr):
    step = pl.program_id(0)

    @pl.when(step == 0)
    def _init():
        z_scr[:, 0:HALO, :] = jnp.zeros((N_LC, HALO, LANES), F32)
        s_scr[...] = jnp.zeros(s_scr.shape, F32)
        cnt_scr[...] = jnp.zeros(cnt_scr.shape, F32)

    def sub_rows(r):
        return pl.ds(pl.multiple_of(r * SB, SB), SB)

    def phase_a(r, carry):
        xb = x_ref[pl.ds(r * BPS, BPS)].reshape(SB, D_MODEL)
        hb_scr[sub_rows(r), :] = _rms(xb, gmix_ref[...]).astype(BF16)
        return carry

    lax.fori_loop(0, NSB, phase_a, 0)

    ht_scr[...] = jnp.dot(perm_ref[...], hb_scr[...], preferred_element_type=F32).astype(BF16)

    def phase_a3(r, carry):
        h = ht_scr[sub_rows(r), :]
        u = jnp.dot(h, win_ref[:, 0:D_SSM], preferred_element_type=F32)
        u_scr[pl.ds(r * CHUNK_ROWS, CHUNK_ROWS)] = u.reshape(CHUNK_ROWS, Q, SUBLANES, D_SSM)
        v = jnp.dot(h, win_ref[:, D_SSM:D_SSM + 2 * D_CONV], preferred_element_type=F32)
        zc = v[:, 0:D_CONV] * jax.nn.sigmoid(v[:, D_CONV:])
        for lc in range(N_LC):
            z_scr[lc, pl.ds(pl.multiple_of(HALO + r * SB, SUBLANES), SB), :] = zc[:, lc * LANES:(lc + 1) * LANES]
        return carry

    lax.fori_loop(0, NSB, phase_a3, 0)

    for s in range(N_SLAB):
        lanes = slice(s * LANES, (s + 1) * LANES)
        z = jnp.concatenate(
            [u_scr[:, i, :, lanes].reshape(ROWS_Z, LANES) for i in range(Q)], axis=1).astype(BF16)
        xp = jnp.dot(z, mp_ref[s], preferred_element_type=F32)
        yi_scr[s] = xp[:, 0:Q * LANES]
        xs_scr[s] = xp[:, Q * LANES:]

    half = STATE_LANES // 2
    for s in range(N_SLAB):
        a_re = jnp.broadcast_to(are_ref[s:s + 1, :], (SUBLANES, half))
        a_im = jnp.broadcast_to(aim_ref[s:s + 1, :], (SUBLANES, half))

        def scan_body(k, carry, s=s, a_re=a_re, a_im=a_im):
            s_re, s_im = carry
            rows = pl.ds(pl.multiple_of(k * SUBLANES, SUBLANES), SUBLANES)
            x_re = xs_scr[s, rows, 0:half]
            x_im = xs_scr[s, rows, half:]
            xs_scr[s, rows, 0:half] = s_re
            xs_scr[s, rows, half:] = s_im
            n_re = a_re * s_re - a_im * s_im + x_re
            n_im = a_re * s_im + a_im * s_re + x_im
            return n_re, n_im

        s_re, s_im = lax.fori_loop(0, ROWS_Z // SUBLANES, scan_body,
                                   (s_scr[s, :, 0:half], s_scr[s, :, half:]), unroll=4)
        s_scr[s, :, 0:half] = s_re
        s_scr[s, :, half:] = s_im

    for s in range(N_SLAB):
        lanes = slice(s * LANES, (s + 1) * LANES)
        y_tot = yi_scr[s] + jnp.dot(xs_scr[s].astype(BF16), r_ref[s], preferred_element_type=F32)
        for j in range(Q):
            y_scr[:, j, :, lanes] = y_tot[:, j * LANES:(j + 1) * LANES].reshape(
                ROWS_Z // SUBLANES, SUBLANES, LANES)

    def phase_c1(r, carry):
        rows = sub_rows(r)
        crow = pl.ds(r * CHUNK_ROWS, CHUNK_ROWS)
        y = y_scr[crow].reshape(SB, D_SSM) + d_ref[...] * u_scr[crow].reshape(SB, D_SSM)
        act_scr[rows, 0:D_SSM] = jax.nn.gelu(y).astype(BF16)
        for lc in range(N_LC):
            lanes = slice(lc * LANES, (lc + 1) * LANES)

            def conv_piece(rc, c, lc=lc, lanes=lanes):
                r0 = r * SB + rc * CONV_ROWS
                piece = jnp.broadcast_to(dwb_ref[:, lanes], (CONV_ROWS, LANES))
                for j in range(CONV_WIDTH):
                    zrows = pl.ds(pl.multiple_of(r0 + j * BATCH, SUBLANES), CONV_ROWS)
                    piece = piece + dw_ref[j:j + 1, lanes] * z_scr[lc, zrows, :]
                conv_scr[pl.ds(pl.multiple_of(rc * CONV_ROWS, CONV_ROWS), CONV_ROWS), lanes] = piece
                return c

            lax.fori_loop(0, SB // CONV_ROWS, conv_piece, 0)
        acc = conv_scr[...]
        mu = jnp.mean(acc, axis=-1, keepdims=True)
        cen = acc - mu
        var = jnp.mean(cen * cen, axis=-1, keepdims=True)
        ln = cen * lax.rsqrt(var + EPS) * lng_ref[...] + lnb_ref[...]
        act_scr[rows, D_SSM:] = jax.nn.silu(ln).astype(BF16)
        return carry

    lax.fori_loop(0, NSB, phase_c1, 0)
    z_scr[:, 0:HALO, :] = z_scr[:, TM:TM + HALO, :]

    actb_scr[...] = jnp.dot(permt_ref[...], act_scr[...], preferred_element_type=F32).astype(BF16)

    lane = lax.broadcasted_iota(I32, (1, LANES), 1).astype(F32)
    grp_mask = lane < float(N_GROUPS_MOE)
    exp_lane = (lane >= float(LANE_EXP0)) & (lane < float(LANE_EXP0 + N_EXPERTS))
    lane_grp = jnp.floor((lane - float(LANE_EXP0)) * (1.0 / EXPERTS_PER_GROUP))
    tri = (lax.broadcasted_iota(I32, (SB, SB), 0) > lax.broadcasted_iota(I32, (SB, SB), 1)).astype(BF16)
    neg_inf = float("-inf")
    big = float(4 * LANES)

    def phase_c3(r, carry):
        rows = sub_rows(r)
        h = hb_scr[rows, :]
        g0 = D_SSM + 2 * D_CONV
        gate_ssm = jnp.dot(h, win_ref[:, g0:g0 + D_MODEL], preferred_element_type=F32) \
            + bgate_ref[:, 0:D_MODEL]
        gate_conv = jnp.dot(h, win_ref[:, g0 + D_MODEL:], preferred_element_type=F32) \
            + bgate_ref[:, D_MODEL:]
        zz = jnp.dot(actb_scr[rows, 0:D_SSM], wglu_ref[...], preferred_element_type=F32)
        y_ssm = zz[:, 0:D_MODEL] * jax.nn.sigmoid(zz[:, D_MODEL:])
        y_conv = jnp.dot(actb_scr[rows, D_SSM:], wco_ref[...], preferred_element_type=F32)

        merged = jax.nn.sigmoid(gate_ssm) * y_ssm + jax.nn.sigmoid(gate_conv) * y_conv
        xb = x_ref[pl.ds(r * BPS, BPS)].reshape(SB, D_MODEL)
        x1 = xb + jnp.dot(merged.astype(BF16), wout_ref[...], preferred_element_type=F32)
        x1_ref[pl.ds(r * BPS, BPS)] = x1.reshape(BPS, TT, D_MODEL)

        h2 = _rms(x1, gmoe_ref[...])
        h2p_ref[rows] = _pack_bf16_pair(h2[:, 0:HALF], h2[:, HALF:]).reshape((SB,) + ROW_TILE)

        h2_hi = h2.astype(BF16)
        h2_lo = (h2 - h2_hi.astype(F32)).astype(BF16)
        l1 = jnp.dot(h2_hi, wr1_ref[...], preferred_element_type=F32)
        l2 = jnp.dot(h2_lo, wr2_ref[...], preferred_element_type=F32)
        logits = l1[:, 0:LANES] + l1[:, LANES:] + l2 + br_ref[...]

        lg = jnp.where(grp_mask, logits, neg_inf)
        g_max = jnp.max(lg, axis=-1, keepdims=True)
        g_sel = jnp.min(jnp.where(lg == g_max, lane, big), axis=-1, keepdims=True)
        p_g = 1.0 / jnp.sum(jnp.where(grp_mask, jnp.exp(logits - g_max), 0.0), axis=-1, keepdims=True)
        le = jnp.where(exp_lane & (lane_grp == g_sel), logits, neg_inf)
        m1 = jnp.max(le, axis=-1, keepdims=True)
        i1 = jnp.min(jnp.where(le == m1, lane, big), axis=-1, keepdims=True)
        le2 = jnp.where(lane == i1, neg_inf, le)
        m2 = jnp.max(le2, axis=-1, keepdims=True)
        i2 = jnp.min(jnp.where(le2 == m2, lane, big), axis=-1, keepdims=True)
        e2 = jnp.exp(m2 - m1)
        den = 1.0 + e2
        w_a = (1.0 / den) * p_g
        w_b = (e2 / den) * p_g

        sel1 = lane == i1
        sel2 = lane == i2
        onehot = jnp.where(sel1 | sel2, 1.0, 0.0)
        prefix = jnp.dot(tri, onehot.astype(BF16), preferred_element_type=F32) + cnt_scr[...]
        rank_a = jnp.sum(jnp.where(sel1, prefix, 0.0), axis=-1, keepdims=True)
        rank_b = jnp.sum(jnp.where(sel2, prefix, 0.0), axis=-1, keepdims=True)
        cnt_scr[...] = cnt_scr[...] + jnp.sum(onehot, axis=0, keepdims=True)

        rec = jnp.where(lane == float(REC_EID0), i1 - float(LANE_EXP0), 0.0)
        rec = jnp.where(lane == float(REC_EID1), i2 - float(LANE_EXP0), rec)
        rec = jnp.where(lane == float(REC_W0), w_a, rec)
        rec = jnp.where(lane == float(REC_W1), w_b, rec)
        rec = jnp.where(lane == float(REC_RANK0), rank_a, rec)
        rec = jnp.where(lane == float(REC_RANK1), rank_b, rec)
        rec_ref[rows, :] = rec
        rect_scr[r] = jnp.transpose(rec)[0:REC_ROWS, :]
        return carry

    lax.fori_loop(0, NSB, phase_c3, 0)

    for r in range(NSB):
        rect_ref[:, r * SB:(r + 1) * SB] = rect_scr[r]
    cnt_ref[...] = cnt_scr[...]


def _mixer(x, gmix, win, bgate, perm, permt, mp, rmat, a_re, a_im, dvec, wglu, dw, dwb, lng, lnb, wco,
           wout, gmoe, wr1, wr2, br):
    seq_spec = pl.BlockSpec((BATCH, TT, D_MODEL), lambda i: (0, i, 0))
    in_specs = [
        seq_spec,
        _const_spec((1, D_MODEL)),
        _const_spec((D_MODEL, D_IN)),
        _const_spec((1, 2 * D_MODEL)),
        _const_spec((TM, TM)),
        _const_spec((TM, TM)),
        _const_spec(mp.shape),
        _const_spec(rmat.shape),
        _const_spec(a_re.shape),
        _const_spec(a_im.shape),
        _const_spec((1, D_SSM)),
        _const_spec((D_SSM, 2 * D_MODEL)),
        _const_spec((CONV_WIDTH, D_CONV)),
        _const_spec((1, D_CONV)),
        _const_spec((1, D_CONV)),
        _const_spec((1, D_CONV)),
        _const_spec((D_CONV, D_MODEL)),
        _const_spec((D_MODEL, D_MODEL)),
        _const_spec((1, D_MODEL)),
        _const_spec((D_MODEL, 2 * LANES)),
        _const_spec((D_MODEL, LANES)),
        _const_spec((1, LANES)),
    ]
    out_specs = [
        seq_spec,
        pl.BlockSpec((TM,) + ROW_TILE, lambda i: (i, 0, 0)),
        pl.BlockSpec((TM, LANES), lambda i: (i, 0)),
        pl.BlockSpec((REC_ROWS, TM), lambda i: (0, i)),
        pl.BlockSpec((1, LANES), lambda i: (0, 0)),
    ]
    out_shape = [
        jax.ShapeDtypeStruct((BATCH, SEQ, D_MODEL), F32),
        jax.ShapeDtypeStruct((N_TOK,) + ROW_TILE, U32),
        jax.ShapeDtypeStruct((N_TOK, LANES), F32),
        jax.ShapeDtypeStruct((REC_ROWS, N_TOK), F32),
        jax.ShapeDtypeStruct((1, LANES), F32),
    ]
    chunk_shape = (ROWS_Z // SUBLANES, Q, SUBLANES, D_SSM)
    scratch = [
        pltpu.VMEM((TM, D_MODEL), BF16),
        pltpu.VMEM((TM, D_MODEL), BF16),
        pltpu.VMEM(chunk_shape, F32),
        pltpu.VMEM(chunk_shape, F32),
        pltpu.VMEM((N_SLAB, ROWS_Z, Q * LANES), F32),
        pltpu.VMEM((N_SLAB, ROWS_Z, STATE_LANES), F32),
        pltpu.VMEM((N_LC, HALO + TM, LANES), F32),
        pltpu.VMEM((SB, D_CONV), F32),
        pltpu.VMEM((TM, D_SSM + D_CONV), BF16),
        pltpu.VMEM((TM, D_SSM + D_CONV), BF16),
        pltpu.VMEM((NSB, REC_ROWS, SB), F32),
        pltpu.VMEM((N_SLAB, SUBLANES, STATE_LANES), F32),
        pltpu.VMEM((1, LANES), F32),
    ]
    return pl.pallas_call(
        _mixer_kernel,
        grid=(N_STEP,),
        in_specs=in_specs,
        out_specs=out_specs,
        out_shape=out_shape,
        scratch_shapes=scratch,
        compiler_params=pltpu.CompilerParams(
            dimension_semantics=("arbitrary",), vmem_limit_bytes=VMEM_LIMIT),
        name="mixer",
    )(x, gmix, win, bgate, perm, permt, mp, rmat, a_re, a_im, dvec, wglu, dw, dwb, lng, lnb, wco, wout,
      gmoe, wr1, wr2, br)


def _cmul(a, b):
    return a[0] * b[0] - a[1] * b[1], a[0] * b[1] + a[1] * b[0]


def _ssm_matrices(a_re, a_im, log_dt, b_re, b_im, c_re, c_im):
    hp = lax.Precision.HIGHEST
    dt = jnp.exp(log_dt)[:, None]
    mag = jnp.exp(a_re * dt)
    lam = (mag * jnp.cos(a_im * dt), mag * jnp.sin(a_im * dt))
    den = a_re * a_re + a_im * a_im
    nr = lam[0] - 1.0
    ni = lam[1]
    z_re = (nr * a_re + ni * a_im) / den
    z_im = (ni * a_re - nr * a_im) / den
    bbar = (z_re[..., None] * b_re - z_im[..., None] * b_im,
            z_re[..., None] * b_im + z_im[..., None] * b_re)
    pw = [(jnp.ones_like(lam[0]), jnp.zeros_like(lam[0])), lam]
    for _ in range(2, Q + 1):
        pw.append(_cmul(pw[-1], lam))
    e = [(c_re * p[0][:, None, :] - c_im * p[1][:, None, :],
          c_re * p[1][:, None, :] + c_im * p[0][:, None, :]) for p in pw]
    k = [jnp.einsum('gcn,gnd->gcd', e[m][0], bbar[0], precision=hp)
         - jnp.einsum('gcn,gnd->gcd', e[m][1], bbar[1], precision=hp) for m in range(Q)]
    eye = jnp.eye(GROUPS_PER_SLAB, dtype=F32)
    split = lambda t: t.reshape((N_SLAB, GROUPS_PER_SLAB) + t.shape[1:])
    zero_k = jnp.zeros_like(k[0])
    kb = jnp.stack([jnp.stack([split(jnp.swapaxes(k[j - i] if j >= i else zero_k, 1, 2))
                               for j in range(Q)]) for i in range(Q)])
    m_mat = jnp.einsum('ijsgdc,gh->sigdjhc', kb, eye).reshape(N_SLAB, Q * LANES, Q * LANES)
    f = [_cmul((pw[Q - 1 - i][0][..., None], pw[Q - 1 - i][1][..., None]), bbar) for i in range(Q)]
    p_parts = []
    for part in range(2):
        fs = jnp.stack([split(f[i][part]) for i in range(Q)])
        p_parts.append(jnp.einsum('isgnd,gh->sigdhn', fs, eye).reshape(N_SLAB, Q * LANES, STATE_LANES // 2))
    p_mat = jnp.concatenate(p_parts, axis=-1)
    r_parts = []
    for part, sign in ((0, 1.0), (1, -1.0)):
        es = jnp.stack([split(e[j + 1][part]) for j in range(Q)])
        r_parts.append(sign * jnp.einsum('jsgcn,gh->shnjgc', es, eye).reshape(
            N_SLAB, STATE_LANES // 2, Q * LANES))
    r_mat = jnp.concatenate(r_parts, axis=1)
    mp = jnp.concatenate([m_mat, p_mat], axis=-1).astype(BF16)
    a_q = pw[Q]
    return (mp, r_mat.astype(BF16),
            a_q[0].reshape(N_SLAB, STATE_LANES // 2), a_q[1].reshape(N_SLAB, STATE_LANES // 2))


def _router_weights(w_rg, b_rg, w_re, b_re):
    pad_g = LANE_EXP0 - LANE_GRP0 - N_GROUPS_MOE
    pad_e = LANES - LANE_EXP0 - N_EXPERTS
    w = jnp.concatenate([w_rg, jnp.zeros((D_MODEL, pad_g), F32), w_re, jnp.zeros((D_MODEL, pad_e), F32)], axis=1)
    b = jnp.concatenate([b_rg, jnp.zeros((pad_g,), F32), b_re, jnp.zeros((pad_e,), F32)]).reshape(1, LANES)
    w_hi = w.astype(BF16)
    w_lo = (w - w_hi.astype(F32)).astype(BF16)
    return jnp.concatenate([w_hi, w_lo], axis=1), w_hi, b


def _time_major_permutation():
    tm = jnp.arange(TM, dtype=I32)
    src = (tm % BATCH) * TT + tm // BATCH
    perm = (src[:, None] == jnp.arange(TM, dtype=I32)[None, :]).astype(BF16)
    return perm, perm.T


def _sc_mesh():
    return plsc.VectorSubcoreMesh(core_axis_name="core", subcore_axis_name="subcore")


def _sc_worker(mesh):
    return lax.axis_index("core") * mesh.num_subcores + lax.axis_index("subcore")


def _dispatch(h2p, dest):
    mesh = _sc_mesh()
    n_win = N_TOK // SC_WINDOW
    per_worker = n_win // (mesh.num_cores * mesh.num_subcores)
    assert per_worker * mesh.num_cores * mesh.num_subcores == n_win

    @pl.kernel(out_type=jax.ShapeDtypeStruct((N_ROWS,) + ROW_TILE, U32), mesh=mesh,
               scratch_types=[pltpu.VMEM((SC_WINDOW,), I32), pltpu.VMEM((SC_WINDOW,) + ROW_TILE, U32)])
    def scatter_rows(h_hbm, dest_hbm, xs_hbm, idx_v, rows_v):
        first = _sc_worker(mesh) * per_worker

        @pl.loop(0, per_worker)
        def _(w):
            win = first + w
            pltpu.sync_copy(h_hbm.at[pl.ds(win * SC_WINDOW, SC_WINDOW)], rows_v)
            for j in range(TOPK):
                pltpu.sync_copy(dest_hbm.at[j, win], idx_v)
                pltpu.sync_copy(rows_v, xs_hbm.at[idx_v])

    return scatter_rows(h2p, dest)


def _collect(ys, dest):
    mesh = _sc_mesh()
    n_win = TOPK * N_TOK // SC_WINDOW
    per_worker = n_win // (mesh.num_cores * mesh.num_subcores)
    assert per_worker * mesh.num_cores * mesh.num_subcores == n_win

    @pl.kernel(out_type=jax.ShapeDtypeStruct((TOPK * N_TOK,) + ROW_TILE, U32), mesh=mesh,
               scratch_types=[pltpu.VMEM((SC_WINDOW,), I32), pltpu.VMEM((SC_WINDOW,) + ROW_TILE, U32)])
    def gather_rows(ys_hbm, dest_hbm, yg_hbm, idx_v, rows_v):
        first = _sc_worker(mesh) * per_worker

        @pl.loop(0, per_worker)
        def _(w):
            win = first + w
            pltpu.sync_copy(dest_hbm.at[win], idx_v)
            pltpu.sync_copy(ys_hbm.at[idx_v], rows_v)
            pltpu.sync_copy(rows_v, yg_hbm.at[pl.ds(win * SC_WINDOW, SC_WINDOW)])

    return gather_rows(ys, dest.reshape(n_win, SC_WINDOW)).reshape((TOPK, N_TOK) + ROW_TILE)


def _expert_kernel(first_ref, nblk_ref, nvalid_ref, nused_ref, xs_hbm, wg_ref, wu_ref, wd_ref, ys_hbm,
                   wg_scr, wu_scr, wd_scr, x_buf, y_buf, in_sem, out_sem):
    e = pl.program_id(0)
    nused = nused_ref[0]

    def in_copy(g):
        slot = lax.rem(g, IN_SLOTS)
        return pltpu.make_async_copy(xs_hbm.at[pl.ds(g * BM, BM)], x_buf.at[slot], in_sem.at[slot])

    def out_copy(g, slot):
        return pltpu.make_async_copy(y_buf.at[slot], ys_hbm.at[pl.ds(g * BM, BM)], out_sem.at[slot])

    @pl.when(e == 0)
    def _first():
        for g in range(IN_AHEAD):
            in_copy(g).start()

    wg_scr[...] = wg_ref[0].astype(BF16)
    wu_scr[...] = wu_ref[0].astype(BF16)
    wd_scr[...] = wd_ref[0].astype(BF16)

    def block(b, carry):
        g = first_ref[e] + b
        slot = lax.rem(g, 2)
        in_copy(g).wait()

        @pl.when(g + IN_AHEAD < nused)
        def _prefetch():
            in_copy(g + IN_AHEAD).start()

        @pl.when(g >= 2)
        def _slot_free():
            out_copy(g - 2, slot).wait()

        valid = lax.broadcasted_iota(I32, (BM, 1), 0) < nvalid_ref[g]
        x_blk = x_buf[lax.rem(g, IN_SLOTS)].reshape(BM, HALF)
        lo, hi = _unpack_bf16_pair(jnp.where(valid, x_blk, jnp.uint32(0)))
        lo = lo.astype(BF16)
        hi = hi.astype(BF16)
        gate = jnp.dot(lo, wg_scr[0:HALF, :], preferred_element_type=F32) \
            + jnp.dot(hi, wg_scr[HALF:, :], preferred_element_type=F32)
        up = jnp.dot(lo, wu_scr[0:HALF, :], preferred_element_type=F32) \
            + jnp.dot(hi, wu_scr[HALF:, :], preferred_element_type=F32)
        act = (jax.nn.silu(gate) * up).astype(BF16)
        o = jnp.dot(act, wd_scr[...], preferred_element_type=F32)
        y_buf[slot] = _pack_bf16_pair(o[:, 0:HALF], o[:, HALF:]).reshape((BM,) + ROW_TILE)
        out_copy(g, slot).start()
        return carry

    lax.fori_loop(0, nblk_ref[e], block, 0)

    @pl.when(e == N_EXPERTS - 1)
    def _drain():
        out_copy(nused - 2, lax.rem(nused, 2)).wait()
        out_copy(nused - 1, 1 - lax.rem(nused, 2)).wait()


def _experts(first, nblk, nvalid, nused, xs, wg, wu, wd):
    grid_spec = pltpu.PrefetchScalarGridSpec(
        num_scalar_prefetch=4,
        grid=(N_EXPERTS,),
        in_specs=[
            pl.BlockSpec(memory_space=pl.ANY),
            pl.BlockSpec((1, D_MODEL, D_EXPERT), lambda e, *_: (e, 0, 0)),
            pl.BlockSpec((1, D_MODEL, D_EXPERT), lambda e, *_: (e, 0, 0)),
            pl.BlockSpec((1, D_EXPERT, D_MODEL), lambda e, *_: (e, 0, 0)),
        ],
        out_specs=pl.BlockSpec(memory_space=pl.ANY),
        scratch_shapes=[
            pltpu.VMEM((D_MODEL, D_EXPERT), BF16),
            pltpu.VMEM((D_MODEL, D_EXPERT), BF16),
            pltpu.VMEM((D_EXPERT, D_MODEL), BF16),
            pltpu.VMEM((IN_SLOTS, BM) + ROW_TILE, U32),
            pltpu.VMEM((2, BM) + ROW_TILE, U32),
            pltpu.SemaphoreType.DMA((IN_SLOTS,)),
            pltpu.SemaphoreType.DMA((2,)),
        ],
    )
    return pl.pallas_call(
        _expert_kernel,
        grid_spec=grid_spec,
        out_shape=jax.ShapeDtypeStruct((N_ROWS,) + ROW_TILE, U32),
        compiler_params=pltpu.CompilerParams(
            dimension_semantics=("arbitrary",), vmem_limit_bytes=VMEM_LIMIT),
        name="experts",
    )(first, nblk, nvalid, nused, xs, wg, wu, wd)


def _combine_kernel(x1_ref, rec_ref, yg_ref, p_ref, gple_ref, wpg_ref, wple_ref, gfin_ref, out_ref):
    ple = jnp.dot(p_ref[0].reshape(TM, D_PLE).astype(BF16), wple_ref[...], preferred_element_type=F32)
    rec = rec_ref[...]
    w0 = rec[:, REC_W0:REC_W0 + 1]
    w1 = rec[:, REC_W1:REC_W1 + 1]
    lo0, hi0 = _unpack_bf16_pair(yg_ref[0].reshape(TM, HALF))
    lo1, hi1 = _unpack_bf16_pair(yg_ref[1].reshape(TM, HALF))
    moe = jnp.concatenate([lo0 * w0 + lo1 * w1, hi0 * w0 + hi1 * w1], axis=1)
    x2 = x1_ref[...].reshape(TM, D_MODEL) + moe
    gate = jax.nn.sigmoid(jnp.dot(_rms(x2, gple_ref[...]).astype(BF16), wpg_ref[...],
                                  preferred_element_type=F32))
    x3 = x2 + gate * ple
    out_ref[...] = _rms(x3, gfin_ref[...]).reshape(BATCH, TT, D_MODEL)


def _combine(x1, rec, yg, p, gple, wpg, wple, gfin):
    seq_spec = pl.BlockSpec((BATCH, TT, D_MODEL), lambda i: (0, i, 0))
    return pl.pallas_call(
        _combine_kernel,
        grid=(N_STEP,),
        in_specs=[
            seq_spec,
            pl.BlockSpec((TM, LANES), lambda i: (i, 0)),
            pl.BlockSpec((TOPK, TM) + ROW_TILE, lambda i: (0, i, 0, 0)),
            pl.BlockSpec((1, BATCH, TT, D_PLE), lambda i: (0, 0, i, 0)),
            _const_spec((1, D_MODEL)),
            _const_spec((D_MODEL, D_MODEL)),
            _const_spec((D_PLE, D_MODEL)),
            _const_spec((1, D_MODEL)),
        ],
        out_specs=seq_spec,
        out_shape=jax.ShapeDtypeStruct((BATCH, SEQ, D_MODEL), F32),
        compiler_params=pltpu.CompilerParams(
            dimension_semantics=("arbitrary",), vmem_limit_bytes=VMEM_LIMIT),
        name="combine",
    )(x1, rec, yg, p, gple, wpg, wple, gfin)


def kernel(x, p, g_mix, w_in, b_gate, ssm_a_re, ssm_a_im, ssm_log_dt, ssm_b_re, ssm_b_im, ssm_c_re,
           ssm_c_im, ssm_d, w_glu, conv_dw, conv_dw_b, conv_ln_g, conv_ln_b, w_conv_out, w_out, g_moe,
           w_router_group, b_router_group, w_router_expert, b_router_expert, w_exp_gate, w_exp_up,
           w_exp_down, g_ple, w_ple_gate, w_ple, g_final):
    assert x.shape == (BATCH, SEQ, D_MODEL) and p.shape == (1, BATCH, SEQ, D_PLE)
    row = lambda v: v.reshape(1, -1)

    mp, rmat, a_re, a_im = _ssm_matrices(ssm_a_re[0], ssm_a_im[0], ssm_log_dt[0], ssm_b_re[0],
                                         ssm_b_im[0], ssm_c_re[0], ssm_c_im[0])
    wr1, wr2, br = _router_weights(w_router_group[0], b_router_group[0], w_router_expert[0],
                                   b_router_expert[0])
    perm, permt = _time_major_permutation()
    x1, h2p, rec, rect, cnt = _mixer(
        x, row(g_mix[0]), w_in[0].astype(BF16), row(b_gate[0]), perm, permt, mp, rmat, a_re, a_im,
        row(ssm_d[0]), w_glu[0].astype(BF16), conv_dw[0], row(conv_dw_b[0]), row(conv_ln_g[0]),
        row(conv_ln_b[0]), w_conv_out[0].astype(BF16), w_out[0].astype(BF16), row(g_moe[0]), wr1, wr2, br)

    counts = cnt[0, LANE_EXP0:LANE_EXP0 + N_EXPERTS].astype(I32)
    pcounts = (counts + BM - 1) // BM * BM
    pends = jnp.cumsum(pcounts)
    pstarts = pends - pcounts
    eid = rect[REC_EID0:REC_EID1 + 1].astype(I32)
    rank = rect[REC_RANK0:REC_RANK1 + 1].astype(I32)
    dest = (jnp.sum(jnp.where(eid[..., None] == jnp.arange(N_EXPERTS, dtype=I32), pstarts, 0), axis=-1)
            + rank).reshape(TOPK, N_TOK // SC_WINDOW, SC_WINDOW)
    nused = (pends[-1] // BM).astype(I32)
    blk = jnp.arange(N_BLK, dtype=I32)[:, None] * BM
    in_expert = (pstarts[None, :] <= blk) & (blk < pends[None, :])
    nvalid = jnp.clip(jnp.sum(jnp.where(in_expert, (pstarts + counts)[None, :] - blk, 0), axis=1), 0, BM)

    xs = _dispatch(h2p, dest)
    ys = _experts(pstarts // BM, pcounts // BM, nvalid.astype(I32), nused.reshape(1), xs,
                  w_exp_gate[0], w_exp_up[0], w_exp_down[0])
    yg = _collect(ys, dest)
    return _combine(x1, rec, yg, p, row(g_ple[0]), w_ple_gate[0].astype(BF16), w_ple[0].astype(BF16),
                    row(g_final))
```

```python
import jax
import jax.numpy as jnp
from jax import lax
from jax.experimental import pallas as pl
from jax.experimental.pallas import tpu as pltpu
from jax.experimental.pallas import tpu_sc as plsc

F32 = jnp.float32
BF16 = jnp.bfloat16
U32 = jnp.uint32
I32 = jnp.int32

D_MODEL = 1024
BATCH = 8
SEQ = 2048
N_TOK = BATCH * SEQ
D_SSM = 512
SSM_GROUP_WIDTH = 16
SSM_GROUPS = 32
SSM_STATE = 64
D_CONV = 512
CONV_WIDTH = 31
D_IN = D_SSM + 2 * D_CONV + 2 * D_MODEL
N_GROUPS_MOE = 4
EXPERTS_PER_GROUP = 8
N_EXPERTS = 32
TOPK = 2
D_EXPERT = 512
D_PLE = 256
EPS = 1e-6

SUBLANES = 8
LANES = 128
assert BATCH == SUBLANES

TT = 64
TM = TT * BATCH
N_STEP = SEQ // TT
SB = 512
NSB = TM // SB
BPS = SB // TT
Q = 2
N_SLAB = D_SSM // LANES
GROUPS_PER_SLAB = SSM_GROUPS // N_SLAB
ROWS_Z = TM // Q
STATE_LANES = 2 * GROUPS_PER_SLAB * SSM_STATE
HALO = (CONV_WIDTH - 1) * BATCH
CHUNK_ROWS = SB // (Q * SUBLANES)
CONV_ROWS = 64
N_LC = D_CONV // LANES

LANE_GRP0 = 0
LANE_EXP0 = 32
REC_EID0, REC_EID1, REC_W0, REC_W1, REC_RANK0, REC_RANK1 = 0, 1, 2, 3, 4, 5
REC_ROWS = 8

BM = 256
N_BLK = (TOPK * N_TOK + N_EXPERTS * (BM - 1) + BM - 1) // BM
N_ROWS = N_BLK * BM
HALF = D_MODEL // 2
ROW_TILE = (HALF // LANES, LANES)
SC_WINDOW = 64
IN_AHEAD = 3
IN_SLOTS = IN_AHEAD + 1

VMEM_LIMIT = 56 * 1024 * 1024


def _const_spec(shape):
    n = len(shape)
    return pl.BlockSpec(shape, lambda *_: (0,) * n, pipeline_mode=pl.Buffered(1))


def _rms(x, g):
    ms = jnp.mean(x * x, axis=-1, keepdims=True)
    return x * lax.rsqrt(ms + EPS) * g


def _pack_bf16_pair(lo, hi):
    ulo = lax.bitcast_convert_type(lo.astype(BF16).astype(F32), U32)
    uhi = lax.bitcast_convert_type(hi.astype(BF16).astype(F32), U32)
    return (ulo >> 16) | (uhi & jnp.uint32(0xFFFF0000))


def _unpack_bf16_pair(w):
    lo = lax.bitcast_convert_type(w << 16, F32)
    hi = lax.bitcast_convert_type(w & jnp.uint32(0xFFFF0000), F32)
    return lo, hi


def _mixer_kernel(x_ref, gmix_ref, win_ref, bgate_ref, perm_ref, permt_ref, mp_ref, r_ref, are_ref,
                  aim_ref, d_ref, wglu_ref, dw_ref, dwb_ref, lng_ref, lnb_ref, wco_ref, wout_ref,
                  gmoe_ref, wr1_ref, wr2_ref, br_ref,
                  x1_ref, h2p_ref, rec_ref, rect_ref, cnt_ref,
                  hb_scr, ht_scr, u_scr, y_scr, yi_scr, xs_scr, z_scr, conv_scr, act_scr, actb_scr,
                  rect_scr, s_scr, cnt_scr):
    step = pl.program_id(0)

    @pl.when(step == 0)
    def _init():
        z_scr[:, 0:HALO, :] = jnp.zeros((N_LC, HALO, LANES), F32)
        s_scr[...] = jnp.zeros(s_scr.shape, F32)
        cnt_scr[...] = jnp.zeros(cnt_scr.shape, F32)

    def sub_rows(r):
        return pl.ds(pl.multiple_of(r * SB, SB), SB)

    def phase_a(r, carry):
        xb = x_ref[pl.ds(r * BPS, BPS)].reshape(SB, D_MODEL)
        hb_scr[sub_rows(r), :] = _rms(xb, gmix_ref[...]).astype(BF16)
        return carry

    lax.fori_loop(0, NSB, phase_a, 0)

    ht_scr[...] = jnp.dot(perm_ref[...], hb_scr[...], preferred_element_type=F32).astype(BF16)

    def phase_a3(r, carry):
        h = ht_scr[sub_rows(r), :]
        u = jnp.dot(h, win_ref[:, 0:D_SSM], preferred_element_type=F32)
        u_scr[pl.ds(r * CHUNK_ROWS, CHUNK_ROWS)] = u.reshape(CHUNK_ROWS, Q, SUBLANES, D_SSM)
        v = jnp.dot(h, win_ref[:, D_SSM:D_SSM + 2 * D_CONV], preferred_element_type=F32)
        zc = v[:, 0:D_CONV] * jax.nn.sigmoid(v[:, D_CONV:])
        for lc in range(N_LC):
            z_scr[lc, pl.ds(pl.multiple_of(HALO + r * SB, SUBLANES), SB), :] = zc[:, lc * LANES:(lc + 1) * LANES]
        return carry

    lax.fori_loop(0, NSB, phase_a3, 0)

    for s in range(N_SLAB):
        lanes = slice(s * LANES, (s + 1) * LANES)
        z = jnp.concatenate(
            [u_scr[:, i, :, lanes].reshape(ROWS_Z, LANES) for i in range(Q)], axis=1).astype(BF16)
        xp = jnp.dot(z, mp_ref[s], preferred_element_type=F32)
        yi_scr[s] = xp[:, 0:Q * LANES]
        xs_scr[s] = xp[:, Q * LANES:]

    half = STATE_LANES // 2
    for s in range(N_SLAB):
        a_re = jnp.broadcast_to(are_ref[s:s + 1, :], (SUBLANES, half))
        a_im = jnp.broadcast_to(aim_ref[s:s + 1, :], (SUBLANES, half))

        def scan_body(k, carry, s=s, a_re=a_re, a_im=a_im):
            s_re, s_im = carry
            rows = pl.ds(pl.multiple_of(k * SUBLANES, SUBLANES), SUBLANES)
            x_re = xs_scr[s, rows, 0:half]
            x_im = xs_scr[s, rows, half:]
            xs_scr[s, rows, 0:half] = s_re
            xs_scr[s, rows, half:] = s_im
            n_re = a_re * s_re - a_im * s_im + x_re
            n_im = a_re * s_im + a_im * s_re + x_im
            return n_re, n_im

        s_re, s_im = lax.fori_loop(0, ROWS_Z // SUBLANES, scan_body,
                                   (s_scr[s, :, 0:half], s_scr[s, :, half:]), unroll=4)
        s_scr[s, :, 0:half] = s_re
        s_scr[s, :, half:] = s_im

    for s in range(N_SLAB):
        lanes = slice(s * LANES, (s + 1) * LANES)
        y_tot = yi_scr[s] + jnp.dot(xs_scr[s].astype(BF16), r_ref[s], preferred_element_type=F32)
        for j in range(Q):
            y_scr[:, j, :, lanes] = y_tot[:, j * LANES:(j + 1) * LANES].reshape(
                ROWS_Z // SUBLANES, SUBLANES, LANES)

    def phase_c1(r, carry):
        rows = sub_rows(r)
        crow = pl.ds(r * CHUNK_ROWS, CHUNK_ROWS)
        y = y_scr[crow].reshape(SB, D_SSM) + d_ref[...] * u_scr[crow].reshape(SB, D_SSM)
        act_scr[rows, 0:D_SSM] = jax.nn.gelu(y).astype(BF16)
        for lc in range(N_LC):
            lanes = slice(lc * LANES, (lc + 1) * LANES)

            def conv_piece(rc, c, lc=lc, lanes=lanes):
                r0 = r * SB + rc * CONV_ROWS
                piece = jnp.broadcast_to(dwb_ref[:, lanes], (CONV_ROWS, LANES))
                for j in range(CONV_WIDTH):
                    zrows = pl.ds(pl.multiple_of(r0 + j * BATCH, SUBLANES), CONV_ROWS)
                    piece = piece + dw_ref[j:j + 1, lanes] * z_scr[lc, zrows, :]
                conv_scr[pl.ds(pl.multiple_of(rc * CONV_ROWS, CONV_ROWS), CONV_ROWS), lanes] = piece
                return c

            lax.fori_loop(0, SB // CONV_ROWS, conv_piece, 0)
        acc = conv_scr[...]
        mu = jnp.mean(acc, axis=-1, keepdims=True)
        cen = acc - mu
        var = jnp.mean(cen * cen, axis=-1, keepdims=True)
        ln = cen * lax.rsqrt(var + EPS) * lng_ref[...] + lnb_ref[...]
        act_scr[rows, D_SSM:] = jax.nn.silu(ln).astype(BF16)
        return carry

    lax.fori_loop(0, NSB, phase_c1, 0)
    z_scr[:, 0:HALO, :] = z_scr[:, TM:TM + HALO, :]

    actb_scr[...] = jnp.dot(permt_ref[...], act_scr[...], preferred_element_type=F32).astype(BF16)

    lane = lax.broadcasted_iota(I32, (1, LANES), 1).astype(F32)
    grp_mask = lane < float(N_GROUPS_MOE)
    exp_lane = (lane >= float(LANE_EXP0)) & (lane < float(LANE_EXP0 + N_EXPERTS))
    lane_grp = jnp.floor((lane - float(LANE_EXP0)) * (1.0 / EXPERTS_PER_GROUP))
    tri = (lax.broadcasted_iota(I32, (SB, SB), 0) > lax.broadcasted_iota(I32, (SB, SB), 1)).astype(BF16)
    neg_inf = float("-inf")
    big = float(4 * LANES)

    def phase_c3(r, carry):
        rows = sub_rows(r)
        h = hb_scr[rows, :]
        g0 = D_SSM + 2 * D_CONV
        gate_ssm = jnp.dot(h, win_ref[:, g0:g0 + D_MODEL], preferred_element_type=F32) \
            + bgate_ref[:, 0:D_MODEL]
        gate_conv = jnp.dot(h, win_ref[:, g0 + D_MODEL:], preferred_element_type=F32) \
            + bgate_ref[:, D_MODEL:]
        zz = jnp.dot(actb_scr[rows, 0:D_SSM], wglu_ref[...], preferred_element_type=F32)
        y_ssm = zz[:, 0:D_MODEL] * jax.nn.sigmoid(zz[:, D_MODEL:])
        y_conv = jnp.dot(actb_scr[rows, D_SSM:], wco_ref[...], preferred_element_type=F32)

        merged = jax.nn.sigmoid(gate_ssm) * y_ssm + jax.nn.sigmoid(gate_conv) * y_conv
        xb = x_ref[pl.ds(r * BPS, BPS)].reshape(SB, D_MODEL)
        x1 = xb + jnp.dot(merged.astype(BF16), wout_ref[...], preferred_element_type=F32)
        x1_ref[pl.ds(r * BPS, BPS)] = x1.reshape(BPS, TT, D_MODEL)

        h2 = _rms(x1, gmoe_ref[...])
        h2p_ref[rows] = _pack_bf16_pair(h2[:, 0:HALF], h2[:, HALF:]).reshape((SB,) + ROW_TILE)

        h2_hi = h2.astype(BF16)
        h2_lo = (h2 - h2_hi.astype(F32)).astype(BF16)
        l1 = jnp.dot(h2_hi, wr1_ref[...], preferred_element_type=F32)
        l2 = jnp.dot(h2_lo, wr2_ref[...], preferred_element_type=F32)
        logits = l1[:, 0:LANES] + l1[:, LANES:] + l2 + br_ref[...]

        lg = jnp.where(grp_mask, logits, neg_inf)
        g_max = jnp.max(lg, axis=-1, keepdims=True)
        g_sel = jnp.min(jnp.where(lg == g_max, lane, big), axis=-1, keepdims=True)
        p_g = 1.0 / jnp.sum(jnp.where(grp_mask, jnp.exp(logits - g_max), 0.0), axis=-1, keepdims=True)
        le = jnp.where(exp_lane & (lane_grp == g_sel), logits, neg_inf)
        m1 = jnp.max(le, axis=-1, keepdims=True)
        i1 = jnp.min(jnp.where(le == m1, lane, big), axis=-1, keepdims=True)
        le2 = jnp.where(lane == i1, neg_inf, le)
        m2 = jnp.max(le2, axis=-1, keepdims=True)
        i2 = jnp.min(jnp.where(le2 == m2, lane, big), axis=-1, keepdims=True)
        e2 = jnp.exp(m2 - m1)
        den = 1.0 + e2
        w_a = (1.0 / den) * p_g
        w_b = (e2 / den) * p_g

        sel1 = lane == i1
        sel2 = lane == i2
        onehot = jnp.where(sel1 | sel2, 1.0, 0.0)
        prefix = jnp.dot(tri, onehot.astype(BF16), preferred_element_type=F32) + cnt_scr[...]
        rank_a = jnp.sum(jnp.where(sel1, prefix, 0.0), axis=-1, keepdims=True)
        rank_b = jnp.sum(jnp.where(sel2, prefix, 0.0), axis=-1, keepdims=True)
        cnt_scr[...] = cnt_scr[...] + jnp.sum(onehot, axis=0, keepdims=True)

        rec = jnp.where(lane == float(REC_EID0), i1 - float(LANE_EXP0), 0.0)
        rec = jnp.where(lane == float(REC_EID1), i2 - float(LANE_EXP0), rec)
        rec = jnp.where(lane == float(REC_W0), w_a, rec)
        rec = jnp.where(lane == float(REC_W1), w_b, rec)
        rec = jnp.where(lane == float(REC_RANK0), rank_a, rec)
        rec = jnp.where(lane == float(REC_RANK1), rank_b, rec)
        rec_ref[rows, :] = rec
        rect_scr[r] = jnp.transpose(rec)[0:REC_ROWS, :]
        return carry

    lax.fori_loop(0, NSB, phase_c3, 0)

    for r in range(NSB):
        rect_ref[:, r * SB:(r + 1) * SB] = rect_scr[r]
    cnt_ref[...] = cnt_scr[...]


def _mixer(x, gmix, win, bgate, perm, permt, mp, rmat, a_re, a_im, dvec, wglu, dw, dwb, lng, lnb, wco,
           wout, gmoe, wr1, wr2, br):
    seq_spec = pl.BlockSpec((BATCH, TT, D_MODEL), lambda i: (0, i, 0))
    in_specs = [
        seq_spec,
        _const_spec((1, D_MODEL)),
        _const_spec((D_MODEL, D_IN)),
        _const_spec((1, 2 * D_MODEL)),
        _const_spec((TM, TM)),
        _const_spec((TM, TM)),
        _const_spec(mp.shape),
        _const_spec(rmat.shape),
        _const_spec(a_re.shape),
        _const_spec(a_im.shape),
        _const_spec((1, D_SSM)),
        _const_spec((D_SSM, 2 * D_MODEL)),
        _const_spec((CONV_WIDTH, D_CONV)),
        _const_spec((1, D_CONV)),
        _const_spec((1, D_CONV)),
        _const_spec((1, D_CONV)),
        _const_spec((D_CONV, D_MODEL)),
        _const_spec((D_MODEL, D_MODEL)),
        _const_spec((1, D_MODEL)),
        _const_spec((D_MODEL, 2 * LANES)),
        _const_spec((D_MODEL, LANES)),
        _const_spec((1, LANES)),
    ]
    out_specs = [
        seq_spec,
        pl.BlockSpec((TM,) + ROW_TILE, lambda i: (i, 0, 0)),
        pl.BlockSpec((TM, LANES), lambda i: (i, 0)),
        pl.BlockSpec((REC_ROWS, TM), lambda i: (0, i)),
        pl.BlockSpec((1, LANES), lambda i: (0, 0)),
    ]
    out_shape = [
        jax.ShapeDtypeStruct((BATCH, SEQ, D_MODEL), F32),
        jax.ShapeDtypeStruct((N_TOK,) + ROW_TILE, U32),
        jax.ShapeDtypeStruct((N_TOK, LANES), F32),
        jax.ShapeDtypeStruct((REC_ROWS, N_TOK), F32),
        jax.ShapeDtypeStruct((1, LANES), F32),
    ]
    chunk_shape = (ROWS_Z // SUBLANES, Q, SUBLANES, D_SSM)
    scratch = [
        pltpu.VMEM((TM, D_MODEL), BF16),
        pltpu.VMEM((TM, D_MODEL), BF16),
        pltpu.VMEM(chunk_shape, F32),
        pltpu.VMEM(chunk_shape, F32),
        pltpu.VMEM((N_SLAB, ROWS_Z, Q * LANES), F32),
        pltpu.VMEM((N_SLAB, ROWS_Z, STATE_LANES), F32),
        pltpu.VMEM((N_LC, HALO + TM, LANES), F32),
        pltpu.VMEM((SB, D_CONV), F32),
        pltpu.VMEM((TM, D_SSM + D_CONV), BF16),
        pltpu.VMEM((TM, D_SSM + D_CONV), BF16),
        pltpu.VMEM((NSB, REC_ROWS, SB), F32),
        pltpu.VMEM((N_SLAB, SUBLANES, STATE_LANES), F32),
        pltpu.VMEM((1, LANES), F32),
    ]
    return pl.pallas_call(
        _mixer_kernel,
        grid=(N_STEP,),
        in_specs=in_specs,
        out_specs=out_specs,
        out_shape=out_shape,
        scratch_shapes=scratch,
        compiler_params=pltpu.CompilerParams(
            dimension_semantics=("arbitrary",), vmem_limit_bytes=VMEM_LIMIT),
        name="mixer",
    )(x, gmix, win, bgate, perm, permt, mp, rmat, a_re, a_im, dvec, wglu, dw, dwb, lng, lnb, wco, wout,
      gmoe, wr1, wr2, br)


def _cmul(a, b):
    return a[0] * b[0] - a[1] * b[1], a[0] * b[1] + a[1] * b[0]


def _ssm_matrices(a_re, a_im, log_dt, b_re, b_im, c_re, c_im):
    hp = lax.Precision.HIGHEST
    dt = jnp.exp(log_dt)[:, None]
    mag = jnp.exp(a_re * dt)
    lam = (mag * jnp.cos(a_im * dt), mag * jnp.sin(a_im * dt))
    den = a_re * a_re + a_im * a_im
    nr = lam[0] - 1.0
    ni = lam[1]
    z_re = (nr * a_re + ni * a_im) / den
    z_im = (ni * a_re - nr * a_im) / den
    bbar = (z_re[..., None] * b_re - z_im[..., None] * b_im,
            z_re[..., None] * b_im + z_im[..., None] * b_re)
    pw = [(jnp.ones_like(lam[0]), jnp.zeros_like(lam[0])), lam]
    for _ in range(2, Q + 1):
        pw.append(_cmul(pw[-1], lam))
    e = [(c_re * p[0][:, None, :] - c_im * p[1][:, None, :],
          c_re * p[1][:, None, :] + c_im * p[0][:, None, :]) for p in pw]
    k = [jnp.einsum('gcn,gnd->gcd', e[m][0], bbar[0], precision=hp)
         - jnp.einsum('gcn,gnd->gcd', e[m][1], bbar[1], precision=hp) for m in range(Q)]
    eye = jnp.eye(GROUPS_PER_SLAB, dtype=F32)
    split = lambda t: t.reshape((N_SLAB, GROUPS_PER_SLAB) + t.shape[1:])
    zero_k = jnp.zeros_like(k[0])
    kb = jnp.stack([jnp.stack([split(jnp.swapaxes(k[j - i] if j >= i else zero_k, 1, 2))
                               for j in range(Q)]) for i in range(Q)])
    m_mat = jnp.einsum('ijsgdc,gh->sigdjhc', kb, eye).reshape(N_SLAB, Q * LANES, Q * LANES)
    f = [_cmul((pw[Q - 1 - i][0][..., None], pw[Q - 1 - i][1][..., None]), bbar) for i in range(Q)]
    p_parts = []
    for part in range(2):
        fs = jnp.stack([split(f[i][part]) for i in range(Q)])
        p_parts.append(jnp.einsum('isgnd,gh->sigdhn', fs, eye).reshape(N_SLAB, Q * LANES, STATE_LANES // 2))
    p_mat = jnp.concatenate(p_parts, axis=-1)
    r_parts = []
    for part, sign in ((0, 1.0), (1, -1.0)):
        es = jnp.stack([split(e[j + 1][part]) for j in range(Q)])
        r_parts.append(sign * jnp.einsum('jsgcn,gh->shnjgc', es, eye).reshape(
            N_SLAB, STATE_LANES // 2, Q * LANES))
    r_mat = jnp.concatenate(r_parts, axis=1)
    mp = jnp.concatenate([m_mat, p_mat], axis=-1).astype(BF16)
    a_q = pw[Q]
    return (mp, r_mat.astype(BF16),
            a_q[0].reshape(N_SLAB, STATE_LANES // 2), a_q[1].reshape(N_SLAB, STATE_LANES // 2))


def _router_weights(w_rg, b_rg, w_re, b_re):
    pad_g = LANE_EXP0 - LANE_GRP0 - N_GROUPS_MOE
    pad_e = LANES - LANE_EXP0 - N_EXPERTS
    w = jnp.concatenate([w_rg, jnp.zeros((D_MODEL, pad_g), F32), w_re, jnp.zeros((D_MODEL, pad_e), F32)], axis=1)
    b = jnp.concatenate([b_rg, jnp.zeros((pad_g,), F32), b_re, jnp.zeros((pad_e,), F32)]).reshape(1, LANES)
    w_hi = w.astype(BF16)
    w_lo = (w - w_hi.astype(F32)).astype(BF16)
    return jnp.concatenate([w_hi, w_lo], axis=1), w_hi, b


def _time_major_permutation():
    tm = jnp.arange(TM, dtype=I32)
    src = (tm % BATCH) * TT + tm // BATCH
    perm = (src[:, None] == jnp.arange(TM, dtype=I32)[None, :]).astype(BF16)
    return perm, perm.T


def _sc_mesh():
    return plsc.VectorSubcoreMesh(core_axis_name="core", subcore_axis_name="subcore")


def _sc_worker(mesh):
    return lax.axis_index("core") * mesh.num_subcores + lax.axis_index("subcore")


def _dispatch(h2p, dest):
    mesh = _sc_mesh()
    n_win = N_TOK // SC_WINDOW
    per_worker = n_win // (mesh.num_cores * mesh.num_subcores)
    assert per_worker * mesh.num_cores * mesh.num_subcores == n_win

    @pl.kernel(out_type=jax.ShapeDtypeStruct((N_ROWS,) + ROW_TILE, U32), mesh=mesh,
               scratch_types=[pltpu.VMEM((SC_WINDOW,), I32), pltpu.VMEM((SC_WINDOW,) + ROW_TILE, U32)])
    def scatter_rows(h_hbm, dest_hbm, xs_hbm, idx_v, rows_v):
        first = _sc_worker(mesh) * per_worker

        @pl.loop(0, per_worker)
        def _(w):
            win = first + w
            pltpu.sync_copy(h_hbm.at[pl.ds(win * SC_WINDOW, SC_WINDOW)], rows_v)
            for j in range(TOPK):
                pltpu.sync_copy(dest_hbm.at[j, win], idx_v)
                pltpu.sync_copy(rows_v, xs_hbm.at[idx_v])

    return scatter_rows(h2p, dest)


def _collect(ys, dest):
    mesh = _sc_mesh()
    n_win = TOPK * N_TOK // SC_WINDOW
    per_worker = n_win // (mesh.num_cores * mesh.num_subcores)
    assert per_worker * mesh.num_cores * mesh.num_subcores == n_win

    @pl.kernel(out_type=jax.ShapeDtypeStruct((TOPK * N_TOK,) + ROW_TILE, U32), mesh=mesh,
               scratch_types=[pltpu.VMEM((SC_WINDOW,), I32), pltpu.VMEM((SC_WINDOW,) + ROW_TILE, U32)])
    def gather_rows(ys_hbm, dest_hbm, yg_hbm, idx_v, rows_v):
        first = _sc_worker(mesh) * per_worker

        @pl.loop(0, per_worker)
        def _(w):
            win = first + w
            pltpu.sync_copy(dest_hbm.at[win], idx_v)
            pltpu.sync_copy(ys_hbm.at[idx_v], rows_v)
            pltpu.sync_copy(rows_v, yg_hbm.at[pl.ds(win * SC_WINDOW, SC_WINDOW)])

    return gather_rows(ys, dest.reshape(n_win, SC_WINDOW)).reshape((TOPK, N_TOK) + ROW_TILE)


def _expert_kernel(first_ref, nblk_ref, nvalid_ref, nused_ref, xs_hbm, wg_ref, wu_ref, wd_ref, ys_hbm,
                   wg_scr, wu_scr, wd_scr, x_buf, y_buf, in_sem, out_sem):
    e = pl.program_id(0)
    nused = nused_ref[0]

    def in_copy(g):
        slot = lax.rem(g, IN_SLOTS)
        return pltpu.make_async_copy(xs_hbm.at[pl.ds(g * BM, BM)], x_buf.at[slot], in_sem.at[slot])

    def out_copy(g, slot):
        return pltpu.make_async_copy(y_buf.at[slot], ys_hbm.at[pl.ds(g * BM, BM)], out_sem.at[slot])

    @pl.when(e == 0)
    def _first():
        for g in range(IN_AHEAD):
            in_copy(g).start()

    wg_scr[...] = wg_ref[0].astype(BF16)
    wu_scr[...] = wu_ref[0].astype(BF16)
    wd_scr[...] = wd_ref[0].astype(BF16)

    def block(b, carry):
        g = first_ref[e] + b
        slot = lax.rem(g, 2)
        in_copy(g).wait()

        @pl.when(g + IN_AHEAD < nused)
        def _prefetch():
            in_copy(g + IN_AHEAD).start()

        @pl.when(g >= 2)
        def _slot_free():
            out_copy(g - 2, slot).wait()

        valid = lax.broadcasted_iota(I32, (BM, 1), 0) < nvalid_ref[g]
        x_blk = x_buf[lax.rem(g, IN_SLOTS)].reshape(BM, HALF)
        lo, hi = _unpack_bf16_pair(jnp.where(valid, x_blk, jnp.uint32(0)))
        lo = lo.astype(BF16)
        hi = hi.astype(BF16)
        gate = jnp.dot(lo, wg_scr[0:HALF, :], preferred_element_type=F32) \
            + jnp.dot(hi, wg_scr[HALF:, :], preferred_element_type=F32)
        up = jnp.dot(lo, wu_scr[0:HALF, :], preferred_element_type=F32) \
            + jnp.dot(hi, wu_scr[HALF:, :], preferred_element_type=F32)
        act = (jax.nn.silu(gate) * up).astype(BF16)
        o = jnp.dot(act, wd_scr[...], preferred_element_type=F32)
        y_buf[slot] = _pack_bf16_pair(o[:, 0:HALF], o[:, HALF:]).reshape((BM,) + ROW_TILE)
        out_copy(g, slot).start()
        return carry

    lax.fori_loop(0, nblk_ref[e], block, 0)

    @pl.when(e == N_EXPERTS - 1)
    def _drain():
        out_copy(nused - 2, lax.rem(nused, 2)).wait()
        out_copy(nused - 1, 1 - lax.rem(nused, 2)).wait()


def _experts(first, nblk, nvalid, nused, xs, wg, wu, wd):
    grid_spec = pltpu.PrefetchScalarGridSpec(
        num_scalar_prefetch=4,
        grid=(N_EXPERTS,),
        in_specs=[
            pl.BlockSpec(memory_space=pl.ANY),
            pl.BlockSpec((1, D_MODEL, D_EXPERT), lambda e, *_: (e, 0, 0)),
            pl.BlockSpec((1, D_MODEL, D_EXPERT), lambda e, *_: (e, 0, 0)),
            pl.BlockSpec((1, D_EXPERT, D_MODEL), lambda e, *_: (e, 0, 0)),
        ],
        out_specs=pl.BlockSpec(memory_space=pl.ANY),
        scratch_shapes=[
            pltpu.VMEM((D_MODEL, D_EXPERT), BF16),
            pltpu.VMEM((D_MODEL, D_EXPERT), BF16),
            pltpu.VMEM((D_EXPERT, D_MODEL), BF16),
            pltpu.VMEM((IN_SLOTS, BM) + ROW_TILE, U32),
            pltpu.VMEM((2, BM) + ROW_TILE, U32),
            pltpu.SemaphoreType.DMA((IN_SLOTS,)),
            pltpu.SemaphoreType.DMA((2,)),
        ],
    )
    return pl.pallas_call(
        _expert_kernel,
        grid_spec=grid_spec,
        out_shape=jax.ShapeDtypeStruct((N_ROWS,) + ROW_TILE, U32),
        compiler_params=pltpu.CompilerParams(
            dimension_semantics=("arbitrary",), vmem_limit_bytes=VMEM_LIMIT),
        name="experts",
    )(first, nblk, nvalid, nused, xs, wg, wu, wd)


def _combine_kernel(x1_ref, rec_ref, yg_ref, p_ref, gple_ref, wpg_ref, wple_ref, gfin_ref, out_ref):
    ple = jnp.dot(p_ref[0].reshape(TM, D_PLE).astype(BF16), wple_ref[...], preferred_element_type=F32)
    rec = rec_ref[...]
    w0 = rec[:, REC_W0:REC_W0 + 1]
    w1 = rec[:, REC_W1:REC_W1 + 1]
    lo0, hi0 = _unpack_bf16_pair(yg_ref[0].reshape(TM, HALF))
    lo1, hi1 = _unpack_bf16_pair(yg_ref[1].reshape(TM, HALF))
    moe = jnp.concatenate([lo0 * w0 + lo1 * w1, hi0 * w0 + hi1 * w1], axis=1)
    x2 = x1_ref[...].reshape(TM, D_MODEL) + moe
    gate = jax.nn.sigmoid(jnp.dot(_rms(x2, gple_ref[...]).astype(BF16), wpg_ref[...],
                                  preferred_element_type=F32))
    x3 = x2 + gate * ple
    out_ref[...] = _rms(x3, gfin_ref[...]).reshape(BATCH, TT, D_MODEL)


def _combine(x1, rec, yg, p, gple, wpg, wple, gfin):
    seq_spec = pl.BlockSpec((BATCH, TT, D_MODEL), lambda i: (0, i, 0))
    return pl.pallas_call(
        _combine_kernel,
        grid=(N_STEP,),
        in_specs=[
            seq_spec,
            pl.BlockSpec((TM, LANES), lambda i: (i, 0)),
            pl.BlockSpec((TOPK, TM) + ROW_TILE, lambda i: (0, i, 0, 0)),
            pl.BlockSpec((1, BATCH, TT, D_PLE), lambda i: (0, 0, i, 0)),
            _const_spec((1, D_MODEL)),
            _const_spec((D_MODEL, D_MODEL)),
            _const_spec((D_PLE, D_MODEL)),
            _const_spec((1, D_MODEL)),
        ],
        out_specs=seq_spec,
        out_shape=jax.ShapeDtypeStruct((BATCH, SEQ, D_MODEL), F32),
        compiler_params=pltpu.CompilerParams(
            dimension_semantics=("arbitrary",), vmem_limit_bytes=VMEM_LIMIT),
        name="combine",
    )(x1, rec, yg, p, gple, wpg, wple, gfin)


def kernel(x, p, g_mix, w_in, b_gate, ssm_a_re, ssm_a_im, ssm_log_dt, ssm_b_re, ssm_b_im, ssm_c_re,
           ssm_c_im, ssm_d, w_glu, conv_dw, conv_dw_b, conv_ln_g, conv_ln_b, w_conv_out, w_out, g_moe,
           w_router_group, b_router_group, w_router_expert, b_router_expert, w_exp_gate, w_exp_up,
           w_exp_down, g_ple, w_ple_gate, w_ple, g_final):
    assert x.shape == (BATCH, SEQ, D_MODEL) and p.shape == (1, BATCH, SEQ, D_PLE)
    row = lambda v: v.reshape(1, -1)

    mp, rmat, a_re, a_im = _ssm_matrices(ssm_a_re[0], ssm_a_im[0], ssm_log_dt[0], ssm_b_re[0],
                                         ssm_b_im[0], ssm_c_re[0], ssm_c_im[0])
    wr1, wr2, br = _router_weights(w_router_group[0], b_router_group[0], w_router_expert[0],
                                   b_router_expert[0])
    perm, permt = _time_major_permutation()
    x1, h2p, rec, rect, cnt = _mixer(
        x, row(g_mix[0]), w_in[0].astype(BF16), row(b_gate[0]), perm, permt, mp, rmat, a_re, a_im,
        row(ssm_d[0]), w_glu[0].astype(BF16), conv_dw[0], row(conv_dw_b[0]), row(conv_ln_g[0]),
        row(conv_ln_b[0]), w_conv_out[0].astype(BF16), w_out[0].astype(BF16), row(g_moe[0]), wr1, wr2, br)

    counts = cnt[0, LANE_EXP0:LANE_EXP0 + N_EXPERTS].astype(I32)
    pcounts = (counts + BM - 1) // BM * BM
    pends = jnp.cumsum(pcounts)
    pstarts = pends - pcounts
    eid = rect[REC_EID0:REC_EID1 + 1].astype(I32)
    rank = rect[REC_RANK0:REC_RANK1 + 1].astype(I32)
    dest = (jnp.sum(jnp.where(eid[..., None] == jnp.arange(N_EXPERTS, dtype=I32), pstarts, 0), axis=-1)
            + rank).reshape(TOPK, N_TOK // SC_WINDOW, SC_WINDOW)
    nused = (pends[-1] // BM).astype(I32)
    blk = jnp.arange(N_BLK, dtype=I32)[:, None] * BM
    in_expert = (pstarts[None, :] <= blk) & (blk < pends[None, :])
    nvalid = jnp.clip(jnp.sum(jnp.where(in_expert, (pstarts + counts)[None, :] - blk, 0), axis=1), 0, BM)

    xs = _dispatch(h2p, dest)
    ys = _experts(pstarts // BM, pcounts // BM, nvalid.astype(I32), nused.reshape(1), xs,
                  w_exp_gate[0], w_exp_up[0], w_exp_down[0])
    yg = _collect(ys, dest)
    return _combine(x1, rec, yg, p, row(g_ple[0]), w_ple_gate[0].astype(BF16), w_ple[0].astype(BF16),
                    row(g_final))
```

```python
import jax
import jax.numpy as jnp
from jax import lax
from jax.experimental import pallas as pl
from jax.experimental.pallas import tpu as pltpu
from jax.experimental.pallas import tpu_sc as plsc

F32 = jnp.float32
BF16 = jnp.bfloat16
U32 = jnp.uint32
I32 = jnp.int32

D_MODEL = 1024
BATCH = 8
SEQ = 2048
N_TOK = BATCH * SEQ
D_SSM = 512
SSM_GROUP_WIDTH = 16
SSM_GROUPS = 32
SSM_STATE = 64
D_CONV = 512
CONV_WIDTH = 31
D_IN = D_SSM + 2 * D_CONV + 2 * D_MODEL
N_GROUPS_MOE = 4
EXPERTS_PER_GROUP = 8
N_EXPERTS = 32
TOPK = 2
D_EXPERT = 512
D_PLE = 256
EPS = 1e-6

SUBLANES = 8
LANES = 128
assert BATCH == SUBLANES

TT = 64
TM = TT * BATCH
N_STEP = SEQ // TT
SB = 512
NSB = TM // SB
BPS = SB // TT
Q = 2
N_SLAB = D_SSM // LANES
GROUPS_PER_SLAB = SSM_GROUPS // N_SLAB
ROWS_Z = TM // Q
STATE_LANES = 2 * GROUPS_PER_SLAB * SSM_STATE
HALO = (CONV_WIDTH - 1) * BATCH
CHUNK_ROWS = SB // (Q * SUBLANES)
CONV_ROWS = 64
N_LC = D_CONV // LANES

LANE_GRP0 = 0
LANE_EXP0 = 32
REC_EID0, REC_EID1, REC_W0, REC_W1, REC_RANK0, REC_RANK1 = 0, 1, 2, 3, 4, 5
REC_ROWS = 8

BM = 256
N_BLK = (TOPK * N_TOK + N_EXPERTS * (BM - 1) + BM - 1) // BM
N_ROWS = N_BLK * BM
HALF = D_MODEL // 2
ROW_TILE = (HALF // LANES, LANES)
SC_WINDOW = 64
IN_AHEAD = 3
IN_SLOTS = IN_AHEAD + 1

VMEM_LIMIT = 56 * 1024 * 1024


def _const_spec(shape):
    n = len(shape)
    return pl.BlockSpec(shape, lambda *_: (0,) * n, pipeline_mode=pl.Buffered(1))


def _rms(x, g):
    ms = jnp.mean(x * x, axis=-1, keepdims=True)
    return x * lax.rsqrt(ms + EPS) * g


def _pack_bf16_pair(lo, hi):
    ulo = lax.bitcast_convert_type(lo.astype(BF16).astype(F32), U32)
    uhi = lax.bitcast_convert_type(hi.astype(BF16).astype(F32), U32)
    return (ulo >> 16) | (uhi & jnp.uint32(0xFFFF0000))


def _unpack_bf16_pair(w):
    lo = lax.bitcast_convert_type(w << 16, F32)
    hi = lax.bitcast_convert_type(w & jnp.uint32(0xFFFF0000), F32)
    return lo, hi


def _mixer_kernel(x_ref, gmix_ref, win_ref, bgate_ref, perm_ref, permt_ref, mp_ref, r_ref, are_ref,
                  aim_ref, d_ref, wglu_ref, dw_ref, dwb_ref, lng_ref, lnb_ref, wco_ref, wout_ref,
                  gmoe_ref, wr1_ref, wr2_ref, br_ref,
                  x1_ref, h2p_ref, rec_ref, rect_ref, cnt_ref,
                  hb_scr, ht_scr, u_scr, y_scr, yi_scr, xs_scr, z_scr, conv_scr, act_scr, actb_scr,
                  rect_scr, s_scr, cnt_scr):
    step = pl.program_id(0)

    @pl.when(step == 0)
    def _init():
        z_scr[:, 0:HALO, :] = jnp.zeros((N_LC, HALO, LANES), F32)
        s_scr[...] = jnp.zeros(s_scr.shape, F32)
        cnt_scr[...] = jnp.zeros(cnt_scr.shape, F32)

    def sub_rows(r):
        return pl.ds(pl.multiple_of(r * SB, SB), SB)

    def phase_a(r, carry):
        xb = x_ref[pl.ds(r * BPS, BPS)].reshape(SB, D_MODEL)
        hb_scr[sub_rows(r), :] = _rms(xb, gmix_ref[...]).astype(BF16)
        return carry

    lax.fori_loop(0, NSB, phase_a, 0)

    ht_scr[...] = jnp.dot(perm_ref[...], hb_scr[...], preferred_element_type=F32).astype(BF16)

    def phase_a3(r, carry):
        h = ht_scr[sub_rows(r), :]
        u = jnp.dot(h, win_ref[:, 0:D_SSM], preferred_element_type=F32)
        u_scr[pl.ds(r * CHUNK_ROWS, CHUNK_ROWS)] = u.reshape(CHUNK_ROWS, Q, SUBLANES, D_SSM)
        v = jnp.dot(h, win_ref[:, D_SSM:D_SSM + 2 * D_CONV], preferred_element_type=F32)
        zc = v[:, 0:D_CONV] * jax.nn.sigmoid(v[:, D_CONV:])
        for lc in range(N_LC):
            z_scr[lc, pl.ds(pl.multiple_of(HALO + r * SB, SUBLANES), SB), :] = zc[:, lc * LANES:(lc + 1) * LANES]
        return carry

    lax.fori_loop(0, NSB, phase_a3, 0)

    for s in range(N_SLAB):
        lanes = slice(s * LANES, (s + 1) * LANES)
        z = jnp.concatenate(
            [u_scr[:, i, :, lanes].reshape(ROWS_Z, LANES) for i in range(Q)], axis=1).astype(BF16)
        xp = jnp.dot(z, mp_ref[s], preferred_element_type=F32)
        yi_scr[s] = xp[:, 0:Q * LANES]
        xs_scr[s] = xp[:, Q * LANES:]

    half = STATE_LANES // 2
    for s in range(N_SLAB):
        a_re = jnp.broadcast_to(are_ref[s:s + 1, :], (SUBLANES, half))
        a_im = jnp.broadcast_to(aim_ref[s:s + 1, :], (SUBLANES, half))

        def scan_body(k, carry, s=s, a_re=a_re, a_im=a_im):
            s_re, s_im = carry
            rows = pl.ds(pl.multiple_of(k * SUBLANES, SUBLANES), SUBLANES)
            x_re = xs_scr[s, rows, 0:half]
            x_im = xs_scr[s, rows, half:]
            xs_scr[s, rows, 0:half] = s_re
            xs_scr[s, rows, half:] = s_im
            n_re = a_re * s_re - a_im * s_im + x_re
            n_im = a_re * s_im + a_im * s_re + x_im
            return n_re, n_im

        s_re, s_im = lax.fori_loop(0, ROWS_Z // SUBLANES, scan_body,
                                   (s_scr[s, :, 0:half], s_scr[s, :, half:]), unroll=True)
        s_scr[s, :, 0:half] = s_re
        s_scr[s, :, half:] = s_im

    for s in range(N_SLAB):
        lanes = slice(s * LANES, (s + 1) * LANES)
        y_tot = yi_scr[s] + jnp.dot(xs_scr[s].astype(BF16), r_ref[s], preferred_element_type=F32)
        for j in range(Q):
            y_scr[:, j, :, lanes] = y_tot[:, j * LANES:(j + 1) * LANES].reshape(
                ROWS_Z // SUBLANES, SUBLANES, LANES)

    def phase_c1(r, carry):
        rows = sub_rows(r)
        crow = pl.ds(r * CHUNK_ROWS, CHUNK_ROWS)
        y = y_scr[crow].reshape(SB, D_SSM) + d_ref[...] * u_scr[crow].reshape(SB, D_SSM)
        act_scr[rows, 0:D_SSM] = jax.nn.gelu(y).astype(BF16)
        for lc in range(N_LC):
            lanes = slice(lc * LANES, (lc + 1) * LANES)

            def conv_piece(rc, c, lc=lc, lanes=lanes):
                r0 = r * SB + rc * CONV_ROWS
                piece = jnp.broadcast_to(dwb_ref[:, lanes], (CONV_ROWS, LANES))
                for j in range(CONV_WIDTH):
                    zrows = pl.ds(pl.multiple_of(r0 + j * BATCH, SUBLANES), CONV_ROWS)
                    piece = piece + dw_ref[j:j + 1, lanes] * z_scr[lc, zrows, :]
                conv_scr[pl.ds(pl.multiple_of(rc * CONV_ROWS, CONV_ROWS), CONV_ROWS), lanes] = piece
                return c

            lax.fori_loop(0, SB // CONV_ROWS, conv_piece, 0)
        acc = conv_scr[...]
        mu = jnp.mean(acc, axis=-1, keepdims=True)
        cen = acc - mu
        var = jnp.mean(cen * cen, axis=-1, keepdims=True)
        ln = cen * lax.rsqrt(var + EPS) * lng_ref[...] + lnb_ref[...]
        act_scr[rows, D_SSM:] = jax.nn.silu(ln).astype(BF16)
        return carry

    lax.fori_loop(0, NSB, phase_c1, 0)
    z_scr[:, 0:HALO, :] = z_scr[:, TM:TM + HALO, :]

    actb_scr[...] = jnp.dot(permt_ref[...], act_scr[...], preferred_element_type=F32).astype(BF16)

    lane = lax.broadcasted_iota(I32, (1, LANES), 1).astype(F32)
    grp_mask = lane < float(N_GROUPS_MOE)
    exp_lane = (lane >= float(LANE_EXP0)) & (lane < float(LANE_EXP0 + N_EXPERTS))
    lane_grp = jnp.floor((lane - float(LANE_EXP0)) * (1.0 / EXPERTS_PER_GROUP))
    tri = (lax.broadcasted_iota(I32, (SB, SB), 0) > lax.broadcasted_iota(I32, (SB, SB), 1)).astype(BF16)
    neg_inf = float("-inf")
    big = float(4 * LANES)

    def phase_c3(r, carry):
        rows = sub_rows(r)
        h = hb_scr[rows, :]
        g0 = D_SSM + 2 * D_CONV
        gate_ssm = jnp.dot(h, win_ref[:, g0:g0 + D_MODEL], preferred_element_type=F32) \
            + bgate_ref[:, 0:D_MODEL]
        gate_conv = jnp.dot(h, win_ref[:, g0 + D_MODEL:], preferred_element_type=F32) \
            + bgate_ref[:, D_MODEL:]
        zz = jnp.dot(actb_scr[rows, 0:D_SSM], wglu_ref[...], preferred_element_type=F32)
        y_ssm = zz[:, 0:D_MODEL] * jax.nn.sigmoid(zz[:, D_MODEL:])
        y_conv = jnp.dot(actb_scr[rows, D_SSM:], wco_ref[...], preferred_element_type=F32)

        merged = jax.nn.sigmoid(gate_ssm) * y_ssm + jax.nn.sigmoid(gate_conv) * y_conv
        xb = x_ref[pl.ds(r * BPS, BPS)].reshape(SB, D_MODEL)
        x1 = xb + jnp.dot(merged.astype(BF16), wout_ref[...], preferred_element_type=F32)
        x1_ref[pl.ds(r * BPS, BPS)] = x1.reshape(BPS, TT, D_MODEL)

        h2 = _rms(x1, gmoe_ref[...])
        h2p_ref[rows] = _pack_bf16_pair(h2[:, 0:HALF], h2[:, HALF:]).reshape((SB,) + ROW_TILE)

        h2_hi = h2.astype(BF16)
        h2_lo = (h2 - h2_hi.astype(F32)).astype(BF16)
        l1 = jnp.dot(h2_hi, wr1_ref[...], preferred_element_type=F32)
        l2 = jnp.dot(h2_lo, wr2_ref[...], preferred_element_type=F32)
        logits = l1[:, 0:LANES] + l1[:, LANES:] + l2 + br_ref[...]

        lg = jnp.where(grp_mask, logits, neg_inf)
        g_max = jnp.max(lg, axis=-1, keepdims=True)
        g_sel = jnp.min(jnp.where(lg == g_max, lane, big), axis=-1, keepdims=True)
        p_g = 1.0 / jnp.sum(jnp.where(grp_mask, jnp.exp(logits - g_max), 0.0), axis=-1, keepdims=True)
        le = jnp.where(exp_lane & (lane_grp == g_sel), logits, neg_inf)
        m1 = jnp.max(le, axis=-1, keepdims=True)
        i1 = jnp.min(jnp.where(le == m1, lane, big), axis=-1, keepdims=True)
        le2 = jnp.where(lane == i1, neg_inf, le)
        m2 = jnp.max(le2, axis=-1, keepdims=True)
        i2 = jnp.min(jnp.where(le2 == m2, lane, big), axis=-1, keepdims=True)
        e2 = jnp.exp(m2 - m1)
        den = 1.0 + e2
        w_a = (1.0 / den) * p_g
        w_b = (e2 / den) * p_g

        sel1 = lane == i1
        sel2 = lane == i2
        onehot = jnp.where(sel1 | sel2, 1.0, 0.0)
        prefix = jnp.dot(tri, onehot.astype(BF16), preferred_element_type=F32) + cnt_scr[...]
        rank_a = jnp.sum(jnp.where(sel1, prefix, 0.0), axis=-1, keepdims=True)
        rank_b = jnp.sum(jnp.where(sel2, prefix, 0.0), axis=-1, keepdims=True)
        cnt_scr[...] = cnt_scr[...] + jnp.sum(onehot, axis=0, keepdims=True)

        rec = jnp.where(lane == float(REC_EID0), i1 - float(LANE_EXP0), 0.0)
        rec = jnp.where(lane == float(REC_EID1), i2 - float(LANE_EXP0), rec)
        rec = jnp.where(lane == float(REC_W0), w_a, rec)
        rec = jnp.where(lane == float(REC_W1), w_b, rec)
        rec = jnp.where(lane == float(REC_RANK0), rank_a, rec)
        rec = jnp.where(lane == float(REC_RANK1), rank_b, rec)
        rec_ref[rows, :] = rec
        rect_scr[r] = jnp.transpose(rec)[0:REC_ROWS, :]
        return carry

    lax.fori_loop(0, NSB, phase_c3, 0)

    for r in range(NSB):
        rect_ref[:, r * SB:(r + 1) * SB] = rect_scr[r]
    cnt_ref[...] = cnt_scr[...]


def _mixer(x, gmix, win, bgate, perm, permt, mp, rmat, a_re, a_im, dvec, wglu, dw, dwb, lng, lnb, wco,
           wout, gmoe, wr1, wr2, br):
    seq_spec = pl.BlockSpec((BATCH, TT, D_MODEL), lambda i: (0, i, 0))
    in_specs = [
        seq_spec,
        _const_spec((1, D_MODEL)),
        _const_spec((D_MODEL, D_IN)),
        _const_spec((1, 2 * D_MODEL)),
        _const_spec((TM, TM)),
        _const_spec((TM, TM)),
        _const_spec(mp.shape),
        _const_spec(rmat.shape),
        _const_spec(a_re.shape),
        _const_spec(a_im.shape),
        _const_spec((1, D_SSM)),
        _const_spec((D_SSM, 2 * D_MODEL)),
        _const_spec((CONV_WIDTH, D_CONV)),
        _const_spec((1, D_CONV)),
        _const_spec((1, D_CONV)),
        _const_spec((1, D_CONV)),
        _const_spec((D_CONV, D_MODEL)),
        _const_spec((D_MODEL, D_MODEL)),
        _const_spec((1, D_MODEL)),
        _const_spec((D_MODEL, 2 * LANES)),
        _const_spec((D_MODEL, LANES)),
        _const_spec((1, LANES)),
    ]
    out_specs = [
        seq_spec,
        pl.BlockSpec((TM,) + ROW_TILE, lambda i: (i, 0, 0)),
        pl.BlockSpec((TM, LANES), lambda i: (i, 0)),
        pl.BlockSpec((REC_ROWS, TM), lambda i: (0, i)),
        pl.BlockSpec((1, LANES), lambda i: (0, 0)),
    ]
    out_shape = [
        jax.ShapeDtypeStruct((BATCH, SEQ, D_MODEL), F32),
        jax.ShapeDtypeStruct((N_TOK,) + ROW_TILE, U32),
        jax.ShapeDtypeStruct((N_TOK, LANES), F32),
        jax.ShapeDtypeStruct((REC_ROWS, N_TOK), F32),
        jax.ShapeDtypeStruct((1, LANES), F32),
    ]
    chunk_shape = (ROWS_Z // SUBLANES, Q, SUBLANES, D_SSM)
    scratch = [
        pltpu.VMEM((TM, D_MODEL), BF16),
        pltpu.VMEM((TM, D_MODEL), BF16),
        pltpu.VMEM(chunk_shape, F32),
        pltpu.VMEM(chunk_shape, F32),
        pltpu.VMEM((N_SLAB, ROWS_Z, Q * LANES), F32),
        pltpu.VMEM((N_SLAB, ROWS_Z, STATE_LANES), F32),
        pltpu.VMEM((N_LC, HALO + TM, LANES), F32),
        pltpu.VMEM((SB, D_CONV), F32),
        pltpu.VMEM((TM, D_SSM + D_CONV), BF16),
        pltpu.VMEM((TM, D_SSM + D_CONV), BF16),
        pltpu.VMEM((NSB, REC_ROWS, SB), F32),
        pltpu.VMEM((N_SLAB, SUBLANES, STATE_LANES), F32),
        pltpu.VMEM((1, LANES), F32),
    ]
    return pl.pallas_call(
        _mixer_kernel,
        grid=(N_STEP,),
        in_specs=in_specs,
        out_specs=out_specs,
        out_shape=out_shape,
        scratch_shapes=scratch,
        compiler_params=pltpu.CompilerParams(
            dimension_semantics=("arbitrary",), vmem_limit_bytes=VMEM_LIMIT),
        name="mixer",
    )(x, gmix, win, bgate, perm, permt, mp, rmat, a_re, a_im, dvec, wglu, dw, dwb, lng, lnb, wco, wout,
      gmoe, wr1, wr2, br)


def _cmul(a, b):
    return a[0] * b[0] - a[1] * b[1], a[0] * b[1] + a[1] * b[0]


def _ssm_matrices(a_re, a_im, log_dt, b_re, b_im, c_re, c_im):
    hp = lax.Precision.HIGHEST
    dt = jnp.exp(log_dt)[:, None]
    mag = jnp.exp(a_re * dt)
    lam = (mag * jnp.cos(a_im * dt), mag * jnp.sin(a_im * dt))
    den = a_re * a_re + a_im * a_im
    nr = lam[0] - 1.0
    ni = lam[1]
    z_re = (nr * a_re + ni * a_im) / den
    z_im = (ni * a_re - nr * a_im) / den
    bbar = (z_re[..., None] * b_re - z_im[..., None] * b_im,
            z_re[..., None] * b_im + z_im[..., None] * b_re)
    pw = [(jnp.ones_like(lam[0]), jnp.zeros_like(lam[0])), lam]
    for _ in range(2, Q + 1):
        pw.append(_cmul(pw[-1], lam))
    e = [(c_re * p[0][:, None, :] - c_im * p[1][:, None, :],
          c_re * p[1][:, None, :] + c_im * p[0][:, None, :]) for p in pw]
    k = [jnp.einsum('gcn,gnd->gcd', e[m][0], bbar[0], precision=hp)
         - jnp.einsum('gcn,gnd->gcd', e[m][1], bbar[1], precision=hp) for m in range(Q)]
    eye = jnp.eye(GROUPS_PER_SLAB, dtype=F32)
    split = lambda t: t.reshape((N_SLAB, GROUPS_PER_SLAB) + t.shape[1:])
    zero_k = jnp.zeros_like(k[0])
    kb = jnp.stack([jnp.stack([split(jnp.swapaxes(k[j - i] if j >= i else zero_k, 1, 2))
                               for j in range(Q)]) for i in range(Q)])
    m_mat = jnp.einsum('ijsgdc,gh->sigdjhc', kb, eye).reshape(N_SLAB, Q * LANES, Q * LANES)
    f = [_cmul((pw[Q - 1 - i][0][..., None], pw[Q - 1 - i][1][..., None]), bbar) for i in range(Q)]
    p_parts = []
    for part in range(2):
        fs = jnp.stack([split(f[i][part]) for i in range(Q)])
        p_parts.append(jnp.einsum('isgnd,gh->sigdhn', fs, eye).reshape(N_SLAB, Q * LANES, STATE_LANES // 2))
    p_mat = jnp.concatenate(p_parts, axis=-1)
    r_parts = []
    for part, sign in ((0, 1.0), (1, -1.0)):
        es = jnp.stack([split(e[j + 1][part]) for j in range(Q)])
        r_parts.append(sign * jnp.einsum('jsgcn,gh->shnjgc', es, eye).reshape(
            N_SLAB, STATE_LANES // 2, Q * LANES))
    r_mat = jnp.concatenate(r_parts, axis=1)
    mp = jnp.concatenate([m_mat, p_mat], axis=-1).astype(BF16)
    a_q = pw[Q]
    return (mp, r_mat.astype(BF16),
            a_q[0].reshape(N_SLAB, STATE_LANES // 2), a_q[1].reshape(N_SLAB, STATE_LANES // 2))


def _router_weights(w_rg, b_rg, w_re, b_re):
    pad_g = LANE_EXP0 - LANE_GRP0 - N_GROUPS_MOE
    pad_e = LANES - LANE_EXP0 - N_EXPERTS
    w = jnp.concatenate([w_rg, jnp.zeros((D_MODEL, pad_g), F32), w_re, jnp.zeros((D_MODEL, pad_e), F32)], axis=1)
    b = jnp.concatenate([b_rg, jnp.zeros((pad_g,), F32), b_re, jnp.zeros((pad_e,), F32)]).reshape(1, LANES)
    w_hi = w.astype(BF16)
    w_lo = (w - w_hi.astype(F32)).astype(BF16)
    return jnp.concatenate([w_hi, w_lo], axis=1), w_hi, b


def _time_major_permutation():
    tm = jnp.arange(TM, dtype=I32)
    src = (tm % BATCH) * TT + tm // BATCH
    perm = (src[:, None] == jnp.arange(TM, dtype=I32)[None, :]).astype(BF16)
    return perm, perm.T


def _sc_mesh():
    return plsc.VectorSubcoreMesh(core_axis_name="core", subcore_axis_name="subcore")


def _sc_worker(mesh):
    return lax.axis_index("core") * mesh.num_subcores + lax.axis_index("subcore")


def _dispatch(h2p, dest):
    mesh = _sc_mesh()
    n_win = N_TOK // SC_WINDOW
    per_worker = n_win // (mesh.num_cores * mesh.num_subcores)
    assert per_worker * mesh.num_cores * mesh.num_subcores == n_win

    @pl.kernel(out_type=jax.ShapeDtypeStruct((N_ROWS,) + ROW_TILE, U32), mesh=mesh,
               scratch_types=[pltpu.VMEM((SC_WINDOW,), I32), pltpu.VMEM((SC_WINDOW,) + ROW_TILE, U32)])
    def scatter_rows(h_hbm, dest_hbm, xs_hbm, idx_v, rows_v):
        first = _sc_worker(mesh) * per_worker

        @pl.loop(0, per_worker)
        def _(w):
            win = first + w
            pltpu.sync_copy(h_hbm.at[pl.ds(win * SC_WINDOW, SC_WINDOW)], rows_v)
            for j in range(TOPK):
                pltpu.sync_copy(dest_hbm.at[j, win], idx_v)
                pltpu.sync_copy(rows_v, xs_hbm.at[idx_v])

    return scatter_rows(h2p, dest)


def _collect(ys, dest):
    mesh = _sc_mesh()
    n_win = TOPK * N_TOK // SC_WINDOW
    per_worker = n_win // (mesh.num_cores * mesh.num_subcores)
    assert per_worker * mesh.num_cores * mesh.num_subcores == n_win

    @pl.kernel(out_type=jax.ShapeDtypeStruct((TOPK * N_TOK,) + ROW_TILE, U32), mesh=mesh,
               scratch_types=[pltpu.VMEM((SC_WINDOW,), I32), pltpu.VMEM((SC_WINDOW,) + ROW_TILE, U32)])
    def gather_rows(ys_hbm, dest_hbm, yg_hbm, idx_v, rows_v):
        first = _sc_worker(mesh) * per_worker

        @pl.loop(0, per_worker)
        def _(w):
            win = first + w
            pltpu.sync_copy(dest_hbm.at[win], idx_v)
            pltpu.sync_copy(ys_hbm.at[idx_v], rows_v)
            pltpu.sync_copy(rows_v, yg_hbm.at[pl.ds(win * SC_WINDOW, SC_WINDOW)])

    return gather_rows(ys, dest.reshape(n_win, SC_WINDOW)).reshape((TOPK, N_TOK) + ROW_TILE)


def _expert_kernel(first_ref, nblk_ref, nvalid_ref, nused_ref, xs_hbm, wg_ref, wu_ref, wd_ref, ys_hbm,
                   wg_scr, wu_scr, wd_scr, x_buf, y_buf, in_sem, out_sem):
    e = pl.program_id(0)
    nused = nused_ref[0]

    def in_copy(g):
        slot = lax.rem(g, IN_SLOTS)
        return pltpu.make_async_copy(xs_hbm.at[pl.ds(g * BM, BM)], x_buf.at[slot], in_sem.at[slot])

    def out_copy(g, slot):
        return pltpu.make_async_copy(y_buf.at[slot], ys_hbm.at[pl.ds(g * BM, BM)], out_sem.at[slot])

    @pl.when(e == 0)
    def _first():
        for g in range(IN_AHEAD):
            in_copy(g).start()

    wg_scr[...] = wg_ref[0].astype(BF16)
    wu_scr[...] = wu_ref[0].astype(BF16)
    wd_scr[...] = wd_ref[0].astype(BF16)

    def block(b, carry):
        g = first_ref[e] + b
        slot = lax.rem(g, 2)
        in_copy(g).wait()

        @pl.when(g + IN_AHEAD < nused)
        def _prefetch():
            in_copy(g + IN_AHEAD).start()

        @pl.when(g >= 2)
        def _slot_free():
            out_copy(g - 2, slot).wait()

        valid = lax.broadcasted_iota(I32, (BM, 1), 0) < nvalid_ref[g]
        x_blk = x_buf[lax.rem(g, IN_SLOTS)].reshape(BM, HALF)
        lo, hi = _unpack_bf16_pair(jnp.where(valid, x_blk, jnp.uint32(0)))
        lo = lo.astype(BF16)
        hi = hi.astype(BF16)
        gate = jnp.dot(lo, wg_scr[0:HALF, :], preferred_element_type=F32) \
            + jnp.dot(hi, wg_scr[HALF:, :], preferred_element_type=F32)
        up = jnp.dot(lo, wu_scr[0:HALF, :], preferred_element_type=F32) \
            + jnp.dot(hi, wu_scr[HALF:, :], preferred_element_type=F32)
        act = (jax.nn.silu(gate) * up).astype(BF16)
        o = jnp.dot(act, wd_scr[...], preferred_element_type=F32)
        y_buf[slot] = _pack_bf16_pair(o[:, 0:HALF], o[:, HALF:]).reshape((BM,) + ROW_TILE)
        out_copy(g, slot).start()
        return carry

    lax.fori_loop(0, nblk_ref[e], block, 0)

    @pl.when(e == N_EXPERTS - 1)
    def _drain():
        out_copy(nused - 2, lax.rem(nused, 2)).wait()
        out_copy(nused - 1, 1 - lax.rem(nused, 2)).wait()


def _experts(first, nblk, nvalid, nused, xs, wg, wu, wd):
    grid_spec = pltpu.PrefetchScalarGridSpec(
        num_scalar_prefetch=4,
        grid=(N_EXPERTS,),
        in_specs=[
            pl.BlockSpec(memory_space=pl.ANY),
            pl.BlockSpec((1, D_MODEL, D_EXPERT), lambda e, *_: (e, 0, 0)),
            pl.BlockSpec((1, D_MODEL, D_EXPERT), lambda e, *_: (e, 0, 0)),
            pl.BlockSpec((1, D_EXPERT, D_MODEL), lambda e, *_: (e, 0, 0)),
        ],
        out_specs=pl.BlockSpec(memory_space=pl.ANY),
        scratch_shapes=[
            pltpu.VMEM((D_MODEL, D_EXPERT), BF16),
            pltpu.VMEM((D_MODEL, D_EXPERT), BF16),
            pltpu.VMEM((D_EXPERT, D_MODEL), BF16),
            pltpu.VMEM((IN_SLOTS, BM) + ROW_TILE, U32),
            pltpu.VMEM((2, BM) + ROW_TILE, U32),
            pltpu.SemaphoreType.DMA((IN_SLOTS,)),
            pltpu.SemaphoreType.DMA((2,)),
        ],
    )
    return pl.pallas_call(
        _expert_kernel,
        grid_spec=grid_spec,
        out_shape=jax.ShapeDtypeStruct((N_ROWS,) + ROW_TILE, U32),
        compiler_params=pltpu.CompilerParams(
            dimension_semantics=("arbitrary",), vmem_limit_bytes=VMEM_LIMIT),
        name="experts",
    )(first, nblk, nvalid, nused, xs, wg, wu, wd)


def _combine_kernel(x1_ref, rec_ref, yg_ref, p_ref, gple_ref, wpg_ref, wple_ref, gfin_ref, out_ref):
    ple = jnp.dot(p_ref[0].reshape(TM, D_PLE).astype(BF16), wple_ref[...], preferred_element_type=F32)
    rec = rec_ref[...]
    w0 = rec[:, REC_W0:REC_W0 + 1]
    w1 = rec[:, REC_W1:REC_W1 + 1]
    lo0, hi0 = _unpack_bf16_pair(yg_ref[0].reshape(TM, HALF))
    lo1, hi1 = _unpack_bf16_pair(yg_ref[1].reshape(TM, HALF))
    moe = jnp.concatenate([lo0 * w0 + lo1 * w1, hi0 * w0 + hi1 * w1], axis=1)
    x2 = x1_ref[...].reshape(TM, D_MODEL) + moe
    gate = jax.nn.sigmoid(jnp.dot(_rms(x2, gple_ref[...]).astype(BF16), wpg_ref[...],
                                  preferred_element_type=F32))
    x3 = x2 + gate * ple
    out_ref[...] = _rms(x3, gfin_ref[...]).reshape(BATCH, TT, D_MODEL)


def _combine(x1, rec, yg, p, gple, wpg, wple, gfin):
    seq_spec = pl.BlockSpec((BATCH, TT, D_MODEL), lambda i: (0, i, 0))
    return pl.pallas_call(
        _combine_kernel,
        grid=(N_STEP,),
        in_specs=[
            seq_spec,
            pl.BlockSpec((TM, LANES), lambda i: (i, 0)),
            pl.BlockSpec((TOPK, TM) + ROW_TILE, lambda i: (0, i, 0, 0)),
            pl.BlockSpec((1, BATCH, TT, D_PLE), lambda i: (0, 0, i, 0)),
            _const_spec((1, D_MODEL)),
            _const_spec((D_MODEL, D_MODEL)),
            _const_spec((D_PLE, D_MODEL)),
            _const_spec((1, D_MODEL)),
        ],
        out_specs=seq_spec,
        out_shape=jax.ShapeDtypeStruct((BATCH, SEQ, D_MODEL), F32),
        compiler_params=pltpu.CompilerParams(
            dimension_semantics=("arbitrary",), vmem_limit_bytes=VMEM_LIMIT),
        name="combine",
    )(x1, rec, yg, p, gple, wpg, wple, gfin)


def kernel(x, p, g_mix, w_in, b_gate, ssm_a_re, ssm_a_im, ssm_log_dt, ssm_b_re, ssm_b_im, ssm_c_re,
           ssm_c_im, ssm_d, w_glu, conv_dw, conv_dw_b, conv_ln_g, conv_ln_b, w_conv_out, w_out, g_moe,
           w_router_group, b_router_group, w_router_expert, b_router_expert, w_exp_gate, w_exp_up,
           w_exp_down, g_ple, w_ple_gate, w_ple, g_final):
    assert x.shape == (BATCH, SEQ, D_MODEL) and p.shape == (1, BATCH, SEQ, D_PLE)
    row = lambda v: v.reshape(1, -1)

    mp, rmat, a_re, a_im = _ssm_matrices(ssm_a_re[0], ssm_a_im[0], ssm_log_dt[0], ssm_b_re[0],
                                         ssm_b_im[0], ssm_c_re[0], ssm_c_im[0])
    wr1, wr2, br = _router_weights(w_router_group[0], b_router_group[0], w_router_expert[0],
                                   b_router_expert[0])
    perm, permt = _time_major_permutation()
    x1, h2p, rec, rect, cnt = _mixer(
        x, row(g_mix[0]), w_in[0].astype(BF16), row(b_gate[0]), perm, permt, mp, rmat, a_re, a_im,
        row(ssm_d[0]), w_glu[0].astype(BF16), conv_dw[0], row(conv_dw_b[0]), row(conv_ln_g[0]),
        row(conv_ln_b[0]), w_conv_out[0].astype(BF16), w_out[0].astype(BF16), row(g_moe[0]), wr1, wr2, br)

    counts = cnt[0, LANE_EXP0:LANE_EXP0 + N_EXPERTS].astype(I32)
    pcounts = (counts + BM - 1) // BM * BM
    pends = jnp.cumsum(pcounts)
    pstarts = pends - pcounts
    eid = rect[REC_EID0:REC_EID1 + 1].astype(I32)
    rank = rect[REC_RANK0:REC_RANK1 + 1].astype(I32)
    dest = (jnp.sum(jnp.where(eid[..., None] == jnp.arange(N_EXPERTS, dtype=I32), pstarts, 0), axis=-1)
            + rank).reshape(TOPK, N_TOK // SC_WINDOW, SC_WINDOW)
    nused = (pends[-1] // BM).astype(I32)
    blk = jnp.arange(N_BLK, dtype=I32)[:, None] * BM
    in_expert = (pstarts[None, :] <= blk) & (blk < pends[None, :])
    nvalid = jnp.clip(jnp.sum(jnp.where(in_expert, (pstarts + counts)[None, :] - blk, 0), axis=1), 0, BM)

    xs = _dispatch(h2p, dest)
    ys = _experts(pstarts // BM, pcounts // BM, nvalid.astype(I32), nused.reshape(1), xs,
                  w_exp_gate[0], w_exp_up[0], w_exp_down[0])
    yg = _collect(ys, dest)
    return _combine(x1, rec, yg, p, row(g_ple[0]), w_ple_gate[0].astype(BF16), w_ple[0].astype(BF16),
                    row(g_final))
```

```python
import jax
import jax.numpy as jnp
from jax import lax
from jax.experimental import pallas as pl
from jax.experimental.pallas import tpu as pltpu
from jax.experimental.pallas import tpu_sc as plsc

F32 = jnp.float32
BF16 = jnp.bfloat16
U32 = jnp.uint32
I32 = jnp.int32

D_MODEL = 1024
BATCH = 8
SEQ = 2048
N_TOK = BATCH * SEQ
D_SSM = 512
SSM_GROUP_WIDTH = 16
SSM_GROUPS = 32
SSM_STATE = 64
D_CONV = 512
CONV_WIDTH = 31
D_IN = D_SSM + 2 * D_CONV + 2 * D_MODEL
N_GROUPS_MOE = 4
EXPERTS_PER_GROUP = 8
N_EXPERTS = 32
TOPK = 2
D_EXPERT = 512
D_PLE = 256
EPS = 1e-6

SUBLANES = 8
LANES = 128
assert BATCH == SUBLANES

TT = 64
TM = TT * BATCH
N_STEP = SEQ // TT
SB = 512
NSB = TM // SB
BPS = SB // TT
Q = 2
N_SLAB = D_SSM // LANES
GROUPS_PER_SLAB = SSM_GROUPS // N_SLAB
ROWS_Z = TM // Q
STATE_LANES = 2 * GROUPS_PER_SLAB * SSM_STATE
HALO = (CONV_WIDTH - 1) * BATCH
CHUNK_ROWS = SB // (Q * SUBLANES)
CONV_ROWS = 64
N_LC = D_CONV // LANES

LANE_GRP0 = 0
LANE_EXP0 = 32
REC_EID0, REC_EID1, REC_W0, REC_W1, REC_RANK0, REC_RANK1 = 0, 1, 2, 3, 4, 5
REC_ROWS = 8

BM = 256
N_BLK = (TOPK * N_TOK + N_EXPERTS * (BM - 1) + BM - 1) // BM
N_ROWS = N_BLK * BM
HALF = D_MODEL // 2
ROW_TILE = (HALF // LANES, LANES)
SC_WINDOW = 64
IN_AHEAD = 3
IN_SLOTS = IN_AHEAD + 1
N_PARTS = 2
PART_STEPS = N_STEP // N_PARTS

VMEM_LIMIT = 56 * 1024 * 1024


def _const_spec(shape):
    n = len(shape)
    return pl.BlockSpec(shape, lambda *_: (0,) * n, pipeline_mode=pl.Buffered(1))


def _rms(x, g):
    ms = jnp.mean(x * x, axis=-1, keepdims=True)
    return x * lax.rsqrt(ms + EPS) * g


def _pack_bf16_pair(lo, hi):
    ulo = lax.bitcast_convert_type(lo.astype(BF16).astype(F32), U32)
    uhi = lax.bitcast_convert_type(hi.astype(BF16).astype(F32), U32)
    return (ulo >> 16) | (uhi & jnp.uint32(0xFFFF0000))


def _unpack_bf16_pair(w):
    lo = lax.bitcast_convert_type(w << 16, F32)
    hi = lax.bitcast_convert_type(w & jnp.uint32(0xFFFF0000), F32)
    return lo, hi


def _mixer_kernel(x_ref, gmix_ref, win_ref, bgate_ref, perm_ref, permt_ref, mp_ref, r_ref, are_ref,
                  aim_ref, d_ref, wglu_ref, dw_ref, dwb_ref, lng_ref, lnb_ref, wco_ref, wout_ref,
                  gmoe_ref, wr1_ref, wr2_ref, br_ref,
                  x1_ref, h2p_ref, rec_ref, rect_ref, cnt_ref,
                  hb_scr, ht_scr, u_scr, y_scr, yi_scr, xs_scr, z_scr, conv_scr, act_scr, actb_scr,
                  rect_scr, s_scr, cnt_scr):
    step = pl.program_id(0)

    @pl.when(step == 0)
    def _init():
        z_scr[:, 0:HALO, :] = jnp.zeros((N_LC, HALO, LANES), F32)
        s_scr[...] = jnp.zeros(s_scr.shape, F32)
        cnt_scr[...] = jnp.zeros(cnt_scr.shape, F32)

    def sub_rows(r):
        return pl.ds(pl.multiple_of(r * SB, SB), SB)

    def phase_a(r, carry):
        xb = x_ref[pl.ds(r * BPS, BPS)].reshape(SB, D_MODEL)
        hb_scr[sub_rows(r), :] = _rms(xb, gmix_ref[...]).astype(BF16)
        return carry

    lax.fori_loop(0, NSB, phase_a, 0)

    ht_scr[...] = jnp.dot(perm_ref[...], hb_scr[...], preferred_element_type=F32).astype(BF16)

    def phase_a3(r, carry):
        h = ht_scr[sub_rows(r), :]
        u = jnp.dot(h, win_ref[:, 0:D_SSM], preferred_element_type=F32)
        u_scr[pl.ds(r * CHUNK_ROWS, CHUNK_ROWS)] = u.reshape(CHUNK_ROWS, Q, SUBLANES, D_SSM)
        v = jnp.dot(h, win_ref[:, D_SSM:D_SSM + 2 * D_CONV], preferred_element_type=F32)
        zc = v[:, 0:D_CONV] * jax.nn.sigmoid(v[:, D_CONV:])
        for lc in range(N_LC):
            z_scr[lc, pl.ds(pl.multiple_of(HALO + r * SB, SUBLANES), SB), :] = zc[:, lc * LANES:(lc + 1) * LANES]
        return carry

    lax.fori_loop(0, NSB, phase_a3, 0)

    for s in range(N_SLAB):
        lanes = slice(s * LANES, (s + 1) * LANES)
        z = jnp.concatenate(
            [u_scr[:, i, :, lanes].reshape(ROWS_Z, LANES) for i in range(Q)], axis=1).astype(BF16)
        xp = jnp.dot(z, mp_ref[s], preferred_element_type=F32)
        yi_scr[s] = xp[:, 0:Q * LANES]
        xs_scr[s] = xp[:, Q * LANES:]

    half = STATE_LANES // 2
    for s in range(N_SLAB):
        a_re = jnp.broadcast_to(are_ref[s:s + 1, :], (SUBLANES, half))
        a_im = jnp.broadcast_to(aim_ref[s:s + 1, :], (SUBLANES, half))

        def scan_body(k, carry, s=s, a_re=a_re, a_im=a_im):
            s_re, s_im = carry
            rows = pl.ds(pl.multiple_of(k * SUBLANES, SUBLANES), SUBLANES)
            x_re = xs_scr[s, rows, 0:half]
            x_im = xs_scr[s, rows, half:]
            xs_scr[s, rows, 0:half] = s_re
            xs_scr[s, rows, half:] = s_im
            n_re = a_re * s_re - a_im * s_im + x_re
            n_im = a_re * s_im + a_im * s_re + x_im
            return n_re, n_im

        s_re, s_im = lax.fori_loop(0, ROWS_Z // SUBLANES, scan_body,
                                   (s_scr[s, :, 0:half], s_scr[s, :, half:]), unroll=True)
        s_scr[s, :, 0:half] = s_re
        s_scr[s, :, half:] = s_im

    for s in range(N_SLAB):
        lanes = slice(s * LANES, (s + 1) * LANES)
        y_tot = yi_scr[s] + jnp.dot(xs_scr[s].astype(BF16), r_ref[s], preferred_element_type=F32)
        for j in range(Q):
            y_scr[:, j, :, lanes] = y_tot[:, j * LANES:(j + 1) * LANES].reshape(
                ROWS_Z // SUBLANES, SUBLANES, LANES)

    def phase_c1(r, carry):
        rows = sub_rows(r)
        crow = pl.ds(r * CHUNK_ROWS, CHUNK_ROWS)
        y = y_scr[crow].reshape(SB, D_SSM) + d_ref[...] * u_scr[crow].reshape(SB, D_SSM)
        act_scr[rows, 0:D_SSM] = jax.nn.gelu(y).astype(BF16)
        for lc in range(N_LC):
            lanes = slice(lc * LANES, (lc + 1) * LANES)

            def conv_piece(rc, c, lc=lc, lanes=lanes):
                r0 = r * SB + rc * CONV_ROWS
                piece = jnp.broadcast_to(dwb_ref[:, lanes], (CONV_ROWS, LANES))
                for j in range(CONV_WIDTH):
                    zrows = pl.ds(pl.multiple_of(r0 + j * BATCH, SUBLANES), CONV_ROWS)
                    piece = piece + dw_ref[j:j + 1, lanes] * z_scr[lc, zrows, :]
                conv_scr[pl.ds(pl.multiple_of(rc * CONV_ROWS, CONV_ROWS), CONV_ROWS), lanes] = piece
                return c

            lax.fori_loop(0, SB // CONV_ROWS, conv_piece, 0)
        acc = conv_scr[...]
        mu = jnp.mean(acc, axis=-1, keepdims=True)
        cen = acc - mu
        var = jnp.mean(cen * cen, axis=-1, keepdims=True)
        ln = cen * lax.rsqrt(var + EPS) * lng_ref[...] + lnb_ref[...]
        act_scr[rows, D_SSM:] = jax.nn.silu(ln).astype(BF16)
        return carry

    lax.fori_loop(0, NSB, phase_c1, 0)
    z_scr[:, 0:HALO, :] = z_scr[:, TM:TM + HALO, :]

    actb_scr[...] = jnp.dot(permt_ref[...], act_scr[...], preferred_element_type=F32).astype(BF16)

    lane = lax.broadcasted_iota(I32, (1, LANES), 1).astype(F32)
    grp_mask = lane < float(N_GROUPS_MOE)
    exp_lane = (lane >= float(LANE_EXP0)) & (lane < float(LANE_EXP0 + N_EXPERTS))
    lane_grp = jnp.floor((lane - float(LANE_EXP0)) * (1.0 / EXPERTS_PER_GROUP))
    tri = (lax.broadcasted_iota(I32, (SB, SB), 0) > lax.broadcasted_iota(I32, (SB, SB), 1)).astype(BF16)
    neg_inf = float("-inf")
    big = float(4 * LANES)

    def phase_c3(r, carry):
        rows = sub_rows(r)
        h = hb_scr[rows, :]
        g0 = D_SSM + 2 * D_CONV
        gate_ssm = jnp.dot(h, win_ref[:, g0:g0 + D_MODEL], preferred_element_type=F32) \
            + bgate_ref[:, 0:D_MODEL]
        gate_conv = jnp.dot(h, win_ref[:, g0 + D_MODEL:], preferred_element_type=F32) \
            + bgate_ref[:, D_MODEL:]
        zz = jnp.dot(actb_scr[rows, 0:D_SSM], wglu_ref[...], preferred_element_type=F32)
        y_ssm = zz[:, 0:D_MODEL] * jax.nn.sigmoid(zz[:, D_MODEL:])
        y_conv = jnp.dot(actb_scr[rows, D_SSM:], wco_ref[...], preferred_element_type=F32)

        merged = jax.nn.sigmoid(gate_ssm) * y_ssm + jax.nn.sigmoid(gate_conv) * y_conv
        xb = x_ref[pl.ds(r * BPS, BPS)].reshape(SB, D_MODEL)
        x1 = xb + jnp.dot(merged.astype(BF16), wout_ref[...], preferred_element_type=F32)
        x1_ref[pl.ds(r * BPS, BPS)] = x1.reshape(BPS, TT, D_MODEL)

        h2 = _rms(x1, gmoe_ref[...])
        h2p_ref[rows] = _pack_bf16_pair(h2[:, 0:HALF], h2[:, HALF:]).reshape((SB,) + ROW_TILE)

        h2_hi = h2.astype(BF16)
        h2_lo = (h2 - h2_hi.astype(F32)).astype(BF16)
        l1 = jnp.dot(h2_hi, wr1_ref[...], preferred_element_type=F32)
        l2 = jnp.dot(h2_lo, wr2_ref[...], preferred_element_type=F32)
        logits = l1[:, 0:LANES] + l1[:, LANES:] + l2 + br_ref[...]

        lg = jnp.where(grp_mask, logits, neg_inf)
        g_max = jnp.max(lg, axis=-1, keepdims=True)
        g_sel = jnp.min(jnp.where(lg == g_max, lane, big), axis=-1, keepdims=True)
        p_g = 1.0 / jnp.sum(jnp.where(grp_mask, jnp.exp(logits - g_max), 0.0), axis=-1, keepdims=True)
        le = jnp.where(exp_lane & (lane_grp == g_sel), logits, neg_inf)
        m1 = jnp.max(le, axis=-1, keepdims=True)
        i1 = jnp.min(jnp.where(le == m1, lane, big), axis=-1, keepdims=True)
        le2 = jnp.where(lane == i1, neg_inf, le)
        m2 = jnp.max(le2, axis=-1, keepdims=True)
        i2 = jnp.min(jnp.where(le2 == m2, lane, big), axis=-1, keepdims=True)
        e2 = jnp.exp(m2 - m1)
        den = 1.0 + e2
        w_a = (1.0 / den) * p_g
        w_b = (e2 / den) * p_g

        sel1 = lane == i1
        sel2 = lane == i2
        onehot = jnp.where(sel1 | sel2, 1.0, 0.0)
        prefix = jnp.dot(tri, onehot.astype(BF16), preferred_element_type=F32) + cnt_scr[...]
        rank_a = jnp.sum(jnp.where(sel1, prefix, 0.0), axis=-1, keepdims=True)
        rank_b = jnp.sum(jnp.where(sel2, prefix, 0.0), axis=-1, keepdims=True)
        cnt_scr[...] = cnt_scr[...] + jnp.sum(onehot, axis=0, keepdims=True)

        rec = jnp.where(lane == float(REC_EID0), i1 - float(LANE_EXP0), 0.0)
        rec = jnp.where(lane == float(REC_EID1), i2 - float(LANE_EXP0), rec)
        rec = jnp.where(lane == float(REC_W0), w_a, rec)
        rec = jnp.where(lane == float(REC_W1), w_b, rec)
        rec = jnp.where(lane == float(REC_RANK0), rank_a, rec)
        rec = jnp.where(lane == float(REC_RANK1), rank_b, rec)
        rec_ref[rows, :] = rec
        rect_scr[r] = jnp.transpose(rec)[0:REC_ROWS, :]
        return carry

    lax.fori_loop(0, NSB, phase_c3, 0)

    for r in range(NSB):
        rect_ref[:, r * SB:(r + 1) * SB] = rect_scr[r]
    cnt_ref[...] = cnt_scr[...]


def _mixer(x, gmix, win, bgate, perm, permt, mp, rmat, a_re, a_im, dvec, wglu, dw, dwb, lng, lnb, wco,
           wout, gmoe, wr1, wr2, br):
    seq_spec = pl.BlockSpec((BATCH, TT, D_MODEL), lambda i: (0, i, 0))
    in_specs = [
        seq_spec,
        _const_spec((1, D_MODEL)),
        _const_spec((D_MODEL, D_IN)),
        _const_spec((1, 2 * D_MODEL)),
        _const_spec((TM, TM)),
        _const_spec((TM, TM)),
        _const_spec(mp.shape),
        _const_spec(rmat.shape),
        _const_spec(a_re.shape),
        _const_spec(a_im.shape),
        _const_spec((1, D_SSM)),
        _const_spec((D_SSM, 2 * D_MODEL)),
        _const_spec((CONV_WIDTH, D_CONV)),
        _const_spec((1, D_CONV)),
        _const_spec((1, D_CONV)),
        _const_spec((1, D_CONV)),
        _const_spec((D_CONV, D_MODEL)),
        _const_spec((D_MODEL, D_MODEL)),
        _const_spec((1, D_MODEL)),
        _const_spec((D_MODEL, 2 * LANES)),
        _const_spec((D_MODEL, LANES)),
        _const_spec((1, LANES)),
    ]
    out_specs = [
        seq_spec,
        pl.BlockSpec((TM,) + ROW_TILE, lambda i: (i, 0, 0)),
        pl.BlockSpec((TM, LANES), lambda i: (i, 0)),
        pl.BlockSpec((REC_ROWS, TM), lambda i: (0, i)),
        pl.BlockSpec((1, LANES), lambda i: (0, 0)),
    ]
    out_shape = [
        jax.ShapeDtypeStruct((BATCH, SEQ, D_MODEL), F32),
        jax.ShapeDtypeStruct((N_TOK,) + ROW_TILE, U32),
        jax.ShapeDtypeStruct((N_TOK, LANES), F32),
        jax.ShapeDtypeStruct((REC_ROWS, N_TOK), F32),
        jax.ShapeDtypeStruct((1, LANES), F32),
    ]
    chunk_shape = (ROWS_Z // SUBLANES, Q, SUBLANES, D_SSM)
    scratch = [
        pltpu.VMEM((TM, D_MODEL), BF16),
        pltpu.VMEM((TM, D_MODEL), BF16),
        pltpu.VMEM(chunk_shape, F32),
        pltpu.VMEM(chunk_shape, F32),
        pltpu.VMEM((N_SLAB, ROWS_Z, Q * LANES), F32),
        pltpu.VMEM((N_SLAB, ROWS_Z, STATE_LANES), F32),
        pltpu.VMEM((N_LC, HALO + TM, LANES), F32),
        pltpu.VMEM((SB, D_CONV), F32),
        pltpu.VMEM((TM, D_SSM + D_CONV), BF16),
        pltpu.VMEM((TM, D_SSM + D_CONV), BF16),
        pltpu.VMEM((NSB, REC_ROWS, SB), F32),
        pltpu.VMEM((N_SLAB, SUBLANES, STATE_LANES), F32),
        pltpu.VMEM((1, LANES), F32),
    ]
    return pl.pallas_call(
        _mixer_kernel,
        grid=(N_STEP,),
        in_specs=in_specs,
        out_specs=out_specs,
        out_shape=out_shape,
        scratch_shapes=scratch,
        compiler_params=pltpu.CompilerParams(
            dimension_semantics=("arbitrary",), vmem_limit_bytes=VMEM_LIMIT),
        name="mixer",
    )(x, gmix, win, bgate, perm, permt, mp, rmat, a_re, a_im, dvec, wglu, dw, dwb, lng, lnb, wco, wout,
      gmoe, wr1, wr2, br)


def _cmul(a, b):
    return a[0] * b[0] - a[1] * b[1], a[0] * b[1] + a[1] * b[0]


def _ssm_matrices(a_re, a_im, log_dt, b_re, b_im, c_re, c_im):
    dt = jnp.exp(log_dt)[:, None]
    mag = jnp.exp(a_re * dt)
    lam = (mag * jnp.cos(a_im * dt), mag * jnp.sin(a_im * dt))
    den = a_re * a_re + a_im * a_im
    nr = lam[0] - 1.0
    ni = lam[1]
    z_re = (nr * a_re + ni * a_im) / den
    z_im = (ni * a_re - nr * a_im) / den
    bbar = (z_re[..., None] * b_re - z_im[..., None] * b_im,
            z_re[..., None] * b_im + z_im[..., None] * b_re)
    pw = [(jnp.ones_like(lam[0]), jnp.zeros_like(lam[0])), lam]
    for _ in range(2, Q + 1):
        pw.append(_cmul(pw[-1], lam))
    e = [(c_re * p[0][:, None, :] - c_im * p[1][:, None, :],
          c_re * p[1][:, None, :] + c_im * p[0][:, None, :]) for p in pw]
    k = [jnp.sum(e[m][0][:, :, :, None] * bbar[0][:, None, :, :]
                 - e[m][1][:, :, :, None] * bbar[1][:, None, :, :], axis=2) for m in range(Q)]
    eye = jnp.eye(GROUPS_PER_SLAB, dtype=F32)
    split = lambda t: t.reshape((N_SLAB, GROUPS_PER_SLAB) + t.shape[1:])
    zero_k = jnp.zeros_like(k[0])
    kb = jnp.stack([jnp.stack([split(jnp.swapaxes(k[j - i] if j >= i else zero_k, 1, 2))
                               for j in range(Q)]) for i in range(Q)])
    m_mat = (jnp.transpose(kb, (2, 0, 3, 4, 1, 5))[:, :, :, :, :, None, :]
             * eye[None, None, :, None, None, :, None]).reshape(N_SLAB, Q * LANES, Q * LANES)
    f = [_cmul((pw[Q - 1 - i][0][..., None], pw[Q - 1 - i][1][..., None]), bbar) for i in range(Q)]
    p_parts = []
    for part in range(2):
        fs = jnp.stack([split(f[i][part]) for i in range(Q)])
        p_parts.append((jnp.transpose(fs, (1, 0, 2, 4, 3))[:, :, :, :, None, :]
                        * eye[None, None, :, None, :, None]).reshape(N_SLAB, Q * LANES, STATE_LANES // 2))
    p_mat = jnp.concatenate(p_parts, axis=-1)
    r_parts = []
    for part, sign in ((0, 1.0), (1, -1.0)):
        es = jnp.stack([split(e[j + 1][part]) for j in range(Q)])
        r_parts.append((sign * jnp.transpose(es, (1, 4, 0, 2, 3))[:, None, :, :, :, :]
                        * eye[None, :, None, None, :, None]).reshape(N_SLAB, STATE_LANES // 2, Q * LANES))
    r_mat = jnp.concatenate(r_parts, axis=1)
    mp = jnp.concatenate([m_mat, p_mat], axis=-1).astype(BF16)
    a_q = pw[Q]
    return (mp, r_mat.astype(BF16),
            a_q[0].reshape(N_SLAB, STATE_LANES // 2), a_q[1].reshape(N_SLAB, STATE_LANES // 2))


def _router_weights(w_rg, b_rg, w_re, b_re):
    pad_g = LANE_EXP0 - LANE_GRP0 - N_GROUPS_MOE
    pad_e = LANES - LANE_EXP0 - N_EXPERTS
    w = jnp.concatenate([w_rg, jnp.zeros((D_MODEL, pad_g), F32), w_re, jnp.zeros((D_MODEL, pad_e), F32)], axis=1)
    b = jnp.concatenate([b_rg, jnp.zeros((pad_g,), F32), b_re, jnp.zeros((pad_e,), F32)]).reshape(1, LANES)
    w_hi = w.astype(BF16)
    w_lo = (w - w_hi.astype(F32)).astype(BF16)
    return jnp.concatenate([w_hi, w_lo], axis=1), w_hi, b


def _time_major_permutation():
    tm = jnp.arange(TM, dtype=I32)
    src = (tm % BATCH) * TT + tm // BATCH
    perm = (src[:, None] == jnp.arange(TM, dtype=I32)[None, :]).astype(BF16)
    return perm, perm.T


def _sc_mesh():
    return plsc.VectorSubcoreMesh(core_axis_name="core", subcore_axis_name="subcore")


def _sc_worker(mesh):
    return lax.axis_index("core") * mesh.num_subcores + lax.axis_index("subcore")


def _dispatch(h2p, dest):
    mesh = _sc_mesh()
    n_win = N_TOK // SC_WINDOW
    per_worker = n_win // (mesh.num_cores * mesh.num_subcores)
    assert per_worker * mesh.num_cores * mesh.num_subcores == n_win

    @pl.kernel(out_type=jax.ShapeDtypeStruct((N_ROWS,) + ROW_TILE, U32), mesh=mesh,
               scratch_types=[pltpu.VMEM((SC_WINDOW,), I32), pltpu.VMEM((SC_WINDOW,) + ROW_TILE, U32)])
    def scatter_rows(h_hbm, dest_hbm, xs_hbm, idx_v, rows_v):
        first = _sc_worker(mesh) * per_worker

        @pl.loop(0, per_worker)
        def _(w):
            win = first + w
            pltpu.sync_copy(h_hbm.at[pl.ds(win * SC_WINDOW, SC_WINDOW)], rows_v)
            for j in range(TOPK):
                pltpu.sync_copy(dest_hbm.at[j, win], idx_v)
                pltpu.sync_copy(rows_v, xs_hbm.at[idx_v])

    return scatter_rows(h2p, dest)


def _collect(ys, dest):
    mesh = _sc_mesh()
    n_tok = dest.shape[1]
    n_win = TOPK * n_tok // SC_WINDOW
    per_worker = n_win // (mesh.num_cores * mesh.num_subcores)
    assert per_worker * mesh.num_cores * mesh.num_subcores == n_win

    @pl.kernel(out_type=jax.ShapeDtypeStruct((TOPK * n_tok,) + ROW_TILE, U32), mesh=mesh,
               scratch_types=[pltpu.VMEM((SC_WINDOW,), I32), pltpu.VMEM((SC_WINDOW,) + ROW_TILE, U32)])
    def gather_rows(ys_hbm, dest_hbm, yg_hbm, idx_v, rows_v):
        first = _sc_worker(mesh) * per_worker

        @pl.loop(0, per_worker)
        def _(w):
            win = first + w
            pltpu.sync_copy(dest_hbm.at[win], idx_v)
            pltpu.sync_copy(ys_hbm.at[idx_v], rows_v)
            pltpu.sync_copy(rows_v, yg_hbm.at[pl.ds(win * SC_WINDOW, SC_WINDOW)])

    return gather_rows(ys, dest.reshape(n_win, SC_WINDOW)).reshape((TOPK, n_tok) + ROW_TILE)


def _expert_kernel(first_ref, nblk_ref, nvalid_ref, nused_ref, xs_hbm, wg_ref, wu_ref, wd_ref, ys_hbm,
                   wg_scr, wu_scr, wd_scr, x_buf, y_buf, in_sem, out_sem):
    e = pl.program_id(0)
    nused = nused_ref[0]

    def in_copy(g):
        slot = lax.rem(g, IN_SLOTS)
        return pltpu.make_async_copy(xs_hbm.at[pl.ds(g * BM, BM)], x_buf.at[slot], in_sem.at[slot])

    def out_copy(g, slot):
        return pltpu.make_async_copy(y_buf.at[slot], ys_hbm.at[pl.ds(g * BM, BM)], out_sem.at[slot])

    @pl.when(e == 0)
    def _first():
        for g in range(IN_AHEAD):
            in_copy(g).start()

    wg_scr[...] = wg_ref[0].astype(BF16)
    wu_scr[...] = wu_ref[0].astype(BF16)
    wd_scr[...] = wd_ref[0].astype(BF16)

    def block(b, carry):
        g = first_ref[e] + b
        slot = lax.rem(g, 2)
        in_copy(g).wait()

        @pl.when(g + IN_AHEAD < nused)
        def _prefetch():
            in_copy(g + IN_AHEAD).start()

        @pl.when(g >= 2)
        def _slot_free():
            out_copy(g - 2, slot).wait()

        valid = lax.broadcasted_iota(I32, (BM, 1), 0) < nvalid_ref[g]
        x_blk = x_buf[lax.rem(g, IN_SLOTS)].reshape(BM, HALF)
        lo, hi = _unpack_bf16_pair(jnp.where(valid, x_blk, jnp.uint32(0)))
        lo = lo.astype(BF16)
        hi = hi.astype(BF16)
        gate = jnp.dot(lo, wg_scr[0:HALF, :], preferred_element_type=F32) \
            + jnp.dot(hi, wg_scr[HALF:, :], preferred_element_type=F32)
        up = jnp.dot(lo, wu_scr[0:HALF, :], preferred_element_type=F32) \
            + jnp.dot(hi, wu_scr[HALF:, :], preferred_element_type=F32)
        act = (jax.nn.silu(gate) * up).astype(BF16)
        o = jnp.dot(act, wd_scr[...], preferred_element_type=F32)
        y_buf[slot] = _pack_bf16_pair(o[:, 0:HALF], o[:, HALF:]).reshape((BM,) + ROW_TILE)
        out_copy(g, slot).start()
        return carry

    lax.fori_loop(0, nblk_ref[e], block, 0)

    @pl.when(e == N_EXPERTS - 1)
    def _drain():
        out_copy(nused - 2, lax.rem(nused, 2)).wait()
        out_copy(nused - 1, 1 - lax.rem(nused, 2)).wait()


def _experts(first, nblk, nvalid, nused, xs, wg, wu, wd):
    grid_spec = pltpu.PrefetchScalarGridSpec(
        num_scalar_prefetch=4,
        grid=(N_EXPERTS,),
        in_specs=[
            pl.BlockSpec(memory_space=pl.ANY),
            pl.BlockSpec((1, D_MODEL, D_EXPERT), lambda e, *_: (e, 0, 0)),
            pl.BlockSpec((1, D_MODEL, D_EXPERT), lambda e, *_: (e, 0, 0)),
            pl.BlockSpec((1, D_EXPERT, D_MODEL), lambda e, *_: (e, 0, 0)),
        ],
        out_specs=pl.BlockSpec(memory_space=pl.ANY),
        scratch_shapes=[
            pltpu.VMEM((D_MODEL, D_EXPERT), BF16),
            pltpu.VMEM((D_MODEL, D_EXPERT), BF16),
            pltpu.VMEM((D_EXPERT, D_MODEL), BF16),
            pltpu.VMEM((IN_SLOTS, BM) + ROW_TILE, U32),
            pltpu.VMEM((2, BM) + ROW_TILE, U32),
            pltpu.SemaphoreType.DMA((IN_SLOTS,)),
            pltpu.SemaphoreType.DMA((2,)),
        ],
    )
    return pl.pallas_call(
        _expert_kernel,
        grid_spec=grid_spec,
        out_shape=jax.ShapeDtypeStruct((N_ROWS,) + ROW_TILE, U32),
        compiler_params=pltpu.CompilerParams(
            dimension_semantics=("arbitrary",), vmem_limit_bytes=VMEM_LIMIT),
        name="experts",
    )(first, nblk, nvalid, nused, xs, wg, wu, wd)


def _combine_kernel(x1_ref, rec_ref, yg_ref, p_ref, gple_ref, wpg_ref, wple_ref, gfin_ref, *rest):
    out_ref = rest[-1]
    ple = jnp.dot(p_ref[0].reshape(TM, D_PLE).astype(BF16), wple_ref[...], preferred_element_type=F32)
    rec = rec_ref[...]
    w0 = rec[:, REC_W0:REC_W0 + 1]
    w1 = rec[:, REC_W1:REC_W1 + 1]
    lo0, hi0 = _unpack_bf16_pair(yg_ref[0].reshape(TM, HALF))
    lo1, hi1 = _unpack_bf16_pair(yg_ref[1].reshape(TM, HALF))
    moe = jnp.concatenate([lo0 * w0 + lo1 * w1, hi0 * w0 + hi1 * w1], axis=1)
    x2 = x1_ref[...].reshape(TM, D_MODEL) + moe
    gate = jax.nn.sigmoid(jnp.dot(_rms(x2, gple_ref[...]).astype(BF16), wpg_ref[...],
                                  preferred_element_type=F32))
    x3 = x2 + gate * ple
    out_ref[...] = _rms(x3, gfin_ref[...]).reshape(BATCH, TT, D_MODEL)


def _combine(part, x1, rec, yg, p, gple, wpg, wple, gfin, out_prev=None):
    s0 = part * PART_STEPS
    seq_spec = pl.BlockSpec((BATCH, TT, D_MODEL), lambda i: (0, s0 + i, 0))
    in_specs = [
        seq_spec,
        pl.BlockSpec((TM, LANES), lambda i: (s0 + i, 0)),
        pl.BlockSpec((TOPK, TM) + ROW_TILE, lambda i: (0, i, 0, 0)),
        pl.BlockSpec((1, BATCH, TT, D_PLE), lambda i: (0, 0, s0 + i, 0)),
        _const_spec((1, D_MODEL)),
        _const_spec((D_MODEL, D_MODEL)),
        _const_spec((D_PLE, D_MODEL)),
        _const_spec((1, D_MODEL)),
    ]
    args = [x1, rec, yg, p, gple, wpg, wple, gfin]
    aliases = {}
    if out_prev is not None:
        in_specs.append(pl.BlockSpec(memory_space=pl.ANY))
        args.append(out_prev)
        aliases = {len(args) - 1: 0}
    return pl.pallas_call(
        _combine_kernel,
        grid=(PART_STEPS,),
        in_specs=in_specs,
        out_specs=seq_spec,
        out_shape=jax.ShapeDtypeStruct((BATCH, SEQ, D_MODEL), F32),
        input_output_aliases=aliases,
        compiler_params=pltpu.CompilerParams(
            dimension_semantics=("arbitrary",), vmem_limit_bytes=VMEM_LIMIT),
        name="combine",
    )(*args)


def kernel(x, p, g_mix, w_in, b_gate, ssm_a_re, ssm_a_im, ssm_log_dt, ssm_b_re, ssm_b_im, ssm_c_re,
           ssm_c_im, ssm_d, w_glu, conv_dw, conv_dw_b, conv_ln_g, conv_ln_b, w_conv_out, w_out, g_moe,
           w_router_group, b_router_group, w_router_expert, b_router_expert, w_exp_gate, w_exp_up,
           w_exp_down, g_ple, w_ple_gate, w_ple, g_final):
    assert x.shape == (BATCH, SEQ, D_MODEL) and p.shape == (1, BATCH, SEQ, D_PLE)
    row = lambda v: v.reshape(1, -1)

    mp, rmat, a_re, a_im = _ssm_matrices(ssm_a_re[0], ssm_a_im[0], ssm_log_dt[0], ssm_b_re[0],
                                         ssm_b_im[0], ssm_c_re[0], ssm_c_im[0])
    wr1, wr2, br = _router_weights(w_router_group[0], b_router_group[0], w_router_expert[0],
                                   b_router_expert[0])
    perm, permt = _time_major_permutation()
    x1, h2p, rec, rect, cnt = _mixer(
        x, row(g_mix[0]), w_in[0].astype(BF16), row(b_gate[0]), perm, permt, mp, rmat, a_re, a_im,
        row(ssm_d[0]), w_glu[0].astype(BF16), conv_dw[0], row(conv_dw_b[0]), row(conv_ln_g[0]),
        row(conv_ln_b[0]), w_conv_out[0].astype(BF16), w_out[0].astype(BF16), row(g_moe[0]), wr1, wr2, br)

    counts = cnt[0, LANE_EXP0:LANE_EXP0 + N_EXPERTS].astype(I32)
    pcounts = (counts + BM - 1) // BM * BM
    pends = jnp.cumsum(pcounts)
    pstarts = pends - pcounts
    eid = rect[REC_EID0:REC_EID1 + 1].astype(I32)
    rank = rect[REC_RANK0:REC_RANK1 + 1].astype(I32)
    dest = (jnp.sum(jnp.where(eid[..., None] == jnp.arange(N_EXPERTS, dtype=I32), pstarts, 0), axis=-1)
            + rank).reshape(TOPK, N_TOK // SC_WINDOW, SC_WINDOW)
    nused = (pends[-1] // BM).astype(I32)
    blk = jnp.arange(N_BLK, dtype=I32)[:, None] * BM
    in_expert = (pstarts[None, :] <= blk) & (blk < pends[None, :])
    nvalid = jnp.clip(jnp.sum(jnp.where(in_expert, (pstarts + counts)[None, :] - blk, 0), axis=1), 0, BM)

    xs = _dispatch(h2p, dest)
    ys = _experts(pstarts // BM, pcounts // BM, nvalid.astype(I32), nused.reshape(1), xs,
                  w_exp_gate[0], w_exp_up[0], w_exp_down[0])
    dest_tok = dest.reshape(TOPK, N_TOK)
    wpg = w_ple_gate[0].astype(BF16)
    wple = w_ple[0].astype(BF16)
    out = None
    for part in range(N_PARTS):
        tok = slice(part * PART_STEPS * TM, (part + 1) * PART_STEPS * TM)
        yg = _collect(ys, dest_tok[:, tok])
        out = _combine(part, x1, rec, yg, p, row(g_ple[0]), wpg, wple, row(g_final), out)
    return out
```

```python
import jax
import jax.numpy as jnp
from jax import lax
from jax.experimental import pallas as pl
from jax.experimental.pallas import tpu as pltpu
from jax.experimental.pallas import tpu_sc as plsc

F32 = jnp.float32
BF16 = jnp.bfloat16
U32 = jnp.uint32
I32 = jnp.int32

D_MODEL = 1024
BATCH = 8
SEQ = 2048
N_TOK = BATCH * SEQ
D_SSM = 512
SSM_GROUP_WIDTH = 16
SSM_GROUPS = 32
SSM_STATE = 64
D_CONV = 512
CONV_WIDTH = 31
D_IN = D_SSM + 2 * D_CONV + 2 * D_MODEL
N_GROUPS_MOE = 4
EXPERTS_PER_GROUP = 8
N_EXPERTS = 32
TOPK = 2
D_EXPERT = 512
D_PLE = 256
EPS = 1e-6

SUBLANES = 8
LANES = 128
assert BATCH == SUBLANES

TT = 64
TM = TT * BATCH
N_STEP = SEQ // TT
SB = 512
NSB = TM // SB
BPS = SB // TT
Q = 2
N_SLAB = D_SSM // LANES
GROUPS_PER_SLAB = SSM_GROUPS // N_SLAB
ROWS_Z = TM // Q
STATE_LANES = 2 * GROUPS_PER_SLAB * SSM_STATE
HALO = (CONV_WIDTH - 1) * BATCH
CHUNK_ROWS = SB // (Q * SUBLANES)
CONV_ROWS = 64
N_LC = D_CONV // LANES

LANE_GRP0 = 0
LANE_EXP0 = 32
REC_EID0, REC_EID1, REC_W0, REC_W1, REC_RANK0, REC_RANK1 = 0, 1, 2, 3, 4, 5
REC_ROWS = 8

BM = 256
N_BLK = (TOPK * N_TOK + N_EXPERTS * (BM - 1) + BM - 1) // BM
N_ROWS = N_BLK * BM
HALF = D_MODEL // 2
ROW_TILE = (HALF // LANES, LANES)
SC_WINDOW = 64
IN_AHEAD = 3
IN_SLOTS = IN_AHEAD + 1
N_PARTS = 2
PART_STEPS = N_STEP // N_PARTS

VMEM_LIMIT = 56 * 1024 * 1024


def _const_spec(shape):
    n = len(shape)
    return pl.BlockSpec(shape, lambda *_: (0,) * n, pipeline_mode=pl.Buffered(1))


def _rms(x, g):
    ms = jnp.mean(x * x, axis=-1, keepdims=True)
    return x * lax.rsqrt(ms + EPS) * g


def _pack_bf16_pair(lo, hi):
    ulo = lax.bitcast_convert_type(lo.astype(BF16).astype(F32), U32)
    uhi = lax.bitcast_convert_type(hi.astype(BF16).astype(F32), U32)
    return (ulo >> 16) | (uhi & jnp.uint32(0xFFFF0000))


def _unpack_bf16_pair(w):
    lo = lax.bitcast_convert_type(w << 16, F32)
    hi = lax.bitcast_convert_type(w & jnp.uint32(0xFFFF0000), F32)
    return lo, hi


def _mixer_kernel(x_ref, gmix_ref, win_ref, bgate_ref, perm_ref, permt_ref, mp_ref, r_ref, are_ref,
                  aim_ref, d_ref, wglu_ref, dw_ref, dwb_ref, lng_ref, lnb_ref, wco_ref, wout_ref,
                  gmoe_ref, wr1_ref, wr2_ref, br_ref,
                  x1_ref, h2p_ref, rec_ref, rect_ref, cnt_ref,
                  hb_scr, ht_scr, u_scr, y_scr, yi_scr, xs_scr, z_scr, conv_scr, act_scr, actb_scr,
                  rect_scr, s_scr, cnt_scr):
    step = pl.program_id(0)

    @pl.when(step == 0)
    def _init():
        z_scr[:, 0:HALO, :] = jnp.zeros((N_LC, HALO, LANES), F32)
        s_scr[...] = jnp.zeros(s_scr.shape, F32)
        cnt_scr[...] = jnp.zeros(cnt_scr.shape, F32)

    def sub_rows(r):
        return pl.ds(pl.multiple_of(r * SB, SB), SB)

    def phase_a(r, carry):
        xb = x_ref[pl.ds(r * BPS, BPS)].reshape(SB, D_MODEL)
        hb_scr[sub_rows(r), :] = _rms(xb, gmix_ref[...]).astype(BF16)
        return carry

    lax.fori_loop(0, NSB, phase_a, 0)

    ht_scr[...] = jnp.dot(perm_ref[...], hb_scr[...], preferred_element_type=F32).astype(BF16)

    def phase_a3(r, carry):
        h = ht_scr[sub_rows(r), :]
        u = jnp.dot(h, win_ref[:, 0:D_SSM], preferred_element_type=F32)
        u_scr[pl.ds(r * CHUNK_ROWS, CHUNK_ROWS)] = u.reshape(CHUNK_ROWS, Q, SUBLANES, D_SSM)
        v = jnp.dot(h, win_ref[:, D_SSM:D_SSM + 2 * D_CONV], preferred_element_type=F32)
        zc = v[:, 0:D_CONV] * jax.nn.sigmoid(v[:, D_CONV:])
        for lc in range(N_LC):
            z_scr[lc, pl.ds(pl.multiple_of(HALO + r * SB, SUBLANES), SB), :] = zc[:, lc * LANES:(lc + 1) * LANES]
        return carry

    lax.fori_loop(0, NSB, phase_a3, 0)

    for s in range(N_SLAB):
        lanes = slice(s * LANES, (s + 1) * LANES)
        z = jnp.concatenate(
            [u_scr[:, i, :, lanes].reshape(ROWS_Z, LANES) for i in range(Q)], axis=1).astype(BF16)
        xp = jnp.dot(z, mp_ref[s], preferred_element_type=F32)
        yi_scr[s] = xp[:, 0:Q * LANES]
        xs_scr[s] = xp[:, Q * LANES:]

    half = STATE_LANES // 2
    for s in range(N_SLAB):
        a_re = jnp.broadcast_to(are_ref[s:s + 1, :], (SUBLANES, half))
        a_im = jnp.broadcast_to(aim_ref[s:s + 1, :], (SUBLANES, half))

        def scan_body(k, carry, s=s, a_re=a_re, a_im=a_im):
            s_re, s_im = carry
            rows = pl.ds(pl.multiple_of(k * SUBLANES, SUBLANES), SUBLANES)
            x_re = xs_scr[s, rows, 0:half]
            x_im = xs_scr[s, rows, half:]
            xs_scr[s, rows, 0:half] = s_re
            xs_scr[s, rows, half:] = s_im
            n_re = a_re * s_re - a_im * s_im + x_re
            n_im = a_re * s_im + a_im * s_re + x_im
            return n_re, n_im

        s_re, s_im = lax.fori_loop(0, ROWS_Z // SUBLANES, scan_body,
                                   (s_scr[s, :, 0:half], s_scr[s, :, half:]), unroll=True)
        s_scr[s, :, 0:half] = s_re
        s_scr[s, :, half:] = s_im

    for s in range(N_SLAB):
        lanes = slice(s * LANES, (s + 1) * LANES)
        y_tot = yi_scr[s] + jnp.dot(xs_scr[s].astype(BF16), r_ref[s], preferred_element_type=F32)
        for j in range(Q):
            y_scr[:, j, :, lanes] = y_tot[:, j * LANES:(j + 1) * LANES].reshape(
                ROWS_Z // SUBLANES, SUBLANES, LANES)

    def phase_c1(r, carry):
        rows = sub_rows(r)
        crow = pl.ds(r * CHUNK_ROWS, CHUNK_ROWS)
        y = y_scr[crow].reshape(SB, D_SSM) + d_ref[...] * u_scr[crow].reshape(SB, D_SSM)
        act_scr[rows, 0:D_SSM] = jax.nn.gelu(y).astype(BF16)
        for lc in range(N_LC):
            lanes = slice(lc * LANES, (lc + 1) * LANES)

            def conv_piece(rc, c, lc=lc, lanes=lanes):
                r0 = r * SB + rc * CONV_ROWS
                piece = jnp.broadcast_to(dwb_ref[:, lanes], (CONV_ROWS, LANES))
                for j in range(CONV_WIDTH):
                    zrows = pl.ds(pl.multiple_of(r0 + j * BATCH, SUBLANES), CONV_ROWS)
                    piece = piece + dw_ref[j:j + 1, lanes] * z_scr[lc, zrows, :]
                conv_scr[pl.ds(pl.multiple_of(rc * CONV_ROWS, CONV_ROWS), CONV_ROWS), lanes] = piece
                return c

            lax.fori_loop(0, SB // CONV_ROWS, conv_piece, 0)
        acc = conv_scr[...]
        mu = jnp.mean(acc, axis=-1, keepdims=True)
        cen = acc - mu
        var = jnp.mean(cen * cen, axis=-1, keepdims=True)
        ln = cen * lax.rsqrt(var + EPS) * lng_ref[...] + lnb_ref[...]
        act_scr[rows, D_SSM:] = jax.nn.silu(ln).astype(BF16)
        return carry

    lax.fori_loop(0, NSB, phase_c1, 0)
    z_scr[:, 0:HALO, :] = z_scr[:, TM:TM + HALO, :]

    actb_scr[...] = jnp.dot(permt_ref[...], act_scr[...], preferred_element_type=F32).astype(BF16)

    lane = lax.broadcasted_iota(I32, (1, LANES), 1).astype(F32)
    grp_mask = lane < float(N_GROUPS_MOE)
    exp_lane = (lane >= float(LANE_EXP0)) & (lane < float(LANE_EXP0 + N_EXPERTS))
    lane_grp = jnp.floor((lane - float(LANE_EXP0)) * (1.0 / EXPERTS_PER_GROUP))
    tri = (lax.broadcasted_iota(I32, (SB, SB), 0) > lax.broadcasted_iota(I32, (SB, SB), 1)).astype(BF16)
    neg_inf = float("-inf")
    big = float(4 * LANES)

    def phase_c3(r, carry):
        rows = sub_rows(r)
        h = hb_scr[rows, :]
        g0 = D_SSM + 2 * D_CONV
        gate_ssm = jnp.dot(h, win_ref[:, g0:g0 + D_MODEL], preferred_element_type=F32) \
            + bgate_ref[:, 0:D_MODEL]
        gate_conv = jnp.dot(h, win_ref[:, g0 + D_MODEL:], preferred_element_type=F32) \
            + bgate_ref[:, D_MODEL:]
        zz = jnp.dot(actb_scr[rows, 0:D_SSM], wglu_ref[...], preferred_element_type=F32)
        y_ssm = zz[:, 0:D_MODEL] * jax.nn.sigmoid(zz[:, D_MODEL:])
        y_conv = jnp.dot(actb_scr[rows, D_SSM:], wco_ref[...], preferred_element_type=F32)

        merged = jax.nn.sigmoid(gate_ssm) * y_ssm + jax.nn.sigmoid(gate_conv) * y_conv
        xb = x_ref[pl.ds(r * BPS, BPS)].reshape(SB, D_MODEL)
        x1 = xb + jnp.dot(merged.astype(BF16), wout_ref[...], preferred_element_type=F32)
        x1_ref[pl.ds(r * BPS, BPS)] = x1.reshape(BPS, TT, D_MODEL)

        h2 = _rms(x1, gmoe_ref[...])
        h2p_ref[rows] = _pack_bf16_pair(h2[:, 0:HALF], h2[:, HALF:]).reshape((SB,) + ROW_TILE)

        h2_hi = h2.astype(BF16)
        h2_lo = (h2 - h2_hi.astype(F32)).astype(BF16)
        l1 = jnp.dot(h2_hi, wr1_ref[...], preferred_element_type=F32)
        l2 = jnp.dot(h2_lo, wr2_ref[...], preferred_element_type=F32)
        logits = l1[:, 0:LANES] + l1[:, LANES:] + l2 + br_ref[...]

        lg = jnp.where(grp_mask, logits, neg_inf)
        g_max = jnp.max(lg, axis=-1, keepdims=True)
        g_sel = jnp.min(jnp.where(lg == g_max, lane, big), axis=-1, keepdims=True)
        p_g = 1.0 / jnp.sum(jnp.where(grp_mask, jnp.exp(logits - g_max), 0.0), axis=-1, keepdims=True)
        le = jnp.where(exp_lane & (lane_grp == g_sel), logits, neg_inf)
        m1 = jnp.max(le, axis=-1, keepdims=True)
        i1 = jnp.min(jnp.where(le == m1, lane, big), axis=-1, keepdims=True)
        le2 = jnp.where(lane == i1, neg_inf, le)
        m2 = jnp.max(le2, axis=-1, keepdims=True)
        i2 = jnp.min(jnp.where(le2 == m2, lane, big), axis=-1, keepdims=True)
        e2 = jnp.exp(m2 - m1)
        den = 1.0 + e2
        w_a = (1.0 / den) * p_g
        w_b = (e2 / den) * p_g

        sel1 = lane == i1
        sel2 = lane == i2
        onehot = jnp.where(sel1 | sel2, 1.0, 0.0)
        prefix = jnp.dot(tri, onehot.astype(BF16), preferred_element_type=F32) + cnt_scr[...]
        rank_a = jnp.sum(jnp.where(sel1, prefix, 0.0), axis=-1, keepdims=True)
        rank_b = jnp.sum(jnp.where(sel2, prefix, 0.0), axis=-1, keepdims=True)
        cnt_scr[...] = cnt_scr[...] + jnp.sum(onehot, axis=0, keepdims=True)

        rec = jnp.where(lane == float(REC_EID0), i1 - float(LANE_EXP0), 0.0)
        rec = jnp.where(lane == float(REC_EID1), i2 - float(LANE_EXP0), rec)
        rec = jnp.where(lane == float(REC_W0), w_a, rec)
        rec = jnp.where(lane == float(REC_W1), w_b, rec)
        rec = jnp.where(lane == float(REC_RANK0), rank_a, rec)
        rec = jnp.where(lane == float(REC_RANK1), rank_b, rec)
        rec_ref[rows, :] = rec
        rect_scr[r] = jnp.transpose(rec)[0:REC_ROWS, :]
        return carry

    lax.fori_loop(0, NSB, phase_c3, 0)

    for r in range(NSB):
        rect_ref[:, r * SB:(r + 1) * SB] = rect_scr[r]
    cnt_ref[...] = cnt_scr[...]


def _mixer(x, gmix, win, bgate, perm, permt, mp, rmat, a_re, a_im, dvec, wglu, dw, dwb, lng, lnb, wco,
           wout, gmoe, wr1, wr2, br):
    seq_spec = pl.BlockSpec((BATCH, TT, D_MODEL), lambda i: (0, i, 0))
    in_specs = [
        seq_spec,
        _const_spec((1, D_MODEL)),
        _const_spec((D_MODEL, D_IN)),
        _const_spec((1, 2 * D_MODEL)),
        _const_spec((TM, TM)),
        _const_spec((TM, TM)),
        _const_spec(mp.shape),
        _const_spec(rmat.shape),
        _const_spec(a_re.shape),
        _const_spec(a_im.shape),
        _const_spec((1, D_SSM)),
        _const_spec((D_SSM, 2 * D_MODEL)),
        _const_spec((CONV_WIDTH, D_CONV)),
        _const_spec((1, D_CONV)),
        _const_spec((1, D_CONV)),
        _const_spec((1, D_CONV)),
        _const_spec((D_CONV, D_MODEL)),
        _const_spec((D_MODEL, D_MODEL)),
        _const_spec((1, D_MODEL)),
        _const_spec((D_MODEL, 2 * LANES)),
        _const_spec((D_MODEL, LANES)),
        _const_spec((1, LANES)),
    ]
    out_specs = [
        seq_spec,
        pl.BlockSpec((TM,) + ROW_TILE, lambda i: (i, 0, 0)),
        pl.BlockSpec((TM, LANES), lambda i: (i, 0)),
        pl.BlockSpec((REC_ROWS, TM), lambda i: (0, i)),
        pl.BlockSpec((1, LANES), lambda i: (0, 0)),
    ]
    out_shape = [
        jax.ShapeDtypeStruct((BATCH, SEQ, D_MODEL), F32),
        jax.ShapeDtypeStruct((N_TOK,) + ROW_TILE, U32),
        jax.ShapeDtypeStruct((N_TOK, LANES), F32),
        jax.ShapeDtypeStruct((REC_ROWS, N_TOK), F32),
        jax.ShapeDtypeStruct((1, LANES), F32),
    ]
    chunk_shape = (ROWS_Z // SUBLANES, Q, SUBLANES, D_SSM)
    scratch = [
        pltpu.VMEM((TM, D_MODEL), BF16),
        pltpu.VMEM((TM, D_MODEL), BF16),
        pltpu.VMEM(chunk_shape, F32),
        pltpu.VMEM(chunk_shape, F32),
        pltpu.VMEM((N_SLAB, ROWS_Z, Q * LANES), F32),
        pltpu.VMEM((N_SLAB, ROWS_Z, STATE_LANES), F32),
        pltpu.VMEM((N_LC, HALO + TM, LANES), F32),
        pltpu.VMEM((SB, D_CONV), F32),
        pltpu.VMEM((TM, D_SSM + D_CONV), BF16),
        pltpu.VMEM((TM, D_SSM + D_CONV), BF16),
        pltpu.VMEM((NSB, REC_ROWS, SB), F32),
        pltpu.VMEM((N_SLAB, SUBLANES, STATE_LANES), F32),
        pltpu.VMEM((1, LANES), F32),
    ]
    return pl.pallas_call(
        _mixer_kernel,
        grid=(N_STEP,),
        in_specs=in_specs,
        out_specs=out_specs,
        out_shape=out_shape,
        scratch_shapes=scratch,
        compiler_params=pltpu.CompilerParams(
            dimension_semantics=("arbitrary",), vmem_limit_bytes=VMEM_LIMIT),
        name="mixer",
    )(x, gmix, win, bgate, perm, permt, mp, rmat, a_re, a_im, dvec, wglu, dw, dwb, lng, lnb, wco, wout,
      gmoe, wr1, wr2, br)


def _cmul(a, b):
    return a[0] * b[0] - a[1] * b[1], a[0] * b[1] + a[1] * b[0]


def _ssm_matrices(a_re, a_im, log_dt, b_re, b_im, c_re, c_im):
    dt = jnp.exp(log_dt)[:, None]
    mag = jnp.exp(a_re * dt)
    lam = (mag * jnp.cos(a_im * dt), mag * jnp.sin(a_im * dt))
    den = a_re * a_re + a_im * a_im
    nr = lam[0] - 1.0
    ni = lam[1]
    z_re = (nr * a_re + ni * a_im) / den
    z_im = (ni * a_re - nr * a_im) / den
    bbar = (z_re[..., None] * b_re - z_im[..., None] * b_im,
            z_re[..., None] * b_im + z_im[..., None] * b_re)
    pw = [(jnp.ones_like(lam[0]), jnp.zeros_like(lam[0])), lam]
    for _ in range(2, Q + 1):
        pw.append(_cmul(pw[-1], lam))
    e = [(c_re * p[0][:, None, :] - c_im * p[1][:, None, :],
          c_re * p[1][:, None, :] + c_im * p[0][:, None, :]) for p in pw]
    hp = lax.Precision.HIGHEST
    k = [jnp.einsum('gcn,gnd->gcd', e[m][0], bbar[0], precision=hp)
         - jnp.einsum('gcn,gnd->gcd', e[m][1], bbar[1], precision=hp) for m in range(Q)]
    eye = jnp.eye(GROUPS_PER_SLAB, dtype=F32)
    split = lambda t: t.reshape((N_SLAB, GROUPS_PER_SLAB) + t.shape[1:])
    zero_k = jnp.zeros_like(k[0])
    kb = jnp.stack([jnp.stack([split(jnp.swapaxes(k[j - i] if j >= i else zero_k, 1, 2))
                               for j in range(Q)]) for i in range(Q)])
    m_mat = jnp.einsum('ijsgdc,gh->sigdjhc', kb, eye).reshape(N_SLAB, Q * LANES, Q * LANES)
    f = [_cmul((pw[Q - 1 - i][0][..., None], pw[Q - 1 - i][1][..., None]), bbar) for i in range(Q)]
    p_parts = []
    for part in range(2):
        fs = jnp.stack([split(f[i][part]) for i in range(Q)])
        p_parts.append(jnp.einsum('isgnd,gh->sigdhn', fs, eye).reshape(N_SLAB, Q * LANES, STATE_LANES // 2))
    p_mat = jnp.concatenate(p_parts, axis=-1)
    r_parts = []
    for part, sign in ((0, 1.0), (1, -1.0)):
        es = jnp.stack([split(e[j + 1][part]) for j in range(Q)])
        r_parts.append(sign * jnp.einsum('jsgcn,gh->shnjgc', es, eye).reshape(
            N_SLAB, STATE_LANES // 2, Q * LANES))
    r_mat = jnp.concatenate(r_parts, axis=1)
    mp = jnp.concatenate([m_mat, p_mat], axis=-1).astype(BF16)
    a_q = pw[Q]
    return (mp, r_mat.astype(BF16),
            a_q[0].reshape(N_SLAB, STATE_LANES // 2), a_q[1].reshape(N_SLAB, STATE_LANES // 2))


def _router_weights(w_rg, b_rg, w_re, b_re):
    pad_g = LANE_EXP0 - LANE_GRP0 - N_GROUPS_MOE
    pad_e = LANES - LANE_EXP0 - N_EXPERTS
    w = jnp.concatenate([w_rg, jnp.zeros((D_MODEL, pad_g), F32), w_re, jnp.zeros((D_MODEL, pad_e), F32)], axis=1)
    b = jnp.concatenate([b_rg, jnp.zeros((pad_g,), F32), b_re, jnp.zeros((pad_e,), F32)]).reshape(1, LANES)
    w_hi = w.astype(BF16)
    w_lo = (w - w_hi.astype(F32)).astype(BF16)
    return jnp.concatenate([w_hi, w_lo], axis=1), w_hi, b


def _time_major_permutation():
    tm = jnp.arange(TM, dtype=I32)
    src = (tm % BATCH) * TT + tm // BATCH
    perm = (src[:, None] == jnp.arange(TM, dtype=I32)[None, :]).astype(BF16)
    return perm, perm.T


def _sc_mesh():
    return plsc.VectorSubcoreMesh(core_axis_name="core", subcore_axis_name="subcore")


def _sc_worker(mesh):
    return lax.axis_index("core") * mesh.num_subcores + lax.axis_index("subcore")


def _dispatch(h2p, dest):
    mesh = _sc_mesh()
    n_win = N_TOK // SC_WINDOW
    per_worker = n_win // (mesh.num_cores * mesh.num_subcores)
    assert per_worker * mesh.num_cores * mesh.num_subcores == n_win

    @pl.kernel(out_type=jax.ShapeDtypeStruct((N_ROWS,) + ROW_TILE, U32), mesh=mesh,
               scratch_types=[pltpu.VMEM((SC_WINDOW,), I32), pltpu.VMEM((SC_WINDOW,) + ROW_TILE, U32)])
    def scatter_rows(h_hbm, dest_hbm, xs_hbm, idx_v, rows_v):
        first = _sc_worker(mesh) * per_worker

        @pl.loop(0, per_worker)
        def _(w):
            win = first + w
            pltpu.sync_copy(h_hbm.at[pl.ds(win * SC_WINDOW, SC_WINDOW)], rows_v)
            for j in range(TOPK):
                pltpu.sync_copy(dest_hbm.at[j, win], idx_v)
                pltpu.sync_copy(rows_v, xs_hbm.at[idx_v])

    return scatter_rows(h2p, dest)


def _collect(ys, dest):
    mesh = _sc_mesh()
    n_tok = dest.shape[1]
    n_win = TOPK * n_tok // SC_WINDOW
    per_worker = n_win // (mesh.num_cores * mesh.num_subcores)
    assert per_worker * mesh.num_cores * mesh.num_subcores == n_win

    @pl.kernel(out_type=jax.ShapeDtypeStruct((TOPK * n_tok,) + ROW_TILE, U32), mesh=mesh,
               scratch_types=[pltpu.VMEM((SC_WINDOW,), I32), pltpu.VMEM((SC_WINDOW,) + ROW_TILE, U32)])
    def gather_rows(ys_hbm, dest_hbm, yg_hbm, idx_v, rows_v):
        first = _sc_worker(mesh) * per_worker

        @pl.loop(0, per_worker)
        def _(w):
            win = first + w
            pltpu.sync_copy(dest_hbm.at[win], idx_v)
            pltpu.sync_copy(ys_hbm.at[idx_v], rows_v)
            pltpu.sync_copy(rows_v, yg_hbm.at[pl.ds(win * SC_WINDOW, SC_WINDOW)])

    return gather_rows(ys, dest.reshape(n_win, SC_WINDOW)).reshape((TOPK, n_tok) + ROW_TILE)


def _expert_kernel(first_ref, nblk_ref, nvalid_ref, nused_ref, xs_hbm, wg_ref, wu_ref, wd_ref, ys_hbm,
                   wg_scr, wu_scr, wd_scr, x_buf, y_buf, in_sem, out_sem):
    e = pl.program_id(0)
    nused = nused_ref[0]

    def in_copy(g):
        slot = lax.rem(g, IN_SLOTS)
        return pltpu.make_async_copy(xs_hbm.at[pl.ds(g * BM, BM)], x_buf.at[slot], in_sem.at[slot])

    def out_copy(g, slot):
        return pltpu.make_async_copy(y_buf.at[slot], ys_hbm.at[pl.ds(g * BM, BM)], out_sem.at[slot])

    @pl.when(e == 0)
    def _first():
        for g in range(IN_AHEAD):
            in_copy(g).start()

    wg_scr[...] = wg_ref[0].astype(BF16)
    wu_scr[...] = wu_ref[0].astype(BF16)
    wd_scr[...] = wd_ref[0].astype(BF16)

    def block(b, carry):
        g = first_ref[e] + b
        slot = lax.rem(g, 2)
        in_copy(g).wait()

        @pl.when(g + IN_AHEAD < nused)
        def _prefetch():
            in_copy(g + IN_AHEAD).start()

        @pl.when(g >= 2)
        def _slot_free():
            out_copy(g - 2, slot).wait()

        valid = lax.broadcasted_iota(I32, (BM, 1), 0) < nvalid_ref[g]
        x_blk = x_buf[lax.rem(g, IN_SLOTS)].reshape(BM, HALF)
        lo, hi = _unpack_bf16_pair(jnp.where(valid, x_blk, jnp.uint32(0)))
        lo = lo.astype(BF16)
        hi = hi.astype(BF16)
        gate = jnp.dot(lo, wg_scr[0:HALF, :], preferred_element_type=F32) \
            + jnp.dot(hi, wg_scr[HALF:, :], preferred_element_type=F32)
        up = jnp.dot(lo, wu_scr[0:HALF, :], preferred_element_type=F32) \
            + jnp.dot(hi, wu_scr[HALF:, :], preferred_element_type=F32)
        act = (jax.nn.silu(gate) * up).astype(BF16)
        o = jnp.dot(act, wd_scr[...], preferred_element_type=F32)
        y_buf[slot] = _pack_bf16_pair(o[:, 0:HALF], o[:, HALF:]).reshape((BM,) + ROW_TILE)
        out_copy(g, slot).start()
        return carry

    lax.fori_loop(0, nblk_ref[e], block, 0)

    @pl.when(e == N_EXPERTS - 1)
    def _drain():
        out_copy(nused - 2, lax.rem(nused, 2)).wait()
        out_copy(nused - 1, 1 - lax.rem(nused, 2)).wait()


def _experts(first, nblk, nvalid, nused, xs, wg, wu, wd):
    grid_spec = pltpu.PrefetchScalarGridSpec(
        num_scalar_prefetch=4,
        grid=(N_EXPERTS,),
        in_specs=[
            pl.BlockSpec(memory_space=pl.ANY),
            pl.BlockSpec((1, D_MODEL, D_EXPERT), lambda e, *_: (e, 0, 0)),
            pl.BlockSpec((1, D_MODEL, D_EXPERT), lambda e, *_: (e, 0, 0)),
            pl.BlockSpec((1, D_EXPERT, D_MODEL), lambda e, *_: (e, 0, 0)),
        ],
        out_specs=pl.BlockSpec(memory_space=pl.ANY),
        scratch_shapes=[
            pltpu.VMEM((D_MODEL, D_EXPERT), BF16),
            pltpu.VMEM((D_MODEL, D_EXPERT), BF16),
            pltpu.VMEM((D_EXPERT, D_MODEL), BF16),
            pltpu.VMEM((IN_SLOTS, BM) + ROW_TILE, U32),
            pltpu.VMEM((2, BM) + ROW_TILE, U32),
            pltpu.SemaphoreType.DMA((IN_SLOTS,)),
            pltpu.SemaphoreType.DMA((2,)),
        ],
    )
    return pl.pallas_call(
        _expert_kernel,
        grid_spec=grid_spec,
        out_shape=jax.ShapeDtypeStruct((N_ROWS,) + ROW_TILE, U32),
        compiler_params=pltpu.CompilerParams(
            dimension_semantics=("arbitrary",), vmem_limit_bytes=VMEM_LIMIT),
        name="experts",
    )(first, nblk, nvalid, nused, xs, wg, wu, wd)


def _combine_kernel(x1_ref, rec_ref, yg_ref, p_ref, gple_ref, wpg_ref, wple_ref, gfin_ref, *rest):
    out_ref = rest[-1]
    ple = jnp.dot(p_ref[0].reshape(TM, D_PLE).astype(BF16), wple_ref[...], preferred_element_type=F32)
    rec = rec_ref[...]
    w0 = rec[:, REC_W0:REC_W0 + 1]
    w1 = rec[:, REC_W1:REC_W1 + 1]
    lo0, hi0 = _unpack_bf16_pair(yg_ref[0].reshape(TM, HALF))
    lo1, hi1 = _unpack_bf16_pair(yg_ref[1].reshape(TM, HALF))
    moe = jnp.concatenate([lo0 * w0 + lo1 * w1, hi0 * w0 + hi1 * w1], axis=1)
    x2 = x1_ref[...].reshape(TM, D_MODEL) + moe
    gate = jax.nn.sigmoid(jnp.dot(_rms(x2, gple_ref[...]).astype(BF16), wpg_ref[...],
                                  preferred_element_type=F32))
    x3 = x2 + gate * ple
    out_ref[...] = _rms(x3, gfin_ref[...]).reshape(BATCH, TT, D_MODEL)


def _combine(part, x1, rec, yg, p, gple, wpg, wple, gfin, out_prev=None):
    s0 = part * PART_STEPS
    seq_spec = pl.BlockSpec((BATCH, TT, D_MODEL), lambda i: (0, s0 + i, 0))
    in_specs = [
        seq_spec,
        pl.BlockSpec((TM, LANES), lambda i: (s0 + i, 0)),
        pl.BlockSpec((TOPK, TM) + ROW_TILE, lambda i: (0, i, 0, 0)),
        pl.BlockSpec((1, BATCH, TT, D_PLE), lambda i: (0, 0, s0 + i, 0)),
        _const_spec((1, D_MODEL)),
        _const_spec((D_MODEL, D_MODEL)),
        _const_spec((D_PLE, D_MODEL)),
        _const_spec((1, D_MODEL)),
    ]
    args = [x1, rec, yg, p, gple, wpg, wple, gfin]
    aliases = {}
    if out_prev is not None:
        in_specs.append(pl.BlockSpec(memory_space=pl.ANY))
        args.append(out_prev)
        aliases = {len(args) - 1: 0}
    return pl.pallas_call(
        _combine_kernel,
        grid=(PART_STEPS,),
        in_specs=in_specs,
        out_specs=seq_spec,
        out_shape=jax.ShapeDtypeStruct((BATCH, SEQ, D_MODEL), F32),
        input_output_aliases=aliases,
        compiler_params=pltpu.CompilerParams(
            dimension_semantics=("arbitrary",), vmem_limit_bytes=VMEM_LIMIT),
        name="combine",
    )(*args)


def kernel(x, p, g_mix, w_in, b_gate, ssm_a_re, ssm_a_im, ssm_log_dt, ssm_b_re, ssm_b_im, ssm_c_re,
           ssm_c_im, ssm_d, w_glu, conv_dw, conv_dw_b, conv_ln_g, conv_ln_b, w_conv_out, w_out, g_moe,
           w_router_group, b_router_group, w_router_expert, b_router_expert, w_exp_gate, w_exp_up,
           w_exp_down, g_ple, w_ple_gate, w_ple, g_final):
    assert x.shape == (BATCH, SEQ, D_MODEL) and p.shape == (1, BATCH, SEQ, D_PLE)
    row = lambda v: v.reshape(1, -1)

    mp, rmat, a_re, a_im = _ssm_matrices(ssm_a_re[0], ssm_a_im[0], ssm_log_dt[0], ssm_b_re[0],
                                         ssm_b_im[0], ssm_c_re[0], ssm_c_im[0])
    wr1, wr2, br = _router_weights(w_router_group[0], b_router_group[0], w_router_expert[0],
                                   b_router_expert[0])
    perm, permt = _time_major_permutation()
    x1, h2p, rec, rect, cnt = _mixer(
        x, row(g_mix[0]), w_in[0].astype(BF16), row(b_gate[0]), perm, permt, mp, rmat, a_re, a_im,
        row(ssm_d[0]), w_glu[0].astype(BF16), conv_dw[0], row(conv_dw_b[0]), row(conv_ln_g[0]),
        row(conv_ln_b[0]), w_conv_out[0].astype(BF16), w_out[0].astype(BF16), row(g_moe[0]), wr1, wr2, br)

    counts = cnt[0, LANE_EXP0:LANE_EXP0 + N_EXPERTS].astype(I32)
    pcounts = (counts + BM - 1) // BM * BM
    pends = jnp.cumsum(pcounts)
    pstarts = pends - pcounts
    eid = rect[REC_EID0:REC_EID1 + 1].astype(I32)
    rank = rect[REC_RANK0:REC_RANK1 + 1].astype(I32)
    dest = (jnp.sum(jnp.where(eid[..., None] == jnp.arange(N_EXPERTS, dtype=I32), pstarts, 0), axis=-1)
            + rank).reshape(TOPK, N_TOK // SC_WINDOW, SC_WINDOW)
    nused = (pends[-1] // BM).astype(I32)
    blk = jnp.arange(N_BLK, dtype=I32)[:, None] * BM
    in_expert = (pstarts[None, :] <= blk) & (blk < pends[None, :])
    nvalid = jnp.clip(jnp.sum(jnp.where(in_expert, (pstarts + counts)[None, :] - blk, 0), axis=1), 0, BM)

    xs = _dispatch(h2p, dest)
    ys = _experts(pstarts // BM, pcounts // BM, nvalid.astype(I32), nused.reshape(1), xs,
                  w_exp_gate[0], w_exp_up[0], w_exp_down[0])
    dest_tok = dest.reshape(TOPK, N_TOK)
    wpg = w_ple_gate[0].astype(BF16)
    wple = w_ple[0].astype(BF16)
    out = None
    for part in range(N_PARTS):
        tok = slice(part * PART_STEPS * TM, (part + 1) * PART_STEPS * TM)
        yg = _collect(ys, dest_tok[:, tok])
        out = _combine(part, x1, rec, yg, p, row(g_ple[0]), wpg, wple, row(g_final), out)
    return out
```

```python
import jax
import jax.numpy as jnp
from jax import lax
from jax.experimental import pallas as pl
from jax.experimental.pallas import tpu as pltpu
from jax.experimental.pallas import tpu_sc as plsc

F32 = jnp.float32
BF16 = jnp.bfloat16
U32 = jnp.uint32
I32 = jnp.int32

D_MODEL = 1024
BATCH = 8
SEQ = 2048
N_TOK = BATCH * SEQ
D_SSM = 512
SSM_GROUP_WIDTH = 16
SSM_GROUPS = 32
SSM_STATE = 64
D_CONV = 512
CONV_WIDTH = 31
D_IN = D_SSM + 2 * D_CONV + 2 * D_MODEL
N_GROUPS_MOE = 4
EXPERTS_PER_GROUP = 8
N_EXPERTS = 32
TOPK = 2
D_EXPERT = 512
D_PLE = 256
EPS = 1e-6

SUBLANES = 8
LANES = 128
assert BATCH == SUBLANES

TT = 64
TM = TT * BATCH
N_STEP = SEQ // TT
SB = 512
NSB = TM // SB
BPS = SB // TT
Q = 2
N_SLAB = D_SSM // LANES
GROUPS_PER_SLAB = SSM_GROUPS // N_SLAB
ROWS_Z = TM // Q
STATE_LANES = 2 * GROUPS_PER_SLAB * SSM_STATE
HALO = (CONV_WIDTH - 1) * BATCH
CHUNK_ROWS = SB // (Q * SUBLANES)
CONV_ROWS = 64
N_LC = D_CONV // LANES

LANE_GRP0 = 0
LANE_EXP0 = 32
REC_EID0, REC_EID1, REC_W0, REC_W1, REC_RANK0, REC_RANK1 = 0, 1, 2, 3, 4, 5
REC_ROWS = 8

BM = 256
N_BLK = (TOPK * N_TOK + N_EXPERTS * (BM - 1) + BM - 1) // BM
N_ROWS = N_BLK * BM
HALF = D_MODEL // 2
ROW_TILE = (HALF // LANES, LANES)
SC_WINDOW = 64
IN_AHEAD = 3
IN_SLOTS = IN_AHEAD + 1
N_PARTS = 2
PART_STEPS = N_STEP // N_PARTS

VMEM_LIMIT = 56 * 1024 * 1024


def _const_spec(shape):
    n = len(shape)
    return pl.BlockSpec(shape, lambda *_: (0,) * n, pipeline_mode=pl.Buffered(1))


def _rms(x, g):
    ms = jnp.mean(x * x, axis=-1, keepdims=True)
    return x * lax.rsqrt(ms + EPS) * g


def _pack_bf16_pair(lo, hi):
    ulo = lax.bitcast_convert_type(lo.astype(BF16).astype(F32), U32)
    uhi = lax.bitcast_convert_type(hi.astype(BF16).astype(F32), U32)
    return (ulo >> 16) | (uhi & jnp.uint32(0xFFFF0000))


def _unpack_bf16_pair(w):
    lo = lax.bitcast_convert_type(w << 16, F32)
    hi = lax.bitcast_convert_type(w & jnp.uint32(0xFFFF0000), F32)
    return lo, hi


def _mixer_kernel(x_ref, gmix_ref, win_ref, bgate_ref, perm_ref, permt_ref, mp_ref, r_ref, are_ref,
                  aim_ref, d_ref, wglu_ref, dw_ref, dwb_ref, lng_ref, lnb_ref, wco_ref, wout_ref,
                  gmoe_ref, wr1_ref, wr2_ref, br_ref,
                  x1_ref, h2p_ref, rec_ref, rect_ref, cnt_ref,
                  hb_scr, ht_scr, u_scr, y_scr, yi_scr, xs_scr, z_scr, conv_scr, act_scr, actb_scr,
                  rect_scr, s_scr, cnt_scr):
    step = pl.program_id(0)

    @pl.when(step == 0)
    def _init():
        z_scr[:, 0:HALO, :] = jnp.zeros((N_LC, HALO, LANES), F32)
        s_scr[...] = jnp.zeros(s_scr.shape, F32)
        cnt_scr[...] = jnp.zeros(cnt_scr.shape, F32)

    def sub_rows(r):
        return pl.ds(pl.multiple_of(r * SB, SB), SB)

    def phase_a(r, carry):
        xb = x_ref[pl.ds(r * BPS, BPS)].reshape(SB, D_MODEL)
        hb_scr[sub_rows(r), :] = _rms(xb, gmix_ref[...]).astype(BF16)
        return carry

    lax.fori_loop(0, NSB, phase_a, 0)

    ht_scr[...] = jnp.dot(perm_ref[...], hb_scr[...], preferred_element_type=F32).astype(BF16)

    def phase_a3(r, carry):
        h = ht_scr[sub_rows(r), :]
        u = jnp.dot(h, win_ref[:, 0:D_SSM], preferred_element_type=F32)
        u_scr[pl.ds(r * CHUNK_ROWS, CHUNK_ROWS)] = u.reshape(CHUNK_ROWS, Q, SUBLANES, D_SSM)
        v = jnp.dot(h, win_ref[:, D_SSM:D_SSM + 2 * D_CONV], preferred_element_type=F32)
        zc = v[:, 0:D_CONV] * jax.nn.sigmoid(v[:, D_CONV:])
        for lc in range(N_LC):
            z_scr[lc, pl.ds(pl.multiple_of(HALO + r * SB, SUBLANES), SB), :] = zc[:, lc * LANES:(lc + 1) * LANES]
        return carry

    lax.fori_loop(0, NSB, phase_a3, 0)

    for s in range(N_SLAB):
        lanes = slice(s * LANES, (s + 1) * LANES)
        z = jnp.concatenate(
            [u_scr[:, i, :, lanes].reshape(ROWS_Z, LANES) for i in range(Q)], axis=1).astype(BF16)
        xp = jnp.dot(z, mp_ref[s], preferred_element_type=F32)
        yi_scr[s] = xp[:, 0:Q * LANES]
        xs_scr[s] = xp[:, Q * LANES:]

    half = STATE_LANES // 2
    for s in range(N_SLAB):
        a_re = jnp.broadcast_to(are_ref[s:s + 1, :], (SUBLANES, half))
        a_im = jnp.broadcast_to(aim_ref[s:s + 1, :], (SUBLANES, half))

        def scan_body(k, carry, s=s, a_re=a_re, a_im=a_im):
            s_re, s_im = carry
            rows = pl.ds(pl.multiple_of(k * SUBLANES, SUBLANES), SUBLANES)
            x_re = xs_scr[s, rows, 0:half]
            x_im = xs_scr[s, rows, half:]
            xs_scr[s, rows, 0:half] = s_re
            xs_scr[s, rows, half:] = s_im
            n_re = a_re * s_re - a_im * s_im + x_re
            n_im = a_re * s_im + a_im * s_re + x_im
            return n_re, n_im

        s_re, s_im = lax.fori_loop(0, ROWS_Z // SUBLANES, scan_body,
                                   (s_scr[s, :, 0:half], s_scr[s, :, half:]), unroll=True)
        s_scr[s, :, 0:half] = s_re
        s_scr[s, :, half:] = s_im

    for s in range(N_SLAB):
        lanes = slice(s * LANES, (s + 1) * LANES)
        y_tot = yi_scr[s] + jnp.dot(xs_scr[s].astype(BF16), r_ref[s], preferred_element_type=F32)
        for j in range(Q):
            y_scr[:, j, :, lanes] = y_tot[:, j * LANES:(j + 1) * LANES].reshape(
                ROWS_Z // SUBLANES, SUBLANES, LANES)

    def phase_c1(r, carry):
        rows = sub_rows(r)
        crow = pl.ds(r * CHUNK_ROWS, CHUNK_ROWS)
        y = y_scr[crow].reshape(SB, D_SSM) + d_ref[...] * u_scr[crow].reshape(SB, D_SSM)
        act_scr[rows, 0:D_SSM] = jax.nn.gelu(y).astype(BF16)
        for lc in range(N_LC):
            lanes = slice(lc * LANES, (lc + 1) * LANES)

            def conv_piece(rc, c, lc=lc, lanes=lanes):
                r0 = r * SB + rc * CONV_ROWS
                piece = jnp.broadcast_to(dwb_ref[:, lanes], (CONV_ROWS, LANES))
                for j in range(CONV_WIDTH):
                    zrows = pl.ds(pl.multiple_of(r0 + j * BATCH, SUBLANES), CONV_ROWS)
                    piece = piece + dw_ref[j:j + 1, lanes] * z_scr[lc, zrows, :]
                conv_scr[pl.ds(pl.multiple_of(rc * CONV_ROWS, CONV_ROWS), CONV_ROWS), lanes] = piece
                return c

            lax.fori_loop(0, SB // CONV_ROWS, conv_piece, 0, unroll=4)
        acc = conv_scr[...]
        mu = jnp.mean(acc, axis=-1, keepdims=True)
        cen = acc - mu
        var = jnp.mean(cen * cen, axis=-1, keepdims=True)
        ln = cen * lax.rsqrt(var + EPS) * lng_ref[...] + lnb_ref[...]
        act_scr[rows, D_SSM:] = jax.nn.silu(ln).astype(BF16)
        return carry

    lax.fori_loop(0, NSB, phase_c1, 0)
    z_scr[:, 0:HALO, :] = z_scr[:, TM:TM + HALO, :]

    actb_scr[...] = jnp.dot(permt_ref[...], act_scr[...], preferred_element_type=F32).astype(BF16)

    lane = lax.broadcasted_iota(I32, (1, LANES), 1).astype(F32)
    grp_mask = lane < float(N_GROUPS_MOE)
    exp_lane = (lane >= float(LANE_EXP0)) & (lane < float(LANE_EXP0 + N_EXPERTS))
    lane_grp = jnp.floor((lane - float(LANE_EXP0)) * (1.0 / EXPERTS_PER_GROUP))
    tri = (lax.broadcasted_iota(I32, (SB, SB), 0) > lax.broadcasted_iota(I32, (SB, SB), 1)).astype(BF16)
    neg_inf = float("-inf")
    big = float(4 * LANES)

    def phase_c3(r, carry):
        rows = sub_rows(r)
        h = hb_scr[rows, :]
        g0 = D_SSM + 2 * D_CONV
        gate_ssm = jnp.dot(h, win_ref[:, g0:g0 + D_MODEL], preferred_element_type=F32) \
            + bgate_ref[:, 0:D_MODEL]
        gate_conv = jnp.dot(h, win_ref[:, g0 + D_MODEL:], preferred_element_type=F32) \
            + bgate_ref[:, D_MODEL:]
        zz = jnp.dot(actb_scr[rows, 0:D_SSM], wglu_ref[...], preferred_element_type=F32)
        y_ssm = zz[:, 0:D_MODEL] * jax.nn.sigmoid(zz[:, D_MODEL:])
        y_conv = jnp.dot(actb_scr[rows, D_SSM:], wco_ref[...], preferred_element_type=F32)

        merged = jax.nn.sigmoid(gate_ssm) * y_ssm + jax.nn.sigmoid(gate_conv) * y_conv
        xb = x_ref[pl.ds(r * BPS, BPS)].reshape(SB, D_MODEL)
        x1 = xb + jnp.dot(merged.astype(BF16), wout_ref[...], preferred_element_type=F32)
        x1_ref[pl.ds(r * BPS, BPS)] = x1.reshape(BPS, TT, D_MODEL)

        h2 = _rms(x1, gmoe_ref[...])
        h2p_ref[rows] = _pack_bf16_pair(h2[:, 0:HALF], h2[:, HALF:]).reshape((SB,) + ROW_TILE)

        h2_hi = h2.astype(BF16)
        h2_lo = (h2 - h2_hi.astype(F32)).astype(BF16)
        l1 = jnp.dot(h2_hi, wr1_ref[...], preferred_element_type=F32)
        l2 = jnp.dot(h2_lo, wr2_ref[...], preferred_element_type=F32)
        logits = l1[:, 0:LANES] + l1[:, LANES:] + l2 + br_ref[...]

        lg = jnp.where(grp_mask, logits, neg_inf)
        g_max = jnp.max(lg, axis=-1, keepdims=True)
        g_sel = jnp.min(jnp.where(lg == g_max, lane, big), axis=-1, keepdims=True)
        p_g = 1.0 / jnp.sum(jnp.where(grp_mask, jnp.exp(logits - g_max), 0.0), axis=-1, keepdims=True)
        le = jnp.where(exp_lane & (lane_grp == g_sel), logits, neg_inf)
        m1 = jnp.max(le, axis=-1, keepdims=True)
        i1 = jnp.min(jnp.where(le == m1, lane, big), axis=-1, keepdims=True)
        le2 = jnp.where(lane == i1, neg_inf, le)
        m2 = jnp.max(le2, axis=-1, keepdims=True)
        i2 = jnp.min(jnp.where(le2 == m2, lane, big), axis=-1, keepdims=True)
        e2 = jnp.exp(m2 - m1)
        den = 1.0 + e2
        w_a = (1.0 / den) * p_g
        w_b = (e2 / den) * p_g

        sel1 = lane == i1
        sel2 = lane == i2
        onehot = jnp.where(sel1 | sel2, 1.0, 0.0)
        prefix = jnp.dot(tri, onehot.astype(BF16), preferred_element_type=F32) + cnt_scr[...]
        rank_a = jnp.sum(jnp.where(sel1, prefix, 0.0), axis=-1, keepdims=True)
        rank_b = jnp.sum(jnp.where(sel2, prefix, 0.0), axis=-1, keepdims=True)
        cnt_scr[...] = cnt_scr[...] + jnp.sum(onehot, axis=0, keepdims=True)

        rec = jnp.where(lane == float(REC_EID0), i1 - float(LANE_EXP0), 0.0)
        rec = jnp.where(lane == float(REC_EID1), i2 - float(LANE_EXP0), rec)
        rec = jnp.where(lane == float(REC_W0), w_a, rec)
        rec = jnp.where(lane == float(REC_W1), w_b, rec)
        rec = jnp.where(lane == float(REC_RANK0), rank_a, rec)
        rec = jnp.where(lane == float(REC_RANK1), rank_b, rec)
        rec_ref[rows, :] = rec
        rect_scr[r] = jnp.transpose(rec)[0:REC_ROWS, :]
        return carry

    lax.fori_loop(0, NSB, phase_c3, 0)

    for r in range(NSB):
        rect_ref[:, r * SB:(r + 1) * SB] = rect_scr[r]
    cnt_ref[...] = cnt_scr[...]


def _mixer(x, gmix, win, bgate, perm, permt, mp, rmat, a_re, a_im, dvec, wglu, dw, dwb, lng, lnb, wco,
           wout, gmoe, wr1, wr2, br):
    seq_spec = pl.BlockSpec((BATCH, TT, D_MODEL), lambda i: (0, i, 0))
    in_specs = [
        seq_spec,
        _const_spec((1, D_MODEL)),
        _const_spec((D_MODEL, D_IN)),
        _const_spec((1, 2 * D_MODEL)),
        _const_spec((TM, TM)),
        _const_spec((TM, TM)),
        _const_spec(mp.shape),
        _const_spec(rmat.shape),
        _const_spec(a_re.shape),
        _const_spec(a_im.shape),
        _const_spec((1, D_SSM)),
        _const_spec((D_SSM, 2 * D_MODEL)),
        _const_spec((CONV_WIDTH, D_CONV)),
        _const_spec((1, D_CONV)),
        _const_spec((1, D_CONV)),
        _const_spec((1, D_CONV)),
        _const_spec((D_CONV, D_MODEL)),
        _const_spec((D_MODEL, D_MODEL)),
        _const_spec((1, D_MODEL)),
        _const_spec((D_MODEL, 2 * LANES)),
        _const_spec((D_MODEL, LANES)),
        _const_spec((1, LANES)),
    ]
    out_specs = [
        seq_spec,
        pl.BlockSpec((TM,) + ROW_TILE, lambda i: (i, 0, 0)),
        pl.BlockSpec((TM, LANES), lambda i: (i, 0)),
        pl.BlockSpec((REC_ROWS, TM), lambda i: (0, i)),
        pl.BlockSpec((1, LANES), lambda i: (0, 0)),
    ]
    out_shape = [
        jax.ShapeDtypeStruct((BATCH, SEQ, D_MODEL), F32),
        jax.ShapeDtypeStruct((N_TOK,) + ROW_TILE, U32),
        jax.ShapeDtypeStruct((N_TOK, LANES), F32),
        jax.ShapeDtypeStruct((REC_ROWS, N_TOK), F32),
        jax.ShapeDtypeStruct((1, LANES), F32),
    ]
    chunk_shape = (ROWS_Z // SUBLANES, Q, SUBLANES, D_SSM)
    scratch = [
        pltpu.VMEM((TM, D_MODEL), BF16),
        pltpu.VMEM((TM, D_MODEL), BF16),
        pltpu.VMEM(chunk_shape, F32),
        pltpu.VMEM(chunk_shape, F32),
        pltpu.VMEM((N_SLAB, ROWS_Z, Q * LANES), F32),
        pltpu.VMEM((N_SLAB, ROWS_Z, STATE_LANES), F32),
        pltpu.VMEM((N_LC, HALO + TM, LANES), F32),
        pltpu.VMEM((SB, D_CONV), F32),
        pltpu.VMEM((TM, D_SSM + D_CONV), BF16),
        pltpu.VMEM((TM, D_SSM + D_CONV), BF16),
        pltpu.VMEM((NSB, REC_ROWS, SB), F32),
        pltpu.VMEM((N_SLAB, SUBLANES, STATE_LANES), F32),
        pltpu.VMEM((1, LANES), F32),
    ]
    return pl.pallas_call(
        _mixer_kernel,
        grid=(N_STEP,),
        in_specs=in_specs,
        out_specs=out_specs,
        out_shape=out_shape,
        scratch_shapes=scratch,
        compiler_params=pltpu.CompilerParams(
            dimension_semantics=("arbitrary",), vmem_limit_bytes=VMEM_LIMIT),
        name="mixer",
    )(x, gmix, win, bgate, perm, permt, mp, rmat, a_re, a_im, dvec, wglu, dw, dwb, lng, lnb, wco, wout,
      gmoe, wr1, wr2, br)


def _cmul(a, b):
    return a[0] * b[0] - a[1] * b[1], a[0] * b[1] + a[1] * b[0]


def _ssm_matrices(a_re, a_im, log_dt, b_re, b_im, c_re, c_im):
    dt = jnp.exp(log_dt)[:, None]
    mag = jnp.exp(a_re * dt)
    lam = (mag * jnp.cos(a_im * dt), mag * jnp.sin(a_im * dt))
    den = a_re * a_re + a_im * a_im
    nr = lam[0] - 1.0
    ni = lam[1]
    z_re = (nr * a_re + ni * a_im) / den
    z_im = (ni * a_re - nr * a_im) / den
    bbar = (z_re[..., None] * b_re - z_im[..., None] * b_im,
            z_re[..., None] * b_im + z_im[..., None] * b_re)
    pw = [(jnp.ones_like(lam[0]), jnp.zeros_like(lam[0])), lam]
    for _ in range(2, Q + 1):
        pw.append(_cmul(pw[-1], lam))
    e = [(c_re * p[0][:, None, :] - c_im * p[1][:, None, :],
          c_re * p[1][:, None, :] + c_im * p[0][:, None, :]) for p in pw]
    hp = lax.Precision.HIGHEST
    k = [jnp.einsum('gcn,gnd->gcd', e[m][0], bbar[0], precision=hp)
         - jnp.einsum('gcn,gnd->gcd', e[m][1], bbar[1], precision=hp) for m in range(Q)]
    eye = jnp.eye(GROUPS_PER_SLAB, dtype=F32)
    split = lambda t: t.reshape((N_SLAB, GROUPS_PER_SLAB) + t.shape[1:])
    zero_k = jnp.zeros_like(k[0])
    kb = jnp.stack([jnp.stack([split(jnp.swapaxes(k[j - i] if j >= i else zero_k, 1, 2))
                               for j in range(Q)]) for i in range(Q)])
    m_mat = jnp.einsum('ijsgdc,gh->sigdjhc', kb, eye).reshape(N_SLAB, Q * LANES, Q * LANES)
    f = [_cmul((pw[Q - 1 - i][0][..., None], pw[Q - 1 - i][1][..., None]), bbar) for i in range(Q)]
    p_parts = []
    for part in range(2):
        fs = jnp.stack([split(f[i][part]) for i in range(Q)])
        p_parts.append(jnp.einsum('isgnd,gh->sigdhn', fs, eye).reshape(N_SLAB, Q * LANES, STATE_LANES // 2))
    p_mat = jnp.concatenate(p_parts, axis=-1)
    r_parts = []
    for part, sign in ((0, 1.0), (1, -1.0)):
        es = jnp.stack([split(e[j + 1][part]) for j in range(Q)])
        r_parts.append(sign * jnp.einsum('jsgcn,gh->shnjgc', es, eye).reshape(
            N_SLAB, STATE_LANES // 2, Q * LANES))
    r_mat = jnp.concatenate(r_parts, axis=1)
    mp = jnp.concatenate([m_mat, p_mat], axis=-1).astype(BF16)
    a_q = pw[Q]
    return (mp, r_mat.astype(BF16),
            a_q[0].reshape(N_SLAB, STATE_LANES // 2), a_q[1].reshape(N_SLAB, STATE_LANES // 2))


def _router_weights(w_rg, b_rg, w_re, b_re):
    pad_g = LANE_EXP0 - LANE_GRP0 - N_GROUPS_MOE
    pad_e = LANES - LANE_EXP0 - N_EXPERTS
    w = jnp.concatenate([w_rg, jnp.zeros((D_MODEL, pad_g), F32), w_re, jnp.zeros((D_MODEL, pad_e), F32)], axis=1)
    b = jnp.concatenate([b_rg, jnp.zeros((pad_g,), F32), b_re, jnp.zeros((pad_e,), F32)]).reshape(1, LANES)
    w_hi = w.astype(BF16)
    w_lo = (w - w_hi.astype(F32)).astype(BF16)
    return jnp.concatenate([w_hi, w_lo], axis=1), w_hi, b


def _time_major_permutation():
    tm = jnp.arange(TM, dtype=I32)
    src = (tm % BATCH) * TT + tm // BATCH
    perm = (src[:, None] == jnp.arange(TM, dtype=I32)[None, :]).astype(BF16)
    return perm, perm.T


def _sc_mesh():
    return plsc.VectorSubcoreMesh(core_axis_name="core", subcore_axis_name="subcore")


def _sc_worker(mesh):
    return lax.axis_index("core") * mesh.num_subcores + lax.axis_index("subcore")


def _dispatch(h2p, dest):
    mesh = _sc_mesh()
    n_win = N_TOK // SC_WINDOW
    per_worker = n_win // (mesh.num_cores * mesh.num_subcores)
    assert per_worker * mesh.num_cores * mesh.num_subcores == n_win

    @pl.kernel(out_type=jax.ShapeDtypeStruct((N_ROWS,) + ROW_TILE, U32), mesh=mesh,
               scratch_types=[pltpu.VMEM((SC_WINDOW,), I32), pltpu.VMEM((SC_WINDOW,) + ROW_TILE, U32)])
    def scatter_rows(h_hbm, dest_hbm, xs_hbm, idx_v, rows_v):
        first = _sc_worker(mesh) * per_worker

        @pl.loop(0, per_worker)
        def _(w):
            win = first + w
            pltpu.sync_copy(h_hbm.at[pl.ds(win * SC_WINDOW, SC_WINDOW)], rows_v)
            for j in range(TOPK):
                pltpu.sync_copy(dest_hbm.at[j, win], idx_v)
                pltpu.sync_copy(rows_v, xs_hbm.at[idx_v])

    return scatter_rows(h2p, dest)


def _collect(ys, dest):
    mesh = _sc_mesh()
    n_tok = dest.shape[1]
    n_win = TOPK * n_tok // SC_WINDOW
    per_worker = n_win // (mesh.num_cores * mesh.num_subcores)
    assert per_worker * mesh.num_cores * mesh.num_subcores == n_win

    @pl.kernel(out_type=jax.ShapeDtypeStruct((TOPK * n_tok,) + ROW_TILE, U32), mesh=mesh,
               scratch_types=[pltpu.VMEM((SC_WINDOW,), I32), pltpu.VMEM((SC_WINDOW,) + ROW_TILE, U32)])
    def gather_rows(ys_hbm, dest_hbm, yg_hbm, idx_v, rows_v):
        first = _sc_worker(mesh) * per_worker

        @pl.loop(0, per_worker)
        def _(w):
            win = first + w
            pltpu.sync_copy(dest_hbm.at[win], idx_v)
            pltpu.sync_copy(ys_hbm.at[idx_v], rows_v)
            pltpu.sync_copy(rows_v, yg_hbm.at[pl.ds(win * SC_WINDOW, SC_WINDOW)])

    return gather_rows(ys, dest.reshape(n_win, SC_WINDOW)).reshape((TOPK, n_tok) + ROW_TILE)


def _expert_kernel(first_ref, nblk_ref, nvalid_ref, nused_ref, xs_hbm, wg_ref, wu_ref, wd_ref, ys_hbm,
                   wg_scr, wu_scr, wd_scr, x_buf, y_buf, in_sem, out_sem):
    e = pl.program_id(0)
    nused = nused_ref[0]

    def in_copy(g):
        slot = lax.rem(g, IN_SLOTS)
        return pltpu.make_async_copy(xs_hbm.at[pl.ds(g * BM, BM)], x_buf.at[slot], in_sem.at[slot])

    def out_copy(g, slot):
        return pltpu.make_async_copy(y_buf.at[slot], ys_hbm.at[pl.ds(g * BM, BM)], out_sem.at[slot])

    @pl.when(e == 0)
    def _first():
        for g in range(IN_AHEAD):
            in_copy(g).start()

    wg_scr[...] = wg_ref[0].astype(BF16)
    wu_scr[...] = wu_ref[0].astype(BF16)
    wd_scr[...] = wd_ref[0].astype(BF16)

    def block(b, carry):
        g = first_ref[e] + b
        slot = lax.rem(g, 2)
        in_copy(g).wait()

        @pl.when(g + IN_AHEAD < nused)
        def _prefetch():
            in_copy(g + IN_AHEAD).start()

        @pl.when(g >= 2)
        def _slot_free():
            out_copy(g - 2, slot).wait()

        valid = lax.broadcasted_iota(I32, (BM, 1), 0) < nvalid_ref[g]
        x_blk = x_buf[lax.rem(g, IN_SLOTS)].reshape(BM, HALF)
        lo, hi = _unpack_bf16_pair(jnp.where(valid, x_blk, jnp.uint32(0)))
        lo = lo.astype(BF16)
        hi = hi.astype(BF16)
        gate = jnp.dot(lo, wg_scr[0:HALF, :], preferred_element_type=F32) \
            + jnp.dot(hi, wg_scr[HALF:, :], preferred_element_type=F32)
        up = jnp.dot(lo, wu_scr[0:HALF, :], preferred_element_type=F32) \
            + jnp.dot(hi, wu_scr[HALF:, :], preferred_element_type=F32)
        act = (jax.nn.silu(gate) * up).astype(BF16)
        o = jnp.dot(act, wd_scr[...], preferred_element_type=F32)
        y_buf[slot] = _pack_bf16_pair(o[:, 0:HALF], o[:, HALF:]).reshape((BM,) + ROW_TILE)
        out_copy(g, slot).start()
        return carry

    lax.fori_loop(0, nblk_ref[e], block, 0)

    @pl.when(e == N_EXPERTS - 1)
    def _drain():
        out_copy(nused - 2, lax.rem(nused, 2)).wait()
        out_copy(nused - 1, 1 - lax.rem(nused, 2)).wait()


def _experts(first, nblk, nvalid, nused, xs, wg, wu, wd):
    grid_spec = pltpu.PrefetchScalarGridSpec(
        num_scalar_prefetch=4,
        grid=(N_EXPERTS,),
        in_specs=[
            pl.BlockSpec(memory_space=pl.ANY),
            pl.BlockSpec((1, D_MODEL, D_EXPERT), lambda e, *_: (e, 0, 0)),
            pl.BlockSpec((1, D_MODEL, D_EXPERT), lambda e, *_: (e, 0, 0)),
            pl.BlockSpec((1, D_EXPERT, D_MODEL), lambda e, *_: (e, 0, 0)),
        ],
        out_specs=pl.BlockSpec(memory_space=pl.ANY),
        scratch_shapes=[
            pltpu.VMEM((D_MODEL, D_EXPERT), BF16),
            pltpu.VMEM((D_MODEL, D_EXPERT), BF16),
            pltpu.VMEM((D_EXPERT, D_MODEL), BF16),
            pltpu.VMEM((IN_SLOTS, BM) + ROW_TILE, U32),
            pltpu.VMEM((2, BM) + ROW_TILE, U32),
            pltpu.SemaphoreType.DMA((IN_SLOTS,)),
            pltpu.SemaphoreType.DMA((2,)),
        ],
    )
    return pl.pallas_call(
        _expert_kernel,
        grid_spec=grid_spec,
        out_shape=jax.ShapeDtypeStruct((N_ROWS,) + ROW_TILE, U32),
        compiler_params=pltpu.CompilerParams(
            dimension_semantics=("arbitrary",), vmem_limit_bytes=VMEM_LIMIT),
        name="experts",
    )(first, nblk, nvalid, nused, xs, wg, wu, wd)


def _combine_kernel(x1_ref, rec_ref, yg_ref, p_ref, gple_ref, wpg_ref, wple_ref, gfin_ref, *rest):
    out_ref = rest[-1]
    ple = jnp.dot(p_ref[0].reshape(TM, D_PLE).astype(BF16), wple_ref[...], preferred_element_type=F32)
    rec = rec_ref[...]
    w0 = rec[:, REC_W0:REC_W0 + 1]
    w1 = rec[:, REC_W1:REC_W1 + 1]
    lo0, hi0 = _unpack_bf16_pair(yg_ref[0].reshape(TM, HALF))
    lo1, hi1 = _unpack_bf16_pair(yg_ref[1].reshape(TM, HALF))
    moe = jnp.concatenate([lo0 * w0 + lo1 * w1, hi0 * w0 + hi1 * w1], axis=1)
    x2 = x1_ref[...].reshape(TM, D_MODEL) + moe
    gate = jax.nn.sigmoid(jnp.dot(_rms(x2, gple_ref[...]).astype(BF16), wpg_ref[...],
                                  preferred_element_type=F32))
    x3 = x2 + gate * ple
    out_ref[...] = _rms(x3, gfin_ref[...]).reshape(BATCH, TT, D_MODEL)


def _combine(part, x1, rec, yg, p, gple, wpg, wple, gfin, out_prev=None):
    s0 = part * PART_STEPS
    seq_spec = pl.BlockSpec((BATCH, TT, D_MODEL), lambda i: (0, s0 + i, 0))
    in_specs = [
        seq_spec,
        pl.BlockSpec((TM, LANES), lambda i: (s0 + i, 0)),
        pl.BlockSpec((TOPK, TM) + ROW_TILE, lambda i: (0, i, 0, 0)),
        pl.BlockSpec((1, BATCH, TT, D_PLE), lambda i: (0, 0, s0 + i, 0)),
        _const_spec((1, D_MODEL)),
        _const_spec((D_MODEL, D_MODEL)),
        _const_spec((D_PLE, D_MODEL)),
        _const_spec((1, D_MODEL)),
    ]
    args = [x1, rec, yg, p, gple, wpg, wple, gfin]
    aliases = {}
    if out_prev is not None:
        in_specs.append(pl.BlockSpec(memory_space=pl.ANY))
        args.append(out_prev)
        aliases = {len(args) - 1: 0}
    return pl.pallas_call(
        _combine_kernel,
        grid=(PART_STEPS,),
        in_specs=in_specs,
        out_specs=seq_spec,
        out_shape=jax.ShapeDtypeStruct((BATCH, SEQ, D_MODEL), F32),
        input_output_aliases=aliases,
        compiler_params=pltpu.CompilerParams(
            dimension_semantics=("arbitrary",), vmem_limit_bytes=VMEM_LIMIT),
        name="combine",
    )(*args)


def kernel(x, p, g_mix, w_in, b_gate, ssm_a_re, ssm_a_im, ssm_log_dt, ssm_b_re, ssm_b_im, ssm_c_re,
           ssm_c_im, ssm_d, w_glu, conv_dw, conv_dw_b, conv_ln_g, conv_ln_b, w_conv_out, w_out, g_moe,
           w_router_group, b_router_group, w_router_expert, b_router_expert, w_exp_gate, w_exp_up,
           w_exp_down, g_ple, w_ple_gate, w_ple, g_final):
    assert x.shape == (BATCH, SEQ, D_MODEL) and p.shape == (1, BATCH, SEQ, D_PLE)
    row = lambda v: v.reshape(1, -1)

    mp, rmat, a_re, a_im = _ssm_matrices(ssm_a_re[0], ssm_a_im[0], ssm_log_dt[0], ssm_b_re[0],
                                         ssm_b_im[0], ssm_c_re[0], ssm_c_im[0])
    wr1, wr2, br = _router_weights(w_router_group[0], b_router_group[0], w_router_expert[0],
                                   b_router_expert[0])
    perm, permt = _time_major_permutation()
    x1, h2p, rec, rect, cnt = _mixer(
        x, row(g_mix[0]), w_in[0].astype(BF16), row(b_gate[0]), perm, permt, mp, rmat, a_re, a_im,
        row(ssm_d[0]), w_glu[0].astype(BF16), conv_dw[0], row(conv_dw_b[0]), row(conv_ln_g[0]),
        row(conv_ln_b[0]), w_conv_out[0].astype(BF16), w_out[0].astype(BF16), row(g_moe[0]), wr1, wr2, br)

    counts = cnt[0, LANE_EXP0:LANE_EXP0 + N_EXPERTS].astype(I32)
    pcounts = (counts + BM - 1) // BM * BM
    pends = jnp.cumsum(pcounts)
    pstarts = pends - pcounts
    eid = rect[REC_EID0:REC_EID1 + 1].astype(I32)
    rank = rect[REC_RANK0:REC_RANK1 + 1].astype(I32)
    dest = (jnp.sum(jnp.where(eid[..., None] == jnp.arange(N_EXPERTS, dtype=I32), pstarts, 0), axis=-1)
            + rank).reshape(TOPK, N_TOK // SC_WINDOW, SC_WINDOW)
    nused = (pends[-1] // BM).astype(I32)
    blk = jnp.arange(N_BLK, dtype=I32)[:, None] * BM
    in_expert = (pstarts[None, :] <= blk) & (blk < pends[None, :])
    nvalid = jnp.clip(jnp.sum(jnp.where(in_expert, (pstarts + counts)[None, :] - blk, 0), axis=1), 0, BM)

    xs = _dispatch(h2p, dest)
    ys = _experts(pstarts // BM, pcounts // BM, nvalid.astype(I32), nused.reshape(1), xs,
                  w_exp_gate[0], w_exp_up[0], w_exp_down[0])
    dest_tok = dest.reshape(TOPK, N_TOK)
    wpg = w_ple_gate[0].astype(BF16)
    wple = w_ple[0].astype(BF16)
    out = None
    for part in range(N_PARTS):
        tok = slice(part * PART_STEPS * TM, (part + 1) * PART_STEPS * TM)
        yg = _collect(ys, dest_tok[:, tok])
        out = _combine(part, x1, rec, yg, p, row(g_ple[0]), wpg, wple, row(g_final), out)
    return out
```

```python
import jax
import jax.numpy as jnp
from jax import lax
from jax.experimental import pallas as pl
from jax.experimental.pallas import tpu as pltpu
from jax.experimental.pallas import tpu_sc as plsc

F32 = jnp.float32
BF16 = jnp.bfloat16
U32 = jnp.uint32
I32 = jnp.int32

D_MODEL = 1024
BATCH = 8
SEQ = 2048
N_TOK = BATCH * SEQ
D_SSM = 512
SSM_GROUP_WIDTH = 16
SSM_GROUPS = 32
SSM_STATE = 64
D_CONV = 512
CONV_WIDTH = 31
D_IN = D_SSM + 2 * D_CONV + 2 * D_MODEL
N_GROUPS_MOE = 4
EXPERTS_PER_GROUP = 8
N_EXPERTS = 32
TOPK = 2
D_EXPERT = 512
D_PLE = 256
EPS = 1e-6

SUBLANES = 8
LANES = 128
assert BATCH == SUBLANES

TT = 64
TM = TT * BATCH
N_STEP = SEQ // TT
SB = 512
NSB = TM // SB
BPS = SB // TT
Q = 2
N_SLAB = D_SSM // LANES
GROUPS_PER_SLAB = SSM_GROUPS // N_SLAB
ROWS_Z = TM // Q
STATE_LANES = 2 * GROUPS_PER_SLAB * SSM_STATE
HALO = (CONV_WIDTH - 1) * BATCH
CHUNK_ROWS = SB // (Q * SUBLANES)
CONV_ROWS = 64
N_LC = D_CONV // LANES

LANE_GRP0 = 0
LANE_EXP0 = 32
REC_EID0, REC_EID1, REC_W0, REC_W1, REC_RANK0, REC_RANK1 = 0, 1, 2, 3, 4, 5
REC_ROWS = 8

BM = 256
N_BLK = (TOPK * N_TOK + N_EXPERTS * (BM - 1) + BM - 1) // BM
N_ROWS = N_BLK * BM
HALF = D_MODEL // 2
ROW_TILE = (HALF // LANES, LANES)
SC_WINDOW = 64
IN_AHEAD = 3
IN_SLOTS = IN_AHEAD + 1
N_PARTS = 2
PART_STEPS = N_STEP // N_PARTS

VMEM_LIMIT = 56 * 1024 * 1024


def _const_spec(shape):
    n = len(shape)
    return pl.BlockSpec(shape, lambda *_: (0,) * n, pipeline_mode=pl.Buffered(1))


def _rms(x, g):
    ms = jnp.mean(x * x, axis=-1, keepdims=True)
    return x * lax.rsqrt(ms + EPS) * g


def _pack_bf16_pair(lo, hi):
    ulo = lax.bitcast_convert_type(lo.astype(BF16).astype(F32), U32)
    uhi = lax.bitcast_convert_type(hi.astype(BF16).astype(F32), U32)
    return (ulo >> 16) | (uhi & jnp.uint32(0xFFFF0000))


def _unpack_bf16_pair(w):
    lo = lax.bitcast_convert_type(w << 16, F32)
    hi = lax.bitcast_convert_type(w & jnp.uint32(0xFFFF0000), F32)
    return lo, hi


def _mixer_kernel(x_ref, gmix_ref, win_ref, bgate_ref, perm_ref, permt_ref, mp_ref, r_ref, are_ref,
                  aim_ref, d_ref, wglu_ref, dw_ref, dwb_ref, lng_ref, lnb_ref, wco_ref, wout_ref,
                  gmoe_ref, wr1_ref, wr2_ref, br_ref,
                  x1_ref, h2p_ref, rec_ref, rect_ref, cnt_ref,
                  hb_scr, ht_scr, u_scr, y_scr, yi_scr, xs_scr, z_scr, conv_scr, act_scr, actb_scr,
                  logit_scr, s_scr, cnt_scr):
    step = pl.program_id(0)
    assert NSB == 1

    @pl.when(step == 0)
    def _init():
        logit_scr[...] = jnp.zeros(logit_scr.shape, F32)
        z_scr[:, 0:HALO, :] = jnp.zeros((N_LC, HALO, LANES), F32)
        s_scr[...] = jnp.zeros(s_scr.shape, F32)
        cnt_scr[...] = jnp.zeros(cnt_scr.shape, F32)

    def sub_rows(r):
        return pl.ds(pl.multiple_of(r * SB, SB), SB)

    def phase_a(r, carry):
        xb = x_ref[pl.ds(r * BPS, BPS)].reshape(SB, D_MODEL)
        hb_scr[sub_rows(r), :] = _rms(xb, gmix_ref[...]).astype(BF16)
        return carry

    def phase_a3(r, carry):
        h = ht_scr[sub_rows(r), :]
        u = jnp.dot(h, win_ref[:, 0:D_SSM], preferred_element_type=F32)
        u_scr[pl.ds(r * CHUNK_ROWS, CHUNK_ROWS)] = u.reshape(CHUNK_ROWS, Q, SUBLANES, D_SSM)
        v = jnp.dot(h, win_ref[:, D_SSM:D_SSM + 2 * D_CONV], preferred_element_type=F32)
        zc = v[:, 0:D_CONV] * jax.nn.sigmoid(v[:, D_CONV:])
        for lc in range(N_LC):
            z_scr[lc, pl.ds(pl.multiple_of(HALO + r * SB, SUBLANES), SB), :] = zc[:, lc * LANES:(lc + 1) * LANES]
        return carry

    def phase_b():
        for s in range(N_SLAB):
            lanes = slice(s * LANES, (s + 1) * LANES)
            z = jnp.concatenate(
                [u_scr[:, i, :, lanes].reshape(ROWS_Z, LANES) for i in range(Q)], axis=1).astype(BF16)
            xp = jnp.dot(z, mp_ref[s], preferred_element_type=F32)
            yi_scr[s] = xp[:, 0:Q * LANES]
            xs_scr[s] = xp[:, Q * LANES:]

        half = STATE_LANES // 2
        for s in range(N_SLAB):
            a_re = jnp.broadcast_to(are_ref[s:s + 1, :], (SUBLANES, half))
            a_im = jnp.broadcast_to(aim_ref[s:s + 1, :], (SUBLANES, half))

            def scan_body(k, carry, s=s, a_re=a_re, a_im=a_im):
                s_re, s_im = carry
                rows = pl.ds(pl.multiple_of(k * SUBLANES, SUBLANES), SUBLANES)
                x_re = xs_scr[s, rows, 0:half]
                x_im = xs_scr[s, rows, half:]
                xs_scr[s, rows, 0:half] = s_re
                xs_scr[s, rows, half:] = s_im
                n_re = a_re * s_re - a_im * s_im + x_re
                n_im = a_re * s_im + a_im * s_re + x_im
                return n_re, n_im

            s_re, s_im = lax.fori_loop(0, ROWS_Z // SUBLANES, scan_body,
                                       (s_scr[s, :, 0:half], s_scr[s, :, half:]), unroll=True)
            s_scr[s, :, 0:half] = s_re
            s_scr[s, :, half:] = s_im

        for s in range(N_SLAB):
            lanes = slice(s * LANES, (s + 1) * LANES)
            y_tot = yi_scr[s] + jnp.dot(xs_scr[s].astype(BF16), r_ref[s], preferred_element_type=F32)
            for j in range(Q):
                y_scr[:, j, :, lanes] = y_tot[:, j * LANES:(j + 1) * LANES].reshape(
                    ROWS_Z // SUBLANES, SUBLANES, LANES)

    def phase_c1(r, carry):
        rows = sub_rows(r)
        crow = pl.ds(r * CHUNK_ROWS, CHUNK_ROWS)
        y = y_scr[crow].reshape(SB, D_SSM) + d_ref[...] * u_scr[crow].reshape(SB, D_SSM)
        act_scr[rows, 0:D_SSM] = jax.nn.gelu(y).astype(BF16)
        for lc in range(N_LC):
            lanes = slice(lc * LANES, (lc + 1) * LANES)

            def conv_piece(rc, c, lc=lc, lanes=lanes):
                r0 = r * SB + rc * CONV_ROWS
                piece = jnp.broadcast_to(dwb_ref[:, lanes], (CONV_ROWS, LANES))
                for j in range(CONV_WIDTH):
                    zrows = pl.ds(pl.multiple_of(r0 + j * BATCH, SUBLANES), CONV_ROWS)
                    piece = piece + dw_ref[j:j + 1, lanes] * z_scr[lc, zrows, :]
                conv_scr[pl.ds(pl.multiple_of(rc * CONV_ROWS, CONV_ROWS), CONV_ROWS), lanes] = piece
                return c

            lax.fori_loop(0, SB // CONV_ROWS, conv_piece, 0, unroll=4)
        acc = conv_scr[...]
        mu = jnp.mean(acc, axis=-1, keepdims=True)
        cen = acc - mu
        var = jnp.mean(cen * cen, axis=-1, keepdims=True)
        ln = cen * lax.rsqrt(var + EPS) * lng_ref[...] + lnb_ref[...]
        act_scr[rows, D_SSM:] = jax.nn.silu(ln).astype(BF16)
        return carry

    lane = lax.broadcasted_iota(I32, (1, LANES), 1).astype(F32)
    grp_mask = lane < float(N_GROUPS_MOE)
    exp_lane = (lane >= float(LANE_EXP0)) & (lane < float(LANE_EXP0 + N_EXPERTS))
    lane_grp = jnp.floor((lane - float(LANE_EXP0)) * (1.0 / EXPERTS_PER_GROUP))
    tri = (lax.broadcasted_iota(I32, (SB, SB), 0) > lax.broadcasted_iota(I32, (SB, SB), 1)).astype(BF16)
    neg_inf = float("-inf")
    big = float(4 * LANES)

    def phase_c3(r, carry):
        rows = sub_rows(r)
        h = hb_scr[rows, :]
        g0 = D_SSM + 2 * D_CONV
        gate_ssm = jnp.dot(h, win_ref[:, g0:g0 + D_MODEL], preferred_element_type=F32) \
            + bgate_ref[:, 0:D_MODEL]
        gate_conv = jnp.dot(h, win_ref[:, g0 + D_MODEL:], preferred_element_type=F32) \
            + bgate_ref[:, D_MODEL:]
        zz = jnp.dot(actb_scr[rows, 0:D_SSM], wglu_ref[...], preferred_element_type=F32)
        y_ssm = zz[:, 0:D_MODEL] * jax.nn.sigmoid(zz[:, D_MODEL:])
        y_conv = jnp.dot(actb_scr[rows, D_SSM:], wco_ref[...], preferred_element_type=F32)

        merged = jax.nn.sigmoid(gate_ssm) * y_ssm + jax.nn.sigmoid(gate_conv) * y_conv
        xb = x_ref[pl.ds(r * BPS, BPS)].reshape(SB, D_MODEL)
        x1 = xb + jnp.dot(merged.astype(BF16), wout_ref[...], preferred_element_type=F32)
        x1_ref[pl.ds(r * BPS, BPS)] = x1.reshape(BPS, TT, D_MODEL)

        h2 = _rms(x1, gmoe_ref[...])
        h2p_ref[rows] = _pack_bf16_pair(h2[:, 0:HALF], h2[:, HALF:]).reshape((SB,) + ROW_TILE)

        h2_hi = h2.astype(BF16)
        h2_lo = (h2 - h2_hi.astype(F32)).astype(BF16)
        l1 = jnp.dot(h2_hi, wr1_ref[...], preferred_element_type=F32)
        l2 = jnp.dot(h2_lo, wr2_ref[...], preferred_element_type=F32)
        logit_scr[rows, :] = l1[:, 0:LANES] + l1[:, LANES:] + l2 + br_ref[...]
        return carry

    def route_previous():
        rows = sub_rows(0)
        logits = logit_scr[...]
        counted = jnp.where(step > 0, 1.0, 0.0)

        lg = jnp.where(grp_mask, logits, neg_inf)
        g_max = jnp.max(lg, axis=-1, keepdims=True)
        g_sel = jnp.min(jnp.where(lg == g_max, lane, big), axis=-1, keepdims=True)
        p_g = 1.0 / jnp.sum(jnp.where(grp_mask, jnp.exp(logits - g_max), 0.0), axis=-1, keepdims=True)
        le = jnp.where(exp_lane & (lane_grp == g_sel), logits, neg_inf)
        m1 = jnp.max(le, axis=-1, keepdims=True)
        i1 = jnp.min(jnp.where(le == m1, lane, big), axis=-1, keepdims=True)
        le2 = jnp.where(lane == i1, neg_inf, le)
        m2 = jnp.max(le2, axis=-1, keepdims=True)
        i2 = jnp.min(jnp.where(le2 == m2, lane, big), axis=-1, keepdims=True)
        e2 = jnp.exp(m2 - m1)
        den = 1.0 + e2
        w_a = (1.0 / den) * p_g
        w_b = (e2 / den) * p_g

        sel1 = lane == i1
        sel2 = lane == i2
        onehot = jnp.where(sel1 | sel2, counted, 0.0)
        prefix = jnp.dot(tri, onehot.astype(BF16), preferred_element_type=F32) + cnt_scr[...]
        rank_a = jnp.sum(jnp.where(sel1, prefix, 0.0), axis=-1, keepdims=True)
        rank_b = jnp.sum(jnp.where(sel2, prefix, 0.0), axis=-1, keepdims=True)
        cnt_scr[...] = cnt_scr[...] + jnp.sum(onehot, axis=0, keepdims=True)

        rec = jnp.where(lane == float(REC_EID0), i1 - float(LANE_EXP0), 0.0)
        rec = jnp.where(lane == float(REC_EID1), i2 - float(LANE_EXP0), rec)
        rec = jnp.where(lane == float(REC_W0), w_a, rec)
        rec = jnp.where(lane == float(REC_W1), w_b, rec)
        rec = jnp.where(lane == float(REC_RANK0), rank_a, rec)
        rec = jnp.where(lane == float(REC_RANK1), rank_b, rec)
        rec_ref[rows, :] = rec
        rect_ref[...] = jnp.transpose(rec)[0:REC_ROWS, :]
        cnt_ref[...] = cnt_scr[...]

    @pl.when(step < N_STEP)
    def _tile():
        route_previous()
        phase_a(0, 0)
        ht_scr[...] = jnp.dot(perm_ref[...], hb_scr[...], preferred_element_type=F32).astype(BF16)
        phase_a3(0, 0)
        phase_b()
        phase_c1(0, 0)
        z_scr[:, 0:HALO, :] = z_scr[:, TM:TM + HALO, :]
        actb_scr[...] = jnp.dot(permt_ref[...], act_scr[...], preferred_element_type=F32).astype(BF16)
        phase_c3(0, 0)

    @pl.when(step == N_STEP)
    def _last():
        route_previous()


def _mixer(x, gmix, win, bgate, perm, permt, mp, rmat, a_re, a_im, dvec, wglu, dw, dwb, lng, lnb, wco,
           wout, gmoe, wr1, wr2, br):
    tile = lambda i: jnp.minimum(i, N_STEP - 1)
    routed = lambda i: jnp.maximum(i - 1, 0)
    seq_spec = pl.BlockSpec((BATCH, TT, D_MODEL), lambda i: (0, tile(i), 0))
    in_specs = [
        seq_spec,
        _const_spec((1, D_MODEL)),
        _const_spec((D_MODEL, D_IN)),
        _const_spec((1, 2 * D_MODEL)),
        _const_spec((TM, TM)),
        _const_spec((TM, TM)),
        _const_spec(mp.shape),
        _const_spec(rmat.shape),
        _const_spec(a_re.shape),
        _const_spec(a_im.shape),
        _const_spec((1, D_SSM)),
        _const_spec((D_SSM, 2 * D_MODEL)),
        _const_spec((CONV_WIDTH, D_CONV)),
        _const_spec((1, D_CONV)),
        _const_spec((1, D_CONV)),
        _const_spec((1, D_CONV)),
        _const_spec((D_CONV, D_MODEL)),
        _const_spec((D_MODEL, D_MODEL)),
        _const_spec((1, D_MODEL)),
        _const_spec((D_MODEL, 2 * LANES)),
        _const_spec((D_MODEL, LANES)),
        _const_spec((1, LANES)),
    ]
    out_specs = [
        seq_spec,
        pl.BlockSpec((TM,) + ROW_TILE, lambda i: (tile(i), 0, 0)),
        pl.BlockSpec((TM, LANES), lambda i: (routed(i), 0)),
        pl.BlockSpec((REC_ROWS, TM), lambda i: (0, routed(i))),
        pl.BlockSpec((1, LANES), lambda i: (0, 0)),
    ]
    out_shape = [
        jax.ShapeDtypeStruct((BATCH, SEQ, D_MODEL), F32),
        jax.ShapeDtypeStruct((N_TOK,) + ROW_TILE, U32),
        jax.ShapeDtypeStruct((N_TOK, LANES), F32),
        jax.ShapeDtypeStruct((REC_ROWS, N_TOK), F32),
        jax.ShapeDtypeStruct((1, LANES), F32),
    ]
    chunk_shape = (ROWS_Z // SUBLANES, Q, SUBLANES, D_SSM)
    scratch = [
        pltpu.VMEM((TM, D_MODEL), BF16),
        pltpu.VMEM((TM, D_MODEL), BF16),
        pltpu.VMEM(chunk_shape, F32),
        pltpu.VMEM(chunk_shape, F32),
        pltpu.VMEM((N_SLAB, ROWS_Z, Q * LANES), F32),
        pltpu.VMEM((N_SLAB, ROWS_Z, STATE_LANES), F32),
        pltpu.VMEM((N_LC, HALO + TM, LANES), F32),
        pltpu.VMEM((SB, D_CONV), F32),
        pltpu.VMEM((TM, D_SSM + D_CONV), BF16),
        pltpu.VMEM((TM, D_SSM + D_CONV), BF16),
        pltpu.VMEM((TM, LANES), F32),
        pltpu.VMEM((N_SLAB, SUBLANES, STATE_LANES), F32),
        pltpu.VMEM((1, LANES), F32),
    ]
    return pl.pallas_call(
        _mixer_kernel,
        grid=(N_STEP + 1,),
        in_specs=in_specs,
        out_specs=out_specs,
        out_shape=out_shape,
        scratch_shapes=scratch,
        compiler_params=pltpu.CompilerParams(
            dimension_semantics=("arbitrary",), vmem_limit_bytes=VMEM_LIMIT),
        name="mixer",
    )(x, gmix, win, bgate, perm, permt, mp, rmat, a_re, a_im, dvec, wglu, dw, dwb, lng, lnb, wco, wout,
      gmoe, wr1, wr2, br)


def _cmul(a, b):
    return a[0] * b[0] - a[1] * b[1], a[0] * b[1] + a[1] * b[0]


def _ssm_matrices(a_re, a_im, log_dt, b_re, b_im, c_re, c_im):
    dt = jnp.exp(log_dt)[:, None]
    mag = jnp.exp(a_re * dt)
    lam = (mag * jnp.cos(a_im * dt), mag * jnp.sin(a_im * dt))
    den = a_re * a_re + a_im * a_im
    nr = lam[0] - 1.0
    ni = lam[1]
    z_re = (nr * a_re + ni * a_im) / den
    z_im = (ni * a_re - nr * a_im) / den
    bbar = (z_re[..., None] * b_re - z_im[..., None] * b_im,
            z_re[..., None] * b_im + z_im[..., None] * b_re)
    pw = [(jnp.ones_like(lam[0]), jnp.zeros_like(lam[0])), lam]
    for _ in range(2, Q + 1):
        pw.append(_cmul(pw[-1], lam))
    e = [(c_re * p[0][:, None, :] - c_im * p[1][:, None, :],
          c_re * p[1][:, None, :] + c_im * p[0][:, None, :]) for p in pw]
    hp = lax.Precision.HIGHEST
    k = [jnp.einsum('gcn,gnd->gcd', e[m][0], bbar[0], precision=hp)
         - jnp.einsum('gcn,gnd->gcd', e[m][1], bbar[1], precision=hp) for m in range(Q)]
    eye = jnp.eye(GROUPS_PER_SLAB, dtype=F32)
    split = lambda t: t.reshape((N_SLAB, GROUPS_PER_SLAB) + t.shape[1:])
    zero_k = jnp.zeros_like(k[0])
    kb = jnp.stack([jnp.stack([split(jnp.swapaxes(k[j - i] if j >= i else zero_k, 1, 2))
                               for j in range(Q)]) for i in range(Q)])
    m_mat = jnp.einsum('ijsgdc,gh->sigdjhc', kb, eye).reshape(N_SLAB, Q * LANES, Q * LANES)
    f = [_cmul((pw[Q - 1 - i][0][..., None], pw[Q - 1 - i][1][..., None]), bbar) for i in range(Q)]
    p_parts = []
    for part in range(2):
        fs = jnp.stack([split(f[i][part]) for i in range(Q)])
        p_parts.append(jnp.einsum('isgnd,gh->sigdhn', fs, eye).reshape(N_SLAB, Q * LANES, STATE_LANES // 2))
    p_mat = jnp.concatenate(p_parts, axis=-1)
    r_parts = []
    for part, sign in ((0, 1.0), (1, -1.0)):
        es = jnp.stack([split(e[j + 1][part]) for j in range(Q)])
        r_parts.append(sign * jnp.einsum('jsgcn,gh->shnjgc', es, eye).reshape(
            N_SLAB, STATE_LANES // 2, Q * LANES))
    r_mat = jnp.concatenate(r_parts, axis=1)
    mp = jnp.concatenate([m_mat, p_mat], axis=-1).astype(BF16)
    a_q = pw[Q]
    return (mp, r_mat.astype(BF16),
            a_q[0].reshape(N_SLAB, STATE_LANES // 2), a_q[1].reshape(N_SLAB, STATE_LANES // 2))


def _router_weights(w_rg, b_rg, w_re, b_re):
    pad_g = LANE_EXP0 - LANE_GRP0 - N_GROUPS_MOE
    pad_e = LANES - LANE_EXP0 - N_EXPERTS
    w = jnp.concatenate([w_rg, jnp.zeros((D_MODEL, pad_g), F32), w_re, jnp.zeros((D_MODEL, pad_e), F32)], axis=1)
    b = jnp.concatenate([b_rg, jnp.zeros((pad_g,), F32), b_re, jnp.zeros((pad_e,), F32)]).reshape(1, LANES)
    w_hi = w.astype(BF16)
    w_lo = (w - w_hi.astype(F32)).astype(BF16)
    return jnp.concatenate([w_hi, w_lo], axis=1), w_hi, b


def _time_major_permutation():
    tm = jnp.arange(TM, dtype=I32)
    src = (tm % BATCH) * TT + tm // BATCH
    perm = (src[:, None] == jnp.arange(TM, dtype=I32)[None, :]).astype(BF16)
    return perm, perm.T


def _sc_mesh():
    return plsc.VectorSubcoreMesh(core_axis_name="core", subcore_axis_name="subcore")


def _sc_worker(mesh):
    return lax.axis_index("core") * mesh.num_subcores + lax.axis_index("subcore")


def _dispatch(h2p, dest):
    mesh = _sc_mesh()
    n_win = N_TOK // SC_WINDOW
    per_worker = n_win // (mesh.num_cores * mesh.num_subcores)
    assert per_worker * mesh.num_cores * mesh.num_subcores == n_win

    @pl.kernel(out_type=jax.ShapeDtypeStruct((N_ROWS,) + ROW_TILE, U32), mesh=mesh,
               scratch_types=[pltpu.VMEM((SC_WINDOW,), I32), pltpu.VMEM((SC_WINDOW,) + ROW_TILE, U32)])
    def scatter_rows(h_hbm, dest_hbm, xs_hbm, idx_v, rows_v):
        first = _sc_worker(mesh) * per_worker

        @pl.loop(0, per_worker)
        def _(w):
            win = first + w
            pltpu.sync_copy(h_hbm.at[pl.ds(win * SC_WINDOW, SC_WINDOW)], rows_v)
            for j in range(TOPK):
                pltpu.sync_copy(dest_hbm.at[j, win], idx_v)
                pltpu.sync_copy(rows_v, xs_hbm.at[idx_v])

    return scatter_rows(h2p, dest)


def _collect(ys, dest):
    mesh = _sc_mesh()
    n_tok = dest.shape[1]
    n_win = TOPK * n_tok // SC_WINDOW
    per_worker = n_win // (mesh.num_cores * mesh.num_subcores)
    assert per_worker * mesh.num_cores * mesh.num_subcores == n_win

    @pl.kernel(out_type=jax.ShapeDtypeStruct((TOPK * n_tok,) + ROW_TILE, U32), mesh=mesh,
               scratch_types=[pltpu.VMEM((SC_WINDOW,), I32), pltpu.VMEM((SC_WINDOW,) + ROW_TILE, U32)])
    def gather_rows(ys_hbm, dest_hbm, yg_hbm, idx_v, rows_v):
        first = _sc_worker(mesh) * per_worker

        @pl.loop(0, per_worker)
        def _(w):
            win = first + w
            pltpu.sync_copy(dest_hbm.at[win], idx_v)
            pltpu.sync_copy(ys_hbm.at[idx_v], rows_v)
            pltpu.sync_copy(rows_v, yg_hbm.at[pl.ds(win * SC_WINDOW, SC_WINDOW)])

    return gather_rows(ys, dest.reshape(n_win, SC_WINDOW)).reshape((TOPK, n_tok) + ROW_TILE)


def _expert_kernel(first_ref, nblk_ref, nvalid_ref, nused_ref, xs_hbm, wg_ref, wu_ref, wd_ref, ys_hbm,
                   wg_scr, wu_scr, wd_scr, x_buf, y_buf, in_sem, out_sem):
    e = pl.program_id(0)
    nused = nused_ref[0]

    def in_copy(g):
        slot = lax.rem(g, IN_SLOTS)
        return pltpu.make_async_copy(xs_hbm.at[pl.ds(g * BM, BM)], x_buf.at[slot], in_sem.at[slot])

    def out_copy(g, slot):
        return pltpu.make_async_copy(y_buf.at[slot], ys_hbm.at[pl.ds(g * BM, BM)], out_sem.at[slot])

    @pl.when(e == 0)
    def _first():
        for g in range(IN_AHEAD):
            in_copy(g).start()

    wg_scr[...] = wg_ref[0].astype(BF16)
    wu_scr[...] = wu_ref[0].astype(BF16)
    wd_scr[...] = wd_ref[0].astype(BF16)

    def block(b, carry):
        g = first_ref[e] + b
        slot = lax.rem(g, 2)
        in_copy(g).wait()

        @pl.when(g + IN_AHEAD < nused)
        def _prefetch():
            in_copy(g + IN_AHEAD).start()

        @pl.when(g >= 2)
        def _slot_free():
            out_copy(g - 2, slot).wait()

        valid = lax.broadcasted_iota(I32, (BM, 1), 0) < nvalid_ref[g]
        x_blk = x_buf[lax.rem(g, IN_SLOTS)].reshape(BM, HALF)
        lo, hi = _unpack_bf16_pair(jnp.where(valid, x_blk, jnp.uint32(0)))
        lo = lo.astype(BF16)
        hi = hi.astype(BF16)
        gate = jnp.dot(lo, wg_scr[0:HALF, :], preferred_element_type=F32) \
            + jnp.dot(hi, wg_scr[HALF:, :], preferred_element_type=F32)
        up = jnp.dot(lo, wu_scr[0:HALF, :], preferred_element_type=F32) \
            + jnp.dot(hi, wu_scr[HALF:, :], preferred_element_type=F32)
        act = (jax.nn.silu(gate) * up).astype(BF16)
        o = jnp.dot(act, wd_scr[...], preferred_element_type=F32)
        y_buf[slot] = _pack_bf16_pair(o[:, 0:HALF], o[:, HALF:]).reshape((BM,) + ROW_TILE)
        out_copy(g, slot).start()
        return carry

    lax.fori_loop(0, nblk_ref[e], block, 0)

    @pl.when(e == N_EXPERTS - 1)
    def _drain():
        out_copy(nused - 2, lax.rem(nused, 2)).wait()
        out_copy(nused - 1, 1 - lax.rem(nused, 2)).wait()


def _experts(first, nblk, nvalid, nused, xs, wg, wu, wd):
    grid_spec = pltpu.PrefetchScalarGridSpec(
        num_scalar_prefetch=4,
        grid=(N_EXPERTS,),
        in_specs=[
            pl.BlockSpec(memory_space=pl.ANY),
            pl.BlockSpec((1, D_MODEL, D_EXPERT), lambda e, *_: (e, 0, 0)),
            pl.BlockSpec((1, D_MODEL, D_EXPERT), lambda e, *_: (e, 0, 0)),
            pl.BlockSpec((1, D_EXPERT, D_MODEL), lambda e, *_: (e, 0, 0)),
        ],
        out_specs=pl.BlockSpec(memory_space=pl.ANY),
        scratch_shapes=[
            pltpu.VMEM((D_MODEL, D_EXPERT), BF16),
            pltpu.VMEM((D_MODEL, D_EXPERT), BF16),
            pltpu.VMEM((D_EXPERT, D_MODEL), BF16),
            pltpu.VMEM((IN_SLOTS, BM) + ROW_TILE, U32),
            pltpu.VMEM((2, BM) + ROW_TILE, U32),
            pltpu.SemaphoreType.DMA((IN_SLOTS,)),
            pltpu.SemaphoreType.DMA((2,)),
        ],
    )
    return pl.pallas_call(
        _expert_kernel,
        grid_spec=grid_spec,
        out_shape=jax.ShapeDtypeStruct((N_ROWS,) + ROW_TILE, U32),
        compiler_params=pltpu.CompilerParams(
            dimension_semantics=("arbitrary",), vmem_limit_bytes=VMEM_LIMIT),
        name="experts",
    )(first, nblk, nvalid, nused, xs, wg, wu, wd)


def _combine_kernel(x1_ref, rec_ref, yg_ref, p_ref, gple_ref, wpg_ref, wple_ref, gfin_ref, *rest):
    out_ref = rest[-1]
    ple = jnp.dot(p_ref[0].reshape(TM, D_PLE).astype(BF16), wple_ref[...], preferred_element_type=F32)
    rec = rec_ref[...]
    w0 = rec[:, REC_W0:REC_W0 + 1]
    w1 = rec[:, REC_W1:REC_W1 + 1]
    lo0, hi0 = _unpack_bf16_pair(yg_ref[0].reshape(TM, HALF))
    lo1, hi1 = _unpack_bf16_pair(yg_ref[1].reshape(TM, HALF))
    moe = jnp.concatenate([lo0 * w0 + lo1 * w1, hi0 * w0 + hi1 * w1], axis=1)
    x2 = x1_ref[...].reshape(TM, D_MODEL) + moe
    gate = jax.nn.sigmoid(jnp.dot(_rms(x2, gple_ref[...]).astype(BF16), wpg_ref[...],
                                  preferred_element_type=F32))
    x3 = x2 + gate * ple
    out_ref[...] = _rms(x3, gfin_ref[...]).reshape(BATCH, TT, D_MODEL)


def _combine(part, x1, rec, yg, p, gple, wpg, wple, gfin, out_prev=None):
    s0 = part * PART_STEPS
    seq_spec = pl.BlockSpec((BATCH, TT, D_MODEL), lambda i: (0, s0 + i, 0))
    in_specs = [
        seq_spec,
        pl.BlockSpec((TM, LANES), lambda i: (s0 + i, 0)),
        pl.BlockSpec((TOPK, TM) + ROW_TILE, lambda i: (0, i, 0, 0)),
        pl.BlockSpec((1, BATCH, TT, D_PLE), lambda i: (0, 0, s0 + i, 0)),
        _const_spec((1, D_MODEL)),
        _const_spec((D_MODEL, D_MODEL)),
        _const_spec((D_PLE, D_MODEL)),
        _const_spec((1, D_MODEL)),
    ]
    args = [x1, rec, yg, p, gple, wpg, wple, gfin]
    aliases = {}
    if out_prev is not None:
        in_specs.append(pl.BlockSpec(memory_space=pl.ANY))
        args.append(out_prev)
        aliases = {len(args) - 1: 0}
    return pl.pallas_call(
        _combine_kernel,
        grid=(PART_STEPS,),
        in_specs=in_specs,
        out_specs=seq_spec,
        out_shape=jax.ShapeDtypeStruct((BATCH, SEQ, D_MODEL), F32),
        input_output_aliases=aliases,
        compiler_params=pltpu.CompilerParams(
            dimension_semantics=("arbitrary",), vmem_limit_bytes=VMEM_LIMIT),
        name="combine",
    )(*args)


def kernel(x, p, g_mix, w_in, b_gate, ssm_a_re, ssm_a_im, ssm_log_dt, ssm_b_re, ssm_b_im, ssm_c_re,
           ssm_c_im, ssm_d, w_glu, conv_dw, conv_dw_b, conv_ln_g, conv_ln_b, w_conv_out, w_out, g_moe,
           w_router_group, b_router_group, w_router_expert, b_router_expert, w_exp_gate, w_exp_up,
           w_exp_down, g_ple, w_ple_gate, w_ple, g_final):
    assert x.shape == (BATCH, SEQ, D_MODEL) and p.shape == (1, BATCH, SEQ, D_PLE)
    row = lambda v: v.reshape(1, -1)

    mp, rmat, a_re, a_im = _ssm_matrices(ssm_a_re[0], ssm_a_im[0], ssm_log_dt[0], ssm_b_re[0],
                                         ssm_b_im[0], ssm_c_re[0], ssm_c_im[0])
    wr1, wr2, br = _router_weights(w_router_group[0], b_router_group[0], w_router_expert[0],
                                   b_router_expert[0])
    perm, permt = _time_major_permutation()
    x1, h2p, rec, rect, cnt = _mixer(
        x, row(g_mix[0]), w_in[0].astype(BF16), row(b_gate[0]), perm, permt, mp, rmat, a_re, a_im,
        row(ssm_d[0]), w_glu[0].astype(BF16), conv_dw[0], row(conv_dw_b[0]), row(conv_ln_g[0]),
        row(conv_ln_b[0]), w_conv_out[0].astype(BF16), w_out[0].astype(BF16), row(g_moe[0]), wr1, wr2, br)

    counts = cnt[0, LANE_EXP0:LANE_EXP0 + N_EXPERTS].astype(I32)
    pcounts = (counts + BM - 1) // BM * BM
    pends = jnp.cumsum(pcounts)
    pstarts = pends - pcounts
    eid = rect[REC_EID0:REC_EID1 + 1].astype(I32)
    rank = rect[REC_RANK0:REC_RANK1 + 1].astype(I32)
    dest = (jnp.sum(jnp.where(eid[..., None] == jnp.arange(N_EXPERTS, dtype=I32), pstarts, 0), axis=-1)
            + rank).reshape(TOPK, N_TOK // SC_WINDOW, SC_WINDOW)
    nused = (pends[-1] // BM).astype(I32)
    blk = jnp.arange(N_BLK, dtype=I32)[:, None] * BM
    in_expert = (pstarts[None, :] <= blk) & (blk < pends[None, :])
    nvalid = jnp.clip(jnp.sum(jnp.where(in_expert, (pstarts + counts)[None, :] - blk, 0), axis=1), 0, BM)

    xs = _dispatch(h2p, dest)
    ys = _experts(pstarts // BM, pcounts // BM, nvalid.astype(I32), nused.reshape(1), xs,
                  w_exp_gate[0], w_exp_up[0], w_exp_down[0])
    dest_tok = dest.reshape(TOPK, N_TOK)
    wpg = w_ple_gate[0].astype(BF16)
    wple = w_ple[0].astype(BF16)
    out = None
    for part in range(N_PARTS):
        tok = slice(part * PART_STEPS * TM, (part + 1) * PART_STEPS * TM)
        yg = _collect(ys, dest_tok[:, tok])
        out = _combine(part, x1, rec, yg, p, row(g_ple[0]), wpg, wple, row(g_final), out)
    return out
```

```python
import jax
import jax.numpy as jnp
from jax import lax
from jax.experimental import pallas as pl
from jax.experimental.pallas import tpu as pltpu
from jax.experimental.pallas import tpu_sc as plsc

F32 = jnp.float32
BF16 = jnp.bfloat16
U32 = jnp.uint32
I32 = jnp.int32

D_MODEL = 1024
BATCH = 8
SEQ = 2048
N_TOK = BATCH * SEQ
D_SSM = 512
SSM_GROUP_WIDTH = 16
SSM_GROUPS = 32
SSM_STATE = 64
D_CONV = 512
CONV_WIDTH = 31
D_IN = D_SSM + 2 * D_CONV + 2 * D_MODEL
N_GROUPS_MOE = 4
EXPERTS_PER_GROUP = 8
N_EXPERTS = 32
TOPK = 2
D_EXPERT = 512
D_PLE = 256
EPS = 1e-6

SUBLANES = 8
LANES = 128
assert BATCH == SUBLANES

TT = 64
TM = TT * BATCH
N_STEP = SEQ // TT
SB = 512
NSB = TM // SB
BPS = SB // TT
Q = 2
N_SLAB = D_SSM // LANES
GROUPS_PER_SLAB = SSM_GROUPS // N_SLAB
ROWS_Z = TM // Q
STATE_LANES = 2 * GROUPS_PER_SLAB * SSM_STATE
HALO = (CONV_WIDTH - 1) * BATCH
CHUNK_ROWS = SB // (Q * SUBLANES)
CONV_ROWS = 64
N_LC = D_CONV // LANES
CONV_UNROLL = 4
CONV_STEPS = SB // CONV_ROWS // CONV_UNROLL
N_GATE_BLK = N_LC * CONV_STEPS
GATE_COLS = 2 * D_MODEL // N_GATE_BLK
assert CONV_STEPS * CONV_UNROLL * CONV_ROWS == SB and N_GATE_BLK * GATE_COLS == 2 * D_MODEL and N_GATE_BLK % 2 == 0

LANE_GRP0 = 0
LANE_EXP0 = 32
REC_EID0, REC_EID1, REC_W0, REC_W1, REC_RANK0, REC_RANK1 = 0, 1, 2, 3, 4, 5
REC_ROWS = 8

BM = 256
N_BLK = (TOPK * N_TOK + N_EXPERTS * (BM - 1) + BM - 1) // BM
N_ROWS = N_BLK * BM
HALF = D_MODEL // 2
ROW_TILE = (HALF // LANES, LANES)
SC_WINDOW = 64
IN_AHEAD = 3
IN_SLOTS = IN_AHEAD + 1
N_PARTS = 2
PART_STEPS = N_STEP // N_PARTS

VMEM_LIMIT = 56 * 1024 * 1024


def _const_spec(shape):
    n = len(shape)
    return pl.BlockSpec(shape, lambda *_: (0,) * n, pipeline_mode=pl.Buffered(1))


def _rms(x, g):
    ms = jnp.mean(x * x, axis=-1, keepdims=True)
    return x * lax.rsqrt(ms + EPS) * g


def _pack_bf16_pair(lo, hi):
    ulo = lax.bitcast_convert_type(lo.astype(BF16).astype(F32), U32)
    uhi = lax.bitcast_convert_type(hi.astype(BF16).astype(F32), U32)
    return (ulo >> 16) | (uhi & jnp.uint32(0xFFFF0000))


def _unpack_bf16_pair(w):
    lo = lax.bitcast_convert_type(w << 16, F32)
    hi = lax.bitcast_convert_type(w & jnp.uint32(0xFFFF0000), F32)
    return lo, hi


def _mixer_kernel(x_ref, gmix_ref, win_ref, wing_ref, bgate_ref, perm_ref, permt_ref, mp_ref, r_ref,
                  are_ref, aim_ref, d_ref, wglu_ref, dw_ref, dwb_ref, lng_ref, lnb_ref, wco_ref, wout_ref,
                  gmoe_ref, wr1_ref, wr2_ref, br_ref,
                  x1_ref, h2p_ref, rec_ref, rect_ref, cnt_ref,
                  hb_scr, ht_scr, u_scr, y_scr, yi_scr, xs_scr, z_scr, conv_scr, graw_scr, act_scr,
                  actb_scr, logit_scr, s_scr, cnt_scr):
    step = pl.program_id(0)
    assert NSB == 1

    @pl.when(step == 0)
    def _init():
        logit_scr[...] = jnp.zeros(logit_scr.shape, F32)
        z_scr[:, 0:HALO, :] = jnp.zeros((N_LC, HALO, LANES), F32)
        s_scr[...] = jnp.zeros(s_scr.shape, F32)
        cnt_scr[...] = jnp.zeros(cnt_scr.shape, F32)

    def sub_rows(r):
        return pl.ds(pl.multiple_of(r * SB, SB), SB)

    def phase_a(r, carry):
        xb = x_ref[pl.ds(r * BPS, BPS)].reshape(SB, D_MODEL)
        hb_scr[sub_rows(r), :] = _rms(xb, gmix_ref[...]).astype(BF16)
        return carry

    def phase_a3(r, carry):
        h = ht_scr[sub_rows(r), :]
        u = jnp.dot(h, win_ref[:, 0:D_SSM], preferred_element_type=F32)
        u_scr[pl.ds(r * CHUNK_ROWS, CHUNK_ROWS)] = u.reshape(CHUNK_ROWS, Q, SUBLANES, D_SSM)
        v = jnp.dot(h, win_ref[:, D_SSM:D_SSM + 2 * D_CONV], preferred_element_type=F32)
        zc = v[:, 0:D_CONV] * jax.nn.sigmoid(v[:, D_CONV:])
        for lc in range(N_LC):
            z_scr[lc, pl.ds(pl.multiple_of(HALO + r * SB, SUBLANES), SB), :] = zc[:, lc * LANES:(lc + 1) * LANES]
        return carry

    def phase_b():
        for s in range(N_SLAB):
            lanes = slice(s * LANES, (s + 1) * LANES)
            z = jnp.concatenate(
                [u_scr[:, i, :, lanes].reshape(ROWS_Z, LANES) for i in range(Q)], axis=1).astype(BF16)
            xp = jnp.dot(z, mp_ref[s], preferred_element_type=F32)
            yi_scr[s] = xp[:, 0:Q * LANES]
            xs_scr[s] = xp[:, Q * LANES:]

        half = STATE_LANES // 2
        for s in range(N_SLAB):
            a_re = jnp.broadcast_to(are_ref[s:s + 1, :], (SUBLANES, half))
            a_im = jnp.broadcast_to(aim_ref[s:s + 1, :], (SUBLANES, half))

            def scan_body(k, carry, s=s, a_re=a_re, a_im=a_im):
                s_re, s_im = carry
                rows = pl.ds(pl.multiple_of(k * SUBLANES, SUBLANES), SUBLANES)
                x_re = xs_scr[s, rows, 0:half]
                x_im = xs_scr[s, rows, half:]
                xs_scr[s, rows, 0:half] = s_re
                xs_scr[s, rows, half:] = s_im
                n_re = a_re * s_re - a_im * s_im + x_re
                n_im = a_re * s_im + a_im * s_re + x_im
                return n_re, n_im

            s_re, s_im = lax.fori_loop(0, ROWS_Z // SUBLANES, scan_body,
                                       (s_scr[s, :, 0:half], s_scr[s, :, half:]), unroll=True)
            s_scr[s, :, 0:half] = s_re
            s_scr[s, :, half:] = s_im

        for s in range(N_SLAB):
            lanes = slice(s * LANES, (s + 1) * LANES)
            y_tot = yi_scr[s] + jnp.dot(xs_scr[s].astype(BF16), r_ref[s], preferred_element_type=F32)
            for j in range(Q):
                y_scr[:, j, :, lanes] = y_tot[:, j * LANES:(j + 1) * LANES].reshape(
                    ROWS_Z // SUBLANES, SUBLANES, LANES)

    def phase_c1(r, carry):
        rows = sub_rows(r)
        crow = pl.ds(r * CHUNK_ROWS, CHUNK_ROWS)
        y = y_scr[crow].reshape(SB, D_SSM) + d_ref[...] * u_scr[crow].reshape(SB, D_SSM)
        act_scr[rows, 0:D_SSM] = jax.nn.gelu(y).astype(BF16)
        for lc in range(N_LC):
            lanes = slice(lc * LANES, (lc + 1) * LANES)

            def conv_pieces(it, c, lc=lc, lanes=lanes):
                for part in range(CONV_UNROLL):
                    rc = it * CONV_UNROLL + part
                    r0 = r * SB + rc * CONV_ROWS
                    piece = jnp.broadcast_to(dwb_ref[:, lanes], (CONV_ROWS, LANES))
                    for j in range(CONV_WIDTH):
                        zrows = pl.ds(pl.multiple_of(r0 + j * BATCH, SUBLANES), CONV_ROWS)
                        piece = piece + dw_ref[j:j + 1, lanes] * z_scr[lc, zrows, :]
                    conv_scr[pl.ds(pl.multiple_of(rc * CONV_ROWS, CONV_ROWS), CONV_ROWS), lanes] = piece
                n = lc * CONV_STEPS + it
                for half_rows in (slice(0, TM // 2), slice(TM // 2, TM)):
                    graw_scr[n, half_rows, :] = jnp.dot(hb_scr[half_rows, :], wing_ref[n],
                                                        preferred_element_type=F32)
                return c

            lax.fori_loop(0, CONV_STEPS, conv_pieces, 0)
        acc = conv_scr[...]
        mu = jnp.mean(acc, axis=-1, keepdims=True)
        cen = acc - mu
        var = jnp.mean(cen * cen, axis=-1, keepdims=True)
        ln = cen * lax.rsqrt(var + EPS) * lng_ref[...] + lnb_ref[...]
        act_scr[rows, D_SSM:] = jax.nn.silu(ln).astype(BF16)
        return carry

    lane = lax.broadcasted_iota(I32, (1, LANES), 1).astype(F32)
    grp_mask = lane < float(N_GROUPS_MOE)
    exp_lane = (lane >= float(LANE_EXP0)) & (lane < float(LANE_EXP0 + N_EXPERTS))
    lane_grp = jnp.floor((lane - float(LANE_EXP0)) * (1.0 / EXPERTS_PER_GROUP))
    tri = (lax.broadcasted_iota(I32, (SB, SB), 0) > lax.broadcasted_iota(I32, (SB, SB), 1)).astype(BF16)
    neg_inf = float("-inf")
    big = float(4 * LANES)

    def phase_c3(r, carry):
        rows = sub_rows(r)
        half_blk = N_GATE_BLK // 2
        gate_ssm = jnp.concatenate([graw_scr[n] for n in range(half_blk)], axis=1) \
            + bgate_ref[:, 0:D_MODEL]
        gate_conv = jnp.concatenate([graw_scr[n] for n in range(half_blk, N_GATE_BLK)], axis=1) \
            + bgate_ref[:, D_MODEL:]
        zz = jnp.dot(actb_scr[rows, 0:D_SSM], wglu_ref[...], preferred_element_type=F32)
        y_ssm = zz[:, 0:D_MODEL] * jax.nn.sigmoid(zz[:, D_MODEL:])
        y_conv = jnp.dot(actb_scr[rows, D_SSM:], wco_ref[...], preferred_element_type=F32)

        merged = jax.nn.sigmoid(gate_ssm) * y_ssm + jax.nn.sigmoid(gate_conv) * y_conv
        xb = x_ref[pl.ds(r * BPS, BPS)].reshape(SB, D_MODEL)
        x1 = xb + jnp.dot(merged.astype(BF16), wout_ref[...], preferred_element_type=F32)
        x1_ref[pl.ds(r * BPS, BPS)] = x1.reshape(BPS, TT, D_MODEL)

        h2 = _rms(x1, gmoe_ref[...])
        h2p_ref[rows] = _pack_bf16_pair(h2[:, 0:HALF], h2[:, HALF:]).reshape((SB,) + ROW_TILE)

        h2_hi = h2.astype(BF16)
        h2_lo = (h2 - h2_hi.astype(F32)).astype(BF16)
        l1 = jnp.dot(h2_hi, wr1_ref[...], preferred_element_type=F32)
        l2 = jnp.dot(h2_lo, wr2_ref[...], preferred_element_type=F32)
        logit_scr[rows, :] = l1[:, 0:LANES] + l1[:, LANES:] + l2 + br_ref[...]
        return carry

    def route_previous():
        rows = sub_rows(0)
        logits = logit_scr[...]
        counted = jnp.where(step > 0, 1.0, 0.0)

        lg = jnp.where(grp_mask, logits, neg_inf)
        g_max = jnp.max(lg, axis=-1, keepdims=True)
        g_sel = jnp.min(jnp.where(lg == g_max, lane, big), axis=-1, keepdims=True)
        p_g = 1.0 / jnp.sum(jnp.where(grp_mask, jnp.exp(logits - g_max), 0.0), axis=-1, keepdims=True)
        le = jnp.where(exp_lane & (lane_grp == g_sel), logits, neg_inf)
        m1 = jnp.max(le, axis=-1, keepdims=True)
        i1 = jnp.min(jnp.where(le == m1, lane, big), axis=-1, keepdims=True)
        le2 = jnp.where(lane == i1, neg_inf, le)
        m2 = jnp.max(le2, axis=-1, keepdims=True)
        i2 = jnp.min(jnp.where(le2 == m2, lane, big), axis=-1, keepdims=True)
        e2 = jnp.exp(m2 - m1)
        den = 1.0 + e2
        w_a = (1.0 / den) * p_g
        w_b = (e2 / den) * p_g

        sel1 = lane == i1
        sel2 = lane == i2
        onehot = jnp.where(sel1 | sel2, counted, 0.0)
        prefix = jnp.dot(tri, onehot.astype(BF16), preferred_element_type=F32) + cnt_scr[...]
        rank_a = jnp.sum(jnp.where(sel1, prefix, 0.0), axis=-1, keepdims=True)
        rank_b = jnp.sum(jnp.where(sel2, prefix, 0.0), axis=-1, keepdims=True)
        cnt_scr[...] = cnt_scr[...] + jnp.sum(onehot, axis=0, keepdims=True)

        rec = jnp.where(lane == float(REC_EID0), i1 - float(LANE_EXP0), 0.0)
        rec = jnp.where(lane == float(REC_EID1), i2 - float(LANE_EXP0), rec)
        rec = jnp.where(lane == float(REC_W0), w_a, rec)
        rec = jnp.where(lane == float(REC_W1), w_b, rec)
        rec = jnp.where(lane == float(REC_RANK0), rank_a, rec)
        rec = jnp.where(lane == float(REC_RANK1), rank_b, rec)
        rec_ref[rows, :] = rec
        rect_ref[...] = jnp.transpose(rec)[0:REC_ROWS, :]
        cnt_ref[...] = cnt_scr[...]

    @pl.when(step < N_STEP)
    def _tile():
        route_previous()
        phase_a(0, 0)
        ht_scr[...] = jnp.dot(perm_ref[...], hb_scr[...], preferred_element_type=F32).astype(BF16)
        phase_a3(0, 0)
        phase_b()
        phase_c1(0, 0)
        z_scr[:, 0:HALO, :] = z_scr[:, TM:TM + HALO, :]
        actb_scr[...] = jnp.dot(permt_ref[...], act_scr[...], preferred_element_type=F32).astype(BF16)
        phase_c3(0, 0)

    @pl.when(step == N_STEP)
    def _last():
        route_previous()


def _mixer(x, gmix, win, wing, bgate, perm, permt, mp, rmat, a_re, a_im, dvec, wglu, dw, dwb, lng, lnb, wco,
           wout, gmoe, wr1, wr2, br):
    tile = lambda i: jnp.minimum(i, N_STEP - 1)
    routed = lambda i: jnp.maximum(i - 1, 0)
    seq_spec = pl.BlockSpec((BATCH, TT, D_MODEL), lambda i: (0, tile(i), 0))
    in_specs = [
        seq_spec,
        _const_spec((1, D_MODEL)),
        _const_spec((D_MODEL, D_SSM + 2 * D_CONV)),
        _const_spec((N_GATE_BLK, D_MODEL, GATE_COLS)),
        _const_spec((1, 2 * D_MODEL)),
        _const_spec((TM, TM)),
        _const_spec((TM, TM)),
        _const_spec(mp.shape),
        _const_spec(rmat.shape),
        _const_spec(a_re.shape),
        _const_spec(a_im.shape),
        _const_spec((1, D_SSM)),
        _const_spec((D_SSM, 2 * D_MODEL)),
        _const_spec((CONV_WIDTH, D_CONV)),
        _const_spec((1, D_CONV)),
        _const_spec((1, D_CONV)),
        _const_spec((1, D_CONV)),
        _const_spec((D_CONV, D_MODEL)),
        _const_spec((D_MODEL, D_MODEL)),
        _const_spec((1, D_MODEL)),
        _const_spec((D_MODEL, 2 * LANES)),
        _const_spec((D_MODEL, LANES)),
        _const_spec((1, LANES)),
    ]
    out_specs = [
        seq_spec,
        pl.BlockSpec((TM,) + ROW_TILE, lambda i: (tile(i), 0, 0)),
        pl.BlockSpec((TM, LANES), lambda i: (routed(i), 0)),
        pl.BlockSpec((REC_ROWS, TM), lambda i: (0, routed(i))),
        pl.BlockSpec((1, LANES), lambda i: (0, 0)),
    ]
    out_shape = [
        jax.ShapeDtypeStruct((BATCH, SEQ, D_MODEL), F32),
        jax.ShapeDtypeStruct((N_TOK,) + ROW_TILE, U32),
        jax.ShapeDtypeStruct((N_TOK, LANES), F32),
        jax.ShapeDtypeStruct((REC_ROWS, N_TOK), F32),
        jax.ShapeDtypeStruct((1, LANES), F32),
    ]
    chunk_shape = (ROWS_Z // SUBLANES, Q, SUBLANES, D_SSM)
    scratch = [
        pltpu.VMEM((TM, D_MODEL), BF16),
        pltpu.VMEM((TM, D_MODEL), BF16),
        pltpu.VMEM(chunk_shape, F32),
        pltpu.VMEM(chunk_shape, F32),
        pltpu.VMEM((N_SLAB, ROWS_Z, Q * LANES), F32),
        pltpu.VMEM((N_SLAB, ROWS_Z, STATE_LANES), F32),
        pltpu.VMEM((N_LC, HALO + TM, LANES), F32),
        pltpu.VMEM((SB, D_CONV), F32),
        pltpu.VMEM((N_GATE_BLK, TM, GATE_COLS), F32),
        pltpu.VMEM((TM, D_SSM + D_CONV), BF16),
        pltpu.VMEM((TM, D_SSM + D_CONV), BF16),
        pltpu.VMEM((TM, LANES), F32),
        pltpu.VMEM((N_SLAB, SUBLANES, STATE_LANES), F32),
        pltpu.VMEM((1, LANES), F32),
    ]
    return pl.pallas_call(
        _mixer_kernel,
        grid=(N_STEP + 1,),
        in_specs=in_specs,
        out_specs=out_specs,
        out_shape=out_shape,
        scratch_shapes=scratch,
        compiler_params=pltpu.CompilerParams(
            dimension_semantics=("arbitrary",), vmem_limit_bytes=VMEM_LIMIT),
        name="mixer",
    )(x, gmix, win, wing, bgate, perm, permt, mp, rmat, a_re, a_im, dvec, wglu, dw, dwb, lng, lnb, wco, wout,
      gmoe, wr1, wr2, br)


def _cmul(a, b):
    return a[0] * b[0] - a[1] * b[1], a[0] * b[1] + a[1] * b[0]


def _ssm_matrices(a_re, a_im, log_dt, b_re, b_im, c_re, c_im):
    dt = jnp.exp(log_dt)[:, None]
    mag = jnp.exp(a_re * dt)
    lam = (mag * jnp.cos(a_im * dt), mag * jnp.sin(a_im * dt))
    den = a_re * a_re + a_im * a_im
    nr = lam[0] - 1.0
    ni = lam[1]
    z_re = (nr * a_re + ni * a_im) / den
    z_im = (ni * a_re - nr * a_im) / den
    bbar = (z_re[..., None] * b_re - z_im[..., None] * b_im,
            z_re[..., None] * b_im + z_im[..., None] * b_re)
    pw = [(jnp.ones_like(lam[0]), jnp.zeros_like(lam[0])), lam]
    for _ in range(2, Q + 1):
        pw.append(_cmul(pw[-1], lam))
    e = [(c_re * p[0][:, None, :] - c_im * p[1][:, None, :],
          c_re * p[1][:, None, :] + c_im * p[0][:, None, :]) for p in pw]
    hp = lax.Precision.HIGHEST
    k = [jnp.einsum('gcn,gnd->gcd', e[m][0], bbar[0], precision=hp)
         - jnp.einsum('gcn,gnd->gcd', e[m][1], bbar[1], precision=hp) for m in range(Q)]
    eye = jnp.eye(GROUPS_PER_SLAB, dtype=F32)
    split = lambda t: t.reshape((N_SLAB, GROUPS_PER_SLAB) + t.shape[1:])
    zero_k = jnp.zeros_like(k[0])
    kb = jnp.stack([jnp.stack([split(jnp.swapaxes(k[j - i] if j >= i else zero_k, 1, 2))
                               for j in range(Q)]) for i in range(Q)])
    m_mat = jnp.einsum('ijsgdc,gh->sigdjhc', kb, eye).reshape(N_SLAB, Q * LANES, Q * LANES)
    f = [_cmul((pw[Q - 1 - i][0][..., None], pw[Q - 1 - i][1][..., None]), bbar) for i in range(Q)]
    p_parts = []
    for part in range(2):
        fs = jnp.stack([split(f[i][part]) for i in range(Q)])
        p_parts.append(jnp.einsum('isgnd,gh->sigdhn', fs, eye).reshape(N_SLAB, Q * LANES, STATE_LANES // 2))
    p_mat = jnp.concatenate(p_parts, axis=-1)
    r_parts = []
    for part, sign in ((0, 1.0), (1, -1.0)):
        es = jnp.stack([split(e[j + 1][part]) for j in range(Q)])
        r_parts.append(sign * jnp.einsum('jsgcn,gh->shnjgc', es, eye).reshape(
            N_SLAB, STATE_LANES // 2, Q * LANES))
    r_mat = jnp.concatenate(r_parts, axis=1)
    mp = jnp.concatenate([m_mat, p_mat], axis=-1).astype(BF16)
    a_q = pw[Q]
    return (mp, r_mat.astype(BF16),
            a_q[0].reshape(N_SLAB, STATE_LANES // 2), a_q[1].reshape(N_SLAB, STATE_LANES // 2))


def _router_weights(w_rg, b_rg, w_re, b_re):
    pad_g = LANE_EXP0 - LANE_GRP0 - N_GROUPS_MOE
    pad_e = LANES - LANE_EXP0 - N_EXPERTS
    w = jnp.concatenate([w_rg, jnp.zeros((D_MODEL, pad_g), F32), w_re, jnp.zeros((D_MODEL, pad_e), F32)], axis=1)
    b = jnp.concatenate([b_rg, jnp.zeros((pad_g,), F32), b_re, jnp.zeros((pad_e,), F32)]).reshape(1, LANES)
    w_hi = w.astype(BF16)
    w_lo = (w - w_hi.astype(F32)).astype(BF16)
    return jnp.concatenate([w_hi, w_lo], axis=1), w_hi, b


def _time_major_permutation():
    tm = jnp.arange(TM, dtype=I32)
    src = (tm % BATCH) * TT + tm // BATCH
    perm = (src[:, None] == jnp.arange(TM, dtype=I32)[None, :]).astype(BF16)
    return perm, perm.T


def _sc_mesh():
    return plsc.VectorSubcoreMesh(core_axis_name="core", subcore_axis_name="subcore")


def _sc_worker(mesh):
    return lax.axis_index("core") * mesh.num_subcores + lax.axis_index("subcore")


def _dispatch(h2p, dest):
    mesh = _sc_mesh()
    n_win = N_TOK // SC_WINDOW
    per_worker = n_win // (mesh.num_cores * mesh.num_subcores)
    assert per_worker * mesh.num_cores * mesh.num_subcores == n_win

    @pl.kernel(out_type=jax.ShapeDtypeStruct((N_ROWS,) + ROW_TILE, U32), mesh=mesh,
               scratch_types=[pltpu.VMEM((SC_WINDOW,), I32), pltpu.VMEM((SC_WINDOW,) + ROW_TILE, U32)])
    def scatter_rows(h_hbm, dest_hbm, xs_hbm, idx_v, rows_v):
        first = _sc_worker(mesh) * per_worker

        @pl.loop(0, per_worker)
        def _(w):
            win = first + w
            pltpu.sync_copy(h_hbm.at[pl.ds(win * SC_WINDOW, SC_WINDOW)], rows_v)
            for j in range(TOPK):
                pltpu.sync_copy(dest_hbm.at[j, win], idx_v)
                pltpu.sync_copy(rows_v, xs_hbm.at[idx_v])

    return scatter_rows(h2p, dest)


def _collect(ys, dest):
    mesh = _sc_mesh()
    n_tok = dest.shape[1]
    n_win = TOPK * n_tok // SC_WINDOW
    per_worker = n_win // (mesh.num_cores * mesh.num_subcores)
    assert per_worker * mesh.num_cores * mesh.num_subcores == n_win

    @pl.kernel(out_type=jax.ShapeDtypeStruct((TOPK * n_tok,) + ROW_TILE, U32), mesh=mesh,
               scratch_types=[pltpu.VMEM((SC_WINDOW,), I32), pltpu.VMEM((SC_WINDOW,) + ROW_TILE, U32)])
    def gather_rows(ys_hbm, dest_hbm, yg_hbm, idx_v, rows_v):
        first = _sc_worker(mesh) * per_worker

        @pl.loop(0, per_worker)
        def _(w):
            win = first + w
            pltpu.sync_copy(dest_hbm.at[win], idx_v)
            pltpu.sync_copy(ys_hbm.at[idx_v], rows_v)
            pltpu.sync_copy(rows_v, yg_hbm.at[pl.ds(win * SC_WINDOW, SC_WINDOW)])

    return gather_rows(ys, dest.reshape(n_win, SC_WINDOW)).reshape((TOPK, n_tok) + ROW_TILE)


def _expert_kernel(first_ref, nblk_ref, nvalid_ref, nused_ref, xs_hbm, wg_ref, wu_ref, wd_ref, ys_hbm,
                   wg_scr, wu_scr, wd_scr, x_buf, y_buf, in_sem, out_sem):
    e = pl.program_id(0)
    nused = nused_ref[0]

    def in_copy(g):
        slot = lax.rem(g, IN_SLOTS)
        return pltpu.make_async_copy(xs_hbm.at[pl.ds(g * BM, BM)], x_buf.at[slot], in_sem.at[slot])

    def out_copy(g, slot):
        return pltpu.make_async_copy(y_buf.at[slot], ys_hbm.at[pl.ds(g * BM, BM)], out_sem.at[slot])

    @pl.when(e == 0)
    def _first():
        for g in range(IN_AHEAD):
            in_copy(g).start()

    wg_scr[...] = wg_ref[0].astype(BF16)
    wu_scr[...] = wu_ref[0].astype(BF16)
    wd_scr[...] = wd_ref[0].astype(BF16)

    def block(b, carry):
        g = first_ref[e] + b
        slot = lax.rem(g, 2)
        in_copy(g).wait()

        @pl.when(g + IN_AHEAD < nused)
        def _prefetch():
            in_copy(g + IN_AHEAD).start()

        @pl.when(g >= 2)
        def _slot_free():
            out_copy(g - 2, slot).wait()

        valid = lax.broadcasted_iota(I32, (BM, 1), 0) < nvalid_ref[g]
        x_blk = x_buf[lax.rem(g, IN_SLOTS)].reshape(BM, HALF)
        lo, hi = _unpack_bf16_pair(jnp.where(valid, x_blk, jnp.uint32(0)))
        lo = lo.astype(BF16)
        hi = hi.astype(BF16)
        gate = jnp.dot(lo, wg_scr[0:HALF, :], preferred_element_type=F32) \
            + jnp.dot(hi, wg_scr[HALF:, :], preferred_element_type=F32)
        up = jnp.dot(lo, wu_scr[0:HALF, :], preferred_element_type=F32) \
            + jnp.dot(hi, wu_scr[HALF:, :], preferred_element_type=F32)
        act = (jax.nn.silu(gate) * up).astype(BF16)
        o = jnp.dot(act, wd_scr[...], preferred_element_type=F32)
        y_buf[slot] = _pack_bf16_pair(o[:, 0:HALF], o[:, HALF:]).reshape((BM,) + ROW_TILE)
        out_copy(g, slot).start()
        return carry

    lax.fori_loop(0, nblk_ref[e], block, 0)

    @pl.when(e == N_EXPERTS - 1)
    def _drain():
        out_copy(nused - 2, lax.rem(nused, 2)).wait()
        out_copy(nused - 1, 1 - lax.rem(nused, 2)).wait()


def _experts(first, nblk, nvalid, nused, xs, wg, wu, wd):
    grid_spec = pltpu.PrefetchScalarGridSpec(
        num_scalar_prefetch=4,
        grid=(N_EXPERTS,),
        in_specs=[
            pl.BlockSpec(memory_space=pl.ANY),
            pl.BlockSpec((1, D_MODEL, D_EXPERT), lambda e, *_: (e, 0, 0)),
            pl.BlockSpec((1, D_MODEL, D_EXPERT), lambda e, *_: (e, 0, 0)),
            pl.BlockSpec((1, D_EXPERT, D_MODEL), lambda e, *_: (e, 0, 0)),
        ],
        out_specs=pl.BlockSpec(memory_space=pl.ANY),
        scratch_shapes=[
            pltpu.VMEM((D_MODEL, D_EXPERT), BF16),
            pltpu.VMEM((D_MODEL, D_EXPERT), BF16),
            pltpu.VMEM((D_EXPERT, D_MODEL), BF16),
            pltpu.VMEM((IN_SLOTS, BM) + ROW_TILE, U32),
            pltpu.VMEM((2, BM) + ROW_TILE, U32),
            pltpu.SemaphoreType.DMA((IN_SLOTS,)),
            pltpu.SemaphoreType.DMA((2,)),
        ],
    )
    return pl.pallas_call(
        _expert_kernel,
        grid_spec=grid_spec,
        out_shape=jax.ShapeDtypeStruct((N_ROWS,) + ROW_TILE, U32),
        compiler_params=pltpu.CompilerParams(
            dimension_semantics=("arbitrary",), vmem_limit_bytes=VMEM_LIMIT),
        name="experts",
    )(first, nblk, nvalid, nused, xs, wg, wu, wd)


def _combine_kernel(x1_ref, rec_ref, yg_ref, p_ref, gple_ref, wpg_ref, wple_ref, gfin_ref, *rest):
    out_ref = rest[-1]
    ple = jnp.dot(p_ref[0].reshape(TM, D_PLE).astype(BF16), wple_ref[...], preferred_element_type=F32)
    rec = rec_ref[...]
    w0 = rec[:, REC_W0:REC_W0 + 1]
    w1 = rec[:, REC_W1:REC_W1 + 1]
    lo0, hi0 = _unpack_bf16_pair(yg_ref[0].reshape(TM, HALF))
    lo1, hi1 = _unpack_bf16_pair(yg_ref[1].reshape(TM, HALF))
    moe = jnp.concatenate([lo0 * w0 + lo1 * w1, hi0 * w0 + hi1 * w1], axis=1)
    x2 = x1_ref[...].reshape(TM, D_MODEL) + moe
    gate = jax.nn.sigmoid(jnp.dot(_rms(x2, gple_ref[...]).astype(BF16), wpg_ref[...],
                                  preferred_element_type=F32))
    x3 = x2 + gate * ple
    out_ref[...] = _rms(x3, gfin_ref[...]).reshape(BATCH, TT, D_MODEL)


def _combine(part, x1, rec, yg, p, gple, wpg, wple, gfin, out_prev=None):
    s0 = part * PART_STEPS
    seq_spec = pl.BlockSpec((BATCH, TT, D_MODEL), lambda i: (0, s0 + i, 0))
    in_specs = [
        seq_spec,
        pl.BlockSpec((TM, LANES), lambda i: (s0 + i, 0)),
        pl.BlockSpec((TOPK, TM) + ROW_TILE, lambda i: (0, i, 0, 0)),
        pl.BlockSpec((1, BATCH, TT, D_PLE), lambda i: (0, 0, s0 + i, 0)),
        _const_spec((1, D_MODEL)),
        _const_spec((D_MODEL, D_MODEL)),
        _const_spec((D_PLE, D_MODEL)),
        _const_spec((1, D_MODEL)),
    ]
    args = [x1, rec, yg, p, gple, wpg, wple, gfin]
    aliases = {}
    if out_prev is not None:
        in_specs.append(pl.BlockSpec(memory_space=pl.ANY))
        args.append(out_prev)
        aliases = {len(args) - 1: 0}
    return pl.pallas_call(
        _combine_kernel,
        grid=(PART_STEPS,),
        in_specs=in_specs,
        out_specs=seq_spec,
        out_shape=jax.ShapeDtypeStruct((BATCH, SEQ, D_MODEL), F32),
        input_output_aliases=aliases,
        compiler_params=pltpu.CompilerParams(
            dimension_semantics=("arbitrary",), vmem_limit_bytes=VMEM_LIMIT),
        name="combine",
    )(*args)


def kernel(x, p, g_mix, w_in, b_gate, ssm_a_re, ssm_a_im, ssm_log_dt, ssm_b_re, ssm_b_im, ssm_c_re,
           ssm_c_im, ssm_d, w_glu, conv_dw, conv_dw_b, conv_ln_g, conv_ln_b, w_conv_out, w_out, g_moe,
           w_router_group, b_router_group, w_router_expert, b_router_expert, w_exp_gate, w_exp_up,
           w_exp_down, g_ple, w_ple_gate, w_ple, g_final):
    assert x.shape == (BATCH, SEQ, D_MODEL) and p.shape == (1, BATCH, SEQ, D_PLE)
    row = lambda v: v.reshape(1, -1)

    mp, rmat, a_re, a_im = _ssm_matrices(ssm_a_re[0], ssm_a_im[0], ssm_log_dt[0], ssm_b_re[0],
                                         ssm_b_im[0], ssm_c_re[0], ssm_c_im[0])
    wr1, wr2, br = _router_weights(w_router_group[0], b_router_group[0], w_router_expert[0],
                                   b_router_expert[0])
    perm, permt = _time_major_permutation()
    n_uv = D_SSM + 2 * D_CONV
    w_gates = jnp.transpose(w_in[0][:, n_uv:].astype(BF16).reshape(D_MODEL, N_GATE_BLK, GATE_COLS), (1, 0, 2))
    x1, h2p, rec, rect, cnt = _mixer(
        x, row(g_mix[0]), w_in[0][:, :n_uv].astype(BF16), w_gates, row(b_gate[0]), perm, permt, mp, rmat, a_re, a_im,
        row(ssm_d[0]), w_glu[0].astype(BF16), conv_dw[0], row(conv_dw_b[0]), row(conv_ln_g[0]),
        row(conv_ln_b[0]), w_conv_out[0].astype(BF16), w_out[0].astype(BF16), row(g_moe[0]), wr1, wr2, br)

    counts = cnt[0, LANE_EXP0:LANE_EXP0 + N_EXPERTS].astype(I32)
    pcounts = (counts + BM - 1) // BM * BM
    pends = jnp.cumsum(pcounts)
    pstarts = pends - pcounts
    eid = rect[REC_EID0:REC_EID1 + 1].astype(I32)
    rank = rect[REC_RANK0:REC_RANK1 + 1].astype(I32)
    dest = (jnp.sum(jnp.where(eid[..., None] == jnp.arange(N_EXPERTS, dtype=I32), pstarts, 0), axis=-1)
            + rank).reshape(TOPK, N_TOK // SC_WINDOW, SC_WINDOW)
    nused = (pends[-1] // BM).astype(I32)
    blk = jnp.arange(N_BLK, dtype=I32)[:, None] * BM
    in_expert = (pstarts[None, :] <= blk) & (blk < pends[None, :])
    nvalid = jnp.clip(jnp.sum(jnp.where(in_expert, (pstarts + counts)[None, :] - blk, 0), axis=1), 0, BM)

    xs = _dispatch(h2p, dest)
    ys = _experts(pstarts // BM, pcounts // BM, nvalid.astype(I32), nused.reshape(1), xs,
                  w_exp_gate[0], w_exp_up[0], w_exp_down[0])
    dest_tok = dest.reshape(TOPK, N_TOK)
    wpg = w_ple_gate[0].astype(BF16)
    wple = w_ple[0].astype(BF16)
    out = None
    for part in range(N_PARTS):
        tok = slice(part * PART_STEPS * TM, (part + 1) * PART_STEPS * TM)
        yg = _collect(ys, dest_tok[:, tok])
        out = _combine(part, x1, rec, yg, p, row(g_ple[0]), wpg, wple, row(g_final), out)
    return out
```

```python
import jax
import jax.numpy as jnp
from jax import lax
from jax.experimental import pallas as pl
from jax.experimental.pallas import tpu as pltpu
from jax.experimental.pallas import tpu_sc as plsc

F32 = jnp.float32
BF16 = jnp.bfloat16
U32 = jnp.uint32
I32 = jnp.int32

D_MODEL = 1024
BATCH = 8
SEQ = 2048
N_TOK = BATCH * SEQ
D_SSM = 512
SSM_GROUP_WIDTH = 16
SSM_GROUPS = 32
SSM_STATE = 64
D_CONV = 512
CONV_WIDTH = 31
D_IN = D_SSM + 2 * D_CONV + 2 * D_MODEL
N_GROUPS_MOE = 4
EXPERTS_PER_GROUP = 8
N_EXPERTS = 32
TOPK = 2
D_EXPERT = 512
D_PLE = 256
EPS = 1e-6

SUBLANES = 8
LANES = 128
assert BATCH == SUBLANES

TT = 64
TM = TT * BATCH
N_STEP = SEQ // TT
SB = 512
NSB = TM // SB
BPS = SB // TT
Q = 2
N_SLAB = D_SSM // LANES
GROUPS_PER_SLAB = SSM_GROUPS // N_SLAB
ROWS_Z = TM // Q
STATE_LANES = 2 * GROUPS_PER_SLAB * SSM_STATE
HALO = (CONV_WIDTH - 1) * BATCH
CHUNK_ROWS = SB // (Q * SUBLANES)
CONV_ROWS = 64
N_LC = D_CONV // LANES

LANE_GRP0 = 0
LANE_EXP0 = 32
REC_EID0, REC_EID1, REC_W0, REC_W1, REC_RANK0, REC_RANK1 = 0, 1, 2, 3, 4, 5
REC_ROWS = 8

BM = 256
N_BLK = (TOPK * N_TOK + N_EXPERTS * (BM - 1) + BM - 1) // BM
N_ROWS = N_BLK * BM
HALF = D_MODEL // 2
ROW_TILE = (HALF // LANES, LANES)
SC_WINDOW = 64
IN_AHEAD = 3
IN_SLOTS = IN_AHEAD + 1
N_PARTS = 4
PART_STEPS = N_STEP // N_PARTS

VMEM_LIMIT = 56 * 1024 * 1024


def _const_spec(shape):
    n = len(shape)
    return pl.BlockSpec(shape, lambda *_: (0,) * n, pipeline_mode=pl.Buffered(1))


def _rms(x, g):
    ms = jnp.mean(x * x, axis=-1, keepdims=True)
    return x * lax.rsqrt(ms + EPS) * g


def _pack_bf16_pair(lo, hi):
    ulo = lax.bitcast_convert_type(lo.astype(BF16).astype(F32), U32)
    uhi = lax.bitcast_convert_type(hi.astype(BF16).astype(F32), U32)
    return (ulo >> 16) | (uhi & jnp.uint32(0xFFFF0000))


def _unpack_bf16_pair(w):
    lo = lax.bitcast_convert_type(w << 16, F32)
    hi = lax.bitcast_convert_type(w & jnp.uint32(0xFFFF0000), F32)
    return lo, hi


def _mixer_kernel(x_ref, gmix_ref, win_ref, bgate_ref, perm_ref, permt_ref, mp_ref, r_ref, are_ref,
                  aim_ref, d_ref, wglu_ref, dw_ref, dwb_ref, lng_ref, lnb_ref, wco_ref, wout_ref,
                  gmoe_ref, wr1_ref, wr2_ref, br_ref,
                  x1_ref, h2p_ref, rec_ref, rect_ref, cnt_ref,
                  hb_scr, ht_scr, u_scr, y_scr, yi_scr, xs_scr, z_scr, conv_scr, act_scr, actb_scr,
                  logit_scr, s_scr, cnt_scr):
    step = pl.program_id(0)
    assert NSB == 1

    @pl.when(step == 0)
    def _init():
        logit_scr[...] = jnp.zeros(logit_scr.shape, F32)
        z_scr[:, 0:HALO, :] = jnp.zeros((N_LC, HALO, LANES), F32)
        s_scr[...] = jnp.zeros(s_scr.shape, F32)
        cnt_scr[...] = jnp.zeros(cnt_scr.shape, F32)

    def sub_rows(r):
        return pl.ds(pl.multiple_of(r * SB, SB), SB)

    def phase_a(r, carry):
        xb = x_ref[pl.ds(r * BPS, BPS)].reshape(SB, D_MODEL)
        hb_scr[sub_rows(r), :] = _rms(xb, gmix_ref[...]).astype(BF16)
        return carry

    def phase_a3(r, carry):
        h = ht_scr[sub_rows(r), :]
        u = jnp.dot(h, win_ref[:, 0:D_SSM], preferred_element_type=F32)
        u_scr[pl.ds(r * CHUNK_ROWS, CHUNK_ROWS)] = u.reshape(CHUNK_ROWS, Q, SUBLANES, D_SSM)
        v = jnp.dot(h, win_ref[:, D_SSM:D_SSM + 2 * D_CONV], preferred_element_type=F32)
        zc = v[:, 0:D_CONV] * jax.nn.sigmoid(v[:, D_CONV:])
        for lc in range(N_LC):
            z_scr[lc, pl.ds(pl.multiple_of(HALO + r * SB, SUBLANES), SB), :] = zc[:, lc * LANES:(lc + 1) * LANES]
        return carry

    def phase_b():
        for s in range(N_SLAB):
            lanes = slice(s * LANES, (s + 1) * LANES)
            z = jnp.concatenate(
                [u_scr[:, i, :, lanes].reshape(ROWS_Z, LANES) for i in range(Q)], axis=1).astype(BF16)
            xp = jnp.dot(z, mp_ref[s], preferred_element_type=F32)
            yi_scr[s] = xp[:, 0:Q * LANES]
            xs_scr[s] = xp[:, Q * LANES:]

        half = STATE_LANES // 2
        for s in range(N_SLAB):
            a_re = jnp.broadcast_to(are_ref[s:s + 1, :], (SUBLANES, half))
            a_im = jnp.broadcast_to(aim_ref[s:s + 1, :], (SUBLANES, half))

            def scan_body(k, carry, s=s, a_re=a_re, a_im=a_im):
                s_re, s_im = carry
                rows = pl.ds(pl.multiple_of(k * SUBLANES, SUBLANES), SUBLANES)
                x_re = xs_scr[s, rows, 0:half]
                x_im = xs_scr[s, rows, half:]
                xs_scr[s, rows, 0:half] = s_re
                xs_scr[s, rows, half:] = s_im
                n_re = a_re * s_re - a_im * s_im + x_re
                n_im = a_re * s_im + a_im * s_re + x_im
                return n_re, n_im

            s_re, s_im = lax.fori_loop(0, ROWS_Z // SUBLANES, scan_body,
                                       (s_scr[s, :, 0:half], s_scr[s, :, half:]), unroll=True)
            s_scr[s, :, 0:half] = s_re
            s_scr[s, :, half:] = s_im

        for s in range(N_SLAB):
            lanes = slice(s * LANES, (s + 1) * LANES)
            y_tot = yi_scr[s] + jnp.dot(xs_scr[s].astype(BF16), r_ref[s], preferred_element_type=F32)
            for j in range(Q):
                y_scr[:, j, :, lanes] = y_tot[:, j * LANES:(j + 1) * LANES].reshape(
                    ROWS_Z // SUBLANES, SUBLANES, LANES)

    def phase_c1(r, carry):
        rows = sub_rows(r)
        crow = pl.ds(r * CHUNK_ROWS, CHUNK_ROWS)
        y = y_scr[crow].reshape(SB, D_SSM) + d_ref[...] * u_scr[crow].reshape(SB, D_SSM)
        act_scr[rows, 0:D_SSM] = jax.nn.gelu(y).astype(BF16)
        for lc in range(N_LC):
            lanes = slice(lc * LANES, (lc + 1) * LANES)

            def conv_piece(rc, c, lc=lc, lanes=lanes):
                r0 = r * SB + rc * CONV_ROWS
                piece = jnp.broadcast_to(dwb_ref[:, lanes], (CONV_ROWS, LANES))
                for j in range(CONV_WIDTH):
                    zrows = pl.ds(pl.multiple_of(r0 + j * BATCH, SUBLANES), CONV_ROWS)
                    piece = piece + dw_ref[j:j + 1, lanes] * z_scr[lc, zrows, :]
                conv_scr[pl.ds(pl.multiple_of(rc * CONV_ROWS, CONV_ROWS), CONV_ROWS), lanes] = piece
                return c

            lax.fori_loop(0, SB // CONV_ROWS, conv_piece, 0, unroll=4)
        acc = conv_scr[...]
        mu = jnp.mean(acc, axis=-1, keepdims=True)
        cen = acc - mu
        var = jnp.mean(cen * cen, axis=-1, keepdims=True)
        ln = cen * lax.rsqrt(var + EPS) * lng_ref[...] + lnb_ref[...]
        act_scr[rows, D_SSM:] = jax.nn.silu(ln).astype(BF16)
        return carry

    lane = lax.broadcasted_iota(I32, (1, LANES), 1).astype(F32)
    grp_mask = lane < float(N_GROUPS_MOE)
    exp_lane = (lane >= float(LANE_EXP0)) & (lane < float(LANE_EXP0 + N_EXPERTS))
    lane_grp = jnp.floor((lane - float(LANE_EXP0)) * (1.0 / EXPERTS_PER_GROUP))
    tri = (lax.broadcasted_iota(I32, (SB, SB), 0) > lax.broadcasted_iota(I32, (SB, SB), 1)).astype(BF16)
    neg_inf = float("-inf")
    big = float(4 * LANES)

    def phase_c3(r, carry):
        rows = sub_rows(r)
        h = hb_scr[rows, :]
        g0 = D_SSM + 2 * D_CONV
        gate_ssm = jnp.dot(h, win_ref[:, g0:g0 + D_MODEL], preferred_element_type=F32) \
            + bgate_ref[:, 0:D_MODEL]
        gate_conv = jnp.dot(h, win_ref[:, g0 + D_MODEL:], preferred_element_type=F32) \
            + bgate_ref[:, D_MODEL:]
        zz = jnp.dot(actb_scr[rows, 0:D_SSM], wglu_ref[...], preferred_element_type=F32)
        y_ssm = zz[:, 0:D_MODEL] * jax.nn.sigmoid(zz[:, D_MODEL:])
        y_conv = jnp.dot(actb_scr[rows, D_SSM:], wco_ref[...], preferred_element_type=F32)

        merged = jax.nn.sigmoid(gate_ssm) * y_ssm + jax.nn.sigmoid(gate_conv) * y_conv
        xb = x_ref[pl.ds(r * BPS, BPS)].reshape(SB, D_MODEL)
        x1 = xb + jnp.dot(merged.astype(BF16), wout_ref[...], preferred_element_type=F32)
        x1_ref[pl.ds(r * BPS, BPS)] = x1.reshape(BPS, TT, D_MODEL)

        h2 = _rms(x1, gmoe_ref[...])
        h2p_ref[rows] = _pack_bf16_pair(h2[:, 0:HALF], h2[:, HALF:]).reshape((SB,) + ROW_TILE)

        h2_hi = h2.astype(BF16)
        h2_lo = (h2 - h2_hi.astype(F32)).astype(BF16)
        l1 = jnp.dot(h2_hi, wr1_ref[...], preferred_element_type=F32)
        l2 = jnp.dot(h2_lo, wr2_ref[...], preferred_element_type=F32)
        logit_scr[rows, :] = l1[:, 0:LANES] + l1[:, LANES:] + l2 + br_ref[...]
        return carry

    def route_previous():
        rows = sub_rows(0)
        logits = logit_scr[...]
        counted = jnp.where(step > 0, 1.0, 0.0)

        lg = jnp.where(grp_mask, logits, neg_inf)
        g_max = jnp.max(lg, axis=-1, keepdims=True)
        g_sel = jnp.min(jnp.where(lg == g_max, lane, big), axis=-1, keepdims=True)
        p_g = 1.0 / jnp.sum(jnp.where(grp_mask, jnp.exp(logits - g_max), 0.0), axis=-1, keepdims=True)
        le = jnp.where(exp_lane & (lane_grp == g_sel), logits, neg_inf)
        m1 = jnp.max(le, axis=-1, keepdims=True)
        i1 = jnp.min(jnp.where(le == m1, lane, big), axis=-1, keepdims=True)
        le2 = jnp.where(lane == i1, neg_inf, le)
        m2 = jnp.max(le2, axis=-1, keepdims=True)
        i2 = jnp.min(jnp.where(le2 == m2, lane, big), axis=-1, keepdims=True)
        e2 = jnp.exp(m2 - m1)
        den = 1.0 + e2
        w_a = (1.0 / den) * p_g
        w_b = (e2 / den) * p_g

        sel1 = lane == i1
        sel2 = lane == i2
        onehot = jnp.where(sel1 | sel2, counted, 0.0)
        prefix = jnp.dot(tri, onehot.astype(BF16), preferred_element_type=F32) + cnt_scr[...]
        rank_a = jnp.sum(jnp.where(sel1, prefix, 0.0), axis=-1, keepdims=True)
        rank_b = jnp.sum(jnp.where(sel2, prefix, 0.0), axis=-1, keepdims=True)
        cnt_scr[...] = cnt_scr[...] + jnp.sum(onehot, axis=0, keepdims=True)

        rec = jnp.where(lane == float(REC_EID0), i1 - float(LANE_EXP0), 0.0)
        rec = jnp.where(lane == float(REC_EID1), i2 - float(LANE_EXP0), rec)
        rec = jnp.where(lane == float(REC_W0), w_a, rec)
        rec = jnp.where(lane == float(REC_W1), w_b, rec)
        rec = jnp.where(lane == float(REC_RANK0), rank_a, rec)
        rec = jnp.where(lane == float(REC_RANK1), rank_b, rec)
        rec_ref[rows, :] = rec
        rect_ref[...] = jnp.transpose(rec)[0:REC_ROWS, :]
        cnt_ref[...] = cnt_scr[...]

    @pl.when(step < N_STEP)
    def _tile():
        route_previous()
        phase_a(0, 0)
        ht_scr[...] = jnp.dot(perm_ref[...], hb_scr[...], preferred_element_type=F32).astype(BF16)
        phase_a3(0, 0)
        phase_b()
        phase_c1(0, 0)
        z_scr[:, 0:HALO, :] = z_scr[:, TM:TM + HALO, :]
        actb_scr[...] = jnp.dot(permt_ref[...], act_scr[...], preferred_element_type=F32).astype(BF16)
        phase_c3(0, 0)

    @pl.when(step == N_STEP)
    def _last():
        route_previous()


def _mixer(x, gmix, win, bgate, perm, permt, mp, rmat, a_re, a_im, dvec, wglu, dw, dwb, lng, lnb, wco,
           wout, gmoe, wr1, wr2, br):
    tile = lambda i: jnp.minimum(i, N_STEP - 1)
    routed = lambda i: jnp.maximum(i - 1, 0)
    seq_spec = pl.BlockSpec((BATCH, TT, D_MODEL), lambda i: (0, tile(i), 0))
    in_specs = [
        seq_spec,
        _const_spec((1, D_MODEL)),
        _const_spec((D_MODEL, D_IN)),
        _const_spec((1, 2 * D_MODEL)),
        _const_spec((TM, TM)),
        _const_spec((TM, TM)),
        _const_spec(mp.shape),
        _const_spec(rmat.shape),
        _const_spec(a_re.shape),
        _const_spec(a_im.shape),
        _const_spec((1, D_SSM)),
        _const_spec((D_SSM, 2 * D_MODEL)),
        _const_spec((CONV_WIDTH, D_CONV)),
        _const_spec((1, D_CONV)),
        _const_spec((1, D_CONV)),
        _const_spec((1, D_CONV)),
        _const_spec((D_CONV, D_MODEL)),
        _const_spec((D_MODEL, D_MODEL)),
        _const_spec((1, D_MODEL)),
        _const_spec((D_MODEL, 2 * LANES)),
        _const_spec((D_MODEL, LANES)),
        _const_spec((1, LANES)),
    ]
    out_specs = [
        seq_spec,
        pl.BlockSpec((TM,) + ROW_TILE, lambda i: (tile(i), 0, 0)),
        pl.BlockSpec((TM, LANES), lambda i: (routed(i), 0)),
        pl.BlockSpec((REC_ROWS, TM), lambda i: (0, routed(i))),
        pl.BlockSpec((1, LANES), lambda i: (0, 0)),
    ]
    out_shape = [
        jax.ShapeDtypeStruct((BATCH, SEQ, D_MODEL), F32),
        jax.ShapeDtypeStruct((N_TOK,) + ROW_TILE, U32),
        jax.ShapeDtypeStruct((N_TOK, LANES), F32),
        jax.ShapeDtypeStruct((REC_ROWS, N_TOK), F32),
        jax.ShapeDtypeStruct((1, LANES), F32),
    ]
    chunk_shape = (ROWS_Z // SUBLANES, Q, SUBLANES, D_SSM)
    scratch = [
        pltpu.VMEM((TM, D_MODEL), BF16),
        pltpu.VMEM((TM, D_MODEL), BF16),
        pltpu.VMEM(chunk_shape, F32),
        pltpu.VMEM(chunk_shape, F32),
        pltpu.VMEM((N_SLAB, ROWS_Z, Q * LANES), F32),
        pltpu.VMEM((N_SLAB, ROWS_Z, STATE_LANES), F32),
        pltpu.VMEM((N_LC, HALO + TM, LANES), F32),
        pltpu.VMEM((SB, D_CONV), F32),
        pltpu.VMEM((TM, D_SSM + D_CONV), BF16),
        pltpu.VMEM((TM, D_SSM + D_CONV), BF16),
        pltpu.VMEM((TM, LANES), F32),
        pltpu.VMEM((N_SLAB, SUBLANES, STATE_LANES), F32),
        pltpu.VMEM((1, LANES), F32),
    ]
    return pl.pallas_call(
        _mixer_kernel,
        grid=(N_STEP + 1,),
        in_specs=in_specs,
        out_specs=out_specs,
        out_shape=out_shape,
        scratch_shapes=scratch,
        compiler_params=pltpu.CompilerParams(
            dimension_semantics=("arbitrary",), vmem_limit_bytes=VMEM_LIMIT),
        name="mixer",
    )(x, gmix, win, bgate, perm, permt, mp, rmat, a_re, a_im, dvec, wglu, dw, dwb, lng, lnb, wco, wout,
      gmoe, wr1, wr2, br)


def _cmul(a, b):
    return a[0] * b[0] - a[1] * b[1], a[0] * b[1] + a[1] * b[0]


def _ssm_matrices(a_re, a_im, log_dt, b_re, b_im, c_re, c_im):
    dt = jnp.exp(log_dt)[:, None]
    mag = jnp.exp(a_re * dt)
    lam = (mag * jnp.cos(a_im * dt), mag * jnp.sin(a_im * dt))
    den = a_re * a_re + a_im * a_im
    nr = lam[0] - 1.0
    ni = lam[1]
    z_re = (nr * a_re + ni * a_im) / den
    z_im = (ni * a_re - nr * a_im) / den
    bbar = (z_re[..., None] * b_re - z_im[..., None] * b_im,
            z_re[..., None] * b_im + z_im[..., None] * b_re)
    pw = [(jnp.ones_like(lam[0]), jnp.zeros_like(lam[0])), lam]
    for _ in range(2, Q + 1):
        pw.append(_cmul(pw[-1], lam))
    e = [(c_re * p[0][:, None, :] - c_im * p[1][:, None, :],
          c_re * p[1][:, None, :] + c_im * p[0][:, None, :]) for p in pw]
    hp = lax.Precision.HIGHEST
    k = [jnp.einsum('gcn,gnd->gcd', e[m][0], bbar[0], precision=hp)
         - jnp.einsum('gcn,gnd->gcd', e[m][1], bbar[1], precision=hp) for m in range(Q)]
    eye = jnp.eye(GROUPS_PER_SLAB, dtype=F32)
    split = lambda t: t.reshape((N_SLAB, GROUPS_PER_SLAB) + t.shape[1:])
    zero_k = jnp.zeros_like(k[0])
    kb = jnp.stack([jnp.stack([split(jnp.swapaxes(k[j - i] if j >= i else zero_k, 1, 2))
                               for j in range(Q)]) for i in range(Q)])
    m_mat = jnp.einsum('ijsgdc,gh->sigdjhc', kb, eye).reshape(N_SLAB, Q * LANES, Q * LANES)
    f = [_cmul((pw[Q - 1 - i][0][..., None], pw[Q - 1 - i][1][..., None]), bbar) for i in range(Q)]
    p_parts = []
    for part in range(2):
        fs = jnp.stack([split(f[i][part]) for i in range(Q)])
        p_parts.append(jnp.einsum('isgnd,gh->sigdhn', fs, eye).reshape(N_SLAB, Q * LANES, STATE_LANES // 2))
    p_mat = jnp.concatenate(p_parts, axis=-1)
    r_parts = []
    for part, sign in ((0, 1.0), (1, -1.0)):
        es = jnp.stack([split(e[j + 1][part]) for j in range(Q)])
        r_parts.append(sign * jnp.einsum('jsgcn,gh->shnjgc', es, eye).reshape(
            N_SLAB, STATE_LANES // 2, Q * LANES))
    r_mat = jnp.concatenate(r_parts, axis=1)
    mp = jnp.concatenate([m_mat, p_mat], axis=-1).astype(BF16)
    a_q = pw[Q]
    return (mp, r_mat.astype(BF16),
            a_q[0].reshape(N_SLAB, STATE_LANES // 2), a_q[1].reshape(N_SLAB, STATE_LANES // 2))


def _router_weights(w_rg, b_rg, w_re, b_re):
    pad_g = LANE_EXP0 - LANE_GRP0 - N_GROUPS_MOE
    pad_e = LANES - LANE_EXP0 - N_EXPERTS
    w = jnp.concatenate([w_rg, jnp.zeros((D_MODEL, pad_g), F32), w_re, jnp.zeros((D_MODEL, pad_e), F32)], axis=1)
    b = jnp.concatenate([b_rg, jnp.zeros((pad_g,), F32), b_re, jnp.zeros((pad_e,), F32)]).reshape(1, LANES)
    w_hi = w.astype(BF16)
    w_lo = (w - w_hi.astype(F32)).astype(BF16)
    return jnp.concatenate([w_hi, w_lo], axis=1), w_hi, b


def _time_major_permutation():
    tm = jnp.arange(TM, dtype=I32)
    src = (tm % BATCH) * TT + tm // BATCH
    perm = (src[:, None] == jnp.arange(TM, dtype=I32)[None, :]).astype(BF16)
    return perm, perm.T


def _sc_mesh():
    return plsc.VectorSubcoreMesh(core_axis_name="core", subcore_axis_name="subcore")


def _sc_worker(mesh):
    return lax.axis_index("core") * mesh.num_subcores + lax.axis_index("subcore")


def _dispatch(h2p, dest):
    mesh = _sc_mesh()
    n_win = N_TOK // SC_WINDOW
    per_worker = n_win // (mesh.num_cores * mesh.num_subcores)
    assert per_worker * mesh.num_cores * mesh.num_subcores == n_win

    @pl.kernel(out_type=jax.ShapeDtypeStruct((N_ROWS,) + ROW_TILE, U32), mesh=mesh,
               scratch_types=[pltpu.VMEM((SC_WINDOW,), I32), pltpu.VMEM((SC_WINDOW,) + ROW_TILE, U32)])
    def scatter_rows(h_hbm, dest_hbm, xs_hbm, idx_v, rows_v):
        first = _sc_worker(mesh) * per_worker

        @pl.loop(0, per_worker)
        def _(w):
            win = first + w
            pltpu.sync_copy(h_hbm.at[pl.ds(win * SC_WINDOW, SC_WINDOW)], rows_v)
            for j in range(TOPK):
                pltpu.sync_copy(dest_hbm.at[j, win], idx_v)
                pltpu.sync_copy(rows_v, xs_hbm.at[idx_v])

    return scatter_rows(h2p, dest)


def _collect(ys, dest):
    mesh = _sc_mesh()
    n_tok = dest.shape[1]
    n_win = TOPK * n_tok // SC_WINDOW
    per_worker = n_win // (mesh.num_cores * mesh.num_subcores)
    assert per_worker * mesh.num_cores * mesh.num_subcores == n_win

    @pl.kernel(out_type=jax.ShapeDtypeStruct((TOPK * n_tok,) + ROW_TILE, U32), mesh=mesh,
               scratch_types=[pltpu.VMEM((SC_WINDOW,), I32), pltpu.VMEM((SC_WINDOW,) + ROW_TILE, U32)])
    def gather_rows(ys_hbm, dest_hbm, yg_hbm, idx_v, rows_v):
        first = _sc_worker(mesh) * per_worker

        @pl.loop(0, per_worker)
        def _(w):
            win = first + w
            pltpu.sync_copy(dest_hbm.at[win], idx_v)
            pltpu.sync_copy(ys_hbm.at[idx_v], rows_v)
            pltpu.sync_copy(rows_v, yg_hbm.at[pl.ds(win * SC_WINDOW, SC_WINDOW)])

    return gather_rows(ys, dest.reshape(n_win, SC_WINDOW)).reshape((TOPK, n_tok) + ROW_TILE)


def _expert_kernel(first_ref, nblk_ref, nvalid_ref, nused_ref, xs_hbm, wg_ref, wu_ref, wd_ref, ys_hbm,
                   wg_scr, wu_scr, wd_scr, x_buf, y_buf, in_sem, out_sem):
    e = pl.program_id(0)
    nused = nused_ref[0]

    def in_copy(g):
        slot = lax.rem(g, IN_SLOTS)
        return pltpu.make_async_copy(xs_hbm.at[pl.ds(g * BM, BM)], x_buf.at[slot], in_sem.at[slot])

    def out_copy(g, slot):
        return pltpu.make_async_copy(y_buf.at[slot], ys_hbm.at[pl.ds(g * BM, BM)], out_sem.at[slot])

    @pl.when(e == 0)
    def _first():
        for g in range(IN_AHEAD):
            in_copy(g).start()

    wg_scr[...] = wg_ref[0].astype(BF16)
    wu_scr[...] = wu_ref[0].astype(BF16)
    wd_scr[...] = wd_ref[0].astype(BF16)

    def block(b, carry):
        g = first_ref[e] + b
        slot = lax.rem(g, 2)
        in_copy(g).wait()

        @pl.when(g + IN_AHEAD < nused)
        def _prefetch():
            in_copy(g + IN_AHEAD).start()

        @pl.when(g >= 2)
        def _slot_free():
            out_copy(g - 2, slot).wait()

        valid = lax.broadcasted_iota(I32, (BM, 1), 0) < nvalid_ref[g]
        x_blk = x_buf[lax.rem(g, IN_SLOTS)].reshape(BM, HALF)
        lo, hi = _unpack_bf16_pair(jnp.where(valid, x_blk, jnp.uint32(0)))
        lo = lo.astype(BF16)
        hi = hi.astype(BF16)
        gate = jnp.dot(lo, wg_scr[0:HALF, :], preferred_element_type=F32) \
            + jnp.dot(hi, wg_scr[HALF:, :], preferred_element_type=F32)
        up = jnp.dot(lo, wu_scr[0:HALF, :], preferred_element_type=F32) \
            + jnp.dot(hi, wu_scr[HALF:, :], preferred_element_type=F32)
        act = (jax.nn.silu(gate) * up).astype(BF16)
        o = jnp.dot(act, wd_scr[...], preferred_element_type=F32)
        y_buf[slot] = _pack_bf16_pair(o[:, 0:HALF], o[:, HALF:]).reshape((BM,) + ROW_TILE)
        out_copy(g, slot).start()
        return carry

    lax.fori_loop(0, nblk_ref[e], block, 0)

    @pl.when(e == N_EXPERTS - 1)
    def _drain():
        out_copy(nused - 2, lax.rem(nused, 2)).wait()
        out_copy(nused - 1, 1 - lax.rem(nused, 2)).wait()


def _experts(first, nblk, nvalid, nused, xs, wg, wu, wd):
    grid_spec = pltpu.PrefetchScalarGridSpec(
        num_scalar_prefetch=4,
        grid=(N_EXPERTS,),
        in_specs=[
            pl.BlockSpec(memory_space=pl.ANY),
            pl.BlockSpec((1, D_MODEL, D_EXPERT), lambda e, *_: (e, 0, 0)),
            pl.BlockSpec((1, D_MODEL, D_EXPERT), lambda e, *_: (e, 0, 0)),
            pl.BlockSpec((1, D_EXPERT, D_MODEL), lambda e, *_: (e, 0, 0)),
        ],
        out_specs=pl.BlockSpec(memory_space=pl.ANY),
        scratch_shapes=[
            pltpu.VMEM((D_MODEL, D_EXPERT), BF16),
            pltpu.VMEM((D_MODEL, D_EXPERT), BF16),
            pltpu.VMEM((D_EXPERT, D_MODEL), BF16),
            pltpu.VMEM((IN_SLOTS, BM) + ROW_TILE, U32),
            pltpu.VMEM((2, BM) + ROW_TILE, U32),
            pltpu.SemaphoreType.DMA((IN_SLOTS,)),
            pltpu.SemaphoreType.DMA((2,)),
        ],
    )
    return pl.pallas_call(
        _expert_kernel,
        grid_spec=grid_spec,
        out_shape=jax.ShapeDtypeStruct((N_ROWS,) + ROW_TILE, U32),
        compiler_params=pltpu.CompilerParams(
            dimension_semantics=("arbitrary",), vmem_limit_bytes=VMEM_LIMIT),
        name="experts",
    )(first, nblk, nvalid, nused, xs, wg, wu, wd)


def _combine_kernel(x1_ref, rec_ref, yg_ref, p_ref, gple_ref, wpg_ref, wple_ref, gfin_ref, *rest):
    out_ref = rest[-1]
    ple = jnp.dot(p_ref[0].reshape(TM, D_PLE).astype(BF16), wple_ref[...], preferred_element_type=F32)
    rec = rec_ref[...]
    w0 = rec[:, REC_W0:REC_W0 + 1]
    w1 = rec[:, REC_W1:REC_W1 + 1]
    lo0, hi0 = _unpack_bf16_pair(yg_ref[0].reshape(TM, HALF))
    lo1, hi1 = _unpack_bf16_pair(yg_ref[1].reshape(TM, HALF))
    moe = jnp.concatenate([lo0 * w0 + lo1 * w1, hi0 * w0 + hi1 * w1], axis=1)
    x2 = x1_ref[...].reshape(TM, D_MODEL) + moe
    gate = jax.nn.sigmoid(jnp.dot(_rms(x2, gple_ref[...]).astype(BF16), wpg_ref[...],
                                  preferred_element_type=F32))
    x3 = x2 + gate * ple
    out_ref[...] = _rms(x3, gfin_ref[...]).reshape(BATCH, TT, D_MODEL)


def _combine(part, x1, rec, yg, p, gple, wpg, wple, gfin, out_prev=None):
    s0 = part * PART_STEPS
    seq_spec = pl.BlockSpec((BATCH, TT, D_MODEL), lambda i: (0, s0 + i, 0))
    in_specs = [
        seq_spec,
        pl.BlockSpec((TM, LANES), lambda i: (s0 + i, 0)),
        pl.BlockSpec((TOPK, TM) + ROW_TILE, lambda i: (0, i, 0, 0)),
        pl.BlockSpec((1, BATCH, TT, D_PLE), lambda i: (0, 0, s0 + i, 0)),
        _const_spec((1, D_MODEL)),
        _const_spec((D_MODEL, D_MODEL)),
        _const_spec((D_PLE, D_MODEL)),
        _const_spec((1, D_MODEL)),
    ]
    args = [x1, rec, yg, p, gple, wpg, wple, gfin]
    aliases = {}
    if out_prev is not None:
        in_specs.append(pl.BlockSpec(memory_space=pl.ANY))
        args.append(out_prev)
        aliases = {len(args) - 1: 0}
    return pl.pallas_call(
        _combine_kernel,
        grid=(PART_STEPS,),
        in_specs=in_specs,
        out_specs=seq_spec,
        out_shape=jax.ShapeDtypeStruct((BATCH, SEQ, D_MODEL), F32),
        input_output_aliases=aliases,
        compiler_params=pltpu.CompilerParams(
            dimension_semantics=("arbitrary",), vmem_limit_bytes=VMEM_LIMIT),
        name="combine",
    )(*args)


def kernel(x, p, g_mix, w_in, b_gate, ssm_a_re, ssm_a_im, ssm_log_dt, ssm_b_re, ssm_b_im, ssm_c_re,
           ssm_c_im, ssm_d, w_glu, conv_dw, conv_dw_b, conv_ln_g, conv_ln_b, w_conv_out, w_out, g_moe,
           w_router_group, b_router_group, w_router_expert, b_router_expert, w_exp_gate, w_exp_up,
           w_exp_down, g_ple, w_ple_gate, w_ple, g_final):
    assert x.shape == (BATCH, SEQ, D_MODEL) and p.shape == (1, BATCH, SEQ, D_PLE)
    row = lambda v: v.reshape(1, -1)

    mp, rmat, a_re, a_im = _ssm_matrices(ssm_a_re[0], ssm_a_im[0], ssm_log_dt[0], ssm_b_re[0],
                                         ssm_b_im[0], ssm_c_re[0], ssm_c_im[0])
    wr1, wr2, br = _router_weights(w_router_group[0], b_router_group[0], w_router_expert[0],
                                   b_router_expert[0])
    perm, permt = _time_major_permutation()
    x1, h2p, rec, rect, cnt = _mixer(
        x, row(g_mix[0]), w_in[0].astype(BF16), row(b_gate[0]), perm, permt, mp, rmat, a_re, a_im,
        row(ssm_d[0]), w_glu[0].astype(BF16), conv_dw[0], row(conv_dw_b[0]), row(conv_ln_g[0]),
        row(conv_ln_b[0]), w_conv_out[0].astype(BF16), w_out[0].astype(BF16), row(g_moe[0]), wr1, wr2, br)

    counts = cnt[0, LANE_EXP0:LANE_EXP0 + N_EXPERTS].astype(I32)
    pcounts = (counts + BM - 1) // BM * BM
    pends = jnp.cumsum(pcounts)
    pstarts = pends - pcounts
    eid = rect[REC_EID0:REC_EID1 + 1].astype(I32)
    rank = rect[REC_RANK0:REC_RANK1 + 1].astype(I32)
    dest = (jnp.sum(jnp.where(eid[..., None] == jnp.arange(N_EXPERTS, dtype=I32), pstarts, 0), axis=-1)
            + rank).reshape(TOPK, N_TOK // SC_WINDOW, SC_WINDOW)
    nused = (pends[-1] // BM).astype(I32)
    blk = jnp.arange(N_BLK, dtype=I32)[:, None] * BM
    in_expert = (pstarts[None, :] <= blk) & (blk < pends[None, :])
    nvalid = jnp.clip(jnp.sum(jnp.where(in_expert, (pstarts + counts)[None, :] - blk, 0), axis=1), 0, BM)

    xs = _dispatch(h2p, dest)
    ys = _experts(pstarts // BM, pcounts // BM, nvalid.astype(I32), nused.reshape(1), xs,
                  w_exp_gate[0], w_exp_up[0], w_exp_down[0])
    dest_tok = dest.reshape(TOPK, N_TOK)
    wpg = w_ple_gate[0].astype(BF16)
    wple = w_ple[0].astype(BF16)
    out = None
    for part in range(N_PARTS):
        tok = slice(part * PART_STEPS * TM, (part + 1) * PART_STEPS * TM)
        yg = _collect(ys, dest_tok[:, tok])
        out = _combine(part, x1, rec, yg, p, row(g_ple[0]), wpg, wple, row(g_final), out)
    return out
```

```python
import jax
import jax.numpy as jnp
from jax import lax
from jax.experimental import pallas as pl
from jax.experimental.pallas import tpu as pltpu
from jax.experimental.pallas import tpu_sc as plsc

F32 = jnp.float32
BF16 = jnp.bfloat16
U32 = jnp.uint32
I32 = jnp.int32

D_MODEL = 1024
BATCH = 8
SEQ = 2048
N_TOK = BATCH * SEQ
D_SSM = 512
SSM_GROUP_WIDTH = 16
SSM_GROUPS = 32
SSM_STATE = 64
D_CONV = 512
CONV_WIDTH = 31
D_IN = D_SSM + 2 * D_CONV + 2 * D_MODEL
N_GROUPS_MOE = 4
EXPERTS_PER_GROUP = 8
N_EXPERTS = 32
TOPK = 2
D_EXPERT = 512
D_PLE = 256
EPS = 1e-6

SUBLANES = 8
LANES = 128
assert BATCH == SUBLANES

TT = 64
TM = TT * BATCH
N_STEP = SEQ // TT
SB = 512
NSB = TM // SB
BPS = SB // TT
Q = 2
N_SLAB = D_SSM // LANES
GROUPS_PER_SLAB = SSM_GROUPS // N_SLAB
ROWS_Z = TM // Q
STATE_LANES = 2 * GROUPS_PER_SLAB * SSM_STATE
HALO = (CONV_WIDTH - 1) * BATCH
CHUNK_ROWS = SB // (Q * SUBLANES)
CONV_ROWS = 64
N_LC = D_CONV // LANES

LANE_GRP0 = 0
LANE_EXP0 = 32
REC_EID0, REC_EID1, REC_W0, REC_W1, REC_RANK0, REC_RANK1 = 0, 1, 2, 3, 4, 5
REC_ROWS = 8

BM = 256
N_BLK = (TOPK * N_TOK + N_EXPERTS * (BM - 1) + BM - 1) // BM
N_ROWS = N_BLK * BM
HALF = D_MODEL // 2
ROW_TILE = (HALF // LANES, LANES)
SC_WINDOW = 64
IN_AHEAD = 3
IN_SLOTS = IN_AHEAD + 1
N_PARTS = 2
PART_STEPS = N_STEP // N_PARTS

VMEM_LIMIT = 56 * 1024 * 1024


def _const_spec(shape):
    n = len(shape)
    return pl.BlockSpec(shape, lambda *_: (0,) * n, pipeline_mode=pl.Buffered(1))


def _rms(x, g):
    ms = jnp.mean(x * x, axis=-1, keepdims=True)
    return x * lax.rsqrt(ms + EPS) * g


def _pack_bf16_pair(lo, hi):
    ulo = lax.bitcast_convert_type(lo.astype(BF16).astype(F32), U32)
    uhi = lax.bitcast_convert_type(hi.astype(BF16).astype(F32), U32)
    return (ulo >> 16) | (uhi & jnp.uint32(0xFFFF0000))


def _unpack_bf16_pair(w):
    lo = lax.bitcast_convert_type(w << 16, F32)
    hi = lax.bitcast_convert_type(w & jnp.uint32(0xFFFF0000), F32)
    return lo, hi


def _mixer_kernel(x_ref, gmix_ref, win_ref, bgate_ref, perm_ref, permt_ref, mp_ref, r_ref, are_ref,
                  aim_ref, d_ref, wglu_ref, dw_ref, dwb_ref, lng_ref, lnb_ref, wco_ref, wout_ref,
                  gmoe_ref, wr1_ref, wr2_ref, br_ref,
                  x1_ref, h2p_ref, rec_ref, rect_ref, cnt_ref,
                  hb_scr, ht_scr, u_scr, y_scr, yi_scr, xs_scr, z_scr, conv_scr, act_scr, actb_scr,
                  logit_scr, s_scr, cnt_scr):
    step = pl.program_id(0)
    assert NSB == 1

    @pl.when(step == 0)
    def _init():
        logit_scr[...] = jnp.zeros(logit_scr.shape, F32)
        z_scr[:, 0:HALO, :] = jnp.zeros((N_LC, HALO, LANES), F32)
        s_scr[...] = jnp.zeros(s_scr.shape, F32)
        cnt_scr[...] = jnp.zeros(cnt_scr.shape, F32)

    def sub_rows(r):
        return pl.ds(pl.multiple_of(r * SB, SB), SB)

    def phase_a(r, carry):
        xb = x_ref[pl.ds(r * BPS, BPS)].reshape(SB, D_MODEL)
        hb_scr[sub_rows(r), :] = _rms(xb, gmix_ref[...]).astype(BF16)
        return carry

    def phase_a3(r, carry):
        h = ht_scr[sub_rows(r), :]
        u = jnp.dot(h, win_ref[:, 0:D_SSM], preferred_element_type=F32)
        u_scr[pl.ds(r * CHUNK_ROWS, CHUNK_ROWS)] = u.reshape(CHUNK_ROWS, Q, SUBLANES, D_SSM)
        v = jnp.dot(h, win_ref[:, D_SSM:D_SSM + 2 * D_CONV], preferred_element_type=F32)
        zc = v[:, 0:D_CONV] * jax.nn.sigmoid(v[:, D_CONV:])
        for lc in range(N_LC):
            z_scr[lc, pl.ds(pl.multiple_of(HALO + r * SB, SUBLANES), SB), :] = zc[:, lc * LANES:(lc + 1) * LANES]
        return carry

    def phase_b():
        for s in range(N_SLAB):
            lanes = slice(s * LANES, (s + 1) * LANES)
            z = jnp.concatenate(
                [u_scr[:, i, :, lanes].reshape(ROWS_Z, LANES) for i in range(Q)], axis=1).astype(BF16)
            xp = jnp.dot(z, mp_ref[s], preferred_element_type=F32)
            yi_scr[s] = xp[:, 0:Q * LANES]
            xs_scr[s] = xp[:, Q * LANES:]

        half = STATE_LANES // 2
        for s in range(N_SLAB):
            a_re = jnp.broadcast_to(are_ref[s:s + 1, :], (SUBLANES, half))
            a_im = jnp.broadcast_to(aim_ref[s:s + 1, :], (SUBLANES, half))

            def scan_body(k, carry, s=s, a_re=a_re, a_im=a_im):
                s_re, s_im = carry
                rows = pl.ds(pl.multiple_of(k * SUBLANES, SUBLANES), SUBLANES)
                x_re = xs_scr[s, rows, 0:half]
                x_im = xs_scr[s, rows, half:]
                xs_scr[s, rows, 0:half] = s_re
                xs_scr[s, rows, half:] = s_im
                n_re = a_re * s_re - a_im * s_im + x_re
                n_im = a_re * s_im + a_im * s_re + x_im
                return n_re, n_im

            s_re, s_im = lax.fori_loop(0, ROWS_Z // SUBLANES, scan_body,
                                       (s_scr[s, :, 0:half], s_scr[s, :, half:]), unroll=True)
            s_scr[s, :, 0:half] = s_re
            s_scr[s, :, half:] = s_im

        for s in range(N_SLAB):
            lanes = slice(s * LANES, (s + 1) * LANES)
            y_tot = yi_scr[s] + jnp.dot(xs_scr[s].astype(BF16), r_ref[s], preferred_element_type=F32)
            for j in range(Q):
                y_scr[:, j, :, lanes] = y_tot[:, j * LANES:(j + 1) * LANES].reshape(
                    ROWS_Z // SUBLANES, SUBLANES, LANES)

    def phase_c1(r, carry):
        rows = sub_rows(r)
        crow = pl.ds(r * CHUNK_ROWS, CHUNK_ROWS)
        y = y_scr[crow].reshape(SB, D_SSM) + d_ref[...] * u_scr[crow].reshape(SB, D_SSM)
        act_scr[rows, 0:D_SSM] = jax.nn.gelu(y).astype(BF16)
        for lc in range(N_LC):
            lanes = slice(lc * LANES, (lc + 1) * LANES)

            def conv_piece(rc, c, lc=lc, lanes=lanes):
                r0 = r * SB + rc * CONV_ROWS
                piece = jnp.broadcast_to(dwb_ref[:, lanes], (CONV_ROWS, LANES))
                for j in range(CONV_WIDTH):
                    zrows = pl.ds(pl.multiple_of(r0 + j * BATCH, SUBLANES), CONV_ROWS)
                    piece = piece + dw_ref[j:j + 1, lanes] * z_scr[lc, zrows, :]
                conv_scr[pl.ds(pl.multiple_of(rc * CONV_ROWS, CONV_ROWS), CONV_ROWS), lanes] = piece
                return c

            lax.fori_loop(0, SB // CONV_ROWS, conv_piece, 0, unroll=4)
        acc = conv_scr[...]
        mu = jnp.mean(acc, axis=-1, keepdims=True)
        cen = acc - mu
        var = jnp.mean(cen * cen, axis=-1, keepdims=True)
        ln = cen * lax.rsqrt(var + EPS) * lng_ref[...] + lnb_ref[...]
        act_scr[rows, D_SSM:] = jax.nn.silu(ln).astype(BF16)
        return carry

    lane = lax.broadcasted_iota(I32, (1, LANES), 1).astype(F32)
    grp_mask = lane < float(N_GROUPS_MOE)
    exp_lane = (lane >= float(LANE_EXP0)) & (lane < float(LANE_EXP0 + N_EXPERTS))
    lane_grp = jnp.floor((lane - float(LANE_EXP0)) * (1.0 / EXPERTS_PER_GROUP))
    tri = (lax.broadcasted_iota(I32, (SB, SB), 0) > lax.broadcasted_iota(I32, (SB, SB), 1)).astype(BF16)
    neg_inf = float("-inf")
    big = float(4 * LANES)

    def phase_c3(r, carry):
        rows = sub_rows(r)
        h = hb_scr[rows, :]
        g0 = D_SSM + 2 * D_CONV
        gate_ssm = jnp.dot(h, win_ref[:, g0:g0 + D_MODEL], preferred_element_type=F32) \
            + bgate_ref[:, 0:D_MODEL]
        gate_conv = jnp.dot(h, win_ref[:, g0 + D_MODEL:], preferred_element_type=F32) \
            + bgate_ref[:, D_MODEL:]
        zz = jnp.dot(actb_scr[rows, 0:D_SSM], wglu_ref[...], preferred_element_type=F32)
        y_ssm = zz[:, 0:D_MODEL] * jax.nn.sigmoid(zz[:, D_MODEL:])
        y_conv = jnp.dot(actb_scr[rows, D_SSM:], wco_ref[...], preferred_element_type=F32)

        merged = jax.nn.sigmoid(gate_ssm) * y_ssm + jax.nn.sigmoid(gate_conv) * y_conv
        xb = x_ref[pl.ds(r * BPS, BPS)].reshape(SB, D_MODEL)
        x1 = xb + jnp.dot(merged.astype(BF16), wout_ref[...], preferred_element_type=F32)
        x1_ref[pl.ds(r * BPS, BPS)] = x1.reshape(BPS, TT, D_MODEL)

        h2 = _rms(x1, gmoe_ref[...])
        h2p_ref[rows] = _pack_bf16_pair(h2[:, 0:HALF], h2[:, HALF:]).reshape((SB,) + ROW_TILE)

        h2_hi = h2.astype(BF16)
        h2_lo = (h2 - h2_hi.astype(F32)).astype(BF16)
        l1 = jnp.dot(h2_hi, wr1_ref[...], preferred_element_type=F32)
        l2 = jnp.dot(h2_lo, wr2_ref[...], preferred_element_type=F32)
        logit_scr[rows, :] = l1[:, 0:LANES] + l1[:, LANES:] + l2 + br_ref[...]
        return carry

    def route_previous():
        rows = sub_rows(0)
        logits = logit_scr[...]
        counted = jnp.where(step > 0, 1.0, 0.0)

        lg = jnp.where(grp_mask, logits, neg_inf)
        g_max = jnp.max(lg, axis=-1, keepdims=True)
        g_sel = jnp.min(jnp.where(lg == g_max, lane, big), axis=-1, keepdims=True)
        p_g = 1.0 / jnp.sum(jnp.where(grp_mask, jnp.exp(logits - g_max), 0.0), axis=-1, keepdims=True)
        le = jnp.where(exp_lane & (lane_grp == g_sel), logits, neg_inf)
        m1 = jnp.max(le, axis=-1, keepdims=True)
        i1 = jnp.min(jnp.where(le == m1, lane, big), axis=-1, keepdims=True)
        le2 = jnp.where(lane == i1, neg_inf, le)
        m2 = jnp.max(le2, axis=-1, keepdims=True)
        i2 = jnp.min(jnp.where(le2 == m2, lane, big), axis=-1, keepdims=True)
        e2 = jnp.exp(m2 - m1)
        den = 1.0 + e2
        w_a = (1.0 / den) * p_g
        w_b = (e2 / den) * p_g

        sel1 = lane == i1
        sel2 = lane == i2
        onehot = jnp.where(sel1 | sel2, counted, 0.0)
        prefix = jnp.dot(tri, onehot.astype(BF16), preferred_element_type=F32) + cnt_scr[...]
        rank_a = jnp.sum(jnp.where(sel1, prefix, 0.0), axis=-1, keepdims=True)
        rank_b = jnp.sum(jnp.where(sel2, prefix, 0.0), axis=-1, keepdims=True)
        cnt_scr[...] = cnt_scr[...] + jnp.sum(onehot, axis=0, keepdims=True)

        rec = jnp.where(lane == float(REC_EID0), i1 - float(LANE_EXP0), 0.0)
        rec = jnp.where(lane == float(REC_EID1), i2 - float(LANE_EXP0), rec)
        rec = jnp.where(lane == float(REC_W0), w_a, rec)
        rec = jnp.where(lane == float(REC_W1), w_b, rec)
        rec = jnp.where(lane == float(REC_RANK0), rank_a, rec)
        rec = jnp.where(lane == float(REC_RANK1), rank_b, rec)
        rec_ref[rows, :] = rec
        rect_ref[...] = jnp.transpose(rec)[0:REC_ROWS, :]
        cnt_ref[...] = cnt_scr[...]

    @pl.when(step < N_STEP)
    def _tile():
        route_previous()
        phase_a(0, 0)
        ht_scr[...] = jnp.swapaxes(hb_scr[...].reshape(BATCH, TT, D_MODEL), 0, 1).reshape(TM, D_MODEL)
        phase_a3(0, 0)
        phase_b()
        phase_c1(0, 0)
        z_scr[:, 0:HALO, :] = z_scr[:, TM:TM + HALO, :]
        actb_scr[...] = jnp.swapaxes(act_scr[...].reshape(TT, BATCH, D_SSM + D_CONV), 0, 1).reshape(
            TM, D_SSM + D_CONV)
        phase_c3(0, 0)

    @pl.when(step == N_STEP)
    def _last():
        route_previous()


def _mixer(x, gmix, win, bgate, perm, permt, mp, rmat, a_re, a_im, dvec, wglu, dw, dwb, lng, lnb, wco,
           wout, gmoe, wr1, wr2, br):
    tile = lambda i: jnp.minimum(i, N_STEP - 1)
    routed = lambda i: jnp.maximum(i - 1, 0)
    seq_spec = pl.BlockSpec((BATCH, TT, D_MODEL), lambda i: (0, tile(i), 0))
    in_specs = [
        seq_spec,
        _const_spec((1, D_MODEL)),
        _const_spec((D_MODEL, D_IN)),
        _const_spec((1, 2 * D_MODEL)),
        _const_spec((TM, TM)),
        _const_spec((TM, TM)),
        _const_spec(mp.shape),
        _const_spec(rmat.shape),
        _const_spec(a_re.shape),
        _const_spec(a_im.shape),
        _const_spec((1, D_SSM)),
        _const_spec((D_SSM, 2 * D_MODEL)),
        _const_spec((CONV_WIDTH, D_CONV)),
        _const_spec((1, D_CONV)),
        _const_spec((1, D_CONV)),
        _const_spec((1, D_CONV)),
        _const_spec((D_CONV, D_MODEL)),
        _const_spec((D_MODEL, D_MODEL)),
        _const_spec((1, D_MODEL)),
        _const_spec((D_MODEL, 2 * LANES)),
        _const_spec((D_MODEL, LANES)),
        _const_spec((1, LANES)),
    ]
    out_specs = [
        seq_spec,
        pl.BlockSpec((TM,) + ROW_TILE, lambda i: (tile(i), 0, 0)),
        pl.BlockSpec((TM, LANES), lambda i: (routed(i), 0)),
        pl.BlockSpec((REC_ROWS, TM), lambda i: (0, routed(i))),
        pl.BlockSpec((1, LANES), lambda i: (0, 0)),
    ]
    out_shape = [
        jax.ShapeDtypeStruct((BATCH, SEQ, D_MODEL), F32),
        jax.ShapeDtypeStruct((N_TOK,) + ROW_TILE, U32),
        jax.ShapeDtypeStruct((N_TOK, LANES), F32),
        jax.ShapeDtypeStruct((REC_ROWS, N_TOK), F32),
        jax.ShapeDtypeStruct((1, LANES), F32),
    ]
    chunk_shape = (ROWS_Z // SUBLANES, Q, SUBLANES, D_SSM)
    scratch = [
        pltpu.VMEM((TM, D_MODEL), BF16),
        pltpu.VMEM((TM, D_MODEL), BF16),
        pltpu.VMEM(chunk_shape, F32),
        pltpu.VMEM(chunk_shape, F32),
        pltpu.VMEM((N_SLAB, ROWS_Z, Q * LANES), F32),
        pltpu.VMEM((N_SLAB, ROWS_Z, STATE_LANES), F32),
        pltpu.VMEM((N_LC, HALO + TM, LANES), F32),
        pltpu.VMEM((SB, D_CONV), F32),
        pltpu.VMEM((TM, D_SSM + D_CONV), BF16),
        pltpu.VMEM((TM, D_SSM + D_CONV), BF16),
        pltpu.VMEM((TM, LANES), F32),
        pltpu.VMEM((N_SLAB, SUBLANES, STATE_LANES), F32),
        pltpu.VMEM((1, LANES), F32),
    ]
    return pl.pallas_call(
        _mixer_kernel,
        grid=(N_STEP + 1,),
        in_specs=in_specs,
        out_specs=out_specs,
        out_shape=out_shape,
        scratch_shapes=scratch,
        compiler_params=pltpu.CompilerParams(
            dimension_semantics=("arbitrary",), vmem_limit_bytes=VMEM_LIMIT),
        name="mixer",
    )(x, gmix, win, bgate, perm, permt, mp, rmat, a_re, a_im, dvec, wglu, dw, dwb, lng, lnb, wco, wout,
      gmoe, wr1, wr2, br)


def _cmul(a, b):
    return a[0] * b[0] - a[1] * b[1], a[0] * b[1] + a[1] * b[0]


def _ssm_matrices(a_re, a_im, log_dt, b_re, b_im, c_re, c_im):
    dt = jnp.exp(log_dt)[:, None]
    mag = jnp.exp(a_re * dt)
    lam = (mag * jnp.cos(a_im * dt), mag * jnp.sin(a_im * dt))
    den = a_re * a_re + a_im * a_im
    nr = lam[0] - 1.0
    ni = lam[1]
    z_re = (nr * a_re + ni * a_im) / den
    z_im = (ni * a_re - nr * a_im) / den
    bbar = (z_re[..., None] * b_re - z_im[..., None] * b_im,
            z_re[..., None] * b_im + z_im[..., None] * b_re)
    pw = [(jnp.ones_like(lam[0]), jnp.zeros_like(lam[0])), lam]
    for _ in range(2, Q + 1):
        pw.append(_cmul(pw[-1], lam))
    e = [(c_re * p[0][:, None, :] - c_im * p[1][:, None, :],
          c_re * p[1][:, None, :] + c_im * p[0][:, None, :]) for p in pw]
    hp = lax.Precision.HIGHEST
    k = [jnp.einsum('gcn,gnd->gcd', e[m][0], bbar[0], precision=hp)
         - jnp.einsum('gcn,gnd->gcd', e[m][1], bbar[1], precision=hp) for m in range(Q)]
    eye = jnp.eye(GROUPS_PER_SLAB, dtype=F32)
    split = lambda t: t.reshape((N_SLAB, GROUPS_PER_SLAB) + t.shape[1:])
    zero_k = jnp.zeros_like(k[0])
    kb = jnp.stack([jnp.stack([split(jnp.swapaxes(k[j - i] if j >= i else zero_k, 1, 2))
                               for j in range(Q)]) for i in range(Q)])
    m_mat = jnp.einsum('ijsgdc,gh->sigdjhc', kb, eye).reshape(N_SLAB, Q * LANES, Q * LANES)
    f = [_cmul((pw[Q - 1 - i][0][..., None], pw[Q - 1 - i][1][..., None]), bbar) for i in range(Q)]
    p_parts = []
    for part in range(2):
        fs = jnp.stack([split(f[i][part]) for i in range(Q)])
        p_parts.append(jnp.einsum('isgnd,gh->sigdhn', fs, eye).reshape(N_SLAB, Q * LANES, STATE_LANES // 2))
    p_mat = jnp.concatenate(p_parts, axis=-1)
    r_parts = []
    for part, sign in ((0, 1.0), (1, -1.0)):
        es = jnp.stack([split(e[j + 1][part]) for j in range(Q)])
        r_parts.append(sign * jnp.einsum('jsgcn,gh->shnjgc', es, eye).reshape(
            N_SLAB, STATE_LANES // 2, Q * LANES))
    r_mat = jnp.concatenate(r_parts, axis=1)
    mp = jnp.concatenate([m_mat, p_mat], axis=-1).astype(BF16)
    a_q = pw[Q]
    return (mp, r_mat.astype(BF16),
            a_q[0].reshape(N_SLAB, STATE_LANES // 2), a_q[1].reshape(N_SLAB, STATE_LANES // 2))


def _router_weights(w_rg, b_rg, w_re, b_re):
    pad_g = LANE_EXP0 - LANE_GRP0 - N_GROUPS_MOE
    pad_e = LANES - LANE_EXP0 - N_EXPERTS
    w = jnp.concatenate([w_rg, jnp.zeros((D_MODEL, pad_g), F32), w_re, jnp.zeros((D_MODEL, pad_e), F32)], axis=1)
    b = jnp.concatenate([b_rg, jnp.zeros((pad_g,), F32), b_re, jnp.zeros((pad_e,), F32)]).reshape(1, LANES)
    w_hi = w.astype(BF16)
    w_lo = (w - w_hi.astype(F32)).astype(BF16)
    return jnp.concatenate([w_hi, w_lo], axis=1), w_hi, b


def _time_major_permutation():
    tm = jnp.arange(TM, dtype=I32)
    src = (tm % BATCH) * TT + tm // BATCH
    perm = (src[:, None] == jnp.arange(TM, dtype=I32)[None, :]).astype(BF16)
    return perm, perm.T


def _sc_mesh():
    return plsc.VectorSubcoreMesh(core_axis_name="core", subcore_axis_name="subcore")


def _sc_worker(mesh):
    return lax.axis_index("core") * mesh.num_subcores + lax.axis_index("subcore")


def _dispatch(h2p, dest):
    mesh = _sc_mesh()
    n_win = N_TOK // SC_WINDOW
    per_worker = n_win // (mesh.num_cores * mesh.num_subcores)
    assert per_worker * mesh.num_cores * mesh.num_subcores == n_win

    @pl.kernel(out_type=jax.ShapeDtypeStruct((N_ROWS,) + ROW_TILE, U32), mesh=mesh,
               scratch_types=[pltpu.VMEM((SC_WINDOW,), I32), pltpu.VMEM((SC_WINDOW,) + ROW_TILE, U32)])
    def scatter_rows(h_hbm, dest_hbm, xs_hbm, idx_v, rows_v):
        first = _sc_worker(mesh) * per_worker

        @pl.loop(0, per_worker)
        def _(w):
            win = first + w
            pltpu.sync_copy(h_hbm.at[pl.ds(win * SC_WINDOW, SC_WINDOW)], rows_v)
            for j in range(TOPK):
                pltpu.sync_copy(dest_hbm.at[j, win], idx_v)
                pltpu.sync_copy(rows_v, xs_hbm.at[idx_v])

    return scatter_rows(h2p, dest)


def _collect(ys, dest):
    mesh = _sc_mesh()
    n_tok = dest.shape[1]
    n_win = TOPK * n_tok // SC_WINDOW
    per_worker = n_win // (mesh.num_cores * mesh.num_subcores)
    assert per_worker * mesh.num_cores * mesh.num_subcores == n_win

    @pl.kernel(out_type=jax.ShapeDtypeStruct((TOPK * n_tok,) + ROW_TILE, U32), mesh=mesh,
               scratch_types=[pltpu.VMEM((SC_WINDOW,), I32), pltpu.VMEM((SC_WINDOW,) + ROW_TILE, U32)])
    def gather_rows(ys_hbm, dest_hbm, yg_hbm, idx_v, rows_v):
        first = _sc_worker(mesh) * per_worker

        @pl.loop(0, per_worker)
        def _(w):
            win = first + w
            pltpu.sync_copy(dest_hbm.at[win], idx_v)
            pltpu.sync_copy(ys_hbm.at[idx_v], rows_v)
            pltpu.sync_copy(rows_v, yg_hbm.at[pl.ds(win * SC_WINDOW, SC_WINDOW)])

    return gather_rows(ys, dest.reshape(n_win, SC_WINDOW)).reshape((TOPK, n_tok) + ROW_TILE)


def _expert_kernel(first_ref, nblk_ref, nvalid_ref, nused_ref, xs_hbm, wg_ref, wu_ref, wd_ref, ys_hbm,
                   wg_scr, wu_scr, wd_scr, x_buf, y_buf, in_sem, out_sem):
    e = pl.program_id(0)
    nused = nused_ref[0]

    def in_copy(g):
        slot = lax.rem(g, IN_SLOTS)
        return pltpu.make_async_copy(xs_hbm.at[pl.ds(g * BM, BM)], x_buf.at[slot], in_sem.at[slot])

    def out_copy(g, slot):
        return pltpu.make_async_copy(y_buf.at[slot], ys_hbm.at[pl.ds(g * BM, BM)], out_sem.at[slot])

    @pl.when(e == 0)
    def _first():
        for g in range(IN_AHEAD):
            in_copy(g).start()

    wg_scr[...] = wg_ref[0].astype(BF16)
    wu_scr[...] = wu_ref[0].astype(BF16)
    wd_scr[...] = wd_ref[0].astype(BF16)

    def block(b, carry):
        g = first_ref[e] + b
        slot = lax.rem(g, 2)
        in_copy(g).wait()

        @pl.when(g + IN_AHEAD < nused)
        def _prefetch():
            in_copy(g + IN_AHEAD).start()

        @pl.when(g >= 2)
        def _slot_free():
            out_copy(g - 2, slot).wait()

        valid = lax.broadcasted_iota(I32, (BM, 1), 0) < nvalid_ref[g]
        x_blk = x_buf[lax.rem(g, IN_SLOTS)].reshape(BM, HALF)
        lo, hi = _unpack_bf16_pair(jnp.where(valid, x_blk, jnp.uint32(0)))
        lo = lo.astype(BF16)
        hi = hi.astype(BF16)
        gate = jnp.dot(lo, wg_scr[0:HALF, :], preferred_element_type=F32) \
            + jnp.dot(hi, wg_scr[HALF:, :], preferred_element_type=F32)
        up = jnp.dot(lo, wu_scr[0:HALF, :], preferred_element_type=F32) \
            + jnp.dot(hi, wu_scr[HALF:, :], preferred_element_type=F32)
        act = (jax.nn.silu(gate) * up).astype(BF16)
        o = jnp.dot(act, wd_scr[...], preferred_element_type=F32)
        y_buf[slot] = _pack_bf16_pair(o[:, 0:HALF], o[:, HALF:]).reshape((BM,) + ROW_TILE)
        out_copy(g, slot).start()
        return carry

    lax.fori_loop(0, nblk_ref[e], block, 0)

    @pl.when(e == N_EXPERTS - 1)
    def _drain():
        out_copy(nused - 2, lax.rem(nused, 2)).wait()
        out_copy(nused - 1, 1 - lax.rem(nused, 2)).wait()


def _experts(first, nblk, nvalid, nused, xs, wg, wu, wd):
    grid_spec = pltpu.PrefetchScalarGridSpec(
        num_scalar_prefetch=4,
        grid=(N_EXPERTS,),
        in_specs=[
            pl.BlockSpec(memory_space=pl.ANY),
            pl.BlockSpec((1, D_MODEL, D_EXPERT), lambda e, *_: (e, 0, 0)),
            pl.BlockSpec((1, D_MODEL, D_EXPERT), lambda e, *_: (e, 0, 0)),
            pl.BlockSpec((1, D_EXPERT, D_MODEL), lambda e, *_: (e, 0, 0)),
        ],
        out_specs=pl.BlockSpec(memory_space=pl.ANY),
        scratch_shapes=[
            pltpu.VMEM((D_MODEL, D_EXPERT), BF16),
            pltpu.VMEM((D_MODEL, D_EXPERT), BF16),
            pltpu.VMEM((D_EXPERT, D_MODEL), BF16),
            pltpu.VMEM((IN_SLOTS, BM) + ROW_TILE, U32),
            pltpu.VMEM((2, BM) + ROW_TILE, U32),
            pltpu.SemaphoreType.DMA((IN_SLOTS,)),
            pltpu.SemaphoreType.DMA((2,)),
        ],
    )
    return pl.pallas_call(
        _expert_kernel,
        grid_spec=grid_spec,
        out_shape=jax.ShapeDtypeStruct((N_ROWS,) + ROW_TILE, U32),
        compiler_params=pltpu.CompilerParams(
            dimension_semantics=("arbitrary",), vmem_limit_bytes=VMEM_LIMIT),
        name="experts",
    )(first, nblk, nvalid, nused, xs, wg, wu, wd)


def _combine_kernel(x1_ref, rec_ref, yg_ref, p_ref, gple_ref, wpg_ref, wple_ref, gfin_ref, *rest):
    out_ref = rest[-1]
    ple = jnp.dot(p_ref[0].reshape(TM, D_PLE).astype(BF16), wple_ref[...], preferred_element_type=F32)
    rec = rec_ref[...]
    w0 = rec[:, REC_W0:REC_W0 + 1]
    w1 = rec[:, REC_W1:REC_W1 + 1]
    lo0, hi0 = _unpack_bf16_pair(yg_ref[0].reshape(TM, HALF))
    lo1, hi1 = _unpack_bf16_pair(yg_ref[1].reshape(TM, HALF))
    moe = jnp.concatenate([lo0 * w0 + lo1 * w1, hi0 * w0 + hi1 * w1], axis=1)
    x2 = x1_ref[...].reshape(TM, D_MODEL) + moe
    gate = jax.nn.sigmoid(jnp.dot(_rms(x2, gple_ref[...]).astype(BF16), wpg_ref[...],
                                  preferred_element_type=F32))
    x3 = x2 + gate * ple
    out_ref[...] = _rms(x3, gfin_ref[...]).reshape(BATCH, TT, D_MODEL)


def _combine(part, x1, rec, yg, p, gple, wpg, wple, gfin, out_prev=None):
    s0 = part * PART_STEPS
    seq_spec = pl.BlockSpec((BATCH, TT, D_MODEL), lambda i: (0, s0 + i, 0))
    in_specs = [
        seq_spec,
        pl.BlockSpec((TM, LANES), lambda i: (s0 + i, 0)),
        pl.BlockSpec((TOPK, TM) + ROW_TILE, lambda i: (0, i, 0, 0)),
        pl.BlockSpec((1, BATCH, TT, D_PLE), lambda i: (0, 0, s0 + i, 0)),
        _const_spec((1, D_MODEL)),
        _const_spec((D_MODEL, D_MODEL)),
        _const_spec((D_PLE, D_MODEL)),
        _const_spec((1, D_MODEL)),
    ]
    args = [x1, rec, yg, p, gple, wpg, wple, gfin]
    aliases = {}
    if out_prev is not None:
        in_specs.append(pl.BlockSpec(memory_space=pl.ANY))
        args.append(out_prev)
        aliases = {len(args) - 1: 0}
    return pl.pallas_call(
        _combine_kernel,
        grid=(PART_STEPS,),
        in_specs=in_specs,
        out_specs=seq_spec,
        out_shape=jax.ShapeDtypeStruct((BATCH, SEQ, D_MODEL), F32),
        input_output_aliases=aliases,
        compiler_params=pltpu.CompilerParams(
            dimension_semantics=("arbitrary",), vmem_limit_bytes=VMEM_LIMIT),
        name="combine",
    )(*args)


def kernel(x, p, g_mix, w_in, b_gate, ssm_a_re, ssm_a_im, ssm_log_dt, ssm_b_re, ssm_b_im, ssm_c_re,
           ssm_c_im, ssm_d, w_glu, conv_dw, conv_dw_b, conv_ln_g, conv_ln_b, w_conv_out, w_out, g_moe,
           w_router_group, b_router_group, w_router_expert, b_router_expert, w_exp_gate, w_exp_up,
           w_exp_down, g_ple, w_ple_gate, w_ple, g_final):
    assert x.shape == (BATCH, SEQ, D_MODEL) and p.shape == (1, BATCH, SEQ, D_PLE)
    row = lambda v: v.reshape(1, -1)

    mp, rmat, a_re, a_im = _ssm_matrices(ssm_a_re[0], ssm_a_im[0], ssm_log_dt[0], ssm_b_re[0],
                                         ssm_b_im[0], ssm_c_re[0], ssm_c_im[0])
    wr1, wr2, br = _router_weights(w_router_group[0], b_router_group[0], w_router_expert[0],
                                   b_router_expert[0])
    perm, permt = _time_major_permutation()
    x1, h2p, rec, rect, cnt = _mixer(
        x, row(g_mix[0]), w_in[0].astype(BF16), row(b_gate[0]), perm, permt, mp, rmat, a_re, a_im,
        row(ssm_d[0]), w_glu[0].astype(BF16), conv_dw[0], row(conv_dw_b[0]), row(conv_ln_g[0]),
        row(conv_ln_b[0]), w_conv_out[0].astype(BF16), w_out[0].astype(BF16), row(g_moe[0]), wr1, wr2, br)

    counts = cnt[0, LANE_EXP0:LANE_EXP0 + N_EXPERTS].astype(I32)
    pcounts = (counts + BM - 1) // BM * BM
    pends = jnp.cumsum(pcounts)
    pstarts = pends - pcounts
    eid = rect[REC_EID0:REC_EID1 + 1].astype(I32)
    rank = rect[REC_RANK0:REC_RANK1 + 1].astype(I32)
    dest = (jnp.sum(jnp.where(eid[..., None] == jnp.arange(N_EXPERTS, dtype=I32), pstarts, 0), axis=-1)
            + rank).reshape(TOPK, N_TOK // SC_WINDOW, SC_WINDOW)
    nused = (pends[-1] // BM).astype(I32)
    blk = jnp.arange(N_BLK, dtype=I32)[:, None] * BM
    in_expert = (pstarts[None, :] <= blk) & (blk < pends[None, :])
    nvalid = jnp.clip(jnp.sum(jnp.where(in_expert, (pstarts + counts)[None, :] - blk, 0), axis=1), 0, BM)

    xs = _dispatch(h2p, dest)
    ys = _experts(pstarts // BM, pcounts // BM, nvalid.astype(I32), nused.reshape(1), xs,
                  w_exp_gate[0], w_exp_up[0], w_exp_down[0])
    dest_tok = dest.reshape(TOPK, N_TOK)
    wpg = w_ple_gate[0].astype(BF16)
    wple = w_ple[0].astype(BF16)
    out = None
    for part in range(N_PARTS):
        tok = slice(part * PART_STEPS * TM, (part + 1) * PART_STEPS * TM)
        yg = _collect(ys, dest_tok[:, tok])
        out = _combine(part, x1, rec, yg, p, row(g_ple[0]), wpg, wple, row(g_final), out)
    return out
```

```python
import jax
import jax.numpy as jnp
from jax import lax
from jax.experimental import pallas as pl
from jax.experimental.pallas import tpu as pltpu
from jax.experimental.pallas import tpu_sc as plsc

F32 = jnp.float32
BF16 = jnp.bfloat16
U32 = jnp.uint32
I32 = jnp.int32

D_MODEL = 1024
BATCH = 8
SEQ = 2048
N_TOK = BATCH * SEQ
D_SSM = 512
SSM_GROUP_WIDTH = 16
SSM_GROUPS = 32
SSM_STATE = 64
D_CONV = 512
CONV_WIDTH = 31
D_IN = D_SSM + 2 * D_CONV + 2 * D_MODEL
N_GROUPS_MOE = 4
EXPERTS_PER_GROUP = 8
N_EXPERTS = 32
TOPK = 2
D_EXPERT = 512
D_PLE = 256
EPS = 1e-6

SUBLANES = 8
LANES = 128
assert BATCH == SUBLANES

TT = 64
TM = TT * BATCH
N_STEP = SEQ // TT
SB = 512
NSB = TM // SB
BPS = SB // TT
Q = 2
N_SLAB = D_SSM // LANES
GROUPS_PER_SLAB = SSM_GROUPS // N_SLAB
ROWS_Z = TM // Q
STATE_LANES = 2 * GROUPS_PER_SLAB * SSM_STATE
HALO = (CONV_WIDTH - 1) * BATCH
CHUNK_ROWS = SB // (Q * SUBLANES)
CONV_ROWS = 64
N_LC = D_CONV // LANES

LANE_GRP0 = 0
LANE_EXP0 = 32
REC_EID0, REC_EID1, REC_W0, REC_W1, REC_RANK0, REC_RANK1 = 0, 1, 2, 3, 4, 5
REC_ROWS = 8

BM = 256
N_BLK = (TOPK * N_TOK + N_EXPERTS * (BM - 1) + BM - 1) // BM
N_ROWS = N_BLK * BM
HALF = D_MODEL // 2
ROW_TILE = (HALF // LANES, LANES)
SC_WINDOW = 64
IN_AHEAD = 3
IN_SLOTS = IN_AHEAD + 2
N_PARTS = 2
PART_STEPS = N_STEP // N_PARTS

VMEM_LIMIT = 56 * 1024 * 1024


def _const_spec(shape):
    n = len(shape)
    return pl.BlockSpec(shape, lambda *_: (0,) * n, pipeline_mode=pl.Buffered(1))


def _rms(x, g):
    ms = jnp.mean(x * x, axis=-1, keepdims=True)
    return x * lax.rsqrt(ms + EPS) * g


def _pack_bf16_pair(lo, hi):
    ulo = lax.bitcast_convert_type(lo.astype(BF16).astype(F32), U32)
    uhi = lax.bitcast_convert_type(hi.astype(BF16).astype(F32), U32)
    return (ulo >> 16) | (uhi & jnp.uint32(0xFFFF0000))


def _unpack_bf16_pair(w):
    lo = lax.bitcast_convert_type(w << 16, F32)
    hi = lax.bitcast_convert_type(w & jnp.uint32(0xFFFF0000), F32)
    return lo, hi


def _mixer_kernel(x_ref, gmix_ref, win_ref, bgate_ref, mp_ref, r_ref, are_ref,
                  aim_ref, d_ref, wglu_ref, dw_ref, dwb_ref, lng_ref, lnb_ref, wco_ref, wout_ref,
                  gmoe_ref, wr1_ref, wr2_ref, br_ref,
                  x1_ref, h2p_ref, rec_ref, rect_ref, cnt_ref,
                  hb_scr, ht_scr, u_scr, y_scr, yi_scr, xs_scr, z_scr, conv_scr, act_scr, actb_scr,
                  logit_scr, s_scr, cnt_scr):
    step = pl.program_id(0)
    assert NSB == 1

    @pl.when(step == 0)
    def _init():
        logit_scr[...] = jnp.zeros(logit_scr.shape, F32)
        z_scr[:, 0:HALO, :] = jnp.zeros((N_LC, HALO, LANES), F32)
        s_scr[...] = jnp.zeros(s_scr.shape, F32)
        cnt_scr[...] = jnp.zeros(cnt_scr.shape, F32)

    def sub_rows(r):
        return pl.ds(pl.multiple_of(r * SB, SB), SB)

    def phase_a(r, carry):
        xb = x_ref[pl.ds(r * BPS, BPS)].reshape(SB, D_MODEL)
        hb_scr[sub_rows(r), :] = _rms(xb, gmix_ref[...]).astype(BF16)
        return carry

    def phase_a3(r, carry):
        h = ht_scr[sub_rows(r), :]
        u = jnp.dot(h, win_ref[:, 0:D_SSM], preferred_element_type=F32)
        u_scr[pl.ds(r * CHUNK_ROWS, CHUNK_ROWS)] = u.reshape(CHUNK_ROWS, Q, SUBLANES, D_SSM)
        v = jnp.dot(h, win_ref[:, D_SSM:D_SSM + 2 * D_CONV], preferred_element_type=F32)
        zc = v[:, 0:D_CONV] * jax.nn.sigmoid(v[:, D_CONV:])
        for lc in range(N_LC):
            z_scr[lc, pl.ds(pl.multiple_of(HALO + r * SB, SUBLANES), SB), :] = zc[:, lc * LANES:(lc + 1) * LANES]
        return carry

    def phase_b():
        for s in range(N_SLAB):
            lanes = slice(s * LANES, (s + 1) * LANES)
            z = jnp.concatenate(
                [u_scr[:, i, :, lanes].reshape(ROWS_Z, LANES) for i in range(Q)], axis=1).astype(BF16)
            xp = jnp.dot(z, mp_ref[s], preferred_element_type=F32)
            yi_scr[s] = xp[:, 0:Q * LANES]
            xs_scr[s] = xp[:, Q * LANES:]

        half = STATE_LANES // 2
        for s in range(N_SLAB):
            a_re = jnp.broadcast_to(are_ref[s:s + 1, :], (SUBLANES, half))
            a_im = jnp.broadcast_to(aim_ref[s:s + 1, :], (SUBLANES, half))

            def scan_body(k, carry, s=s, a_re=a_re, a_im=a_im):
                s_re, s_im = carry
                rows = pl.ds(pl.multiple_of(k * SUBLANES, SUBLANES), SUBLANES)
                x_re = xs_scr[s, rows, 0:half]
                x_im = xs_scr[s, rows, half:]
                xs_scr[s, rows, 0:half] = s_re
                xs_scr[s, rows, half:] = s_im
                n_re = a_re * s_re - a_im * s_im + x_re
                n_im = a_re * s_im + a_im * s_re + x_im
                return n_re, n_im

            s_re, s_im = lax.fori_loop(0, ROWS_Z // SUBLANES, scan_body,
                                       (s_scr[s, :, 0:half], s_scr[s, :, half:]), unroll=True)
            s_scr[s, :, 0:half] = s_re
            s_scr[s, :, half:] = s_im

        for s in range(N_SLAB):
            lanes = slice(s * LANES, (s + 1) * LANES)
            y_tot = yi_scr[s] + jnp.dot(xs_scr[s].astype(BF16), r_ref[s], preferred_element_type=F32)
            for j in range(Q):
                y_scr[:, j, :, lanes] = y_tot[:, j * LANES:(j + 1) * LANES].reshape(
                    ROWS_Z // SUBLANES, SUBLANES, LANES)

    def phase_c1(r, carry):
        rows = sub_rows(r)
        crow = pl.ds(r * CHUNK_ROWS, CHUNK_ROWS)
        y = y_scr[crow].reshape(SB, D_SSM) + d_ref[...] * u_scr[crow].reshape(SB, D_SSM)
        act_scr[rows, 0:D_SSM] = jax.nn.gelu(y).astype(BF16)
        for lc in range(N_LC):
            lanes = slice(lc * LANES, (lc + 1) * LANES)

            def conv_piece(rc, c, lc=lc, lanes=lanes):
                r0 = r * SB + rc * CONV_ROWS
                piece = jnp.broadcast_to(dwb_ref[:, lanes], (CONV_ROWS, LANES))
                for j in range(CONV_WIDTH):
                    zrows = pl.ds(pl.multiple_of(r0 + j * BATCH, SUBLANES), CONV_ROWS)
                    piece = piece + dw_ref[j:j + 1, lanes] * z_scr[lc, zrows, :]
                conv_scr[pl.ds(pl.multiple_of(rc * CONV_ROWS, CONV_ROWS), CONV_ROWS), lanes] = piece
                return c

            lax.fori_loop(0, SB // CONV_ROWS, conv_piece, 0, unroll=4)
        acc = conv_scr[...]
        mu = jnp.mean(acc, axis=-1, keepdims=True)
        cen = acc - mu
        var = jnp.mean(cen * cen, axis=-1, keepdims=True)
        ln = cen * lax.rsqrt(var + EPS) * lng_ref[...] + lnb_ref[...]
        act_scr[rows, D_SSM:] = jax.nn.silu(ln).astype(BF16)
        return carry

    lane = lax.broadcasted_iota(I32, (1, LANES), 1).astype(F32)
    grp_mask = lane < float(N_GROUPS_MOE)
    exp_lane = (lane >= float(LANE_EXP0)) & (lane < float(LANE_EXP0 + N_EXPERTS))
    lane_grp = jnp.floor((lane - float(LANE_EXP0)) * (1.0 / EXPERTS_PER_GROUP))
    tri = (lax.broadcasted_iota(I32, (SB, SB), 0) > lax.broadcasted_iota(I32, (SB, SB), 1)).astype(BF16)
    neg_inf = float("-inf")
    big = float(4 * LANES)

    def phase_c3(r, carry):
        rows = sub_rows(r)
        h = hb_scr[rows, :]
        g0 = D_SSM + 2 * D_CONV
        gate_ssm = jnp.dot(h, win_ref[:, g0:g0 + D_MODEL], preferred_element_type=F32) \
            + bgate_ref[:, 0:D_MODEL]
        gate_conv = jnp.dot(h, win_ref[:, g0 + D_MODEL:], preferred_element_type=F32) \
            + bgate_ref[:, D_MODEL:]
        zz = jnp.dot(actb_scr[rows, 0:D_SSM], wglu_ref[...], preferred_element_type=F32)
        y_ssm = zz[:, 0:D_MODEL] * jax.nn.sigmoid(zz[:, D_MODEL:])
        y_conv = jnp.dot(actb_scr[rows, D_SSM:], wco_ref[...], preferred_element_type=F32)

        merged = jax.nn.sigmoid(gate_ssm) * y_ssm + jax.nn.sigmoid(gate_conv) * y_conv
        xb = x_ref[pl.ds(r * BPS, BPS)].reshape(SB, D_MODEL)
        x1 = xb + jnp.dot(merged.astype(BF16), wout_ref[...], preferred_element_type=F32)
        x1_ref[pl.ds(r * BPS, BPS)] = x1.reshape(BPS, TT, D_MODEL)

        h2 = _rms(x1, gmoe_ref[...])
        h2p_ref[rows] = _pack_bf16_pair(h2[:, 0:HALF], h2[:, HALF:]).reshape((SB,) + ROW_TILE)

        h2_hi = h2.astype(BF16)
        h2_lo = (h2 - h2_hi.astype(F32)).astype(BF16)
        l1 = jnp.dot(h2_hi, wr1_ref[...], preferred_element_type=F32)
        l2 = jnp.dot(h2_lo, wr2_ref[...], preferred_element_type=F32)
        logit_scr[rows, :] = l1[:, 0:LANES] + l1[:, LANES:] + l2 + br_ref[...]
        return carry

    def route_previous():
        rows = sub_rows(0)
        logits = logit_scr[...]
        counted = jnp.where(step > 0, 1.0, 0.0)

        lg = jnp.where(grp_mask, logits, neg_inf)
        g_max = jnp.max(lg, axis=-1, keepdims=True)
        g_sel = jnp.min(jnp.where(lg == g_max, lane, big), axis=-1, keepdims=True)
        p_g = 1.0 / jnp.sum(jnp.where(grp_mask, jnp.exp(logits - g_max), 0.0), axis=-1, keepdims=True)
        le = jnp.where(exp_lane & (lane_grp == g_sel), logits, neg_inf)
        m1 = jnp.max(le, axis=-1, keepdims=True)
        i1 = jnp.min(jnp.where(le == m1, lane, big), axis=-1, keepdims=True)
        le2 = jnp.where(lane == i1, neg_inf, le)
        m2 = jnp.max(le2, axis=-1, keepdims=True)
        i2 = jnp.min(jnp.where(le2 == m2, lane, big), axis=-1, keepdims=True)
        e2 = jnp.exp(m2 - m1)
        den = 1.0 + e2
        w_a = (1.0 / den) * p_g
        w_b = (e2 / den) * p_g

        sel1 = lane == i1
        sel2 = lane == i2
        onehot = jnp.where(sel1 | sel2, counted, 0.0)
        prefix = jnp.dot(tri, onehot.astype(BF16), preferred_element_type=F32) + cnt_scr[...]
        rank_a = jnp.sum(jnp.where(sel1, prefix, 0.0), axis=-1, keepdims=True)
        rank_b = jnp.sum(jnp.where(sel2, prefix, 0.0), axis=-1, keepdims=True)
        cnt_scr[...] = cnt_scr[...] + jnp.sum(onehot, axis=0, keepdims=True)

        rec = jnp.where(lane == float(REC_EID0), i1 - float(LANE_EXP0), 0.0)
        rec = jnp.where(lane == float(REC_EID1), i2 - float(LANE_EXP0), rec)
        rec = jnp.where(lane == float(REC_W0), w_a, rec)
        rec = jnp.where(lane == float(REC_W1), w_b, rec)
        rec = jnp.where(lane == float(REC_RANK0), rank_a, rec)
        rec = jnp.where(lane == float(REC_RANK1), rank_b, rec)
        rec_ref[rows, :] = rec
        rect_ref[...] = jnp.transpose(rec)[0:REC_ROWS, :]
        cnt_ref[...] = cnt_scr[...]

    @pl.when(step < N_STEP)
    def _tile():
        route_previous()
        phase_a(0, 0)
        ht_scr[...] = jnp.swapaxes(hb_scr[...].reshape(BATCH, TT, D_MODEL), 0, 1).reshape(TM, D_MODEL)
        phase_a3(0, 0)
        phase_b()
        phase_c1(0, 0)
        z_scr[:, 0:HALO, :] = z_scr[:, TM:TM + HALO, :]
        actb_scr[...] = jnp.swapaxes(act_scr[...].reshape(TT, BATCH, D_SSM + D_CONV), 0, 1).reshape(
            TM, D_SSM + D_CONV)
        phase_c3(0, 0)

    @pl.when(step == N_STEP)
    def _last():
        route_previous()


def _mixer(x, gmix, win, bgate, mp, rmat, a_re, a_im, dvec, wglu, dw, dwb, lng, lnb, wco,
           wout, gmoe, wr1, wr2, br):
    tile = lambda i: jnp.minimum(i, N_STEP - 1)
    routed = lambda i: jnp.maximum(i - 1, 0)
    seq_spec = pl.BlockSpec((BATCH, TT, D_MODEL), lambda i: (0, tile(i), 0))
    in_specs = [
        seq_spec,
        _const_spec((1, D_MODEL)),
        _const_spec((D_MODEL, D_IN)),
        _const_spec((1, 2 * D_MODEL)),
        _const_spec(mp.shape),
        _const_spec(rmat.shape),
        _const_spec(a_re.shape),
        _const_spec(a_im.shape),
        _const_spec((1, D_SSM)),
        _const_spec((D_SSM, 2 * D_MODEL)),
        _const_spec((CONV_WIDTH, D_CONV)),
        _const_spec((1, D_CONV)),
        _const_spec((1, D_CONV)),
        _const_spec((1, D_CONV)),
        _const_spec((D_CONV, D_MODEL)),
        _const_spec((D_MODEL, D_MODEL)),
        _const_spec((1, D_MODEL)),
        _const_spec((D_MODEL, 2 * LANES)),
        _const_spec((D_MODEL, LANES)),
        _const_spec((1, LANES)),
    ]
    out_specs = [
        seq_spec,
        pl.BlockSpec((TM,) + ROW_TILE, lambda i: (tile(i), 0, 0)),
        pl.BlockSpec((TM, LANES), lambda i: (routed(i), 0)),
        pl.BlockSpec((REC_ROWS, TM), lambda i: (0, routed(i))),
        pl.BlockSpec((1, LANES), lambda i: (0, 0)),
    ]
    out_shape = [
        jax.ShapeDtypeStruct((BATCH, SEQ, D_MODEL), F32),
        jax.ShapeDtypeStruct((N_TOK,) + ROW_TILE, U32),
        jax.ShapeDtypeStruct((N_TOK, LANES), F32),
        jax.ShapeDtypeStruct((REC_ROWS, N_TOK), F32),
        jax.ShapeDtypeStruct((1, LANES), F32),
    ]
    chunk_shape = (ROWS_Z // SUBLANES, Q, SUBLANES, D_SSM)
    scratch = [
        pltpu.VMEM((TM, D_MODEL), BF16),
        pltpu.VMEM((TM, D_MODEL), BF16),
        pltpu.VMEM(chunk_shape, F32),
        pltpu.VMEM(chunk_shape, F32),
        pltpu.VMEM((N_SLAB, ROWS_Z, Q * LANES), F32),
        pltpu.VMEM((N_SLAB, ROWS_Z, STATE_LANES), F32),
        pltpu.VMEM((N_LC, HALO + TM, LANES), F32),
        pltpu.VMEM((SB, D_CONV), F32),
        pltpu.VMEM((TM, D_SSM + D_CONV), BF16),
        pltpu.VMEM((TM, D_SSM + D_CONV), BF16),
        pltpu.VMEM((TM, LANES), F32),
        pltpu.VMEM((N_SLAB, SUBLANES, STATE_LANES), F32),
        pltpu.VMEM((1, LANES), F32),
    ]
    return pl.pallas_call(
        _mixer_kernel,
        grid=(N_STEP + 1,),
        in_specs=in_specs,
        out_specs=out_specs,
        out_shape=out_shape,
        scratch_shapes=scratch,
        compiler_params=pltpu.CompilerParams(
            dimension_semantics=("arbitrary",), vmem_limit_bytes=VMEM_LIMIT),
        name="mixer",
    )(x, gmix, win, bgate, mp, rmat, a_re, a_im, dvec, wglu, dw, dwb, lng, lnb, wco, wout,
      gmoe, wr1, wr2, br)


def _cmul(a, b):
    return a[0] * b[0] - a[1] * b[1], a[0] * b[1] + a[1] * b[0]


def _ssm_matrices(a_re, a_im, log_dt, b_re, b_im, c_re, c_im):
    dt = jnp.exp(log_dt)[:, None]
    mag = jnp.exp(a_re * dt)
    lam = (mag * jnp.cos(a_im * dt), mag * jnp.sin(a_im * dt))
    den = a_re * a_re + a_im * a_im
    nr = lam[0] - 1.0
    ni = lam[1]
    z_re = (nr * a_re + ni * a_im) / den
    z_im = (ni * a_re - nr * a_im) / den
    bbar = (z_re[..., None] * b_re - z_im[..., None] * b_im,
            z_re[..., None] * b_im + z_im[..., None] * b_re)
    pw = [(jnp.ones_like(lam[0]), jnp.zeros_like(lam[0])), lam]
    for _ in range(2, Q + 1):
        pw.append(_cmul(pw[-1], lam))
    e = [(c_re * p[0][:, None, :] - c_im * p[1][:, None, :],
          c_re * p[1][:, None, :] + c_im * p[0][:, None, :]) for p in pw]
    hp = lax.Precision.HIGHEST
    k = [jnp.einsum('gcn,gnd->gcd', e[m][0], bbar[0], precision=hp)
         - jnp.einsum('gcn,gnd->gcd', e[m][1], bbar[1], precision=hp) for m in range(Q)]
    eye = jnp.eye(GROUPS_PER_SLAB, dtype=F32)
    split = lambda t: t.reshape((N_SLAB, GROUPS_PER_SLAB) + t.shape[1:])
    zero_k = jnp.zeros_like(k[0])
    kb = jnp.stack([jnp.stack([split(jnp.swapaxes(k[j - i] if j >= i else zero_k, 1, 2))
                               for j in range(Q)]) for i in range(Q)])
    m_mat = jnp.einsum('ijsgdc,gh->sigdjhc', kb, eye).reshape(N_SLAB, Q * LANES, Q * LANES)
    f = [_cmul((pw[Q - 1 - i][0][..., None], pw[Q - 1 - i][1][..., None]), bbar) for i in range(Q)]
    p_parts = []
    for part in range(2):
        fs = jnp.stack([split(f[i][part]) for i in range(Q)])
        p_parts.append(jnp.einsum('isgnd,gh->sigdhn', fs, eye).reshape(N_SLAB, Q * LANES, STATE_LANES // 2))
    p_mat = jnp.concatenate(p_parts, axis=-1)
    r_parts = []
    for part, sign in ((0, 1.0), (1, -1.0)):
        es = jnp.stack([split(e[j + 1][part]) for j in range(Q)])
        r_parts.append(sign * jnp.einsum('jsgcn,gh->shnjgc', es, eye).reshape(
            N_SLAB, STATE_LANES // 2, Q * LANES))
    r_mat = jnp.concatenate(r_parts, axis=1)
    mp = jnp.concatenate([m_mat, p_mat], axis=-1).astype(BF16)
    a_q = pw[Q]
    return (mp, r_mat.astype(BF16),
            a_q[0].reshape(N_SLAB, STATE_LANES // 2), a_q[1].reshape(N_SLAB, STATE_LANES // 2))


def _router_weights(w_rg, b_rg, w_re, b_re):
    pad_g = LANE_EXP0 - LANE_GRP0 - N_GROUPS_MOE
    pad_e = LANES - LANE_EXP0 - N_EXPERTS
    w = jnp.concatenate([w_rg, jnp.zeros((D_MODEL, pad_g), F32), w_re, jnp.zeros((D_MODEL, pad_e), F32)], axis=1)
    b = jnp.concatenate([b_rg, jnp.zeros((pad_g,), F32), b_re, jnp.zeros((pad_e,), F32)]).reshape(1, LANES)
    w_hi = w.astype(BF16)
    w_lo = (w - w_hi.astype(F32)).astype(BF16)
    return jnp.concatenate([w_hi, w_lo], axis=1), w_hi, b


def _sc_mesh():
    return plsc.VectorSubcoreMesh(core_axis_name="core", subcore_axis_name="subcore")


def _sc_worker(mesh):
    return lax.axis_index("core") * mesh.num_subcores + lax.axis_index("subcore")


def _dispatch(h2p, dest):
    mesh = _sc_mesh()
    n_win = N_TOK // SC_WINDOW
    per_worker = n_win // (mesh.num_cores * mesh.num_subcores)
    assert per_worker * mesh.num_cores * mesh.num_subcores == n_win

    @pl.kernel(out_type=jax.ShapeDtypeStruct((N_ROWS,) + ROW_TILE, U32), mesh=mesh,
               scratch_types=[pltpu.VMEM((SC_WINDOW,), I32), pltpu.VMEM((SC_WINDOW,) + ROW_TILE, U32)])
    def scatter_rows(h_hbm, dest_hbm, xs_hbm, idx_v, rows_v):
        first = _sc_worker(mesh) * per_worker

        @pl.loop(0, per_worker)
        def _(w):
            win = first + w
            pltpu.sync_copy(h_hbm.at[pl.ds(win * SC_WINDOW, SC_WINDOW)], rows_v)
            for j in range(TOPK):
                pltpu.sync_copy(dest_hbm.at[j, win], idx_v)
                pltpu.sync_copy(rows_v, xs_hbm.at[idx_v])

    return scatter_rows(h2p, dest)


def _collect(ys, dest):
    mesh = _sc_mesh()
    n_tok = dest.shape[1]
    n_win = TOPK * n_tok // SC_WINDOW
    per_worker = n_win // (mesh.num_cores * mesh.num_subcores)
    assert per_worker * mesh.num_cores * mesh.num_subcores == n_win

    @pl.kernel(out_type=jax.ShapeDtypeStruct((TOPK * n_tok,) + ROW_TILE, U32), mesh=mesh,
               scratch_types=[pltpu.VMEM((SC_WINDOW,), I32), pltpu.VMEM((SC_WINDOW,) + ROW_TILE, U32)])
    def gather_rows(ys_hbm, dest_hbm, yg_hbm, idx_v, rows_v):
        first = _sc_worker(mesh) * per_worker

        @pl.loop(0, per_worker)
        def _(w):
            win = first + w
            pltpu.sync_copy(dest_hbm.at[win], idx_v)
            pltpu.sync_copy(ys_hbm.at[idx_v], rows_v)
            pltpu.sync_copy(rows_v, yg_hbm.at[pl.ds(win * SC_WINDOW, SC_WINDOW)])

    return gather_rows(ys, dest.reshape(n_win, SC_WINDOW)).reshape((TOPK, n_tok) + ROW_TILE)


def _expert_kernel(first_ref, nblk_ref, nvalid_ref, nused_ref, xs_hbm, wg_ref, wu_ref, wd_ref, ys_hbm,
                   wg_scr, wu_scr, wd_scr, x_buf, y_buf, in_sem, out_sem):
    e = pl.program_id(0)
    nused = nused_ref[0]

    def in_copy(g):
        slot = lax.rem(g, IN_SLOTS)
        return pltpu.make_async_copy(xs_hbm.at[pl.ds(g * BM, BM)], x_buf.at[slot], in_sem.at[slot])

    def out_copy(g, slot):
        return pltpu.make_async_copy(y_buf.at[slot], ys_hbm.at[pl.ds(g * BM, BM)], out_sem.at[slot])

    @pl.when(e == 0)
    def _first():
        for g in range(IN_AHEAD):
            in_copy(g).start()

    wg_scr[...] = wg_ref[0].astype(BF16)
    wu_scr[...] = wu_ref[0].astype(BF16)
    wd_scr[...] = wd_ref[0].astype(BF16)

    def acquire(g):
        in_copy(g).wait()

        @pl.when(g + IN_AHEAD < nused)
        def _prefetch():
            in_copy(g + IN_AHEAD).start()

        @pl.when(g >= 2)
        def _slot_free():
            out_copy(g - 2, lax.rem(g, 2)).wait()

    def compute(g):
        slot = lax.rem(g, 2)
        valid = lax.broadcasted_iota(I32, (BM, 1), 0) < nvalid_ref[g]
        x_blk = x_buf[lax.rem(g, IN_SLOTS)].reshape(BM, HALF)
        lo, hi = _unpack_bf16_pair(jnp.where(valid, x_blk, jnp.uint32(0)))
        lo = lo.astype(BF16)
        hi = hi.astype(BF16)
        gate = jnp.dot(lo, wg_scr[0:HALF, :], preferred_element_type=F32) \
            + jnp.dot(hi, wg_scr[HALF:, :], preferred_element_type=F32)
        up = jnp.dot(lo, wu_scr[0:HALF, :], preferred_element_type=F32) \
            + jnp.dot(hi, wu_scr[HALF:, :], preferred_element_type=F32)
        act = (jax.nn.silu(gate) * up).astype(BF16)
        o = jnp.dot(act, wd_scr[...], preferred_element_type=F32)
        y_buf[slot] = _pack_bf16_pair(o[:, 0:HALF], o[:, HALF:]).reshape((BM,) + ROW_TILE)

    def pair(b2, carry):
        g = first_ref[e] + 2 * b2
        acquire(g)
        acquire(g + 1)
        compute(g)
        compute(g + 1)
        out_copy(g, lax.rem(g, 2)).start()
        out_copy(g + 1, lax.rem(g + 1, 2)).start()
        return carry

    nblk = nblk_ref[e]
    lax.fori_loop(0, nblk // 2, pair, 0)

    @pl.when(lax.rem(nblk, 2) == 1)
    def _odd_block():
        g = first_ref[e] + nblk - 1
        acquire(g)
        compute(g)
        out_copy(g, lax.rem(g, 2)).start()

    @pl.when(e == N_EXPERTS - 1)
    def _drain():
        out_copy(nused - 2, lax.rem(nused, 2)).wait()
        out_copy(nused - 1, 1 - lax.rem(nused, 2)).wait()


def _experts(first, nblk, nvalid, nused, xs, wg, wu, wd):
    grid_spec = pltpu.PrefetchScalarGridSpec(
        num_scalar_prefetch=4,
        grid=(N_EXPERTS,),
        in_specs=[
            pl.BlockSpec(memory_space=pl.ANY),
            pl.BlockSpec((1, D_MODEL, D_EXPERT), lambda e, *_: (e, 0, 0)),
            pl.BlockSpec((1, D_MODEL, D_EXPERT), lambda e, *_: (e, 0, 0)),
            pl.BlockSpec((1, D_EXPERT, D_MODEL), lambda e, *_: (e, 0, 0)),
        ],
        out_specs=pl.BlockSpec(memory_space=pl.ANY),
        scratch_shapes=[
            pltpu.VMEM((D_MODEL, D_EXPERT), BF16),
            pltpu.VMEM((D_MODEL, D_EXPERT), BF16),
            pltpu.VMEM((D_EXPERT, D_MODEL), BF16),
            pltpu.VMEM((IN_SLOTS, BM) + ROW_TILE, U32),
            pltpu.VMEM((2, BM) + ROW_TILE, U32),
            pltpu.SemaphoreType.DMA((IN_SLOTS,)),
            pltpu.SemaphoreType.DMA((2,)),
        ],
    )
    return pl.pallas_call(
        _expert_kernel,
        grid_spec=grid_spec,
        out_shape=jax.ShapeDtypeStruct((N_ROWS,) + ROW_TILE, U32),
        compiler_params=pltpu.CompilerParams(
            dimension_semantics=("arbitrary",), vmem_limit_bytes=VMEM_LIMIT),
        name="experts",
    )(first, nblk, nvalid, nused, xs, wg, wu, wd)


def _combine_kernel(x1_ref, rec_ref, yg_ref, p_ref, gple_ref, wpg_ref, wple_ref, gfin_ref, *rest):
    out_ref = rest[-1]
    ple = jnp.dot(p_ref[0].reshape(TM, D_PLE).astype(BF16), wple_ref[...], preferred_element_type=F32)
    rec = rec_ref[...]
    w0 = rec[:, REC_W0:REC_W0 + 1]
    w1 = rec[:, REC_W1:REC_W1 + 1]
    lo0, hi0 = _unpack_bf16_pair(yg_ref[0].reshape(TM, HALF))
    lo1, hi1 = _unpack_bf16_pair(yg_ref[1].reshape(TM, HALF))
    moe = jnp.concatenate([lo0 * w0 + lo1 * w1, hi0 * w0 + hi1 * w1], axis=1)
    x2 = x1_ref[...].reshape(TM, D_MODEL) + moe
    gate = jax.nn.sigmoid(jnp.dot(_rms(x2, gple_ref[...]).astype(BF16), wpg_ref[...],
                                  preferred_element_type=F32))
    x3 = x2 + gate * ple
    out_ref[...] = _rms(x3, gfin_ref[...]).reshape(BATCH, TT, D_MODEL)


def _combine(part, x1, rec, yg, p, gple, wpg, wple, gfin, out_prev=None):
    s0 = part * PART_STEPS
    seq_spec = pl.BlockSpec((BATCH, TT, D_MODEL), lambda i: (0, s0 + i, 0))
    in_specs = [
        seq_spec,
        pl.BlockSpec((TM, LANES), lambda i: (s0 + i, 0)),
        pl.BlockSpec((TOPK, TM) + ROW_TILE, lambda i: (0, i, 0, 0)),
        pl.BlockSpec((1, BATCH, TT, D_PLE), lambda i: (0, 0, s0 + i, 0)),
        _const_spec((1, D_MODEL)),
        _const_spec((D_MODEL, D_MODEL)),
        _const_spec((D_PLE, D_MODEL)),
        _const_spec((1, D_MODEL)),
    ]
    args = [x1, rec, yg, p, gple, wpg, wple, gfin]
    aliases = {}
    if out_prev is not None:
        in_specs.append(pl.BlockSpec(memory_space=pl.ANY))
        args.append(out_prev)
        aliases = {len(args) - 1: 0}
    return pl.pallas_call(
        _combine_kernel,
        grid=(PART_STEPS,),
        in_specs=in_specs,
        out_specs=seq_spec,
        out_shape=jax.ShapeDtypeStruct((BATCH, SEQ, D_MODEL), F32),
        input_output_aliases=aliases,
        compiler_params=pltpu.CompilerParams(
            dimension_semantics=("arbitrary",), vmem_limit_bytes=VMEM_LIMIT),
        name="combine",
    )(*args)


def kernel(x, p, g_mix, w_in, b_gate, ssm_a_re, ssm_a_im, ssm_log_dt, ssm_b_re, ssm_b_im, ssm_c_re,
           ssm_c_im, ssm_d, w_glu, conv_dw, conv_dw_b, conv_ln_g, conv_ln_b, w_conv_out, w_out, g_moe,
           w_router_group, b_router_group, w_router_expert, b_router_expert, w_exp_gate, w_exp_up,
           w_exp_down, g_ple, w_ple_gate, w_ple, g_final):
    assert x.shape == (BATCH, SEQ, D_MODEL) and p.shape == (1, BATCH, SEQ, D_PLE)
    row = lambda v: v.reshape(1, -1)

    mp, rmat, a_re, a_im = _ssm_matrices(ssm_a_re[0], ssm_a_im[0], ssm_log_dt[0], ssm_b_re[0],
                                         ssm_b_im[0], ssm_c_re[0], ssm_c_im[0])
    wr1, wr2, br = _router_weights(w_router_group[0], b_router_group[0], w_router_expert[0],
                                   b_router_expert[0])
    x1, h2p, rec, rect, cnt = _mixer(
        x, row(g_mix[0]), w_in[0].astype(BF16), row(b_gate[0]), mp, rmat, a_re, a_im,
        row(ssm_d[0]), w_glu[0].astype(BF16), conv_dw[0], row(conv_dw_b[0]), row(conv_ln_g[0]),
        row(conv_ln_b[0]), w_conv_out[0].astype(BF16), w_out[0].astype(BF16), row(g_moe[0]), wr1, wr2, br)

    counts = cnt[0, LANE_EXP0:LANE_EXP0 + N_EXPERTS].astype(I32)
    pcounts = (counts + BM - 1) // BM * BM
    pends = jnp.cumsum(pcounts)
    pstarts = pends - pcounts
    eid = rect[REC_EID0:REC_EID1 + 1].astype(I32)
    rank = rect[REC_RANK0:REC_RANK1 + 1].astype(I32)
    dest = (jnp.sum(jnp.where(eid[..., None] == jnp.arange(N_EXPERTS, dtype=I32), pstarts, 0), axis=-1)
            + rank).reshape(TOPK, N_TOK // SC_WINDOW, SC_WINDOW)
    nused = (pends[-1] // BM).astype(I32)
    blk = jnp.arange(N_BLK, dtype=I32)[:, None] * BM
    in_expert = (pstarts[None, :] <= blk) & (blk < pends[None, :])
    nvalid = jnp.clip(jnp.sum(jnp.where(in_expert, (pstarts + counts)[None, :] - blk, 0), axis=1), 0, BM)

    xs = _dispatch(h2p, dest)
    ys = _experts(pstarts // BM, pcounts // BM, nvalid.astype(I32), nused.reshape(1), xs,
                  w_exp_gate[0], w_exp_up[0], w_exp_down[0])
    dest_tok = dest.reshape(TOPK, N_TOK)
    wpg = w_ple_gate[0].astype(BF16)
    wple = w_ple[0].astype(BF16)
    out = None
    for part in range(N_PARTS):
        tok = slice(part * PART_STEPS * TM, (part + 1) * PART_STEPS * TM)
        yg = _collect(ys, dest_tok[:, tok])
        out = _combine(part, x1, rec, yg, p, row(g_ple[0]), wpg, wple, row(g_final), out)
    return out
```

```python
import jax
import jax.numpy as jnp
from jax import lax
from jax.experimental import pallas as pl
from jax.experimental.pallas import tpu as pltpu
from jax.experimental.pallas import tpu_sc as plsc

F32 = jnp.float32
BF16 = jnp.bfloat16
U32 = jnp.uint32
I32 = jnp.int32

D_MODEL = 1024
BATCH = 8
SEQ = 2048
N_TOK = BATCH * SEQ
D_SSM = 512
SSM_GROUP_WIDTH = 16
SSM_GROUPS = 32
SSM_STATE = 64
D_CONV = 512
CONV_WIDTH = 31
D_IN = D_SSM + 2 * D_CONV + 2 * D_MODEL
N_GROUPS_MOE = 4
EXPERTS_PER_GROUP = 8
N_EXPERTS = 32
TOPK = 2
D_EXPERT = 512
D_PLE = 256
EPS = 1e-6

SUBLANES = 8
LANES = 128
assert BATCH == SUBLANES

TT = 64
TM = TT * BATCH
N_STEP = SEQ // TT
SB = 512
NSB = TM // SB
BPS = SB // TT
Q = 2
N_SLAB = D_SSM // LANES
GROUPS_PER_SLAB = SSM_GROUPS // N_SLAB
ROWS_Z = TM // Q
STATE_LANES = 2 * GROUPS_PER_SLAB * SSM_STATE
HALO = (CONV_WIDTH - 1) * BATCH
CHUNK_ROWS = SB // (Q * SUBLANES)
CONV_ROWS = 64
N_LC = D_CONV // LANES

LANE_GRP0 = 0
LANE_EXP0 = 32
REC_EID0, REC_EID1, REC_W0, REC_W1, REC_RANK0, REC_RANK1 = 0, 1, 2, 3, 4, 5
REC_ROWS = 8

BM = 256
N_BLK = (TOPK * N_TOK + N_EXPERTS * (BM - 1) + BM - 1) // BM
N_ROWS = N_BLK * BM
HALF = D_MODEL // 2
ROW_TILE = (HALF // LANES, LANES)
SC_WINDOW = 64
IN_AHEAD = 3
IN_SLOTS = IN_AHEAD + 1
N_PARTS = 2
PART_STEPS = N_STEP // N_PARTS

VMEM_LIMIT = 56 * 1024 * 1024


def _const_spec(shape):
    n = len(shape)
    return pl.BlockSpec(shape, lambda *_: (0,) * n, pipeline_mode=pl.Buffered(1))


def _rms(x, g):
    ms = jnp.mean(x * x, axis=-1, keepdims=True)
    return x * lax.rsqrt(ms + EPS) * g


def _pack_bf16_pair(lo, hi):
    ulo = lax.bitcast_convert_type(lo.astype(BF16).astype(F32), U32)
    uhi = lax.bitcast_convert_type(hi.astype(BF16).astype(F32), U32)
    return (ulo >> 16) | (uhi & jnp.uint32(0xFFFF0000))


def _unpack_bf16_pair(w):
    lo = lax.bitcast_convert_type(w << 16, F32)
    hi = lax.bitcast_convert_type(w & jnp.uint32(0xFFFF0000), F32)
    return lo, hi


def _mixer_kernel(x_ref, gmix_ref, win_ref, bgate_ref, mp_ref, r_ref, are_ref,
                  aim_ref, d_ref, wglu_ref, dw_ref, dwb_ref, lng_ref, lnb_ref, wco_ref, wout_ref,
                  gmoe_ref, wr1_ref, wr2_ref, br_ref,
                  x1_ref, h2p_ref, rec_ref, rect_ref, cnt_ref,
                  hb_scr, ht_scr, u_scr, y_scr, yi_scr, xs_scr, z_scr, conv_scr, act_scr, actb_scr,
                  logit_scr, s_scr, cnt_scr):
    step = pl.program_id(0)
    assert NSB == 1

    @pl.when(step == 0)
    def _init():
        logit_scr[...] = jnp.zeros(logit_scr.shape, F32)
        z_scr[:, 0:HALO, :] = jnp.zeros((N_LC, HALO, LANES), F32)
        s_scr[...] = jnp.zeros(s_scr.shape, F32)
        cnt_scr[...] = jnp.zeros(cnt_scr.shape, F32)

    def sub_rows(r):
        return pl.ds(pl.multiple_of(r * SB, SB), SB)

    def phase_a(r, carry):
        xb = x_ref[pl.ds(r * BPS, BPS)].reshape(SB, D_MODEL)
        hb_scr[sub_rows(r), :] = _rms(xb, gmix_ref[...]).astype(BF16)
        return carry

    def phase_a3(r, carry):
        h = ht_scr[sub_rows(r), :]
        u = jnp.dot(h, win_ref[:, 0:D_SSM], preferred_element_type=F32)
        u_scr[pl.ds(r * CHUNK_ROWS, CHUNK_ROWS)] = u.reshape(CHUNK_ROWS, Q, SUBLANES, D_SSM)
        v = jnp.dot(h, win_ref[:, D_SSM:D_SSM + 2 * D_CONV], preferred_element_type=F32)
        zc = v[:, 0:D_CONV] * jax.nn.sigmoid(v[:, D_CONV:])
        for lc in range(N_LC):
            z_scr[lc, pl.ds(pl.multiple_of(HALO + r * SB, SUBLANES), SB), :] = zc[:, lc * LANES:(lc + 1) * LANES]
        return carry

    def phase_b():
        for s in range(N_SLAB):
            lanes = slice(s * LANES, (s + 1) * LANES)
            z = jnp.concatenate(
                [u_scr[:, i, :, lanes].reshape(ROWS_Z, LANES) for i in range(Q)], axis=1).astype(BF16)
            xp = jnp.dot(z, mp_ref[s], preferred_element_type=F32)
            yi_scr[s] = xp[:, 0:Q * LANES]
            xs_scr[s] = xp[:, Q * LANES:]

        half = STATE_LANES // 2
        for s in range(N_SLAB):
            a_re = jnp.broadcast_to(are_ref[s:s + 1, :], (SUBLANES, half))
            a_im = jnp.broadcast_to(aim_ref[s:s + 1, :], (SUBLANES, half))

            def scan_body(k, carry, s=s, a_re=a_re, a_im=a_im):
                s_re, s_im = carry
                rows = pl.ds(pl.multiple_of(k * SUBLANES, SUBLANES), SUBLANES)
                x_re = xs_scr[s, rows, 0:half]
                x_im = xs_scr[s, rows, half:]
                xs_scr[s, rows, 0:half] = s_re
                xs_scr[s, rows, half:] = s_im
                n_re = a_re * s_re - a_im * s_im + x_re
                n_im = a_re * s_im + a_im * s_re + x_im
                return n_re, n_im

            s_re, s_im = lax.fori_loop(0, ROWS_Z // SUBLANES, scan_body,
                                       (s_scr[s, :, 0:half], s_scr[s, :, half:]), unroll=True)
            s_scr[s, :, 0:half] = s_re
            s_scr[s, :, half:] = s_im

        for s in range(N_SLAB):
            lanes = slice(s * LANES, (s + 1) * LANES)
            y_tot = yi_scr[s] + jnp.dot(xs_scr[s].astype(BF16), r_ref[s], preferred_element_type=F32)
            for j in range(Q):
                y_scr[:, j, :, lanes] = y_tot[:, j * LANES:(j + 1) * LANES].reshape(
                    ROWS_Z // SUBLANES, SUBLANES, LANES)

    def phase_c1(r, carry):
        rows = sub_rows(r)
        crow = pl.ds(r * CHUNK_ROWS, CHUNK_ROWS)
        y = y_scr[crow].reshape(SB, D_SSM) + d_ref[...] * u_scr[crow].reshape(SB, D_SSM)
        act_scr[rows, 0:D_SSM] = jax.nn.gelu(y).astype(BF16)
        for lc in range(N_LC):
            lanes = slice(lc * LANES, (lc + 1) * LANES)

            def conv_piece(rc, c, lc=lc, lanes=lanes):
                r0 = r * SB + rc * CONV_ROWS
                piece = jnp.broadcast_to(dwb_ref[:, lanes], (CONV_ROWS, LANES))
                for j in range(CONV_WIDTH):
                    zrows = pl.ds(pl.multiple_of(r0 + j * BATCH, SUBLANES), CONV_ROWS)
                    piece = piece + dw_ref[j:j + 1, lanes] * z_scr[lc, zrows, :]
                conv_scr[pl.ds(pl.multiple_of(rc * CONV_ROWS, CONV_ROWS), CONV_ROWS), lanes] = piece
                return c

            lax.fori_loop(0, SB // CONV_ROWS, conv_piece, 0, unroll=4)
        acc = conv_scr[...]
        mu = jnp.mean(acc, axis=-1, keepdims=True)
        cen = acc - mu
        var = jnp.mean(cen * cen, axis=-1, keepdims=True)
        ln = cen * lax.rsqrt(var + EPS) * lng_ref[...] + lnb_ref[...]
        act_scr[rows, D_SSM:] = jax.nn.silu(ln).astype(BF16)
        return carry

    lane = lax.broadcasted_iota(I32, (1, LANES), 1).astype(F32)
    grp_mask = lane < float(N_GROUPS_MOE)
    exp_lane = (lane >= float(LANE_EXP0)) & (lane < float(LANE_EXP0 + N_EXPERTS))
    lane_grp = jnp.floor((lane - float(LANE_EXP0)) * (1.0 / EXPERTS_PER_GROUP))
    tri = (lax.broadcasted_iota(I32, (SB, SB), 0) > lax.broadcasted_iota(I32, (SB, SB), 1)).astype(BF16)
    neg_inf = float("-inf")
    big = float(4 * LANES)

    def phase_c3(r, carry):
        rows = sub_rows(r)
        h = hb_scr[rows, :]
        g0 = D_SSM + 2 * D_CONV
        gate_ssm = jnp.dot(h, win_ref[:, g0:g0 + D_MODEL], preferred_element_type=F32) \
            + bgate_ref[:, 0:D_MODEL]
        gate_conv = jnp.dot(h, win_ref[:, g0 + D_MODEL:], preferred_element_type=F32) \
            + bgate_ref[:, D_MODEL:]
        zz = jnp.dot(actb_scr[rows, 0:D_SSM], wglu_ref[...], preferred_element_type=F32)
        y_ssm = zz[:, 0:D_MODEL] * jax.nn.sigmoid(zz[:, D_MODEL:])
        y_conv = jnp.dot(actb_scr[rows, D_SSM:], wco_ref[...], preferred_element_type=F32)

        merged = jax.nn.sigmoid(gate_ssm) * y_ssm + jax.nn.sigmoid(gate_conv) * y_conv
        xb = x_ref[pl.ds(r * BPS, BPS)].reshape(SB, D_MODEL)
        x1 = xb + jnp.dot(merged.astype(BF16), wout_ref[...], preferred_element_type=F32)
        x1_ref[pl.ds(r * BPS, BPS)] = x1.reshape(BPS, TT, D_MODEL)

        h2 = _rms(x1, gmoe_ref[...])
        h2p_ref[rows] = _pack_bf16_pair(h2[:, 0:HALF], h2[:, HALF:]).reshape((SB,) + ROW_TILE)

        h2_hi = h2.astype(BF16)
        h2_lo = (h2 - h2_hi.astype(F32)).astype(BF16)
        l1 = jnp.dot(h2_hi, wr1_ref[...], preferred_element_type=F32)
        l2 = jnp.dot(h2_lo, wr2_ref[...], preferred_element_type=F32)
        logit_scr[rows, :] = l1[:, 0:LANES] + l1[:, LANES:] + l2 + br_ref[...]
        return carry

    def route_previous():
        rows = sub_rows(0)
        logits = logit_scr[...]
        counted = jnp.where(step > 0, 1.0, 0.0)

        lg = jnp.where(grp_mask, logits, neg_inf)
        g_max = jnp.max(lg, axis=-1, keepdims=True)
        g_sel = jnp.min(jnp.where(lg == g_max, lane, big), axis=-1, keepdims=True)
        p_g = 1.0 / jnp.sum(jnp.where(grp_mask, jnp.exp(logits - g_max), 0.0), axis=-1, keepdims=True)
        le = jnp.where(exp_lane & (lane_grp == g_sel), logits, neg_inf)
        m1 = jnp.max(le, axis=-1, keepdims=True)
        i1 = jnp.min(jnp.where(le == m1, lane, big), axis=-1, keepdims=True)
        le2 = jnp.where(lane == i1, neg_inf, le)
        m2 = jnp.max(le2, axis=-1, keepdims=True)
        i2 = jnp.min(jnp.where(le2 == m2, lane, big), axis=-1, keepdims=True)
        e2 = jnp.exp(m2 - m1)
        den = 1.0 + e2
        w_a = (1.0 / den) * p_g
        w_b = (e2 / den) * p_g

        sel1 = lane == i1
        sel2 = lane == i2
        onehot = jnp.where(sel1 | sel2, counted, 0.0)
        prefix = jnp.dot(tri, onehot.astype(BF16), preferred_element_type=F32) + cnt_scr[...]
        rank_a = jnp.sum(jnp.where(sel1, prefix, 0.0), axis=-1, keepdims=True)
        rank_b = jnp.sum(jnp.where(sel2, prefix, 0.0), axis=-1, keepdims=True)
        cnt_scr[...] = cnt_scr[...] + jnp.sum(onehot, axis=0, keepdims=True)

        rec = jnp.where(lane == float(REC_EID0), i1 - float(LANE_EXP0), 0.0)
        rec = jnp.where(lane == float(REC_EID1), i2 - float(LANE_EXP0), rec)
        rec = jnp.where(lane == float(REC_W0), w_a, rec)
        rec = jnp.where(lane == float(REC_W1), w_b, rec)
        rec = jnp.where(lane == float(REC_RANK0), rank_a, rec)
        rec = jnp.where(lane == float(REC_RANK1), rank_b, rec)
        rec_ref[rows, :] = rec
        rect_ref[...] = jnp.transpose(rec)[0:REC_ROWS, :]
        cnt_ref[...] = cnt_scr[...]

    @pl.when(step < N_STEP)
    def _tile():
        route_previous()
        phase_a(0, 0)
        ht_scr[...] = jnp.swapaxes(hb_scr[...].reshape(BATCH, TT, D_MODEL), 0, 1).reshape(TM, D_MODEL)
        phase_a3(0, 0)
        phase_b()
        phase_c1(0, 0)
        z_scr[:, 0:HALO, :] = z_scr[:, TM:TM + HALO, :]
        actb_scr[...] = jnp.swapaxes(act_scr[...].reshape(TT, BATCH, D_SSM + D_CONV), 0, 1).reshape(
            TM, D_SSM + D_CONV)
        phase_c3(0, 0)

    @pl.when(step == N_STEP)
    def _last():
        route_previous()


def _mixer(x, gmix, win, bgate, mp, rmat, a_re, a_im, dvec, wglu, dw, dwb, lng, lnb, wco,
           wout, gmoe, wr1, wr2, br):
    tile = lambda i: jnp.minimum(i, N_STEP - 1)
    routed = lambda i: jnp.maximum(i - 1, 0)
    seq_spec = pl.BlockSpec((BATCH, TT, D_MODEL), lambda i: (0, tile(i), 0))
    in_specs = [
        seq_spec,
        _const_spec((1, D_MODEL)),
        _const_spec((D_MODEL, D_IN)),
        _const_spec((1, 2 * D_MODEL)),
        _const_spec(mp.shape),
        _const_spec(rmat.shape),
        _const_spec(a_re.shape),
        _const_spec(a_im.shape),
        _const_spec((1, D_SSM)),
        _const_spec((D_SSM, 2 * D_MODEL)),
        _const_spec((CONV_WIDTH, D_CONV)),
        _const_spec((1, D_CONV)),
        _const_spec((1, D_CONV)),
        _const_spec((1, D_CONV)),
        _const_spec((D_CONV, D_MODEL)),
        _const_spec((D_MODEL, D_MODEL)),
        _const_spec((1, D_MODEL)),
        _const_spec((D_MODEL, 2 * LANES)),
        _const_spec((D_MODEL, LANES)),
        _const_spec((1, LANES)),
    ]
    out_specs = [
        seq_spec,
        pl.BlockSpec((TM,) + ROW_TILE, lambda i: (tile(i), 0, 0)),
        pl.BlockSpec((TM, LANES), lambda i: (routed(i), 0)),
        pl.BlockSpec((REC_ROWS, TM), lambda i: (0, routed(i))),
        pl.BlockSpec((1, LANES), lambda i: (0, 0)),
    ]
    out_shape = [
        jax.ShapeDtypeStruct((BATCH, SEQ, D_MODEL), F32),
        jax.ShapeDtypeStruct((N_TOK,) + ROW_TILE, U32),
        jax.ShapeDtypeStruct((N_TOK, LANES), F32),
        jax.ShapeDtypeStruct((REC_ROWS, N_TOK), F32),
        jax.ShapeDtypeStruct((1, LANES), F32),
    ]
    chunk_shape = (ROWS_Z // SUBLANES, Q, SUBLANES, D_SSM)
    scratch = [
        pltpu.VMEM((TM, D_MODEL), BF16),
        pltpu.VMEM((TM, D_MODEL), BF16),
        pltpu.VMEM(chunk_shape, F32),
        pltpu.VMEM(chunk_shape, F32),
        pltpu.VMEM((N_SLAB, ROWS_Z, Q * LANES), F32),
        pltpu.VMEM((N_SLAB, ROWS_Z, STATE_LANES), F32),
        pltpu.VMEM((N_LC, HALO + TM, LANES), F32),
        pltpu.VMEM((SB, D_CONV), F32),
        pltpu.VMEM((TM, D_SSM + D_CONV), BF16),
        pltpu.VMEM((TM, D_SSM + D_CONV), BF16),
        pltpu.VMEM((TM, LANES), F32),
        pltpu.VMEM((N_SLAB, SUBLANES, STATE_LANES), F32),
        pltpu.VMEM((1, LANES), F32),
    ]
    return pl.pallas_call(
        _mixer_kernel,
        grid=(N_STEP + 1,),
        in_specs=in_specs,
        out_specs=out_specs,
        out_shape=out_shape,
        scratch_shapes=scratch,
        compiler_params=pltpu.CompilerParams(
            dimension_semantics=("arbitrary",), vmem_limit_bytes=VMEM_LIMIT),
        name="mixer",
    )(x, gmix, win, bgate, mp, rmat, a_re, a_im, dvec, wglu, dw, dwb, lng, lnb, wco, wout,
      gmoe, wr1, wr2, br)


def _cmul(a, b):
    return a[0] * b[0] - a[1] * b[1], a[0] * b[1] + a[1] * b[0]


def _ssm_matrices(a_re, a_im, log_dt, b_re, b_im, c_re, c_im):
    dt = jnp.exp(log_dt)[:, None]
    mag = jnp.exp(a_re * dt)
    lam = (mag * jnp.cos(a_im * dt), mag * jnp.sin(a_im * dt))
    den = a_re * a_re + a_im * a_im
    nr = lam[0] - 1.0
    ni = lam[1]
    z_re = (nr * a_re + ni * a_im) / den
    z_im = (ni * a_re - nr * a_im) / den
    bbar = (z_re[..., None] * b_re - z_im[..., None] * b_im,
            z_re[..., None] * b_im + z_im[..., None] * b_re)
    pw = [(jnp.ones_like(lam[0]), jnp.zeros_like(lam[0])), lam]
    for _ in range(2, Q + 1):
        pw.append(_cmul(pw[-1], lam))
    e = [(c_re * p[0][:, None, :] - c_im * p[1][:, None, :],
          c_re * p[1][:, None, :] + c_im * p[0][:, None, :]) for p in pw]
    hp = lax.Precision.HIGHEST
    k = [jnp.einsum('gcn,gnd->gcd', e[m][0], bbar[0], precision=hp)
         - jnp.einsum('gcn,gnd->gcd', e[m][1], bbar[1], precision=hp) for m in range(Q)]
    eye = jnp.eye(GROUPS_PER_SLAB, dtype=F32)
    split = lambda t: t.reshape((N_SLAB, GROUPS_PER_SLAB) + t.shape[1:])
    zero_k = jnp.zeros_like(k[0])
    kb = jnp.stack([jnp.stack([split(jnp.swapaxes(k[j - i] if j >= i else zero_k, 1, 2))
                               for j in range(Q)]) for i in range(Q)])
    m_mat = jnp.einsum('ijsgdc,gh->sigdjhc', kb, eye).reshape(N_SLAB, Q * LANES, Q * LANES)
    f = [_cmul((pw[Q - 1 - i][0][..., None], pw[Q - 1 - i][1][..., None]), bbar) for i in range(Q)]
    p_parts = []
    for part in range(2):
        fs = jnp.stack([split(f[i][part]) for i in range(Q)])
        p_parts.append(jnp.einsum('isgnd,gh->sigdhn', fs, eye).reshape(N_SLAB, Q * LANES, STATE_LANES // 2))
    p_mat = jnp.concatenate(p_parts, axis=-1)
    r_parts = []
    for part, sign in ((0, 1.0), (1, -1.0)):
        es = jnp.stack([split(e[j + 1][part]) for j in range(Q)])
        r_parts.append(sign * jnp.einsum('jsgcn,gh->shnjgc', es, eye).reshape(
            N_SLAB, STATE_LANES // 2, Q * LANES))
    r_mat = jnp.concatenate(r_parts, axis=1)
    mp = jnp.concatenate([m_mat, p_mat], axis=-1).astype(BF16)
    a_q = pw[Q]
    return (mp, r_mat.astype(BF16),
            a_q[0].reshape(N_SLAB, STATE_LANES // 2), a_q[1].reshape(N_SLAB, STATE_LANES // 2))


def _router_weights(w_rg, b_rg, w_re, b_re):
    pad_g = LANE_EXP0 - LANE_GRP0 - N_GROUPS_MOE
    pad_e = LANES - LANE_EXP0 - N_EXPERTS
    w = jnp.concatenate([w_rg, jnp.zeros((D_MODEL, pad_g), F32), w_re, jnp.zeros((D_MODEL, pad_e), F32)], axis=1)
    b = jnp.concatenate([b_rg, jnp.zeros((pad_g,), F32), b_re, jnp.zeros((pad_e,), F32)]).reshape(1, LANES)
    w_hi = w.astype(BF16)
    w_lo = (w - w_hi.astype(F32)).astype(BF16)
    return jnp.concatenate([w_hi, w_lo], axis=1), w_hi, b


def _sc_mesh():
    return plsc.VectorSubcoreMesh(core_axis_name="core", subcore_axis_name="subcore")


def _sc_worker(mesh):
    return lax.axis_index("core") * mesh.num_subcores + lax.axis_index("subcore")


def _dispatch(h2p, dest):
    mesh = _sc_mesh()
    n_win = N_TOK // SC_WINDOW
    per_worker = n_win // (mesh.num_cores * mesh.num_subcores)
    assert per_worker * mesh.num_cores * mesh.num_subcores == n_win

    @pl.kernel(out_type=jax.ShapeDtypeStruct((N_ROWS,) + ROW_TILE, U32), mesh=mesh,
               scratch_types=[pltpu.VMEM((SC_WINDOW,), I32), pltpu.VMEM((SC_WINDOW,) + ROW_TILE, U32)])
    def scatter_rows(h_hbm, dest_hbm, xs_hbm, idx_v, rows_v):
        first = _sc_worker(mesh) * per_worker

        @pl.loop(0, per_worker)
        def _(w):
            win = first + w
            pltpu.sync_copy(h_hbm.at[pl.ds(win * SC_WINDOW, SC_WINDOW)], rows_v)
            for j in range(TOPK):
                pltpu.sync_copy(dest_hbm.at[j, win], idx_v)
                pltpu.sync_copy(rows_v, xs_hbm.at[idx_v])

    return scatter_rows(h2p, dest)


def _collect(ys, dest):
    mesh = _sc_mesh()
    n_tok = dest.shape[1]
    n_win = TOPK * n_tok // SC_WINDOW
    per_worker = n_win // (mesh.num_cores * mesh.num_subcores)
    assert per_worker * mesh.num_cores * mesh.num_subcores == n_win

    @pl.kernel(out_type=jax.ShapeDtypeStruct((TOPK * n_tok,) + ROW_TILE, U32), mesh=mesh,
               scratch_types=[pltpu.VMEM((SC_WINDOW,), I32), pltpu.VMEM((SC_WINDOW,) + ROW_TILE, U32)])
    def gather_rows(ys_hbm, dest_hbm, yg_hbm, idx_v, rows_v):
        first = _sc_worker(mesh) * per_worker

        @pl.loop(0, per_worker)
        def _(w):
            win = first + w
            pltpu.sync_copy(dest_hbm.at[win], idx_v)
            pltpu.sync_copy(ys_hbm.at[idx_v], rows_v)
            pltpu.sync_copy(rows_v, yg_hbm.at[pl.ds(win * SC_WINDOW, SC_WINDOW)])

    return gather_rows(ys, dest.reshape(n_win, SC_WINDOW)).reshape((TOPK, n_tok) + ROW_TILE)


def _expert_kernel(first_ref, nblk_ref, nvalid_ref, nused_ref, xs_hbm, wg_ref, wu_ref, wd_ref, ys_hbm,
                   wg_scr, wu_scr, wd_scr, x_buf, y_buf, in_sem, out_sem):
    e = pl.program_id(0)
    nused = nused_ref[0]

    def in_copy(g):
        slot = lax.rem(g, IN_SLOTS)
        return pltpu.make_async_copy(xs_hbm.at[pl.ds(g * BM, BM)], x_buf.at[slot], in_sem.at[slot])

    def out_copy(g, slot):
        return pltpu.make_async_copy(y_buf.at[slot], ys_hbm.at[pl.ds(g * BM, BM)], out_sem.at[slot])

    @pl.when(e == 0)
    def _first():
        for g in range(IN_AHEAD):
            in_copy(g).start()

    wg_scr[...] = wg_ref[0].astype(BF16)
    wu_scr[...] = wu_ref[0].astype(BF16)
    wd_scr[...] = wd_ref[0].astype(BF16)

    def block(b, carry):
        g = first_ref[e] + b
        slot = lax.rem(g, 2)
        in_copy(g).wait()

        @pl.when(g + IN_AHEAD < nused)
        def _prefetch():
            in_copy(g + IN_AHEAD).start()

        @pl.when(g >= 2)
        def _slot_free():
            out_copy(g - 2, slot).wait()

        valid = lax.broadcasted_iota(I32, (BM, 1), 0) < nvalid_ref[g]
        x_blk = x_buf[lax.rem(g, IN_SLOTS)].reshape(BM, HALF)
        lo, hi = _unpack_bf16_pair(jnp.where(valid, x_blk, jnp.uint32(0)))
        lo = lo.astype(BF16)
        hi = hi.astype(BF16)
        gate = jnp.dot(lo, wg_scr[0:HALF, :], preferred_element_type=F32) \
            + jnp.dot(hi, wg_scr[HALF:, :], preferred_element_type=F32)
        up = jnp.dot(lo, wu_scr[0:HALF, :], preferred_element_type=F32) \
            + jnp.dot(hi, wu_scr[HALF:, :], preferred_element_type=F32)
        act = (jax.nn.silu(gate) * up).astype(BF16)
        o = jnp.dot(act, wd_scr[...], preferred_element_type=F32)
        y_buf[slot] = _pack_bf16_pair(o[:, 0:HALF], o[:, HALF:]).reshape((BM,) + ROW_TILE)
        out_copy(g, slot).start()
        return carry

    lax.fori_loop(0, nblk_ref[e], block, 0)

    @pl.when(e == N_EXPERTS - 1)
    def _drain():
        out_copy(nused - 2, lax.rem(nused, 2)).wait()
        out_copy(nused - 1, 1 - lax.rem(nused, 2)).wait()


def _experts(first, nblk, nvalid, nused, xs, wg, wu, wd):
    grid_spec = pltpu.PrefetchScalarGridSpec(
        num_scalar_prefetch=4,
        grid=(N_EXPERTS,),
        in_specs=[
            pl.BlockSpec(memory_space=pl.ANY),
            pl.BlockSpec((1, D_MODEL, D_EXPERT), lambda e, *_: (e, 0, 0)),
            pl.BlockSpec((1, D_MODEL, D_EXPERT), lambda e, *_: (e, 0, 0)),
            pl.BlockSpec((1, D_EXPERT, D_MODEL), lambda e, *_: (e, 0, 0)),
        ],
        out_specs=pl.BlockSpec(memory_space=pl.ANY),
        scratch_shapes=[
            pltpu.VMEM((D_MODEL, D_EXPERT), BF16),
            pltpu.VMEM((D_MODEL, D_EXPERT), BF16),
            pltpu.VMEM((D_EXPERT, D_MODEL), BF16),
            pltpu.VMEM((IN_SLOTS, BM) + ROW_TILE, U32),
            pltpu.VMEM((2, BM) + ROW_TILE, U32),
            pltpu.SemaphoreType.DMA((IN_SLOTS,)),
            pltpu.SemaphoreType.DMA((2,)),
        ],
    )
    return pl.pallas_call(
        _expert_kernel,
        grid_spec=grid_spec,
        out_shape=jax.ShapeDtypeStruct((N_ROWS,) + ROW_TILE, U32),
        compiler_params=pltpu.CompilerParams(
            dimension_semantics=("arbitrary",), vmem_limit_bytes=VMEM_LIMIT),
        name="experts",
    )(first, nblk, nvalid, nused, xs, wg, wu, wd)


def _combine_kernel(x1_ref, rec_ref, yg_ref, p_ref, gple_ref, wpg_ref, wple_ref, gfin_ref, *rest):
    out_ref = rest[-1]
    ple = jnp.dot(p_ref[0].reshape(TM, D_PLE).astype(BF16), wple_ref[...], preferred_element_type=F32)
    rec = rec_ref[...]
    w0 = rec[:, REC_W0:REC_W0 + 1]
    w1 = rec[:, REC_W1:REC_W1 + 1]
    lo0, hi0 = _unpack_bf16_pair(yg_ref[0].reshape(TM, HALF))
    lo1, hi1 = _unpack_bf16_pair(yg_ref[1].reshape(TM, HALF))
    moe = jnp.concatenate([lo0 * w0 + lo1 * w1, hi0 * w0 + hi1 * w1], axis=1)
    x2 = x1_ref[...].reshape(TM, D_MODEL) + moe
    gate = jax.nn.sigmoid(jnp.dot(_rms(x2, gple_ref[...]).astype(BF16), wpg_ref[...],
                                  preferred_element_type=F32))
    x3 = x2 + gate * ple
    out_ref[...] = _rms(x3, gfin_ref[...]).reshape(BATCH, TT, D_MODEL)


def _combine(part, x1, rec, yg, p, gple, wpg, wple, gfin, out_prev=None):
    s0 = part * PART_STEPS
    seq_spec = pl.BlockSpec((BATCH, TT, D_MODEL), lambda i: (0, s0 + i, 0))
    in_specs = [
        seq_spec,
        pl.BlockSpec((TM, LANES), lambda i: (s0 + i, 0)),
        pl.BlockSpec((TOPK, TM) + ROW_TILE, lambda i: (0, i, 0, 0)),
        pl.BlockSpec((1, BATCH, TT, D_PLE), lambda i: (0, 0, s0 + i, 0)),
        _const_spec((1, D_MODEL)),
        _const_spec((D_MODEL, D_MODEL)),
        _const_spec((D_PLE, D_MODEL)),
        _const_spec((1, D_MODEL)),
    ]
    args = [x1, rec, yg, p, gple, wpg, wple, gfin]
    aliases = {}
    if out_prev is not None:
        in_specs.append(pl.BlockSpec(memory_space=pl.ANY))
        args.append(out_prev)
        aliases = {len(args) - 1: 0}
    return pl.pallas_call(
        _combine_kernel,
        grid=(PART_STEPS,),
        in_specs=in_specs,
        out_specs=seq_spec,
        out_shape=jax.ShapeDtypeStruct((BATCH, SEQ, D_MODEL), F32),
        input_output_aliases=aliases,
        compiler_params=pltpu.CompilerParams(
            dimension_semantics=("arbitrary",), vmem_limit_bytes=VMEM_LIMIT),
        name="combine",
    )(*args)


def kernel(x, p, g_mix, w_in, b_gate, ssm_a_re, ssm_a_im, ssm_log_dt, ssm_b_re, ssm_b_im, ssm_c_re,
           ssm_c_im, ssm_d, w_glu, conv_dw, conv_dw_b, conv_ln_g, conv_ln_b, w_conv_out, w_out, g_moe,
           w_router_group, b_router_group, w_router_expert, b_router_expert, w_exp_gate, w_exp_up,
           w_exp_down, g_ple, w_ple_gate, w_ple, g_final):
    assert x.shape == (BATCH, SEQ, D_MODEL) and p.shape == (1, BATCH, SEQ, D_PLE)
    row = lambda v: v.reshape(1, -1)

    mp, rmat, a_re, a_im = _ssm_matrices(ssm_a_re[0], ssm_a_im[0], ssm_log_dt[0], ssm_b_re[0],
                                         ssm_b_im[0], ssm_c_re[0], ssm_c_im[0])
    wr1, wr2, br = _router_weights(w_router_group[0], b_router_group[0], w_router_expert[0],
                                   b_router_expert[0])
    x1, h2p, rec, rect, cnt = _mixer(
        x, row(g_mix[0]), w_in[0].astype(BF16), row(b_gate[0]), mp, rmat, a_re, a_im,
        row(ssm_d[0]), w_glu[0].astype(BF16), conv_dw[0], row(conv_dw_b[0]), row(conv_ln_g[0]),
        row(conv_ln_b[0]), w_conv_out[0].astype(BF16), w_out[0].astype(BF16), row(g_moe[0]), wr1, wr2, br)

    counts = cnt[0, LANE_EXP0:LANE_EXP0 + N_EXPERTS].astype(I32)
    pcounts = (counts + BM - 1) // BM * BM
    pends = jnp.cumsum(pcounts)
    pstarts = pends - pcounts
    eid = rect[REC_EID0:REC_EID1 + 1].astype(I32)
    rank = rect[REC_RANK0:REC_RANK1 + 1].astype(I32)
    dest = (jnp.sum(jnp.where(eid[..., None] == jnp.arange(N_EXPERTS, dtype=I32), pstarts, 0), axis=-1)
            + rank).reshape(TOPK, N_TOK // SC_WINDOW, SC_WINDOW)
    nused = (pends[-1] // BM).astype(I32)
    blk = jnp.arange(N_BLK, dtype=I32)[:, None] * BM
    in_expert = (pstarts[None, :] <= blk) & (blk < pends[None, :])
    nvalid = jnp.clip(jnp.sum(jnp.where(in_expert, (pstarts + counts)[None, :] - blk, 0), axis=1), 0, BM)

    xs = _dispatch(h2p, dest)
    ys = _experts(pstarts // BM, pcounts // BM, nvalid.astype(I32), nused.reshape(1), xs,
                  w_exp_gate[0], w_exp_up[0], w_exp_down[0])
    dest_tok = dest.reshape(TOPK, N_TOK)
    wpg = w_ple_gate[0].astype(BF16)
    wple = w_ple[0].astype(BF16)
    out = None
    for part in range(N_PARTS):
        tok = slice(part * PART_STEPS * TM, (part + 1) * PART_STEPS * TM)
        yg = _collect(ys, dest_tok[:, tok])
        out = _combine(part, x1, rec, yg, p, row(g_ple[0]), wpg, wple, row(g_final), out)
    return out
```

```python
import jax
import jax.numpy as jnp
from jax import lax
from jax.experimental import pallas as pl
from jax.experimental.pallas import tpu as pltpu
from jax.experimental.pallas import tpu_sc as plsc

F32 = jnp.float32
BF16 = jnp.bfloat16
U32 = jnp.uint32
I32 = jnp.int32

D_MODEL = 1024
BATCH = 8
SEQ = 2048
N_TOK = BATCH * SEQ
D_SSM = 512
SSM_GROUP_WIDTH = 16
SSM_GROUPS = 32
SSM_STATE = 64
D_CONV = 512
CONV_WIDTH = 31
D_IN = D_SSM + 2 * D_CONV + 2 * D_MODEL
N_GROUPS_MOE = 4
EXPERTS_PER_GROUP = 8
N_EXPERTS = 32
TOPK = 2
D_EXPERT = 512
D_PLE = 256
EPS = 1e-6

SUBLANES = 8
LANES = 128
assert BATCH == SUBLANES

TT = 64
TM = TT * BATCH
N_STEP = SEQ // TT
SB = 512
NSB = TM // SB
BPS = SB // TT
Q = 2
N_SLAB = D_SSM // LANES
GROUPS_PER_SLAB = SSM_GROUPS // N_SLAB
ROWS_Z = TM // Q
STATE_LANES = 2 * GROUPS_PER_SLAB * SSM_STATE
HALO = (CONV_WIDTH - 1) * BATCH
CHUNK_ROWS = SB // (Q * SUBLANES)
CONV_ROWS = 64
N_LC = D_CONV // LANES

LANE_GRP0 = 0
LANE_EXP0 = 32
REC_EID0, REC_EID1, REC_W0, REC_W1, REC_RANK0, REC_RANK1 = 0, 1, 2, 3, 4, 5
REC_ROWS = 8

BM = 256
N_BLK = (TOPK * N_TOK + N_EXPERTS * (BM - 1) + BM - 1) // BM
N_ROWS = N_BLK * BM
HALF = D_MODEL // 2
ROW_TILE = (HALF // LANES, LANES)
SC_WINDOW = 64
IN_AHEAD = 3
IN_SLOTS = IN_AHEAD + 1
N_PARTS = 2
PART_STEPS = N_STEP // N_PARTS

VMEM_LIMIT = 56 * 1024 * 1024


def _const_spec(shape):
    n = len(shape)
    return pl.BlockSpec(shape, lambda *_: (0,) * n, pipeline_mode=pl.Buffered(1))


def _rms(x, g):
    ms = jnp.mean(x * x, axis=-1, keepdims=True)
    return x * lax.rsqrt(ms + EPS) * g


def _pack_bf16_pair(lo, hi):
    ulo = lax.bitcast_convert_type(lo.astype(BF16).astype(F32), U32)
    uhi = lax.bitcast_convert_type(hi.astype(BF16).astype(F32), U32)
    return (ulo >> 16) | (uhi & jnp.uint32(0xFFFF0000))


def _unpack_bf16_pair(w):
    lo = lax.bitcast_convert_type(w << 16, F32)
    hi = lax.bitcast_convert_type(w & jnp.uint32(0xFFFF0000), F32)
    return lo, hi


def _mixer_kernel(x_ref, gmix_ref, win_ref, bgate_ref, mp_ref, r_ref, are_ref,
                  aim_ref, d_ref, wglu_ref, dw_ref, dwb_ref, lng_ref, lnb_ref, wco_ref, wout_ref,
                  gmoe_ref, wr1_ref, br_ref,
                  x1_ref, h2p_ref, rec_ref, rect_ref, cnt_ref,
                  hb_scr, ht_scr, u_scr, y_scr, yi_scr, xs_scr, z_scr, conv_scr, act_scr, actb_scr,
                  logit_scr, s_scr, cnt_scr):
    step = pl.program_id(0)
    assert NSB == 1

    @pl.when(step == 0)
    def _init():
        logit_scr[...] = jnp.zeros(logit_scr.shape, F32)
        z_scr[:, 0:HALO, :] = jnp.zeros((N_LC, HALO, LANES), F32)
        s_scr[...] = jnp.zeros(s_scr.shape, F32)
        cnt_scr[...] = jnp.zeros(cnt_scr.shape, F32)

    def sub_rows(r):
        return pl.ds(pl.multiple_of(r * SB, SB), SB)

    def phase_a(r, carry):
        xb = x_ref[pl.ds(r * BPS, BPS)].reshape(SB, D_MODEL)
        hb_scr[sub_rows(r), :] = _rms(xb, gmix_ref[...]).astype(BF16)
        return carry

    def phase_a3(r, carry):
        h = ht_scr[sub_rows(r), :]
        u = jnp.dot(h, win_ref[:, 0:D_SSM], preferred_element_type=F32)
        u_scr[pl.ds(r * CHUNK_ROWS, CHUNK_ROWS)] = u.reshape(CHUNK_ROWS, Q, SUBLANES, D_SSM)
        v = jnp.dot(h, win_ref[:, D_SSM:D_SSM + 2 * D_CONV], preferred_element_type=F32)
        zc = v[:, 0:D_CONV] * jax.nn.sigmoid(v[:, D_CONV:])
        for lc in range(N_LC):
            z_scr[lc, pl.ds(pl.multiple_of(HALO + r * SB, SUBLANES), SB), :] = zc[:, lc * LANES:(lc + 1) * LANES]
        return carry

    def phase_b():
        for s in range(N_SLAB):
            lanes = slice(s * LANES, (s + 1) * LANES)
            z = jnp.concatenate(
                [u_scr[:, i, :, lanes].reshape(ROWS_Z, LANES) for i in range(Q)], axis=1).astype(BF16)
            xp = jnp.dot(z, mp_ref[s], preferred_element_type=F32)
            yi_scr[s] = xp[:, 0:Q * LANES]
            xs_scr[s] = xp[:, Q * LANES:]

        half = STATE_LANES // 2
        for s in range(N_SLAB):
            a_re = jnp.broadcast_to(are_ref[s:s + 1, :], (SUBLANES, half))
            a_im = jnp.broadcast_to(aim_ref[s:s + 1, :], (SUBLANES, half))

            def scan_body(k, carry, s=s, a_re=a_re, a_im=a_im):
                s_re, s_im = carry
                rows = pl.ds(pl.multiple_of(k * SUBLANES, SUBLANES), SUBLANES)
                x_re = xs_scr[s, rows, 0:half]
                x_im = xs_scr[s, rows, half:]
                xs_scr[s, rows, 0:half] = s_re
                xs_scr[s, rows, half:] = s_im
                n_re = a_re * s_re - a_im * s_im + x_re
                n_im = a_re * s_im + a_im * s_re + x_im
                return n_re, n_im

            s_re, s_im = lax.fori_loop(0, ROWS_Z // SUBLANES, scan_body,
                                       (s_scr[s, :, 0:half], s_scr[s, :, half:]), unroll=True)
            s_scr[s, :, 0:half] = s_re
            s_scr[s, :, half:] = s_im

        for s in range(N_SLAB):
            lanes = slice(s * LANES, (s + 1) * LANES)
            y_tot = yi_scr[s] + jnp.dot(xs_scr[s].astype(BF16), r_ref[s], preferred_element_type=F32)
            for j in range(Q):
                y_scr[:, j, :, lanes] = y_tot[:, j * LANES:(j + 1) * LANES].reshape(
                    ROWS_Z // SUBLANES, SUBLANES, LANES)

    def phase_c1(r, carry):
        rows = sub_rows(r)
        crow = pl.ds(r * CHUNK_ROWS, CHUNK_ROWS)
        y = y_scr[crow].reshape(SB, D_SSM) + d_ref[...] * u_scr[crow].reshape(SB, D_SSM)
        act_scr[rows, 0:D_SSM] = jax.nn.gelu(y).astype(BF16)
        for lc in range(N_LC):
            lanes = slice(lc * LANES, (lc + 1) * LANES)

            def conv_piece(rc, c, lc=lc, lanes=lanes):
                r0 = r * SB + rc * CONV_ROWS
                piece = jnp.broadcast_to(dwb_ref[:, lanes], (CONV_ROWS, LANES))
                for j in range(CONV_WIDTH):
                    zrows = pl.ds(pl.multiple_of(r0 + j * BATCH, SUBLANES), CONV_ROWS)
                    piece = piece + dw_ref[j:j + 1, lanes] * z_scr[lc, zrows, :]
                conv_scr[pl.ds(pl.multiple_of(rc * CONV_ROWS, CONV_ROWS), CONV_ROWS), lanes] = piece
                return c

            lax.fori_loop(0, SB // CONV_ROWS, conv_piece, 0, unroll=4)
        acc = conv_scr[...]
        mu = jnp.mean(acc, axis=-1, keepdims=True)
        cen = acc - mu
        var = jnp.mean(cen * cen, axis=-1, keepdims=True)
        ln = cen * lax.rsqrt(var + EPS) * lng_ref[...] + lnb_ref[...]
        act_scr[rows, D_SSM:] = jax.nn.silu(ln).astype(BF16)
        return carry

    lane = lax.broadcasted_iota(I32, (1, LANES), 1).astype(F32)
    grp_mask = lane < float(N_GROUPS_MOE)
    exp_lane = (lane >= float(LANE_EXP0)) & (lane < float(LANE_EXP0 + N_EXPERTS))
    lane_grp = jnp.floor((lane - float(LANE_EXP0)) * (1.0 / EXPERTS_PER_GROUP))
    tri = (lax.broadcasted_iota(I32, (SB, SB), 0) > lax.broadcasted_iota(I32, (SB, SB), 1)).astype(BF16)
    neg_inf = float("-inf")
    big = float(4 * LANES)

    def phase_c3(r, carry):
        rows = sub_rows(r)
        h = hb_scr[rows, :]
        g0 = D_SSM + 2 * D_CONV
        gate_ssm = jnp.dot(h, win_ref[:, g0:g0 + D_MODEL], preferred_element_type=F32) \
            + bgate_ref[:, 0:D_MODEL]
        gate_conv = jnp.dot(h, win_ref[:, g0 + D_MODEL:], preferred_element_type=F32) \
            + bgate_ref[:, D_MODEL:]
        zz = jnp.dot(actb_scr[rows, 0:D_SSM], wglu_ref[...], preferred_element_type=F32)
        y_ssm = zz[:, 0:D_MODEL] * jax.nn.sigmoid(zz[:, D_MODEL:])
        y_conv = jnp.dot(actb_scr[rows, D_SSM:], wco_ref[...], preferred_element_type=F32)

        merged = jax.nn.sigmoid(gate_ssm) * y_ssm + jax.nn.sigmoid(gate_conv) * y_conv
        xb = x_ref[pl.ds(r * BPS, BPS)].reshape(SB, D_MODEL)
        x1 = xb + jnp.dot(merged.astype(BF16), wout_ref[...], preferred_element_type=F32)
        x1_ref[pl.ds(r * BPS, BPS)] = x1.reshape(BPS, TT, D_MODEL)

        h2 = _rms(x1, gmoe_ref[...])
        h2p_ref[rows] = _pack_bf16_pair(h2[:, 0:HALF], h2[:, HALF:]).reshape((SB,) + ROW_TILE)

        l1 = jnp.dot(h2.astype(BF16), wr1_ref[...], preferred_element_type=F32)
        logit_scr[rows, :] = l1[:, 0:LANES] + l1[:, LANES:] + br_ref[...]
        return carry

    def route_previous():
        rows = sub_rows(0)
        logits = logit_scr[...]
        counted = jnp.where(step > 0, 1.0, 0.0)

        lg = jnp.where(grp_mask, logits, neg_inf)
        g_max = jnp.max(lg, axis=-1, keepdims=True)
        g_sel = jnp.min(jnp.where(lg == g_max, lane, big), axis=-1, keepdims=True)
        p_g = 1.0 / jnp.sum(jnp.where(grp_mask, jnp.exp(logits - g_max), 0.0), axis=-1, keepdims=True)
        le = jnp.where(exp_lane & (lane_grp == g_sel), logits, neg_inf)
        m1 = jnp.max(le, axis=-1, keepdims=True)
        i1 = jnp.min(jnp.where(le == m1, lane, big), axis=-1, keepdims=True)
        le2 = jnp.where(lane == i1, neg_inf, le)
        m2 = jnp.max(le2, axis=-1, keepdims=True)
        i2 = jnp.min(jnp.where(le2 == m2, lane, big), axis=-1, keepdims=True)
        e2 = jnp.exp(m2 - m1)
        den = 1.0 + e2
        w_a = (1.0 / den) * p_g
        w_b = (e2 / den) * p_g

        sel1 = lane == i1
        sel2 = lane == i2
        onehot = jnp.where(sel1 | sel2, counted, 0.0)
        prefix = jnp.dot(tri, onehot.astype(BF16), preferred_element_type=F32) + cnt_scr[...]
        rank_a = jnp.sum(jnp.where(sel1, prefix, 0.0), axis=-1, keepdims=True)
        rank_b = jnp.sum(jnp.where(sel2, prefix, 0.0), axis=-1, keepdims=True)
        cnt_scr[...] = cnt_scr[...] + jnp.sum(onehot, axis=0, keepdims=True)

        rec = jnp.where(lane == float(REC_EID0), i1 - float(LANE_EXP0), 0.0)
        rec = jnp.where(lane == float(REC_EID1), i2 - float(LANE_EXP0), rec)
        rec = jnp.where(lane == float(REC_W0), w_a, rec)
        rec = jnp.where(lane == float(REC_W1), w_b, rec)
        rec = jnp.where(lane == float(REC_RANK0), rank_a, rec)
        rec = jnp.where(lane == float(REC_RANK1), rank_b, rec)
        rec_ref[rows, :] = rec
        rect_ref[...] = jnp.transpose(rec)[0:REC_ROWS, :]
        cnt_ref[...] = cnt_scr[...]

    @pl.when(step < N_STEP)
    def _tile():
        route_previous()
        phase_a(0, 0)
        ht_scr[...] = jnp.swapaxes(hb_scr[...].reshape(BATCH, TT, D_MODEL), 0, 1).reshape(TM, D_MODEL)
        phase_a3(0, 0)
        phase_b()
        phase_c1(0, 0)
        z_scr[:, 0:HALO, :] = z_scr[:, TM:TM + HALO, :]
        actb_scr[...] = jnp.swapaxes(act_scr[...].reshape(TT, BATCH, D_SSM + D_CONV), 0, 1).reshape(
            TM, D_SSM + D_CONV)
        phase_c3(0, 0)

    @pl.when(step == N_STEP)
    def _last():
        route_previous()


def _mixer(x, gmix, win, bgate, mp, rmat, a_re, a_im, dvec, wglu, dw, dwb, lng, lnb, wco,
           wout, gmoe, wr1, br):
    tile = lambda i: jnp.minimum(i, N_STEP - 1)
    routed = lambda i: jnp.maximum(i - 1, 0)
    seq_spec = pl.BlockSpec((BATCH, TT, D_MODEL), lambda i: (0, tile(i), 0))
    in_specs = [
        seq_spec,
        _const_spec((1, D_MODEL)),
        _const_spec((D_MODEL, D_IN)),
        _const_spec((1, 2 * D_MODEL)),
        _const_spec(mp.shape),
        _const_spec(rmat.shape),
        _const_spec(a_re.shape),
        _const_spec(a_im.shape),
        _const_spec((1, D_SSM)),
        _const_spec((D_SSM, 2 * D_MODEL)),
        _const_spec((CONV_WIDTH, D_CONV)),
        _const_spec((1, D_CONV)),
        _const_spec((1, D_CONV)),
        _const_spec((1, D_CONV)),
        _const_spec((D_CONV, D_MODEL)),
        _const_spec((D_MODEL, D_MODEL)),
        _const_spec((1, D_MODEL)),
        _const_spec((D_MODEL, 2 * LANES)),
        _const_spec((1, LANES)),
    ]
    out_specs = [
        seq_spec,
        pl.BlockSpec((TM,) + ROW_TILE, lambda i: (tile(i), 0, 0)),
        pl.BlockSpec((TM, LANES), lambda i: (routed(i), 0)),
        pl.BlockSpec((REC_ROWS, TM), lambda i: (0, routed(i))),
        pl.BlockSpec((1, LANES), lambda i: (0, 0)),
    ]
    out_shape = [
        jax.ShapeDtypeStruct((BATCH, SEQ, D_MODEL), F32),
        jax.ShapeDtypeStruct((N_TOK,) + ROW_TILE, U32),
        jax.ShapeDtypeStruct((N_TOK, LANES), F32),
        jax.ShapeDtypeStruct((REC_ROWS, N_TOK), F32),
        jax.ShapeDtypeStruct((1, LANES), F32),
    ]
    chunk_shape = (ROWS_Z // SUBLANES, Q, SUBLANES, D_SSM)
    scratch = [
        pltpu.VMEM((TM, D_MODEL), BF16),
        pltpu.VMEM((TM, D_MODEL), BF16),
        pltpu.VMEM(chunk_shape, F32),
        pltpu.VMEM(chunk_shape, F32),
        pltpu.VMEM((N_SLAB, ROWS_Z, Q * LANES), F32),
        pltpu.VMEM((N_SLAB, ROWS_Z, STATE_LANES), F32),
        pltpu.VMEM((N_LC, HALO + TM, LANES), F32),
        pltpu.VMEM((SB, D_CONV), F32),
        pltpu.VMEM((TM, D_SSM + D_CONV), BF16),
        pltpu.VMEM((TM, D_SSM + D_CONV), BF16),
        pltpu.VMEM((TM, LANES), F32),
        pltpu.VMEM((N_SLAB, SUBLANES, STATE_LANES), F32),
        pltpu.VMEM((1, LANES), F32),
    ]
    return pl.pallas_call(
        _mixer_kernel,
        grid=(N_STEP + 1,),
        in_specs=in_specs,
        out_specs=out_specs,
        out_shape=out_shape,
        scratch_shapes=scratch,
        compiler_params=pltpu.CompilerParams(
            dimension_semantics=("arbitrary",), vmem_limit_bytes=VMEM_LIMIT),
        name="mixer",
    )(x, gmix, win, bgate, mp, rmat, a_re, a_im, dvec, wglu, dw, dwb, lng, lnb, wco, wout,
      gmoe, wr1, br)


def _cmul(a, b):
    return a[0] * b[0] - a[1] * b[1], a[0] * b[1] + a[1] * b[0]


def _ssm_matrices(a_re, a_im, log_dt, b_re, b_im, c_re, c_im):
    dt = jnp.exp(log_dt)[:, None]
    mag = jnp.exp(a_re * dt)
    lam = (mag * jnp.cos(a_im * dt), mag * jnp.sin(a_im * dt))
    den = a_re * a_re + a_im * a_im
    nr = lam[0] - 1.0
    ni = lam[1]
    z_re = (nr * a_re + ni * a_im) / den
    z_im = (ni * a_re - nr * a_im) / den
    bbar = (z_re[..., None] * b_re - z_im[..., None] * b_im,
            z_re[..., None] * b_im + z_im[..., None] * b_re)
    pw = [(jnp.ones_like(lam[0]), jnp.zeros_like(lam[0])), lam]
    for _ in range(2, Q + 1):
        pw.append(_cmul(pw[-1], lam))
    e = [(c_re * p[0][:, None, :] - c_im * p[1][:, None, :],
          c_re * p[1][:, None, :] + c_im * p[0][:, None, :]) for p in pw]
    hp = lax.Precision.HIGHEST
    k = [jnp.einsum('gcn,gnd->gcd', e[m][0], bbar[0], precision=hp)
         - jnp.einsum('gcn,gnd->gcd', e[m][1], bbar[1], precision=hp) for m in range(Q)]
    eye = jnp.eye(GROUPS_PER_SLAB, dtype=F32)
    split = lambda t: t.reshape((N_SLAB, GROUPS_PER_SLAB) + t.shape[1:])
    zero_k = jnp.zeros_like(k[0])
    kb = jnp.stack([jnp.stack([split(jnp.swapaxes(k[j - i] if j >= i else zero_k, 1, 2))
                               for j in range(Q)]) for i in range(Q)])
    m_mat = jnp.einsum('ijsgdc,gh->sigdjhc', kb, eye).reshape(N_SLAB, Q * LANES, Q * LANES)
    f = [_cmul((pw[Q - 1 - i][0][..., None], pw[Q - 1 - i][1][..., None]), bbar) for i in range(Q)]
    p_parts = []
    for part in range(2):
        fs = jnp.stack([split(f[i][part]) for i in range(Q)])
        p_parts.append(jnp.einsum('isgnd,gh->sigdhn', fs, eye).reshape(N_SLAB, Q * LANES, STATE_LANES // 2))
    p_mat = jnp.concatenate(p_parts, axis=-1)
    r_parts = []
    for part, sign in ((0, 1.0), (1, -1.0)):
        es = jnp.stack([split(e[j + 1][part]) for j in range(Q)])
        r_parts.append(sign * jnp.einsum('jsgcn,gh->shnjgc', es, eye).reshape(
            N_SLAB, STATE_LANES // 2, Q * LANES))
    r_mat = jnp.concatenate(r_parts, axis=1)
    mp = jnp.concatenate([m_mat, p_mat], axis=-1).astype(BF16)
    a_q = pw[Q]
    return (mp, r_mat.astype(BF16),
            a_q[0].reshape(N_SLAB, STATE_LANES // 2), a_q[1].reshape(N_SLAB, STATE_LANES // 2))


def _router_weights(w_rg, b_rg, w_re, b_re):
    pad_g = LANE_EXP0 - LANE_GRP0 - N_GROUPS_MOE
    pad_e = LANES - LANE_EXP0 - N_EXPERTS
    w = jnp.concatenate([w_rg, jnp.zeros((D_MODEL, pad_g), F32), w_re, jnp.zeros((D_MODEL, pad_e), F32)], axis=1)
    b = jnp.concatenate([b_rg, jnp.zeros((pad_g,), F32), b_re, jnp.zeros((pad_e,), F32)]).reshape(1, LANES)
    w_hi = w.astype(BF16)
    w_lo = (w - w_hi.astype(F32)).astype(BF16)
    return jnp.concatenate([w_hi, w_lo], axis=1), b


def _sc_mesh():
    return plsc.VectorSubcoreMesh(core_axis_name="core", subcore_axis_name="subcore")


def _sc_worker(mesh):
    return lax.axis_index("core") * mesh.num_subcores + lax.axis_index("subcore")


def _dispatch(h2p, dest):
    mesh = _sc_mesh()
    n_win = N_TOK // SC_WINDOW
    per_worker = n_win // (mesh.num_cores * mesh.num_subcores)
    assert per_worker * mesh.num_cores * mesh.num_subcores == n_win

    @pl.kernel(out_type=jax.ShapeDtypeStruct((N_ROWS,) + ROW_TILE, U32), mesh=mesh,
               scratch_types=[pltpu.VMEM((SC_WINDOW,), I32), pltpu.VMEM((SC_WINDOW,) + ROW_TILE, U32)])
    def scatter_rows(h_hbm, dest_hbm, xs_hbm, idx_v, rows_v):
        first = _sc_worker(mesh) * per_worker

        @pl.loop(0, per_worker)
        def _(w):
            win = first + w
            pltpu.sync_copy(h_hbm.at[pl.ds(win * SC_WINDOW, SC_WINDOW)], rows_v)
            for j in range(TOPK):
                pltpu.sync_copy(dest_hbm.at[j, win], idx_v)
                pltpu.sync_copy(rows_v, xs_hbm.at[idx_v])

    return scatter_rows(h2p, dest)


def _collect(ys, dest):
    mesh = _sc_mesh()
    n_tok = dest.shape[1]
    n_win = TOPK * n_tok // SC_WINDOW
    per_worker = n_win // (mesh.num_cores * mesh.num_subcores)
    assert per_worker * mesh.num_cores * mesh.num_subcores == n_win

    @pl.kernel(out_type=jax.ShapeDtypeStruct((TOPK * n_tok,) + ROW_TILE, U32), mesh=mesh,
               scratch_types=[pltpu.VMEM((SC_WINDOW,), I32), pltpu.VMEM((SC_WINDOW,) + ROW_TILE, U32)])
    def gather_rows(ys_hbm, dest_hbm, yg_hbm, idx_v, rows_v):
        first = _sc_worker(mesh) * per_worker

        @pl.loop(0, per_worker)
        def _(w):
            win = first + w
            pltpu.sync_copy(dest_hbm.at[win], idx_v)
            pltpu.sync_copy(ys_hbm.at[idx_v], rows_v)
            pltpu.sync_copy(rows_v, yg_hbm.at[pl.ds(win * SC_WINDOW, SC_WINDOW)])

    return gather_rows(ys, dest.reshape(n_win, SC_WINDOW)).reshape((TOPK, n_tok) + ROW_TILE)


def _expert_kernel(first_ref, nblk_ref, nvalid_ref, nused_ref, xs_hbm, wg_ref, wu_ref, wd_ref, ys_hbm,
                   wg_scr, wu_scr, wd_scr, x_buf, y_buf, in_sem, out_sem):
    e = pl.program_id(0)
    nused = nused_ref[0]

    def in_copy(g):
        slot = lax.rem(g, IN_SLOTS)
        return pltpu.make_async_copy(xs_hbm.at[pl.ds(g * BM, BM)], x_buf.at[slot], in_sem.at[slot])

    def out_copy(g, slot):
        return pltpu.make_async_copy(y_buf.at[slot], ys_hbm.at[pl.ds(g * BM, BM)], out_sem.at[slot])

    @pl.when(e == 0)
    def _first():
        for g in range(IN_AHEAD):
            in_copy(g).start()

    wg_scr[...] = wg_ref[0].astype(BF16)
    wu_scr[...] = wu_ref[0].astype(BF16)
    wd_scr[...] = wd_ref[0].astype(BF16)

    def block(b, carry):
        g = first_ref[e] + b
        slot = lax.rem(g, 2)
        in_copy(g).wait()

        @pl.when(g + IN_AHEAD < nused)
        def _prefetch():
            in_copy(g + IN_AHEAD).start()

        @pl.when(g >= 2)
        def _slot_free():
            out_copy(g - 2, slot).wait()

        valid = lax.broadcasted_iota(I32, (BM, 1), 0) < nvalid_ref[g]
        x_blk = x_buf[lax.rem(g, IN_SLOTS)].reshape(BM, HALF)
        lo, hi = _unpack_bf16_pair(jnp.where(valid, x_blk, jnp.uint32(0)))
        lo = lo.astype(BF16)
        hi = hi.astype(BF16)
        gate = jnp.dot(lo, wg_scr[0:HALF, :], preferred_element_type=F32) \
            + jnp.dot(hi, wg_scr[HALF:, :], preferred_element_type=F32)
        up = jnp.dot(lo, wu_scr[0:HALF, :], preferred_element_type=F32) \
            + jnp.dot(hi, wu_scr[HALF:, :], preferred_element_type=F32)
        act = (jax.nn.silu(gate) * up).astype(BF16)
        o = jnp.dot(act, wd_scr[...], preferred_element_type=F32)
        y_buf[slot] = _pack_bf16_pair(o[:, 0:HALF], o[:, HALF:]).reshape((BM,) + ROW_TILE)
        out_copy(g, slot).start()
        return carry

    lax.fori_loop(0, nblk_ref[e], block, 0)

    @pl.when(e == N_EXPERTS - 1)
    def _drain():
        out_copy(nused - 2, lax.rem(nused, 2)).wait()
        out_copy(nused - 1, 1 - lax.rem(nused, 2)).wait()


def _experts(first, nblk, nvalid, nused, xs, wg, wu, wd):
    grid_spec = pltpu.PrefetchScalarGridSpec(
        num_scalar_prefetch=4,
        grid=(N_EXPERTS,),
        in_specs=[
            pl.BlockSpec(memory_space=pl.ANY),
            pl.BlockSpec((1, D_MODEL, D_EXPERT), lambda e, *_: (e, 0, 0)),
            pl.BlockSpec((1, D_MODEL, D_EXPERT), lambda e, *_: (e, 0, 0)),
            pl.BlockSpec((1, D_EXPERT, D_MODEL), lambda e, *_: (e, 0, 0)),
        ],
        out_specs=pl.BlockSpec(memory_space=pl.ANY),
        scratch_shapes=[
            pltpu.VMEM((D_MODEL, D_EXPERT), BF16),
            pltpu.VMEM((D_MODEL, D_EXPERT), BF16),
            pltpu.VMEM((D_EXPERT, D_MODEL), BF16),
            pltpu.VMEM((IN_SLOTS, BM) + ROW_TILE, U32),
            pltpu.VMEM((2, BM) + ROW_TILE, U32),
            pltpu.SemaphoreType.DMA((IN_SLOTS,)),
            pltpu.SemaphoreType.DMA((2,)),
        ],
    )
    return pl.pallas_call(
        _expert_kernel,
        grid_spec=grid_spec,
        out_shape=jax.ShapeDtypeStruct((N_ROWS,) + ROW_TILE, U32),
        compiler_params=pltpu.CompilerParams(
            dimension_semantics=("arbitrary",), vmem_limit_bytes=VMEM_LIMIT),
        name="experts",
    )(first, nblk, nvalid, nused, xs, wg, wu, wd)


def _combine_kernel(x1_ref, rec_ref, yg_ref, p_ref, gple_ref, wpg_ref, wple_ref, gfin_ref, *rest):
    out_ref = rest[-1]
    ple = jnp.dot(p_ref[0].reshape(TM, D_PLE).astype(BF16), wple_ref[...], preferred_element_type=F32)
    rec = rec_ref[...]
    w0 = rec[:, REC_W0:REC_W0 + 1]
    w1 = rec[:, REC_W1:REC_W1 + 1]
    lo0, hi0 = _unpack_bf16_pair(yg_ref[0].reshape(TM, HALF))
    lo1, hi1 = _unpack_bf16_pair(yg_ref[1].reshape(TM, HALF))
    moe = jnp.concatenate([lo0 * w0 + lo1 * w1, hi0 * w0 + hi1 * w1], axis=1)
    x2 = x1_ref[...].reshape(TM, D_MODEL) + moe
    gate = jax.nn.sigmoid(jnp.dot(_rms(x2, gple_ref[...]).astype(BF16), wpg_ref[...],
                                  preferred_element_type=F32))
    x3 = x2 + gate * ple
    out_ref[...] = _rms(x3, gfin_ref[...]).reshape(BATCH, TT, D_MODEL)


def _combine(part, x1, rec, yg, p, gple, wpg, wple, gfin, out_prev=None):
    s0 = part * PART_STEPS
    seq_spec = pl.BlockSpec((BATCH, TT, D_MODEL), lambda i: (0, s0 + i, 0))
    in_specs = [
        seq_spec,
        pl.BlockSpec((TM, LANES), lambda i: (s0 + i, 0)),
        pl.BlockSpec((TOPK, TM) + ROW_TILE, lambda i: (0, i, 0, 0)),
        pl.BlockSpec((1, BATCH, TT, D_PLE), lambda i: (0, 0, s0 + i, 0)),
        _const_spec((1, D_MODEL)),
        _const_spec((D_MODEL, D_MODEL)),
        _const_spec((D_PLE, D_MODEL)),
        _const_spec((1, D_MODEL)),
    ]
    args = [x1, rec, yg, p, gple, wpg, wple, gfin]
    aliases = {}
    if out_prev is not None:
        in_specs.append(pl.BlockSpec(memory_space=pl.ANY))
        args.append(out_prev)
        aliases = {len(args) - 1: 0}
    return pl.pallas_call(
        _combine_kernel,
        grid=(PART_STEPS,),
        in_specs=in_specs,
        out_specs=seq_spec,
        out_shape=jax.ShapeDtypeStruct((BATCH, SEQ, D_MODEL), F32),
        input_output_aliases=aliases,
        compiler_params=pltpu.CompilerParams(
            dimension_semantics=("arbitrary",), vmem_limit_bytes=VMEM_LIMIT),
        name="combine",
    )(*args)


def kernel(x, p, g_mix, w_in, b_gate, ssm_a_re, ssm_a_im, ssm_log_dt, ssm_b_re, ssm_b_im, ssm_c_re,
           ssm_c_im, ssm_d, w_glu, conv_dw, conv_dw_b, conv_ln_g, conv_ln_b, w_conv_out, w_out, g_moe,
           w_router_group, b_router_group, w_router_expert, b_router_expert, w_exp_gate, w_exp_up,
           w_exp_down, g_ple, w_ple_gate, w_ple, g_final):
    assert x.shape == (BATCH, SEQ, D_MODEL) and p.shape == (1, BATCH, SEQ, D_PLE)
    row = lambda v: v.reshape(1, -1)

    mp, rmat, a_re, a_im = _ssm_matrices(ssm_a_re[0], ssm_a_im[0], ssm_log_dt[0], ssm_b_re[0],
                                         ssm_b_im[0], ssm_c_re[0], ssm_c_im[0])
    wr1, br = _router_weights(w_router_group[0], b_router_group[0], w_router_expert[0],
                                   b_router_expert[0])
    x1, h2p, rec, rect, cnt = _mixer(
        x, row(g_mix[0]), w_in[0].astype(BF16), row(b_gate[0]), mp, rmat, a_re, a_im,
        row(ssm_d[0]), w_glu[0].astype(BF16), conv_dw[0], row(conv_dw_b[0]), row(conv_ln_g[0]),
        row(conv_ln_b[0]), w_conv_out[0].astype(BF16), w_out[0].astype(BF16), row(g_moe[0]), wr1, br)

    counts = cnt[0, LANE_EXP0:LANE_EXP0 + N_EXPERTS].astype(I32)
    pcounts = (counts + BM - 1) // BM * BM
    pends = jnp.cumsum(pcounts)
    pstarts = pends - pcounts
    eid = rect[REC_EID0:REC_EID1 + 1].astype(I32)
    rank = rect[REC_RANK0:REC_RANK1 + 1].astype(I32)
    dest = (jnp.sum(jnp.where(eid[..., None] == jnp.arange(N_EXPERTS, dtype=I32), pstarts, 0), axis=-1)
            + rank).reshape(TOPK, N_TOK // SC_WINDOW, SC_WINDOW)
    nused = (pends[-1] // BM).astype(I32)
    blk = jnp.arange(N_BLK, dtype=I32)[:, None] * BM
    in_expert = (pstarts[None, :] <= blk) & (blk < pends[None, :])
    nvalid = jnp.clip(jnp.sum(jnp.where(in_expert, (pstarts + counts)[None, :] - blk, 0), axis=1), 0, BM)

    xs = _dispatch(h2p, dest)
    ys = _experts(pstarts // BM, pcounts // BM, nvalid.astype(I32), nused.reshape(1), xs,
                  w_exp_gate[0], w_exp_up[0], w_exp_down[0])
    dest_tok = dest.reshape(TOPK, N_TOK)
    wpg = w_ple_gate[0].astype(BF16)
    wple = w_ple[0].astype(BF16)
    out = None
    for part in range(N_PARTS):
        tok = slice(part * PART_STEPS * TM, (part + 1) * PART_STEPS * TM)
        yg = _collect(ys, dest_tok[:, tok])
        out = _combine(part, x1, rec, yg, p, row(g_ple[0]), wpg, wple, row(g_final), out)
    return out
```

```python
import jax
import jax.numpy as jnp
from jax import lax
from jax.experimental import pallas as pl
from jax.experimental.pallas import tpu as pltpu
from jax.experimental.pallas import tpu_sc as plsc

F32 = jnp.float32
BF16 = jnp.bfloat16
U32 = jnp.uint32
I32 = jnp.int32

D_MODEL = 1024
BATCH = 8
SEQ = 2048
N_TOK = BATCH * SEQ
D_SSM = 512
SSM_GROUP_WIDTH = 16
SSM_GROUPS = 32
SSM_STATE = 64
D_CONV = 512
CONV_WIDTH = 31
D_IN = D_SSM + 2 * D_CONV + 2 * D_MODEL
N_GROUPS_MOE = 4
EXPERTS_PER_GROUP = 8
N_EXPERTS = 32
TOPK = 2
D_EXPERT = 512
D_PLE = 256
EPS = 1e-6

SUBLANES = 8
LANES = 128
assert BATCH == SUBLANES

TT = 64
TM = TT * BATCH
N_STEP = SEQ // TT
SB = 512
NSB = TM // SB
BPS = SB // TT
Q = 2
N_SLAB = D_SSM // LANES
GROUPS_PER_SLAB = SSM_GROUPS // N_SLAB
ROWS_Z = TM // Q
STATE_LANES = 2 * GROUPS_PER_SLAB * SSM_STATE
HALO = (CONV_WIDTH - 1) * BATCH
CHUNK_ROWS = SB // (Q * SUBLANES)
CONV_ROWS = 64
N_LC = D_CONV // LANES

LANE_GRP0 = 0
LANE_EXP0 = 32
REC_EID0, REC_EID1, REC_W0, REC_W1, REC_RANK0, REC_RANK1 = 0, 1, 2, 3, 4, 5
REC_ROWS = 8

BM = 256
N_BLK = (TOPK * N_TOK + N_EXPERTS * (BM - 1) + BM - 1) // BM
N_ROWS = N_BLK * BM
HALF = D_MODEL // 2
ROW_TILE = (HALF // LANES, LANES)
SC_WINDOW = 128
IN_AHEAD = 3
IN_SLOTS = IN_AHEAD + 1
N_PARTS = 2
PART_STEPS = N_STEP // N_PARTS

VMEM_LIMIT = 56 * 1024 * 1024


def _const_spec(shape):
    n = len(shape)
    return pl.BlockSpec(shape, lambda *_: (0,) * n, pipeline_mode=pl.Buffered(1))


def _rms(x, g):
    ms = jnp.mean(x * x, axis=-1, keepdims=True)
    return x * lax.rsqrt(ms + EPS) * g


def _pack_bf16_pair(lo, hi):
    ulo = lax.bitcast_convert_type(lo.astype(BF16).astype(F32), U32)
    uhi = lax.bitcast_convert_type(hi.astype(BF16).astype(F32), U32)
    return (ulo >> 16) | (uhi & jnp.uint32(0xFFFF0000))


def _unpack_bf16_pair(w):
    lo = lax.bitcast_convert_type(w << 16, F32)
    hi = lax.bitcast_convert_type(w & jnp.uint32(0xFFFF0000), F32)
    return lo, hi


def _mixer_kernel(x_ref, gmix_ref, win_ref, bgate_ref, mp_ref, r_ref, are_ref,
                  aim_ref, d_ref, wglu_ref, dw_ref, dwb_ref, lng_ref, lnb_ref, wco_ref, wout_ref,
                  gmoe_ref, wr1_ref, wr2_ref, br_ref,
                  x1_ref, h2p_ref, rec_ref, rect_ref, cnt_ref,
                  hb_scr, ht_scr, u_scr, y_scr, yi_scr, xs_scr, z_scr, conv_scr, act_scr, actb_scr,
                  logit_scr, s_scr, cnt_scr):
    step = pl.program_id(0)
    assert NSB == 1

    @pl.when(step == 0)
    def _init():
        logit_scr[...] = jnp.zeros(logit_scr.shape, F32)
        z_scr[:, 0:HALO, :] = jnp.zeros((N_LC, HALO, LANES), F32)
        s_scr[...] = jnp.zeros(s_scr.shape, F32)
        cnt_scr[...] = jnp.zeros(cnt_scr.shape, F32)

    def sub_rows(r):
        return pl.ds(pl.multiple_of(r * SB, SB), SB)

    def phase_a(r, carry):
        xb = x_ref[pl.ds(r * BPS, BPS)].reshape(SB, D_MODEL)
        hb_scr[sub_rows(r), :] = _rms(xb, gmix_ref[...]).astype(BF16)
        return carry

    def phase_a3(r, carry):
        h = ht_scr[sub_rows(r), :]
        u = jnp.dot(h, win_ref[:, 0:D_SSM], preferred_element_type=F32)
        u_scr[pl.ds(r * CHUNK_ROWS, CHUNK_ROWS)] = u.reshape(CHUNK_ROWS, Q, SUBLANES, D_SSM)
        v = jnp.dot(h, win_ref[:, D_SSM:D_SSM + 2 * D_CONV], preferred_element_type=F32)
        zc = v[:, 0:D_CONV] * jax.nn.sigmoid(v[:, D_CONV:])
        for lc in range(N_LC):
            z_scr[lc, pl.ds(pl.multiple_of(HALO + r * SB, SUBLANES), SB), :] = zc[:, lc * LANES:(lc + 1) * LANES]
        return carry

    def phase_b():
        for s in range(N_SLAB):
            lanes = slice(s * LANES, (s + 1) * LANES)
            z = jnp.concatenate(
                [u_scr[:, i, :, lanes].reshape(ROWS_Z, LANES) for i in range(Q)], axis=1).astype(BF16)
            xp = jnp.dot(z, mp_ref[s], preferred_element_type=F32)
            yi_scr[s] = xp[:, 0:Q * LANES]
            xs_scr[s] = xp[:, Q * LANES:]

        half = STATE_LANES // 2
        for s in range(N_SLAB):
            a_re = jnp.broadcast_to(are_ref[s:s + 1, :], (SUBLANES, half))
            a_im = jnp.broadcast_to(aim_ref[s:s + 1, :], (SUBLANES, half))

            def scan_body(k, carry, s=s, a_re=a_re, a_im=a_im):
                s_re, s_im = carry
                rows = pl.ds(pl.multiple_of(k * SUBLANES, SUBLANES), SUBLANES)
                x_re = xs_scr[s, rows, 0:half]
                x_im = xs_scr[s, rows, half:]
                xs_scr[s, rows, 0:half] = s_re
                xs_scr[s, rows, half:] = s_im
                n_re = a_re * s_re - a_im * s_im + x_re
                n_im = a_re * s_im + a_im * s_re + x_im
                return n_re, n_im

            s_re, s_im = lax.fori_loop(0, ROWS_Z // SUBLANES, scan_body,
                                       (s_scr[s, :, 0:half], s_scr[s, :, half:]), unroll=True)
            s_scr[s, :, 0:half] = s_re
            s_scr[s, :, half:] = s_im

        for s in range(N_SLAB):
            lanes = slice(s * LANES, (s + 1) * LANES)
            y_tot = yi_scr[s] + jnp.dot(xs_scr[s].astype(BF16), r_ref[s], preferred_element_type=F32)
            for j in range(Q):
                y_scr[:, j, :, lanes] = y_tot[:, j * LANES:(j + 1) * LANES].reshape(
                    ROWS_Z // SUBLANES, SUBLANES, LANES)

    def phase_c1(r, carry):
        rows = sub_rows(r)
        crow = pl.ds(r * CHUNK_ROWS, CHUNK_ROWS)
        y = y_scr[crow].reshape(SB, D_SSM) + d_ref[...] * u_scr[crow].reshape(SB, D_SSM)
        act_scr[rows, 0:D_SSM] = jax.nn.gelu(y).astype(BF16)
        for lc in range(N_LC):
            lanes = slice(lc * LANES, (lc + 1) * LANES)

            def conv_piece(rc, c, lc=lc, lanes=lanes):
                r0 = r * SB + rc * CONV_ROWS
                piece = jnp.broadcast_to(dwb_ref[:, lanes], (CONV_ROWS, LANES))
                for j in range(CONV_WIDTH):
                    zrows = pl.ds(pl.multiple_of(r0 + j * BATCH, SUBLANES), CONV_ROWS)
                    piece = piece + dw_ref[j:j + 1, lanes] * z_scr[lc, zrows, :]
                conv_scr[pl.ds(pl.multiple_of(rc * CONV_ROWS, CONV_ROWS), CONV_ROWS), lanes] = piece
                return c

            lax.fori_loop(0, SB // CONV_ROWS, conv_piece, 0, unroll=4)
        acc = conv_scr[...]
        mu = jnp.mean(acc, axis=-1, keepdims=True)
        cen = acc - mu
        var = jnp.mean(cen * cen, axis=-1, keepdims=True)
        ln = cen * lax.rsqrt(var + EPS) * lng_ref[...] + lnb_ref[...]
        act_scr[rows, D_SSM:] = jax.nn.silu(ln).astype(BF16)
        return carry

    lane = lax.broadcasted_iota(I32, (1, LANES), 1).astype(F32)
    grp_mask = lane < float(N_GROUPS_MOE)
    exp_lane = (lane >= float(LANE_EXP0)) & (lane < float(LANE_EXP0 + N_EXPERTS))
    lane_grp = jnp.floor((lane - float(LANE_EXP0)) * (1.0 / EXPERTS_PER_GROUP))
    tri = (lax.broadcasted_iota(I32, (SB, SB), 0) > lax.broadcasted_iota(I32, (SB, SB), 1)).astype(BF16)
    neg_inf = float("-inf")
    big = float(4 * LANES)

    def phase_c3(r, carry):
        rows = sub_rows(r)
        h = hb_scr[rows, :]
        g0 = D_SSM + 2 * D_CONV
        gate_ssm = jnp.dot(h, win_ref[:, g0:g0 + D_MODEL], preferred_element_type=F32) \
            + bgate_ref[:, 0:D_MODEL]
        gate_conv = jnp.dot(h, win_ref[:, g0 + D_MODEL:], preferred_element_type=F32) \
            + bgate_ref[:, D_MODEL:]
        zz = jnp.dot(actb_scr[rows, 0:D_SSM], wglu_ref[...], preferred_element_type=F32)
        y_ssm = zz[:, 0:D_MODEL] * jax.nn.sigmoid(zz[:, D_MODEL:])
        y_conv = jnp.dot(actb_scr[rows, D_SSM:], wco_ref[...], preferred_element_type=F32)

        merged = jax.nn.sigmoid(gate_ssm) * y_ssm + jax.nn.sigmoid(gate_conv) * y_conv
        xb = x_ref[pl.ds(r * BPS, BPS)].reshape(SB, D_MODEL)
        x1 = xb + jnp.dot(merged.astype(BF16), wout_ref[...], preferred_element_type=F32)
        x1_ref[pl.ds(r * BPS, BPS)] = x1.reshape(BPS, TT, D_MODEL)

        h2 = _rms(x1, gmoe_ref[...])
        h2p_ref[rows] = _pack_bf16_pair(h2[:, 0:HALF], h2[:, HALF:]).reshape((SB,) + ROW_TILE)

        h2_hi = h2.astype(BF16)
        h2_lo = (h2 - h2_hi.astype(F32)).astype(BF16)
        l1 = jnp.dot(h2_hi, wr1_ref[...], preferred_element_type=F32)
        l2 = jnp.dot(h2_lo, wr2_ref[...], preferred_element_type=F32)
        logit_scr[rows, :] = l1[:, 0:LANES] + l1[:, LANES:] + l2 + br_ref[...]
        return carry

    def route_previous():
        rows = sub_rows(0)
        logits = logit_scr[...]
        counted = jnp.where(step > 0, 1.0, 0.0)

        lg = jnp.where(grp_mask, logits, neg_inf)
        g_max = jnp.max(lg, axis=-1, keepdims=True)
        g_sel = jnp.min(jnp.where(lg == g_max, lane, big), axis=-1, keepdims=True)
        p_g = 1.0 / jnp.sum(jnp.where(grp_mask, jnp.exp(logits - g_max), 0.0), axis=-1, keepdims=True)
        le = jnp.where(exp_lane & (lane_grp == g_sel), logits, neg_inf)
        m1 = jnp.max(le, axis=-1, keepdims=True)
        i1 = jnp.min(jnp.where(le == m1, lane, big), axis=-1, keepdims=True)
        le2 = jnp.where(lane == i1, neg_inf, le)
        m2 = jnp.max(le2, axis=-1, keepdims=True)
        i2 = jnp.min(jnp.where(le2 == m2, lane, big), axis=-1, keepdims=True)
        e2 = jnp.exp(m2 - m1)
        den = 1.0 + e2
        w_a = (1.0 / den) * p_g
        w_b = (e2 / den) * p_g

        sel1 = lane == i1
        sel2 = lane == i2
        onehot = jnp.where(sel1 | sel2, counted, 0.0)
        prefix = jnp.dot(tri, onehot.astype(BF16), preferred_element_type=F32) + cnt_scr[...]
        rank_a = jnp.sum(jnp.where(sel1, prefix, 0.0), axis=-1, keepdims=True)
        rank_b = jnp.sum(jnp.where(sel2, prefix, 0.0), axis=-1, keepdims=True)
        cnt_scr[...] = cnt_scr[...] + jnp.sum(onehot, axis=0, keepdims=True)

        rec = jnp.where(lane == float(REC_EID0), i1 - float(LANE_EXP0), 0.0)
        rec = jnp.where(lane == float(REC_EID1), i2 - float(LANE_EXP0), rec)
        rec = jnp.where(lane == float(REC_W0), w_a, rec)
        rec = jnp.where(lane == float(REC_W1), w_b, rec)
        rec = jnp.where(lane == float(REC_RANK0), rank_a, rec)
        rec = jnp.where(lane == float(REC_RANK1), rank_b, rec)
        rec_ref[rows, :] = rec
        rect_ref[...] = jnp.transpose(rec)[0:REC_ROWS, :]
        cnt_ref[...] = cnt_scr[...]

    @pl.when(step < N_STEP)
    def _tile():
        route_previous()
        phase_a(0, 0)
        ht_scr[...] = jnp.swapaxes(hb_scr[...].reshape(BATCH, TT, D_MODEL), 0, 1).reshape(TM, D_MODEL)
        phase_a3(0, 0)
        phase_b()
        phase_c1(0, 0)
        z_scr[:, 0:HALO, :] = z_scr[:, TM:TM + HALO, :]
        actb_scr[...] = jnp.swapaxes(act_scr[...].reshape(TT, BATCH, D_SSM + D_CONV), 0, 1).reshape(
            TM, D_SSM + D_CONV)
        phase_c3(0, 0)

    @pl.when(step == N_STEP)
    def _last():
        route_previous()


def _mixer(x, gmix, win, bgate, mp, rmat, a_re, a_im, dvec, wglu, dw, dwb, lng, lnb, wco,
           wout, gmoe, wr1, wr2, br):
    tile = lambda i: jnp.minimum(i, N_STEP - 1)
    routed = lambda i: jnp.maximum(i - 1, 0)
    seq_spec = pl.BlockSpec((BATCH, TT, D_MODEL), lambda i: (0, tile(i), 0))
    in_specs = [
        seq_spec,
        _const_spec((1, D_MODEL)),
        _const_spec((D_MODEL, D_IN)),
        _const_spec((1, 2 * D_MODEL)),
        _const_spec(mp.shape),
        _const_spec(rmat.shape),
        _const_spec(a_re.shape),
        _const_spec(a_im.shape),
        _const_spec((1, D_SSM)),
        _const_spec((D_SSM, 2 * D_MODEL)),
        _const_spec((CONV_WIDTH, D_CONV)),
        _const_spec((1, D_CONV)),
        _const_spec((1, D_CONV)),
        _const_spec((1, D_CONV)),
        _const_spec((D_CONV, D_MODEL)),
        _const_spec((D_MODEL, D_MODEL)),
        _const_spec((1, D_MODEL)),
        _const_spec((D_MODEL, 2 * LANES)),
        _const_spec((D_MODEL, LANES)),
        _const_spec((1, LANES)),
    ]
    out_specs = [
        seq_spec,
        pl.BlockSpec((TM,) + ROW_TILE, lambda i: (tile(i), 0, 0)),
        pl.BlockSpec((TM, LANES), lambda i: (routed(i), 0)),
        pl.BlockSpec((REC_ROWS, TM), lambda i: (0, routed(i))),
        pl.BlockSpec((1, LANES), lambda i: (0, 0)),
    ]
    out_shape = [
        jax.ShapeDtypeStruct((BATCH, SEQ, D_MODEL), F32),
        jax.ShapeDtypeStruct((N_TOK,) + ROW_TILE, U32),
        jax.ShapeDtypeStruct((N_TOK, LANES), F32),
        jax.ShapeDtypeStruct((REC_ROWS, N_TOK), F32),
        jax.ShapeDtypeStruct((1, LANES), F32),
    ]
    chunk_shape = (ROWS_Z // SUBLANES, Q, SUBLANES, D_SSM)
    scratch = [
        pltpu.VMEM((TM, D_MODEL), BF16),
        pltpu.VMEM((TM, D_MODEL), BF16),
        pltpu.VMEM(chunk_shape, F32),
        pltpu.VMEM(chunk_shape, F32),
        pltpu.VMEM((N_SLAB, ROWS_Z, Q * LANES), F32),
        pltpu.VMEM((N_SLAB, ROWS_Z, STATE_LANES), F32),
        pltpu.VMEM((N_LC, HALO + TM, LANES), F32),
        pltpu.VMEM((SB, D_CONV), F32),
        pltpu.VMEM((TM, D_SSM + D_CONV), BF16),
        pltpu.VMEM((TM, D_SSM + D_CONV), BF16),
        pltpu.VMEM((TM, LANES), F32),
        pltpu.VMEM((N_SLAB, SUBLANES, STATE_LANES), F32),
        pltpu.VMEM((1, LANES), F32),
    ]
    return pl.pallas_call(
        _mixer_kernel,
        grid=(N_STEP + 1,),
        in_specs=in_specs,
        out_specs=out_specs,
        out_shape=out_shape,
        scratch_shapes=scratch,
        compiler_params=pltpu.CompilerParams(
            dimension_semantics=("arbitrary",), vmem_limit_bytes=VMEM_LIMIT),
        name="mixer",
    )(x, gmix, win, bgate, mp, rmat, a_re, a_im, dvec, wglu, dw, dwb, lng, lnb, wco, wout,
      gmoe, wr1, wr2, br)


def _cmul(a, b):
    return a[0] * b[0] - a[1] * b[1], a[0] * b[1] + a[1] * b[0]


def _ssm_matrices(a_re, a_im, log_dt, b_re, b_im, c_re, c_im):
    dt = jnp.exp(log_dt)[:, None]
    mag = jnp.exp(a_re * dt)
    lam = (mag * jnp.cos(a_im * dt), mag * jnp.sin(a_im * dt))
    den = a_re * a_re + a_im * a_im
    nr = lam[0] - 1.0
    ni = lam[1]
    z_re = (nr * a_re + ni * a_im) / den
    z_im = (ni * a_re - nr * a_im) / den
    bbar = (z_re[..., None] * b_re - z_im[..., None] * b_im,
            z_re[..., None] * b_im + z_im[..., None] * b_re)
    pw = [(jnp.ones_like(lam[0]), jnp.zeros_like(lam[0])), lam]
    for _ in range(2, Q + 1):
        pw.append(_cmul(pw[-1], lam))
    e = [(c_re * p[0][:, None, :] - c_im * p[1][:, None, :],
          c_re * p[1][:, None, :] + c_im * p[0][:, None, :]) for p in pw]
    hp = lax.Precision.HIGHEST
    k = [jnp.einsum('gcn,gnd->gcd', e[m][0], bbar[0], precision=hp)
         - jnp.einsum('gcn,gnd->gcd', e[m][1], bbar[1], precision=hp) for m in range(Q)]
    eye = jnp.eye(GROUPS_PER_SLAB, dtype=F32)
    split = lambda t: t.reshape((N_SLAB, GROUPS_PER_SLAB) + t.shape[1:])
    zero_k = jnp.zeros_like(k[0])
    kb = jnp.stack([jnp.stack([split(jnp.swapaxes(k[j - i] if j >= i else zero_k, 1, 2))
                               for j in range(Q)]) for i in range(Q)])
    m_mat = jnp.einsum('ijsgdc,gh->sigdjhc', kb, eye).reshape(N_SLAB, Q * LANES, Q * LANES)
    f = [_cmul((pw[Q - 1 - i][0][..., None], pw[Q - 1 - i][1][..., None]), bbar) for i in range(Q)]
    p_parts = []
    for part in range(2):
        fs = jnp.stack([split(f[i][part]) for i in range(Q)])
        p_parts.append(jnp.einsum('isgnd,gh->sigdhn', fs, eye).reshape(N_SLAB, Q * LANES, STATE_LANES // 2))
    p_mat = jnp.concatenate(p_parts, axis=-1)
    r_parts = []
    for part, sign in ((0, 1.0), (1, -1.0)):
        es = jnp.stack([split(e[j + 1][part]) for j in range(Q)])
        r_parts.append(sign * jnp.einsum('jsgcn,gh->shnjgc', es, eye).reshape(
            N_SLAB, STATE_LANES // 2, Q * LANES))
    r_mat = jnp.concatenate(r_parts, axis=1)
    mp = jnp.concatenate([m_mat, p_mat], axis=-1).astype(BF16)
    a_q = pw[Q]
    return (mp, r_mat.astype(BF16),
            a_q[0].reshape(N_SLAB, STATE_LANES // 2), a_q[1].reshape(N_SLAB, STATE_LANES // 2))


def _router_weights(w_rg, b_rg, w_re, b_re):
    pad_g = LANE_EXP0 - LANE_GRP0 - N_GROUPS_MOE
    pad_e = LANES - LANE_EXP0 - N_EXPERTS
    w = jnp.concatenate([w_rg, jnp.zeros((D_MODEL, pad_g), F32), w_re, jnp.zeros((D_MODEL, pad_e), F32)], axis=1)
    b = jnp.concatenate([b_rg, jnp.zeros((pad_g,), F32), b_re, jnp.zeros((pad_e,), F32)]).reshape(1, LANES)
    w_hi = w.astype(BF16)
    w_lo = (w - w_hi.astype(F32)).astype(BF16)
    return jnp.concatenate([w_hi, w_lo], axis=1), w_hi, b


def _sc_mesh():
    return plsc.VectorSubcoreMesh(core_axis_name="core", subcore_axis_name="subcore")


def _sc_worker(mesh):
    return lax.axis_index("core") * mesh.num_subcores + lax.axis_index("subcore")


def _dispatch(h2p, dest):
    mesh = _sc_mesh()
    n_win = N_TOK // SC_WINDOW
    per_worker = n_win // (mesh.num_cores * mesh.num_subcores)
    assert per_worker * mesh.num_cores * mesh.num_subcores == n_win

    @pl.kernel(out_type=jax.ShapeDtypeStruct((N_ROWS,) + ROW_TILE, U32), mesh=mesh,
               scratch_types=[pltpu.VMEM((SC_WINDOW,), I32), pltpu.VMEM((SC_WINDOW,) + ROW_TILE, U32)])
    def scatter_rows(h_hbm, dest_hbm, xs_hbm, idx_v, rows_v):
        first = _sc_worker(mesh) * per_worker

        @pl.loop(0, per_worker)
        def _(w):
            win = first + w
            pltpu.sync_copy(h_hbm.at[pl.ds(win * SC_WINDOW, SC_WINDOW)], rows_v)
            for j in range(TOPK):
                pltpu.sync_copy(dest_hbm.at[j, win], idx_v)
                pltpu.sync_copy(rows_v, xs_hbm.at[idx_v])

    return scatter_rows(h2p, dest)


def _collect(ys, dest):
    mesh = _sc_mesh()
    n_tok = dest.shape[1]
    n_win = TOPK * n_tok // SC_WINDOW
    per_worker = n_win // (mesh.num_cores * mesh.num_subcores)
    assert per_worker * mesh.num_cores * mesh.num_subcores == n_win

    @pl.kernel(out_type=jax.ShapeDtypeStruct((TOPK * n_tok,) + ROW_TILE, U32), mesh=mesh,
               scratch_types=[pltpu.VMEM((SC_WINDOW,), I32), pltpu.VMEM((SC_WINDOW,) + ROW_TILE, U32)])
    def gather_rows(ys_hbm, dest_hbm, yg_hbm, idx_v, rows_v):
        first = _sc_worker(mesh) * per_worker

        @pl.loop(0, per_worker)
        def _(w):
            win = first + w
            pltpu.sync_copy(dest_hbm.at[win], idx_v)
            pltpu.sync_copy(ys_hbm.at[idx_v], rows_v)
            pltpu.sync_copy(rows_v, yg_hbm.at[pl.ds(win * SC_WINDOW, SC_WINDOW)])

    return gather_rows(ys, dest.reshape(n_win, SC_WINDOW)).reshape((TOPK, n_tok) + ROW_TILE)


def _expert_kernel(first_ref, nblk_ref, nvalid_ref, nused_ref, xs_hbm, wg_ref, wu_ref, wd_ref, ys_hbm,
                   wg_scr, wu_scr, wd_scr, x_buf, y_buf, in_sem, out_sem):
    e = pl.program_id(0)
    nused = nused_ref[0]

    def in_copy(g):
        slot = lax.rem(g, IN_SLOTS)
        return pltpu.make_async_copy(xs_hbm.at[pl.ds(g * BM, BM)], x_buf.at[slot], in_sem.at[slot])

    def out_copy(g, slot):
        return pltpu.make_async_copy(y_buf.at[slot], ys_hbm.at[pl.ds(g * BM, BM)], out_sem.at[slot])

    @pl.when(e == 0)
    def _first():
        for g in range(IN_AHEAD):
            in_copy(g).start()

    wg_scr[...] = wg_ref[0].astype(BF16)
    wu_scr[...] = wu_ref[0].astype(BF16)
    wd_scr[...] = wd_ref[0].astype(BF16)

    def block(b, carry):
        g = first_ref[e] + b
        slot = lax.rem(g, 2)
        in_copy(g).wait()

        @pl.when(g + IN_AHEAD < nused)
        def _prefetch():
            in_copy(g + IN_AHEAD).start()

        @pl.when(g >= 2)
        def _slot_free():
            out_copy(g - 2, slot).wait()

        valid = lax.broadcasted_iota(I32, (BM, 1), 0) < nvalid_ref[g]
        x_blk = x_buf[lax.rem(g, IN_SLOTS)].reshape(BM, HALF)
        lo, hi = _unpack_bf16_pair(jnp.where(valid, x_blk, jnp.uint32(0)))
        lo = lo.astype(BF16)
        hi = hi.astype(BF16)
        gate = jnp.dot(lo, wg_scr[0:HALF, :], preferred_element_type=F32) \
            + jnp.dot(hi, wg_scr[HALF:, :], preferred_element_type=F32)
        up = jnp.dot(lo, wu_scr[0:HALF, :], preferred_element_type=F32) \
            + jnp.dot(hi, wu_scr[HALF:, :], preferred_element_type=F32)
        act = (jax.nn.silu(gate) * up).astype(BF16)
        o = jnp.dot(act, wd_scr[...], preferred_element_type=F32)
        y_buf[slot] = _pack_bf16_pair(o[:, 0:HALF], o[:, HALF:]).reshape((BM,) + ROW_TILE)
        out_copy(g, slot).start()
        return carry

    lax.fori_loop(0, nblk_ref[e], block, 0)

    @pl.when(e == N_EXPERTS - 1)
    def _drain():
        out_copy(nused - 2, lax.rem(nused, 2)).wait()
        out_copy(nused - 1, 1 - lax.rem(nused, 2)).wait()


def _experts(first, nblk, nvalid, nused, xs, wg, wu, wd):
    grid_spec = pltpu.PrefetchScalarGridSpec(
        num_scalar_prefetch=4,
        grid=(N_EXPERTS,),
        in_specs=[
            pl.BlockSpec(memory_space=pl.ANY),
            pl.BlockSpec((1, D_MODEL, D_EXPERT), lambda e, *_: (e, 0, 0)),
            pl.BlockSpec((1, D_MODEL, D_EXPERT), lambda e, *_: (e, 0, 0)),
            pl.BlockSpec((1, D_EXPERT, D_MODEL), lambda e, *_: (e, 0, 0)),
        ],
        out_specs=pl.BlockSpec(memory_space=pl.ANY),
        scratch_shapes=[
            pltpu.VMEM((D_MODEL, D_EXPERT), BF16),
            pltpu.VMEM((D_MODEL, D_EXPERT), BF16),
            pltpu.VMEM((D_EXPERT, D_MODEL), BF16),
            pltpu.VMEM((IN_SLOTS, BM) + ROW_TILE, U32),
            pltpu.VMEM((2, BM) + ROW_TILE, U32),
            pltpu.SemaphoreType.DMA((IN_SLOTS,)),
            pltpu.SemaphoreType.DMA((2,)),
        ],
    )
    return pl.pallas_call(
        _expert_kernel,
        grid_spec=grid_spec,
        out_shape=jax.ShapeDtypeStruct((N_ROWS,) + ROW_TILE, U32),
        compiler_params=pltpu.CompilerParams(
            dimension_semantics=("arbitrary",), vmem_limit_bytes=VMEM_LIMIT),
        name="experts",
    )(first, nblk, nvalid, nused, xs, wg, wu, wd)


def _combine_kernel(x1_ref, rec_ref, yg_ref, p_ref, gple_ref, wpg_ref, wple_ref, gfin_ref, *rest):
    out_ref = rest[-1]
    ple = jnp.dot(p_ref[0].reshape(TM, D_PLE).astype(BF16), wple_ref[...], preferred_element_type=F32)
    rec = rec_ref[...]
    w0 = rec[:, REC_W0:REC_W0 + 1]
    w1 = rec[:, REC_W1:REC_W1 + 1]
    lo0, hi0 = _unpack_bf16_pair(yg_ref[0].reshape(TM, HALF))
    lo1, hi1 = _unpack_bf16_pair(yg_ref[1].reshape(TM, HALF))
    moe = jnp.concatenate([lo0 * w0 + lo1 * w1, hi0 * w0 + hi1 * w1], axis=1)
    x2 = x1_ref[...].reshape(TM, D_MODEL) + moe
    gate = jax.nn.sigmoid(jnp.dot(_rms(x2, gple_ref[...]).astype(BF16), wpg_ref[...],
                                  preferred_element_type=F32))
    x3 = x2 + gate * ple
    out_ref[...] = _rms(x3, gfin_ref[...]).reshape(BATCH, TT, D_MODEL)


def _combine(part, x1, rec, yg, p, gple, wpg, wple, gfin, out_prev=None):
    s0 = part * PART_STEPS
    seq_spec = pl.BlockSpec((BATCH, TT, D_MODEL), lambda i: (0, s0 + i, 0))
    in_specs = [
        seq_spec,
        pl.BlockSpec((TM, LANES), lambda i: (s0 + i, 0)),
        pl.BlockSpec((TOPK, TM) + ROW_TILE, lambda i: (0, i, 0, 0)),
        pl.BlockSpec((1, BATCH, TT, D_PLE), lambda i: (0, 0, s0 + i, 0)),
        _const_spec((1, D_MODEL)),
        _const_spec((D_MODEL, D_MODEL)),
        _const_spec((D_PLE, D_MODEL)),
        _const_spec((1, D_MODEL)),
    ]
    args = [x1, rec, yg, p, gple, wpg, wple, gfin]
    aliases = {}
    if out_prev is not None:
        in_specs.append(pl.BlockSpec(memory_space=pl.ANY))
        args.append(out_prev)
        aliases = {len(args) - 1: 0}
    return pl.pallas_call(
        _combine_kernel,
        grid=(PART_STEPS,),
        in_specs=in_specs,
        out_specs=seq_spec,
        out_shape=jax.ShapeDtypeStruct((BATCH, SEQ, D_MODEL), F32),
        input_output_aliases=aliases,
        compiler_params=pltpu.CompilerParams(
            dimension_semantics=("arbitrary",), vmem_limit_bytes=VMEM_LIMIT),
        name="combine",
    )(*args)


def kernel(x, p, g_mix, w_in, b_gate, ssm_a_re, ssm_a_im, ssm_log_dt, ssm_b_re, ssm_b_im, ssm_c_re,
           ssm_c_im, ssm_d, w_glu, conv_dw, conv_dw_b, conv_ln_g, conv_ln_b, w_conv_out, w_out, g_moe,
           w_router_group, b_router_group, w_router_expert, b_router_expert, w_exp_gate, w_exp_up,
           w_exp_down, g_ple, w_ple_gate, w_ple, g_final):
    assert x.shape == (BATCH, SEQ, D_MODEL) and p.shape == (1, BATCH, SEQ, D_PLE)
    row = lambda v: v.reshape(1, -1)

    mp, rmat, a_re, a_im = _ssm_matrices(ssm_a_re[0], ssm_a_im[0], ssm_log_dt[0], ssm_b_re[0],
                                         ssm_b_im[0], ssm_c_re[0], ssm_c_im[0])
    wr1, wr2, br = _router_weights(w_router_group[0], b_router_group[0], w_router_expert[0],
                                   b_router_expert[0])
    x1, h2p, rec, rect, cnt = _mixer(
        x, row(g_mix[0]), w_in[0].astype(BF16), row(b_gate[0]), mp, rmat, a_re, a_im,
        row(ssm_d[0]), w_glu[0].astype(BF16), conv_dw[0], row(conv_dw_b[0]), row(conv_ln_g[0]),
        row(conv_ln_b[0]), w_conv_out[0].astype(BF16), w_out[0].astype(BF16), row(g_moe[0]), wr1, wr2, br)

    counts = cnt[0, LANE_EXP0:LANE_EXP0 + N_EXPERTS].astype(I32)
    pcounts = (counts + BM - 1) // BM * BM
    pends = jnp.cumsum(pcounts)
    pstarts = pends - pcounts
    eid = rect[REC_EID0:REC_EID1 + 1].astype(I32)
    rank = rect[REC_RANK0:REC_RANK1 + 1].astype(I32)
    dest = (jnp.sum(jnp.where(eid[..., None] == jnp.arange(N_EXPERTS, dtype=I32), pstarts, 0), axis=-1)
            + rank).reshape(TOPK, N_TOK // SC_WINDOW, SC_WINDOW)
    nused = (pends[-1] // BM).astype(I32)
    blk = jnp.arange(N_BLK, dtype=I32)[:, None] * BM
    in_expert = (pstarts[None, :] <= blk) & (blk < pends[None, :])
    nvalid = jnp.clip(jnp.sum(jnp.where(in_expert, (pstarts + counts)[None, :] - blk, 0), axis=1), 0, BM)

    xs = _dispatch(h2p, dest)
    ys = _experts(pstarts // BM, pcounts // BM, nvalid.astype(I32), nused.reshape(1), xs,
                  w_exp_gate[0], w_exp_up[0], w_exp_down[0])
    dest_tok = dest.reshape(TOPK, N_TOK)
    wpg = w_ple_gate[0].astype(BF16)
    wple = w_ple[0].astype(BF16)
    out = None
    for part in range(N_PARTS):
        tok = slice(part * PART_STEPS * TM, (part + 1) * PART_STEPS * TM)
        yg = _collect(ys, dest_tok[:, tok])
        out = _combine(part, x1, rec, yg, p, row(g_ple[0]), wpg, wple, row(g_final), out)
    return out
```

```python
import jax
import jax.numpy as jnp
from jax import lax
from jax.experimental import pallas as pl
from jax.experimental.pallas import tpu as pltpu
from jax.experimental.pallas import tpu_sc as plsc

F32 = jnp.float32
BF16 = jnp.bfloat16
U32 = jnp.uint32
I32 = jnp.int32

D_MODEL = 1024
BATCH = 8
SEQ = 2048
N_TOK = BATCH * SEQ
D_SSM = 512
SSM_GROUP_WIDTH = 16
SSM_GROUPS = 32
SSM_STATE = 64
D_CONV = 512
CONV_WIDTH = 31
D_IN = D_SSM + 2 * D_CONV + 2 * D_MODEL
N_GROUPS_MOE = 4
EXPERTS_PER_GROUP = 8
N_EXPERTS = 32
TOPK = 2
D_EXPERT = 512
D_PLE = 256
EPS = 1e-6

SUBLANES = 8
LANES = 128
assert BATCH == SUBLANES

TT = 64
TM = TT * BATCH
N_STEP = SEQ // TT
SB = 512
NSB = TM // SB
BPS = SB // TT
Q = 2
N_SLAB = D_SSM // LANES
GROUPS_PER_SLAB = SSM_GROUPS // N_SLAB
ROWS_Z = TM // Q
STATE_LANES = 2 * GROUPS_PER_SLAB * SSM_STATE
HALO = (CONV_WIDTH - 1) * BATCH
CHUNK_ROWS = SB // (Q * SUBLANES)
W_STAGE_ROWS = 128
CONV_ROWS = 64
N_LC = D_CONV // LANES

LANE_GRP0 = 0
LANE_EXP0 = 32
REC_EID0, REC_EID1, REC_W0, REC_W1, REC_RANK0, REC_RANK1 = 0, 1, 2, 3, 4, 5
REC_ROWS = 8

BM = 256
N_BLK = (TOPK * N_TOK + N_EXPERTS * (BM - 1) + BM - 1) // BM
N_ROWS = N_BLK * BM
HALF = D_MODEL // 2
ROW_TILE = (HALF // LANES, LANES)
SC_WINDOW = 128
IN_AHEAD = 3
IN_SLOTS = IN_AHEAD + 1
N_PARTS = 2
PART_STEPS = N_STEP // N_PARTS

VMEM_LIMIT = 56 * 1024 * 1024


def _const_spec(shape):
    n = len(shape)
    return pl.BlockSpec(shape, lambda *_: (0,) * n, pipeline_mode=pl.Buffered(1))


def _rms(x, g):
    ms = jnp.mean(x * x, axis=-1, keepdims=True)
    return x * lax.rsqrt(ms + EPS) * g


def _pack_bf16_pair(lo, hi):
    ulo = lax.bitcast_convert_type(lo.astype(BF16).astype(F32), U32)
    uhi = lax.bitcast_convert_type(hi.astype(BF16).astype(F32), U32)
    return (ulo >> 16) | (uhi & jnp.uint32(0xFFFF0000))


def _unpack_bf16_pair(w):
    lo = lax.bitcast_convert_type(w << 16, F32)
    hi = lax.bitcast_convert_type(w & jnp.uint32(0xFFFF0000), F32)
    return lo, hi


def _load_weights_bf16(pairs, stage, sem):
    chunks = [(src, dst, r0) for src, dst in pairs for r0 in range(0, src.shape[0], W_STAGE_ROWS)]

    def copy(c):
        src, _, r0 = chunks[c]
        return pltpu.make_async_copy(src.at[pl.ds(r0, W_STAGE_ROWS)],
                                     stage.at[c % 2, :, 0:src.shape[1]], sem.at[c % 2])

    copy(0).start()
    for c, (src, dst, r0) in enumerate(chunks):
        if c + 1 < len(chunks):
            copy(c + 1).start()
        copy(c).wait()
        dst[r0:r0 + W_STAGE_ROWS, :] = stage[c % 2, :, 0:src.shape[1]].astype(BF16)


def _mixer_kernel(x_ref, gmix_ref, win_hbm, bgate_ref, mp_ref, r_ref, are_ref,
                  aim_ref, d_ref, wglu_hbm, dw_ref, dwb_ref, lng_ref, lnb_ref, wco_hbm, wout_hbm,
                  gmoe_ref, wr1_ref, wr2_ref, br_ref,
                  x1_ref, h2p_ref, rec_ref, rect_ref, cnt_ref,
                  hb_scr, ht_scr, u_scr, y_scr, yi_scr, xs_scr, z_scr, conv_scr, act_scr, actb_scr,
                  logit_scr, s_scr, cnt_scr, win_ref, wglu_ref, wco_ref, wout_ref, wstage_scr, wstage_sem):
    step = pl.program_id(0)
    assert NSB == 1

    @pl.when(step == 0)
    def _init():
        logit_scr[...] = jnp.zeros(logit_scr.shape, F32)
        z_scr[:, 0:HALO, :] = jnp.zeros((N_LC, HALO, LANES), F32)
        s_scr[...] = jnp.zeros(s_scr.shape, F32)
        cnt_scr[...] = jnp.zeros(cnt_scr.shape, F32)
        _load_weights_bf16([(win_hbm, win_ref), (wglu_hbm, wglu_ref), (wco_hbm, wco_ref),
                            (wout_hbm, wout_ref)], wstage_scr, wstage_sem)

    def sub_rows(r):
        return pl.ds(pl.multiple_of(r * SB, SB), SB)

    def phase_a(r, carry):
        xb = x_ref[pl.ds(r * BPS, BPS)].reshape(SB, D_MODEL)
        hb_scr[sub_rows(r), :] = _rms(xb, gmix_ref[...]).astype(BF16)
        return carry

    def phase_a3(r, carry):
        h = ht_scr[sub_rows(r), :]
        u = jnp.dot(h, win_ref[:, 0:D_SSM], preferred_element_type=F32)
        u_scr[pl.ds(r * CHUNK_ROWS, CHUNK_ROWS)] = u.reshape(CHUNK_ROWS, Q, SUBLANES, D_SSM)
        v = jnp.dot(h, win_ref[:, D_SSM:D_SSM + 2 * D_CONV], preferred_element_type=F32)
        zc = v[:, 0:D_CONV] * jax.nn.sigmoid(v[:, D_CONV:])
        for lc in range(N_LC):
            z_scr[lc, pl.ds(pl.multiple_of(HALO + r * SB, SUBLANES), SB), :] = zc[:, lc * LANES:(lc + 1) * LANES]
        return carry

    def phase_b():
        for s in range(N_SLAB):
            lanes = slice(s * LANES, (s + 1) * LANES)
            z = jnp.concatenate(
                [u_scr[:, i, :, lanes].reshape(ROWS_Z, LANES) for i in range(Q)], axis=1).astype(BF16)
            xp = jnp.dot(z, mp_ref[s], preferred_element_type=F32)
            yi_scr[s] = xp[:, 0:Q * LANES]
            xs_scr[s] = xp[:, Q * LANES:]

        half = STATE_LANES // 2
        for s in range(N_SLAB):
            a_re = jnp.broadcast_to(are_ref[s:s + 1, :], (SUBLANES, half))
            a_im = jnp.broadcast_to(aim_ref[s:s + 1, :], (SUBLANES, half))

            def scan_body(k, carry, s=s, a_re=a_re, a_im=a_im):
                s_re, s_im = carry
                rows = pl.ds(pl.multiple_of(k * SUBLANES, SUBLANES), SUBLANES)
                x_re = xs_scr[s, rows, 0:half]
                x_im = xs_scr[s, rows, half:]
                xs_scr[s, rows, 0:half] = s_re
                xs_scr[s, rows, half:] = s_im
                n_re = a_re * s_re - a_im * s_im + x_re
                n_im = a_re * s_im + a_im * s_re + x_im
                return n_re, n_im

            s_re, s_im = lax.fori_loop(0, ROWS_Z // SUBLANES, scan_body,
                                       (s_scr[s, :, 0:half], s_scr[s, :, half:]), unroll=True)
            s_scr[s, :, 0:half] = s_re
            s_scr[s, :, half:] = s_im

        for s in range(N_SLAB):
            lanes = slice(s * LANES, (s + 1) * LANES)
            y_tot = yi_scr[s] + jnp.dot(xs_scr[s].astype(BF16), r_ref[s], preferred_element_type=F32)
            for j in range(Q):
                y_scr[:, j, :, lanes] = y_tot[:, j * LANES:(j + 1) * LANES].reshape(
                    ROWS_Z // SUBLANES, SUBLANES, LANES)

    def phase_c1(r, carry):
        rows = sub_rows(r)
        crow = pl.ds(r * CHUNK_ROWS, CHUNK_ROWS)
        y = y_scr[crow].reshape(SB, D_SSM) + d_ref[...] * u_scr[crow].reshape(SB, D_SSM)
        act_scr[rows, 0:D_SSM] = jax.nn.gelu(y).astype(BF16)
        for lc in range(N_LC):
            lanes = slice(lc * LANES, (lc + 1) * LANES)

            def conv_piece(rc, c, lc=lc, lanes=lanes):
                r0 = r * SB + rc * CONV_ROWS
                piece = jnp.broadcast_to(dwb_ref[:, lanes], (CONV_ROWS, LANES))
                for j in range(CONV_WIDTH):
                    zrows = pl.ds(pl.multiple_of(r0 + j * BATCH, SUBLANES), CONV_ROWS)
                    piece = piece + dw_ref[j:j + 1, lanes] * z_scr[lc, zrows, :]
                conv_scr[pl.ds(pl.multiple_of(rc * CONV_ROWS, CONV_ROWS), CONV_ROWS), lanes] = piece
                return c

            lax.fori_loop(0, SB // CONV_ROWS, conv_piece, 0, unroll=4)
        acc = conv_scr[...]
        mu = jnp.mean(acc, axis=-1, keepdims=True)
        cen = acc - mu
        var = jnp.mean(cen * cen, axis=-1, keepdims=True)
        ln = cen * lax.rsqrt(var + EPS) * lng_ref[...] + lnb_ref[...]
        act_scr[rows, D_SSM:] = jax.nn.silu(ln).astype(BF16)
        return carry

    lane = lax.broadcasted_iota(I32, (1, LANES), 1).astype(F32)
    grp_mask = lane < float(N_GROUPS_MOE)
    exp_lane = (lane >= float(LANE_EXP0)) & (lane < float(LANE_EXP0 + N_EXPERTS))
    lane_grp = jnp.floor((lane - float(LANE_EXP0)) * (1.0 / EXPERTS_PER_GROUP))
    tri = (lax.broadcasted_iota(I32, (SB, SB), 0) > lax.broadcasted_iota(I32, (SB, SB), 1)).astype(BF16)
    neg_inf = float("-inf")
    big = float(4 * LANES)

    def phase_c3(r, carry):
        rows = sub_rows(r)
        h = hb_scr[rows, :]
        g0 = D_SSM + 2 * D_CONV
        gate_ssm = jnp.dot(h, win_ref[:, g0:g0 + D_MODEL], preferred_element_type=F32) \
            + bgate_ref[:, 0:D_MODEL]
        gate_conv = jnp.dot(h, win_ref[:, g0 + D_MODEL:], preferred_element_type=F32) \
            + bgate_ref[:, D_MODEL:]
        zz = jnp.dot(actb_scr[rows, 0:D_SSM], wglu_ref[...], preferred_element_type=F32)
        y_ssm = zz[:, 0:D_MODEL] * jax.nn.sigmoid(zz[:, D_MODEL:])
        y_conv = jnp.dot(actb_scr[rows, D_SSM:], wco_ref[...], preferred_element_type=F32)

        merged = jax.nn.sigmoid(gate_ssm) * y_ssm + jax.nn.sigmoid(gate_conv) * y_conv
        xb = x_ref[pl.ds(r * BPS, BPS)].reshape(SB, D_MODEL)
        x1 = xb + jnp.dot(merged.astype(BF16), wout_ref[...], preferred_element_type=F32)
        x1_ref[pl.ds(r * BPS, BPS)] = x1.reshape(BPS, TT, D_MODEL)

        h2 = _rms(x1, gmoe_ref[...])
        h2p_ref[rows] = _pack_bf16_pair(h2[:, 0:HALF], h2[:, HALF:]).reshape((SB,) + ROW_TILE)

        h2_hi = h2.astype(BF16)
        h2_lo = (h2 - h2_hi.astype(F32)).astype(BF16)
        l1 = jnp.dot(h2_hi, wr1_ref[...], preferred_element_type=F32)
        l2 = jnp.dot(h2_lo, wr2_ref[...], preferred_element_type=F32)
        logit_scr[rows, :] = l1[:, 0:LANES] + l1[:, LANES:] + l2 + br_ref[...]
        return carry

    def route_previous():
        rows = sub_rows(0)
        logits = logit_scr[...]
        counted = jnp.where(step > 0, 1.0, 0.0)

        lg = jnp.where(grp_mask, logits, neg_inf)
        g_max = jnp.max(lg, axis=-1, keepdims=True)
        g_sel = jnp.min(jnp.where(lg == g_max, lane, big), axis=-1, keepdims=True)
        p_g = 1.0 / jnp.sum(jnp.where(grp_mask, jnp.exp(logits - g_max), 0.0), axis=-1, keepdims=True)
        le = jnp.where(exp_lane & (lane_grp == g_sel), logits, neg_inf)
        m1 = jnp.max(le, axis=-1, keepdims=True)
        i1 = jnp.min(jnp.where(le == m1, lane, big), axis=-1, keepdims=True)
        le2 = jnp.where(lane == i1, neg_inf, le)
        m2 = jnp.max(le2, axis=-1, keepdims=True)
        i2 = jnp.min(jnp.where(le2 == m2, lane, big), axis=-1, keepdims=True)
        e2 = jnp.exp(m2 - m1)
        den = 1.0 + e2
        w_a = (1.0 / den) * p_g
        w_b = (e2 / den) * p_g

        sel1 = lane == i1
        sel2 = lane == i2
        onehot = jnp.where(sel1 | sel2, counted, 0.0)
        prefix = jnp.dot(tri, onehot.astype(BF16), preferred_element_type=F32) + cnt_scr[...]
        rank_a = jnp.sum(jnp.where(sel1, prefix, 0.0), axis=-1, keepdims=True)
        rank_b = jnp.sum(jnp.where(sel2, prefix, 0.0), axis=-1, keepdims=True)
        cnt_scr[...] = cnt_scr[...] + jnp.sum(onehot, axis=0, keepdims=True)

        rec = jnp.where(lane == float(REC_EID0), i1 - float(LANE_EXP0), 0.0)
        rec = jnp.where(lane == float(REC_EID1), i2 - float(LANE_EXP0), rec)
        rec = jnp.where(lane == float(REC_W0), w_a, rec)
        rec = jnp.where(lane == float(REC_W1), w_b, rec)
        rec = jnp.where(lane == float(REC_RANK0), rank_a, rec)
        rec = jnp.where(lane == float(REC_RANK1), rank_b, rec)
        rec_ref[rows, :] = rec
        rect_ref[...] = jnp.transpose(rec)[0:REC_ROWS, :]
        cnt_ref[...] = cnt_scr[...]

    @pl.when(step < N_STEP)
    def _tile():
        route_previous()
        phase_a(0, 0)
        ht_scr[...] = jnp.swapaxes(hb_scr[...].reshape(BATCH, TT, D_MODEL), 0, 1).reshape(TM, D_MODEL)
        phase_a3(0, 0)
        phase_b()
        phase_c1(0, 0)
        z_scr[:, 0:HALO, :] = z_scr[:, TM:TM + HALO, :]
        actb_scr[...] = jnp.swapaxes(act_scr[...].reshape(TT, BATCH, D_SSM + D_CONV), 0, 1).reshape(
            TM, D_SSM + D_CONV)
        phase_c3(0, 0)

    @pl.when(step == N_STEP)
    def _last():
        route_previous()


def _mixer(x, gmix, win, bgate, mp, rmat, a_re, a_im, dvec, wglu, dw, dwb, lng, lnb, wco,
           wout, gmoe, wr1, wr2, br):
    tile = lambda i: jnp.minimum(i, N_STEP - 1)
    routed = lambda i: jnp.maximum(i - 1, 0)
    seq_spec = pl.BlockSpec((BATCH, TT, D_MODEL), lambda i: (0, tile(i), 0))
    in_hbm = pl.BlockSpec(memory_space=pl.ANY)
    in_specs = [
        seq_spec,
        _const_spec((1, D_MODEL)),
        in_hbm,
        _const_spec((1, 2 * D_MODEL)),
        _const_spec(mp.shape),
        _const_spec(rmat.shape),
        _const_spec(a_re.shape),
        _const_spec(a_im.shape),
        _const_spec((1, D_SSM)),
        in_hbm,
        _const_spec((CONV_WIDTH, D_CONV)),
        _const_spec((1, D_CONV)),
        _const_spec((1, D_CONV)),
        _const_spec((1, D_CONV)),
        in_hbm,
        in_hbm,
        _const_spec((1, D_MODEL)),
        _const_spec((D_MODEL, 2 * LANES)),
        _const_spec((D_MODEL, LANES)),
        _const_spec((1, LANES)),
    ]
    out_specs = [
        seq_spec,
        pl.BlockSpec((TM,) + ROW_TILE, lambda i: (tile(i), 0, 0)),
        pl.BlockSpec((TM, LANES), lambda i: (routed(i), 0)),
        pl.BlockSpec((REC_ROWS, TM), lambda i: (0, routed(i))),
        pl.BlockSpec((1, LANES), lambda i: (0, 0)),
    ]
    out_shape = [
        jax.ShapeDtypeStruct((BATCH, SEQ, D_MODEL), F32),
        jax.ShapeDtypeStruct((N_TOK,) + ROW_TILE, U32),
        jax.ShapeDtypeStruct((N_TOK, LANES), F32),
        jax.ShapeDtypeStruct((REC_ROWS, N_TOK), F32),
        jax.ShapeDtypeStruct((1, LANES), F32),
    ]
    chunk_shape = (ROWS_Z // SUBLANES, Q, SUBLANES, D_SSM)
    scratch = [
        pltpu.VMEM((TM, D_MODEL), BF16),
        pltpu.VMEM((TM, D_MODEL), BF16),
        pltpu.VMEM(chunk_shape, F32),
        pltpu.VMEM(chunk_shape, F32),
        pltpu.VMEM((N_SLAB, ROWS_Z, Q * LANES), F32),
        pltpu.VMEM((N_SLAB, ROWS_Z, STATE_LANES), F32),
        pltpu.VMEM((N_LC, HALO + TM, LANES), F32),
        pltpu.VMEM((SB, D_CONV), F32),
        pltpu.VMEM((TM, D_SSM + D_CONV), BF16),
        pltpu.VMEM((TM, D_SSM + D_CONV), BF16),
        pltpu.VMEM((TM, LANES), F32),
        pltpu.VMEM((N_SLAB, SUBLANES, STATE_LANES), F32),
        pltpu.VMEM((1, LANES), F32),
        pltpu.VMEM(win.shape, BF16),
        pltpu.VMEM(wglu.shape, BF16),
        pltpu.VMEM(wco.shape, BF16),
        pltpu.VMEM(wout.shape, BF16),
        pltpu.VMEM((2, W_STAGE_ROWS, D_IN), F32),
        pltpu.SemaphoreType.DMA((2,)),
    ]
    assert win.shape == (D_MODEL, D_IN) and wglu.shape == (D_SSM, 2 * D_MODEL)
    assert wco.shape == (D_CONV, D_MODEL) and wout.shape == (D_MODEL, D_MODEL)
    return pl.pallas_call(
        _mixer_kernel,
        grid=(N_STEP + 1,),
        in_specs=in_specs,
        out_specs=out_specs,
        out_shape=out_shape,
        scratch_shapes=scratch,
        compiler_params=pltpu.CompilerParams(
            dimension_semantics=("arbitrary",), vmem_limit_bytes=VMEM_LIMIT),
        name="mixer",
    )(x, gmix, win, bgate, mp, rmat, a_re, a_im, dvec, wglu, dw, dwb, lng, lnb, wco, wout,
      gmoe, wr1, wr2, br)


def _cmul(a, b):
    return a[0] * b[0] - a[1] * b[1], a[0] * b[1] + a[1] * b[0]


def _ssm_matrices(a_re, a_im, log_dt, b_re, b_im, c_re, c_im):
    dt = jnp.exp(log_dt)[:, None]
    mag = jnp.exp(a_re * dt)
    lam = (mag * jnp.cos(a_im * dt), mag * jnp.sin(a_im * dt))
    den = a_re * a_re + a_im * a_im
    nr = lam[0] - 1.0
    ni = lam[1]
    z_re = (nr * a_re + ni * a_im) / den
    z_im = (ni * a_re - nr * a_im) / den
    bbar = (z_re[..., None] * b_re - z_im[..., None] * b_im,
            z_re[..., None] * b_im + z_im[..., None] * b_re)
    pw = [(jnp.ones_like(lam[0]), jnp.zeros_like(lam[0])), lam]
    for _ in range(2, Q + 1):
        pw.append(_cmul(pw[-1], lam))
    e = [(c_re * p[0][:, None, :] - c_im * p[1][:, None, :],
          c_re * p[1][:, None, :] + c_im * p[0][:, None, :]) for p in pw]
    hp = lax.Precision.HIGHEST
    k = [jnp.einsum('gcn,gnd->gcd', e[m][0], bbar[0], precision=hp)
         - jnp.einsum('gcn,gnd->gcd', e[m][1], bbar[1], precision=hp) for m in range(Q)]
    eye = jnp.eye(GROUPS_PER_SLAB, dtype=F32)
    split = lambda t: t.reshape((N_SLAB, GROUPS_PER_SLAB) + t.shape[1:])
    zero_k = jnp.zeros_like(k[0])
    kb = jnp.stack([jnp.stack([split(jnp.swapaxes(k[j - i] if j >= i else zero_k, 1, 2))
                               for j in range(Q)]) for i in range(Q)])
    m_mat = jnp.einsum('ijsgdc,gh->sigdjhc', kb, eye).reshape(N_SLAB, Q * LANES, Q * LANES)
    f = [_cmul((pw[Q - 1 - i][0][..., None], pw[Q - 1 - i][1][..., None]), bbar) for i in range(Q)]
    p_parts = []
    for part in range(2):
        fs = jnp.stack([split(f[i][part]) for i in range(Q)])
        p_parts.append(jnp.einsum('isgnd,gh->sigdhn', fs, eye).reshape(N_SLAB, Q * LANES, STATE_LANES // 2))
    p_mat = jnp.concatenate(p_parts, axis=-1)
    r_parts = []
    for part, sign in ((0, 1.0), (1, -1.0)):
        es = jnp.stack([split(e[j + 1][part]) for j in range(Q)])
        r_parts.append(sign * jnp.einsum('jsgcn,gh->shnjgc', es, eye).reshape(
            N_SLAB, STATE_LANES // 2, Q * LANES))
    r_mat = jnp.concatenate(r_parts, axis=1)
    mp = jnp.concatenate([m_mat, p_mat], axis=-1).astype(BF16)
    a_q = pw[Q]
    return (mp, r_mat.astype(BF16),
            a_q[0].reshape(N_SLAB, STATE_LANES // 2), a_q[1].reshape(N_SLAB, STATE_LANES // 2))


def _router_weights(w_rg, b_rg, w_re, b_re):
    pad_g = LANE_EXP0 - LANE_GRP0 - N_GROUPS_MOE
    pad_e = LANES - LANE_EXP0 - N_EXPERTS
    w = jnp.concatenate([w_rg, jnp.zeros((D_MODEL, pad_g), F32), w_re, jnp.zeros((D_MODEL, pad_e), F32)], axis=1)
    b = jnp.concatenate([b_rg, jnp.zeros((pad_g,), F32), b_re, jnp.zeros((pad_e,), F32)]).reshape(1, LANES)
    w_hi = w.astype(BF16)
    w_lo = (w - w_hi.astype(F32)).astype(BF16)
    return jnp.concatenate([w_hi, w_lo], axis=1), w_hi, b


def _sc_mesh():
    return plsc.VectorSubcoreMesh(core_axis_name="core", subcore_axis_name="subcore")


def _sc_worker(mesh):
    return lax.axis_index("core") * mesh.num_subcores + lax.axis_index("subcore")


def _dispatch(h2p, dest):
    mesh = _sc_mesh()
    n_win = N_TOK // SC_WINDOW
    per_worker = n_win // (mesh.num_cores * mesh.num_subcores)
    assert per_worker * mesh.num_cores * mesh.num_subcores == n_win

    @pl.kernel(out_type=jax.ShapeDtypeStruct((N_ROWS,) + ROW_TILE, U32), mesh=mesh,
               scratch_types=[pltpu.VMEM((SC_WINDOW,), I32), pltpu.VMEM((SC_WINDOW,) + ROW_TILE, U32)])
    def scatter_rows(h_hbm, dest_hbm, xs_hbm, idx_v, rows_v):
        first = _sc_worker(mesh) * per_worker

        @pl.loop(0, per_worker)
        def _(w):
            win = first + w
            pltpu.sync_copy(h_hbm.at[pl.ds(win * SC_WINDOW, SC_WINDOW)], rows_v)
            for j in range(TOPK):
                pltpu.sync_copy(dest_hbm.at[j, win], idx_v)
                pltpu.sync_copy(rows_v, xs_hbm.at[idx_v])

    return scatter_rows(h2p, dest)


def _collect(ys, dest):
    mesh = _sc_mesh()
    n_tok = dest.shape[1]
    n_win = TOPK * n_tok // SC_WINDOW
    per_worker = n_win // (mesh.num_cores * mesh.num_subcores)
    assert per_worker * mesh.num_cores * mesh.num_subcores == n_win

    @pl.kernel(out_type=jax.ShapeDtypeStruct((TOPK * n_tok,) + ROW_TILE, U32), mesh=mesh,
               scratch_types=[pltpu.VMEM((SC_WINDOW,), I32), pltpu.VMEM((SC_WINDOW,) + ROW_TILE, U32)])
    def gather_rows(ys_hbm, dest_hbm, yg_hbm, idx_v, rows_v):
        first = _sc_worker(mesh) * per_worker

        @pl.loop(0, per_worker)
        def _(w):
            win = first + w
            pltpu.sync_copy(dest_hbm.at[win], idx_v)
            pltpu.sync_copy(ys_hbm.at[idx_v], rows_v)
            pltpu.sync_copy(rows_v, yg_hbm.at[pl.ds(win * SC_WINDOW, SC_WINDOW)])

    return gather_rows(ys, dest.reshape(n_win, SC_WINDOW)).reshape((TOPK, n_tok) + ROW_TILE)


def _expert_kernel(first_ref, nblk_ref, nvalid_ref, nused_ref, xs_hbm, wg_ref, wu_ref, wd_ref, ys_hbm,
                   wg_scr, wu_scr, wd_scr, x_buf, y_buf, in_sem, out_sem):
    e = pl.program_id(0)
    nused = nused_ref[0]

    def in_copy(g):
        slot = lax.rem(g, IN_SLOTS)
        return pltpu.make_async_copy(xs_hbm.at[pl.ds(g * BM, BM)], x_buf.at[slot], in_sem.at[slot])

    def out_copy(g, slot):
        return pltpu.make_async_copy(y_buf.at[slot], ys_hbm.at[pl.ds(g * BM, BM)], out_sem.at[slot])

    @pl.when(e == 0)
    def _first():
        for g in range(IN_AHEAD):
            in_copy(g).start()

    wg_scr[...] = wg_ref[0].astype(BF16)
    wu_scr[...] = wu_ref[0].astype(BF16)
    wd_scr[...] = wd_ref[0].astype(BF16)

    def block(b, carry):
        g = first_ref[e] + b
        slot = lax.rem(g, 2)
        in_copy(g).wait()

        @pl.when(g + IN_AHEAD < nused)
        def _prefetch():
            in_copy(g + IN_AHEAD).start()

        @pl.when(g >= 2)
        def _slot_free():
            out_copy(g - 2, slot).wait()

        valid = lax.broadcasted_iota(I32, (BM, 1), 0) < nvalid_ref[g]
        x_blk = x_buf[lax.rem(g, IN_SLOTS)].reshape(BM, HALF)
        lo, hi = _unpack_bf16_pair(jnp.where(valid, x_blk, jnp.uint32(0)))
        lo = lo.astype(BF16)
        hi = hi.astype(BF16)
        gate = jnp.dot(lo, wg_scr[0:HALF, :], preferred_element_type=F32) \
            + jnp.dot(hi, wg_scr[HALF:, :], preferred_element_type=F32)
        up = jnp.dot(lo, wu_scr[0:HALF, :], preferred_element_type=F32) \
            + jnp.dot(hi, wu_scr[HALF:, :], preferred_element_type=F32)
        act = (jax.nn.silu(gate) * up).astype(BF16)
        o = jnp.dot(act, wd_scr[...], preferred_element_type=F32)
        y_buf[slot] = _pack_bf16_pair(o[:, 0:HALF], o[:, HALF:]).reshape((BM,) + ROW_TILE)
        out_copy(g, slot).start()
        return carry

    lax.fori_loop(0, nblk_ref[e], block, 0)

    @pl.when(e == N_EXPERTS - 1)
    def _drain():
        out_copy(nused - 2, lax.rem(nused, 2)).wait()
        out_copy(nused - 1, 1 - lax.rem(nused, 2)).wait()


def _experts(first, nblk, nvalid, nused, xs, wg, wu, wd):
    grid_spec = pltpu.PrefetchScalarGridSpec(
        num_scalar_prefetch=4,
        grid=(N_EXPERTS,),
        in_specs=[
            pl.BlockSpec(memory_space=pl.ANY),
            pl.BlockSpec((1, D_MODEL, D_EXPERT), lambda e, *_: (e, 0, 0)),
            pl.BlockSpec((1, D_MODEL, D_EXPERT), lambda e, *_: (e, 0, 0)),
            pl.BlockSpec((1, D_EXPERT, D_MODEL), lambda e, *_: (e, 0, 0)),
        ],
        out_specs=pl.BlockSpec(memory_space=pl.ANY),
        scratch_shapes=[
            pltpu.VMEM((D_MODEL, D_EXPERT), BF16),
            pltpu.VMEM((D_MODEL, D_EXPERT), BF16),
            pltpu.VMEM((D_EXPERT, D_MODEL), BF16),
            pltpu.VMEM((IN_SLOTS, BM) + ROW_TILE, U32),
            pltpu.VMEM((2, BM) + ROW_TILE, U32),
            pltpu.SemaphoreType.DMA((IN_SLOTS,)),
            pltpu.SemaphoreType.DMA((2,)),
        ],
    )
    return pl.pallas_call(
        _expert_kernel,
        grid_spec=grid_spec,
        out_shape=jax.ShapeDtypeStruct((N_ROWS,) + ROW_TILE, U32),
        compiler_params=pltpu.CompilerParams(
            dimension_semantics=("arbitrary",), vmem_limit_bytes=VMEM_LIMIT),
        name="experts",
    )(first, nblk, nvalid, nused, xs, wg, wu, wd)


def _combine_kernel(x1_ref, rec_ref, yg_ref, p_ref, gple_ref, wpg_ref, wple_ref, gfin_ref, *rest):
    out_ref = rest[-1]
    ple = jnp.dot(p_ref[0].reshape(TM, D_PLE).astype(BF16), wple_ref[...], preferred_element_type=F32)
    rec = rec_ref[...]
    w0 = rec[:, REC_W0:REC_W0 + 1]
    w1 = rec[:, REC_W1:REC_W1 + 1]
    lo0, hi0 = _unpack_bf16_pair(yg_ref[0].reshape(TM, HALF))
    lo1, hi1 = _unpack_bf16_pair(yg_ref[1].reshape(TM, HALF))
    moe = jnp.concatenate([lo0 * w0 + lo1 * w1, hi0 * w0 + hi1 * w1], axis=1)
    x2 = x1_ref[...].reshape(TM, D_MODEL) + moe
    gate = jax.nn.sigmoid(jnp.dot(_rms(x2, gple_ref[...]).astype(BF16), wpg_ref[...],
                                  preferred_element_type=F32))
    x3 = x2 + gate * ple
    out_ref[...] = _rms(x3, gfin_ref[...]).reshape(BATCH, TT, D_MODEL)


def _combine(part, x1, rec, yg, p, gple, wpg, wple, gfin, out_prev=None):
    s0 = part * PART_STEPS
    seq_spec = pl.BlockSpec((BATCH, TT, D_MODEL), lambda i: (0, s0 + i, 0))
    in_specs = [
        seq_spec,
        pl.BlockSpec((TM, LANES), lambda i: (s0 + i, 0)),
        pl.BlockSpec((TOPK, TM) + ROW_TILE, lambda i: (0, i, 0, 0)),
        pl.BlockSpec((1, BATCH, TT, D_PLE), lambda i: (0, 0, s0 + i, 0)),
        _const_spec((1, D_MODEL)),
        _const_spec((D_MODEL, D_MODEL)),
        _const_spec((D_PLE, D_MODEL)),
        _const_spec((1, D_MODEL)),
    ]
    args = [x1, rec, yg, p, gple, wpg, wple, gfin]
    aliases = {}
    if out_prev is not None:
        in_specs.append(pl.BlockSpec(memory_space=pl.ANY))
        args.append(out_prev)
        aliases = {len(args) - 1: 0}
    return pl.pallas_call(
        _combine_kernel,
        grid=(PART_STEPS,),
        in_specs=in_specs,
        out_specs=seq_spec,
        out_shape=jax.ShapeDtypeStruct((BATCH, SEQ, D_MODEL), F32),
        input_output_aliases=aliases,
        compiler_params=pltpu.CompilerParams(
            dimension_semantics=("arbitrary",), vmem_limit_bytes=VMEM_LIMIT),
        name="combine",
    )(*args)


def kernel(x, p, g_mix, w_in, b_gate, ssm_a_re, ssm_a_im, ssm_log_dt, ssm_b_re, ssm_b_im, ssm_c_re,
           ssm_c_im, ssm_d, w_glu, conv_dw, conv_dw_b, conv_ln_g, conv_ln_b, w_conv_out, w_out, g_moe,
           w_router_group, b_router_group, w_router_expert, b_router_expert, w_exp_gate, w_exp_up,
           w_exp_down, g_ple, w_ple_gate, w_ple, g_final):
    assert x.shape == (BATCH, SEQ, D_MODEL) and p.shape == (1, BATCH, SEQ, D_PLE)
    row = lambda v: v.reshape(1, -1)

    mp, rmat, a_re, a_im = _ssm_matrices(ssm_a_re[0], ssm_a_im[0], ssm_log_dt[0], ssm_b_re[0],
                                         ssm_b_im[0], ssm_c_re[0], ssm_c_im[0])
    wr1, wr2, br = _router_weights(w_router_group[0], b_router_group[0], w_router_expert[0],
                                   b_router_expert[0])
    x1, h2p, rec, rect, cnt = _mixer(
        x, row(g_mix[0]), w_in[0], row(b_gate[0]), mp, rmat, a_re, a_im,
        row(ssm_d[0]), w_glu[0], conv_dw[0], row(conv_dw_b[0]), row(conv_ln_g[0]),
        row(conv_ln_b[0]), w_conv_out[0], w_out[0], row(g_moe[0]), wr1, wr2, br)

    counts = cnt[0, LANE_EXP0:LANE_EXP0 + N_EXPERTS].astype(I32)
    pcounts = (counts + BM - 1) // BM * BM
    pends = jnp.cumsum(pcounts)
    pstarts = pends - pcounts
    eid = rect[REC_EID0:REC_EID1 + 1].astype(I32)
    rank = rect[REC_RANK0:REC_RANK1 + 1].astype(I32)
    dest = (jnp.sum(jnp.where(eid[..., None] == jnp.arange(N_EXPERTS, dtype=I32), pstarts, 0), axis=-1)
            + rank).reshape(TOPK, N_TOK // SC_WINDOW, SC_WINDOW)
    nused = (pends[-1] // BM).astype(I32)
    blk = jnp.arange(N_BLK, dtype=I32)[:, None] * BM
    in_expert = (pstarts[None, :] <= blk) & (blk < pends[None, :])
    nvalid = jnp.clip(jnp.sum(jnp.where(in_expert, (pstarts + counts)[None, :] - blk, 0), axis=1), 0, BM)

    xs = _dispatch(h2p, dest)
    ys = _experts(pstarts // BM, pcounts // BM, nvalid.astype(I32), nused.reshape(1), xs,
                  w_exp_gate[0], w_exp_up[0], w_exp_down[0])
    dest_tok = dest.reshape(TOPK, N_TOK)
    wpg = w_ple_gate[0].astype(BF16)
    wple = w_ple[0].astype(BF16)
    out = None
    for part in range(N_PARTS):
        tok = slice(part * PART_STEPS * TM, (part + 1) * PART_STEPS * TM)
        yg = _collect(ys, dest_tok[:, tok])
        out = _combine(part, x1, rec, yg, p, row(g_ple[0]), wpg, wple, row(g_final), out)
    return out
```

```python
import jax
import jax.numpy as jnp
from jax import lax
from jax.experimental import pallas as pl
from jax.experimental.pallas import tpu as pltpu
from jax.experimental.pallas import tpu_sc as plsc

F32 = jnp.float32
BF16 = jnp.bfloat16
U32 = jnp.uint32
I32 = jnp.int32

D_MODEL = 1024
BATCH = 8
SEQ = 2048
N_TOK = BATCH * SEQ
D_SSM = 512
SSM_GROUP_WIDTH = 16
SSM_GROUPS = 32
SSM_STATE = 64
D_CONV = 512
CONV_WIDTH = 31
D_IN = D_SSM + 2 * D_CONV + 2 * D_MODEL
N_GROUPS_MOE = 4
EXPERTS_PER_GROUP = 8
N_EXPERTS = 32
TOPK = 2
D_EXPERT = 512
D_PLE = 256
EPS = 1e-6

SUBLANES = 8
LANES = 128
assert BATCH == SUBLANES

TT = 64
TM = TT * BATCH
N_STEP = SEQ // TT
SB = 512
NSB = TM // SB
BPS = SB // TT
Q = 2
N_SLAB = D_SSM // LANES
GROUPS_PER_SLAB = SSM_GROUPS // N_SLAB
ROWS_Z = TM // Q
STATE_LANES = 2 * GROUPS_PER_SLAB * SSM_STATE
HALO = (CONV_WIDTH - 1) * BATCH
CHUNK_ROWS = SB // (Q * SUBLANES)
W_STAGE_ROWS = 64
W_STAGE_SLOTS = 4
CONV_ROWS = 64
N_LC = D_CONV // LANES

LANE_GRP0 = 0
LANE_EXP0 = 32
REC_EID0, REC_EID1, REC_W0, REC_W1, REC_RANK0, REC_RANK1 = 0, 1, 2, 3, 4, 5
REC_ROWS = 8

BM = 256
N_BLK = (TOPK * N_TOK + N_EXPERTS * (BM - 1) + BM - 1) // BM
N_ROWS = N_BLK * BM
HALF = D_MODEL // 2
ROW_TILE = (HALF // LANES, LANES)
SC_WINDOW = 128
IN_AHEAD = 3
IN_SLOTS = IN_AHEAD + 1
N_PARTS = 2
PART_STEPS = N_STEP // N_PARTS

VMEM_LIMIT = 56 * 1024 * 1024


def _const_spec(shape):
    n = len(shape)
    return pl.BlockSpec(shape, lambda *_: (0,) * n, pipeline_mode=pl.Buffered(1))


def _rms(x, g):
    ms = jnp.mean(x * x, axis=-1, keepdims=True)
    return x * lax.rsqrt(ms + EPS) * g


def _pack_bf16_pair(lo, hi):
    ulo = lax.bitcast_convert_type(lo.astype(BF16).astype(F32), U32)
    uhi = lax.bitcast_convert_type(hi.astype(BF16).astype(F32), U32)
    return (ulo >> 16) | (uhi & jnp.uint32(0xFFFF0000))


def _unpack_bf16_pair(w):
    lo = lax.bitcast_convert_type(w << 16, F32)
    hi = lax.bitcast_convert_type(w & jnp.uint32(0xFFFF0000), F32)
    return lo, hi


def _load_weights_bf16(pairs, stage, sem):
    chunks = [(src, dst, r0) for src, dst in pairs for r0 in range(0, src.shape[0], W_STAGE_ROWS)]

    def copy(c):
        src, _, r0 = chunks[c]
        slot = c % W_STAGE_SLOTS
        return pltpu.make_async_copy(src.at[pl.ds(r0, W_STAGE_ROWS)],
                                     stage.at[slot, :, 0:src.shape[1]], sem.at[slot])

    for c in range(W_STAGE_SLOTS):
        copy(c).start()
    for c, (src, dst, r0) in enumerate(chunks):
        copy(c).wait()
        dst[r0:r0 + W_STAGE_ROWS, :] = stage[c % W_STAGE_SLOTS, :, 0:src.shape[1]].astype(BF16)
        if c + W_STAGE_SLOTS < len(chunks):
            copy(c + W_STAGE_SLOTS).start()


def _mixer_kernel(x_ref, gmix_ref, win_hbm, bgate_ref, mp_ref, r_ref, are_ref,
                  aim_ref, d_ref, wglu_hbm, dw_ref, dwb_ref, lng_ref, lnb_ref, wco_hbm, wout_hbm,
                  gmoe_ref, wr1_ref, wr2_ref, br_ref,
                  x1_ref, h2p_ref, rec_ref, rect_ref, cnt_ref,
                  hb_scr, ht_scr, u_scr, y_scr, yi_scr, xs_scr, z_scr, conv_scr, act_scr, actb_scr,
                  logit_scr, s_scr, cnt_scr, win_ref, wglu_ref, wco_ref, wout_ref, wstage_scr, wstage_sem):
    step = pl.program_id(0)
    assert NSB == 1

    @pl.when(step == 0)
    def _init():
        logit_scr[...] = jnp.zeros(logit_scr.shape, F32)
        z_scr[:, 0:HALO, :] = jnp.zeros((N_LC, HALO, LANES), F32)
        s_scr[...] = jnp.zeros(s_scr.shape, F32)
        cnt_scr[...] = jnp.zeros(cnt_scr.shape, F32)
        _load_weights_bf16([(win_hbm, win_ref), (wglu_hbm, wglu_ref), (wco_hbm, wco_ref),
                            (wout_hbm, wout_ref)], wstage_scr, wstage_sem)

    def sub_rows(r):
        return pl.ds(pl.multiple_of(r * SB, SB), SB)

    def phase_a(r, carry):
        xb = x_ref[pl.ds(r * BPS, BPS)].reshape(SB, D_MODEL)
        hb_scr[sub_rows(r), :] = _rms(xb, gmix_ref[...]).astype(BF16)
        return carry

    def phase_a3(r, carry):
        h = ht_scr[sub_rows(r), :]
        u = jnp.dot(h, win_ref[:, 0:D_SSM], preferred_element_type=F32)
        u_scr[pl.ds(r * CHUNK_ROWS, CHUNK_ROWS)] = u.reshape(CHUNK_ROWS, Q, SUBLANES, D_SSM)
        v = jnp.dot(h, win_ref[:, D_SSM:D_SSM + 2 * D_CONV], preferred_element_type=F32)
        zc = v[:, 0:D_CONV] * jax.nn.sigmoid(v[:, D_CONV:])
        for lc in range(N_LC):
            z_scr[lc, pl.ds(pl.multiple_of(HALO + r * SB, SUBLANES), SB), :] = zc[:, lc * LANES:(lc + 1) * LANES]
        return carry

    def phase_b():
        for s in range(N_SLAB):
            lanes = slice(s * LANES, (s + 1) * LANES)
            z = jnp.concatenate(
                [u_scr[:, i, :, lanes].reshape(ROWS_Z, LANES) for i in range(Q)], axis=1).astype(BF16)
            xp = jnp.dot(z, mp_ref[s], preferred_element_type=F32)
            yi_scr[s] = xp[:, 0:Q * LANES]
            xs_scr[s] = xp[:, Q * LANES:]

        half = STATE_LANES // 2
        for s in range(N_SLAB):
            a_re = jnp.broadcast_to(are_ref[s:s + 1, :], (SUBLANES, half))
            a_im = jnp.broadcast_to(aim_ref[s:s + 1, :], (SUBLANES, half))

            def scan_body(k, carry, s=s, a_re=a_re, a_im=a_im):
                s_re, s_im = carry
                rows = pl.ds(pl.multiple_of(k * SUBLANES, SUBLANES), SUBLANES)
                x_re = xs_scr[s, rows, 0:half]
                x_im = xs_scr[s, rows, half:]
                xs_scr[s, rows, 0:half] = s_re
                xs_scr[s, rows, half:] = s_im
                n_re = a_re * s_re - a_im * s_im + x_re
                n_im = a_re * s_im + a_im * s_re + x_im
                return n_re, n_im

            s_re, s_im = lax.fori_loop(0, ROWS_Z // SUBLANES, scan_body,
                                       (s_scr[s, :, 0:half], s_scr[s, :, half:]), unroll=True)
            s_scr[s, :, 0:half] = s_re
            s_scr[s, :, half:] = s_im

        for s in range(N_SLAB):
            lanes = slice(s * LANES, (s + 1) * LANES)
            y_tot = yi_scr[s] + jnp.dot(xs_scr[s].astype(BF16), r_ref[s], preferred_element_type=F32)
            for j in range(Q):
                y_scr[:, j, :, lanes] = y_tot[:, j * LANES:(j + 1) * LANES].reshape(
                    ROWS_Z // SUBLANES, SUBLANES, LANES)

    def phase_c1(r, carry):
        rows = sub_rows(r)
        crow = pl.ds(r * CHUNK_ROWS, CHUNK_ROWS)
        y = y_scr[crow].reshape(SB, D_SSM) + d_ref[...] * u_scr[crow].reshape(SB, D_SSM)
        act_scr[rows, 0:D_SSM] = jax.nn.gelu(y).astype(BF16)
        for lc in range(N_LC):
            lanes = slice(lc * LANES, (lc + 1) * LANES)

            def conv_piece(rc, c, lc=lc, lanes=lanes):
                r0 = r * SB + rc * CONV_ROWS
                piece = jnp.broadcast_to(dwb_ref[:, lanes], (CONV_ROWS, LANES))
                for j in range(CONV_WIDTH):
                    zrows = pl.ds(pl.multiple_of(r0 + j * BATCH, SUBLANES), CONV_ROWS)
                    piece = piece + dw_ref[j:j + 1, lanes] * z_scr[lc, zrows, :]
                conv_scr[pl.ds(pl.multiple_of(rc * CONV_ROWS, CONV_ROWS), CONV_ROWS), lanes] = piece
                return c

            lax.fori_loop(0, SB // CONV_ROWS, conv_piece, 0, unroll=4)
        acc = conv_scr[...]
        mu = jnp.mean(acc, axis=-1, keepdims=True)
        cen = acc - mu
        var = jnp.mean(cen * cen, axis=-1, keepdims=True)
        ln = cen * lax.rsqrt(var + EPS) * lng_ref[...] + lnb_ref[...]
        act_scr[rows, D_SSM:] = jax.nn.silu(ln).astype(BF16)
        return carry

    lane = lax.broadcasted_iota(I32, (1, LANES), 1).astype(F32)
    grp_mask = lane < float(N_GROUPS_MOE)
    exp_lane = (lane >= float(LANE_EXP0)) & (lane < float(LANE_EXP0 + N_EXPERTS))
    lane_grp = jnp.floor((lane - float(LANE_EXP0)) * (1.0 / EXPERTS_PER_GROUP))
    tri = (lax.broadcasted_iota(I32, (SB, SB), 0) > lax.broadcasted_iota(I32, (SB, SB), 1)).astype(BF16)
    neg_inf = float("-inf")
    big = float(4 * LANES)

    def phase_c3(r, carry):
        rows = sub_rows(r)
        h = hb_scr[rows, :]
        g0 = D_SSM + 2 * D_CONV
        gate_ssm = jnp.dot(h, win_ref[:, g0:g0 + D_MODEL], preferred_element_type=F32) \
            + bgate_ref[:, 0:D_MODEL]
        gate_conv = jnp.dot(h, win_ref[:, g0 + D_MODEL:], preferred_element_type=F32) \
            + bgate_ref[:, D_MODEL:]
        zz = jnp.dot(actb_scr[rows, 0:D_SSM], wglu_ref[...], preferred_element_type=F32)
        y_ssm = zz[:, 0:D_MODEL] * jax.nn.sigmoid(zz[:, D_MODEL:])
        y_conv = jnp.dot(actb_scr[rows, D_SSM:], wco_ref[...], preferred_element_type=F32)

        merged = jax.nn.sigmoid(gate_ssm) * y_ssm + jax.nn.sigmoid(gate_conv) * y_conv
        xb = x_ref[pl.ds(r * BPS, BPS)].reshape(SB, D_MODEL)
        x1 = xb + jnp.dot(merged.astype(BF16), wout_ref[...], preferred_element_type=F32)
        x1_ref[pl.ds(r * BPS, BPS)] = x1.reshape(BPS, TT, D_MODEL)

        h2 = _rms(x1, gmoe_ref[...])
        h2p_ref[rows] = _pack_bf16_pair(h2[:, 0:HALF], h2[:, HALF:]).reshape((SB,) + ROW_TILE)

        h2_hi = h2.astype(BF16)
        h2_lo = (h2 - h2_hi.astype(F32)).astype(BF16)
        l1 = jnp.dot(h2_hi, wr1_ref[...], preferred_element_type=F32)
        l2 = jnp.dot(h2_lo, wr2_ref[...], preferred_element_type=F32)
        logit_scr[rows, :] = l1[:, 0:LANES] + l1[:, LANES:] + l2 + br_ref[...]
        return carry

    def route_previous():
        rows = sub_rows(0)
        logits = logit_scr[...]
        counted = jnp.where(step > 0, 1.0, 0.0)

        lg = jnp.where(grp_mask, logits, neg_inf)
        g_max = jnp.max(lg, axis=-1, keepdims=True)
        g_sel = jnp.min(jnp.where(lg == g_max, lane, big), axis=-1, keepdims=True)
        p_g = 1.0 / jnp.sum(jnp.where(grp_mask, jnp.exp(logits - g_max), 0.0), axis=-1, keepdims=True)
        le = jnp.where(exp_lane & (lane_grp == g_sel), logits, neg_inf)
        m1 = jnp.max(le, axis=-1, keepdims=True)
        i1 = jnp.min(jnp.where(le == m1, lane, big), axis=-1, keepdims=True)
        le2 = jnp.where(lane == i1, neg_inf, le)
        m2 = jnp.max(le2, axis=-1, keepdims=True)
        i2 = jnp.min(jnp.where(le2 == m2, lane, big), axis=-1, keepdims=True)
        e2 = jnp.exp(m2 - m1)
        den = 1.0 + e2
        w_a = (1.0 / den) * p_g
        w_b = (e2 / den) * p_g

        sel1 = lane == i1
        sel2 = lane == i2
        onehot = jnp.where(sel1 | sel2, counted, 0.0)
        prefix = jnp.dot(tri, onehot.astype(BF16), preferred_element_type=F32) + cnt_scr[...]
        rank_a = jnp.sum(jnp.where(sel1, prefix, 0.0), axis=-1, keepdims=True)
        rank_b = jnp.sum(jnp.where(sel2, prefix, 0.0), axis=-1, keepdims=True)
        cnt_scr[...] = cnt_scr[...] + jnp.sum(onehot, axis=0, keepdims=True)

        rec = jnp.where(lane == float(REC_EID0), i1 - float(LANE_EXP0), 0.0)
        rec = jnp.where(lane == float(REC_EID1), i2 - float(LANE_EXP0), rec)
        rec = jnp.where(lane == float(REC_W0), w_a, rec)
        rec = jnp.where(lane == float(REC_W1), w_b, rec)
        rec = jnp.where(lane == float(REC_RANK0), rank_a, rec)
        rec = jnp.where(lane == float(REC_RANK1), rank_b, rec)
        rec_ref[rows, :] = rec
        rect_ref[...] = jnp.transpose(rec)[0:REC_ROWS, :]
        cnt_ref[...] = cnt_scr[...]

    @pl.when(step < N_STEP)
    def _tile():
        route_previous()
        phase_a(0, 0)
        ht_scr[...] = jnp.swapaxes(hb_scr[...].reshape(BATCH, TT, D_MODEL), 0, 1).reshape(TM, D_MODEL)
        phase_a3(0, 0)
        phase_b()
        phase_c1(0, 0)
        z_scr[:, 0:HALO, :] = z_scr[:, TM:TM + HALO, :]
        actb_scr[...] = jnp.swapaxes(act_scr[...].reshape(TT, BATCH, D_SSM + D_CONV), 0, 1).reshape(
            TM, D_SSM + D_CONV)
        phase_c3(0, 0)

    @pl.when(step == N_STEP)
    def _last():
        route_previous()


def _mixer(x, gmix, win, bgate, mp, rmat, a_re, a_im, dvec, wglu, dw, dwb, lng, lnb, wco,
           wout, gmoe, wr1, wr2, br):
    tile = lambda i: jnp.minimum(i, N_STEP - 1)
    routed = lambda i: jnp.maximum(i - 1, 0)
    seq_spec = pl.BlockSpec((BATCH, TT, D_MODEL), lambda i: (0, tile(i), 0))
    in_hbm = pl.BlockSpec(memory_space=pl.ANY)
    in_specs = [
        seq_spec,
        _const_spec((1, D_MODEL)),
        in_hbm,
        _const_spec((1, 2 * D_MODEL)),
        _const_spec(mp.shape),
        _const_spec(rmat.shape),
        _const_spec(a_re.shape),
        _const_spec(a_im.shape),
        _const_spec((1, D_SSM)),
        in_hbm,
        _const_spec((CONV_WIDTH, D_CONV)),
        _const_spec((1, D_CONV)),
        _const_spec((1, D_CONV)),
        _const_spec((1, D_CONV)),
        in_hbm,
        in_hbm,
        _const_spec((1, D_MODEL)),
        _const_spec((D_MODEL, 2 * LANES)),
        _const_spec((D_MODEL, LANES)),
        _const_spec((1, LANES)),
    ]
    out_specs = [
        seq_spec,
        pl.BlockSpec((TM,) + ROW_TILE, lambda i: (tile(i), 0, 0)),
        pl.BlockSpec((TM, LANES), lambda i: (routed(i), 0)),
        pl.BlockSpec((REC_ROWS, TM), lambda i: (0, routed(i))),
        pl.BlockSpec((1, LANES), lambda i: (0, 0)),
    ]
    out_shape = [
        jax.ShapeDtypeStruct((BATCH, SEQ, D_MODEL), F32),
        jax.ShapeDtypeStruct((N_TOK,) + ROW_TILE, U32),
        jax.ShapeDtypeStruct((N_TOK, LANES), F32),
        jax.ShapeDtypeStruct((REC_ROWS, N_TOK), F32),
        jax.ShapeDtypeStruct((1, LANES), F32),
    ]
    chunk_shape = (ROWS_Z // SUBLANES, Q, SUBLANES, D_SSM)
    scratch = [
        pltpu.VMEM((TM, D_MODEL), BF16),
        pltpu.VMEM((TM, D_MODEL), BF16),
        pltpu.VMEM(chunk_shape, F32),
        pltpu.VMEM(chunk_shape, F32),
        pltpu.VMEM((N_SLAB, ROWS_Z, Q * LANES), F32),
        pltpu.VMEM((N_SLAB, ROWS_Z, STATE_LANES), F32),
        pltpu.VMEM((N_LC, HALO + TM, LANES), F32),
        pltpu.VMEM((SB, D_CONV), F32),
        pltpu.VMEM((TM, D_SSM + D_CONV), BF16),
        pltpu.VMEM((TM, D_SSM + D_CONV), BF16),
        pltpu.VMEM((TM, LANES), F32),
        pltpu.VMEM((N_SLAB, SUBLANES, STATE_LANES), F32),
        pltpu.VMEM((1, LANES), F32),
        pltpu.VMEM(win.shape, BF16),
        pltpu.VMEM(wglu.shape, BF16),
        pltpu.VMEM(wco.shape, BF16),
        pltpu.VMEM(wout.shape, BF16),
        pltpu.VMEM((W_STAGE_SLOTS, W_STAGE_ROWS, D_IN), F32),
        pltpu.SemaphoreType.DMA((W_STAGE_SLOTS,)),
    ]
    assert win.shape == (D_MODEL, D_IN) and wglu.shape == (D_SSM, 2 * D_MODEL)
    assert wco.shape == (D_CONV, D_MODEL) and wout.shape == (D_MODEL, D_MODEL)
    return pl.pallas_call(
        _mixer_kernel,
        grid=(N_STEP + 1,),
        in_specs=in_specs,
        out_specs=out_specs,
        out_shape=out_shape,
        scratch_shapes=scratch,
        compiler_params=pltpu.CompilerParams(
            dimension_semantics=("arbitrary",), vmem_limit_bytes=VMEM_LIMIT),
        name="mixer",
    )(x, gmix, win, bgate, mp, rmat, a_re, a_im, dvec, wglu, dw, dwb, lng, lnb, wco, wout,
      gmoe, wr1, wr2, br)


def _cmul(a, b):
    return a[0] * b[0] - a[1] * b[1], a[0] * b[1] + a[1] * b[0]


def _ssm_matrices(a_re, a_im, log_dt, b_re, b_im, c_re, c_im):
    dt = jnp.exp(log_dt)[:, None]
    mag = jnp.exp(a_re * dt)
    lam = (mag * jnp.cos(a_im * dt), mag * jnp.sin(a_im * dt))
    den = a_re * a_re + a_im * a_im
    nr = lam[0] - 1.0
    ni = lam[1]
    z_re = (nr * a_re + ni * a_im) / den
    z_im = (ni * a_re - nr * a_im) / den
    bbar = (z_re[..., None] * b_re - z_im[..., None] * b_im,
            z_re[..., None] * b_im + z_im[..., None] * b_re)
    pw = [(jnp.ones_like(lam[0]), jnp.zeros_like(lam[0])), lam]
    for _ in range(2, Q + 1):
        pw.append(_cmul(pw[-1], lam))
    e = [(c_re * p[0][:, None, :] - c_im * p[1][:, None, :],
          c_re * p[1][:, None, :] + c_im * p[0][:, None, :]) for p in pw]
    hp = lax.Precision.HIGHEST
    k = [jnp.einsum('gcn,gnd->gcd', e[m][0], bbar[0], precision=hp)
         - jnp.einsum('gcn,gnd->gcd', e[m][1], bbar[1], precision=hp) for m in range(Q)]
    eye = jnp.eye(GROUPS_PER_SLAB, dtype=F32)
    split = lambda t: t.reshape((N_SLAB, GROUPS_PER_SLAB) + t.shape[1:])
    zero_k = jnp.zeros_like(k[0])
    kb = jnp.stack([jnp.stack([split(jnp.swapaxes(k[j - i] if j >= i else zero_k, 1, 2))
                               for j in range(Q)]) for i in range(Q)])
    m_mat = jnp.einsum('ijsgdc,gh->sigdjhc', kb, eye).reshape(N_SLAB, Q * LANES, Q * LANES)
    f = [_cmul((pw[Q - 1 - i][0][..., None], pw[Q - 1 - i][1][..., None]), bbar) for i in range(Q)]
    p_parts = []
    for part in range(2):
        fs = jnp.stack([split(f[i][part]) for i in range(Q)])
        p_parts.append(jnp.einsum('isgnd,gh->sigdhn', fs, eye).reshape(N_SLAB, Q * LANES, STATE_LANES // 2))
    p_mat = jnp.concatenate(p_parts, axis=-1)
    r_parts = []
    for part, sign in ((0, 1.0), (1, -1.0)):
        es = jnp.stack([split(e[j + 1][part]) for j in range(Q)])
        r_parts.append(sign * jnp.einsum('jsgcn,gh->shnjgc', es, eye).reshape(
            N_SLAB, STATE_LANES // 2, Q * LANES))
    r_mat = jnp.concatenate(r_parts, axis=1)
    mp = jnp.concatenate([m_mat, p_mat], axis=-1).astype(BF16)
    a_q = pw[Q]
    return (mp, r_mat.astype(BF16),
            a_q[0].reshape(N_SLAB, STATE_LANES // 2), a_q[1].reshape(N_SLAB, STATE_LANES // 2))


def _router_weights(w_rg, b_rg, w_re, b_re):
    pad_g = LANE_EXP0 - LANE_GRP0 - N_GROUPS_MOE
    pad_e = LANES - LANE_EXP0 - N_EXPERTS
    w = jnp.concatenate([w_rg, jnp.zeros((D_MODEL, pad_g), F32), w_re, jnp.zeros((D_MODEL, pad_e), F32)], axis=1)
    b = jnp.concatenate([b_rg, jnp.zeros((pad_g,), F32), b_re, jnp.zeros((pad_e,), F32)]).reshape(1, LANES)
    w_hi = w.astype(BF16)
    w_lo = (w - w_hi.astype(F32)).astype(BF16)
    return jnp.concatenate([w_hi, w_lo], axis=1), w_hi, b


def _sc_mesh():
    return plsc.VectorSubcoreMesh(core_axis_name="core", subcore_axis_name="subcore")


def _sc_worker(mesh):
    return lax.axis_index("core") * mesh.num_subcores + lax.axis_index("subcore")


def _dispatch(h2p, dest):
    mesh = _sc_mesh()
    n_win = N_TOK // SC_WINDOW
    per_worker = n_win // (mesh.num_cores * mesh.num_subcores)
    assert per_worker * mesh.num_cores * mesh.num_subcores == n_win

    @pl.kernel(out_type=jax.ShapeDtypeStruct((N_ROWS,) + ROW_TILE, U32), mesh=mesh,
               scratch_types=[pltpu.VMEM((SC_WINDOW,), I32), pltpu.VMEM((SC_WINDOW,) + ROW_TILE, U32)])
    def scatter_rows(h_hbm, dest_hbm, xs_hbm, idx_v, rows_v):
        first = _sc_worker(mesh) * per_worker

        @pl.loop(0, per_worker)
        def _(w):
            win = first + w
            pltpu.sync_copy(h_hbm.at[pl.ds(win * SC_WINDOW, SC_WINDOW)], rows_v)
            for j in range(TOPK):
                pltpu.sync_copy(dest_hbm.at[j, win], idx_v)
                pltpu.sync_copy(rows_v, xs_hbm.at[idx_v])

    return scatter_rows(h2p, dest)


def _collect(ys, dest):
    mesh = _sc_mesh()
    n_tok = dest.shape[1]
    n_win = TOPK * n_tok // SC_WINDOW
    per_worker = n_win // (mesh.num_cores * mesh.num_subcores)
    assert per_worker * mesh.num_cores * mesh.num_subcores == n_win

    @pl.kernel(out_type=jax.ShapeDtypeStruct((TOPK * n_tok,) + ROW_TILE, U32), mesh=mesh,
               scratch_types=[pltpu.VMEM((SC_WINDOW,), I32), pltpu.VMEM((SC_WINDOW,) + ROW_TILE, U32)])
    def gather_rows(ys_hbm, dest_hbm, yg_hbm, idx_v, rows_v):
        first = _sc_worker(mesh) * per_worker

        @pl.loop(0, per_worker)
        def _(w):
            win = first + w
            pltpu.sync_copy(dest_hbm.at[win], idx_v)
            pltpu.sync_copy(ys_hbm.at[idx_v], rows_v)
            pltpu.sync_copy(rows_v, yg_hbm.at[pl.ds(win * SC_WINDOW, SC_WINDOW)])

    return gather_rows(ys, dest.reshape(n_win, SC_WINDOW)).reshape((TOPK, n_tok) + ROW_TILE)


def _expert_kernel(first_ref, nblk_ref, nvalid_ref, nused_ref, xs_hbm, wg_ref, wu_ref, wd_ref, ys_hbm,
                   wg_scr, wu_scr, wd_scr, x_buf, y_buf, in_sem, out_sem):
    e = pl.program_id(0)
    nused = nused_ref[0]

    def in_copy(g):
        slot = lax.rem(g, IN_SLOTS)
        return pltpu.make_async_copy(xs_hbm.at[pl.ds(g * BM, BM)], x_buf.at[slot], in_sem.at[slot])

    def out_copy(g, slot):
        return pltpu.make_async_copy(y_buf.at[slot], ys_hbm.at[pl.ds(g * BM, BM)], out_sem.at[slot])

    @pl.when(e == 0)
    def _first():
        for g in range(IN_AHEAD):
            in_copy(g).start()

    wg_scr[...] = wg_ref[0].astype(BF16)
    wu_scr[...] = wu_ref[0].astype(BF16)
    wd_scr[...] = wd_ref[0].astype(BF16)

    def block(b, carry):
        g = first_ref[e] + b
        slot = lax.rem(g, 2)
        in_copy(g).wait()

        @pl.when(g + IN_AHEAD < nused)
        def _prefetch():
            in_copy(g + IN_AHEAD).start()

        @pl.when(g >= 2)
        def _slot_free():
            out_copy(g - 2, slot).wait()

        valid = lax.broadcasted_iota(I32, (BM, 1), 0) < nvalid_ref[g]
        x_blk = x_buf[lax.rem(g, IN_SLOTS)].reshape(BM, HALF)
        lo, hi = _unpack_bf16_pair(jnp.where(valid, x_blk, jnp.uint32(0)))
        lo = lo.astype(BF16)
        hi = hi.astype(BF16)
        gate = jnp.dot(lo, wg_scr[0:HALF, :], preferred_element_type=F32) \
            + jnp.dot(hi, wg_scr[HALF:, :], preferred_element_type=F32)
        up = jnp.dot(lo, wu_scr[0:HALF, :], preferred_element_type=F32) \
            + jnp.dot(hi, wu_scr[HALF:, :], preferred_element_type=F32)
        act = (jax.nn.silu(gate) * up).astype(BF16)
        o = jnp.dot(act, wd_scr[...], preferred_element_type=F32)
        y_buf[slot] = _pack_bf16_pair(o[:, 0:HALF], o[:, HALF:]).reshape((BM,) + ROW_TILE)
        out_copy(g, slot).start()
        return carry

    lax.fori_loop(0, nblk_ref[e], block, 0)

    @pl.when(e == N_EXPERTS - 1)
    def _drain():
        out_copy(nused - 2, lax.rem(nused, 2)).wait()
        out_copy(nused - 1, 1 - lax.rem(nused, 2)).wait()


def _experts(first, nblk, nvalid, nused, xs, wg, wu, wd):
    grid_spec = pltpu.PrefetchScalarGridSpec(
        num_scalar_prefetch=4,
        grid=(N_EXPERTS,),
        in_specs=[
            pl.BlockSpec(memory_space=pl.ANY),
            pl.BlockSpec((1, D_MODEL, D_EXPERT), lambda e, *_: (e, 0, 0)),
            pl.BlockSpec((1, D_MODEL, D_EXPERT), lambda e, *_: (e, 0, 0)),
            pl.BlockSpec((1, D_EXPERT, D_MODEL), lambda e, *_: (e, 0, 0)),
        ],
        out_specs=pl.BlockSpec(memory_space=pl.ANY),
        scratch_shapes=[
            pltpu.VMEM((D_MODEL, D_EXPERT), BF16),
            pltpu.VMEM((D_MODEL, D_EXPERT), BF16),
            pltpu.VMEM((D_EXPERT, D_MODEL), BF16),
            pltpu.VMEM((IN_SLOTS, BM) + ROW_TILE, U32),
            pltpu.VMEM((2, BM) + ROW_TILE, U32),
            pltpu.SemaphoreType.DMA((IN_SLOTS,)),
            pltpu.SemaphoreType.DMA((2,)),
        ],
    )
    return pl.pallas_call(
        _expert_kernel,
        grid_spec=grid_spec,
        out_shape=jax.ShapeDtypeStruct((N_ROWS,) + ROW_TILE, U32),
        compiler_params=pltpu.CompilerParams(
            dimension_semantics=("arbitrary",), vmem_limit_bytes=VMEM_LIMIT),
        name="experts",
    )(first, nblk, nvalid, nused, xs, wg, wu, wd)


def _combine_kernel(x1_ref, rec_ref, yg_ref, p_ref, gple_ref, wpg_ref, wple_ref, gfin_ref, *rest):
    out_ref = rest[-1]
    ple = jnp.dot(p_ref[0].reshape(TM, D_PLE).astype(BF16), wple_ref[...], preferred_element_type=F32)
    rec = rec_ref[...]
    w0 = rec[:, REC_W0:REC_W0 + 1]
    w1 = rec[:, REC_W1:REC_W1 + 1]
    lo0, hi0 = _unpack_bf16_pair(yg_ref[0].reshape(TM, HALF))
    lo1, hi1 = _unpack_bf16_pair(yg_ref[1].reshape(TM, HALF))
    moe = jnp.concatenate([lo0 * w0 + lo1 * w1, hi0 * w0 + hi1 * w1], axis=1)
    x2 = x1_ref[...].reshape(TM, D_MODEL) + moe
    gate = jax.nn.sigmoid(jnp.dot(_rms(x2, gple_ref[...]).astype(BF16), wpg_ref[...],
                                  preferred_element_type=F32))
    x3 = x2 + gate * ple
    out_ref[...] = _rms(x3, gfin_ref[...]).reshape(BATCH, TT, D_MODEL)


def _combine(part, x1, rec, yg, p, gple, wpg, wple, gfin, out_prev=None):
    s0 = part * PART_STEPS
    seq_spec = pl.BlockSpec((BATCH, TT, D_MODEL), lambda i: (0, s0 + i, 0))
    in_specs = [
        seq_spec,
        pl.BlockSpec((TM, LANES), lambda i: (s0 + i, 0)),
        pl.BlockSpec((TOPK, TM) + ROW_TILE, lambda i: (0, i, 0, 0)),
        pl.BlockSpec((1, BATCH, TT, D_PLE), lambda i: (0, 0, s0 + i, 0)),
        _const_spec((1, D_MODEL)),
        _const_spec((D_MODEL, D_MODEL)),
        _const_spec((D_PLE, D_MODEL)),
        _const_spec((1, D_MODEL)),
    ]
    args = [x1, rec, yg, p, gple, wpg, wple, gfin]
    aliases = {}
    if out_prev is not None:
        in_specs.append(pl.BlockSpec(memory_space=pl.ANY))
        args.append(out_prev)
        aliases = {len(args) - 1: 0}
    return pl.pallas_call(
        _combine_kernel,
        grid=(PART_STEPS,),
        in_specs=in_specs,
        out_specs=seq_spec,
        out_shape=jax.ShapeDtypeStruct((BATCH, SEQ, D_MODEL), F32),
        input_output_aliases=aliases,
        compiler_params=pltpu.CompilerParams(
            dimension_semantics=("arbitrary",), vmem_limit_bytes=VMEM_LIMIT),
        name="combine",
    )(*args)


def kernel(x, p, g_mix, w_in, b_gate, ssm_a_re, ssm_a_im, ssm_log_dt, ssm_b_re, ssm_b_im, ssm_c_re,
           ssm_c_im, ssm_d, w_glu, conv_dw, conv_dw_b, conv_ln_g, conv_ln_b, w_conv_out, w_out, g_moe,
           w_router_group, b_router_group, w_router_expert, b_router_expert, w_exp_gate, w_exp_up,
           w_exp_down, g_ple, w_ple_gate, w_ple, g_final):
    assert x.shape == (BATCH, SEQ, D_MODEL) and p.shape == (1, BATCH, SEQ, D_PLE)
    row = lambda v: v.reshape(1, -1)

    mp, rmat, a_re, a_im = _ssm_matrices(ssm_a_re[0], ssm_a_im[0], ssm_log_dt[0], ssm_b_re[0],
                                         ssm_b_im[0], ssm_c_re[0], ssm_c_im[0])
    wr1, wr2, br = _router_weights(w_router_group[0], b_router_group[0], w_router_expert[0],
                                   b_router_expert[0])
    x1, h2p, rec, rect, cnt = _mixer(
        x, row(g_mix[0]), w_in[0], row(b_gate[0]), mp, rmat, a_re, a_im,
        row(ssm_d[0]), w_glu[0], conv_dw[0], row(conv_dw_b[0]), row(conv_ln_g[0]),
        row(conv_ln_b[0]), w_conv_out[0], w_out[0], row(g_moe[0]), wr1, wr2, br)

    counts = cnt[0, LANE_EXP0:LANE_EXP0 + N_EXPERTS].astype(I32)
    pcounts = (counts + BM - 1) // BM * BM
    pends = jnp.cumsum(pcounts)
    pstarts = pends - pcounts
    eid = rect[REC_EID0:REC_EID1 + 1].astype(I32)
    rank = rect[REC_RANK0:REC_RANK1 + 1].astype(I32)
    dest = (jnp.sum(jnp.where(eid[..., None] == jnp.arange(N_EXPERTS, dtype=I32), pstarts, 0), axis=-1)
            + rank).reshape(TOPK, N_TOK // SC_WINDOW, SC_WINDOW)
    nused = (pends[-1] // BM).astype(I32)
    blk = jnp.arange(N_BLK, dtype=I32)[:, None] * BM
    in_expert = (pstarts[None, :] <= blk) & (blk < pends[None, :])
    nvalid = jnp.clip(jnp.sum(jnp.where(in_expert, (pstarts + counts)[None, :] - blk, 0), axis=1), 0, BM)

    xs = _dispatch(h2p, dest)
    ys = _experts(pstarts // BM, pcounts // BM, nvalid.astype(I32), nused.reshape(1), xs,
                  w_exp_gate[0], w_exp_up[0], w_exp_down[0])
    dest_tok = dest.reshape(TOPK, N_TOK)
    wpg = w_ple_gate[0].astype(BF16)
    wple = w_ple[0].astype(BF16)
    out = None
    for part in range(N_PARTS):
        tok = slice(part * PART_STEPS * TM, (part + 1) * PART_STEPS * TM)
        yg = _collect(ys, dest_tok[:, tok])
        out = _combine(part, x1, rec, yg, p, row(g_ple[0]), wpg, wple, row(g_final), out)
    return out
```

```python
import jax
import jax.numpy as jnp
from jax import lax
from jax.experimental import pallas as pl
from jax.experimental.pallas import tpu as pltpu
from jax.experimental.pallas import tpu_sc as plsc

F32 = jnp.float32
BF16 = jnp.bfloat16
U32 = jnp.uint32
I32 = jnp.int32

D_MODEL = 1024
BATCH = 8
SEQ = 2048
N_TOK = BATCH * SEQ
D_SSM = 512
SSM_GROUP_WIDTH = 16
SSM_GROUPS = 32
SSM_STATE = 64
D_CONV = 512
CONV_WIDTH = 31
D_IN = D_SSM + 2 * D_CONV + 2 * D_MODEL
N_GROUPS_MOE = 4
EXPERTS_PER_GROUP = 8
N_EXPERTS = 32
TOPK = 2
D_EXPERT = 512
D_PLE = 256
EPS = 1e-6

SUBLANES = 8
LANES = 128
assert BATCH == SUBLANES

TT = 64
TM = TT * BATCH
N_STEP = SEQ // TT
SB = 512
NSB = TM // SB
BPS = SB // TT
Q = 2
N_SLAB = D_SSM // LANES
GROUPS_PER_SLAB = SSM_GROUPS // N_SLAB
ROWS_Z = TM // Q
STATE_LANES = 2 * GROUPS_PER_SLAB * SSM_STATE
HALO = (CONV_WIDTH - 1) * BATCH
CHUNK_ROWS = SB // (Q * SUBLANES)
W_STAGE_ROWS = 64
W_STAGE_SLOTS = 4
CONV_ROWS = 64
N_LC = D_CONV // LANES

LANE_GRP0 = 0
LANE_EXP0 = 32
REC_EID0, REC_EID1, REC_W0, REC_W1, REC_RANK0, REC_RANK1 = 0, 1, 2, 3, 4, 5
REC_ROWS = 8

BM = 256
N_BLK = (TOPK * N_TOK + N_EXPERTS * (BM - 1) + BM - 1) // BM
N_ROWS = N_BLK * BM
HALF = D_MODEL // 2
ROW_TILE = (HALF // LANES, LANES)
SC_WINDOW = 128
IN_AHEAD = 3
IN_SLOTS = IN_AHEAD + 1
PART_STEPS = (8, 24)
assert sum(PART_STEPS) == N_STEP

VMEM_LIMIT = 56 * 1024 * 1024


def _const_spec(shape):
    n = len(shape)
    return pl.BlockSpec(shape, lambda *_: (0,) * n, pipeline_mode=pl.Buffered(1))


def _rms(x, g):
    ms = jnp.mean(x * x, axis=-1, keepdims=True)
    return x * lax.rsqrt(ms + EPS) * g


def _pack_bf16_pair(lo, hi):
    ulo = lax.bitcast_convert_type(lo.astype(BF16).astype(F32), U32)
    uhi = lax.bitcast_convert_type(hi.astype(BF16).astype(F32), U32)
    return (ulo >> 16) | (uhi & jnp.uint32(0xFFFF0000))


def _unpack_bf16_pair(w):
    lo = lax.bitcast_convert_type(w << 16, F32)
    hi = lax.bitcast_convert_type(w & jnp.uint32(0xFFFF0000), F32)
    return lo, hi


def _load_weights_bf16(pairs, stage, sem):
    chunks = [(src, dst, r0) for src, dst in pairs for r0 in range(0, src.shape[0], W_STAGE_ROWS)]

    def copy(c):
        src, _, r0 = chunks[c]
        slot = c % W_STAGE_SLOTS
        return pltpu.make_async_copy(src.at[pl.ds(r0, W_STAGE_ROWS)],
                                     stage.at[slot, :, 0:src.shape[1]], sem.at[slot])

    for c in range(W_STAGE_SLOTS):
        copy(c).start()
    for c, (src, dst, r0) in enumerate(chunks):
        copy(c).wait()
        dst[r0:r0 + W_STAGE_ROWS, :] = stage[c % W_STAGE_SLOTS, :, 0:src.shape[1]].astype(BF16)
        if c + W_STAGE_SLOTS < len(chunks):
            copy(c + W_STAGE_SLOTS).start()


def _mixer_kernel(x_ref, gmix_ref, win_hbm, bgate_ref, mp_ref, r_ref, are_ref,
                  aim_ref, d_ref, wglu_hbm, dw_ref, dwb_ref, lng_ref, lnb_ref, wco_hbm, wout_hbm,
                  gmoe_ref, wr1_ref, wr2_ref, br_ref,
                  x1_ref, h2p_ref, rec_ref, rect_ref, cnt_ref,
                  hb_scr, ht_scr, u_scr, y_scr, yi_scr, xs_scr, z_scr, conv_scr, act_scr, actb_scr,
                  logit_scr, s_scr, cnt_scr, win_ref, wglu_ref, wco_ref, wout_ref, wstage_scr, wstage_sem):
    step = pl.program_id(0)
    assert NSB == 1

    @pl.when(step == 0)
    def _init():
        logit_scr[...] = jnp.zeros(logit_scr.shape, F32)
        z_scr[:, 0:HALO, :] = jnp.zeros((N_LC, HALO, LANES), F32)
        s_scr[...] = jnp.zeros(s_scr.shape, F32)
        cnt_scr[...] = jnp.zeros(cnt_scr.shape, F32)
        _load_weights_bf16([(win_hbm, win_ref), (wglu_hbm, wglu_ref), (wco_hbm, wco_ref),
                            (wout_hbm, wout_ref)], wstage_scr, wstage_sem)

    def sub_rows(r):
        return pl.ds(pl.multiple_of(r * SB, SB), SB)

    def phase_a(r, carry):
        xb = x_ref[pl.ds(r * BPS, BPS)].reshape(SB, D_MODEL)
        hb_scr[sub_rows(r), :] = _rms(xb, gmix_ref[...]).astype(BF16)
        return carry

    def phase_a3(r, carry):
        h = ht_scr[sub_rows(r), :]
        u = jnp.dot(h, win_ref[:, 0:D_SSM], preferred_element_type=F32)
        u_scr[pl.ds(r * CHUNK_ROWS, CHUNK_ROWS)] = u.reshape(CHUNK_ROWS, Q, SUBLANES, D_SSM)
        v = jnp.dot(h, win_ref[:, D_SSM:D_SSM + 2 * D_CONV], preferred_element_type=F32)
        zc = v[:, 0:D_CONV] * jax.nn.sigmoid(v[:, D_CONV:])
        for lc in range(N_LC):
            z_scr[lc, pl.ds(pl.multiple_of(HALO + r * SB, SUBLANES), SB), :] = zc[:, lc * LANES:(lc + 1) * LANES]
        return carry

    def phase_b():
        for s in range(N_SLAB):
            lanes = slice(s * LANES, (s + 1) * LANES)
            z = jnp.concatenate(
                [u_scr[:, i, :, lanes].reshape(ROWS_Z, LANES) for i in range(Q)], axis=1).astype(BF16)
            xp = jnp.dot(z, mp_ref[s], preferred_element_type=F32)
            yi_scr[s] = xp[:, 0:Q * LANES]
            xs_scr[s] = xp[:, Q * LANES:]

        half = STATE_LANES // 2
        for s in range(N_SLAB):
            a_re = jnp.broadcast_to(are_ref[s:s + 1, :], (SUBLANES, half))
            a_im = jnp.broadcast_to(aim_ref[s:s + 1, :], (SUBLANES, half))

            def scan_body(k, carry, s=s, a_re=a_re, a_im=a_im):
                s_re, s_im = carry
                rows = pl.ds(pl.multiple_of(k * SUBLANES, SUBLANES), SUBLANES)
                x_re = xs_scr[s, rows, 0:half]
                x_im = xs_scr[s, rows, half:]
                xs_scr[s, rows, 0:half] = s_re
                xs_scr[s, rows, half:] = s_im
                n_re = a_re * s_re - a_im * s_im + x_re
                n_im = a_re * s_im + a_im * s_re + x_im
                return n_re, n_im

            s_re, s_im = lax.fori_loop(0, ROWS_Z // SUBLANES, scan_body,
                                       (s_scr[s, :, 0:half], s_scr[s, :, half:]), unroll=True)
            s_scr[s, :, 0:half] = s_re
            s_scr[s, :, half:] = s_im

        for s in range(N_SLAB):
            lanes = slice(s * LANES, (s + 1) * LANES)
            y_tot = yi_scr[s] + jnp.dot(xs_scr[s].astype(BF16), r_ref[s], preferred_element_type=F32)
            for j in range(Q):
                y_scr[:, j, :, lanes] = y_tot[:, j * LANES:(j + 1) * LANES].reshape(
                    ROWS_Z // SUBLANES, SUBLANES, LANES)

    def phase_c1(r, carry):
        rows = sub_rows(r)
        crow = pl.ds(r * CHUNK_ROWS, CHUNK_ROWS)
        y = y_scr[crow].reshape(SB, D_SSM) + d_ref[...] * u_scr[crow].reshape(SB, D_SSM)
        act_scr[rows, 0:D_SSM] = jax.nn.gelu(y).astype(BF16)
        for lc in range(N_LC):
            lanes = slice(lc * LANES, (lc + 1) * LANES)

            def conv_piece(rc, c, lc=lc, lanes=lanes):
                r0 = r * SB + rc * CONV_ROWS
                piece = jnp.broadcast_to(dwb_ref[:, lanes], (CONV_ROWS, LANES))
                for j in range(CONV_WIDTH):
                    zrows = pl.ds(pl.multiple_of(r0 + j * BATCH, SUBLANES), CONV_ROWS)
                    piece = piece + dw_ref[j:j + 1, lanes] * z_scr[lc, zrows, :]
                conv_scr[pl.ds(pl.multiple_of(rc * CONV_ROWS, CONV_ROWS), CONV_ROWS), lanes] = piece
                return c

            lax.fori_loop(0, SB // CONV_ROWS, conv_piece, 0, unroll=4)
        acc = conv_scr[...]
        mu = jnp.mean(acc, axis=-1, keepdims=True)
        cen = acc - mu
        var = jnp.mean(cen * cen, axis=-1, keepdims=True)
        ln = cen * lax.rsqrt(var + EPS) * lng_ref[...] + lnb_ref[...]
        act_scr[rows, D_SSM:] = jax.nn.silu(ln).astype(BF16)
        return carry

    lane = lax.broadcasted_iota(I32, (1, LANES), 1).astype(F32)
    grp_mask = lane < float(N_GROUPS_MOE)
    exp_lane = (lane >= float(LANE_EXP0)) & (lane < float(LANE_EXP0 + N_EXPERTS))
    lane_grp = jnp.floor((lane - float(LANE_EXP0)) * (1.0 / EXPERTS_PER_GROUP))
    tri = (lax.broadcasted_iota(I32, (SB, SB), 0) > lax.broadcasted_iota(I32, (SB, SB), 1)).astype(BF16)
    neg_inf = float("-inf")
    big = float(4 * LANES)

    def phase_c3(r, carry):
        rows = sub_rows(r)
        h = hb_scr[rows, :]
        g0 = D_SSM + 2 * D_CONV
        gate_ssm = jnp.dot(h, win_ref[:, g0:g0 + D_MODEL], preferred_element_type=F32) \
            + bgate_ref[:, 0:D_MODEL]
        gate_conv = jnp.dot(h, win_ref[:, g0 + D_MODEL:], preferred_element_type=F32) \
            + bgate_ref[:, D_MODEL:]
        zz = jnp.dot(actb_scr[rows, 0:D_SSM], wglu_ref[...], preferred_element_type=F32)
        y_ssm = zz[:, 0:D_MODEL] * jax.nn.sigmoid(zz[:, D_MODEL:])
        y_conv = jnp.dot(actb_scr[rows, D_SSM:], wco_ref[...], preferred_element_type=F32)

        merged = jax.nn.sigmoid(gate_ssm) * y_ssm + jax.nn.sigmoid(gate_conv) * y_conv
        xb = x_ref[pl.ds(r * BPS, BPS)].reshape(SB, D_MODEL)
        x1 = xb + jnp.dot(merged.astype(BF16), wout_ref[...], preferred_element_type=F32)
        x1_ref[pl.ds(r * BPS, BPS)] = x1.reshape(BPS, TT, D_MODEL)

        h2 = _rms(x1, gmoe_ref[...])
        h2p_ref[rows] = _pack_bf16_pair(h2[:, 0:HALF], h2[:, HALF:]).reshape((SB,) + ROW_TILE)

        h2_hi = h2.astype(BF16)
        h2_lo = (h2 - h2_hi.astype(F32)).astype(BF16)
        l1 = jnp.dot(h2_hi, wr1_ref[...], preferred_element_type=F32)
        l2 = jnp.dot(h2_lo, wr2_ref[...], preferred_element_type=F32)
        logit_scr[rows, :] = l1[:, 0:LANES] + l1[:, LANES:] + l2 + br_ref[...]
        return carry

    def route_previous():
        rows = sub_rows(0)
        logits = logit_scr[...]
        counted = jnp.where(step > 0, 1.0, 0.0)

        lg = jnp.where(grp_mask, logits, neg_inf)
        g_max = jnp.max(lg, axis=-1, keepdims=True)
        g_sel = jnp.min(jnp.where(lg == g_max, lane, big), axis=-1, keepdims=True)
        p_g = 1.0 / jnp.sum(jnp.where(grp_mask, jnp.exp(logits - g_max), 0.0), axis=-1, keepdims=True)
        le = jnp.where(exp_lane & (lane_grp == g_sel), logits, neg_inf)
        m1 = jnp.max(le, axis=-1, keepdims=True)
        i1 = jnp.min(jnp.where(le == m1, lane, big), axis=-1, keepdims=True)
        le2 = jnp.where(lane == i1, neg_inf, le)
        m2 = jnp.max(le2, axis=-1, keepdims=True)
        i2 = jnp.min(jnp.where(le2 == m2, lane, big), axis=-1, keepdims=True)
        e2 = jnp.exp(m2 - m1)
        den = 1.0 + e2
        w_a = (1.0 / den) * p_g
        w_b = (e2 / den) * p_g

        sel1 = lane == i1
        sel2 = lane == i2
        onehot = jnp.where(sel1 | sel2, counted, 0.0)
        prefix = jnp.dot(tri, onehot.astype(BF16), preferred_element_type=F32) + cnt_scr[...]
        rank_a = jnp.sum(jnp.where(sel1, prefix, 0.0), axis=-1, keepdims=True)
        rank_b = jnp.sum(jnp.where(sel2, prefix, 0.0), axis=-1, keepdims=True)
        cnt_scr[...] = cnt_scr[...] + jnp.sum(onehot, axis=0, keepdims=True)

        rec = jnp.where(lane == float(REC_EID0), i1 - float(LANE_EXP0), 0.0)
        rec = jnp.where(lane == float(REC_EID1), i2 - float(LANE_EXP0), rec)
        rec = jnp.where(lane == float(REC_W0), w_a, rec)
        rec = jnp.where(lane == float(REC_W1), w_b, rec)
        rec = jnp.where(lane == float(REC_RANK0), rank_a, rec)
        rec = jnp.where(lane == float(REC_RANK1), rank_b, rec)
        rec_ref[rows, :] = rec
        rect_ref[...] = jnp.transpose(rec)[0:REC_ROWS, :]
        cnt_ref[...] = cnt_scr[...]

    @pl.when(step < N_STEP)
    def _tile():
        route_previous()
        phase_a(0, 0)
        ht_scr[...] = jnp.swapaxes(hb_scr[...].reshape(BATCH, TT, D_MODEL), 0, 1).reshape(TM, D_MODEL)
        phase_a3(0, 0)
        phase_b()
        phase_c1(0, 0)
        z_scr[:, 0:HALO, :] = z_scr[:, TM:TM + HALO, :]
        actb_scr[...] = jnp.swapaxes(act_scr[...].reshape(TT, BATCH, D_SSM + D_CONV), 0, 1).reshape(
            TM, D_SSM + D_CONV)
        phase_c3(0, 0)

    @pl.when(step == N_STEP)
    def _last():
        route_previous()


def _mixer(x, gmix, win, bgate, mp, rmat, a_re, a_im, dvec, wglu, dw, dwb, lng, lnb, wco,
           wout, gmoe, wr1, wr2, br):
    tile = lambda i: jnp.minimum(i, N_STEP - 1)
    routed = lambda i: jnp.maximum(i - 1, 0)
    seq_spec = pl.BlockSpec((BATCH, TT, D_MODEL), lambda i: (0, tile(i), 0))
    in_hbm = pl.BlockSpec(memory_space=pl.ANY)
    in_specs = [
        seq_spec,
        _const_spec((1, D_MODEL)),
        in_hbm,
        _const_spec((1, 2 * D_MODEL)),
        _const_spec(mp.shape),
        _const_spec(rmat.shape),
        _const_spec(a_re.shape),
        _const_spec(a_im.shape),
        _const_spec((1, D_SSM)),
        in_hbm,
        _const_spec((CONV_WIDTH, D_CONV)),
        _const_spec((1, D_CONV)),
        _const_spec((1, D_CONV)),
        _const_spec((1, D_CONV)),
        in_hbm,
        in_hbm,
        _const_spec((1, D_MODEL)),
        _const_spec((D_MODEL, 2 * LANES)),
        _const_spec((D_MODEL, LANES)),
        _const_spec((1, LANES)),
    ]
    out_specs = [
        seq_spec,
        pl.BlockSpec((TM,) + ROW_TILE, lambda i: (tile(i), 0, 0)),
        pl.BlockSpec((TM, LANES), lambda i: (routed(i), 0)),
        pl.BlockSpec((REC_ROWS, TM), lambda i: (0, routed(i))),
        pl.BlockSpec((1, LANES), lambda i: (0, 0)),
    ]
    out_shape = [
        jax.ShapeDtypeStruct((BATCH, SEQ, D_MODEL), F32),
        jax.ShapeDtypeStruct((N_TOK,) + ROW_TILE, U32),
        jax.ShapeDtypeStruct((N_TOK, LANES), F32),
        jax.ShapeDtypeStruct((REC_ROWS, N_TOK), F32),
        jax.ShapeDtypeStruct((1, LANES), F32),
    ]
    chunk_shape = (ROWS_Z // SUBLANES, Q, SUBLANES, D_SSM)
    scratch = [
        pltpu.VMEM((TM, D_MODEL), BF16),
        pltpu.VMEM((TM, D_MODEL), BF16),
        pltpu.VMEM(chunk_shape, F32),
        pltpu.VMEM(chunk_shape, F32),
        pltpu.VMEM((N_SLAB, ROWS_Z, Q * LANES), F32),
        pltpu.VMEM((N_SLAB, ROWS_Z, STATE_LANES), F32),
        pltpu.VMEM((N_LC, HALO + TM, LANES), F32),
        pltpu.VMEM((SB, D_CONV), F32),
        pltpu.VMEM((TM, D_SSM + D_CONV), BF16),
        pltpu.VMEM((TM, D_SSM + D_CONV), BF16),
        pltpu.VMEM((TM, LANES), F32),
        pltpu.VMEM((N_SLAB, SUBLANES, STATE_LANES), F32),
        pltpu.VMEM((1, LANES), F32),
        pltpu.VMEM(win.shape, BF16),
        pltpu.VMEM(wglu.shape, BF16),
        pltpu.VMEM(wco.shape, BF16),
        pltpu.VMEM(wout.shape, BF16),
        pltpu.VMEM((W_STAGE_SLOTS, W_STAGE_ROWS, D_IN), F32),
        pltpu.SemaphoreType.DMA((W_STAGE_SLOTS,)),
    ]
    assert win.shape == (D_MODEL, D_IN) and wglu.shape == (D_SSM, 2 * D_MODEL)
    assert wco.shape == (D_CONV, D_MODEL) and wout.shape == (D_MODEL, D_MODEL)
    return pl.pallas_call(
        _mixer_kernel,
        grid=(N_STEP + 1,),
        in_specs=in_specs,
        out_specs=out_specs,
        out_shape=out_shape,
        scratch_shapes=scratch,
        compiler_params=pltpu.CompilerParams(
            dimension_semantics=("arbitrary",), vmem_limit_bytes=VMEM_LIMIT),
        name="mixer",
    )(x, gmix, win, bgate, mp, rmat, a_re, a_im, dvec, wglu, dw, dwb, lng, lnb, wco, wout,
      gmoe, wr1, wr2, br)


def _cmul(a, b):
    return a[0] * b[0] - a[1] * b[1], a[0] * b[1] + a[1] * b[0]


def _ssm_matrices(a_re, a_im, log_dt, b_re, b_im, c_re, c_im):
    dt = jnp.exp(log_dt)[:, None]
    mag = jnp.exp(a_re * dt)
    lam = (mag * jnp.cos(a_im * dt), mag * jnp.sin(a_im * dt))
    den = a_re * a_re + a_im * a_im
    nr = lam[0] - 1.0
    ni = lam[1]
    z_re = (nr * a_re + ni * a_im) / den
    z_im = (ni * a_re - nr * a_im) / den
    bbar = (z_re[..., None] * b_re - z_im[..., None] * b_im,
            z_re[..., None] * b_im + z_im[..., None] * b_re)
    pw = [(jnp.ones_like(lam[0]), jnp.zeros_like(lam[0])), lam]
    for _ in range(2, Q + 1):
        pw.append(_cmul(pw[-1], lam))
    e = [(c_re * p[0][:, None, :] - c_im * p[1][:, None, :],
          c_re * p[1][:, None, :] + c_im * p[0][:, None, :]) for p in pw]
    hp = lax.Precision.HIGHEST
    k = [jnp.einsum('gcn,gnd->gcd', e[m][0], bbar[0], precision=hp)
         - jnp.einsum('gcn,gnd->gcd', e[m][1], bbar[1], precision=hp) for m in range(Q)]
    eye = jnp.eye(GROUPS_PER_SLAB, dtype=F32)
    split = lambda t: t.reshape((N_SLAB, GROUPS_PER_SLAB) + t.shape[1:])
    zero_k = jnp.zeros_like(k[0])
    kb = jnp.stack([jnp.stack([split(jnp.swapaxes(k[j - i] if j >= i else zero_k, 1, 2))
                               for j in range(Q)]) for i in range(Q)])
    m_mat = jnp.einsum('ijsgdc,gh->sigdjhc', kb, eye).reshape(N_SLAB, Q * LANES, Q * LANES)
    f = [_cmul((pw[Q - 1 - i][0][..., None], pw[Q - 1 - i][1][..., None]), bbar) for i in range(Q)]
    p_parts = []
    for part in range(2):
        fs = jnp.stack([split(f[i][part]) for i in range(Q)])
        p_parts.append(jnp.einsum('isgnd,gh->sigdhn', fs, eye).reshape(N_SLAB, Q * LANES, STATE_LANES // 2))
    p_mat = jnp.concatenate(p_parts, axis=-1)
    r_parts = []
    for part, sign in ((0, 1.0), (1, -1.0)):
        es = jnp.stack([split(e[j + 1][part]) for j in range(Q)])
        r_parts.append(sign * jnp.einsum('jsgcn,gh->shnjgc', es, eye).reshape(
            N_SLAB, STATE_LANES // 2, Q * LANES))
    r_mat = jnp.concatenate(r_parts, axis=1)
    mp = jnp.concatenate([m_mat, p_mat], axis=-1).astype(BF16)
    a_q = pw[Q]
    return (mp, r_mat.astype(BF16),
            a_q[0].reshape(N_SLAB, STATE_LANES // 2), a_q[1].reshape(N_SLAB, STATE_LANES // 2))


def _router_weights(w_rg, b_rg, w_re, b_re):
    pad_g = LANE_EXP0 - LANE_GRP0 - N_GROUPS_MOE
    pad_e = LANES - LANE_EXP0 - N_EXPERTS
    w = jnp.concatenate([w_rg, jnp.zeros((D_MODEL, pad_g), F32), w_re, jnp.zeros((D_MODEL, pad_e), F32)], axis=1)
    b = jnp.concatenate([b_rg, jnp.zeros((pad_g,), F32), b_re, jnp.zeros((pad_e,), F32)]).reshape(1, LANES)
    w_hi = w.astype(BF16)
    w_lo = (w - w_hi.astype(F32)).astype(BF16)
    return jnp.concatenate([w_hi, w_lo], axis=1), w_hi, b


def _sc_mesh():
    return plsc.VectorSubcoreMesh(core_axis_name="core", subcore_axis_name="subcore")


def _sc_worker(mesh):
    return lax.axis_index("core") * mesh.num_subcores + lax.axis_index("subcore")


def _dispatch(h2p, dest):
    mesh = _sc_mesh()
    n_win = N_TOK // SC_WINDOW
    per_worker = n_win // (mesh.num_cores * mesh.num_subcores)
    assert per_worker * mesh.num_cores * mesh.num_subcores == n_win

    @pl.kernel(out_type=jax.ShapeDtypeStruct((N_ROWS,) + ROW_TILE, U32), mesh=mesh,
               scratch_types=[pltpu.VMEM((SC_WINDOW,), I32), pltpu.VMEM((SC_WINDOW,) + ROW_TILE, U32)])
    def scatter_rows(h_hbm, dest_hbm, xs_hbm, idx_v, rows_v):
        first = _sc_worker(mesh) * per_worker

        @pl.loop(0, per_worker)
        def _(w):
            win = first + w
            pltpu.sync_copy(h_hbm.at[pl.ds(win * SC_WINDOW, SC_WINDOW)], rows_v)
            for j in range(TOPK):
                pltpu.sync_copy(dest_hbm.at[j, win], idx_v)
                pltpu.sync_copy(rows_v, xs_hbm.at[idx_v])

    return scatter_rows(h2p, dest)


def _collect(ys, dest):
    mesh = _sc_mesh()
    n_tok = dest.shape[1]
    n_win = TOPK * n_tok // SC_WINDOW
    per_worker = n_win // (mesh.num_cores * mesh.num_subcores)
    assert per_worker * mesh.num_cores * mesh.num_subcores == n_win

    @pl.kernel(out_type=jax.ShapeDtypeStruct((TOPK * n_tok,) + ROW_TILE, U32), mesh=mesh,
               scratch_types=[pltpu.VMEM((SC_WINDOW,), I32), pltpu.VMEM((SC_WINDOW,) + ROW_TILE, U32)])
    def gather_rows(ys_hbm, dest_hbm, yg_hbm, idx_v, rows_v):
        first = _sc_worker(mesh) * per_worker

        @pl.loop(0, per_worker)
        def _(w):
            win = first + w
            pltpu.sync_copy(dest_hbm.at[win], idx_v)
            pltpu.sync_copy(ys_hbm.at[idx_v], rows_v)
            pltpu.sync_copy(rows_v, yg_hbm.at[pl.ds(win * SC_WINDOW, SC_WINDOW)])

    return gather_rows(ys, dest.reshape(n_win, SC_WINDOW)).reshape((TOPK, n_tok) + ROW_TILE)


def _expert_kernel(first_ref, nblk_ref, nvalid_ref, nused_ref, xs_hbm, wg_ref, wu_ref, wd_ref, ys_hbm,
                   wg_scr, wu_scr, wd_scr, x_buf, y_buf, in_sem, out_sem):
    e = pl.program_id(0)
    nused = nused_ref[0]

    def in_copy(g):
        slot = lax.rem(g, IN_SLOTS)
        return pltpu.make_async_copy(xs_hbm.at[pl.ds(g * BM, BM)], x_buf.at[slot], in_sem.at[slot])

    def out_copy(g, slot):
        return pltpu.make_async_copy(y_buf.at[slot], ys_hbm.at[pl.ds(g * BM, BM)], out_sem.at[slot])

    @pl.when(e == 0)
    def _first():
        for g in range(IN_AHEAD):
            in_copy(g).start()

    wg_scr[...] = wg_ref[0].astype(BF16)
    wu_scr[...] = wu_ref[0].astype(BF16)
    wd_scr[...] = wd_ref[0].astype(BF16)

    def block(b, carry):
        g = first_ref[e] + b
        slot = lax.rem(g, 2)
        in_copy(g).wait()

        @pl.when(g + IN_AHEAD < nused)
        def _prefetch():
            in_copy(g + IN_AHEAD).start()

        @pl.when(g >= 2)
        def _slot_free():
            out_copy(g - 2, slot).wait()

        valid = lax.broadcasted_iota(I32, (BM, 1), 0) < nvalid_ref[g]
        x_blk = x_buf[lax.rem(g, IN_SLOTS)].reshape(BM, HALF)
        lo, hi = _unpack_bf16_pair(jnp.where(valid, x_blk, jnp.uint32(0)))
        lo = lo.astype(BF16)
        hi = hi.astype(BF16)
        gate = jnp.dot(lo, wg_scr[0:HALF, :], preferred_element_type=F32) \
            + jnp.dot(hi, wg_scr[HALF:, :], preferred_element_type=F32)
        up = jnp.dot(lo, wu_scr[0:HALF, :], preferred_element_type=F32) \
            + jnp.dot(hi, wu_scr[HALF:, :], preferred_element_type=F32)
        act = (jax.nn.silu(gate) * up).astype(BF16)
        o = jnp.dot(act, wd_scr[...], preferred_element_type=F32)
        y_buf[slot] = _pack_bf16_pair(o[:, 0:HALF], o[:, HALF:]).reshape((BM,) + ROW_TILE)
        out_copy(g, slot).start()
        return carry

    lax.fori_loop(0, nblk_ref[e], block, 0)

    @pl.when(e == N_EXPERTS - 1)
    def _drain():
        out_copy(nused - 2, lax.rem(nused, 2)).wait()
        out_copy(nused - 1, 1 - lax.rem(nused, 2)).wait()


def _experts(first, nblk, nvalid, nused, xs, wg, wu, wd):
    grid_spec = pltpu.PrefetchScalarGridSpec(
        num_scalar_prefetch=4,
        grid=(N_EXPERTS,),
        in_specs=[
            pl.BlockSpec(memory_space=pl.ANY),
            pl.BlockSpec((1, D_MODEL, D_EXPERT), lambda e, *_: (e, 0, 0)),
            pl.BlockSpec((1, D_MODEL, D_EXPERT), lambda e, *_: (e, 0, 0)),
            pl.BlockSpec((1, D_EXPERT, D_MODEL), lambda e, *_: (e, 0, 0)),
        ],
        out_specs=pl.BlockSpec(memory_space=pl.ANY),
        scratch_shapes=[
            pltpu.VMEM((D_MODEL, D_EXPERT), BF16),
            pltpu.VMEM((D_MODEL, D_EXPERT), BF16),
            pltpu.VMEM((D_EXPERT, D_MODEL), BF16),
            pltpu.VMEM((IN_SLOTS, BM) + ROW_TILE, U32),
            pltpu.VMEM((2, BM) + ROW_TILE, U32),
            pltpu.SemaphoreType.DMA((IN_SLOTS,)),
            pltpu.SemaphoreType.DMA((2,)),
        ],
    )
    return pl.pallas_call(
        _expert_kernel,
        grid_spec=grid_spec,
        out_shape=jax.ShapeDtypeStruct((N_ROWS,) + ROW_TILE, U32),
        compiler_params=pltpu.CompilerParams(
            dimension_semantics=("arbitrary",), vmem_limit_bytes=VMEM_LIMIT),
        name="experts",
    )(first, nblk, nvalid, nused, xs, wg, wu, wd)


def _combine_kernel(x1_ref, rec_ref, yg_ref, p_ref, gple_ref, wpg_ref, wple_ref, gfin_ref, *rest):
    out_ref = rest[-1]
    ple = jnp.dot(p_ref[0].reshape(TM, D_PLE).astype(BF16), wple_ref[...], preferred_element_type=F32)
    rec = rec_ref[...]
    w0 = rec[:, REC_W0:REC_W0 + 1]
    w1 = rec[:, REC_W1:REC_W1 + 1]
    lo0, hi0 = _unpack_bf16_pair(yg_ref[0].reshape(TM, HALF))
    lo1, hi1 = _unpack_bf16_pair(yg_ref[1].reshape(TM, HALF))
    moe = jnp.concatenate([lo0 * w0 + lo1 * w1, hi0 * w0 + hi1 * w1], axis=1)
    x2 = x1_ref[...].reshape(TM, D_MODEL) + moe
    gate = jax.nn.sigmoid(jnp.dot(_rms(x2, gple_ref[...]).astype(BF16), wpg_ref[...],
                                  preferred_element_type=F32))
    x3 = x2 + gate * ple
    out_ref[...] = _rms(x3, gfin_ref[...]).reshape(BATCH, TT, D_MODEL)


def _combine(s0, n_steps, x1, rec, yg, p, gple, wpg, wple, gfin, out_prev=None):
    seq_spec = pl.BlockSpec((BATCH, TT, D_MODEL), lambda i: (0, s0 + i, 0))
    in_specs = [
        seq_spec,
        pl.BlockSpec((TM, LANES), lambda i: (s0 + i, 0)),
        pl.BlockSpec((TOPK, TM) + ROW_TILE, lambda i: (0, i, 0, 0)),
        pl.BlockSpec((1, BATCH, TT, D_PLE), lambda i: (0, 0, s0 + i, 0)),
        _const_spec((1, D_MODEL)),
        _const_spec((D_MODEL, D_MODEL)),
        _const_spec((D_PLE, D_MODEL)),
        _const_spec((1, D_MODEL)),
    ]
    args = [x1, rec, yg, p, gple, wpg, wple, gfin]
    aliases = {}
    if out_prev is not None:
        in_specs.append(pl.BlockSpec(memory_space=pl.ANY))
        args.append(out_prev)
        aliases = {len(args) - 1: 0}
    return pl.pallas_call(
        _combine_kernel,
        grid=(n_steps,),
        in_specs=in_specs,
        out_specs=seq_spec,
        out_shape=jax.ShapeDtypeStruct((BATCH, SEQ, D_MODEL), F32),
        input_output_aliases=aliases,
        compiler_params=pltpu.CompilerParams(
            dimension_semantics=("arbitrary",), vmem_limit_bytes=VMEM_LIMIT),
        name="combine",
    )(*args)


def kernel(x, p, g_mix, w_in, b_gate, ssm_a_re, ssm_a_im, ssm_log_dt, ssm_b_re, ssm_b_im, ssm_c_re,
           ssm_c_im, ssm_d, w_glu, conv_dw, conv_dw_b, conv_ln_g, conv_ln_b, w_conv_out, w_out, g_moe,
           w_router_group, b_router_group, w_router_expert, b_router_expert, w_exp_gate, w_exp_up,
           w_exp_down, g_ple, w_ple_gate, w_ple, g_final):
    assert x.shape == (BATCH, SEQ, D_MODEL) and p.shape == (1, BATCH, SEQ, D_PLE)
    row = lambda v: v.reshape(1, -1)

    mp, rmat, a_re, a_im = _ssm_matrices(ssm_a_re[0], ssm_a_im[0], ssm_log_dt[0], ssm_b_re[0],
                                         ssm_b_im[0], ssm_c_re[0], ssm_c_im[0])
    wr1, wr2, br = _router_weights(w_router_group[0], b_router_group[0], w_router_expert[0],
                                   b_router_expert[0])
    x1, h2p, rec, rect, cnt = _mixer(
        x, row(g_mix[0]), w_in[0], row(b_gate[0]), mp, rmat, a_re, a_im,
        row(ssm_d[0]), w_glu[0], conv_dw[0], row(conv_dw_b[0]), row(conv_ln_g[0]),
        row(conv_ln_b[0]), w_conv_out[0], w_out[0], row(g_moe[0]), wr1, wr2, br)

    counts = cnt[0, LANE_EXP0:LANE_EXP0 + N_EXPERTS].astype(I32)
    pcounts = (counts + BM - 1) // BM * BM
    pends = jnp.cumsum(pcounts)
    pstarts = pends - pcounts
    eid = rect[REC_EID0:REC_EID1 + 1].astype(I32)
    rank = rect[REC_RANK0:REC_RANK1 + 1].astype(I32)
    dest = (jnp.sum(jnp.where(eid[..., None] == jnp.arange(N_EXPERTS, dtype=I32), pstarts, 0), axis=-1)
            + rank).reshape(TOPK, N_TOK // SC_WINDOW, SC_WINDOW)
    nused = (pends[-1] // BM).astype(I32)
    blk = jnp.arange(N_BLK, dtype=I32)[:, None] * BM
    in_expert = (pstarts[None, :] <= blk) & (blk < pends[None, :])
    nvalid = jnp.clip(jnp.sum(jnp.where(in_expert, (pstarts + counts)[None, :] - blk, 0), axis=1), 0, BM)

    xs = _dispatch(h2p, dest)
    ys = _experts(pstarts // BM, pcounts // BM, nvalid.astype(I32), nused.reshape(1), xs,
                  w_exp_gate[0], w_exp_up[0], w_exp_down[0])
    dest_tok = dest.reshape(TOPK, N_TOK)
    wpg = w_ple_gate[0].astype(BF16)
    wple = w_ple[0].astype(BF16)
    out = None
    s0 = 0
    for n_steps in PART_STEPS:
        yg = _collect(ys, dest_tok[:, s0 * TM:(s0 + n_steps) * TM])
        out = _combine(s0, n_steps, x1, rec, yg, p, row(g_ple[0]), wpg, wple, row(g_final), out)
        s0 += n_steps
    return out
```

```python
import jax
import jax.numpy as jnp
from jax import lax
from jax.experimental import pallas as pl
from jax.experimental.pallas import tpu as pltpu
from jax.experimental.pallas import tpu_sc as plsc

F32 = jnp.float32
BF16 = jnp.bfloat16
U32 = jnp.uint32
I32 = jnp.int32

D_MODEL = 1024
BATCH = 8
SEQ = 2048
N_TOK = BATCH * SEQ
D_SSM = 512
SSM_GROUP_WIDTH = 16
SSM_GROUPS = 32
SSM_STATE = 64
D_CONV = 512
CONV_WIDTH = 31
D_IN = D_SSM + 2 * D_CONV + 2 * D_MODEL
N_GROUPS_MOE = 4
EXPERTS_PER_GROUP = 8
N_EXPERTS = 32
TOPK = 2
D_EXPERT = 512
D_PLE = 256
EPS = 1e-6

SUBLANES = 8
LANES = 128
assert BATCH == SUBLANES

TT = 64
TM = TT * BATCH
N_STEP = SEQ // TT
SB = 512
NSB = TM // SB
BPS = SB // TT
Q = 2
N_SLAB = D_SSM // LANES
GROUPS_PER_SLAB = SSM_GROUPS // N_SLAB
ROWS_Z = TM // Q
STATE_LANES = 2 * GROUPS_PER_SLAB * SSM_STATE
HALO = (CONV_WIDTH - 1) * BATCH
CHUNK_ROWS = SB // (Q * SUBLANES)
W_STAGE_ROWS = 64
W_STAGE_SLOTS = 4
CONV_ROWS = 64
N_LC = D_CONV // LANES

LANE_GRP0 = 0
LANE_EXP0 = 32
REC_EID0, REC_EID1, REC_W0, REC_W1, REC_RANK0, REC_RANK1 = 0, 1, 2, 3, 4, 5
REC_ROWS = 8

BM = 256
N_BLK = (TOPK * N_TOK + N_EXPERTS * (BM - 1) + BM - 1) // BM
N_ROWS = N_BLK * BM
HALF = D_MODEL // 2
ROW_TILE = (HALF // LANES, LANES)
SC_WINDOW = 128
IN_AHEAD = 3
IN_SLOTS = IN_AHEAD + 1
PART_STEPS = (12, 20)
assert sum(PART_STEPS) == N_STEP

VMEM_LIMIT = 56 * 1024 * 1024


def _const_spec(shape):
    n = len(shape)
    return pl.BlockSpec(shape, lambda *_: (0,) * n, pipeline_mode=pl.Buffered(1))


def _rms(x, g):
    ms = jnp.mean(x * x, axis=-1, keepdims=True)
    return x * lax.rsqrt(ms + EPS) * g


def _pack_bf16_pair(lo, hi):
    ulo = lax.bitcast_convert_type(lo.astype(BF16).astype(F32), U32)
    uhi = lax.bitcast_convert_type(hi.astype(BF16).astype(F32), U32)
    return (ulo >> 16) | (uhi & jnp.uint32(0xFFFF0000))


def _unpack_bf16_pair(w):
    lo = lax.bitcast_convert_type(w << 16, F32)
    hi = lax.bitcast_convert_type(w & jnp.uint32(0xFFFF0000), F32)
    return lo, hi


def _load_weights_bf16(pairs, stage, sem):
    chunks = [(src, dst, r0) for src, dst in pairs for r0 in range(0, src.shape[0], W_STAGE_ROWS)]

    def copy(c):
        src, _, r0 = chunks[c]
        slot = c % W_STAGE_SLOTS
        return pltpu.make_async_copy(src.at[pl.ds(r0, W_STAGE_ROWS)],
                                     stage.at[slot, :, 0:src.shape[1]], sem.at[slot])

    for c in range(W_STAGE_SLOTS):
        copy(c).start()
    for c, (src, dst, r0) in enumerate(chunks):
        copy(c).wait()
        dst[r0:r0 + W_STAGE_ROWS, :] = stage[c % W_STAGE_SLOTS, :, 0:src.shape[1]].astype(BF16)
        if c + W_STAGE_SLOTS < len(chunks):
            copy(c + W_STAGE_SLOTS).start()


def _mixer_kernel(x_ref, gmix_ref, win_hbm, bgate_ref, mp_ref, r_ref, are_ref,
                  aim_ref, d_ref, wglu_hbm, dw_ref, dwb_ref, lng_ref, lnb_ref, wco_hbm, wout_hbm,
                  gmoe_ref, wr1_ref, wr2_ref, br_ref,
                  x1_ref, h2p_ref, rec_ref, rect_ref, cnt_ref,
                  hb_scr, ht_scr, u_scr, y_scr, yi_scr, xs_scr, z_scr, conv_scr, act_scr, actb_scr,
                  logit_scr, s_scr, cnt_scr, win_ref, wglu_ref, wco_ref, wout_ref, wstage_scr, wstage_sem):
    step = pl.program_id(0)
    assert NSB == 1

    @pl.when(step == 0)
    def _init():
        logit_scr[...] = jnp.zeros(logit_scr.shape, F32)
        z_scr[:, 0:HALO, :] = jnp.zeros((N_LC, HALO, LANES), F32)
        s_scr[...] = jnp.zeros(s_scr.shape, F32)
        cnt_scr[...] = jnp.zeros(cnt_scr.shape, F32)
        _load_weights_bf16([(win_hbm, win_ref), (wglu_hbm, wglu_ref), (wco_hbm, wco_ref),
                            (wout_hbm, wout_ref)], wstage_scr, wstage_sem)

    def sub_rows(r):
        return pl.ds(pl.multiple_of(r * SB, SB), SB)

    def phase_a(r, carry):
        xb = x_ref[pl.ds(r * BPS, BPS)].reshape(SB, D_MODEL)
        hb_scr[sub_rows(r), :] = _rms(xb, gmix_ref[...]).astype(BF16)
        return carry

    def phase_a3(r, carry):
        h = ht_scr[sub_rows(r), :]
        u = jnp.dot(h, win_ref[:, 0:D_SSM], preferred_element_type=F32)
        u_scr[pl.ds(r * CHUNK_ROWS, CHUNK_ROWS)] = u.reshape(CHUNK_ROWS, Q, SUBLANES, D_SSM)
        v = jnp.dot(h, win_ref[:, D_SSM:D_SSM + 2 * D_CONV], preferred_element_type=F32)
        zc = v[:, 0:D_CONV] * jax.nn.sigmoid(v[:, D_CONV:])
        for lc in range(N_LC):
            z_scr[lc, pl.ds(pl.multiple_of(HALO + r * SB, SUBLANES), SB), :] = zc[:, lc * LANES:(lc + 1) * LANES]
        return carry

    def phase_b():
        for s in range(N_SLAB):
            lanes = slice(s * LANES, (s + 1) * LANES)
            z = jnp.concatenate(
                [u_scr[:, i, :, lanes].reshape(ROWS_Z, LANES) for i in range(Q)], axis=1).astype(BF16)
            xp = jnp.dot(z, mp_ref[s], preferred_element_type=F32)
            yi_scr[s] = xp[:, 0:Q * LANES]
            xs_scr[s] = xp[:, Q * LANES:]

        half = STATE_LANES // 2
        for s in range(N_SLAB):
            a_re = jnp.broadcast_to(are_ref[s:s + 1, :], (SUBLANES, half))
            a_im = jnp.broadcast_to(aim_ref[s:s + 1, :], (SUBLANES, half))

            def scan_body(k, carry, s=s, a_re=a_re, a_im=a_im):
                s_re, s_im = carry
                rows = pl.ds(pl.multiple_of(k * SUBLANES, SUBLANES), SUBLANES)
                x_re = xs_scr[s, rows, 0:half]
                x_im = xs_scr[s, rows, half:]
                xs_scr[s, rows, 0:half] = s_re
                xs_scr[s, rows, half:] = s_im
                n_re = a_re * s_re - a_im * s_im + x_re
                n_im = a_re * s_im + a_im * s_re + x_im
                return n_re, n_im

            s_re, s_im = lax.fori_loop(0, ROWS_Z // SUBLANES, scan_body,
                                       (s_scr[s, :, 0:half], s_scr[s, :, half:]), unroll=True)
            s_scr[s, :, 0:half] = s_re
            s_scr[s, :, half:] = s_im

        for s in range(N_SLAB):
            lanes = slice(s * LANES, (s + 1) * LANES)
            y_tot = yi_scr[s] + jnp.dot(xs_scr[s].astype(BF16), r_ref[s], preferred_element_type=F32)
            for j in range(Q):
                y_scr[:, j, :, lanes] = y_tot[:, j * LANES:(j + 1) * LANES].reshape(
                    ROWS_Z // SUBLANES, SUBLANES, LANES)

    def phase_c1(r, carry):
        rows = sub_rows(r)
        crow = pl.ds(r * CHUNK_ROWS, CHUNK_ROWS)
        y = y_scr[crow].reshape(SB, D_SSM) + d_ref[...] * u_scr[crow].reshape(SB, D_SSM)
        act_scr[rows, 0:D_SSM] = jax.nn.gelu(y).astype(BF16)
        for lc in range(N_LC):
            lanes = slice(lc * LANES, (lc + 1) * LANES)

            def conv_piece(rc, c, lc=lc, lanes=lanes):
                r0 = r * SB + rc * CONV_ROWS
                piece = jnp.broadcast_to(dwb_ref[:, lanes], (CONV_ROWS, LANES))
                for j in range(CONV_WIDTH):
                    zrows = pl.ds(pl.multiple_of(r0 + j * BATCH, SUBLANES), CONV_ROWS)
                    piece = piece + dw_ref[j:j + 1, lanes] * z_scr[lc, zrows, :]
                conv_scr[pl.ds(pl.multiple_of(rc * CONV_ROWS, CONV_ROWS), CONV_ROWS), lanes] = piece
                return c

            lax.fori_loop(0, SB // CONV_ROWS, conv_piece, 0, unroll=4)
        acc = conv_scr[...]
        mu = jnp.mean(acc, axis=-1, keepdims=True)
        cen = acc - mu
        var = jnp.mean(cen * cen, axis=-1, keepdims=True)
        ln = cen * lax.rsqrt(var + EPS) * lng_ref[...] + lnb_ref[...]
        act_scr[rows, D_SSM:] = jax.nn.silu(ln).astype(BF16)
        return carry

    lane = lax.broadcasted_iota(I32, (1, LANES), 1).astype(F32)
    grp_mask = lane < float(N_GROUPS_MOE)
    exp_lane = (lane >= float(LANE_EXP0)) & (lane < float(LANE_EXP0 + N_EXPERTS))
    lane_grp = jnp.floor((lane - float(LANE_EXP0)) * (1.0 / EXPERTS_PER_GROUP))
    tri = (lax.broadcasted_iota(I32, (SB, SB), 0) > lax.broadcasted_iota(I32, (SB, SB), 1)).astype(BF16)
    neg_inf = float("-inf")
    big = float(4 * LANES)

    def phase_c3(r, carry):
        rows = sub_rows(r)
        h = hb_scr[rows, :]
        g0 = D_SSM + 2 * D_CONV
        gate_ssm = jnp.dot(h, win_ref[:, g0:g0 + D_MODEL], preferred_element_type=F32) \
            + bgate_ref[:, 0:D_MODEL]
        gate_conv = jnp.dot(h, win_ref[:, g0 + D_MODEL:], preferred_element_type=F32) \
            + bgate_ref[:, D_MODEL:]
        zz = jnp.dot(actb_scr[rows, 0:D_SSM], wglu_ref[...], preferred_element_type=F32)
        y_ssm = zz[:, 0:D_MODEL] * jax.nn.sigmoid(zz[:, D_MODEL:])
        y_conv = jnp.dot(actb_scr[rows, D_SSM:], wco_ref[...], preferred_element_type=F32)

        merged = jax.nn.sigmoid(gate_ssm) * y_ssm + jax.nn.sigmoid(gate_conv) * y_conv
        xb = x_ref[pl.ds(r * BPS, BPS)].reshape(SB, D_MODEL)
        x1 = xb + jnp.dot(merged.astype(BF16), wout_ref[...], preferred_element_type=F32)
        x1_ref[pl.ds(r * BPS, BPS)] = x1.reshape(BPS, TT, D_MODEL)

        h2 = _rms(x1, gmoe_ref[...])
        h2p_ref[rows] = _pack_bf16_pair(h2[:, 0:HALF], h2[:, HALF:]).reshape((SB,) + ROW_TILE)

        h2_hi = h2.astype(BF16)
        h2_lo = (h2 - h2_hi.astype(F32)).astype(BF16)
        l1 = jnp.dot(h2_hi, wr1_ref[...], preferred_element_type=F32)
        l2 = jnp.dot(h2_lo, wr2_ref[...], preferred_element_type=F32)
        logit_scr[rows, :] = l1[:, 0:LANES] + l1[:, LANES:] + l2 + br_ref[...]
        return carry

    def route_previous():
        rows = sub_rows(0)
        logits = logit_scr[...]
        counted = jnp.where(step > 0, 1.0, 0.0)

        lg = jnp.where(grp_mask, logits, neg_inf)
        g_max = jnp.max(lg, axis=-1, keepdims=True)
        g_sel = jnp.min(jnp.where(lg == g_max, lane, big), axis=-1, keepdims=True)
        p_g = 1.0 / jnp.sum(jnp.where(grp_mask, jnp.exp(logits - g_max), 0.0), axis=-1, keepdims=True)
        le = jnp.where(exp_lane & (lane_grp == g_sel), logits, neg_inf)
        m1 = jnp.max(le, axis=-1, keepdims=True)
        i1 = jnp.min(jnp.where(le == m1, lane, big), axis=-1, keepdims=True)
        le2 = jnp.where(lane == i1, neg_inf, le)
        m2 = jnp.max(le2, axis=-1, keepdims=True)
        i2 = jnp.min(jnp.where(le2 == m2, lane, big), axis=-1, keepdims=True)
        e2 = jnp.exp(m2 - m1)
        den = 1.0 + e2
        w_a = (1.0 / den) * p_g
        w_b = (e2 / den) * p_g

        sel1 = lane == i1
        sel2 = lane == i2
        onehot = jnp.where(sel1 | sel2, counted, 0.0)
        prefix = jnp.dot(tri, onehot.astype(BF16), preferred_element_type=F32) + cnt_scr[...]
        rank_a = jnp.sum(jnp.where(sel1, prefix, 0.0), axis=-1, keepdims=True)
        rank_b = jnp.sum(jnp.where(sel2, prefix, 0.0), axis=-1, keepdims=True)
        cnt_scr[...] = cnt_scr[...] + jnp.sum(onehot, axis=0, keepdims=True)

        rec = jnp.where(lane == float(REC_EID0), i1 - float(LANE_EXP0), 0.0)
        rec = jnp.where(lane == float(REC_EID1), i2 - float(LANE_EXP0), rec)
        rec = jnp.where(lane == float(REC_W0), w_a, rec)
        rec = jnp.where(lane == float(REC_W1), w_b, rec)
        rec = jnp.where(lane == float(REC_RANK0), rank_a, rec)
        rec = jnp.where(lane == float(REC_RANK1), rank_b, rec)
        rec_ref[rows, :] = rec
        rect_ref[...] = jnp.transpose(rec)[0:REC_ROWS, :]
        cnt_ref[...] = cnt_scr[...]

    @pl.when(step < N_STEP)
    def _tile():
        route_previous()
        phase_a(0, 0)
        ht_scr[...] = jnp.swapaxes(hb_scr[...].reshape(BATCH, TT, D_MODEL), 0, 1).reshape(TM, D_MODEL)
        phase_a3(0, 0)
        phase_b()
        phase_c1(0, 0)
        z_scr[:, 0:HALO, :] = z_scr[:, TM:TM + HALO, :]
        actb_scr[...] = jnp.swapaxes(act_scr[...].reshape(TT, BATCH, D_SSM + D_CONV), 0, 1).reshape(
            TM, D_SSM + D_CONV)
        phase_c3(0, 0)

    @pl.when(step == N_STEP)
    def _last():
        route_previous()


def _mixer(x, gmix, win, bgate, mp, rmat, a_re, a_im, dvec, wglu, dw, dwb, lng, lnb, wco,
           wout, gmoe, wr1, wr2, br):
    tile = lambda i: jnp.minimum(i, N_STEP - 1)
    routed = lambda i: jnp.maximum(i - 1, 0)
    seq_spec = pl.BlockSpec((BATCH, TT, D_MODEL), lambda i: (0, tile(i), 0))
    in_hbm = pl.BlockSpec(memory_space=pl.ANY)
    in_specs = [
        seq_spec,
        _const_spec((1, D_MODEL)),
        in_hbm,
        _const_spec((1, 2 * D_MODEL)),
        _const_spec(mp.shape),
        _const_spec(rmat.shape),
        _const_spec(a_re.shape),
        _const_spec(a_im.shape),
        _const_spec((1, D_SSM)),
        in_hbm,
        _const_spec((CONV_WIDTH, D_CONV)),
        _const_spec((1, D_CONV)),
        _const_spec((1, D_CONV)),
        _const_spec((1, D_CONV)),
        in_hbm,
        in_hbm,
        _const_spec((1, D_MODEL)),
        _const_spec((D_MODEL, 2 * LANES)),
        _const_spec((D_MODEL, LANES)),
        _const_spec((1, LANES)),
    ]
    out_specs = [
        seq_spec,
        pl.BlockSpec((TM,) + ROW_TILE, lambda i: (tile(i), 0, 0)),
        pl.BlockSpec((TM, LANES), lambda i: (routed(i), 0)),
        pl.BlockSpec((REC_ROWS, TM), lambda i: (0, routed(i))),
        pl.BlockSpec((1, LANES), lambda i: (0, 0)),
    ]
    out_shape = [
        jax.ShapeDtypeStruct((BATCH, SEQ, D_MODEL), F32),
        jax.ShapeDtypeStruct((N_TOK,) + ROW_TILE, U32),
        jax.ShapeDtypeStruct((N_TOK, LANES), F32),
        jax.ShapeDtypeStruct((REC_ROWS, N_TOK), F32),
        jax.ShapeDtypeStruct((1, LANES), F32),
    ]
    chunk_shape = (ROWS_Z // SUBLANES, Q, SUBLANES, D_SSM)
    scratch = [
        pltpu.VMEM((TM, D_MODEL), BF16),
        pltpu.VMEM((TM, D_MODEL), BF16),
        pltpu.VMEM(chunk_shape, F32),
        pltpu.VMEM(chunk_shape, F32),
        pltpu.VMEM((N_SLAB, ROWS_Z, Q * LANES), F32),
        pltpu.VMEM((N_SLAB, ROWS_Z, STATE_LANES), F32),
        pltpu.VMEM((N_LC, HALO + TM, LANES), F32),
        pltpu.VMEM((SB, D_CONV), F32),
        pltpu.VMEM((TM, D_SSM + D_CONV), BF16),
        pltpu.VMEM((TM, D_SSM + D_CONV), BF16),
        pltpu.VMEM((TM, LANES), F32),
        pltpu.VMEM((N_SLAB, SUBLANES, STATE_LANES), F32),
        pltpu.VMEM((1, LANES), F32),
        pltpu.VMEM(win.shape, BF16),
        pltpu.VMEM(wglu.shape, BF16),
        pltpu.VMEM(wco.shape, BF16),
        pltpu.VMEM(wout.shape, BF16),
        pltpu.VMEM((W_STAGE_SLOTS, W_STAGE_ROWS, D_IN), F32),
        pltpu.SemaphoreType.DMA((W_STAGE_SLOTS,)),
    ]
    assert win.shape == (D_MODEL, D_IN) and wglu.shape == (D_SSM, 2 * D_MODEL)
    assert wco.shape == (D_CONV, D_MODEL) and wout.shape == (D_MODEL, D_MODEL)
    return pl.pallas_call(
        _mixer_kernel,
        grid=(N_STEP + 1,),
        in_specs=in_specs,
        out_specs=out_specs,
        out_shape=out_shape,
        scratch_shapes=scratch,
        compiler_params=pltpu.CompilerParams(
            dimension_semantics=("arbitrary",), vmem_limit_bytes=VMEM_LIMIT),
        name="mixer",
    )(x, gmix, win, bgate, mp, rmat, a_re, a_im, dvec, wglu, dw, dwb, lng, lnb, wco, wout,
      gmoe, wr1, wr2, br)


def _cmul(a, b):
    return a[0] * b[0] - a[1] * b[1], a[0] * b[1] + a[1] * b[0]


def _ssm_matrices(a_re, a_im, log_dt, b_re, b_im, c_re, c_im):
    dt = jnp.exp(log_dt)[:, None]
    mag = jnp.exp(a_re * dt)
    lam = (mag * jnp.cos(a_im * dt), mag * jnp.sin(a_im * dt))
    den = a_re * a_re + a_im * a_im
    nr = lam[0] - 1.0
    ni = lam[1]
    z_re = (nr * a_re + ni * a_im) / den
    z_im = (ni * a_re - nr * a_im) / den
    bbar = (z_re[..., None] * b_re - z_im[..., None] * b_im,
            z_re[..., None] * b_im + z_im[..., None] * b_re)
    pw = [(jnp.ones_like(lam[0]), jnp.zeros_like(lam[0])), lam]
    for _ in range(2, Q + 1):
        pw.append(_cmul(pw[-1], lam))
    e = [(c_re * p[0][:, None, :] - c_im * p[1][:, None, :],
          c_re * p[1][:, None, :] + c_im * p[0][:, None, :]) for p in pw]
    hp = lax.Precision.HIGHEST
    k = [jnp.einsum('gcn,gnd->gcd', e[m][0], bbar[0], precision=hp)
         - jnp.einsum('gcn,gnd->gcd', e[m][1], bbar[1], precision=hp) for m in range(Q)]
    eye = jnp.eye(GROUPS_PER_SLAB, dtype=F32)
    split = lambda t: t.reshape((N_SLAB, GROUPS_PER_SLAB) + t.shape[1:])
    zero_k = jnp.zeros_like(k[0])
    kb = jnp.stack([jnp.stack([split(jnp.swapaxes(k[j - i] if j >= i else zero_k, 1, 2))
                               for j in range(Q)]) for i in range(Q)])
    m_mat = jnp.einsum('ijsgdc,gh->sigdjhc', kb, eye).reshape(N_SLAB, Q * LANES, Q * LANES)
    f = [_cmul((pw[Q - 1 - i][0][..., None], pw[Q - 1 - i][1][..., None]), bbar) for i in range(Q)]
    p_parts = []
    for part in range(2):
        fs = jnp.stack([split(f[i][part]) for i in range(Q)])
        p_parts.append(jnp.einsum('isgnd,gh->sigdhn', fs, eye).reshape(N_SLAB, Q * LANES, STATE_LANES // 2))
    p_mat = jnp.concatenate(p_parts, axis=-1)
    r_parts = []
    for part, sign in ((0, 1.0), (1, -1.0)):
        es = jnp.stack([split(e[j + 1][part]) for j in range(Q)])
        r_parts.append(sign * jnp.einsum('jsgcn,gh->shnjgc', es, eye).reshape(
            N_SLAB, STATE_LANES // 2, Q * LANES))
    r_mat = jnp.concatenate(r_parts, axis=1)
    mp = jnp.concatenate([m_mat, p_mat], axis=-1).astype(BF16)
    a_q = pw[Q]
    return (mp, r_mat.astype(BF16),
            a_q[0].reshape(N_SLAB, STATE_LANES // 2), a_q[1].reshape(N_SLAB, STATE_LANES // 2))


def _router_weights(w_rg, b_rg, w_re, b_re):
    pad_g = LANE_EXP0 - LANE_GRP0 - N_GROUPS_MOE
    pad_e = LANES - LANE_EXP0 - N_EXPERTS
    w = jnp.concatenate([w_rg, jnp.zeros((D_MODEL, pad_g), F32), w_re, jnp.zeros((D_MODEL, pad_e), F32)], axis=1)
    b = jnp.concatenate([b_rg, jnp.zeros((pad_g,), F32), b_re, jnp.zeros((pad_e,), F32)]).reshape(1, LANES)
    w_hi = w.astype(BF16)
    w_lo = (w - w_hi.astype(F32)).astype(BF16)
    return jnp.concatenate([w_hi, w_lo], axis=1), w_hi, b


def _sc_mesh():
    return plsc.VectorSubcoreMesh(core_axis_name="core", subcore_axis_name="subcore")


def _sc_worker(mesh):
    return lax.axis_index("core") * mesh.num_subcores + lax.axis_index("subcore")


def _dispatch(h2p, dest):
    mesh = _sc_mesh()
    n_win = N_TOK // SC_WINDOW
    per_worker = n_win // (mesh.num_cores * mesh.num_subcores)
    assert per_worker * mesh.num_cores * mesh.num_subcores == n_win

    @pl.kernel(out_type=jax.ShapeDtypeStruct((N_ROWS,) + ROW_TILE, U32), mesh=mesh,
               scratch_types=[pltpu.VMEM((SC_WINDOW,), I32), pltpu.VMEM((SC_WINDOW,) + ROW_TILE, U32)])
    def scatter_rows(h_hbm, dest_hbm, xs_hbm, idx_v, rows_v):
        first = _sc_worker(mesh) * per_worker

        @pl.loop(0, per_worker)
        def _(w):
            win = first + w
            pltpu.sync_copy(h_hbm.at[pl.ds(win * SC_WINDOW, SC_WINDOW)], rows_v)
            for j in range(TOPK):
                pltpu.sync_copy(dest_hbm.at[j, win], idx_v)
                pltpu.sync_copy(rows_v, xs_hbm.at[idx_v])

    return scatter_rows(h2p, dest)


def _collect(ys, dest):
    mesh = _sc_mesh()
    n_tok = dest.shape[1]
    n_win = TOPK * n_tok // SC_WINDOW
    per_worker = n_win // (mesh.num_cores * mesh.num_subcores)
    assert per_worker * mesh.num_cores * mesh.num_subcores == n_win

    @pl.kernel(out_type=jax.ShapeDtypeStruct((TOPK * n_tok,) + ROW_TILE, U32), mesh=mesh,
               scratch_types=[pltpu.VMEM((SC_WINDOW,), I32), pltpu.VMEM((SC_WINDOW,) + ROW_TILE, U32)])
    def gather_rows(ys_hbm, dest_hbm, yg_hbm, idx_v, rows_v):
        first = _sc_worker(mesh) * per_worker

        @pl.loop(0, per_worker)
        def _(w):
            win = first + w
            pltpu.sync_copy(dest_hbm.at[win], idx_v)
            pltpu.sync_copy(ys_hbm.at[idx_v], rows_v)
            pltpu.sync_copy(rows_v, yg_hbm.at[pl.ds(win * SC_WINDOW, SC_WINDOW)])

    return gather_rows(ys, dest.reshape(n_win, SC_WINDOW)).reshape((TOPK, n_tok) + ROW_TILE)


def _expert_kernel(first_ref, nblk_ref, nvalid_ref, nused_ref, xs_hbm, wg_ref, wu_ref, wd_ref, ys_hbm,
                   wg_scr, wu_scr, wd_scr, x_buf, y_buf, in_sem, out_sem):
    e = pl.program_id(0)
    nused = nused_ref[0]

    def in_copy(g):
        slot = lax.rem(g, IN_SLOTS)
        return pltpu.make_async_copy(xs_hbm.at[pl.ds(g * BM, BM)], x_buf.at[slot], in_sem.at[slot])

    def out_copy(g, slot):
        return pltpu.make_async_copy(y_buf.at[slot], ys_hbm.at[pl.ds(g * BM, BM)], out_sem.at[slot])

    @pl.when(e == 0)
    def _first():
        for g in range(IN_AHEAD):
            in_copy(g).start()

    wg_scr[...] = wg_ref[0].astype(BF16)
    wu_scr[...] = wu_ref[0].astype(BF16)
    wd_scr[...] = wd_ref[0].astype(BF16)

    def block(b, carry):
        g = first_ref[e] + b
        slot = lax.rem(g, 2)
        in_copy(g).wait()

        @pl.when(g + IN_AHEAD < nused)
        def _prefetch():
            in_copy(g + IN_AHEAD).start()

        @pl.when(g >= 2)
        def _slot_free():
            out_copy(g - 2, slot).wait()

        valid = lax.broadcasted_iota(I32, (BM, 1), 0) < nvalid_ref[g]
        x_blk = x_buf[lax.rem(g, IN_SLOTS)].reshape(BM, HALF)
        lo, hi = _unpack_bf16_pair(jnp.where(valid, x_blk, jnp.uint32(0)))
        lo = lo.astype(BF16)
        hi = hi.astype(BF16)
        gate = jnp.dot(lo, wg_scr[0:HALF, :], preferred_element_type=F32) \
            + jnp.dot(hi, wg_scr[HALF:, :], preferred_element_type=F32)
        up = jnp.dot(lo, wu_scr[0:HALF, :], preferred_element_type=F32) \
            + jnp.dot(hi, wu_scr[HALF:, :], preferred_element_type=F32)
        act = (jax.nn.silu(gate) * up).astype(BF16)
        o = jnp.dot(act, wd_scr[...], preferred_element_type=F32)
        y_buf[slot] = _pack_bf16_pair(o[:, 0:HALF], o[:, HALF:]).reshape((BM,) + ROW_TILE)
        out_copy(g, slot).start()
        return carry

    lax.fori_loop(0, nblk_ref[e], block, 0)

    @pl.when(e == N_EXPERTS - 1)
    def _drain():
        out_copy(nused - 2, lax.rem(nused, 2)).wait()
        out_copy(nused - 1, 1 - lax.rem(nused, 2)).wait()


def _experts(first, nblk, nvalid, nused, xs, wg, wu, wd):
    grid_spec = pltpu.PrefetchScalarGridSpec(
        num_scalar_prefetch=4,
        grid=(N_EXPERTS,),
        in_specs=[
            pl.BlockSpec(memory_space=pl.ANY),
            pl.BlockSpec((1, D_MODEL, D_EXPERT), lambda e, *_: (e, 0, 0)),
            pl.BlockSpec((1, D_MODEL, D_EXPERT), lambda e, *_: (e, 0, 0)),
            pl.BlockSpec((1, D_EXPERT, D_MODEL), lambda e, *_: (e, 0, 0)),
        ],
        out_specs=pl.BlockSpec(memory_space=pl.ANY),
        scratch_shapes=[
            pltpu.VMEM((D_MODEL, D_EXPERT), BF16),
            pltpu.VMEM((D_MODEL, D_EXPERT), BF16),
            pltpu.VMEM((D_EXPERT, D_MODEL), BF16),
            pltpu.VMEM((IN_SLOTS, BM) + ROW_TILE, U32),
            pltpu.VMEM((2, BM) + ROW_TILE, U32),
            pltpu.SemaphoreType.DMA((IN_SLOTS,)),
            pltpu.SemaphoreType.DMA((2,)),
        ],
    )
    return pl.pallas_call(
        _expert_kernel,
        grid_spec=grid_spec,
        out_shape=jax.ShapeDtypeStruct((N_ROWS,) + ROW_TILE, U32),
        compiler_params=pltpu.CompilerParams(
            dimension_semantics=("arbitrary",), vmem_limit_bytes=VMEM_LIMIT),
        name="experts",
    )(first, nblk, nvalid, nused, xs, wg, wu, wd)


def _combine_kernel(x1_ref, rec_ref, yg_ref, p_ref, gple_ref, wpg_ref, wple_ref, gfin_ref, *rest):
    out_ref = rest[-1]
    ple = jnp.dot(p_ref[0].reshape(TM, D_PLE).astype(BF16), wple_ref[...], preferred_element_type=F32)
    rec = rec_ref[...]
    w0 = rec[:, REC_W0:REC_W0 + 1]
    w1 = rec[:, REC_W1:REC_W1 + 1]
    lo0, hi0 = _unpack_bf16_pair(yg_ref[0].reshape(TM, HALF))
    lo1, hi1 = _unpack_bf16_pair(yg_ref[1].reshape(TM, HALF))
    moe = jnp.concatenate([lo0 * w0 + lo1 * w1, hi0 * w0 + hi1 * w1], axis=1)
    x2 = x1_ref[...].reshape(TM, D_MODEL) + moe
    gate = jax.nn.sigmoid(jnp.dot(_rms(x2, gple_ref[...]).astype(BF16), wpg_ref[...],
                                  preferred_element_type=F32))
    x3 = x2 + gate * ple
    out_ref[...] = _rms(x3, gfin_ref[...]).reshape(BATCH, TT, D_MODEL)


def _combine(s0, n_steps, x1, rec, yg, p, gple, wpg, wple, gfin, out_prev=None):
    seq_spec = pl.BlockSpec((BATCH, TT, D_MODEL), lambda i: (0, s0 + i, 0))
    in_specs = [
        seq_spec,
        pl.BlockSpec((TM, LANES), lambda i: (s0 + i, 0)),
        pl.BlockSpec((TOPK, TM) + ROW_TILE, lambda i: (0, i, 0, 0)),
        pl.BlockSpec((1, BATCH, TT, D_PLE), lambda i: (0, 0, s0 + i, 0)),
        _const_spec((1, D_MODEL)),
        _const_spec((D_MODEL, D_MODEL)),
        _const_spec((D_PLE, D_MODEL)),
        _const_spec((1, D_MODEL)),
    ]
    args = [x1, rec, yg, p, gple, wpg, wple, gfin]
    aliases = {}
    if out_prev is not None:
        in_specs.append(pl.BlockSpec(memory_space=pl.ANY))
        args.append(out_prev)
        aliases = {len(args) - 1: 0}
    return pl.pallas_call(
        _combine_kernel,
        grid=(n_steps,),
        in_specs=in_specs,
        out_specs=seq_spec,
        out_shape=jax.ShapeDtypeStruct((BATCH, SEQ, D_MODEL), F32),
        input_output_aliases=aliases,
        compiler_params=pltpu.CompilerParams(
            dimension_semantics=("arbitrary",), vmem_limit_bytes=VMEM_LIMIT),
        name="combine",
    )(*args)


def kernel(x, p, g_mix, w_in, b_gate, ssm_a_re, ssm_a_im, ssm_log_dt, ssm_b_re, ssm_b_im, ssm_c_re,
           ssm_c_im, ssm_d, w_glu, conv_dw, conv_dw_b, conv_ln_g, conv_ln_b, w_conv_out, w_out, g_moe,
           w_router_group, b_router_group, w_router_expert, b_router_expert, w_exp_gate, w_exp_up,
           w_exp_down, g_ple, w_ple_gate, w_ple, g_final):
    assert x.shape == (BATCH, SEQ, D_MODEL) and p.shape == (1, BATCH, SEQ, D_PLE)
    row = lambda v: v.reshape(1, -1)

    mp, rmat, a_re, a_im = _ssm_matrices(ssm_a_re[0], ssm_a_im[0], ssm_log_dt[0], ssm_b_re[0],
                                         ssm_b_im[0], ssm_c_re[0], ssm_c_im[0])
    wr1, wr2, br = _router_weights(w_router_group[0], b_router_group[0], w_router_expert[0],
                                   b_router_expert[0])
    x1, h2p, rec, rect, cnt = _mixer(
        x, row(g_mix[0]), w_in[0], row(b_gate[0]), mp, rmat, a_re, a_im,
        row(ssm_d[0]), w_glu[0], conv_dw[0], row(conv_dw_b[0]), row(conv_ln_g[0]),
        row(conv_ln_b[0]), w_conv_out[0], w_out[0], row(g_moe[0]), wr1, wr2, br)

    counts = cnt[0, LANE_EXP0:LANE_EXP0 + N_EXPERTS].astype(I32)
    pcounts = (counts + BM - 1) // BM * BM
    pends = jnp.cumsum(pcounts)
    pstarts = pends - pcounts
    eid = rect[REC_EID0:REC_EID1 + 1].astype(I32)
    rank = rect[REC_RANK0:REC_RANK1 + 1].astype(I32)
    dest = (jnp.sum(jnp.where(eid[..., None] == jnp.arange(N_EXPERTS, dtype=I32), pstarts, 0), axis=-1)
            + rank).reshape(TOPK, N_TOK // SC_WINDOW, SC_WINDOW)
    nused = (pends[-1] // BM).astype(I32)
    blk = jnp.arange(N_BLK, dtype=I32)[:, None] * BM
    in_expert = (pstarts[None, :] <= blk) & (blk < pends[None, :])
    nvalid = jnp.clip(jnp.sum(jnp.where(in_expert, (pstarts + counts)[None, :] - blk, 0), axis=1), 0, BM)

    xs = _dispatch(h2p, dest)
    ys = _experts(pstarts // BM, pcounts // BM, nvalid.astype(I32), nused.reshape(1), xs,
                  w_exp_gate[0], w_exp_up[0], w_exp_down[0])
    dest_tok = dest.reshape(TOPK, N_TOK)
    wpg = w_ple_gate[0].astype(BF16)
    wple = w_ple[0].astype(BF16)
    out = None
    s0 = 0
    for n_steps in PART_STEPS:
        yg = _collect(ys, dest_tok[:, s0 * TM:(s0 + n_steps) * TM])
        out = _combine(s0, n_steps, x1, rec, yg, p, row(g_ple[0]), wpg, wple, row(g_final), out)
        s0 += n_steps
    return out
```

```python
import jax
import jax.numpy as jnp
from jax import lax
from jax.experimental import pallas as pl
from jax.experimental.pallas import tpu as pltpu
from jax.experimental.pallas import tpu_sc as plsc

F32 = jnp.float32
BF16 = jnp.bfloat16
U32 = jnp.uint32
I32 = jnp.int32

D_MODEL = 1024
BATCH = 8
SEQ = 2048
N_TOK = BATCH * SEQ
D_SSM = 512
SSM_GROUP_WIDTH = 16
SSM_GROUPS = 32
SSM_STATE = 64
D_CONV = 512
CONV_WIDTH = 31
D_IN = D_SSM + 2 * D_CONV + 2 * D_MODEL
N_GROUPS_MOE = 4
EXPERTS_PER_GROUP = 8
N_EXPERTS = 32
TOPK = 2
D_EXPERT = 512
D_PLE = 256
EPS = 1e-6

SUBLANES = 8
LANES = 128
assert BATCH == SUBLANES

TT = 64
TM = TT * BATCH
N_STEP = SEQ // TT
SB = 512
NSB = TM // SB
BPS = SB // TT
Q = 2
N_SLAB = D_SSM // LANES
GROUPS_PER_SLAB = SSM_GROUPS // N_SLAB
ROWS_Z = TM // Q
STATE_LANES = 2 * GROUPS_PER_SLAB * SSM_STATE
HALO = (CONV_WIDTH - 1) * BATCH
CHUNK_ROWS = SB // (Q * SUBLANES)
W_STAGE_ROWS = 64
W_STAGE_SLOTS = 4
CONV_ROWS = 64
N_LC = D_CONV // LANES

LANE_GRP0 = 0
LANE_EXP0 = 32
REC_EID0, REC_EID1, REC_W0, REC_W1, REC_RANK0, REC_RANK1 = 0, 1, 2, 3, 4, 5
REC_ROWS = 8

BM = 256
N_BLK = (TOPK * N_TOK + N_EXPERTS * (BM - 1) + BM - 1) // BM
N_ROWS = N_BLK * BM
HALF = D_MODEL // 2
ROW_TILE = (HALF // LANES, LANES)
SC_WINDOW = 128
IN_AHEAD = 3
IN_SLOTS = IN_AHEAD + 1
PART_STEPS = (12, 20)
assert sum(PART_STEPS) == N_STEP

VMEM_LIMIT = 56 * 1024 * 1024


def _const_spec(shape):
    n = len(shape)
    return pl.BlockSpec(shape, lambda *_: (0,) * n, pipeline_mode=pl.Buffered(1))


def _rms(x, g):
    ms = jnp.mean(x * x, axis=-1, keepdims=True)
    return x * lax.rsqrt(ms + EPS) * g


def _pack_bf16_pair(lo, hi):
    ulo = lax.bitcast_convert_type(lo.astype(BF16).astype(F32), U32)
    uhi = lax.bitcast_convert_type(hi.astype(BF16).astype(F32), U32)
    return (ulo >> 16) | (uhi & jnp.uint32(0xFFFF0000))


def _unpack_bf16_pair(w):
    lo = lax.bitcast_convert_type(w << 16, F32)
    hi = lax.bitcast_convert_type(w & jnp.uint32(0xFFFF0000), F32)
    return lo, hi


def _load_weights_bf16(pairs, stage, sem):
    chunks = [(src, dst, r0) for src, dst in pairs for r0 in range(0, src.shape[0], W_STAGE_ROWS)]

    def copy(c):
        src, _, r0 = chunks[c]
        slot = c % W_STAGE_SLOTS
        return pltpu.make_async_copy(src.at[pl.ds(r0, W_STAGE_ROWS)],
                                     stage.at[slot, :, 0:src.shape[1]], sem.at[slot])

    for c in range(W_STAGE_SLOTS):
        copy(c).start()
    for c, (src, dst, r0) in enumerate(chunks):
        copy(c).wait()
        dst[r0:r0 + W_STAGE_ROWS, :] = stage[c % W_STAGE_SLOTS, :, 0:src.shape[1]].astype(BF16)
        if c + W_STAGE_SLOTS < len(chunks):
            copy(c + W_STAGE_SLOTS).start()


def _expand_ssm(um_ref, up_ref, ur_ref, mp_ref, r_ref):
    gw, ns, half = SSM_GROUP_WIDTH, SSM_STATE, STATE_LANES // 2
    div = lambda a, n: lax.shift_right_logical(a, n.bit_length() - 1)
    mod = lambda a, n: a & (n - 1)
    iota2 = lambda shape: (lax.broadcasted_iota(I32, shape, 0), lax.broadcasted_iota(I32, shape, 1))
    one = lambda cond: jnp.where(cond, 1.0, 0.0).astype(BF16)

    r, q = iota2((Q * gw, Q * LANES))
    x_m = one((div(r, gw) == div(q, LANES)) & (mod(r, gw) == mod(q, gw)))
    r, q = iota2((2 * ns, STATE_LANES))
    x_p = one((div(r, ns) == div(q, half)) & (mod(r, ns) == mod(q, ns)))
    p, r = iota2((STATE_LANES, 2 * ns))
    x_r = one((div(p, half) == div(r, ns)) & (mod(p, ns) == mod(r, ns)))
    p, q = iota2((Q * LANES, Q * LANES))
    same_m = div(mod(p, LANES), gw) == div(mod(q, LANES), gw)
    p, q = iota2((Q * LANES, STATE_LANES))
    same_p = div(mod(p, LANES), gw) == div(mod(q, half), ns)
    p, q = iota2((STATE_LANES, Q * LANES))
    same_r = div(mod(p, half), ns) == div(mod(q, LANES), gw)
    for s in range(N_SLAB):
        m = jnp.dot(um_ref[s], x_m, preferred_element_type=F32)
        mp_ref[s, :, 0:Q * LANES] = jnp.where(same_m, m, 0.0).astype(BF16)
        pm = jnp.dot(up_ref[s], x_p, preferred_element_type=F32)
        mp_ref[s, :, Q * LANES:] = jnp.where(same_p, pm, 0.0).astype(BF16)
        rm = jnp.dot(x_r, ur_ref[s], preferred_element_type=F32)
        r_ref[s] = jnp.where(same_r, rm, 0.0).astype(BF16)


def _mixer_kernel(x_ref, gmix_ref, win_hbm, bgate_ref, um_ref, up_ref, ur_ref, are_ref,
                  aim_ref, d_ref, wglu_hbm, dw_ref, dwb_ref, lng_ref, lnb_ref, wco_hbm, wout_hbm,
                  gmoe_ref, wr1_ref, wr2_ref, br_ref,
                  x1_ref, h2p_ref, rec_ref, rect_ref, cnt_ref,
                  hb_scr, ht_scr, u_scr, y_scr, yi_scr, xs_scr, z_scr, conv_scr, act_scr, actb_scr,
                  logit_scr, s_scr, cnt_scr, win_ref, wglu_ref, wco_ref, wout_ref, wstage_scr, wstage_sem,
                  mp_ref, r_ref):
    step = pl.program_id(0)
    assert NSB == 1

    @pl.when(step == 0)
    def _init():
        logit_scr[...] = jnp.zeros(logit_scr.shape, F32)
        z_scr[:, 0:HALO, :] = jnp.zeros((N_LC, HALO, LANES), F32)
        s_scr[...] = jnp.zeros(s_scr.shape, F32)
        cnt_scr[...] = jnp.zeros(cnt_scr.shape, F32)
        _expand_ssm(um_ref, up_ref, ur_ref, mp_ref, r_ref)
        _load_weights_bf16([(win_hbm, win_ref), (wglu_hbm, wglu_ref), (wco_hbm, wco_ref),
                            (wout_hbm, wout_ref)], wstage_scr, wstage_sem)

    def sub_rows(r):
        return pl.ds(pl.multiple_of(r * SB, SB), SB)

    def phase_a(r, carry):
        xb = x_ref[pl.ds(r * BPS, BPS)].reshape(SB, D_MODEL)
        hb_scr[sub_rows(r), :] = _rms(xb, gmix_ref[...]).astype(BF16)
        return carry

    def phase_a3(r, carry):
        h = ht_scr[sub_rows(r), :]
        u = jnp.dot(h, win_ref[:, 0:D_SSM], preferred_element_type=F32)
        u_scr[pl.ds(r * CHUNK_ROWS, CHUNK_ROWS)] = u.reshape(CHUNK_ROWS, Q, SUBLANES, D_SSM)
        v = jnp.dot(h, win_ref[:, D_SSM:D_SSM + 2 * D_CONV], preferred_element_type=F32)
        zc = v[:, 0:D_CONV] * jax.nn.sigmoid(v[:, D_CONV:])
        for lc in range(N_LC):
            z_scr[lc, pl.ds(pl.multiple_of(HALO + r * SB, SUBLANES), SB), :] = zc[:, lc * LANES:(lc + 1) * LANES]
        return carry

    def phase_b():
        for s in range(N_SLAB):
            lanes = slice(s * LANES, (s + 1) * LANES)
            z = jnp.concatenate(
                [u_scr[:, i, :, lanes].reshape(ROWS_Z, LANES) for i in range(Q)], axis=1).astype(BF16)
            xp = jnp.dot(z, mp_ref[s], preferred_element_type=F32)
            yi_scr[s] = xp[:, 0:Q * LANES]
            xs_scr[s] = xp[:, Q * LANES:]

        half = STATE_LANES // 2
        for s in range(N_SLAB):
            a_re = jnp.broadcast_to(are_ref[s:s + 1, :], (SUBLANES, half))
            a_im = jnp.broadcast_to(aim_ref[s:s + 1, :], (SUBLANES, half))

            def scan_body(k, carry, s=s, a_re=a_re, a_im=a_im):
                s_re, s_im = carry
                rows = pl.ds(pl.multiple_of(k * SUBLANES, SUBLANES), SUBLANES)
                x_re = xs_scr[s, rows, 0:half]
                x_im = xs_scr[s, rows, half:]
                xs_scr[s, rows, 0:half] = s_re
                xs_scr[s, rows, half:] = s_im
                n_re = a_re * s_re - a_im * s_im + x_re
                n_im = a_re * s_im + a_im * s_re + x_im
                return n_re, n_im

            s_re, s_im = lax.fori_loop(0, ROWS_Z // SUBLANES, scan_body,
                                       (s_scr[s, :, 0:half], s_scr[s, :, half:]), unroll=True)
            s_scr[s, :, 0:half] = s_re
            s_scr[s, :, half:] = s_im

        for s in range(N_SLAB):
            lanes = slice(s * LANES, (s + 1) * LANES)
            y_tot = yi_scr[s] + jnp.dot(xs_scr[s].astype(BF16), r_ref[s], preferred_element_type=F32)
            for j in range(Q):
                y_scr[:, j, :, lanes] = y_tot[:, j * LANES:(j + 1) * LANES].reshape(
                    ROWS_Z // SUBLANES, SUBLANES, LANES)

    def phase_c1(r, carry):
        rows = sub_rows(r)
        crow = pl.ds(r * CHUNK_ROWS, CHUNK_ROWS)
        y = y_scr[crow].reshape(SB, D_SSM) + d_ref[...] * u_scr[crow].reshape(SB, D_SSM)
        act_scr[rows, 0:D_SSM] = jax.nn.gelu(y).astype(BF16)
        for lc in range(N_LC):
            lanes = slice(lc * LANES, (lc + 1) * LANES)

            def conv_piece(rc, c, lc=lc, lanes=lanes):
                r0 = r * SB + rc * CONV_ROWS
                piece = jnp.broadcast_to(dwb_ref[:, lanes], (CONV_ROWS, LANES))
                for j in range(CONV_WIDTH):
                    zrows = pl.ds(pl.multiple_of(r0 + j * BATCH, SUBLANES), CONV_ROWS)
                    piece = piece + dw_ref[j:j + 1, lanes] * z_scr[lc, zrows, :]
                conv_scr[pl.ds(pl.multiple_of(rc * CONV_ROWS, CONV_ROWS), CONV_ROWS), lanes] = piece
                return c

            lax.fori_loop(0, SB // CONV_ROWS, conv_piece, 0, unroll=4)
        acc = conv_scr[...]
        mu = jnp.mean(acc, axis=-1, keepdims=True)
        cen = acc - mu
        var = jnp.mean(cen * cen, axis=-1, keepdims=True)
        ln = cen * lax.rsqrt(var + EPS) * lng_ref[...] + lnb_ref[...]
        act_scr[rows, D_SSM:] = jax.nn.silu(ln).astype(BF16)
        return carry

    lane = lax.broadcasted_iota(I32, (1, LANES), 1).astype(F32)
    grp_mask = lane < float(N_GROUPS_MOE)
    exp_lane = (lane >= float(LANE_EXP0)) & (lane < float(LANE_EXP0 + N_EXPERTS))
    lane_grp = jnp.floor((lane - float(LANE_EXP0)) * (1.0 / EXPERTS_PER_GROUP))
    tri = (lax.broadcasted_iota(I32, (SB, SB), 0) > lax.broadcasted_iota(I32, (SB, SB), 1)).astype(BF16)
    neg_inf = float("-inf")
    big = float(4 * LANES)

    def phase_c3(r, carry):
        rows = sub_rows(r)
        h = hb_scr[rows, :]
        g0 = D_SSM + 2 * D_CONV
        gate_ssm = jnp.dot(h, win_ref[:, g0:g0 + D_MODEL], preferred_element_type=F32) \
            + bgate_ref[:, 0:D_MODEL]
        gate_conv = jnp.dot(h, win_ref[:, g0 + D_MODEL:], preferred_element_type=F32) \
            + bgate_ref[:, D_MODEL:]
        zz = jnp.dot(actb_scr[rows, 0:D_SSM], wglu_ref[...], preferred_element_type=F32)
        y_ssm = zz[:, 0:D_MODEL] * jax.nn.sigmoid(zz[:, D_MODEL:])
        y_conv = jnp.dot(actb_scr[rows, D_SSM:], wco_ref[...], preferred_element_type=F32)

        merged = jax.nn.sigmoid(gate_ssm) * y_ssm + jax.nn.sigmoid(gate_conv) * y_conv
        xb = x_ref[pl.ds(r * BPS, BPS)].reshape(SB, D_MODEL)
        x1 = xb + jnp.dot(merged.astype(BF16), wout_ref[...], preferred_element_type=F32)
        x1_ref[pl.ds(r * BPS, BPS)] = x1.reshape(BPS, TT, D_MODEL)

        h2 = _rms(x1, gmoe_ref[...])
        h2p_ref[rows] = _pack_bf16_pair(h2[:, 0:HALF], h2[:, HALF:]).reshape((SB,) + ROW_TILE)

        h2_hi = h2.astype(BF16)
        h2_lo = (h2 - h2_hi.astype(F32)).astype(BF16)
        l1 = jnp.dot(h2_hi, wr1_ref[...], preferred_element_type=F32)
        l2 = jnp.dot(h2_lo, wr2_ref[...], preferred_element_type=F32)
        logit_scr[rows, :] = l1[:, 0:LANES] + l1[:, LANES:] + l2 + br_ref[...]
        return carry

    def route_previous():
        rows = sub_rows(0)
        logits = logit_scr[...]
        counted = jnp.where(step > 0, 1.0, 0.0)

        lg = jnp.where(grp_mask, logits, neg_inf)
        g_max = jnp.max(lg, axis=-1, keepdims=True)
        g_sel = jnp.min(jnp.where(lg == g_max, lane, big), axis=-1, keepdims=True)
        p_g = 1.0 / jnp.sum(jnp.where(grp_mask, jnp.exp(logits - g_max), 0.0), axis=-1, keepdims=True)
        le = jnp.where(exp_lane & (lane_grp == g_sel), logits, neg_inf)
        m1 = jnp.max(le, axis=-1, keepdims=True)
        i1 = jnp.min(jnp.where(le == m1, lane, big), axis=-1, keepdims=True)
        le2 = jnp.where(lane == i1, neg_inf, le)
        m2 = jnp.max(le2, axis=-1, keepdims=True)
        i2 = jnp.min(jnp.where(le2 == m2, lane, big), axis=-1, keepdims=True)
        e2 = jnp.exp(m2 - m1)
        den = 1.0 + e2
        w_a = (1.0 / den) * p_g
        w_b = (e2 / den) * p_g

        sel1 = lane == i1
        sel2 = lane == i2
        onehot = jnp.where(sel1 | sel2, counted, 0.0)
        prefix = jnp.dot(tri, onehot.astype(BF16), preferred_element_type=F32) + cnt_scr[...]
        rank_a = jnp.sum(jnp.where(sel1, prefix, 0.0), axis=-1, keepdims=True)
        rank_b = jnp.sum(jnp.where(sel2, prefix, 0.0), axis=-1, keepdims=True)
        cnt_scr[...] = cnt_scr[...] + jnp.sum(onehot, axis=0, keepdims=True)

        rec = jnp.where(lane == float(REC_EID0), i1 - float(LANE_EXP0), 0.0)
        rec = jnp.where(lane == float(REC_EID1), i2 - float(LANE_EXP0), rec)
        rec = jnp.where(lane == float(REC_W0), w_a, rec)
        rec = jnp.where(lane == float(REC_W1), w_b, rec)
        rec = jnp.where(lane == float(REC_RANK0), rank_a, rec)
        rec = jnp.where(lane == float(REC_RANK1), rank_b, rec)
        rec_ref[rows, :] = rec
        rect_ref[...] = jnp.transpose(rec)[0:REC_ROWS, :]
        cnt_ref[...] = cnt_scr[...]

    @pl.when(step < N_STEP)
    def _tile():
        route_previous()
        phase_a(0, 0)
        ht_scr[...] = jnp.swapaxes(hb_scr[...].reshape(BATCH, TT, D_MODEL), 0, 1).reshape(TM, D_MODEL)
        phase_a3(0, 0)
        phase_b()
        phase_c1(0, 0)
        z_scr[:, 0:HALO, :] = z_scr[:, TM:TM + HALO, :]
        actb_scr[...] = jnp.swapaxes(act_scr[...].reshape(TT, BATCH, D_SSM + D_CONV), 0, 1).reshape(
            TM, D_SSM + D_CONV)
        phase_c3(0, 0)

    @pl.when(step == N_STEP)
    def _last():
        route_previous()


def _mixer(x, gmix, win, bgate, um, up, ur, a_re, a_im, dvec, wglu, dw, dwb, lng, lnb, wco,
           wout, gmoe, wr1, wr2, br):
    tile = lambda i: jnp.minimum(i, N_STEP - 1)
    routed = lambda i: jnp.maximum(i - 1, 0)
    seq_spec = pl.BlockSpec((BATCH, TT, D_MODEL), lambda i: (0, tile(i), 0))
    in_hbm = pl.BlockSpec(memory_space=pl.ANY)
    in_specs = [
        seq_spec,
        _const_spec((1, D_MODEL)),
        in_hbm,
        _const_spec((1, 2 * D_MODEL)),
        _const_spec(um.shape),
        _const_spec(up.shape),
        _const_spec(ur.shape),
        _const_spec(a_re.shape),
        _const_spec(a_im.shape),
        _const_spec((1, D_SSM)),
        in_hbm,
        _const_spec((CONV_WIDTH, D_CONV)),
        _const_spec((1, D_CONV)),
        _const_spec((1, D_CONV)),
        _const_spec((1, D_CONV)),
        in_hbm,
        in_hbm,
        _const_spec((1, D_MODEL)),
        _const_spec((D_MODEL, 2 * LANES)),
        _const_spec((D_MODEL, LANES)),
        _const_spec((1, LANES)),
    ]
    out_specs = [
        seq_spec,
        pl.BlockSpec((TM,) + ROW_TILE, lambda i: (tile(i), 0, 0)),
        pl.BlockSpec((TM, LANES), lambda i: (routed(i), 0)),
        pl.BlockSpec((REC_ROWS, TM), lambda i: (0, routed(i))),
        pl.BlockSpec((1, LANES), lambda i: (0, 0)),
    ]
    out_shape = [
        jax.ShapeDtypeStruct((BATCH, SEQ, D_MODEL), F32),
        jax.ShapeDtypeStruct((N_TOK,) + ROW_TILE, U32),
        jax.ShapeDtypeStruct((N_TOK, LANES), F32),
        jax.ShapeDtypeStruct((REC_ROWS, N_TOK), F32),
        jax.ShapeDtypeStruct((1, LANES), F32),
    ]
    chunk_shape = (ROWS_Z // SUBLANES, Q, SUBLANES, D_SSM)
    scratch = [
        pltpu.VMEM((TM, D_MODEL), BF16),
        pltpu.VMEM((TM, D_MODEL), BF16),
        pltpu.VMEM(chunk_shape, F32),
        pltpu.VMEM(chunk_shape, F32),
        pltpu.VMEM((N_SLAB, ROWS_Z, Q * LANES), F32),
        pltpu.VMEM((N_SLAB, ROWS_Z, STATE_LANES), F32),
        pltpu.VMEM((N_LC, HALO + TM, LANES), F32),
        pltpu.VMEM((SB, D_CONV), F32),
        pltpu.VMEM((TM, D_SSM + D_CONV), BF16),
        pltpu.VMEM((TM, D_SSM + D_CONV), BF16),
        pltpu.VMEM((TM, LANES), F32),
        pltpu.VMEM((N_SLAB, SUBLANES, STATE_LANES), F32),
        pltpu.VMEM((1, LANES), F32),
        pltpu.VMEM(win.shape, BF16),
        pltpu.VMEM(wglu.shape, BF16),
        pltpu.VMEM(wco.shape, BF16),
        pltpu.VMEM(wout.shape, BF16),
        pltpu.VMEM((W_STAGE_SLOTS, W_STAGE_ROWS, D_IN), F32),
        pltpu.SemaphoreType.DMA((W_STAGE_SLOTS,)),
        pltpu.VMEM((N_SLAB, Q * LANES, Q * LANES + STATE_LANES), BF16),
        pltpu.VMEM((N_SLAB, STATE_LANES, Q * LANES), BF16),
    ]
    assert win.shape == (D_MODEL, D_IN) and wglu.shape == (D_SSM, 2 * D_MODEL)
    assert wco.shape == (D_CONV, D_MODEL) and wout.shape == (D_MODEL, D_MODEL)
    return pl.pallas_call(
        _mixer_kernel,
        grid=(N_STEP + 1,),
        in_specs=in_specs,
        out_specs=out_specs,
        out_shape=out_shape,
        scratch_shapes=scratch,
        compiler_params=pltpu.CompilerParams(
            dimension_semantics=("arbitrary",), vmem_limit_bytes=VMEM_LIMIT),
        name="mixer",
    )(x, gmix, win, bgate, um, up, ur, a_re, a_im, dvec, wglu, dw, dwb, lng, lnb, wco, wout,
      gmoe, wr1, wr2, br)


def _cmul(a, b):
    return a[0] * b[0] - a[1] * b[1], a[0] * b[1] + a[1] * b[0]


def _ssm_matrices(a_re, a_im, log_dt, b_re, b_im, c_re, c_im):
    dt = jnp.exp(log_dt)[:, None]
    mag = jnp.exp(a_re * dt)
    lam = (mag * jnp.cos(a_im * dt), mag * jnp.sin(a_im * dt))
    den = a_re * a_re + a_im * a_im
    nr = lam[0] - 1.0
    ni = lam[1]
    z_re = (nr * a_re + ni * a_im) / den
    z_im = (ni * a_re - nr * a_im) / den
    bbar = (z_re[..., None] * b_re - z_im[..., None] * b_im,
            z_re[..., None] * b_im + z_im[..., None] * b_re)
    pw = [(jnp.ones_like(lam[0]), jnp.zeros_like(lam[0])), lam]
    for _ in range(2, Q + 1):
        pw.append(_cmul(pw[-1], lam))
    e = [(c_re * p[0][:, None, :] - c_im * p[1][:, None, :],
          c_re * p[1][:, None, :] + c_im * p[0][:, None, :]) for p in pw]
    hp = lax.Precision.HIGHEST
    k = [jnp.einsum('gcn,gnd->gcd', e[m][0], bbar[0], precision=hp)
         - jnp.einsum('gcn,gnd->gcd', e[m][1], bbar[1], precision=hp) for m in range(Q)]
    split = lambda t: t.reshape((N_SLAB, GROUPS_PER_SLAB) + t.shape[1:])
    zero_k = jnp.zeros_like(k[0])
    kb = jnp.stack([jnp.stack([split(jnp.swapaxes(k[j - i] if j >= i else zero_k, 1, 2))
                               for j in range(Q)]) for i in range(Q)])
    um = jnp.transpose(kb, (2, 0, 3, 4, 1, 5)).reshape(N_SLAB, Q * LANES, Q * SSM_GROUP_WIDTH)
    f = [_cmul((pw[Q - 1 - i][0][..., None], pw[Q - 1 - i][1][..., None]), bbar) for i in range(Q)]
    fs = jnp.stack([jnp.stack([split(f[i][part]) for i in range(Q)]) for part in range(2)])
    up = jnp.transpose(fs, (2, 1, 3, 5, 0, 4)).reshape(N_SLAB, Q * LANES, 2 * SSM_STATE)
    es = jnp.stack([sign * jnp.stack([split(e[j + 1][part]) for j in range(Q)])
                    for part, sign in ((0, 1.0), (1, -1.0))])
    ur = jnp.transpose(es, (2, 0, 5, 1, 3, 4)).reshape(N_SLAB, 2 * SSM_STATE, Q * LANES)
    a_q = pw[Q]
    return (um.astype(BF16), up.astype(BF16), ur.astype(BF16),
            a_q[0].reshape(N_SLAB, STATE_LANES // 2), a_q[1].reshape(N_SLAB, STATE_LANES // 2))


def _router_weights(w_rg, b_rg, w_re, b_re):
    pad_g = LANE_EXP0 - LANE_GRP0 - N_GROUPS_MOE
    pad_e = LANES - LANE_EXP0 - N_EXPERTS
    w = jnp.concatenate([w_rg, jnp.zeros((D_MODEL, pad_g), F32), w_re, jnp.zeros((D_MODEL, pad_e), F32)], axis=1)
    b = jnp.concatenate([b_rg, jnp.zeros((pad_g,), F32), b_re, jnp.zeros((pad_e,), F32)]).reshape(1, LANES)
    w_hi = w.astype(BF16)
    w_lo = (w - w_hi.astype(F32)).astype(BF16)
    return jnp.concatenate([w_hi, w_lo], axis=1), w_hi, b


def _sc_mesh():
    return plsc.VectorSubcoreMesh(core_axis_name="core", subcore_axis_name="subcore")


def _sc_worker(mesh):
    return lax.axis_index("core") * mesh.num_subcores + lax.axis_index("subcore")


def _dispatch(h2p, dest):
    mesh = _sc_mesh()
    n_win = N_TOK // SC_WINDOW
    per_worker = n_win // (mesh.num_cores * mesh.num_subcores)
    assert per_worker * mesh.num_cores * mesh.num_subcores == n_win

    @pl.kernel(out_type=jax.ShapeDtypeStruct((N_ROWS,) + ROW_TILE, U32), mesh=mesh,
               scratch_types=[pltpu.VMEM((SC_WINDOW,), I32), pltpu.VMEM((SC_WINDOW,) + ROW_TILE, U32)])
    def scatter_rows(h_hbm, dest_hbm, xs_hbm, idx_v, rows_v):
        first = _sc_worker(mesh) * per_worker

        @pl.loop(0, per_worker)
        def _(w):
            win = first + w
            pltpu.sync_copy(h_hbm.at[pl.ds(win * SC_WINDOW, SC_WINDOW)], rows_v)
            for j in range(TOPK):
                pltpu.sync_copy(dest_hbm.at[j, win], idx_v)
                pltpu.sync_copy(rows_v, xs_hbm.at[idx_v])

    return scatter_rows(h2p, dest)


def _collect(ys, dest):
    mesh = _sc_mesh()
    n_tok = dest.shape[1]
    n_win = TOPK * n_tok // SC_WINDOW
    per_worker = n_win // (mesh.num_cores * mesh.num_subcores)
    assert per_worker * mesh.num_cores * mesh.num_subcores == n_win

    @pl.kernel(out_type=jax.ShapeDtypeStruct((TOPK * n_tok,) + ROW_TILE, U32), mesh=mesh,
               scratch_types=[pltpu.VMEM((SC_WINDOW,), I32), pltpu.VMEM((SC_WINDOW,) + ROW_TILE, U32)])
    def gather_rows(ys_hbm, dest_hbm, yg_hbm, idx_v, rows_v):
        first = _sc_worker(mesh) * per_worker

        @pl.loop(0, per_worker)
        def _(w):
            win = first + w
            pltpu.sync_copy(dest_hbm.at[win], idx_v)
            pltpu.sync_copy(ys_hbm.at[idx_v], rows_v)
            pltpu.sync_copy(rows_v, yg_hbm.at[pl.ds(win * SC_WINDOW, SC_WINDOW)])

    return gather_rows(ys, dest.reshape(n_win, SC_WINDOW)).reshape((TOPK, n_tok) + ROW_TILE)


def _expert_kernel(first_ref, nblk_ref, nvalid_ref, nused_ref, xs_hbm, wg_ref, wu_ref, wd_ref, ys_hbm,
                   wg_scr, wu_scr, wd_scr, x_buf, y_buf, in_sem, out_sem):
    e = pl.program_id(0)
    nused = nused_ref[0]

    def in_copy(g):
        slot = lax.rem(g, IN_SLOTS)
        return pltpu.make_async_copy(xs_hbm.at[pl.ds(g * BM, BM)], x_buf.at[slot], in_sem.at[slot])

    def out_copy(g, slot):
        return pltpu.make_async_copy(y_buf.at[slot], ys_hbm.at[pl.ds(g * BM, BM)], out_sem.at[slot])

    @pl.when(e == 0)
    def _first():
        for g in range(IN_AHEAD):
            in_copy(g).start()

    wg_scr[...] = wg_ref[0].astype(BF16)
    wu_scr[...] = wu_ref[0].astype(BF16)
    wd_scr[...] = wd_ref[0].astype(BF16)

    def block(b, carry):
        g = first_ref[e] + b
        slot = lax.rem(g, 2)
        in_copy(g).wait()

        @pl.when(g + IN_AHEAD < nused)
        def _prefetch():
            in_copy(g + IN_AHEAD).start()

        @pl.when(g >= 2)
        def _slot_free():
            out_copy(g - 2, slot).wait()

        valid = lax.broadcasted_iota(I32, (BM, 1), 0) < nvalid_ref[g]
        x_blk = x_buf[lax.rem(g, IN_SLOTS)].reshape(BM, HALF)
        lo, hi = _unpack_bf16_pair(jnp.where(valid, x_blk, jnp.uint32(0)))
        lo = lo.astype(BF16)
        hi = hi.astype(BF16)
        gate = jnp.dot(lo, wg_scr[0:HALF, :], preferred_element_type=F32) \
            + jnp.dot(hi, wg_scr[HALF:, :], preferred_element_type=F32)
        up = jnp.dot(lo, wu_scr[0:HALF, :], preferred_element_type=F32) \
            + jnp.dot(hi, wu_scr[HALF:, :], preferred_element_type=F32)
        act = (jax.nn.silu(gate) * up).astype(BF16)
        o = jnp.dot(act, wd_scr[...], preferred_element_type=F32)
        y_buf[slot] = _pack_bf16_pair(o[:, 0:HALF], o[:, HALF:]).reshape((BM,) + ROW_TILE)
        out_copy(g, slot).start()
        return carry

    lax.fori_loop(0, nblk_ref[e], block, 0)

    @pl.when(e == N_EXPERTS - 1)
    def _drain():
        out_copy(nused - 2, lax.rem(nused, 2)).wait()
        out_copy(nused - 1, 1 - lax.rem(nused, 2)).wait()


def _experts(first, nblk, nvalid, nused, xs, wg, wu, wd):
    grid_spec = pltpu.PrefetchScalarGridSpec(
        num_scalar_prefetch=4,
        grid=(N_EXPERTS,),
        in_specs=[
            pl.BlockSpec(memory_space=pl.ANY),
            pl.BlockSpec((1, D_MODEL, D_EXPERT), lambda e, *_: (e, 0, 0)),
            pl.BlockSpec((1, D_MODEL, D_EXPERT), lambda e, *_: (e, 0, 0)),
            pl.BlockSpec((1, D_EXPERT, D_MODEL), lambda e, *_: (e, 0, 0)),
        ],
        out_specs=pl.BlockSpec(memory_space=pl.ANY),
        scratch_shapes=[
            pltpu.VMEM((D_MODEL, D_EXPERT), BF16),
            pltpu.VMEM((D_MODEL, D_EXPERT), BF16),
            pltpu.VMEM((D_EXPERT, D_MODEL), BF16),
            pltpu.VMEM((IN_SLOTS, BM) + ROW_TILE, U32),
            pltpu.VMEM((2, BM) + ROW_TILE, U32),
            pltpu.SemaphoreType.DMA((IN_SLOTS,)),
            pltpu.SemaphoreType.DMA((2,)),
        ],
    )
    return pl.pallas_call(
        _expert_kernel,
        grid_spec=grid_spec,
        out_shape=jax.ShapeDtypeStruct((N_ROWS,) + ROW_TILE, U32),
        compiler_params=pltpu.CompilerParams(
            dimension_semantics=("arbitrary",), vmem_limit_bytes=VMEM_LIMIT),
        name="experts",
    )(first, nblk, nvalid, nused, xs, wg, wu, wd)


def _combine_kernel(x1_ref, rec_ref, yg_ref, p_ref, gple_ref, wpg_ref, wple_ref, gfin_ref, *rest):
    out_ref = rest[-1]
    ple = jnp.dot(p_ref[0].reshape(TM, D_PLE).astype(BF16), wple_ref[...], preferred_element_type=F32)
    rec = rec_ref[...]
    w0 = rec[:, REC_W0:REC_W0 + 1]
    w1 = rec[:, REC_W1:REC_W1 + 1]
    lo0, hi0 = _unpack_bf16_pair(yg_ref[0].reshape(TM, HALF))
    lo1, hi1 = _unpack_bf16_pair(yg_ref[1].reshape(TM, HALF))
    moe = jnp.concatenate([lo0 * w0 + lo1 * w1, hi0 * w0 + hi1 * w1], axis=1)
    x2 = x1_ref[...].reshape(TM, D_MODEL) + moe
    gate = jax.nn.sigmoid(jnp.dot(_rms(x2, gple_ref[...]).astype(BF16), wpg_ref[...],
                                  preferred_element_type=F32))
    x3 = x2 + gate * ple
    out_ref[...] = _rms(x3, gfin_ref[...]).reshape(BATCH, TT, D_MODEL)


def _combine(s0, n_steps, x1, rec, yg, p, gple, wpg, wple, gfin, out_prev=None):
    seq_spec = pl.BlockSpec((BATCH, TT, D_MODEL), lambda i: (0, s0 + i, 0))
    in_specs = [
        seq_spec,
        pl.BlockSpec((TM, LANES), lambda i: (s0 + i, 0)),
        pl.BlockSpec((TOPK, TM) + ROW_TILE, lambda i: (0, i, 0, 0)),
        pl.BlockSpec((1, BATCH, TT, D_PLE), lambda i: (0, 0, s0 + i, 0)),
        _const_spec((1, D_MODEL)),
        _const_spec((D_MODEL, D_MODEL)),
        _const_spec((D_PLE, D_MODEL)),
        _const_spec((1, D_MODEL)),
    ]
    args = [x1, rec, yg, p, gple, wpg, wple, gfin]
    aliases = {}
    if out_prev is not None:
        in_specs.append(pl.BlockSpec(memory_space=pl.ANY))
        args.append(out_prev)
        aliases = {len(args) - 1: 0}
    return pl.pallas_call(
        _combine_kernel,
        grid=(n_steps,),
        in_specs=in_specs,
        out_specs=seq_spec,
        out_shape=jax.ShapeDtypeStruct((BATCH, SEQ, D_MODEL), F32),
        input_output_aliases=aliases,
        compiler_params=pltpu.CompilerParams(
            dimension_semantics=("arbitrary",), vmem_limit_bytes=VMEM_LIMIT),
        name="combine",
    )(*args)


def kernel(x, p, g_mix, w_in, b_gate, ssm_a_re, ssm_a_im, ssm_log_dt, ssm_b_re, ssm_b_im, ssm_c_re,
           ssm_c_im, ssm_d, w_glu, conv_dw, conv_dw_b, conv_ln_g, conv_ln_b, w_conv_out, w_out, g_moe,
           w_router_group, b_router_group, w_router_expert, b_router_expert, w_exp_gate, w_exp_up,
           w_exp_down, g_ple, w_ple_gate, w_ple, g_final):
    assert x.shape == (BATCH, SEQ, D_MODEL) and p.shape == (1, BATCH, SEQ, D_PLE)
    row = lambda v: v.reshape(1, -1)

    um, up, ur, a_re, a_im = _ssm_matrices(ssm_a_re[0], ssm_a_im[0], ssm_log_dt[0], ssm_b_re[0],
                                           ssm_b_im[0], ssm_c_re[0], ssm_c_im[0])
    wr1, wr2, br = _router_weights(w_router_group[0], b_router_group[0], w_router_expert[0],
                                   b_router_expert[0])
    x1, h2p, rec, rect, cnt = _mixer(
        x, row(g_mix[0]), w_in[0], row(b_gate[0]), um, up, ur, a_re, a_im,
        row(ssm_d[0]), w_glu[0], conv_dw[0], row(conv_dw_b[0]), row(conv_ln_g[0]),
        row(conv_ln_b[0]), w_conv_out[0], w_out[0], row(g_moe[0]), wr1, wr2, br)

    counts = cnt[0, LANE_EXP0:LANE_EXP0 + N_EXPERTS].astype(I32)
    pcounts = (counts + BM - 1) // BM * BM
    pends = jnp.cumsum(pcounts)
    pstarts = pends - pcounts
    eid = rect[REC_EID0:REC_EID1 + 1].astype(I32)
    rank = rect[REC_RANK0:REC_RANK1 + 1].astype(I32)
    dest = (jnp.sum(jnp.where(eid[..., None] == jnp.arange(N_EXPERTS, dtype=I32), pstarts, 0), axis=-1)
            + rank).reshape(TOPK, N_TOK // SC_WINDOW, SC_WINDOW)
    nused = (pends[-1] // BM).astype(I32)
    blk = jnp.arange(N_BLK, dtype=I32)[:, None] * BM
    in_expert = (pstarts[None, :] <= blk) & (blk < pends[None, :])
    nvalid = jnp.clip(jnp.sum(jnp.where(in_expert, (pstarts + counts)[None, :] - blk, 0), axis=1), 0, BM)

    xs = _dispatch(h2p, dest)
    ys = _experts(pstarts // BM, pcounts // BM, nvalid.astype(I32), nused.reshape(1), xs,
                  w_exp_gate[0], w_exp_up[0], w_exp_down[0])
    dest_tok = dest.reshape(TOPK, N_TOK)
    wpg = w_ple_gate[0].astype(BF16)
    wple = w_ple[0].astype(BF16)
    out = None
    s0 = 0
    for n_steps in PART_STEPS:
        yg = _collect(ys, dest_tok[:, s0 * TM:(s0 + n_steps) * TM])
        out = _combine(s0, n_steps, x1, rec, yg, p, row(g_ple[0]), wpg, wple, row(g_final), out)
        s0 += n_steps
    return out
```

```python
import jax
import jax.numpy as jnp
from jax import lax
from jax.experimental import pallas as pl
from jax.experimental.pallas import tpu as pltpu
from jax.experimental.pallas import tpu_sc as plsc

F32 = jnp.float32
BF16 = jnp.bfloat16
U32 = jnp.uint32
I32 = jnp.int32

D_MODEL = 1024
BATCH = 8
SEQ = 2048
N_TOK = BATCH * SEQ
D_SSM = 512
SSM_GROUP_WIDTH = 16
SSM_GROUPS = 32
SSM_STATE = 64
D_CONV = 512
CONV_WIDTH = 31
D_IN = D_SSM + 2 * D_CONV + 2 * D_MODEL
N_GROUPS_MOE = 4
EXPERTS_PER_GROUP = 8
N_EXPERTS = 32
TOPK = 2
D_EXPERT = 512
D_PLE = 256
EPS = 1e-6

SUBLANES = 8
LANES = 128
assert BATCH == SUBLANES

TT = 64
TM = TT * BATCH
N_STEP = SEQ // TT
SB = 512
NSB = TM // SB
BPS = SB // TT
Q = 2
N_SLAB = D_SSM // LANES
GROUPS_PER_SLAB = SSM_GROUPS // N_SLAB
ROWS_Z = TM // Q
STATE_LANES = 2 * GROUPS_PER_SLAB * SSM_STATE
HALO = (CONV_WIDTH - 1) * BATCH
CHUNK_ROWS = SB // (Q * SUBLANES)
W_STAGE_ROWS = 64
W_STAGE_SLOTS = 4
CONV_ROWS = 64
N_LC = D_CONV // LANES

LANE_GRP0 = 0
LANE_EXP0 = 32
REC_EID0, REC_EID1, REC_W0, REC_W1, REC_RANK0, REC_RANK1 = 0, 1, 2, 3, 4, 5
REC_ROWS = 8

BM = 256
N_BLK = (TOPK * N_TOK + N_EXPERTS * (BM - 1) + BM - 1) // BM
N_ROWS = N_BLK * BM
HALF = D_MODEL // 2
ROW_TILE = (HALF // LANES, LANES)
SC_WINDOW = 128
IN_AHEAD = 3
IN_SLOTS = IN_AHEAD + 1
PART_STEPS = (12, 20)
assert sum(PART_STEPS) == N_STEP

VMEM_LIMIT = 56 * 1024 * 1024


def _const_spec(shape):
    n = len(shape)
    return pl.BlockSpec(shape, lambda *_: (0,) * n, pipeline_mode=pl.Buffered(1))


def _rms(x, g):
    ms = jnp.mean(x * x, axis=-1, keepdims=True)
    return x * lax.rsqrt(ms + EPS) * g


def _pack_bf16_pair(lo, hi):
    ulo = lax.bitcast_convert_type(lo.astype(BF16).astype(F32), U32)
    uhi = lax.bitcast_convert_type(hi.astype(BF16).astype(F32), U32)
    return (ulo >> 16) | (uhi & jnp.uint32(0xFFFF0000))


def _unpack_bf16_pair(w):
    lo = lax.bitcast_convert_type(w << 16, F32)
    hi = lax.bitcast_convert_type(w & jnp.uint32(0xFFFF0000), F32)
    return lo, hi


def _load_weights_bf16(pairs, stage, sem):
    chunks = [(src, dst, r0) for src, dst in pairs for r0 in range(0, src.shape[0], W_STAGE_ROWS)]

    def copy(c):
        src, _, r0 = chunks[c]
        slot = c % W_STAGE_SLOTS
        return pltpu.make_async_copy(src.at[pl.ds(r0, W_STAGE_ROWS)],
                                     stage.at[slot, :, 0:src.shape[1]], sem.at[slot])

    for c in range(W_STAGE_SLOTS):
        copy(c).start()
    for c, (src, dst, r0) in enumerate(chunks):
        copy(c).wait()
        dst[r0:r0 + W_STAGE_ROWS, :] = stage[c % W_STAGE_SLOTS, :, 0:src.shape[1]].astype(BF16)
        if c + W_STAGE_SLOTS < len(chunks):
            copy(c + W_STAGE_SLOTS).start()


def _expand_ssm(um_ref, up_ref, ur_ref, mp_ref, r_ref):
    gw, ns, half = SSM_GROUP_WIDTH, SSM_STATE, STATE_LANES // 2
    div = lambda a, n: lax.shift_right_logical(a, n.bit_length() - 1)
    mod = lambda a, n: a & (n - 1)
    iota2 = lambda shape: (lax.broadcasted_iota(I32, shape, 0), lax.broadcasted_iota(I32, shape, 1))
    one = lambda cond: jnp.where(cond, 1.0, 0.0).astype(BF16)

    r, q = iota2((Q * gw, Q * LANES))
    x_m = one((div(r, gw) == div(q, LANES)) & (mod(r, gw) == mod(q, gw)))
    r, q = iota2((2 * ns, STATE_LANES))
    x_p = one((div(r, ns) == div(q, half)) & (mod(r, ns) == mod(q, ns)))
    p, r = iota2((STATE_LANES, 2 * ns))
    x_r = one((div(p, half) == div(r, ns)) & (mod(p, ns) == mod(r, ns)))
    p, q = iota2((Q * LANES, Q * LANES))
    same_m = div(mod(p, LANES), gw) == div(mod(q, LANES), gw)
    p, q = iota2((Q * LANES, STATE_LANES))
    same_p = div(mod(p, LANES), gw) == div(mod(q, half), ns)
    p, q = iota2((STATE_LANES, Q * LANES))
    same_r = div(mod(p, half), ns) == div(mod(q, LANES), gw)
    for s in range(N_SLAB):
        m = jnp.dot(um_ref[s], x_m, preferred_element_type=F32)
        mp_ref[s, :, 0:Q * LANES] = jnp.where(same_m, m, 0.0).astype(BF16)
        pm = jnp.dot(up_ref[s], x_p, preferred_element_type=F32)
        mp_ref[s, :, Q * LANES:] = jnp.where(same_p, pm, 0.0).astype(BF16)
        rm = jnp.dot(x_r, ur_ref[s], preferred_element_type=F32)
        r_ref[s] = jnp.where(same_r, rm, 0.0).astype(BF16)


def _mixer_kernel(x_ref, gmix_ref, win_hbm, bgate_ref, um_ref, up_ref, ur_ref, are_ref,
                  aim_ref, d_ref, wglu_hbm, dw_ref, dwb_ref, lng_ref, lnb_ref, wco_hbm, wout_hbm,
                  gmoe_ref, wr1_ref, wr2_ref, br_ref,
                  x1_ref, h2p_ref, rec_ref, rect_ref, cnt_ref,
                  hb_scr, ht_scr, u_scr, y_scr, yi_scr, xs_scr, z_scr, conv_scr, act_scr, actb_scr,
                  logit_scr, s_scr, cnt_scr, win_ref, wglu_ref, wco_ref, wout_ref, wstage_scr, wstage_sem,
                  mp_ref, r_ref):
    step = pl.program_id(0)
    assert NSB == 1

    @pl.when(step == 0)
    def _init():
        logit_scr[...] = jnp.zeros(logit_scr.shape, F32)
        z_scr[:, 0:HALO, :] = jnp.zeros((N_LC, HALO, LANES), F32)
        s_scr[...] = jnp.zeros(s_scr.shape, F32)
        cnt_scr[...] = jnp.zeros(cnt_scr.shape, F32)
        _expand_ssm(um_ref, up_ref, ur_ref, mp_ref, r_ref)
        _load_weights_bf16([(win_hbm, win_ref), (wglu_hbm, wglu_ref), (wco_hbm, wco_ref),
                            (wout_hbm, wout_ref)], wstage_scr, wstage_sem)

    def sub_rows(r):
        return pl.ds(pl.multiple_of(r * SB, SB), SB)

    def phase_a(r, carry):
        xb = x_ref[pl.ds(r * BPS, BPS)].reshape(SB, D_MODEL)
        hb_scr[sub_rows(r), :] = _rms(xb, gmix_ref[...]).astype(BF16)
        return carry

    def phase_a3(r, carry):
        h = ht_scr[sub_rows(r), :]
        u = jnp.dot(h, win_ref[:, 0:D_SSM], preferred_element_type=F32)
        u_scr[pl.ds(r * CHUNK_ROWS, CHUNK_ROWS)] = u.reshape(CHUNK_ROWS, Q, SUBLANES, D_SSM)
        v = jnp.dot(h, win_ref[:, D_SSM:D_SSM + 2 * D_CONV], preferred_element_type=F32)
        zc = v[:, 0:D_CONV] * jax.nn.sigmoid(v[:, D_CONV:])
        for lc in range(N_LC):
            z_scr[lc, pl.ds(pl.multiple_of(HALO + r * SB, SUBLANES), SB), :] = zc[:, lc * LANES:(lc + 1) * LANES]
        return carry

    def phase_b():
        for s in range(N_SLAB):
            lanes = slice(s * LANES, (s + 1) * LANES)
            z = jnp.concatenate(
                [u_scr[:, i, :, lanes].reshape(ROWS_Z, LANES) for i in range(Q)], axis=1).astype(BF16)
            xp = jnp.dot(z, mp_ref[s], preferred_element_type=F32)
            yi_scr[s] = xp[:, 0:Q * LANES]
            xs_scr[s] = xp[:, Q * LANES:]

        half = STATE_LANES // 2
        for s in range(N_SLAB):
            a_re = jnp.broadcast_to(are_ref[s:s + 1, :], (SUBLANES, half))
            a_im = jnp.broadcast_to(aim_ref[s:s + 1, :], (SUBLANES, half))

            def scan_body(k, carry, s=s, a_re=a_re, a_im=a_im):
                s_re, s_im = carry
                rows = pl.ds(pl.multiple_of(k * SUBLANES, SUBLANES), SUBLANES)
                x_re = xs_scr[s, rows, 0:half]
                x_im = xs_scr[s, rows, half:]
                xs_scr[s, rows, 0:half] = s_re
                xs_scr[s, rows, half:] = s_im
                n_re = a_re * s_re - a_im * s_im + x_re
                n_im = a_re * s_im + a_im * s_re + x_im
                return n_re, n_im

            s_re, s_im = lax.fori_loop(0, ROWS_Z // SUBLANES, scan_body,
                                       (s_scr[s, :, 0:half], s_scr[s, :, half:]), unroll=True)
            s_scr[s, :, 0:half] = s_re
            s_scr[s, :, half:] = s_im

        for s in range(N_SLAB):
            lanes = slice(s * LANES, (s + 1) * LANES)
            y_tot = yi_scr[s] + jnp.dot(xs_scr[s].astype(BF16), r_ref[s], preferred_element_type=F32)
            for j in range(Q):
                y_scr[:, j, :, lanes] = y_tot[:, j * LANES:(j + 1) * LANES].reshape(
                    ROWS_Z // SUBLANES, SUBLANES, LANES)

    def phase_c1(r, carry):
        rows = sub_rows(r)
        crow = pl.ds(r * CHUNK_ROWS, CHUNK_ROWS)
        y = y_scr[crow].reshape(SB, D_SSM) + d_ref[...] * u_scr[crow].reshape(SB, D_SSM)
        act_scr[rows, 0:D_SSM] = jax.nn.gelu(y).astype(BF16)
        for lc in range(N_LC):
            lanes = slice(lc * LANES, (lc + 1) * LANES)

            def conv_piece(rc, c, lc=lc, lanes=lanes):
                r0 = r * SB + rc * CONV_ROWS
                piece = jnp.broadcast_to(dwb_ref[:, lanes], (CONV_ROWS, LANES))
                for j in range(CONV_WIDTH):
                    zrows = pl.ds(pl.multiple_of(r0 + j * BATCH, SUBLANES), CONV_ROWS)
                    piece = piece + dw_ref[j:j + 1, lanes] * z_scr[lc, zrows, :]
                conv_scr[pl.ds(pl.multiple_of(rc * CONV_ROWS, CONV_ROWS), CONV_ROWS), lanes] = piece
                return c

            lax.fori_loop(0, SB // CONV_ROWS, conv_piece, 0, unroll=4)
        acc = conv_scr[...]
        mu = jnp.mean(acc, axis=-1, keepdims=True)
        cen = acc - mu
        var = jnp.mean(cen * cen, axis=-1, keepdims=True)
        ln = cen * lax.rsqrt(var + EPS) * lng_ref[...] + lnb_ref[...]
        act_scr[rows, D_SSM:] = jax.nn.silu(ln).astype(BF16)
        return carry

    lane = lax.broadcasted_iota(I32, (1, LANES), 1).astype(F32)
    grp_mask = lane < float(N_GROUPS_MOE)
    exp_lane = (lane >= float(LANE_EXP0)) & (lane < float(LANE_EXP0 + N_EXPERTS))
    lane_grp = jnp.floor((lane - float(LANE_EXP0)) * (1.0 / EXPERTS_PER_GROUP))
    tri = (lax.broadcasted_iota(I32, (SB, SB), 0) > lax.broadcasted_iota(I32, (SB, SB), 1)).astype(BF16)
    neg_inf = float("-inf")
    big = float(4 * LANES)

    def phase_c3(r, carry):
        rows = sub_rows(r)
        h = hb_scr[rows, :]
        g0 = D_SSM + 2 * D_CONV
        gate_ssm = jnp.dot(h, win_ref[:, g0:g0 + D_MODEL], preferred_element_type=F32) \
            + bgate_ref[:, 0:D_MODEL]
        gate_conv = jnp.dot(h, win_ref[:, g0 + D_MODEL:], preferred_element_type=F32) \
            + bgate_ref[:, D_MODEL:]
        zz = jnp.dot(actb_scr[rows, 0:D_SSM], wglu_ref[...], preferred_element_type=F32)
        y_ssm = zz[:, 0:D_MODEL] * jax.nn.sigmoid(zz[:, D_MODEL:])
        y_conv = jnp.dot(actb_scr[rows, D_SSM:], wco_ref[...], preferred_element_type=F32)

        merged = jax.nn.sigmoid(gate_ssm) * y_ssm + jax.nn.sigmoid(gate_conv) * y_conv
        xb = x_ref[pl.ds(r * BPS, BPS)].reshape(SB, D_MODEL)
        x1 = xb + jnp.dot(merged.astype(BF16), wout_ref[...], preferred_element_type=F32)
        x1_ref[pl.ds(r * BPS, BPS)] = x1.reshape(BPS, TT, D_MODEL)

        h2 = _rms(x1, gmoe_ref[...])
        h2p_ref[rows] = _pack_bf16_pair(h2[:, 0:HALF], h2[:, HALF:]).reshape((SB,) + ROW_TILE)

        h2_hi = h2.astype(BF16)
        h2_lo = (h2 - h2_hi.astype(F32)).astype(BF16)
        l1 = jnp.dot(h2_hi, wr1_ref[...], preferred_element_type=F32)
        l2 = jnp.dot(h2_lo, wr2_ref[...], preferred_element_type=F32)
        logit_scr[rows, :] = l1[:, 0:LANES] + l1[:, LANES:] + l2 + br_ref[...]
        return carry

    def route_previous():
        rows = sub_rows(0)
        logits = logit_scr[...]
        counted = jnp.where(step > 0, 1.0, 0.0)

        lg = jnp.where(grp_mask, logits, neg_inf)
        g_max = jnp.max(lg, axis=-1, keepdims=True)
        g_sel = jnp.min(jnp.where(lg == g_max, lane, big), axis=-1, keepdims=True)
        p_g = 1.0 / jnp.sum(jnp.where(grp_mask, jnp.exp(logits - g_max), 0.0), axis=-1, keepdims=True)
        le = jnp.where(exp_lane & (lane_grp == g_sel), logits, neg_inf)
        m1 = jnp.max(le, axis=-1, keepdims=True)
        i1 = jnp.min(jnp.where(le == m1, lane, big), axis=-1, keepdims=True)
        le2 = jnp.where(lane == i1, neg_inf, le)
        m2 = jnp.max(le2, axis=-1, keepdims=True)
        i2 = jnp.min(jnp.where(le2 == m2, lane, big), axis=-1, keepdims=True)
        e2 = jnp.exp(m2 - m1)
        den = 1.0 + e2
        w_a = (1.0 / den) * p_g
        w_b = (e2 / den) * p_g

        sel1 = lane == i1
        sel2 = lane == i2
        onehot = jnp.where(sel1 | sel2, counted, 0.0)
        prefix = jnp.dot(tri, onehot.astype(BF16), preferred_element_type=F32) + cnt_scr[...]
        rank_a = jnp.sum(jnp.where(sel1, prefix, 0.0), axis=-1, keepdims=True)
        rank_b = jnp.sum(jnp.where(sel2, prefix, 0.0), axis=-1, keepdims=True)
        cnt_scr[...] = cnt_scr[...] + jnp.sum(onehot, axis=0, keepdims=True)

        rec = jnp.where(lane == float(REC_EID0), i1 - float(LANE_EXP0), 0.0)
        rec = jnp.where(lane == float(REC_EID1), i2 - float(LANE_EXP0), rec)
        rec = jnp.where(lane == float(REC_W0), w_a, rec)
        rec = jnp.where(lane == float(REC_W1), w_b, rec)
        rec = jnp.where(lane == float(REC_RANK0), rank_a, rec)
        rec = jnp.where(lane == float(REC_RANK1), rank_b, rec)
        rec_ref[rows, :] = rec
        rect_ref[...] = jnp.transpose(rec)[0:REC_ROWS, :]
        cnt_ref[...] = cnt_scr[...]

    @pl.when(step < N_STEP)
    def _tile():
        route_previous()
        phase_a(0, 0)
        ht_scr[...] = jnp.swapaxes(hb_scr[...].reshape(BATCH, TT, D_MODEL), 0, 1).reshape(TM, D_MODEL)
        phase_a3(0, 0)
        phase_b()
        phase_c1(0, 0)
        z_scr[:, 0:HALO, :] = z_scr[:, TM:TM + HALO, :]
        actb_scr[...] = jnp.swapaxes(act_scr[...].reshape(TT, BATCH, D_SSM + D_CONV), 0, 1).reshape(
            TM, D_SSM + D_CONV)
        phase_c3(0, 0)

    @pl.when(step == N_STEP)
    def _last():
        route_previous()


def _mixer(x, gmix, win, bgate, um, up, ur, a_re, a_im, dvec, wglu, dw, dwb, lng, lnb, wco,
           wout, gmoe, wr1, wr2, br):
    tile = lambda i: jnp.minimum(i, N_STEP - 1)
    routed = lambda i: jnp.maximum(i - 1, 0)
    seq_spec = pl.BlockSpec((BATCH, TT, D_MODEL), lambda i: (0, tile(i), 0))
    in_hbm = pl.BlockSpec(memory_space=pl.ANY)
    in_specs = [
        seq_spec,
        _const_spec((1, D_MODEL)),
        in_hbm,
        _const_spec((1, 2 * D_MODEL)),
        _const_spec(um.shape),
        _const_spec(up.shape),
        _const_spec(ur.shape),
        _const_spec(a_re.shape),
        _const_spec(a_im.shape),
        _const_spec((1, D_SSM)),
        in_hbm,
        _const_spec((CONV_WIDTH, D_CONV)),
        _const_spec((1, D_CONV)),
        _const_spec((1, D_CONV)),
        _const_spec((1, D_CONV)),
        in_hbm,
        in_hbm,
        _const_spec((1, D_MODEL)),
        _const_spec((D_MODEL, 2 * LANES)),
        _const_spec((D_MODEL, LANES)),
        _const_spec((1, LANES)),
    ]
    out_specs = [
        seq_spec,
        pl.BlockSpec((TM,) + ROW_TILE, lambda i: (tile(i), 0, 0)),
        pl.BlockSpec((TM, LANES), lambda i: (routed(i), 0)),
        pl.BlockSpec((REC_ROWS, TM), lambda i: (0, routed(i))),
        pl.BlockSpec((1, LANES), lambda i: (0, 0)),
    ]
    out_shape = [
        jax.ShapeDtypeStruct((BATCH, SEQ, D_MODEL), F32),
        jax.ShapeDtypeStruct((N_TOK,) + ROW_TILE, U32),
        jax.ShapeDtypeStruct((N_TOK, LANES), F32),
        jax.ShapeDtypeStruct((REC_ROWS, N_TOK), F32),
        jax.ShapeDtypeStruct((1, LANES), F32),
    ]
    chunk_shape = (ROWS_Z // SUBLANES, Q, SUBLANES, D_SSM)
    scratch = [
        pltpu.VMEM((TM, D_MODEL), BF16),
        pltpu.VMEM((TM, D_MODEL), BF16),
        pltpu.VMEM(chunk_shape, F32),
        pltpu.VMEM(chunk_shape, F32),
        pltpu.VMEM((N_SLAB, ROWS_Z, Q * LANES), F32),
        pltpu.VMEM((N_SLAB, ROWS_Z, STATE_LANES), F32),
        pltpu.VMEM((N_LC, HALO + TM, LANES), F32),
        pltpu.VMEM((SB, D_CONV), F32),
        pltpu.VMEM((TM, D_SSM + D_CONV), BF16),
        pltpu.VMEM((TM, D_SSM + D_CONV), BF16),
        pltpu.VMEM((TM, LANES), F32),
        pltpu.VMEM((N_SLAB, SUBLANES, STATE_LANES), F32),
        pltpu.VMEM((1, LANES), F32),
        pltpu.VMEM(win.shape, BF16),
        pltpu.VMEM(wglu.shape, BF16),
        pltpu.VMEM(wco.shape, BF16),
        pltpu.VMEM(wout.shape, BF16),
        pltpu.VMEM((W_STAGE_SLOTS, W_STAGE_ROWS, D_IN), F32),
        pltpu.SemaphoreType.DMA((W_STAGE_SLOTS,)),
        pltpu.VMEM((N_SLAB, Q * LANES, Q * LANES + STATE_LANES), BF16),
        pltpu.VMEM((N_SLAB, STATE_LANES, Q * LANES), BF16),
    ]
    assert win.shape == (D_MODEL, D_IN) and wglu.shape == (D_SSM, 2 * D_MODEL)
    assert wco.shape == (D_CONV, D_MODEL) and wout.shape == (D_MODEL, D_MODEL)
    return pl.pallas_call(
        _mixer_kernel,
        grid=(N_STEP + 1,),
        in_specs=in_specs,
        out_specs=out_specs,
        out_shape=out_shape,
        scratch_shapes=scratch,
        compiler_params=pltpu.CompilerParams(
            dimension_semantics=("arbitrary",), vmem_limit_bytes=VMEM_LIMIT),
        name="mixer",
    )(x, gmix, win, bgate, um, up, ur, a_re, a_im, dvec, wglu, dw, dwb, lng, lnb, wco, wout,
      gmoe, wr1, wr2, br)


def _cmul(a, b):
    return a[0] * b[0] - a[1] * b[1], a[0] * b[1] + a[1] * b[0]


def _ssm_matrices(a_re, a_im, log_dt, b_re, b_im, c_re, c_im):
    dt = jnp.exp(log_dt)[:, None]
    mag = jnp.exp(a_re * dt)
    lam = (mag * jnp.cos(a_im * dt), mag * jnp.sin(a_im * dt))
    den = a_re * a_re + a_im * a_im
    nr = lam[0] - 1.0
    ni = lam[1]
    z_re = (nr * a_re + ni * a_im) / den
    z_im = (ni * a_re - nr * a_im) / den
    bbar = (z_re[..., None] * b_re - z_im[..., None] * b_im,
            z_re[..., None] * b_im + z_im[..., None] * b_re)
    pw = [(jnp.ones_like(lam[0]), jnp.zeros_like(lam[0])), lam]
    for _ in range(2, Q + 1):
        pw.append(_cmul(pw[-1], lam))
    e = [(c_re * p[0][:, None, :] - c_im * p[1][:, None, :],
          c_re * p[1][:, None, :] + c_im * p[0][:, None, :]) for p in pw]
    e_cat = jnp.concatenate([jnp.concatenate([e[m][0], -e[m][1]], axis=-1) for m in range(Q)], axis=1)
    k_cat = jnp.einsum('gcn,gnd->gcd', e_cat, jnp.concatenate(bbar, axis=1), precision=lax.Precision.HIGHEST)
    k = [k_cat[:, m * SSM_GROUP_WIDTH:(m + 1) * SSM_GROUP_WIDTH, :] for m in range(Q)]
    split = lambda t: t.reshape((N_SLAB, GROUPS_PER_SLAB) + t.shape[1:])
    zero_k = jnp.zeros_like(k[0])
    kb = jnp.stack([jnp.stack([split(jnp.swapaxes(k[j - i] if j >= i else zero_k, 1, 2))
                               for j in range(Q)]) for i in range(Q)])
    um = jnp.transpose(kb, (2, 0, 3, 4, 1, 5)).reshape(N_SLAB, Q * LANES, Q * SSM_GROUP_WIDTH)
    f = [_cmul((pw[Q - 1 - i][0][..., None], pw[Q - 1 - i][1][..., None]), bbar) for i in range(Q)]
    fs = jnp.stack([jnp.stack([split(f[i][part]) for i in range(Q)]) for part in range(2)])
    up = jnp.transpose(fs, (2, 1, 3, 5, 0, 4)).reshape(N_SLAB, Q * LANES, 2 * SSM_STATE)
    es = jnp.stack([sign * jnp.stack([split(e[j + 1][part]) for j in range(Q)])
                    for part, sign in ((0, 1.0), (1, -1.0))])
    ur = jnp.transpose(es, (2, 0, 5, 1, 3, 4)).reshape(N_SLAB, 2 * SSM_STATE, Q * LANES)
    a_q = pw[Q]
    return (um.astype(BF16), up.astype(BF16), ur.astype(BF16),
            a_q[0].reshape(N_SLAB, STATE_LANES // 2), a_q[1].reshape(N_SLAB, STATE_LANES // 2))


def _router_weights(w_rg, b_rg, w_re, b_re):
    pad_g = LANE_EXP0 - LANE_GRP0 - N_GROUPS_MOE
    pad_e = LANES - LANE_EXP0 - N_EXPERTS
    w = jnp.concatenate([w_rg, jnp.zeros((D_MODEL, pad_g), F32), w_re, jnp.zeros((D_MODEL, pad_e), F32)], axis=1)
    b = jnp.concatenate([b_rg, jnp.zeros((pad_g,), F32), b_re, jnp.zeros((pad_e,), F32)]).reshape(1, LANES)
    w_hi = w.astype(BF16)
    w_lo = (w - w_hi.astype(F32)).astype(BF16)
    return jnp.concatenate([w_hi, w_lo], axis=1), w_hi, b


def _sc_mesh():
    return plsc.VectorSubcoreMesh(core_axis_name="core", subcore_axis_name="subcore")


def _sc_worker(mesh):
    return lax.axis_index("core") * mesh.num_subcores + lax.axis_index("subcore")


def _dispatch(h2p, dest):
    mesh = _sc_mesh()
    n_win = N_TOK // SC_WINDOW
    per_worker = n_win // (mesh.num_cores * mesh.num_subcores)
    assert per_worker * mesh.num_cores * mesh.num_subcores == n_win

    @pl.kernel(out_type=jax.ShapeDtypeStruct((N_ROWS,) + ROW_TILE, U32), mesh=mesh,
               scratch_types=[pltpu.VMEM((SC_WINDOW,), I32), pltpu.VMEM((SC_WINDOW,) + ROW_TILE, U32)])
    def scatter_rows(h_hbm, dest_hbm, xs_hbm, idx_v, rows_v):
        first = _sc_worker(mesh) * per_worker

        @pl.loop(0, per_worker)
        def _(w):
            win = first + w
            pltpu.sync_copy(h_hbm.at[pl.ds(win * SC_WINDOW, SC_WINDOW)], rows_v)
            for j in range(TOPK):
                pltpu.sync_copy(dest_hbm.at[j, win], idx_v)
                pltpu.sync_copy(rows_v, xs_hbm.at[idx_v])

    return scatter_rows(h2p, dest)


def _collect(ys, dest):
    mesh = _sc_mesh()
    n_tok = dest.shape[1]
    n_win = TOPK * n_tok // SC_WINDOW
    per_worker = n_win // (mesh.num_cores * mesh.num_subcores)
    assert per_worker * mesh.num_cores * mesh.num_subcores == n_win

    @pl.kernel(out_type=jax.ShapeDtypeStruct((TOPK * n_tok,) + ROW_TILE, U32), mesh=mesh,
               scratch_types=[pltpu.VMEM((SC_WINDOW,), I32), pltpu.VMEM((SC_WINDOW,) + ROW_TILE, U32)])
    def gather_rows(ys_hbm, dest_hbm, yg_hbm, idx_v, rows_v):
        first = _sc_worker(mesh) * per_worker

        @pl.loop(0, per_worker)
        def _(w):
            win = first + w
            pltpu.sync_copy(dest_hbm.at[win], idx_v)
            pltpu.sync_copy(ys_hbm.at[idx_v], rows_v)
            pltpu.sync_copy(rows_v, yg_hbm.at[pl.ds(win * SC_WINDOW, SC_WINDOW)])

    return gather_rows(ys, dest.reshape(n_win, SC_WINDOW)).reshape((TOPK, n_tok) + ROW_TILE)


def _expert_kernel(first_ref, nblk_ref, nvalid_ref, nused_ref, xs_hbm, wg_ref, wu_ref, wd_ref, ys_hbm,
                   wg_scr, wu_scr, wd_scr, x_buf, y_buf, in_sem, out_sem):
    e = pl.program_id(0)
    nused = nused_ref[0]

    def in_copy(g):
        slot = lax.rem(g, IN_SLOTS)
        return pltpu.make_async_copy(xs_hbm.at[pl.ds(g * BM, BM)], x_buf.at[slot], in_sem.at[slot])

    def out_copy(g, slot):
        return pltpu.make_async_copy(y_buf.at[slot], ys_hbm.at[pl.ds(g * BM, BM)], out_sem.at[slot])

    @pl.when(e == 0)
    def _first():
        for g in range(IN_AHEAD):
            in_copy(g).start()

    wg_scr[...] = wg_ref[0].astype(BF16)
    wu_scr[...] = wu_ref[0].astype(BF16)
    wd_scr[...] = wd_ref[0].astype(BF16)

    def block(b, carry):
        g = first_ref[e] + b
        slot = lax.rem(g, 2)
        in_copy(g).wait()

        @pl.when(g + IN_AHEAD < nused)
        def _prefetch():
            in_copy(g + IN_AHEAD).start()

        @pl.when(g >= 2)
        def _slot_free():
            out_copy(g - 2, slot).wait()

        valid = lax.broadcasted_iota(I32, (BM, 1), 0) < nvalid_ref[g]
        x_blk = x_buf[lax.rem(g, IN_SLOTS)].reshape(BM, HALF)
        lo, hi = _unpack_bf16_pair(jnp.where(valid, x_blk, jnp.uint32(0)))
        lo = lo.astype(BF16)
        hi = hi.astype(BF16)
        gate = jnp.dot(lo, wg_scr[0:HALF, :], preferred_element_type=F32) \
            + jnp.dot(hi, wg_scr[HALF:, :], preferred_element_type=F32)
        up = jnp.dot(lo, wu_scr[0:HALF, :], preferred_element_type=F32) \
            + jnp.dot(hi, wu_scr[HALF:, :], preferred_element_type=F32)
        act = (jax.nn.silu(gate) * up).astype(BF16)
        o = jnp.dot(act, wd_scr[...], preferred_element_type=F32)
        y_buf[slot] = _pack_bf16_pair(o[:, 0:HALF], o[:, HALF:]).reshape((BM,) + ROW_TILE)
        out_copy(g, slot).start()
        return carry

    lax.fori_loop(0, nblk_ref[e], block, 0)

    @pl.when(e == N_EXPERTS - 1)
    def _drain():
        out_copy(nused - 2, lax.rem(nused, 2)).wait()
        out_copy(nused - 1, 1 - lax.rem(nused, 2)).wait()


def _experts(first, nblk, nvalid, nused, xs, wg, wu, wd):
    grid_spec = pltpu.PrefetchScalarGridSpec(
        num_scalar_prefetch=4,
        grid=(N_EXPERTS,),
        in_specs=[
            pl.BlockSpec(memory_space=pl.ANY),
            pl.BlockSpec((1, D_MODEL, D_EXPERT), lambda e, *_: (e, 0, 0)),
            pl.BlockSpec((1, D_MODEL, D_EXPERT), lambda e, *_: (e, 0, 0)),
            pl.BlockSpec((1, D_EXPERT, D_MODEL), lambda e, *_: (e, 0, 0)),
        ],
        out_specs=pl.BlockSpec(memory_space=pl.ANY),
        scratch_shapes=[
            pltpu.VMEM((D_MODEL, D_EXPERT), BF16),
            pltpu.VMEM((D_MODEL, D_EXPERT), BF16),
            pltpu.VMEM((D_EXPERT, D_MODEL), BF16),
            pltpu.VMEM((IN_SLOTS, BM) + ROW_TILE, U32),
            pltpu.VMEM((2, BM) + ROW_TILE, U32),
            pltpu.SemaphoreType.DMA((IN_SLOTS,)),
            pltpu.SemaphoreType.DMA((2,)),
        ],
    )
    return pl.pallas_call(
        _expert_kernel,
        grid_spec=grid_spec,
        out_shape=jax.ShapeDtypeStruct((N_ROWS,) + ROW_TILE, U32),
        compiler_params=pltpu.CompilerParams(
            dimension_semantics=("arbitrary",), vmem_limit_bytes=VMEM_LIMIT),
        name="experts",
    )(first, nblk, nvalid, nused, xs, wg, wu, wd)


def _combine_kernel(x1_ref, rec_ref, yg_ref, p_ref, gple_ref, wpg_ref, wple_ref, gfin_ref, *rest):
    out_ref = rest[-1]
    ple = jnp.dot(p_ref[0].reshape(TM, D_PLE).astype(BF16), wple_ref[...], preferred_element_type=F32)
    rec = rec_ref[...]
    w0 = rec[:, REC_W0:REC_W0 + 1]
    w1 = rec[:, REC_W1:REC_W1 + 1]
    lo0, hi0 = _unpack_bf16_pair(yg_ref[0].reshape(TM, HALF))
    lo1, hi1 = _unpack_bf16_pair(yg_ref[1].reshape(TM, HALF))
    moe = jnp.concatenate([lo0 * w0 + lo1 * w1, hi0 * w0 + hi1 * w1], axis=1)
    x2 = x1_ref[...].reshape(TM, D_MODEL) + moe
    gate = jax.nn.sigmoid(jnp.dot(_rms(x2, gple_ref[...]).astype(BF16), wpg_ref[...],
                                  preferred_element_type=F32))
    x3 = x2 + gate * ple
    out_ref[...] = _rms(x3, gfin_ref[...]).reshape(BATCH, TT, D_MODEL)


def _combine(s0, n_steps, x1, rec, yg, p, gple, wpg, wple, gfin, out_prev=None):
    seq_spec = pl.BlockSpec((BATCH, TT, D_MODEL), lambda i: (0, s0 + i, 0))
    in_specs = [
        seq_spec,
        pl.BlockSpec((TM, LANES), lambda i: (s0 + i, 0)),
        pl.BlockSpec((TOPK, TM) + ROW_TILE, lambda i: (0, i, 0, 0)),
        pl.BlockSpec((1, BATCH, TT, D_PLE), lambda i: (0, 0, s0 + i, 0)),
        _const_spec((1, D_MODEL)),
        _const_spec((D_MODEL, D_MODEL)),
        _const_spec((D_PLE, D_MODEL)),
        _const_spec((1, D_MODEL)),
    ]
    args = [x1, rec, yg, p, gple, wpg, wple, gfin]
    aliases = {}
    if out_prev is not None:
        in_specs.append(pl.BlockSpec(memory_space=pl.ANY))
        args.append(out_prev)
        aliases = {len(args) - 1: 0}
    return pl.pallas_call(
        _combine_kernel,
        grid=(n_steps,),
        in_specs=in_specs,
        out_specs=seq_spec,
        out_shape=jax.ShapeDtypeStruct((BATCH, SEQ, D_MODEL), F32),
        input_output_aliases=aliases,
        compiler_params=pltpu.CompilerParams(
            dimension_semantics=("arbitrary",), vmem_limit_bytes=VMEM_LIMIT),
        name="combine",
    )(*args)


def kernel(x, p, g_mix, w_in, b_gate, ssm_a_re, ssm_a_im, ssm_log_dt, ssm_b_re, ssm_b_im, ssm_c_re,
           ssm_c_im, ssm_d, w_glu, conv_dw, conv_dw_b, conv_ln_g, conv_ln_b, w_conv_out, w_out, g_moe,
           w_router_group, b_router_group, w_router_expert, b_router_expert, w_exp_gate, w_exp_up,
           w_exp_down, g_ple, w_ple_gate, w_ple, g_final):
    assert x.shape == (BATCH, SEQ, D_MODEL) and p.shape == (1, BATCH, SEQ, D_PLE)
    row = lambda v: v.reshape(1, -1)

    um, up, ur, a_re, a_im = _ssm_matrices(ssm_a_re[0], ssm_a_im[0], ssm_log_dt[0], ssm_b_re[0],
                                           ssm_b_im[0], ssm_c_re[0], ssm_c_im[0])
    wr1, wr2, br = _router_weights(w_router_group[0], b_router_group[0], w_router_expert[0],
                                   b_router_expert[0])
    x1, h2p, rec, rect, cnt = _mixer(
        x, row(g_mix[0]), w_in[0], row(b_gate[0]), um, up, ur, a_re, a_im,
        row(ssm_d[0]), w_glu[0], conv_dw[0], row(conv_dw_b[0]), row(conv_ln_g[0]),
        row(conv_ln_b[0]), w_conv_out[0], w_out[0], row(g_moe[0]), wr1, wr2, br)

    counts = cnt[0, LANE_EXP0:LANE_EXP0 + N_EXPERTS].astype(I32)
    pcounts = (counts + BM - 1) // BM * BM
    pends = jnp.cumsum(pcounts)
    pstarts = pends - pcounts
    eid = rect[REC_EID0:REC_EID1 + 1].astype(I32)
    rank = rect[REC_RANK0:REC_RANK1 + 1].astype(I32)
    dest = (jnp.sum(jnp.where(eid[..., None] == jnp.arange(N_EXPERTS, dtype=I32), pstarts, 0), axis=-1)
            + rank).reshape(TOPK, N_TOK // SC_WINDOW, SC_WINDOW)
    nused = (pends[-1] // BM).astype(I32)
    blk = jnp.arange(N_BLK, dtype=I32)[:, None] * BM
    in_expert = (pstarts[None, :] <= blk) & (blk < pends[None, :])
    nvalid = jnp.clip(jnp.sum(jnp.where(in_expert, (pstarts + counts)[None, :] - blk, 0), axis=1), 0, BM)

    xs = _dispatch(h2p, dest)
    ys = _experts(pstarts // BM, pcounts // BM, nvalid.astype(I32), nused.reshape(1), xs,
                  w_exp_gate[0], w_exp_up[0], w_exp_down[0])
    dest_tok = dest.reshape(TOPK, N_TOK)
    wpg = w_ple_gate[0].astype(BF16)
    wple = w_ple[0].astype(BF16)
    out = None
    s0 = 0
    for n_steps in PART_STEPS:
        yg = _collect(ys, dest_tok[:, s0 * TM:(s0 + n_steps) * TM])
        out = _combine(s0, n_steps, x1, rec, yg, p, row(g_ple[0]), wpg, wple, row(g_final), out)
        s0 += n_steps
    return out
```

```python
import jax
import jax.numpy as jnp
from jax import lax
from jax.experimental import pallas as pl
from jax.experimental.pallas import tpu as pltpu
from jax.experimental.pallas import tpu_sc as plsc

F32 = jnp.float32
BF16 = jnp.bfloat16
U32 = jnp.uint32
I32 = jnp.int32

D_MODEL = 1024
BATCH = 8
SEQ = 2048
N_TOK = BATCH * SEQ
D_SSM = 512
SSM_GROUP_WIDTH = 16
SSM_GROUPS = 32
SSM_STATE = 64
D_CONV = 512
CONV_WIDTH = 31
D_IN = D_SSM + 2 * D_CONV + 2 * D_MODEL
N_GROUPS_MOE = 4
EXPERTS_PER_GROUP = 8
N_EXPERTS = 32
TOPK = 2
D_EXPERT = 512
D_PLE = 256
EPS = 1e-6

SUBLANES = 8
LANES = 128
assert BATCH == SUBLANES

TT = 64
TM = TT * BATCH
N_STEP = SEQ // TT
SB = 512
NSB = TM // SB
BPS = SB // TT
Q = 2
N_SLAB = D_SSM // LANES
GROUPS_PER_SLAB = SSM_GROUPS // N_SLAB
ROWS_Z = TM // Q
STATE_LANES = 2 * GROUPS_PER_SLAB * SSM_STATE
HALO = (CONV_WIDTH - 1) * BATCH
CHUNK_ROWS = SB // (Q * SUBLANES)
W_STAGE_ROWS = 64
W_STAGE_SLOTS = 4
CONV_ROWS = 64
N_LC = D_CONV // LANES

LANE_GRP0 = 0
LANE_EXP0 = 32
REC_EID0, REC_EID1, REC_W0, REC_W1, REC_RANK0, REC_RANK1 = 0, 1, 2, 3, 4, 5
REC_ROWS = 8

BM = 256
N_BLK = (TOPK * N_TOK + N_EXPERTS * (BM - 1) + BM - 1) // BM
N_ROWS = N_BLK * BM
HALF = D_MODEL // 2
ROW_TILE = (HALF // LANES, LANES)
SC_WINDOW = 128
IN_AHEAD = 3
IN_SLOTS = IN_AHEAD + 1
PART_STEPS = (12, 20)
assert sum(PART_STEPS) == N_STEP

VMEM_LIMIT = 56 * 1024 * 1024
MIXER_VMEM_LIMIT = 60 * 1024 * 1024


def _const_spec(shape):
    n = len(shape)
    return pl.BlockSpec(shape, lambda *_: (0,) * n, pipeline_mode=pl.Buffered(1))


def _rms(x, g):
    ms = jnp.mean(x * x, axis=-1, keepdims=True)
    return x * lax.rsqrt(ms + EPS) * g


def _pack_bf16_pair(lo, hi):
    ulo = lax.bitcast_convert_type(lo.astype(BF16).astype(F32), U32)
    uhi = lax.bitcast_convert_type(hi.astype(BF16).astype(F32), U32)
    return (ulo >> 16) | (uhi & jnp.uint32(0xFFFF0000))


def _unpack_bf16_pair(w):
    lo = lax.bitcast_convert_type(w << 16, F32)
    hi = lax.bitcast_convert_type(w & jnp.uint32(0xFFFF0000), F32)
    return lo, hi


def _load_weights_bf16(pairs, stage, sem):
    chunks = [(src, dst, r0) for src, dst in pairs for r0 in range(0, src.shape[0], W_STAGE_ROWS)]

    def copy(c):
        src, _, r0 = chunks[c]
        slot = c % W_STAGE_SLOTS
        return pltpu.make_async_copy(src.at[pl.ds(r0, W_STAGE_ROWS)],
                                     stage.at[slot, :, 0:src.shape[1]], sem.at[slot])

    for c in range(W_STAGE_SLOTS):
        copy(c).start()
    for c, (src, dst, r0) in enumerate(chunks):
        copy(c).wait()
        dst[r0:r0 + W_STAGE_ROWS, :] = stage[c % W_STAGE_SLOTS, :, 0:src.shape[1]].astype(BF16)
        if c + W_STAGE_SLOTS < len(chunks):
            copy(c + W_STAGE_SLOTS).start()


def _expand_ssm(um_ref, up_ref, ur_ref, mp_ref, r_ref):
    gw, ns, half = SSM_GROUP_WIDTH, SSM_STATE, STATE_LANES // 2
    div = lambda a, n: lax.shift_right_logical(a, n.bit_length() - 1)
    mod = lambda a, n: a & (n - 1)
    iota2 = lambda shape: (lax.broadcasted_iota(I32, shape, 0), lax.broadcasted_iota(I32, shape, 1))
    one = lambda cond: jnp.where(cond, 1.0, 0.0).astype(BF16)

    r, q = iota2((Q * gw, Q * LANES))
    x_m = one((div(r, gw) == div(q, LANES)) & (mod(r, gw) == mod(q, gw)))
    r, q = iota2((2 * ns, STATE_LANES))
    x_p = one((div(r, ns) == div(q, half)) & (mod(r, ns) == mod(q, ns)))
    p, r = iota2((STATE_LANES, 2 * ns))
    x_r = one((div(p, half) == div(r, ns)) & (mod(p, ns) == mod(r, ns)))
    p, q = iota2((Q * LANES, Q * LANES))
    same_m = div(mod(p, LANES), gw) == div(mod(q, LANES), gw)
    p, q = iota2((Q * LANES, STATE_LANES))
    same_p = div(mod(p, LANES), gw) == div(mod(q, half), ns)
    p, q = iota2((STATE_LANES, Q * LANES))
    same_r = div(mod(p, half), ns) == div(mod(q, LANES), gw)
    for s in range(N_SLAB):
        m = jnp.dot(um_ref[s], x_m, preferred_element_type=F32)
        mp_ref[s, :, 0:Q * LANES] = jnp.where(same_m, m, 0.0).astype(BF16)
        pm = jnp.dot(up_ref[s], x_p, preferred_element_type=F32)
        mp_ref[s, :, Q * LANES:] = jnp.where(same_p, pm, 0.0).astype(BF16)
        rm = jnp.dot(x_r, ur_ref[s], preferred_element_type=F32)
        r_ref[s] = jnp.where(same_r, rm, 0.0).astype(BF16)


def _mixer_kernel(x_ref, gmix_ref, win_hbm, bgate_ref, um_ref, up_ref, ur_ref, are_ref,
                  aim_ref, d_ref, wglu_hbm, dw_ref, dwb_ref, lng_ref, lnb_ref, wco_hbm, wout_hbm,
                  gmoe_ref, wr1_ref, wr2_ref, br_ref, wdown_ref,
                  x1_ref, h2p_ref, rec_ref, rect_ref, cnt_ref, wdown_bf_ref,
                  hb_scr, ht_scr, u_scr, y_scr, yi_scr, xs_scr, z_scr, conv_scr, act_scr, actb_scr,
                  logit_scr, s_scr, cnt_scr, win_ref, wglu_ref, wco_ref, wout_ref, wstage_scr, wstage_sem,
                  mp_ref, r_ref):
    step = pl.program_id(0)
    assert NSB == 1

    @pl.when(step == 0)
    def _init():
        logit_scr[...] = jnp.zeros(logit_scr.shape, F32)
        z_scr[:, 0:HALO, :] = jnp.zeros((N_LC, HALO, LANES), F32)
        s_scr[...] = jnp.zeros(s_scr.shape, F32)
        cnt_scr[...] = jnp.zeros(cnt_scr.shape, F32)
        _expand_ssm(um_ref, up_ref, ur_ref, mp_ref, r_ref)
        _load_weights_bf16([(win_hbm, win_ref), (wglu_hbm, wglu_ref), (wco_hbm, wco_ref),
                            (wout_hbm, wout_ref)], wstage_scr, wstage_sem)

    def sub_rows(r):
        return pl.ds(pl.multiple_of(r * SB, SB), SB)

    def phase_a(r, carry):
        xb = x_ref[pl.ds(r * BPS, BPS)].reshape(SB, D_MODEL)
        hb_scr[sub_rows(r), :] = _rms(xb, gmix_ref[...]).astype(BF16)
        return carry

    def phase_a3(r, carry):
        h = ht_scr[sub_rows(r), :]
        u = jnp.dot(h, win_ref[:, 0:D_SSM], preferred_element_type=F32)
        u_scr[pl.ds(r * CHUNK_ROWS, CHUNK_ROWS)] = u.reshape(CHUNK_ROWS, Q, SUBLANES, D_SSM)
        v = jnp.dot(h, win_ref[:, D_SSM:D_SSM + 2 * D_CONV], preferred_element_type=F32)
        zc = v[:, 0:D_CONV] * jax.nn.sigmoid(v[:, D_CONV:])
        for lc in range(N_LC):
            z_scr[lc, pl.ds(pl.multiple_of(HALO + r * SB, SUBLANES), SB), :] = zc[:, lc * LANES:(lc + 1) * LANES]
        return carry

    def phase_b():
        for s in range(N_SLAB):
            lanes = slice(s * LANES, (s + 1) * LANES)
            z = jnp.concatenate(
                [u_scr[:, i, :, lanes].reshape(ROWS_Z, LANES) for i in range(Q)], axis=1).astype(BF16)
            xp = jnp.dot(z, mp_ref[s], preferred_element_type=F32)
            yi_scr[s] = xp[:, 0:Q * LANES]
            xs_scr[s] = xp[:, Q * LANES:]

        half = STATE_LANES // 2
        for s in range(N_SLAB):
            a_re = jnp.broadcast_to(are_ref[s:s + 1, :], (SUBLANES, half))
            a_im = jnp.broadcast_to(aim_ref[s:s + 1, :], (SUBLANES, half))

            def scan_body(k, carry, s=s, a_re=a_re, a_im=a_im):
                s_re, s_im = carry
                rows = pl.ds(pl.multiple_of(k * SUBLANES, SUBLANES), SUBLANES)
                x_re = xs_scr[s, rows, 0:half]
                x_im = xs_scr[s, rows, half:]
                xs_scr[s, rows, 0:half] = s_re
                xs_scr[s, rows, half:] = s_im
                n_re = a_re * s_re - a_im * s_im + x_re
                n_im = a_re * s_im + a_im * s_re + x_im
                return n_re, n_im

            s_re, s_im = lax.fori_loop(0, ROWS_Z // SUBLANES, scan_body,
                                       (s_scr[s, :, 0:half], s_scr[s, :, half:]), unroll=True)
            s_scr[s, :, 0:half] = s_re
            s_scr[s, :, half:] = s_im

        for s in range(N_SLAB):
            lanes = slice(s * LANES, (s + 1) * LANES)
            y_tot = yi_scr[s] + jnp.dot(xs_scr[s].astype(BF16), r_ref[s], preferred_element_type=F32)
            for j in range(Q):
                y_scr[:, j, :, lanes] = y_tot[:, j * LANES:(j + 1) * LANES].reshape(
                    ROWS_Z // SUBLANES, SUBLANES, LANES)

    def phase_c1(r, carry):
        rows = sub_rows(r)
        crow = pl.ds(r * CHUNK_ROWS, CHUNK_ROWS)
        y = y_scr[crow].reshape(SB, D_SSM) + d_ref[...] * u_scr[crow].reshape(SB, D_SSM)
        act_scr[rows, 0:D_SSM] = jax.nn.gelu(y).astype(BF16)
        for lc in range(N_LC):
            lanes = slice(lc * LANES, (lc + 1) * LANES)

            def conv_piece(rc, c, lc=lc, lanes=lanes):
                r0 = r * SB + rc * CONV_ROWS
                piece = jnp.broadcast_to(dwb_ref[:, lanes], (CONV_ROWS, LANES))
                for j in range(CONV_WIDTH):
                    zrows = pl.ds(pl.multiple_of(r0 + j * BATCH, SUBLANES), CONV_ROWS)
                    piece = piece + dw_ref[j:j + 1, lanes] * z_scr[lc, zrows, :]
                conv_scr[pl.ds(pl.multiple_of(rc * CONV_ROWS, CONV_ROWS), CONV_ROWS), lanes] = piece
                return c

            lax.fori_loop(0, SB // CONV_ROWS, conv_piece, 0, unroll=4)
        acc = conv_scr[...]
        mu = jnp.mean(acc, axis=-1, keepdims=True)
        cen = acc - mu
        var = jnp.mean(cen * cen, axis=-1, keepdims=True)
        ln = cen * lax.rsqrt(var + EPS) * lng_ref[...] + lnb_ref[...]
        act_scr[rows, D_SSM:] = jax.nn.silu(ln).astype(BF16)
        return carry

    lane = lax.broadcasted_iota(I32, (1, LANES), 1).astype(F32)
    grp_mask = lane < float(N_GROUPS_MOE)
    exp_lane = (lane >= float(LANE_EXP0)) & (lane < float(LANE_EXP0 + N_EXPERTS))
    lane_grp = jnp.floor((lane - float(LANE_EXP0)) * (1.0 / EXPERTS_PER_GROUP))
    tri = (lax.broadcasted_iota(I32, (SB, SB), 0) > lax.broadcasted_iota(I32, (SB, SB), 1)).astype(BF16)
    neg_inf = float("-inf")
    big = float(4 * LANES)

    def phase_c3(r, carry):
        rows = sub_rows(r)
        h = hb_scr[rows, :]
        g0 = D_SSM + 2 * D_CONV
        gate_ssm = jnp.dot(h, win_ref[:, g0:g0 + D_MODEL], preferred_element_type=F32) \
            + bgate_ref[:, 0:D_MODEL]
        gate_conv = jnp.dot(h, win_ref[:, g0 + D_MODEL:], preferred_element_type=F32) \
            + bgate_ref[:, D_MODEL:]
        zz = jnp.dot(actb_scr[rows, 0:D_SSM], wglu_ref[...], preferred_element_type=F32)
        y_ssm = zz[:, 0:D_MODEL] * jax.nn.sigmoid(zz[:, D_MODEL:])
        y_conv = jnp.dot(actb_scr[rows, D_SSM:], wco_ref[...], preferred_element_type=F32)

        merged = jax.nn.sigmoid(gate_ssm) * y_ssm + jax.nn.sigmoid(gate_conv) * y_conv
        xb = x_ref[pl.ds(r * BPS, BPS)].reshape(SB, D_MODEL)
        x1 = xb + jnp.dot(merged.astype(BF16), wout_ref[...], preferred_element_type=F32)
        x1_ref[pl.ds(r * BPS, BPS)] = x1.reshape(BPS, TT, D_MODEL)

        h2 = _rms(x1, gmoe_ref[...])
        h2p_ref[rows] = _pack_bf16_pair(h2[:, 0:HALF], h2[:, HALF:]).reshape((SB,) + ROW_TILE)

        h2_hi = h2.astype(BF16)
        h2_lo = (h2 - h2_hi.astype(F32)).astype(BF16)
        l1 = jnp.dot(h2_hi, wr1_ref[...], preferred_element_type=F32)
        l2 = jnp.dot(h2_lo, wr2_ref[...], preferred_element_type=F32)
        logit_scr[rows, :] = l1[:, 0:LANES] + l1[:, LANES:] + l2 + br_ref[...]
        return carry

    def route_previous():
        rows = sub_rows(0)
        logits = logit_scr[...]
        counted = jnp.where(step > 0, 1.0, 0.0)

        lg = jnp.where(grp_mask, logits, neg_inf)
        g_max = jnp.max(lg, axis=-1, keepdims=True)
        g_sel = jnp.min(jnp.where(lg == g_max, lane, big), axis=-1, keepdims=True)
        p_g = 1.0 / jnp.sum(jnp.where(grp_mask, jnp.exp(logits - g_max), 0.0), axis=-1, keepdims=True)
        le = jnp.where(exp_lane & (lane_grp == g_sel), logits, neg_inf)
        m1 = jnp.max(le, axis=-1, keepdims=True)
        i1 = jnp.min(jnp.where(le == m1, lane, big), axis=-1, keepdims=True)
        le2 = jnp.where(lane == i1, neg_inf, le)
        m2 = jnp.max(le2, axis=-1, keepdims=True)
        i2 = jnp.min(jnp.where(le2 == m2, lane, big), axis=-1, keepdims=True)
        e2 = jnp.exp(m2 - m1)
        den = 1.0 + e2
        w_a = (1.0 / den) * p_g
        w_b = (e2 / den) * p_g

        sel1 = lane == i1
        sel2 = lane == i2
        onehot = jnp.where(sel1 | sel2, counted, 0.0)
        prefix = jnp.dot(tri, onehot.astype(BF16), preferred_element_type=F32) + cnt_scr[...]
        rank_a = jnp.sum(jnp.where(sel1, prefix, 0.0), axis=-1, keepdims=True)
        rank_b = jnp.sum(jnp.where(sel2, prefix, 0.0), axis=-1, keepdims=True)
        cnt_scr[...] = cnt_scr[...] + jnp.sum(onehot, axis=0, keepdims=True)

        rec = jnp.where(lane == float(REC_EID0), i1 - float(LANE_EXP0), 0.0)
        rec = jnp.where(lane == float(REC_EID1), i2 - float(LANE_EXP0), rec)
        rec = jnp.where(lane == float(REC_W0), w_a, rec)
        rec = jnp.where(lane == float(REC_W1), w_b, rec)
        rec = jnp.where(lane == float(REC_RANK0), rank_a, rec)
        rec = jnp.where(lane == float(REC_RANK1), rank_b, rec)
        rec_ref[rows, :] = rec
        rect_ref[...] = jnp.transpose(rec)[0:REC_ROWS, :]
        cnt_ref[...] = cnt_scr[...]

    @pl.when(step < N_STEP)
    def _tile():
        wdown_bf_ref[...] = wdown_ref[...].astype(BF16)
        route_previous()
        phase_a(0, 0)
        ht_scr[...] = jnp.swapaxes(hb_scr[...].reshape(BATCH, TT, D_MODEL), 0, 1).reshape(TM, D_MODEL)
        phase_a3(0, 0)
        phase_b()
        phase_c1(0, 0)
        z_scr[:, 0:HALO, :] = z_scr[:, TM:TM + HALO, :]
        actb_scr[...] = jnp.swapaxes(act_scr[...].reshape(TT, BATCH, D_SSM + D_CONV), 0, 1).reshape(
            TM, D_SSM + D_CONV)
        phase_c3(0, 0)

    @pl.when(step == N_STEP)
    def _last():
        route_previous()


def _mixer(x, gmix, win, bgate, um, up, ur, a_re, a_im, dvec, wglu, dw, dwb, lng, lnb, wco,
           wout, gmoe, wr1, wr2, br, wdown):
    assert wdown.shape == (N_STEP, D_EXPERT, D_MODEL)
    tile = lambda i: jnp.minimum(i, N_STEP - 1)
    wdown_spec = pl.BlockSpec((1, D_EXPERT, D_MODEL), lambda i: (tile(i), 0, 0))
    routed = lambda i: jnp.maximum(i - 1, 0)
    seq_spec = pl.BlockSpec((BATCH, TT, D_MODEL), lambda i: (0, tile(i), 0))
    in_hbm = pl.BlockSpec(memory_space=pl.ANY)
    in_specs = [
        seq_spec,
        _const_spec((1, D_MODEL)),
        in_hbm,
        _const_spec((1, 2 * D_MODEL)),
        _const_spec(um.shape),
        _const_spec(up.shape),
        _const_spec(ur.shape),
        _const_spec(a_re.shape),
        _const_spec(a_im.shape),
        _const_spec((1, D_SSM)),
        in_hbm,
        _const_spec((CONV_WIDTH, D_CONV)),
        _const_spec((1, D_CONV)),
        _const_spec((1, D_CONV)),
        _const_spec((1, D_CONV)),
        in_hbm,
        in_hbm,
        _const_spec((1, D_MODEL)),
        _const_spec((D_MODEL, 2 * LANES)),
        _const_spec((D_MODEL, LANES)),
        _const_spec((1, LANES)),
        wdown_spec,
    ]
    out_specs = [
        seq_spec,
        pl.BlockSpec((TM,) + ROW_TILE, lambda i: (tile(i), 0, 0)),
        pl.BlockSpec((TM, LANES), lambda i: (routed(i), 0)),
        pl.BlockSpec((REC_ROWS, TM), lambda i: (0, routed(i))),
        pl.BlockSpec((1, LANES), lambda i: (0, 0)),
        wdown_spec,
    ]
    out_shape = [
        jax.ShapeDtypeStruct((BATCH, SEQ, D_MODEL), F32),
        jax.ShapeDtypeStruct((N_TOK,) + ROW_TILE, U32),
        jax.ShapeDtypeStruct((N_TOK, LANES), F32),
        jax.ShapeDtypeStruct((REC_ROWS, N_TOK), F32),
        jax.ShapeDtypeStruct((1, LANES), F32),
        jax.ShapeDtypeStruct(wdown.shape, BF16),
    ]
    chunk_shape = (ROWS_Z // SUBLANES, Q, SUBLANES, D_SSM)
    scratch = [
        pltpu.VMEM((TM, D_MODEL), BF16),
        pltpu.VMEM((TM, D_MODEL), BF16),
        pltpu.VMEM(chunk_shape, F32),
        pltpu.VMEM(chunk_shape, F32),
        pltpu.VMEM((N_SLAB, ROWS_Z, Q * LANES), F32),
        pltpu.VMEM((N_SLAB, ROWS_Z, STATE_LANES), F32),
        pltpu.VMEM((N_LC, HALO + TM, LANES), F32),
        pltpu.VMEM((SB, D_CONV), F32),
        pltpu.VMEM((TM, D_SSM + D_CONV), BF16),
        pltpu.VMEM((TM, D_SSM + D_CONV), BF16),
        pltpu.VMEM((TM, LANES), F32),
        pltpu.VMEM((N_SLAB, SUBLANES, STATE_LANES), F32),
        pltpu.VMEM((1, LANES), F32),
        pltpu.VMEM(win.shape, BF16),
        pltpu.VMEM(wglu.shape, BF16),
        pltpu.VMEM(wco.shape, BF16),
        pltpu.VMEM(wout.shape, BF16),
        pltpu.VMEM((W_STAGE_SLOTS, W_STAGE_ROWS, D_IN), F32),
        pltpu.SemaphoreType.DMA((W_STAGE_SLOTS,)),
        pltpu.VMEM((N_SLAB, Q * LANES, Q * LANES + STATE_LANES), BF16),
        pltpu.VMEM((N_SLAB, STATE_LANES, Q * LANES), BF16),
    ]
    assert win.shape == (D_MODEL, D_IN) and wglu.shape == (D_SSM, 2 * D_MODEL)
    assert wco.shape == (D_CONV, D_MODEL) and wout.shape == (D_MODEL, D_MODEL)
    return pl.pallas_call(
        _mixer_kernel,
        grid=(N_STEP + 1,),
        in_specs=in_specs,
        out_specs=out_specs,
        out_shape=out_shape,
        scratch_shapes=scratch,
        compiler_params=pltpu.CompilerParams(
            dimension_semantics=("arbitrary",), vmem_limit_bytes=MIXER_VMEM_LIMIT),
        name="mixer",
    )(x, gmix, win, bgate, um, up, ur, a_re, a_im, dvec, wglu, dw, dwb, lng, lnb, wco, wout,
      gmoe, wr1, wr2, br, wdown)


def _cmul(a, b):
    return a[0] * b[0] - a[1] * b[1], a[0] * b[1] + a[1] * b[0]


def _ssm_matrices(a_re, a_im, log_dt, b_re, b_im, c_re, c_im):
    dt = jnp.exp(log_dt)[:, None]
    mag = jnp.exp(a_re * dt)
    lam = (mag * jnp.cos(a_im * dt), mag * jnp.sin(a_im * dt))
    den = a_re * a_re + a_im * a_im
    nr = lam[0] - 1.0
    ni = lam[1]
    z_re = (nr * a_re + ni * a_im) / den
    z_im = (ni * a_re - nr * a_im) / den
    bbar = (z_re[..., None] * b_re - z_im[..., None] * b_im,
            z_re[..., None] * b_im + z_im[..., None] * b_re)
    pw = [(jnp.ones_like(lam[0]), jnp.zeros_like(lam[0])), lam]
    for _ in range(2, Q + 1):
        pw.append(_cmul(pw[-1], lam))
    e = [(c_re * p[0][:, None, :] - c_im * p[1][:, None, :],
          c_re * p[1][:, None, :] + c_im * p[0][:, None, :]) for p in pw]
    e_cat = jnp.concatenate([jnp.concatenate([e[m][0], -e[m][1]], axis=-1) for m in range(Q)], axis=1)
    k_cat = jnp.einsum('gcn,gnd->gcd', e_cat, jnp.concatenate(bbar, axis=1), precision=lax.Precision.HIGHEST)
    k = [k_cat[:, m * SSM_GROUP_WIDTH:(m + 1) * SSM_GROUP_WIDTH, :] for m in range(Q)]
    split = lambda t: t.reshape((N_SLAB, GROUPS_PER_SLAB) + t.shape[1:])
    zero_k = jnp.zeros_like(k[0])
    kb = jnp.stack([jnp.stack([split(jnp.swapaxes(k[j - i] if j >= i else zero_k, 1, 2))
                               for j in range(Q)]) for i in range(Q)])
    um = jnp.transpose(kb, (2, 0, 3, 4, 1, 5)).reshape(N_SLAB, Q * LANES, Q * SSM_GROUP_WIDTH)
    f = [_cmul((pw[Q - 1 - i][0][..., None], pw[Q - 1 - i][1][..., None]), bbar) for i in range(Q)]
    fs = jnp.stack([jnp.stack([split(f[i][part]) for i in range(Q)]) for part in range(2)])
    up = jnp.transpose(fs, (2, 1, 3, 5, 0, 4)).reshape(N_SLAB, Q * LANES, 2 * SSM_STATE)
    es = jnp.stack([sign * jnp.stack([split(e[j + 1][part]) for j in range(Q)])
                    for part, sign in ((0, 1.0), (1, -1.0))])
    ur = jnp.transpose(es, (2, 0, 5, 1, 3, 4)).reshape(N_SLAB, 2 * SSM_STATE, Q * LANES)
    a_q = pw[Q]
    return (um.astype(BF16), up.astype(BF16), ur.astype(BF16),
            a_q[0].reshape(N_SLAB, STATE_LANES // 2), a_q[1].reshape(N_SLAB, STATE_LANES // 2))


def _router_weights(w_rg, b_rg, w_re, b_re):
    pad_g = LANE_EXP0 - LANE_GRP0 - N_GROUPS_MOE
    pad_e = LANES - LANE_EXP0 - N_EXPERTS
    w = jnp.concatenate([w_rg, jnp.zeros((D_MODEL, pad_g), F32), w_re, jnp.zeros((D_MODEL, pad_e), F32)], axis=1)
    b = jnp.concatenate([b_rg, jnp.zeros((pad_g,), F32), b_re, jnp.zeros((pad_e,), F32)]).reshape(1, LANES)
    w_hi = w.astype(BF16)
    w_lo = (w - w_hi.astype(F32)).astype(BF16)
    return jnp.concatenate([w_hi, w_lo], axis=1), w_hi, b


def _sc_mesh():
    return plsc.VectorSubcoreMesh(core_axis_name="core", subcore_axis_name="subcore")


def _sc_worker(mesh):
    return lax.axis_index("core") * mesh.num_subcores + lax.axis_index("subcore")


def _dispatch(h2p, dest):
    mesh = _sc_mesh()
    n_win = N_TOK // SC_WINDOW
    per_worker = n_win // (mesh.num_cores * mesh.num_subcores)
    assert per_worker * mesh.num_cores * mesh.num_subcores == n_win

    @pl.kernel(out_type=jax.ShapeDtypeStruct((N_ROWS,) + ROW_TILE, U32), mesh=mesh,
               scratch_types=[pltpu.VMEM((SC_WINDOW,), I32), pltpu.VMEM((SC_WINDOW,) + ROW_TILE, U32)])
    def scatter_rows(h_hbm, dest_hbm, xs_hbm, idx_v, rows_v):
        first = _sc_worker(mesh) * per_worker

        @pl.loop(0, per_worker)
        def _(w):
            win = first + w
            pltpu.sync_copy(h_hbm.at[pl.ds(win * SC_WINDOW, SC_WINDOW)], rows_v)
            for j in range(TOPK):
                pltpu.sync_copy(dest_hbm.at[j, win], idx_v)
                pltpu.sync_copy(rows_v, xs_hbm.at[idx_v])

    return scatter_rows(h2p, dest)


def _collect(ys, dest):
    mesh = _sc_mesh()
    n_tok = dest.shape[1]
    n_win = TOPK * n_tok // SC_WINDOW
    per_worker = n_win // (mesh.num_cores * mesh.num_subcores)
    assert per_worker * mesh.num_cores * mesh.num_subcores == n_win

    @pl.kernel(out_type=jax.ShapeDtypeStruct((TOPK * n_tok,) + ROW_TILE, U32), mesh=mesh,
               scratch_types=[pltpu.VMEM((SC_WINDOW,), I32), pltpu.VMEM((SC_WINDOW,) + ROW_TILE, U32)])
    def gather_rows(ys_hbm, dest_hbm, yg_hbm, idx_v, rows_v):
        first = _sc_worker(mesh) * per_worker

        @pl.loop(0, per_worker)
        def _(w):
            win = first + w
            pltpu.sync_copy(dest_hbm.at[win], idx_v)
            pltpu.sync_copy(ys_hbm.at[idx_v], rows_v)
            pltpu.sync_copy(rows_v, yg_hbm.at[pl.ds(win * SC_WINDOW, SC_WINDOW)])

    return gather_rows(ys, dest.reshape(n_win, SC_WINDOW)).reshape((TOPK, n_tok) + ROW_TILE)


def _expert_kernel(first_ref, nblk_ref, nvalid_ref, nused_ref, xs_hbm, wg_ref, wu_ref, wd_ref, ys_hbm,
                   wg_scr, wu_scr, x_buf, y_buf, in_sem, out_sem):
    e = pl.program_id(0)
    nused = nused_ref[0]

    def in_copy(g):
        slot = lax.rem(g, IN_SLOTS)
        return pltpu.make_async_copy(xs_hbm.at[pl.ds(g * BM, BM)], x_buf.at[slot], in_sem.at[slot])

    def out_copy(g, slot):
        return pltpu.make_async_copy(y_buf.at[slot], ys_hbm.at[pl.ds(g * BM, BM)], out_sem.at[slot])

    @pl.when(e == 0)
    def _first():
        for g in range(IN_AHEAD):
            in_copy(g).start()

    wg_scr[...] = wg_ref[0].astype(BF16)
    wu_scr[...] = wu_ref[0].astype(BF16)

    def block(b, carry):
        g = first_ref[e] + b
        slot = lax.rem(g, 2)
        in_copy(g).wait()

        @pl.when(g + IN_AHEAD < nused)
        def _prefetch():
            in_copy(g + IN_AHEAD).start()

        @pl.when(g >= 2)
        def _slot_free():
            out_copy(g - 2, slot).wait()

        valid = lax.broadcasted_iota(I32, (BM, 1), 0) < nvalid_ref[g]
        x_blk = x_buf[lax.rem(g, IN_SLOTS)].reshape(BM, HALF)
        lo, hi = _unpack_bf16_pair(jnp.where(valid, x_blk, jnp.uint32(0)))
        lo = lo.astype(BF16)
        hi = hi.astype(BF16)
        gate = jnp.dot(lo, wg_scr[0:HALF, :], preferred_element_type=F32) \
            + jnp.dot(hi, wg_scr[HALF:, :], preferred_element_type=F32)
        up = jnp.dot(lo, wu_scr[0:HALF, :], preferred_element_type=F32) \
            + jnp.dot(hi, wu_scr[HALF:, :], preferred_element_type=F32)
        act = (jax.nn.silu(gate) * up).astype(BF16)
        o = jnp.dot(act, wd_ref[0], preferred_element_type=F32)
        y_buf[slot] = _pack_bf16_pair(o[:, 0:HALF], o[:, HALF:]).reshape((BM,) + ROW_TILE)
        out_copy(g, slot).start()
        return carry

    lax.fori_loop(0, nblk_ref[e], block, 0)

    @pl.when(e == N_EXPERTS - 1)
    def _drain():
        out_copy(nused - 2, lax.rem(nused, 2)).wait()
        out_copy(nused - 1, 1 - lax.rem(nused, 2)).wait()


def _experts(first, nblk, nvalid, nused, xs, wg, wu, wd):
    grid_spec = pltpu.PrefetchScalarGridSpec(
        num_scalar_prefetch=4,
        grid=(N_EXPERTS,),
        in_specs=[
            pl.BlockSpec(memory_space=pl.ANY),
            pl.BlockSpec((1, D_MODEL, D_EXPERT), lambda e, *_: (e, 0, 0)),
            pl.BlockSpec((1, D_MODEL, D_EXPERT), lambda e, *_: (e, 0, 0)),
            pl.BlockSpec((1, D_EXPERT, D_MODEL), lambda e, *_: (e, 0, 0)),
        ],
        out_specs=pl.BlockSpec(memory_space=pl.ANY),
        scratch_shapes=[
            pltpu.VMEM((D_MODEL, D_EXPERT), BF16),
            pltpu.VMEM((D_MODEL, D_EXPERT), BF16),
            pltpu.VMEM((IN_SLOTS, BM) + ROW_TILE, U32),
            pltpu.VMEM((2, BM) + ROW_TILE, U32),
            pltpu.SemaphoreType.DMA((IN_SLOTS,)),
            pltpu.SemaphoreType.DMA((2,)),
        ],
    )
    return pl.pallas_call(
        _expert_kernel,
        grid_spec=grid_spec,
        out_shape=jax.ShapeDtypeStruct((N_ROWS,) + ROW_TILE, U32),
        compiler_params=pltpu.CompilerParams(
            dimension_semantics=("arbitrary",), vmem_limit_bytes=VMEM_LIMIT),
        name="experts",
    )(first, nblk, nvalid, nused, xs, wg, wu, wd)


def _combine_kernel(x1_ref, rec_ref, yg_ref, p_ref, gple_ref, wpg_ref, wple_ref, gfin_ref, *rest):
    out_ref = rest[-1]
    ple = jnp.dot(p_ref[0].reshape(TM, D_PLE).astype(BF16), wple_ref[...], preferred_element_type=F32)
    rec = rec_ref[...]
    w0 = rec[:, REC_W0:REC_W0 + 1]
    w1 = rec[:, REC_W1:REC_W1 + 1]
    lo0, hi0 = _unpack_bf16_pair(yg_ref[0].reshape(TM, HALF))
    lo1, hi1 = _unpack_bf16_pair(yg_ref[1].reshape(TM, HALF))
    moe = jnp.concatenate([lo0 * w0 + lo1 * w1, hi0 * w0 + hi1 * w1], axis=1)
    x2 = x1_ref[...].reshape(TM, D_MODEL) + moe
    gate = jax.nn.sigmoid(jnp.dot(_rms(x2, gple_ref[...]).astype(BF16), wpg_ref[...],
                                  preferred_element_type=F32))
    x3 = x2 + gate * ple
    out_ref[...] = _rms(x3, gfin_ref[...]).reshape(BATCH, TT, D_MODEL)


def _combine(s0, n_steps, x1, rec, yg, p, gple, wpg, wple, gfin, out_prev=None):
    seq_spec = pl.BlockSpec((BATCH, TT, D_MODEL), lambda i: (0, s0 + i, 0))
    in_specs = [
        seq_spec,
        pl.BlockSpec((TM, LANES), lambda i: (s0 + i, 0)),
        pl.BlockSpec((TOPK, TM) + ROW_TILE, lambda i: (0, i, 0, 0)),
        pl.BlockSpec((1, BATCH, TT, D_PLE), lambda i: (0, 0, s0 + i, 0)),
        _const_spec((1, D_MODEL)),
        _const_spec((D_MODEL, D_MODEL)),
        _const_spec((D_PLE, D_MODEL)),
        _const_spec((1, D_MODEL)),
    ]
    args = [x1, rec, yg, p, gple, wpg, wple, gfin]
    aliases = {}
    if out_prev is not None:
        in_specs.append(pl.BlockSpec(memory_space=pl.ANY))
        args.append(out_prev)
        aliases = {len(args) - 1: 0}
    return pl.pallas_call(
        _combine_kernel,
        grid=(n_steps,),
        in_specs=in_specs,
        out_specs=seq_spec,
        out_shape=jax.ShapeDtypeStruct((BATCH, SEQ, D_MODEL), F32),
        input_output_aliases=aliases,
        compiler_params=pltpu.CompilerParams(
            dimension_semantics=("arbitrary",), vmem_limit_bytes=VMEM_LIMIT),
        name="combine",
    )(*args)


def kernel(x, p, g_mix, w_in, b_gate, ssm_a_re, ssm_a_im, ssm_log_dt, ssm_b_re, ssm_b_im, ssm_c_re,
           ssm_c_im, ssm_d, w_glu, conv_dw, conv_dw_b, conv_ln_g, conv_ln_b, w_conv_out, w_out, g_moe,
           w_router_group, b_router_group, w_router_expert, b_router_expert, w_exp_gate, w_exp_up,
           w_exp_down, g_ple, w_ple_gate, w_ple, g_final):
    assert x.shape == (BATCH, SEQ, D_MODEL) and p.shape == (1, BATCH, SEQ, D_PLE)
    row = lambda v: v.reshape(1, -1)

    um, up, ur, a_re, a_im = _ssm_matrices(ssm_a_re[0], ssm_a_im[0], ssm_log_dt[0], ssm_b_re[0],
                                           ssm_b_im[0], ssm_c_re[0], ssm_c_im[0])
    wr1, wr2, br = _router_weights(w_router_group[0], b_router_group[0], w_router_expert[0],
                                   b_router_expert[0])
    x1, h2p, rec, rect, cnt, wdown_bf = _mixer(
        x, row(g_mix[0]), w_in[0], row(b_gate[0]), um, up, ur, a_re, a_im,
        row(ssm_d[0]), w_glu[0], conv_dw[0], row(conv_dw_b[0]), row(conv_ln_g[0]),
        row(conv_ln_b[0]), w_conv_out[0], w_out[0], row(g_moe[0]), wr1, wr2, br, w_exp_down[0])

    counts = cnt[0, LANE_EXP0:LANE_EXP0 + N_EXPERTS].astype(I32)
    pcounts = (counts + BM - 1) // BM * BM
    pends = jnp.cumsum(pcounts)
    pstarts = pends - pcounts
    eid = rect[REC_EID0:REC_EID1 + 1].astype(I32)
    rank = rect[REC_RANK0:REC_RANK1 + 1].astype(I32)
    dest = (jnp.sum(jnp.where(eid[..., None] == jnp.arange(N_EXPERTS, dtype=I32), pstarts, 0), axis=-1)
            + rank).reshape(TOPK, N_TOK // SC_WINDOW, SC_WINDOW)
    nused = (pends[-1] // BM).astype(I32)
    blk = jnp.arange(N_BLK, dtype=I32)[:, None] * BM
    in_expert = (pstarts[None, :] <= blk) & (blk < pends[None, :])
    nvalid = jnp.clip(jnp.sum(jnp.where(in_expert, (pstarts + counts)[None, :] - blk, 0), axis=1), 0, BM)

    xs = _dispatch(h2p, dest)
    ys = _experts(pstarts // BM, pcounts // BM, nvalid.astype(I32), nused.reshape(1), xs,
                  w_exp_gate[0], w_exp_up[0], wdown_bf)
    dest_tok = dest.reshape(TOPK, N_TOK)
    wpg = w_ple_gate[0].astype(BF16)
    wple = w_ple[0].astype(BF16)
    out = None
    s0 = 0
    for n_steps in PART_STEPS:
        yg = _collect(ys, dest_tok[:, s0 * TM:(s0 + n_steps) * TM])
        out = _combine(s0, n_steps, x1, rec, yg, p, row(g_ple[0]), wpg, wple, row(g_final), out)
        s0 += n_steps
    return out
```

```python
import functools

import jax
import jax.numpy as jnp
from jax import lax
from jax.experimental import pallas as pl
from jax.experimental.pallas import tpu as pltpu
from jax.experimental.pallas import tpu_sc as plsc

F32 = jnp.float32
BF16 = jnp.bfloat16
U32 = jnp.uint32
I32 = jnp.int32

D_MODEL = 1024
BATCH = 8
SEQ = 2048
N_TOK = BATCH * SEQ
D_SSM = 512
SSM_GROUP_WIDTH = 16
SSM_GROUPS = 32
SSM_STATE = 64
D_CONV = 512
CONV_WIDTH = 31
D_IN = D_SSM + 2 * D_CONV + 2 * D_MODEL
N_GROUPS_MOE = 4
EXPERTS_PER_GROUP = 8
N_EXPERTS = 32
TOPK = 2
D_EXPERT = 512
D_PLE = 256
EPS = 1e-6

SUBLANES = 8
LANES = 128
assert BATCH == SUBLANES

TT = 64
TM = TT * BATCH
N_STEP = SEQ // TT
SB = 512
NSB = TM // SB
BPS = SB // TT
Q = 2
N_SLAB = D_SSM // LANES
GROUPS_PER_SLAB = SSM_GROUPS // N_SLAB
ROWS_Z = TM // Q
STATE_LANES = 2 * GROUPS_PER_SLAB * SSM_STATE
HALO = (CONV_WIDTH - 1) * BATCH
CHUNK_ROWS = SB // (Q * SUBLANES)
W_STAGE_ROWS = 64
W_STAGE_SLOTS = 4
CONV_ROWS = 64
N_LC = D_CONV // LANES

LANE_GRP0 = 0
LANE_EXP0 = 32
REC_EID0, REC_EID1, REC_W0, REC_W1, REC_RANK0, REC_RANK1 = 0, 1, 2, 3, 4, 5
REC_ROWS = 8

BM = 256
N_BLK = (TOPK * N_TOK + N_EXPERTS * (BM - 1) + BM - 1) // BM
N_ROWS = N_BLK * BM
HALF = D_MODEL // 2
ROW_TILE = (HALF // LANES, LANES)
SC_WINDOW = 128
IN_AHEAD = 3
IN_SLOTS = IN_AHEAD + 1
EXP_PER_STEP = 2
PART_STEPS = (12, 20)
assert sum(PART_STEPS) == N_STEP

VMEM_LIMIT = 56 * 1024 * 1024


def _const_spec(shape):
    n = len(shape)
    return pl.BlockSpec(shape, lambda *_: (0,) * n, pipeline_mode=pl.Buffered(1))


def _rms(x, g):
    ms = jnp.mean(x * x, axis=-1, keepdims=True)
    return x * lax.rsqrt(ms + EPS) * g


def _pack_bf16_pair(lo, hi):
    ulo = lax.bitcast_convert_type(lo.astype(BF16).astype(F32), U32)
    uhi = lax.bitcast_convert_type(hi.astype(BF16).astype(F32), U32)
    return (ulo >> 16) | (uhi & jnp.uint32(0xFFFF0000))


def _unpack_bf16_pair(w):
    lo = lax.bitcast_convert_type(w << 16, F32)
    hi = lax.bitcast_convert_type(w & jnp.uint32(0xFFFF0000), F32)
    return lo, hi


def _load_weights_bf16(pairs, stage, sem):
    chunks = [(src, dst, r0) for src, dst in pairs for r0 in range(0, src.shape[0], W_STAGE_ROWS)]

    def copy(c):
        src, _, r0 = chunks[c]
        slot = c % W_STAGE_SLOTS
        return pltpu.make_async_copy(src.at[pl.ds(r0, W_STAGE_ROWS)],
                                     stage.at[slot, :, 0:src.shape[1]], sem.at[slot])

    for c in range(W_STAGE_SLOTS):
        copy(c).start()
    for c, (src, dst, r0) in enumerate(chunks):
        copy(c).wait()
        dst[r0:r0 + W_STAGE_ROWS, :] = stage[c % W_STAGE_SLOTS, :, 0:src.shape[1]].astype(BF16)
        if c + W_STAGE_SLOTS < len(chunks):
            copy(c + W_STAGE_SLOTS).start()


def _expand_ssm(um_ref, up_ref, ur_ref, mp_ref, r_ref):
    gw, ns, half = SSM_GROUP_WIDTH, SSM_STATE, STATE_LANES // 2
    div = lambda a, n: lax.shift_right_logical(a, n.bit_length() - 1)
    mod = lambda a, n: a & (n - 1)
    iota2 = lambda shape: (lax.broadcasted_iota(I32, shape, 0), lax.broadcasted_iota(I32, shape, 1))
    one = lambda cond: jnp.where(cond, 1.0, 0.0).astype(BF16)

    r, q = iota2((Q * gw, Q * LANES))
    x_m = one((div(r, gw) == div(q, LANES)) & (mod(r, gw) == mod(q, gw)))
    r, q = iota2((2 * ns, STATE_LANES))
    x_p = one((div(r, ns) == div(q, half)) & (mod(r, ns) == mod(q, ns)))
    p, r = iota2((STATE_LANES, 2 * ns))
    x_r = one((div(p, half) == div(r, ns)) & (mod(p, ns) == mod(r, ns)))
    p, q = iota2((Q * LANES, Q * LANES))
    same_m = div(mod(p, LANES), gw) == div(mod(q, LANES), gw)
    p, q = iota2((Q * LANES, STATE_LANES))
    same_p = div(mod(p, LANES), gw) == div(mod(q, half), ns)
    p, q = iota2((STATE_LANES, Q * LANES))
    same_r = div(mod(p, half), ns) == div(mod(q, LANES), gw)
    for s in range(N_SLAB):
        m = jnp.dot(um_ref[s], x_m, preferred_element_type=F32)
        mp_ref[s, :, 0:Q * LANES] = jnp.where(same_m, m, 0.0).astype(BF16)
        pm = jnp.dot(up_ref[s], x_p, preferred_element_type=F32)
        mp_ref[s, :, Q * LANES:] = jnp.where(same_p, pm, 0.0).astype(BF16)
        rm = jnp.dot(x_r, ur_ref[s], preferred_element_type=F32)
        r_ref[s] = jnp.where(same_r, rm, 0.0).astype(BF16)


def _mixer_kernel(x_ref, gmix_ref, win_hbm, bgate_ref, um_ref, up_ref, ur_ref, are_ref,
                  aim_ref, d_ref, wglu_hbm, dw_ref, dwb_ref, lng_ref, lnb_ref, wco_hbm, wout_hbm,
                  gmoe_ref, wr1_ref, wr2_ref, br_ref,
                  x1_ref, h2p_ref, rec_ref, rect_ref, cnt_ref,
                  hb_scr, ht_scr, u_scr, y_scr, yi_scr, xs_scr, z_scr, conv_scr, act_scr, actb_scr,
                  logit_scr, s_scr, cnt_scr, win_ref, wglu_ref, wco_ref, wout_ref, wstage_scr, wstage_sem,
                  mp_ref, r_ref):
    step = pl.program_id(0)
    assert NSB == 1

    @pl.when(step == 0)
    def _init():
        logit_scr[...] = jnp.zeros(logit_scr.shape, F32)
        z_scr[:, 0:HALO, :] = jnp.zeros((N_LC, HALO, LANES), F32)
        s_scr[...] = jnp.zeros(s_scr.shape, F32)
        cnt_scr[...] = jnp.zeros(cnt_scr.shape, F32)
        _expand_ssm(um_ref, up_ref, ur_ref, mp_ref, r_ref)
        _load_weights_bf16([(win_hbm, win_ref), (wglu_hbm, wglu_ref), (wco_hbm, wco_ref),
                            (wout_hbm, wout_ref)], wstage_scr, wstage_sem)

    def sub_rows(r):
        return pl.ds(pl.multiple_of(r * SB, SB), SB)

    def phase_a(r, carry):
        xb = x_ref[pl.ds(r * BPS, BPS)].reshape(SB, D_MODEL)
        hb_scr[sub_rows(r), :] = _rms(xb, gmix_ref[...]).astype(BF16)
        return carry

    def phase_a3(r, carry):
        h = ht_scr[sub_rows(r), :]
        u = jnp.dot(h, win_ref[:, 0:D_SSM], preferred_element_type=F32)
        u_scr[pl.ds(r * CHUNK_ROWS, CHUNK_ROWS)] = u.reshape(CHUNK_ROWS, Q, SUBLANES, D_SSM)
        v = jnp.dot(h, win_ref[:, D_SSM:D_SSM + 2 * D_CONV], preferred_element_type=F32)
        zc = v[:, 0:D_CONV] * jax.nn.sigmoid(v[:, D_CONV:])
        for lc in range(N_LC):
            z_scr[lc, pl.ds(pl.multiple_of(HALO + r * SB, SUBLANES), SB), :] = zc[:, lc * LANES:(lc + 1) * LANES]
        return carry

    def phase_b():
        for s in range(N_SLAB):
            lanes = slice(s * LANES, (s + 1) * LANES)
            z = jnp.concatenate(
                [u_scr[:, i, :, lanes].reshape(ROWS_Z, LANES) for i in range(Q)], axis=1).astype(BF16)
            xp = jnp.dot(z, mp_ref[s], preferred_element_type=F32)
            yi_scr[s] = xp[:, 0:Q * LANES]
            xs_scr[s] = xp[:, Q * LANES:]

        half = STATE_LANES // 2
        for s in range(N_SLAB):
            a_re = jnp.broadcast_to(are_ref[s:s + 1, :], (SUBLANES, half))
            a_im = jnp.broadcast_to(aim_ref[s:s + 1, :], (SUBLANES, half))

            def scan_body(k, carry, s=s, a_re=a_re, a_im=a_im):
                s_re, s_im = carry
                rows = pl.ds(pl.multiple_of(k * SUBLANES, SUBLANES), SUBLANES)
                x_re = xs_scr[s, rows, 0:half]
                x_im = xs_scr[s, rows, half:]
                xs_scr[s, rows, 0:half] = s_re
                xs_scr[s, rows, half:] = s_im
                n_re = a_re * s_re - a_im * s_im + x_re
                n_im = a_re * s_im + a_im * s_re + x_im
                return n_re, n_im

            s_re, s_im = lax.fori_loop(0, ROWS_Z // SUBLANES, scan_body,
                                       (s_scr[s, :, 0:half], s_scr[s, :, half:]), unroll=True)
            s_scr[s, :, 0:half] = s_re
            s_scr[s, :, half:] = s_im

        for s in range(N_SLAB):
            lanes = slice(s * LANES, (s + 1) * LANES)
            y_tot = yi_scr[s] + jnp.dot(xs_scr[s].astype(BF16), r_ref[s], preferred_element_type=F32)
            for j in range(Q):
                y_scr[:, j, :, lanes] = y_tot[:, j * LANES:(j + 1) * LANES].reshape(
                    ROWS_Z // SUBLANES, SUBLANES, LANES)

    def phase_c1(r, carry):
        rows = sub_rows(r)
        crow = pl.ds(r * CHUNK_ROWS, CHUNK_ROWS)
        y = y_scr[crow].reshape(SB, D_SSM) + d_ref[...] * u_scr[crow].reshape(SB, D_SSM)
        act_scr[rows, 0:D_SSM] = jax.nn.gelu(y).astype(BF16)
        for lc in range(N_LC):
            lanes = slice(lc * LANES, (lc + 1) * LANES)

            def conv_piece(rc, c, lc=lc, lanes=lanes):
                r0 = r * SB + rc * CONV_ROWS
                piece = jnp.broadcast_to(dwb_ref[:, lanes], (CONV_ROWS, LANES))
                for j in range(CONV_WIDTH):
                    zrows = pl.ds(pl.multiple_of(r0 + j * BATCH, SUBLANES), CONV_ROWS)
                    piece = piece + dw_ref[j:j + 1, lanes] * z_scr[lc, zrows, :]
                conv_scr[pl.ds(pl.multiple_of(rc * CONV_ROWS, CONV_ROWS), CONV_ROWS), lanes] = piece
                return c

            lax.fori_loop(0, SB // CONV_ROWS, conv_piece, 0, unroll=4)
        acc = conv_scr[...]
        mu = jnp.mean(acc, axis=-1, keepdims=True)
        cen = acc - mu
        var = jnp.mean(cen * cen, axis=-1, keepdims=True)
        ln = cen * lax.rsqrt(var + EPS) * lng_ref[...] + lnb_ref[...]
        act_scr[rows, D_SSM:] = jax.nn.silu(ln).astype(BF16)
        return carry

    lane = lax.broadcasted_iota(I32, (1, LANES), 1).astype(F32)
    grp_mask = lane < float(N_GROUPS_MOE)
    exp_lane = (lane >= float(LANE_EXP0)) & (lane < float(LANE_EXP0 + N_EXPERTS))
    lane_grp = jnp.floor((lane - float(LANE_EXP0)) * (1.0 / EXPERTS_PER_GROUP))
    tri = (lax.broadcasted_iota(I32, (SB, SB), 0) > lax.broadcasted_iota(I32, (SB, SB), 1)).astype(BF16)
    neg_inf = float("-inf")
    big = float(4 * LANES)

    def phase_c3(r, carry):
        rows = sub_rows(r)
        h = hb_scr[rows, :]
        g0 = D_SSM + 2 * D_CONV
        gate_ssm = jnp.dot(h, win_ref[:, g0:g0 + D_MODEL], preferred_element_type=F32) \
            + bgate_ref[:, 0:D_MODEL]
        gate_conv = jnp.dot(h, win_ref[:, g0 + D_MODEL:], preferred_element_type=F32) \
            + bgate_ref[:, D_MODEL:]
        zz = jnp.dot(actb_scr[rows, 0:D_SSM], wglu_ref[...], preferred_element_type=F32)
        y_ssm = zz[:, 0:D_MODEL] * jax.nn.sigmoid(zz[:, D_MODEL:])
        y_conv = jnp.dot(actb_scr[rows, D_SSM:], wco_ref[...], preferred_element_type=F32)

        merged = jax.nn.sigmoid(gate_ssm) * y_ssm + jax.nn.sigmoid(gate_conv) * y_conv
        xb = x_ref[pl.ds(r * BPS, BPS)].reshape(SB, D_MODEL)
        x1 = xb + jnp.dot(merged.astype(BF16), wout_ref[...], preferred_element_type=F32)
        x1_ref[pl.ds(r * BPS, BPS)] = x1.reshape(BPS, TT, D_MODEL)

        h2 = _rms(x1, gmoe_ref[...])
        h2p_ref[rows] = _pack_bf16_pair(h2[:, 0:HALF], h2[:, HALF:]).reshape((SB,) + ROW_TILE)

        h2_hi = h2.astype(BF16)
        h2_lo = (h2 - h2_hi.astype(F32)).astype(BF16)
        l1 = jnp.dot(h2_hi, wr1_ref[...], preferred_element_type=F32)
        l2 = jnp.dot(h2_lo, wr2_ref[...], preferred_element_type=F32)
        logit_scr[rows, :] = l1[:, 0:LANES] + l1[:, LANES:] + l2 + br_ref[...]
        return carry

    def route_previous():
        rows = sub_rows(0)
        logits = logit_scr[...]
        counted = jnp.where(step > 0, 1.0, 0.0)

        lg = jnp.where(grp_mask, logits, neg_inf)
        g_max = jnp.max(lg, axis=-1, keepdims=True)
        g_sel = jnp.min(jnp.where(lg == g_max, lane, big), axis=-1, keepdims=True)
        p_g = 1.0 / jnp.sum(jnp.where(grp_mask, jnp.exp(logits - g_max), 0.0), axis=-1, keepdims=True)
        le = jnp.where(exp_lane & (lane_grp == g_sel), logits, neg_inf)
        m1 = jnp.max(le, axis=-1, keepdims=True)
        i1 = jnp.min(jnp.where(le == m1, lane, big), axis=-1, keepdims=True)
        le2 = jnp.where(lane == i1, neg_inf, le)
        m2 = jnp.max(le2, axis=-1, keepdims=True)
        i2 = jnp.min(jnp.where(le2 == m2, lane, big), axis=-1, keepdims=True)
        e2 = jnp.exp(m2 - m1)
        den = 1.0 + e2
        w_a = (1.0 / den) * p_g
        w_b = (e2 / den) * p_g

        sel1 = lane == i1
        sel2 = lane == i2
        onehot = jnp.where(sel1 | sel2, counted, 0.0)
        prefix = jnp.dot(tri, onehot.astype(BF16), preferred_element_type=F32) + cnt_scr[...]
        rank_a = jnp.sum(jnp.where(sel1, prefix, 0.0), axis=-1, keepdims=True)
        rank_b = jnp.sum(jnp.where(sel2, prefix, 0.0), axis=-1, keepdims=True)
        cnt_scr[...] = cnt_scr[...] + jnp.sum(onehot, axis=0, keepdims=True)

        rec = jnp.where(lane == float(REC_EID0), i1 - float(LANE_EXP0), 0.0)
        rec = jnp.where(lane == float(REC_EID1), i2 - float(LANE_EXP0), rec)
        rec = jnp.where(lane == float(REC_W0), w_a, rec)
        rec = jnp.where(lane == float(REC_W1), w_b, rec)
        rec = jnp.where(lane == float(REC_RANK0), rank_a, rec)
        rec = jnp.where(lane == float(REC_RANK1), rank_b, rec)
        rec_ref[rows, :] = rec
        rect_ref[...] = jnp.transpose(rec)[0:REC_ROWS, :]
        cnt_ref[...] = cnt_scr[...]

    @pl.when(step < N_STEP)
    def _tile():
        route_previous()
        phase_a(0, 0)
        ht_scr[...] = jnp.swapaxes(hb_scr[...].reshape(BATCH, TT, D_MODEL), 0, 1).reshape(TM, D_MODEL)
        phase_a3(0, 0)
        phase_b()
        phase_c1(0, 0)
        z_scr[:, 0:HALO, :] = z_scr[:, TM:TM + HALO, :]
        actb_scr[...] = jnp.swapaxes(act_scr[...].reshape(TT, BATCH, D_SSM + D_CONV), 0, 1).reshape(
            TM, D_SSM + D_CONV)
        phase_c3(0, 0)

    @pl.when(step == N_STEP)
    def _last():
        route_previous()


def _mixer(x, gmix, win, bgate, um, up, ur, a_re, a_im, dvec, wglu, dw, dwb, lng, lnb, wco,
           wout, gmoe, wr1, wr2, br):
    tile = lambda i: jnp.minimum(i, N_STEP - 1)
    routed = lambda i: jnp.maximum(i - 1, 0)
    seq_spec = pl.BlockSpec((BATCH, TT, D_MODEL), lambda i: (0, tile(i), 0))
    in_hbm = pl.BlockSpec(memory_space=pl.ANY)
    in_specs = [
        seq_spec,
        _const_spec((1, D_MODEL)),
        in_hbm,
        _const_spec((1, 2 * D_MODEL)),
        _const_spec(um.shape),
        _const_spec(up.shape),
        _const_spec(ur.shape),
        _const_spec(a_re.shape),
        _const_spec(a_im.shape),
        _const_spec((1, D_SSM)),
        in_hbm,
        _const_spec((CONV_WIDTH, D_CONV)),
        _const_spec((1, D_CONV)),
        _const_spec((1, D_CONV)),
        _const_spec((1, D_CONV)),
        in_hbm,
        in_hbm,
        _const_spec((1, D_MODEL)),
        _const_spec((D_MODEL, 2 * LANES)),
        _const_spec((D_MODEL, LANES)),
        _const_spec((1, LANES)),
    ]
    out_specs = [
        seq_spec,
        pl.BlockSpec((TM,) + ROW_TILE, lambda i: (tile(i), 0, 0)),
        pl.BlockSpec((TM, LANES), lambda i: (routed(i), 0)),
        pl.BlockSpec((REC_ROWS, TM), lambda i: (0, routed(i))),
        pl.BlockSpec((1, LANES), lambda i: (0, 0)),
    ]
    out_shape = [
        jax.ShapeDtypeStruct((BATCH, SEQ, D_MODEL), F32),
        jax.ShapeDtypeStruct((N_TOK,) + ROW_TILE, U32),
        jax.ShapeDtypeStruct((N_TOK, LANES), F32),
        jax.ShapeDtypeStruct((REC_ROWS, N_TOK), F32),
        jax.ShapeDtypeStruct((1, LANES), F32),
    ]
    chunk_shape = (ROWS_Z // SUBLANES, Q, SUBLANES, D_SSM)
    scratch = [
        pltpu.VMEM((TM, D_MODEL), BF16),
        pltpu.VMEM((TM, D_MODEL), BF16),
        pltpu.VMEM(chunk_shape, F32),
        pltpu.VMEM(chunk_shape, F32),
        pltpu.VMEM((N_SLAB, ROWS_Z, Q * LANES), F32),
        pltpu.VMEM((N_SLAB, ROWS_Z, STATE_LANES), F32),
        pltpu.VMEM((N_LC, HALO + TM, LANES), F32),
        pltpu.VMEM((SB, D_CONV), F32),
        pltpu.VMEM((TM, D_SSM + D_CONV), BF16),
        pltpu.VMEM((TM, D_SSM + D_CONV), BF16),
        pltpu.VMEM((TM, LANES), F32),
        pltpu.VMEM((N_SLAB, SUBLANES, STATE_LANES), F32),
        pltpu.VMEM((1, LANES), F32),
        pltpu.VMEM(win.shape, BF16),
        pltpu.VMEM(wglu.shape, BF16),
        pltpu.VMEM(wco.shape, BF16),
        pltpu.VMEM(wout.shape, BF16),
        pltpu.VMEM((W_STAGE_SLOTS, W_STAGE_ROWS, D_IN), F32),
        pltpu.SemaphoreType.DMA((W_STAGE_SLOTS,)),
        pltpu.VMEM((N_SLAB, Q * LANES, Q * LANES + STATE_LANES), BF16),
        pltpu.VMEM((N_SLAB, STATE_LANES, Q * LANES), BF16),
    ]
    assert win.shape == (D_MODEL, D_IN) and wglu.shape == (D_SSM, 2 * D_MODEL)
    assert wco.shape == (D_CONV, D_MODEL) and wout.shape == (D_MODEL, D_MODEL)
    return pl.pallas_call(
        _mixer_kernel,
        grid=(N_STEP + 1,),
        in_specs=in_specs,
        out_specs=out_specs,
        out_shape=out_shape,
        scratch_shapes=scratch,
        compiler_params=pltpu.CompilerParams(
            dimension_semantics=("arbitrary",), vmem_limit_bytes=VMEM_LIMIT),
        name="mixer",
    )(x, gmix, win, bgate, um, up, ur, a_re, a_im, dvec, wglu, dw, dwb, lng, lnb, wco, wout,
      gmoe, wr1, wr2, br)


def _cmul(a, b):
    return a[0] * b[0] - a[1] * b[1], a[0] * b[1] + a[1] * b[0]


def _ssm_matrices(a_re, a_im, log_dt, b_re, b_im, c_re, c_im):
    dt = jnp.exp(log_dt)[:, None]
    mag = jnp.exp(a_re * dt)
    lam = (mag * jnp.cos(a_im * dt), mag * jnp.sin(a_im * dt))
    den = a_re * a_re + a_im * a_im
    nr = lam[0] - 1.0
    ni = lam[1]
    z_re = (nr * a_re + ni * a_im) / den
    z_im = (ni * a_re - nr * a_im) / den
    bbar = (z_re[..., None] * b_re - z_im[..., None] * b_im,
            z_re[..., None] * b_im + z_im[..., None] * b_re)
    pw = [(jnp.ones_like(lam[0]), jnp.zeros_like(lam[0])), lam]
    for _ in range(2, Q + 1):
        pw.append(_cmul(pw[-1], lam))
    e = [(c_re * p[0][:, None, :] - c_im * p[1][:, None, :],
          c_re * p[1][:, None, :] + c_im * p[0][:, None, :]) for p in pw]
    e_cat = jnp.concatenate([jnp.concatenate([e[m][0], -e[m][1]], axis=-1) for m in range(Q)], axis=1)
    k_cat = jnp.einsum('gcn,gnd->gcd', e_cat, jnp.concatenate(bbar, axis=1), precision=lax.Precision.HIGHEST)
    k = [k_cat[:, m * SSM_GROUP_WIDTH:(m + 1) * SSM_GROUP_WIDTH, :] for m in range(Q)]
    split = lambda t: t.reshape((N_SLAB, GROUPS_PER_SLAB) + t.shape[1:])
    zero_k = jnp.zeros_like(k[0])
    kb = jnp.stack([jnp.stack([split(jnp.swapaxes(k[j - i] if j >= i else zero_k, 1, 2))
                               for j in range(Q)]) for i in range(Q)])
    um = jnp.transpose(kb, (2, 0, 3, 4, 1, 5)).reshape(N_SLAB, Q * LANES, Q * SSM_GROUP_WIDTH)
    f = [_cmul((pw[Q - 1 - i][0][..., None], pw[Q - 1 - i][1][..., None]), bbar) for i in range(Q)]
    fs = jnp.stack([jnp.stack([split(f[i][part]) for i in range(Q)]) for part in range(2)])
    up = jnp.transpose(fs, (2, 1, 3, 5, 0, 4)).reshape(N_SLAB, Q * LANES, 2 * SSM_STATE)
    es = jnp.stack([sign * jnp.stack([split(e[j + 1][part]) for j in range(Q)])
                    for part, sign in ((0, 1.0), (1, -1.0))])
    ur = jnp.transpose(es, (2, 0, 5, 1, 3, 4)).reshape(N_SLAB, 2 * SSM_STATE, Q * LANES)
    a_q = pw[Q]
    return (um.astype(BF16), up.astype(BF16), ur.astype(BF16),
            a_q[0].reshape(N_SLAB, STATE_LANES // 2), a_q[1].reshape(N_SLAB, STATE_LANES // 2))


def _router_weights(w_rg, b_rg, w_re, b_re):
    pad_g = LANE_EXP0 - LANE_GRP0 - N_GROUPS_MOE
    pad_e = LANES - LANE_EXP0 - N_EXPERTS
    w = jnp.concatenate([w_rg, jnp.zeros((D_MODEL, pad_g), F32), w_re, jnp.zeros((D_MODEL, pad_e), F32)], axis=1)
    b = jnp.concatenate([b_rg, jnp.zeros((pad_g,), F32), b_re, jnp.zeros((pad_e,), F32)]).reshape(1, LANES)
    w_hi = w.astype(BF16)
    w_lo = (w - w_hi.astype(F32)).astype(BF16)
    return jnp.concatenate([w_hi, w_lo], axis=1), w_hi, b


def _sc_mesh():
    return plsc.VectorSubcoreMesh(core_axis_name="core", subcore_axis_name="subcore")


def _sc_worker(mesh):
    return lax.axis_index("core") * mesh.num_subcores + lax.axis_index("subcore")


def _dispatch(h2p, dest):
    mesh = _sc_mesh()
    n_win = N_TOK // SC_WINDOW
    per_worker = n_win // (mesh.num_cores * mesh.num_subcores)
    assert per_worker * mesh.num_cores * mesh.num_subcores == n_win

    @pl.kernel(out_type=jax.ShapeDtypeStruct((N_ROWS,) + ROW_TILE, U32), mesh=mesh,
               scratch_types=[pltpu.VMEM((SC_WINDOW,), I32), pltpu.VMEM((SC_WINDOW,) + ROW_TILE, U32)])
    def scatter_rows(h_hbm, dest_hbm, xs_hbm, idx_v, rows_v):
        first = _sc_worker(mesh) * per_worker

        @pl.loop(0, per_worker)
        def _(w):
            win = first + w
            pltpu.sync_copy(h_hbm.at[pl.ds(win * SC_WINDOW, SC_WINDOW)], rows_v)
            for j in range(TOPK):
                pltpu.sync_copy(dest_hbm.at[j, win], idx_v)
                pltpu.sync_copy(rows_v, xs_hbm.at[idx_v])

    return scatter_rows(h2p, dest)


def _collect(ys, dest):
    mesh = _sc_mesh()
    n_tok = dest.shape[1]
    n_win = TOPK * n_tok // SC_WINDOW
    per_worker = n_win // (mesh.num_cores * mesh.num_subcores)
    assert per_worker * mesh.num_cores * mesh.num_subcores == n_win

    @pl.kernel(out_type=jax.ShapeDtypeStruct((TOPK * n_tok,) + ROW_TILE, U32), mesh=mesh,
               scratch_types=[pltpu.VMEM((SC_WINDOW,), I32), pltpu.VMEM((SC_WINDOW,) + ROW_TILE, U32)])
    def gather_rows(ys_hbm, dest_hbm, yg_hbm, idx_v, rows_v):
        first = _sc_worker(mesh) * per_worker

        @pl.loop(0, per_worker)
        def _(w):
            win = first + w
            pltpu.sync_copy(dest_hbm.at[win], idx_v)
            pltpu.sync_copy(ys_hbm.at[idx_v], rows_v)
            pltpu.sync_copy(rows_v, yg_hbm.at[pl.ds(win * SC_WINDOW, SC_WINDOW)])

    return gather_rows(ys, dest.reshape(n_win, SC_WINDOW)).reshape((TOPK, n_tok) + ROW_TILE)


def _expert_kernel(first_ref, nblk_ref, nvalid_ref, nused_ref, xs_hbm, wg_ref, wu_ref, wd_ref, ys_hbm,
                   wg_scr, wu_scr, wd_scr, x_buf, y_buf, in_sem, out_sem):
    step = pl.program_id(0)
    nused = nused_ref[0]

    def in_copy(g):
        slot = lax.rem(g, IN_SLOTS)
        return pltpu.make_async_copy(xs_hbm.at[pl.ds(g * BM, BM)], x_buf.at[slot], in_sem.at[slot])

    def out_copy(g, slot):
        return pltpu.make_async_copy(y_buf.at[slot], ys_hbm.at[pl.ds(g * BM, BM)], out_sem.at[slot])

    @pl.when(step == 0)
    def _first():
        for g in range(IN_AHEAD):
            in_copy(g).start()

    def block(b, carry, e):
        g = first_ref[e] + b
        slot = lax.rem(g, 2)
        in_copy(g).wait()

        @pl.when(g + IN_AHEAD < nused)
        def _prefetch():
            in_copy(g + IN_AHEAD).start()

        @pl.when(g >= 2)
        def _slot_free():
            out_copy(g - 2, slot).wait()

        valid = lax.broadcasted_iota(I32, (BM, 1), 0) < nvalid_ref[g]
        x_blk = x_buf[lax.rem(g, IN_SLOTS)].reshape(BM, HALF)
        lo, hi = _unpack_bf16_pair(jnp.where(valid, x_blk, jnp.uint32(0)))
        lo = lo.astype(BF16)
        hi = hi.astype(BF16)
        gate = jnp.dot(lo, wg_scr[0:HALF, :], preferred_element_type=F32) \
            + jnp.dot(hi, wg_scr[HALF:, :], preferred_element_type=F32)
        up = jnp.dot(lo, wu_scr[0:HALF, :], preferred_element_type=F32) \
            + jnp.dot(hi, wu_scr[HALF:, :], preferred_element_type=F32)
        act = (jax.nn.silu(gate) * up).astype(BF16)
        o = jnp.dot(act, wd_scr[...], preferred_element_type=F32)
        y_buf[slot] = _pack_bf16_pair(o[:, 0:HALF], o[:, HALF:]).reshape((BM,) + ROW_TILE)
        out_copy(g, slot).start()
        return carry

    for k in range(EXP_PER_STEP):
        e = step * EXP_PER_STEP + k
        wg_scr[...] = wg_ref[k].astype(BF16)
        wu_scr[...] = wu_ref[k].astype(BF16)
        wd_scr[...] = wd_ref[k].astype(BF16)
        lax.fori_loop(0, nblk_ref[e], functools.partial(block, e=e), 0)

    @pl.when(step == N_EXPERTS // EXP_PER_STEP - 1)
    def _drain():
        out_copy(nused - 2, lax.rem(nused, 2)).wait()
        out_copy(nused - 1, 1 - lax.rem(nused, 2)).wait()


def _experts(first, nblk, nvalid, nused, xs, wg, wu, wd):
    grid_spec = pltpu.PrefetchScalarGridSpec(
        num_scalar_prefetch=4,
        grid=(N_EXPERTS // EXP_PER_STEP,),
        in_specs=[
            pl.BlockSpec(memory_space=pl.ANY),
            pl.BlockSpec((EXP_PER_STEP, D_MODEL, D_EXPERT), lambda s, *_: (s, 0, 0)),
            pl.BlockSpec((EXP_PER_STEP, D_MODEL, D_EXPERT), lambda s, *_: (s, 0, 0)),
            pl.BlockSpec((EXP_PER_STEP, D_EXPERT, D_MODEL), lambda s, *_: (s, 0, 0)),
        ],
        out_specs=pl.BlockSpec(memory_space=pl.ANY),
        scratch_shapes=[
            pltpu.VMEM((D_MODEL, D_EXPERT), BF16),
            pltpu.VMEM((D_MODEL, D_EXPERT), BF16),
            pltpu.VMEM((D_EXPERT, D_MODEL), BF16),
            pltpu.VMEM((IN_SLOTS, BM) + ROW_TILE, U32),
            pltpu.VMEM((2, BM) + ROW_TILE, U32),
            pltpu.SemaphoreType.DMA((IN_SLOTS,)),
            pltpu.SemaphoreType.DMA((2,)),
        ],
    )
    return pl.pallas_call(
        _expert_kernel,
        grid_spec=grid_spec,
        out_shape=jax.ShapeDtypeStruct((N_ROWS,) + ROW_TILE, U32),
        compiler_params=pltpu.CompilerParams(
            dimension_semantics=("arbitrary",), vmem_limit_bytes=VMEM_LIMIT),
        name="experts",
    )(first, nblk, nvalid, nused, xs, wg, wu, wd)


def _combine_kernel(x1_ref, rec_ref, yg_ref, p_ref, gple_ref, wpg_ref, wple_ref, gfin_ref, *rest):
    out_ref = rest[-1]
    ple = jnp.dot(p_ref[0].reshape(TM, D_PLE).astype(BF16), wple_ref[...], preferred_element_type=F32)
    rec = rec_ref[...]
    w0 = rec[:, REC_W0:REC_W0 + 1]
    w1 = rec[:, REC_W1:REC_W1 + 1]
    lo0, hi0 = _unpack_bf16_pair(yg_ref[0].reshape(TM, HALF))
    lo1, hi1 = _unpack_bf16_pair(yg_ref[1].reshape(TM, HALF))
    moe = jnp.concatenate([lo0 * w0 + lo1 * w1, hi0 * w0 + hi1 * w1], axis=1)
    x2 = x1_ref[...].reshape(TM, D_MODEL) + moe
    gate = jax.nn.sigmoid(jnp.dot(_rms(x2, gple_ref[...]).astype(BF16), wpg_ref[...],
                                  preferred_element_type=F32))
    x3 = x2 + gate * ple
    out_ref[...] = _rms(x3, gfin_ref[...]).reshape(BATCH, TT, D_MODEL)


def _combine(s0, n_steps, x1, rec, yg, p, gple, wpg, wple, gfin, out_prev=None):
    seq_spec = pl.BlockSpec((BATCH, TT, D_MODEL), lambda i: (0, s0 + i, 0))
    in_specs = [
        seq_spec,
        pl.BlockSpec((TM, LANES), lambda i: (s0 + i, 0)),
        pl.BlockSpec((TOPK, TM) + ROW_TILE, lambda i: (0, i, 0, 0)),
        pl.BlockSpec((1, BATCH, TT, D_PLE), lambda i: (0, 0, s0 + i, 0)),
        _const_spec((1, D_MODEL)),
        _const_spec((D_MODEL, D_MODEL)),
        _const_spec((D_PLE, D_MODEL)),
        _const_spec((1, D_MODEL)),
    ]
    args = [x1, rec, yg, p, gple, wpg, wple, gfin]
    aliases = {}
    if out_prev is not None:
        in_specs.append(pl.BlockSpec(memory_space=pl.ANY))
        args.append(out_prev)
        aliases = {len(args) - 1: 0}
    return pl.pallas_call(
        _combine_kernel,
        grid=(n_steps,),
        in_specs=in_specs,
        out_specs=seq_spec,
        out_shape=jax.ShapeDtypeStruct((BATCH, SEQ, D_MODEL), F32),
        input_output_aliases=aliases,
        compiler_params=pltpu.CompilerParams(
            dimension_semantics=("arbitrary",), vmem_limit_bytes=VMEM_LIMIT),
        name="combine",
    )(*args)


def kernel(x, p, g_mix, w_in, b_gate, ssm_a_re, ssm_a_im, ssm_log_dt, ssm_b_re, ssm_b_im, ssm_c_re,
           ssm_c_im, ssm_d, w_glu, conv_dw, conv_dw_b, conv_ln_g, conv_ln_b, w_conv_out, w_out, g_moe,
           w_router_group, b_router_group, w_router_expert, b_router_expert, w_exp_gate, w_exp_up,
           w_exp_down, g_ple, w_ple_gate, w_ple, g_final):
    assert x.shape == (BATCH, SEQ, D_MODEL) and p.shape == (1, BATCH, SEQ, D_PLE)
    row = lambda v: v.reshape(1, -1)

    um, up, ur, a_re, a_im = _ssm_matrices(ssm_a_re[0], ssm_a_im[0], ssm_log_dt[0], ssm_b_re[0],
                                           ssm_b_im[0], ssm_c_re[0], ssm_c_im[0])
    wr1, wr2, br = _router_weights(w_router_group[0], b_router_group[0], w_router_expert[0],
                                   b_router_expert[0])
    x1, h2p, rec, rect, cnt = _mixer(
        x, row(g_mix[0]), w_in[0], row(b_gate[0]), um, up, ur, a_re, a_im,
        row(ssm_d[0]), w_glu[0], conv_dw[0], row(conv_dw_b[0]), row(conv_ln_g[0]),
        row(conv_ln_b[0]), w_conv_out[0], w_out[0], row(g_moe[0]), wr1, wr2, br)

    counts = cnt[0, LANE_EXP0:LANE_EXP0 + N_EXPERTS].astype(I32)
    pcounts = (counts + BM - 1) // BM * BM
    pends = jnp.cumsum(pcounts)
    pstarts = pends - pcounts
    eid = rect[REC_EID0:REC_EID1 + 1].astype(I32)
    rank = rect[REC_RANK0:REC_RANK1 + 1].astype(I32)
    dest = (jnp.sum(jnp.where(eid[..., None] == jnp.arange(N_EXPERTS, dtype=I32), pstarts, 0), axis=-1)
            + rank).reshape(TOPK, N_TOK // SC_WINDOW, SC_WINDOW)
    nused = (pends[-1] // BM).astype(I32)
    blk = jnp.arange(N_BLK, dtype=I32)[:, None] * BM
    in_expert = (pstarts[None, :] <= blk) & (blk < pends[None, :])
    nvalid = jnp.clip(jnp.sum(jnp.where(in_expert, (pstarts + counts)[None, :] - blk, 0), axis=1), 0, BM)

    xs = _dispatch(h2p, dest)
    ys = _experts(pstarts // BM, pcounts // BM, nvalid.astype(I32), nused.reshape(1), xs,
                  w_exp_gate[0], w_exp_up[0], w_exp_down[0])
    dest_tok = dest.reshape(TOPK, N_TOK)
    wpg = w_ple_gate[0].astype(BF16)
    wple = w_ple[0].astype(BF16)
    out = None
    s0 = 0
    for n_steps in PART_STEPS:
        yg = _collect(ys, dest_tok[:, s0 * TM:(s0 + n_steps) * TM])
        out = _combine(s0, n_steps, x1, rec, yg, p, row(g_ple[0]), wpg, wple, row(g_final), out)
        s0 += n_steps
    return out
```

```python
import jax
import jax.numpy as jnp
from jax import lax
from jax.experimental import pallas as pl
from jax.experimental.pallas import tpu as pltpu
from jax.experimental.pallas import tpu_sc as plsc

F32 = jnp.float32
BF16 = jnp.bfloat16
U32 = jnp.uint32
I32 = jnp.int32

D_MODEL = 1024
BATCH = 8
SEQ = 2048
N_TOK = BATCH * SEQ
D_SSM = 512
SSM_GROUP_WIDTH = 16
SSM_GROUPS = 32
SSM_STATE = 64
D_CONV = 512
CONV_WIDTH = 31
D_IN = D_SSM + 2 * D_CONV + 2 * D_MODEL
N_GROUPS_MOE = 4
EXPERTS_PER_GROUP = 8
N_EXPERTS = 32
TOPK = 2
D_EXPERT = 512
D_PLE = 256
EPS = 1e-6

SUBLANES = 8
LANES = 128
assert BATCH == SUBLANES

TT = 64
TM = TT * BATCH
N_STEP = SEQ // TT
SB = 512
NSB = TM // SB
BPS = SB // TT
Q = 2
N_SLAB = D_SSM // LANES
GROUPS_PER_SLAB = SSM_GROUPS // N_SLAB
ROWS_Z = TM // Q
STATE_LANES = 2 * GROUPS_PER_SLAB * SSM_STATE
HALO = (CONV_WIDTH - 1) * BATCH
CHUNK_ROWS = SB // (Q * SUBLANES)
W_STAGE_ROWS = 64
W_STAGE_SLOTS = 4
CONV_ROWS = 64
N_LC = D_CONV // LANES

LANE_GRP0 = 0
LANE_EXP0 = 32
REC_EID0, REC_EID1, REC_W0, REC_W1, REC_RANK0, REC_RANK1 = 0, 1, 2, 3, 4, 5
REC_ROWS = 8

BM = 256
N_BLK = (TOPK * N_TOK + N_EXPERTS * (BM - 1) + BM - 1) // BM
N_ROWS = N_BLK * BM
HALF = D_MODEL // 2
ROW_TILE = (HALF // LANES, LANES)
SC_WINDOW = 128
IN_AHEAD = 3
IN_SLOTS = IN_AHEAD + 1
PART_STEPS = (8, 12, 12)
assert sum(PART_STEPS) == N_STEP

VMEM_LIMIT = 56 * 1024 * 1024


def _const_spec(shape):
    n = len(shape)
    return pl.BlockSpec(shape, lambda *_: (0,) * n, pipeline_mode=pl.Buffered(1))


def _rms(x, g):
    ms = jnp.mean(x * x, axis=-1, keepdims=True)
    return x * lax.rsqrt(ms + EPS) * g


def _pack_bf16_pair(lo, hi):
    ulo = lax.bitcast_convert_type(lo.astype(BF16).astype(F32), U32)
    uhi = lax.bitcast_convert_type(hi.astype(BF16).astype(F32), U32)
    return (ulo >> 16) | (uhi & jnp.uint32(0xFFFF0000))


def _unpack_bf16_pair(w):
    lo = lax.bitcast_convert_type(w << 16, F32)
    hi = lax.bitcast_convert_type(w & jnp.uint32(0xFFFF0000), F32)
    return lo, hi


def _load_weights_bf16(pairs, stage, sem):
    chunks = [(src, dst, r0) for src, dst in pairs for r0 in range(0, src.shape[0], W_STAGE_ROWS)]

    def copy(c):
        src, _, r0 = chunks[c]
        slot = c % W_STAGE_SLOTS
        return pltpu.make_async_copy(src.at[pl.ds(r0, W_STAGE_ROWS)],
                                     stage.at[slot, :, 0:src.shape[1]], sem.at[slot])

    for c in range(W_STAGE_SLOTS):
        copy(c).start()
    for c, (src, dst, r0) in enumerate(chunks):
        copy(c).wait()
        dst[r0:r0 + W_STAGE_ROWS, :] = stage[c % W_STAGE_SLOTS, :, 0:src.shape[1]].astype(BF16)
        if c + W_STAGE_SLOTS < len(chunks):
            copy(c + W_STAGE_SLOTS).start()


def _expand_ssm(um_ref, up_ref, ur_ref, mp_ref, r_ref):
    gw, ns, half = SSM_GROUP_WIDTH, SSM_STATE, STATE_LANES // 2
    div = lambda a, n: lax.shift_right_logical(a, n.bit_length() - 1)
    mod = lambda a, n: a & (n - 1)
    iota2 = lambda shape: (lax.broadcasted_iota(I32, shape, 0), lax.broadcasted_iota(I32, shape, 1))
    one = lambda cond: jnp.where(cond, 1.0, 0.0).astype(BF16)

    r, q = iota2((Q * gw, Q * LANES))
    x_m = one((div(r, gw) == div(q, LANES)) & (mod(r, gw) == mod(q, gw)))
    r, q = iota2((2 * ns, STATE_LANES))
    x_p = one((div(r, ns) == div(q, half)) & (mod(r, ns) == mod(q, ns)))
    p, r = iota2((STATE_LANES, 2 * ns))
    x_r = one((div(p, half) == div(r, ns)) & (mod(p, ns) == mod(r, ns)))
    p, q = iota2((Q * LANES, Q * LANES))
    same_m = div(mod(p, LANES), gw) == div(mod(q, LANES), gw)
    p, q = iota2((Q * LANES, STATE_LANES))
    same_p = div(mod(p, LANES), gw) == div(mod(q, half), ns)
    p, q = iota2((STATE_LANES, Q * LANES))
    same_r = div(mod(p, half), ns) == div(mod(q, LANES), gw)
    for s in range(N_SLAB):
        m = jnp.dot(um_ref[s], x_m, preferred_element_type=F32)
        mp_ref[s, :, 0:Q * LANES] = jnp.where(same_m, m, 0.0).astype(BF16)
        pm = jnp.dot(up_ref[s], x_p, preferred_element_type=F32)
        mp_ref[s, :, Q * LANES:] = jnp.where(same_p, pm, 0.0).astype(BF16)
        rm = jnp.dot(x_r, ur_ref[s], preferred_element_type=F32)
        r_ref[s] = jnp.where(same_r, rm, 0.0).astype(BF16)


def _mixer_kernel(x_ref, gmix_ref, win_hbm, bgate_ref, um_ref, up_ref, ur_ref, are_ref,
                  aim_ref, d_ref, wglu_hbm, dw_ref, dwb_ref, lng_ref, lnb_ref, wco_hbm, wout_hbm,
                  gmoe_ref, wr1_ref, wr2_ref, br_ref,
                  x1_ref, h2p_ref, rec_ref, rect_ref, cnt_ref,
                  hb_scr, ht_scr, u_scr, y_scr, yi_scr, xs_scr, z_scr, conv_scr, act_scr, actb_scr,
                  logit_scr, s_scr, cnt_scr, win_ref, wglu_ref, wco_ref, wout_ref, wstage_scr, wstage_sem,
                  mp_ref, r_ref):
    step = pl.program_id(0)
    assert NSB == 1

    @pl.when(step == 0)
    def _init():
        logit_scr[...] = jnp.zeros(logit_scr.shape, F32)
        z_scr[:, 0:HALO, :] = jnp.zeros((N_LC, HALO, LANES), F32)
        s_scr[...] = jnp.zeros(s_scr.shape, F32)
        cnt_scr[...] = jnp.zeros(cnt_scr.shape, F32)
        _expand_ssm(um_ref, up_ref, ur_ref, mp_ref, r_ref)
        _load_weights_bf16([(win_hbm, win_ref), (wglu_hbm, wglu_ref), (wco_hbm, wco_ref),
                            (wout_hbm, wout_ref)], wstage_scr, wstage_sem)

    def sub_rows(r):
        return pl.ds(pl.multiple_of(r * SB, SB), SB)

    def phase_a(r, carry):
        xb = x_ref[pl.ds(r * BPS, BPS)].reshape(SB, D_MODEL)
        hb_scr[sub_rows(r), :] = _rms(xb, gmix_ref[...]).astype(BF16)
        return carry

    def phase_a3(r, carry):
        h = ht_scr[sub_rows(r), :]
        u = jnp.dot(h, win_ref[:, 0:D_SSM], preferred_element_type=F32)
        u_scr[pl.ds(r * CHUNK_ROWS, CHUNK_ROWS)] = u.reshape(CHUNK_ROWS, Q, SUBLANES, D_SSM)
        v = jnp.dot(h, win_ref[:, D_SSM:D_SSM + 2 * D_CONV], preferred_element_type=F32)
        zc = v[:, 0:D_CONV] * jax.nn.sigmoid(v[:, D_CONV:])
        for lc in range(N_LC):
            z_scr[lc, pl.ds(pl.multiple_of(HALO + r * SB, SUBLANES), SB), :] = zc[:, lc * LANES:(lc + 1) * LANES]
        return carry

    def phase_b():
        for s in range(N_SLAB):
            lanes = slice(s * LANES, (s + 1) * LANES)
            z = jnp.concatenate(
                [u_scr[:, i, :, lanes].reshape(ROWS_Z, LANES) for i in range(Q)], axis=1).astype(BF16)
            xp = jnp.dot(z, mp_ref[s], preferred_element_type=F32)
            yi_scr[s] = xp[:, 0:Q * LANES]
            xs_scr[s] = xp[:, Q * LANES:]

        half = STATE_LANES // 2
        for s in range(N_SLAB):
            a_re = jnp.broadcast_to(are_ref[s:s + 1, :], (SUBLANES, half))
            a_im = jnp.broadcast_to(aim_ref[s:s + 1, :], (SUBLANES, half))

            def scan_body(k, carry, s=s, a_re=a_re, a_im=a_im):
                s_re, s_im = carry
                rows = pl.ds(pl.multiple_of(k * SUBLANES, SUBLANES), SUBLANES)
                x_re = xs_scr[s, rows, 0:half]
                x_im = xs_scr[s, rows, half:]
                xs_scr[s, rows, 0:half] = s_re
                xs_scr[s, rows, half:] = s_im
                n_re = a_re * s_re - a_im * s_im + x_re
                n_im = a_re * s_im + a_im * s_re + x_im
                return n_re, n_im

            s_re, s_im = lax.fori_loop(0, ROWS_Z // SUBLANES, scan_body,
                                       (s_scr[s, :, 0:half], s_scr[s, :, half:]), unroll=True)
            s_scr[s, :, 0:half] = s_re
            s_scr[s, :, half:] = s_im

        for s in range(N_SLAB):
            lanes = slice(s * LANES, (s + 1) * LANES)
            y_tot = yi_scr[s] + jnp.dot(xs_scr[s].astype(BF16), r_ref[s], preferred_element_type=F32)
            for j in range(Q):
                y_scr[:, j, :, lanes] = y_tot[:, j * LANES:(j + 1) * LANES].reshape(
                    ROWS_Z // SUBLANES, SUBLANES, LANES)

    def phase_c1(r, carry):
        rows = sub_rows(r)
        crow = pl.ds(r * CHUNK_ROWS, CHUNK_ROWS)
        y = y_scr[crow].reshape(SB, D_SSM) + d_ref[...] * u_scr[crow].reshape(SB, D_SSM)
        act_scr[rows, 0:D_SSM] = jax.nn.gelu(y).astype(BF16)
        for lc in range(N_LC):
            lanes = slice(lc * LANES, (lc + 1) * LANES)

            def conv_piece(rc, c, lc=lc, lanes=lanes):
                r0 = r * SB + rc * CONV_ROWS
                piece = jnp.broadcast_to(dwb_ref[:, lanes], (CONV_ROWS, LANES))
                for j in range(CONV_WIDTH):
                    zrows = pl.ds(pl.multiple_of(r0 + j * BATCH, SUBLANES), CONV_ROWS)
                    piece = piece + dw_ref[j:j + 1, lanes] * z_scr[lc, zrows, :]
                conv_scr[pl.ds(pl.multiple_of(rc * CONV_ROWS, CONV_ROWS), CONV_ROWS), lanes] = piece
                return c

            lax.fori_loop(0, SB // CONV_ROWS, conv_piece, 0, unroll=4)
        acc = conv_scr[...]
        mu = jnp.mean(acc, axis=-1, keepdims=True)
        cen = acc - mu
        var = jnp.mean(cen * cen, axis=-1, keepdims=True)
        ln = cen * lax.rsqrt(var + EPS) * lng_ref[...] + lnb_ref[...]
        act_scr[rows, D_SSM:] = jax.nn.silu(ln).astype(BF16)
        return carry

    lane = lax.broadcasted_iota(I32, (1, LANES), 1).astype(F32)
    grp_mask = lane < float(N_GROUPS_MOE)
    exp_lane = (lane >= float(LANE_EXP0)) & (lane < float(LANE_EXP0 + N_EXPERTS))
    lane_grp = jnp.floor((lane - float(LANE_EXP0)) * (1.0 / EXPERTS_PER_GROUP))
    tri = (lax.broadcasted_iota(I32, (SB, SB), 0) > lax.broadcasted_iota(I32, (SB, SB), 1)).astype(BF16)
    neg_inf = float("-inf")
    big = float(4 * LANES)

    def phase_c3(r, carry):
        rows = sub_rows(r)
        h = hb_scr[rows, :]
        g0 = D_SSM + 2 * D_CONV
        gate_ssm = jnp.dot(h, win_ref[:, g0:g0 + D_MODEL], preferred_element_type=F32) \
            + bgate_ref[:, 0:D_MODEL]
        gate_conv = jnp.dot(h, win_ref[:, g0 + D_MODEL:], preferred_element_type=F32) \
            + bgate_ref[:, D_MODEL:]
        zz = jnp.dot(actb_scr[rows, 0:D_SSM], wglu_ref[...], preferred_element_type=F32)
        y_ssm = zz[:, 0:D_MODEL] * jax.nn.sigmoid(zz[:, D_MODEL:])
        y_conv = jnp.dot(actb_scr[rows, D_SSM:], wco_ref[...], preferred_element_type=F32)

        merged = jax.nn.sigmoid(gate_ssm) * y_ssm + jax.nn.sigmoid(gate_conv) * y_conv
        xb = x_ref[pl.ds(r * BPS, BPS)].reshape(SB, D_MODEL)
        x1 = xb + jnp.dot(merged.astype(BF16), wout_ref[...], preferred_element_type=F32)
        x1_ref[pl.ds(r * BPS, BPS)] = x1.reshape(BPS, TT, D_MODEL)

        h2 = _rms(x1, gmoe_ref[...])
        h2p_ref[rows] = _pack_bf16_pair(h2[:, 0:HALF], h2[:, HALF:]).reshape((SB,) + ROW_TILE)

        h2_hi = h2.astype(BF16)
        h2_lo = (h2 - h2_hi.astype(F32)).astype(BF16)
        l1 = jnp.dot(h2_hi, wr1_ref[...], preferred_element_type=F32)
        l2 = jnp.dot(h2_lo, wr2_ref[...], preferred_element_type=F32)
        logit_scr[rows, :] = l1[:, 0:LANES] + l1[:, LANES:] + l2 + br_ref[...]
        return carry

    def route_previous():
        rows = sub_rows(0)
        logits = logit_scr[...]
        counted = jnp.where(step > 0, 1.0, 0.0)

        lg = jnp.where(grp_mask, logits, neg_inf)
        g_max = jnp.max(lg, axis=-1, keepdims=True)
        g_sel = jnp.min(jnp.where(lg == g_max, lane, big), axis=-1, keepdims=True)
        p_g = 1.0 / jnp.sum(jnp.where(grp_mask, jnp.exp(logits - g_max), 0.0), axis=-1, keepdims=True)
        le = jnp.where(exp_lane & (lane_grp == g_sel), logits, neg_inf)
        m1 = jnp.max(le, axis=-1, keepdims=True)
        i1 = jnp.min(jnp.where(le == m1, lane, big), axis=-1, keepdims=True)
        le2 = jnp.where(lane == i1, neg_inf, le)
        m2 = jnp.max(le2, axis=-1, keepdims=True)
        i2 = jnp.min(jnp.where(le2 == m2, lane, big), axis=-1, keepdims=True)
        e2 = jnp.exp(m2 - m1)
        den = 1.0 + e2
        w_a = (1.0 / den) * p_g
        w_b = (e2 / den) * p_g

        sel1 = lane == i1
        sel2 = lane == i2
        onehot = jnp.where(sel1 | sel2, counted, 0.0)
        prefix = jnp.dot(tri, onehot.astype(BF16), preferred_element_type=F32) + cnt_scr[...]
        rank_a = jnp.sum(jnp.where(sel1, prefix, 0.0), axis=-1, keepdims=True)
        rank_b = jnp.sum(jnp.where(sel2, prefix, 0.0), axis=-1, keepdims=True)
        cnt_scr[...] = cnt_scr[...] + jnp.sum(onehot, axis=0, keepdims=True)

        rec = jnp.where(lane == float(REC_EID0), i1 - float(LANE_EXP0), 0.0)
        rec = jnp.where(lane == float(REC_EID1), i2 - float(LANE_EXP0), rec)
        rec = jnp.where(lane == float(REC_W0), w_a, rec)
        rec = jnp.where(lane == float(REC_W1), w_b, rec)
        rec = jnp.where(lane == float(REC_RANK0), rank_a, rec)
        rec = jnp.where(lane == float(REC_RANK1), rank_b, rec)
        rec_ref[rows, :] = rec
        rect_ref[...] = jnp.transpose(rec)[0:REC_ROWS, :]
        cnt_ref[...] = cnt_scr[...]

    @pl.when(step < N_STEP)
    def _tile():
        route_previous()
        phase_a(0, 0)
        ht_scr[...] = jnp.swapaxes(hb_scr[...].reshape(BATCH, TT, D_MODEL), 0, 1).reshape(TM, D_MODEL)
        phase_a3(0, 0)
        phase_b()
        phase_c1(0, 0)
        z_scr[:, 0:HALO, :] = z_scr[:, TM:TM + HALO, :]
        actb_scr[...] = jnp.swapaxes(act_scr[...].reshape(TT, BATCH, D_SSM + D_CONV), 0, 1).reshape(
            TM, D_SSM + D_CONV)
        phase_c3(0, 0)

    @pl.when(step == N_STEP)
    def _last():
        route_previous()


def _mixer(x, gmix, win, bgate, um, up, ur, a_re, a_im, dvec, wglu, dw, dwb, lng, lnb, wco,
           wout, gmoe, wr1, wr2, br):
    tile = lambda i: jnp.minimum(i, N_STEP - 1)
    routed = lambda i: jnp.maximum(i - 1, 0)
    seq_spec = pl.BlockSpec((BATCH, TT, D_MODEL), lambda i: (0, tile(i), 0))
    in_hbm = pl.BlockSpec(memory_space=pl.ANY)
    in_specs = [
        seq_spec,
        _const_spec((1, D_MODEL)),
        in_hbm,
        _const_spec((1, 2 * D_MODEL)),
        _const_spec(um.shape),
        _const_spec(up.shape),
        _const_spec(ur.shape),
        _const_spec(a_re.shape),
        _const_spec(a_im.shape),
        _const_spec((1, D_SSM)),
        in_hbm,
        _const_spec((CONV_WIDTH, D_CONV)),
        _const_spec((1, D_CONV)),
        _const_spec((1, D_CONV)),
        _const_spec((1, D_CONV)),
        in_hbm,
        in_hbm,
        _const_spec((1, D_MODEL)),
        _const_spec((D_MODEL, 2 * LANES)),
        _const_spec((D_MODEL, LANES)),
        _const_spec((1, LANES)),
    ]
    out_specs = [
        seq_spec,
        pl.BlockSpec((TM,) + ROW_TILE, lambda i: (tile(i), 0, 0)),
        pl.BlockSpec((TM, LANES), lambda i: (routed(i), 0)),
        pl.BlockSpec((REC_ROWS, TM), lambda i: (0, routed(i))),
        pl.BlockSpec((1, LANES), lambda i: (0, 0)),
    ]
    out_shape = [
        jax.ShapeDtypeStruct((BATCH, SEQ, D_MODEL), F32),
        jax.ShapeDtypeStruct((N_TOK,) + ROW_TILE, U32),
        jax.ShapeDtypeStruct((N_TOK, LANES), F32),
        jax.ShapeDtypeStruct((REC_ROWS, N_TOK), F32),
        jax.ShapeDtypeStruct((1, LANES), F32),
    ]
    chunk_shape = (ROWS_Z // SUBLANES, Q, SUBLANES, D_SSM)
    scratch = [
        pltpu.VMEM((TM, D_MODEL), BF16),
        pltpu.VMEM((TM, D_MODEL), BF16),
        pltpu.VMEM(chunk_shape, F32),
        pltpu.VMEM(chunk_shape, F32),
        pltpu.VMEM((N_SLAB, ROWS_Z, Q * LANES), F32),
        pltpu.VMEM((N_SLAB, ROWS_Z, STATE_LANES), F32),
        pltpu.VMEM((N_LC, HALO + TM, LANES), F32),
        pltpu.VMEM((SB, D_CONV), F32),
        pltpu.VMEM((TM, D_SSM + D_CONV), BF16),
        pltpu.VMEM((TM, D_SSM + D_CONV), BF16),
        pltpu.VMEM((TM, LANES), F32),
        pltpu.VMEM((N_SLAB, SUBLANES, STATE_LANES), F32),
        pltpu.VMEM((1, LANES), F32),
        pltpu.VMEM(win.shape, BF16),
        pltpu.VMEM(wglu.shape, BF16),
        pltpu.VMEM(wco.shape, BF16),
        pltpu.VMEM(wout.shape, BF16),
        pltpu.VMEM((W_STAGE_SLOTS, W_STAGE_ROWS, D_IN), F32),
        pltpu.SemaphoreType.DMA((W_STAGE_SLOTS,)),
        pltpu.VMEM((N_SLAB, Q * LANES, Q * LANES + STATE_LANES), BF16),
        pltpu.VMEM((N_SLAB, STATE_LANES, Q * LANES), BF16),
    ]
    assert win.shape == (D_MODEL, D_IN) and wglu.shape == (D_SSM, 2 * D_MODEL)
    assert wco.shape == (D_CONV, D_MODEL) and wout.shape == (D_MODEL, D_MODEL)
    return pl.pallas_call(
        _mixer_kernel,
        grid=(N_STEP + 1,),
        in_specs=in_specs,
        out_specs=out_specs,
        out_shape=out_shape,
        scratch_shapes=scratch,
        compiler_params=pltpu.CompilerParams(
            dimension_semantics=("arbitrary",), vmem_limit_bytes=VMEM_LIMIT),
        name="mixer",
    )(x, gmix, win, bgate, um, up, ur, a_re, a_im, dvec, wglu, dw, dwb, lng, lnb, wco, wout,
      gmoe, wr1, wr2, br)


def _cmul(a, b):
    return a[0] * b[0] - a[1] * b[1], a[0] * b[1] + a[1] * b[0]


def _ssm_matrices(a_re, a_im, log_dt, b_re, b_im, c_re, c_im):
    dt = jnp.exp(log_dt)[:, None]
    mag = jnp.exp(a_re * dt)
    lam = (mag * jnp.cos(a_im * dt), mag * jnp.sin(a_im * dt))
    den = a_re * a_re + a_im * a_im
    nr = lam[0] - 1.0
    ni = lam[1]
    z_re = (nr * a_re + ni * a_im) / den
    z_im = (ni * a_re - nr * a_im) / den
    bbar = (z_re[..., None] * b_re - z_im[..., None] * b_im,
            z_re[..., None] * b_im + z_im[..., None] * b_re)
    pw = [(jnp.ones_like(lam[0]), jnp.zeros_like(lam[0])), lam]
    for _ in range(2, Q + 1):
        pw.append(_cmul(pw[-1], lam))
    e = [(c_re * p[0][:, None, :] - c_im * p[1][:, None, :],
          c_re * p[1][:, None, :] + c_im * p[0][:, None, :]) for p in pw]
    e_cat = jnp.concatenate([jnp.concatenate([e[m][0], -e[m][1]], axis=-1) for m in range(Q)], axis=1)
    k_cat = jnp.einsum('gcn,gnd->gcd', e_cat, jnp.concatenate(bbar, axis=1), precision=lax.Precision.HIGHEST)
    k = [k_cat[:, m * SSM_GROUP_WIDTH:(m + 1) * SSM_GROUP_WIDTH, :] for m in range(Q)]
    split = lambda t: t.reshape((N_SLAB, GROUPS_PER_SLAB) + t.shape[1:])
    zero_k = jnp.zeros_like(k[0])
    kb = jnp.stack([jnp.stack([split(jnp.swapaxes(k[j - i] if j >= i else zero_k, 1, 2))
                               for j in range(Q)]) for i in range(Q)])
    um = jnp.transpose(kb, (2, 0, 3, 4, 1, 5)).reshape(N_SLAB, Q * LANES, Q * SSM_GROUP_WIDTH)
    f = [_cmul((pw[Q - 1 - i][0][..., None], pw[Q - 1 - i][1][..., None]), bbar) for i in range(Q)]
    fs = jnp.stack([jnp.stack([split(f[i][part]) for i in range(Q)]) for part in range(2)])
    up = jnp.transpose(fs, (2, 1, 3, 5, 0, 4)).reshape(N_SLAB, Q * LANES, 2 * SSM_STATE)
    es = jnp.stack([sign * jnp.stack([split(e[j + 1][part]) for j in range(Q)])
                    for part, sign in ((0, 1.0), (1, -1.0))])
    ur = jnp.transpose(es, (2, 0, 5, 1, 3, 4)).reshape(N_SLAB, 2 * SSM_STATE, Q * LANES)
    a_q = pw[Q]
    return (um.astype(BF16), up.astype(BF16), ur.astype(BF16),
            a_q[0].reshape(N_SLAB, STATE_LANES // 2), a_q[1].reshape(N_SLAB, STATE_LANES // 2))


def _router_weights(w_rg, b_rg, w_re, b_re):
    pad_g = LANE_EXP0 - LANE_GRP0 - N_GROUPS_MOE
    pad_e = LANES - LANE_EXP0 - N_EXPERTS
    w = jnp.concatenate([w_rg, jnp.zeros((D_MODEL, pad_g), F32), w_re, jnp.zeros((D_MODEL, pad_e), F32)], axis=1)
    b = jnp.concatenate([b_rg, jnp.zeros((pad_g,), F32), b_re, jnp.zeros((pad_e,), F32)]).reshape(1, LANES)
    w_hi = w.astype(BF16)
    w_lo = (w - w_hi.astype(F32)).astype(BF16)
    return jnp.concatenate([w_hi, w_lo], axis=1), w_hi, b


def _sc_mesh():
    return plsc.VectorSubcoreMesh(core_axis_name="core", subcore_axis_name="subcore")


def _sc_worker(mesh):
    return lax.axis_index("core") * mesh.num_subcores + lax.axis_index("subcore")


def _dispatch(h2p, dest):
    mesh = _sc_mesh()
    n_win = N_TOK // SC_WINDOW
    per_worker = n_win // (mesh.num_cores * mesh.num_subcores)
    assert per_worker * mesh.num_cores * mesh.num_subcores == n_win

    @pl.kernel(out_type=jax.ShapeDtypeStruct((N_ROWS,) + ROW_TILE, U32), mesh=mesh,
               scratch_types=[pltpu.VMEM((SC_WINDOW,), I32), pltpu.VMEM((SC_WINDOW,) + ROW_TILE, U32)])
    def scatter_rows(h_hbm, dest_hbm, xs_hbm, idx_v, rows_v):
        first = _sc_worker(mesh) * per_worker

        @pl.loop(0, per_worker)
        def _(w):
            win = first + w
            pltpu.sync_copy(h_hbm.at[pl.ds(win * SC_WINDOW, SC_WINDOW)], rows_v)
            for j in range(TOPK):
                pltpu.sync_copy(dest_hbm.at[j, win], idx_v)
                pltpu.sync_copy(rows_v, xs_hbm.at[idx_v])

    return scatter_rows(h2p, dest)


def _collect(ys, dest):
    mesh = _sc_mesh()
    n_tok = dest.shape[1]
    n_win = TOPK * n_tok // SC_WINDOW
    per_worker = n_win // (mesh.num_cores * mesh.num_subcores)
    assert per_worker * mesh.num_cores * mesh.num_subcores == n_win

    @pl.kernel(out_type=jax.ShapeDtypeStruct((TOPK * n_tok,) + ROW_TILE, U32), mesh=mesh,
               scratch_types=[pltpu.VMEM((SC_WINDOW,), I32), pltpu.VMEM((SC_WINDOW,) + ROW_TILE, U32)])
    def gather_rows(ys_hbm, dest_hbm, yg_hbm, idx_v, rows_v):
        first = _sc_worker(mesh) * per_worker

        @pl.loop(0, per_worker)
        def _(w):
            win = first + w
            pltpu.sync_copy(dest_hbm.at[win], idx_v)
            pltpu.sync_copy(ys_hbm.at[idx_v], rows_v)
            pltpu.sync_copy(rows_v, yg_hbm.at[pl.ds(win * SC_WINDOW, SC_WINDOW)])

    return gather_rows(ys, dest.reshape(n_win, SC_WINDOW)).reshape((TOPK, n_tok) + ROW_TILE)


def _expert_kernel(first_ref, nblk_ref, nvalid_ref, nused_ref, xs_hbm, wg_ref, wu_ref, wd_ref, ys_hbm,
                   wg_scr, wu_scr, wd_scr, x_buf, y_buf, in_sem, out_sem):
    e = pl.program_id(0)
    nused = nused_ref[0]

    def in_copy(g):
        slot = lax.rem(g, IN_SLOTS)
        return pltpu.make_async_copy(xs_hbm.at[pl.ds(g * BM, BM)], x_buf.at[slot], in_sem.at[slot])

    def out_copy(g, slot):
        return pltpu.make_async_copy(y_buf.at[slot], ys_hbm.at[pl.ds(g * BM, BM)], out_sem.at[slot])

    @pl.when(e == 0)
    def _first():
        for g in range(IN_AHEAD):
            in_copy(g).start()

    wg_scr[...] = wg_ref[0].astype(BF16)
    wu_scr[...] = wu_ref[0].astype(BF16)
    wd_scr[...] = wd_ref[0].astype(BF16)

    def block(b, carry):
        g = first_ref[e] + b
        slot = lax.rem(g, 2)
        in_copy(g).wait()

        @pl.when(g + IN_AHEAD < nused)
        def _prefetch():
            in_copy(g + IN_AHEAD).start()

        @pl.when(g >= 2)
        def _slot_free():
            out_copy(g - 2, slot).wait()

        valid = lax.broadcasted_iota(I32, (BM, 1), 0) < nvalid_ref[g]
        x_blk = x_buf[lax.rem(g, IN_SLOTS)].reshape(BM, HALF)
        lo, hi = _unpack_bf16_pair(jnp.where(valid, x_blk, jnp.uint32(0)))
        lo = lo.astype(BF16)
        hi = hi.astype(BF16)
        gate = jnp.dot(lo, wg_scr[0:HALF, :], preferred_element_type=F32) \
            + jnp.dot(hi, wg_scr[HALF:, :], preferred_element_type=F32)
        up = jnp.dot(lo, wu_scr[0:HALF, :], preferred_element_type=F32) \
            + jnp.dot(hi, wu_scr[HALF:, :], preferred_element_type=F32)
        act = (jax.nn.silu(gate) * up).astype(BF16)
        o = jnp.dot(act, wd_scr[...], preferred_element_type=F32)
        y_buf[slot] = _pack_bf16_pair(o[:, 0:HALF], o[:, HALF:]).reshape((BM,) + ROW_TILE)
        out_copy(g, slot).start()
        return carry

    lax.fori_loop(0, nblk_ref[e], block, 0)

    @pl.when(e == N_EXPERTS - 1)
    def _drain():
        out_copy(nused - 2, lax.rem(nused, 2)).wait()
        out_copy(nused - 1, 1 - lax.rem(nused, 2)).wait()


def _experts(first, nblk, nvalid, nused, xs, wg, wu, wd):
    grid_spec = pltpu.PrefetchScalarGridSpec(
        num_scalar_prefetch=4,
        grid=(N_EXPERTS,),
        in_specs=[
            pl.BlockSpec(memory_space=pl.ANY),
            pl.BlockSpec((1, D_MODEL, D_EXPERT), lambda e, *_: (e, 0, 0)),
            pl.BlockSpec((1, D_MODEL, D_EXPERT), lambda e, *_: (e, 0, 0)),
            pl.BlockSpec((1, D_EXPERT, D_MODEL), lambda e, *_: (e, 0, 0)),
        ],
        out_specs=pl.BlockSpec(memory_space=pl.ANY),
        scratch_shapes=[
            pltpu.VMEM((D_MODEL, D_EXPERT), BF16),
            pltpu.VMEM((D_MODEL, D_EXPERT), BF16),
            pltpu.VMEM((D_EXPERT, D_MODEL), BF16),
            pltpu.VMEM((IN_SLOTS, BM) + ROW_TILE, U32),
            pltpu.VMEM((2, BM) + ROW_TILE, U32),
            pltpu.SemaphoreType.DMA((IN_SLOTS,)),
            pltpu.SemaphoreType.DMA((2,)),
        ],
    )
    return pl.pallas_call(
        _expert_kernel,
        grid_spec=grid_spec,
        out_shape=jax.ShapeDtypeStruct((N_ROWS,) + ROW_TILE, U32),
        compiler_params=pltpu.CompilerParams(
            dimension_semantics=("arbitrary",), vmem_limit_bytes=VMEM_LIMIT),
        name="experts",
    )(first, nblk, nvalid, nused, xs, wg, wu, wd)


def _combine_kernel(x1_ref, rec_ref, yg_ref, p_ref, gple_ref, wpg_ref, wple_ref, gfin_ref, *rest):
    out_ref = rest[-1]
    ple = jnp.dot(p_ref[0].reshape(TM, D_PLE).astype(BF16), wple_ref[...], preferred_element_type=F32)
    rec = rec_ref[...]
    w0 = rec[:, REC_W0:REC_W0 + 1]
    w1 = rec[:, REC_W1:REC_W1 + 1]
    lo0, hi0 = _unpack_bf16_pair(yg_ref[0].reshape(TM, HALF))
    lo1, hi1 = _unpack_bf16_pair(yg_ref[1].reshape(TM, HALF))
    moe = jnp.concatenate([lo0 * w0 + lo1 * w1, hi0 * w0 + hi1 * w1], axis=1)
    x2 = x1_ref[...].reshape(TM, D_MODEL) + moe
    gate = jax.nn.sigmoid(jnp.dot(_rms(x2, gple_ref[...]).astype(BF16), wpg_ref[...],
                                  preferred_element_type=F32))
    x3 = x2 + gate * ple
    out_ref[...] = _rms(x3, gfin_ref[...]).reshape(BATCH, TT, D_MODEL)


def _combine(s0, n_steps, x1, rec, yg, p, gple, wpg, wple, gfin, out_prev=None):
    seq_spec = pl.BlockSpec((BATCH, TT, D_MODEL), lambda i: (0, s0 + i, 0))
    in_specs = [
        seq_spec,
        pl.BlockSpec((TM, LANES), lambda i: (s0 + i, 0)),
        pl.BlockSpec((TOPK, TM) + ROW_TILE, lambda i: (0, i, 0, 0)),
        pl.BlockSpec((1, BATCH, TT, D_PLE), lambda i: (0, 0, s0 + i, 0)),
        _const_spec((1, D_MODEL)),
        _const_spec((D_MODEL, D_MODEL)),
        _const_spec((D_PLE, D_MODEL)),
        _const_spec((1, D_MODEL)),
    ]
    args = [x1, rec, yg, p, gple, wpg, wple, gfin]
    aliases = {}
    if out_prev is not None:
        in_specs.append(pl.BlockSpec(memory_space=pl.ANY))
        args.append(out_prev)
        aliases = {len(args) - 1: 0}
    return pl.pallas_call(
        _combine_kernel,
        grid=(n_steps,),
        in_specs=in_specs,
        out_specs=seq_spec,
        out_shape=jax.ShapeDtypeStruct((BATCH, SEQ, D_MODEL), F32),
        input_output_aliases=aliases,
        compiler_params=pltpu.CompilerParams(
            dimension_semantics=("arbitrary",), vmem_limit_bytes=VMEM_LIMIT),
        name="combine",
    )(*args)


def kernel(x, p, g_mix, w_in, b_gate, ssm_a_re, ssm_a_im, ssm_log_dt, ssm_b_re, ssm_b_im, ssm_c_re,
           ssm_c_im, ssm_d, w_glu, conv_dw, conv_dw_b, conv_ln_g, conv_ln_b, w_conv_out, w_out, g_moe,
           w_router_group, b_router_group, w_router_expert, b_router_expert, w_exp_gate, w_exp_up,
           w_exp_down, g_ple, w_ple_gate, w_ple, g_final):
    assert x.shape == (BATCH, SEQ, D_MODEL) and p.shape == (1, BATCH, SEQ, D_PLE)
    row = lambda v: v.reshape(1, -1)

    um, up, ur, a_re, a_im = _ssm_matrices(ssm_a_re[0], ssm_a_im[0], ssm_log_dt[0], ssm_b_re[0],
                                           ssm_b_im[0], ssm_c_re[0], ssm_c_im[0])
    wr1, wr2, br = _router_weights(w_router_group[0], b_router_group[0], w_router_expert[0],
                                   b_router_expert[0])
    x1, h2p, rec, rect, cnt = _mixer(
        x, row(g_mix[0]), w_in[0], row(b_gate[0]), um, up, ur, a_re, a_im,
        row(ssm_d[0]), w_glu[0], conv_dw[0], row(conv_dw_b[0]), row(conv_ln_g[0]),
        row(conv_ln_b[0]), w_conv_out[0], w_out[0], row(g_moe[0]), wr1, wr2, br)

    counts = cnt[0, LANE_EXP0:LANE_EXP0 + N_EXPERTS].astype(I32)
    pcounts = (counts + BM - 1) // BM * BM
    pends = jnp.cumsum(pcounts)
    pstarts = pends - pcounts
    eid = rect[REC_EID0:REC_EID1 + 1].astype(I32)
    rank = rect[REC_RANK0:REC_RANK1 + 1].astype(I32)
    dest = (jnp.sum(jnp.where(eid[..., None] == jnp.arange(N_EXPERTS, dtype=I32), pstarts, 0), axis=-1)
            + rank).reshape(TOPK, N_TOK // SC_WINDOW, SC_WINDOW)
    nused = (pends[-1] // BM).astype(I32)
    blk = jnp.arange(N_BLK, dtype=I32)[:, None] * BM
    in_expert = (pstarts[None, :] <= blk) & (blk < pends[None, :])
    nvalid = jnp.clip(jnp.sum(jnp.where(in_expert, (pstarts + counts)[None, :] - blk, 0), axis=1), 0, BM)

    xs = _dispatch(h2p, dest)
    ys = _experts(pstarts // BM, pcounts // BM, nvalid.astype(I32), nused.reshape(1), xs,
                  w_exp_gate[0], w_exp_up[0], w_exp_down[0])
    dest_tok = dest.reshape(TOPK, N_TOK)
    wpg = w_ple_gate[0].astype(BF16)
    wple = w_ple[0].astype(BF16)
    out = None
    s0 = 0
    for n_steps in PART_STEPS:
        yg = _collect(ys, dest_tok[:, s0 * TM:(s0 + n_steps) * TM])
        out = _combine(s0, n_steps, x1, rec, yg, p, row(g_ple[0]), wpg, wple, row(g_final), out)
        s0 += n_steps
    return out
```

```python
import jax
import jax.numpy as jnp
from jax import lax
from jax.experimental import pallas as pl
from jax.experimental.pallas import tpu as pltpu
from jax.experimental.pallas import tpu_sc as plsc

F32 = jnp.float32
BF16 = jnp.bfloat16
U32 = jnp.uint32
I32 = jnp.int32

D_MODEL = 1024
BATCH = 8
SEQ = 2048
N_TOK = BATCH * SEQ
D_SSM = 512
SSM_GROUP_WIDTH = 16
SSM_GROUPS = 32
SSM_STATE = 64
D_CONV = 512
CONV_WIDTH = 31
D_IN = D_SSM + 2 * D_CONV + 2 * D_MODEL
N_GROUPS_MOE = 4
EXPERTS_PER_GROUP = 8
N_EXPERTS = 32
TOPK = 2
D_EXPERT = 512
D_PLE = 256
EPS = 1e-6

SUBLANES = 8
LANES = 128
assert BATCH == SUBLANES

TT = 64
TM = TT * BATCH
N_STEP = SEQ // TT
SB = 512
NSB = TM // SB
BPS = SB // TT
Q = 2
N_SLAB = D_SSM // LANES
GROUPS_PER_SLAB = SSM_GROUPS // N_SLAB
ROWS_Z = TM // Q
STATE_LANES = 2 * GROUPS_PER_SLAB * SSM_STATE
HALO = (CONV_WIDTH - 1) * BATCH
CHUNK_ROWS = SB // (Q * SUBLANES)
W_STAGE_ROWS = 64
W_STAGE_SLOTS = 4
CONV_ROWS = 64
N_LC = D_CONV // LANES

LANE_GRP0 = 0
LANE_EXP0 = 32
REC_EID0, REC_EID1, REC_W0, REC_W1, REC_RANK0, REC_RANK1 = 0, 1, 2, 3, 4, 5
REC_ROWS = 8

BM = 256
N_BLK = (TOPK * N_TOK + N_EXPERTS * (BM - 1) + BM - 1) // BM
N_ROWS = N_BLK * BM
HALF = D_MODEL // 2
ROW_TILE = (HALF // LANES, LANES)
SC_WINDOW = 128
IN_AHEAD = 3
IN_SLOTS = IN_AHEAD + 1
PART_STEPS = (12, 20)
assert sum(PART_STEPS) == N_STEP

VMEM_LIMIT = 56 * 1024 * 1024


def _const_spec(shape):
    n = len(shape)
    return pl.BlockSpec(shape, lambda *_: (0,) * n, pipeline_mode=pl.Buffered(1))


def _rms(x, g):
    ms = jnp.mean(x * x, axis=-1, keepdims=True)
    return x * lax.rsqrt(ms + EPS) * g


def _pack_bf16_pair(lo, hi):
    ulo = lax.bitcast_convert_type(lo.astype(BF16).astype(F32), U32)
    uhi = lax.bitcast_convert_type(hi.astype(BF16).astype(F32), U32)
    return (ulo >> 16) | (uhi & jnp.uint32(0xFFFF0000))


def _unpack_bf16_pair(w):
    lo = lax.bitcast_convert_type(w << 16, F32)
    hi = lax.bitcast_convert_type(w & jnp.uint32(0xFFFF0000), F32)
    return lo, hi


def _load_weights_bf16(pairs, stage, sem):
    chunks = [(src, dst, r0) for src, dst in pairs for r0 in range(0, src.shape[0], W_STAGE_ROWS)]

    def copy(c):
        src, _, r0 = chunks[c]
        slot = c % W_STAGE_SLOTS
        return pltpu.make_async_copy(src.at[pl.ds(r0, W_STAGE_ROWS)],
                                     stage.at[slot, :, 0:src.shape[1]], sem.at[slot])

    for c in range(W_STAGE_SLOTS):
        copy(c).start()
    for c, (src, dst, r0) in enumerate(chunks):
        copy(c).wait()
        dst[r0:r0 + W_STAGE_ROWS, :] = stage[c % W_STAGE_SLOTS, :, 0:src.shape[1]].astype(BF16)
        if c + W_STAGE_SLOTS < len(chunks):
            copy(c + W_STAGE_SLOTS).start()


def _expand_ssm(um_ref, up_ref, ur_ref, mp_ref, r_ref):
    gw, ns, half = SSM_GROUP_WIDTH, SSM_STATE, STATE_LANES // 2
    div = lambda a, n: lax.shift_right_logical(a, n.bit_length() - 1)
    mod = lambda a, n: a & (n - 1)
    iota2 = lambda shape: (lax.broadcasted_iota(I32, shape, 0), lax.broadcasted_iota(I32, shape, 1))
    one = lambda cond: jnp.where(cond, 1.0, 0.0).astype(BF16)

    r, q = iota2((Q * gw, Q * LANES))
    x_m = one((div(r, gw) == div(q, LANES)) & (mod(r, gw) == mod(q, gw)))
    r, q = iota2((2 * ns, STATE_LANES))
    x_p = one((div(r, ns) == div(q, half)) & (mod(r, ns) == mod(q, ns)))
    p, r = iota2((STATE_LANES, 2 * ns))
    x_r = one((div(p, half) == div(r, ns)) & (mod(p, ns) == mod(r, ns)))
    p, q = iota2((Q * LANES, Q * LANES))
    same_m = div(mod(p, LANES), gw) == div(mod(q, LANES), gw)
    p, q = iota2((Q * LANES, STATE_LANES))
    same_p = div(mod(p, LANES), gw) == div(mod(q, half), ns)
    p, q = iota2((STATE_LANES, Q * LANES))
    same_r = div(mod(p, half), ns) == div(mod(q, LANES), gw)
    for s in range(N_SLAB):
        m = jnp.dot(um_ref[s], x_m, preferred_element_type=F32)
        mp_ref[s, :, 0:Q * LANES] = jnp.where(same_m, m, 0.0).astype(BF16)
        pm = jnp.dot(up_ref[s], x_p, preferred_element_type=F32)
        mp_ref[s, :, Q * LANES:] = jnp.where(same_p, pm, 0.0).astype(BF16)
        rm = jnp.dot(x_r, ur_ref[s], preferred_element_type=F32)
        r_ref[s] = jnp.where(same_r, rm, 0.0).astype(BF16)


def _mixer_kernel(x_ref, gmix_ref, win_hbm, bgate_ref, um_ref, up_ref, ur_ref, are_ref,
                  aim_ref, d_ref, wglu_hbm, dw_ref, dwb_ref, lng_ref, lnb_ref, wco_hbm, wout_hbm,
                  gmoe_ref, wr1_ref, wr2_ref, br_ref,
                  x1_ref, h2p_ref, rec_ref, rect_ref, cnt_ref,
                  hb_scr, ht_scr, u_scr, y_scr, yi_scr, xs_scr, z_scr, conv_scr, act_scr, actb_scr,
                  logit_scr, s_scr, win_ref, wglu_ref, wco_ref, wout_ref, wstage_scr, wstage_sem,
                  mp_ref, r_ref, cnt_scr):
    step = pl.program_id(0)
    assert NSB == 1

    @pl.when(step == 0)
    def _init():
        logit_scr[...] = jnp.zeros(logit_scr.shape, F32)
        z_scr[:, 0:HALO, :] = jnp.zeros((N_LC, HALO, LANES), F32)
        s_scr[...] = jnp.zeros(s_scr.shape, F32)
        cnt_scr[...] = jnp.zeros(cnt_scr.shape, F32)
        _expand_ssm(um_ref, up_ref, ur_ref, mp_ref, r_ref)
        _load_weights_bf16([(win_hbm, win_ref), (wglu_hbm, wglu_ref), (wco_hbm, wco_ref),
                            (wout_hbm, wout_ref)], wstage_scr, wstage_sem)

    def sub_rows(r):
        return pl.ds(pl.multiple_of(r * SB, SB), SB)

    def phase_a(r, carry):
        xb = x_ref[pl.ds(r * BPS, BPS)].reshape(SB, D_MODEL)
        hb_scr[sub_rows(r), :] = _rms(xb, gmix_ref[...]).astype(BF16)
        return carry

    def phase_a3(r, carry):
        h = ht_scr[sub_rows(r), :]
        u = jnp.dot(h, win_ref[:, 0:D_SSM], preferred_element_type=F32)
        u_scr[pl.ds(r * CHUNK_ROWS, CHUNK_ROWS)] = u.reshape(CHUNK_ROWS, Q, SUBLANES, D_SSM)
        v = jnp.dot(h, win_ref[:, D_SSM:D_SSM + 2 * D_CONV], preferred_element_type=F32)
        zc = v[:, 0:D_CONV] * jax.nn.sigmoid(v[:, D_CONV:])
        for lc in range(N_LC):
            z_scr[lc, pl.ds(pl.multiple_of(HALO + r * SB, SUBLANES), SB), :] = zc[:, lc * LANES:(lc + 1) * LANES]
        return carry

    def phase_b():
        for s in range(N_SLAB):
            lanes = slice(s * LANES, (s + 1) * LANES)
            z = jnp.concatenate(
                [u_scr[:, i, :, lanes].reshape(ROWS_Z, LANES) for i in range(Q)], axis=1).astype(BF16)
            xp = jnp.dot(z, mp_ref[s], preferred_element_type=F32)
            yi_scr[s] = xp[:, 0:Q * LANES]
            xs_scr[s] = xp[:, Q * LANES:]

        half = STATE_LANES // 2
        for s in range(N_SLAB):
            a_re = jnp.broadcast_to(are_ref[s:s + 1, :], (SUBLANES, half))
            a_im = jnp.broadcast_to(aim_ref[s:s + 1, :], (SUBLANES, half))

            def scan_body(k, carry, s=s, a_re=a_re, a_im=a_im):
                s_re, s_im = carry
                rows = pl.ds(pl.multiple_of(k * SUBLANES, SUBLANES), SUBLANES)
                x_re = xs_scr[s, rows, 0:half]
                x_im = xs_scr[s, rows, half:]
                xs_scr[s, rows, 0:half] = s_re
                xs_scr[s, rows, half:] = s_im
                n_re = a_re * s_re - a_im * s_im + x_re
                n_im = a_re * s_im + a_im * s_re + x_im
                return n_re, n_im

            s_re, s_im = lax.fori_loop(0, ROWS_Z // SUBLANES, scan_body,
                                       (s_scr[s, :, 0:half], s_scr[s, :, half:]), unroll=True)
            s_scr[s, :, 0:half] = s_re
            s_scr[s, :, half:] = s_im

        for s in range(N_SLAB):
            lanes = slice(s * LANES, (s + 1) * LANES)
            y_tot = yi_scr[s] + jnp.dot(xs_scr[s].astype(BF16), r_ref[s], preferred_element_type=F32)
            for j in range(Q):
                y_scr[:, j, :, lanes] = y_tot[:, j * LANES:(j + 1) * LANES].reshape(
                    ROWS_Z // SUBLANES, SUBLANES, LANES)

    def phase_c1(r, carry):
        rows = sub_rows(r)
        crow = pl.ds(r * CHUNK_ROWS, CHUNK_ROWS)
        y = y_scr[crow].reshape(SB, D_SSM) + d_ref[...] * u_scr[crow].reshape(SB, D_SSM)
        act_scr[rows, 0:D_SSM] = jax.nn.gelu(y).astype(BF16)
        for lc in range(N_LC):
            lanes = slice(lc * LANES, (lc + 1) * LANES)

            def conv_piece(rc, c, lc=lc, lanes=lanes):
                r0 = r * SB + rc * CONV_ROWS
                piece = jnp.broadcast_to(dwb_ref[:, lanes], (CONV_ROWS, LANES))
                for j in range(CONV_WIDTH):
                    zrows = pl.ds(pl.multiple_of(r0 + j * BATCH, SUBLANES), CONV_ROWS)
                    piece = piece + dw_ref[j:j + 1, lanes] * z_scr[lc, zrows, :]
                conv_scr[pl.ds(pl.multiple_of(rc * CONV_ROWS, CONV_ROWS), CONV_ROWS), lanes] = piece
                return c

            lax.fori_loop(0, SB // CONV_ROWS, conv_piece, 0, unroll=4)
        acc = conv_scr[...]
        mu = jnp.mean(acc, axis=-1, keepdims=True)
        cen = acc - mu
        var = jnp.mean(cen * cen, axis=-1, keepdims=True)
        ln = cen * lax.rsqrt(var + EPS) * lng_ref[...] + lnb_ref[...]
        act_scr[rows, D_SSM:] = jax.nn.silu(ln).astype(BF16)
        return carry

    lane = lax.broadcasted_iota(I32, (1, LANES), 1).astype(F32)
    grp_mask = lane < float(N_GROUPS_MOE)
    exp_lane = (lane >= float(LANE_EXP0)) & (lane < float(LANE_EXP0 + N_EXPERTS))
    lane_grp = jnp.floor((lane - float(LANE_EXP0)) * (1.0 / EXPERTS_PER_GROUP))
    tri = (lax.broadcasted_iota(I32, (SB, SB), 0) > lax.broadcasted_iota(I32, (SB, SB), 1)).astype(BF16)
    neg_inf = float("-inf")
    big = float(4 * LANES)

    def phase_c3(r, carry):
        rows = sub_rows(r)
        h = hb_scr[rows, :]
        g0 = D_SSM + 2 * D_CONV
        gate_ssm = jnp.dot(h, win_ref[:, g0:g0 + D_MODEL], preferred_element_type=F32) \
            + bgate_ref[:, 0:D_MODEL]
        gate_conv = jnp.dot(h, win_ref[:, g0 + D_MODEL:], preferred_element_type=F32) \
            + bgate_ref[:, D_MODEL:]
        zz = jnp.dot(actb_scr[rows, 0:D_SSM], wglu_ref[...], preferred_element_type=F32)
        y_ssm = zz[:, 0:D_MODEL] * jax.nn.sigmoid(zz[:, D_MODEL:])
        y_conv = jnp.dot(actb_scr[rows, D_SSM:], wco_ref[...], preferred_element_type=F32)

        merged = jax.nn.sigmoid(gate_ssm) * y_ssm + jax.nn.sigmoid(gate_conv) * y_conv
        xb = x_ref[pl.ds(r * BPS, BPS)].reshape(SB, D_MODEL)
        x1 = xb + jnp.dot(merged.astype(BF16), wout_ref[...], preferred_element_type=F32)
        x1_ref[pl.ds(r * BPS, BPS)] = x1.reshape(BPS, TT, D_MODEL)

        h2 = _rms(x1, gmoe_ref[...])
        h2p_ref[rows] = _pack_bf16_pair(h2[:, 0:HALF], h2[:, HALF:]).reshape((SB,) + ROW_TILE)

        h2_hi = h2.astype(BF16)
        h2_lo = (h2 - h2_hi.astype(F32)).astype(BF16)
        l1 = jnp.dot(h2_hi, wr1_ref[...], preferred_element_type=F32)
        l2 = jnp.dot(h2_lo, wr2_ref[...], preferred_element_type=F32)
        logit_scr[rows, :] = l1[:, 0:LANES] + l1[:, LANES:] + l2 + br_ref[...]
        return carry

    def route_previous():
        rows = sub_rows(0)
        logits = logit_scr[...]
        counted = jnp.where(step > 0, 1.0, 0.0)

        lg = jnp.where(grp_mask, logits, neg_inf)
        g_max = jnp.max(lg, axis=-1, keepdims=True)
        g_sel = jnp.min(jnp.where(lg == g_max, lane, big), axis=-1, keepdims=True)
        p_g = 1.0 / jnp.sum(jnp.where(grp_mask, jnp.exp(logits - g_max), 0.0), axis=-1, keepdims=True)
        le = jnp.where(exp_lane & (lane_grp == g_sel), logits, neg_inf)
        m1 = jnp.max(le, axis=-1, keepdims=True)
        i1 = jnp.min(jnp.where(le == m1, lane, big), axis=-1, keepdims=True)
        le2 = jnp.where(lane == i1, neg_inf, le)
        m2 = jnp.max(le2, axis=-1, keepdims=True)
        i2 = jnp.min(jnp.where(le2 == m2, lane, big), axis=-1, keepdims=True)
        e2 = jnp.exp(m2 - m1)
        den = 1.0 + e2
        w_a = (1.0 / den) * p_g
        w_b = (e2 / den) * p_g

        sel1 = lane == i1
        sel2 = lane == i2
        onehot = jnp.where(sel1 | sel2, counted, 0.0)
        prefix = jnp.dot(tri, onehot.astype(BF16), preferred_element_type=F32) + cnt_scr[...]
        rank_a = jnp.sum(jnp.where(sel1, prefix, 0.0), axis=-1, keepdims=True)
        rank_b = jnp.sum(jnp.where(sel2, prefix, 0.0), axis=-1, keepdims=True)
        cnt_scr[...] = cnt_scr[...] + jnp.sum(onehot, axis=0, keepdims=True)

        rec = jnp.where(lane == float(REC_EID0), i1 - float(LANE_EXP0), 0.0)
        rec = jnp.where(lane == float(REC_EID1), i2 - float(LANE_EXP0), rec)
        rec = jnp.where(lane == float(REC_W0), w_a, rec)
        rec = jnp.where(lane == float(REC_W1), w_b, rec)
        rec = jnp.where(lane == float(REC_RANK0), rank_a, rec)
        rec = jnp.where(lane == float(REC_RANK1), rank_b, rec)
        rec_ref[rows, :] = rec
        rect_ref[...] = jnp.transpose(rec)[0:REC_ROWS, :]
        cnt_ref[...] = cnt_scr[...]

    @pl.when(step < N_STEP)
    def _tile():
        route_previous()
        phase_a(0, 0)
        ht_scr[...] = jnp.swapaxes(hb_scr[...].reshape(BATCH, TT, D_MODEL), 0, 1).reshape(TM, D_MODEL)
        phase_a3(0, 0)
        phase_b()
        phase_c1(0, 0)
        z_scr[:, 0:HALO, :] = z_scr[:, TM:TM + HALO, :]
        actb_scr[...] = jnp.swapaxes(act_scr[...].reshape(TT, BATCH, D_SSM + D_CONV), 0, 1).reshape(
            TM, D_SSM + D_CONV)
        phase_c3(0, 0)

    @pl.when(step == N_STEP)
    def _last():
        route_previous()


def _mixer(x, gmix, win, bgate, um, up, ur, a_re, a_im, dvec, wglu, dw, dwb, lng, lnb, wco,
           wout, gmoe, wr1, wr2, br):
    tile = lambda i: jnp.minimum(i, N_STEP - 1)
    routed = lambda i: jnp.maximum(i - 1, 0)
    seq_spec = pl.BlockSpec((BATCH, TT, D_MODEL), lambda i: (0, tile(i), 0))
    in_hbm = pl.BlockSpec(memory_space=pl.ANY)
    in_specs = [
        seq_spec,
        _const_spec((1, D_MODEL)),
        in_hbm,
        _const_spec((1, 2 * D_MODEL)),
        _const_spec(um.shape),
        _const_spec(up.shape),
        _const_spec(ur.shape),
        _const_spec(a_re.shape),
        _const_spec(a_im.shape),
        _const_spec((1, D_SSM)),
        in_hbm,
        _const_spec((CONV_WIDTH, D_CONV)),
        _const_spec((1, D_CONV)),
        _const_spec((1, D_CONV)),
        _const_spec((1, D_CONV)),
        in_hbm,
        in_hbm,
        _const_spec((1, D_MODEL)),
        _const_spec((D_MODEL, 2 * LANES)),
        _const_spec((D_MODEL, LANES)),
        _const_spec((1, LANES)),
    ]
    out_specs = [
        seq_spec,
        pl.BlockSpec((TM,) + ROW_TILE, lambda i: (tile(i), 0, 0)),
        pl.BlockSpec((TM, LANES), lambda i: (routed(i), 0)),
        pl.BlockSpec((REC_ROWS, TM), lambda i: (0, routed(i))),
        pl.BlockSpec((1, LANES), lambda i: (0, 0)),
    ]
    out_shape = [
        jax.ShapeDtypeStruct((BATCH, SEQ, D_MODEL), F32),
        jax.ShapeDtypeStruct((N_TOK,) + ROW_TILE, U32),
        jax.ShapeDtypeStruct((N_TOK, LANES), F32),
        jax.ShapeDtypeStruct((REC_ROWS, N_TOK), F32),
        jax.ShapeDtypeStruct((1, LANES), F32),
    ]
    chunk_shape = (ROWS_Z // SUBLANES, Q, SUBLANES, D_SSM)
    scratch = [
        pltpu.VMEM((TM, D_MODEL), BF16),
        pltpu.VMEM((TM, D_MODEL), BF16),
        pltpu.VMEM(chunk_shape, F32),
        pltpu.VMEM(chunk_shape, F32),
        pltpu.VMEM((N_SLAB, ROWS_Z, Q * LANES), F32),
        pltpu.VMEM((N_SLAB, ROWS_Z, STATE_LANES), F32),
        pltpu.VMEM((N_LC, HALO + TM, LANES), F32),
        pltpu.VMEM((SB, D_CONV), F32),
        pltpu.VMEM((TM, D_SSM + D_CONV), BF16),
        pltpu.VMEM((TM, D_SSM + D_CONV), BF16),
        pltpu.VMEM((TM, LANES), F32),
        pltpu.VMEM((N_SLAB, SUBLANES, STATE_LANES), F32),
        pltpu.VMEM(win.shape, BF16),
        pltpu.VMEM(wglu.shape, BF16),
        pltpu.VMEM(wco.shape, BF16),
        pltpu.VMEM(wout.shape, BF16),
        pltpu.VMEM((W_STAGE_SLOTS, W_STAGE_ROWS, D_IN), F32),
        pltpu.SemaphoreType.DMA((W_STAGE_SLOTS,)),
        pltpu.VMEM((N_SLAB, Q * LANES, Q * LANES + STATE_LANES), BF16),
        pltpu.VMEM((N_SLAB, STATE_LANES, Q * LANES), BF16),
        pltpu.VMEM((1, LANES), F32),
    ]
    assert win.shape == (D_MODEL, D_IN) and wglu.shape == (D_SSM, 2 * D_MODEL)
    assert wco.shape == (D_CONV, D_MODEL) and wout.shape == (D_MODEL, D_MODEL)
    return pl.pallas_call(
        _mixer_kernel,
        grid=(N_STEP + 1,),
        in_specs=in_specs,
        out_specs=out_specs,
        out_shape=out_shape,
        scratch_shapes=scratch,
        compiler_params=pltpu.CompilerParams(
            dimension_semantics=("arbitrary",), vmem_limit_bytes=VMEM_LIMIT),
        name="mixer",
    )(x, gmix, win, bgate, um, up, ur, a_re, a_im, dvec, wglu, dw, dwb, lng, lnb, wco, wout,
      gmoe, wr1, wr2, br)


def _cmul(a, b):
    return a[0] * b[0] - a[1] * b[1], a[0] * b[1] + a[1] * b[0]


def _ssm_matrices(a_re, a_im, log_dt, b_re, b_im, c_re, c_im):
    dt = jnp.exp(log_dt)[:, None]
    mag = jnp.exp(a_re * dt)
    lam = (mag * jnp.cos(a_im * dt), mag * jnp.sin(a_im * dt))
    den = a_re * a_re + a_im * a_im
    nr = lam[0] - 1.0
    ni = lam[1]
    z_re = (nr * a_re + ni * a_im) / den
    z_im = (ni * a_re - nr * a_im) / den
    bbar = (z_re[..., None] * b_re - z_im[..., None] * b_im,
            z_re[..., None] * b_im + z_im[..., None] * b_re)
    pw = [(jnp.ones_like(lam[0]), jnp.zeros_like(lam[0])), lam]
    for _ in range(2, Q + 1):
        pw.append(_cmul(pw[-1], lam))
    e = [(c_re * p[0][:, None, :] - c_im * p[1][:, None, :],
          c_re * p[1][:, None, :] + c_im * p[0][:, None, :]) for p in pw]
    e_cat = jnp.concatenate([jnp.concatenate([e[m][0], -e[m][1]], axis=-1) for m in range(Q)], axis=1)
    k_cat = jnp.einsum('gcn,gnd->gcd', e_cat, jnp.concatenate(bbar, axis=1), precision=lax.Precision.HIGHEST)
    k = [k_cat[:, m * SSM_GROUP_WIDTH:(m + 1) * SSM_GROUP_WIDTH, :] for m in range(Q)]
    split = lambda t: t.reshape((N_SLAB, GROUPS_PER_SLAB) + t.shape[1:])
    zero_k = jnp.zeros_like(k[0])
    kb = jnp.stack([jnp.stack([split(jnp.swapaxes(k[j - i] if j >= i else zero_k, 1, 2))
                               for j in range(Q)]) for i in range(Q)])
    um = jnp.transpose(kb, (2, 0, 3, 4, 1, 5)).reshape(N_SLAB, Q * LANES, Q * SSM_GROUP_WIDTH)
    f = [_cmul((pw[Q - 1 - i][0][..., None], pw[Q - 1 - i][1][..., None]), bbar) for i in range(Q)]
    fs = jnp.stack([jnp.stack([split(f[i][part]) for i in range(Q)]) for part in range(2)])
    up = jnp.transpose(fs, (2, 1, 3, 5, 0, 4)).reshape(N_SLAB, Q * LANES, 2 * SSM_STATE)
    es = jnp.stack([sign * jnp.stack([split(e[j + 1][part]) for j in range(Q)])
                    for part, sign in ((0, 1.0), (1, -1.0))])
    ur = jnp.transpose(es, (2, 0, 5, 1, 3, 4)).reshape(N_SLAB, 2 * SSM_STATE, Q * LANES)
    a_q = pw[Q]
    return (um.astype(BF16), up.astype(BF16), ur.astype(BF16),
            a_q[0].reshape(N_SLAB, STATE_LANES // 2), a_q[1].reshape(N_SLAB, STATE_LANES // 2))


def _router_weights(w_rg, b_rg, w_re, b_re):
    pad_g = LANE_EXP0 - LANE_GRP0 - N_GROUPS_MOE
    pad_e = LANES - LANE_EXP0 - N_EXPERTS
    w = jnp.concatenate([w_rg, jnp.zeros((D_MODEL, pad_g), F32), w_re, jnp.zeros((D_MODEL, pad_e), F32)], axis=1)
    b = jnp.concatenate([b_rg, jnp.zeros((pad_g,), F32), b_re, jnp.zeros((pad_e,), F32)]).reshape(1, LANES)
    w_hi = w.astype(BF16)
    w_lo = (w - w_hi.astype(F32)).astype(BF16)
    return jnp.concatenate([w_hi, w_lo], axis=1), w_hi, b


def _sc_mesh():
    return plsc.VectorSubcoreMesh(core_axis_name="core", subcore_axis_name="subcore")


def _sc_worker(mesh):
    return lax.axis_index("core") * mesh.num_subcores + lax.axis_index("subcore")


def _dispatch(h2p, dest):
    mesh = _sc_mesh()
    n_win = N_TOK // SC_WINDOW
    per_worker = n_win // (mesh.num_cores * mesh.num_subcores)
    assert per_worker * mesh.num_cores * mesh.num_subcores == n_win

    @pl.kernel(out_type=jax.ShapeDtypeStruct((N_ROWS,) + ROW_TILE, U32), mesh=mesh,
               scratch_types=[pltpu.VMEM((SC_WINDOW,), I32), pltpu.VMEM((SC_WINDOW,) + ROW_TILE, U32)])
    def scatter_rows(h_hbm, dest_hbm, xs_hbm, idx_v, rows_v):
        first = _sc_worker(mesh) * per_worker

        @pl.loop(0, per_worker)
        def _(w):
            win = first + w
            pltpu.sync_copy(h_hbm.at[pl.ds(win * SC_WINDOW, SC_WINDOW)], rows_v)
            for j in range(TOPK):
                pltpu.sync_copy(dest_hbm.at[j, win], idx_v)
                pltpu.sync_copy(rows_v, xs_hbm.at[idx_v])

    return scatter_rows(h2p, dest)


def _collect(ys, dest):
    mesh = _sc_mesh()
    n_tok = dest.shape[1]
    n_win = TOPK * n_tok // SC_WINDOW
    per_worker = n_win // (mesh.num_cores * mesh.num_subcores)
    assert per_worker * mesh.num_cores * mesh.num_subcores == n_win

    @pl.kernel(out_type=jax.ShapeDtypeStruct((TOPK * n_tok,) + ROW_TILE, U32), mesh=mesh,
               scratch_types=[pltpu.VMEM((SC_WINDOW,), I32), pltpu.VMEM((SC_WINDOW,) + ROW_TILE, U32)])
    def gather_rows(ys_hbm, dest_hbm, yg_hbm, idx_v, rows_v):
        first = _sc_worker(mesh) * per_worker

        @pl.loop(0, per_worker)
        def _(w):
            win = first + w
            pltpu.sync_copy(dest_hbm.at[win], idx_v)
            pltpu.sync_copy(ys_hbm.at[idx_v], rows_v)
            pltpu.sync_copy(rows_v, yg_hbm.at[pl.ds(win * SC_WINDOW, SC_WINDOW)])

    return gather_rows(ys, dest.reshape(n_win, SC_WINDOW)).reshape((TOPK, n_tok) + ROW_TILE)


def _expert_kernel(first_ref, nblk_ref, nvalid_ref, nused_ref, xs_hbm, wg_ref, wu_ref, wd_ref, ys_hbm,
                   wg_scr, wu_scr, wd_scr, x_buf, y_buf, in_sem, out_sem):
    e = pl.program_id(0)
    nused = nused_ref[0]

    def in_copy(g):
        slot = lax.rem(g, IN_SLOTS)
        return pltpu.make_async_copy(xs_hbm.at[pl.ds(g * BM, BM)], x_buf.at[slot], in_sem.at[slot])

    def out_copy(g, slot):
        return pltpu.make_async_copy(y_buf.at[slot], ys_hbm.at[pl.ds(g * BM, BM)], out_sem.at[slot])

    @pl.when(e == 0)
    def _first():
        for g in range(IN_AHEAD):
            in_copy(g).start()

    wg_scr[...] = wg_ref[0].astype(BF16)
    wu_scr[...] = wu_ref[0].astype(BF16)
    wd_scr[...] = wd_ref[0].astype(BF16)

    def block(b, carry):
        g = first_ref[e] + b
        slot = lax.rem(g, 2)
        in_copy(g).wait()

        @pl.when(g + IN_AHEAD < nused)
        def _prefetch():
            in_copy(g + IN_AHEAD).start()

        @pl.when(g >= 2)
        def _slot_free():
            out_copy(g - 2, slot).wait()

        valid = lax.broadcasted_iota(I32, (BM, 1), 0) < nvalid_ref[g]
        x_blk = x_buf[lax.rem(g, IN_SLOTS)].reshape(BM, HALF)
        lo, hi = _unpack_bf16_pair(jnp.where(valid, x_blk, jnp.uint32(0)))
        lo = lo.astype(BF16)
        hi = hi.astype(BF16)
        gate = jnp.dot(lo, wg_scr[0:HALF, :], preferred_element_type=F32) \
            + jnp.dot(hi, wg_scr[HALF:, :], preferred_element_type=F32)
        up = jnp.dot(lo, wu_scr[0:HALF, :], preferred_element_type=F32) \
            + jnp.dot(hi, wu_scr[HALF:, :], preferred_element_type=F32)
        act = (jax.nn.silu(gate) * up).astype(BF16)
        o = jnp.dot(act, wd_scr[...], preferred_element_type=F32)
        y_buf[slot] = _pack_bf16_pair(o[:, 0:HALF], o[:, HALF:]).reshape((BM,) + ROW_TILE)
        out_copy(g, slot).start()
        return carry

    lax.fori_loop(0, nblk_ref[e], block, 0)

    @pl.when(e == N_EXPERTS - 1)
    def _drain():
        out_copy(nused - 2, lax.rem(nused, 2)).wait()
        out_copy(nused - 1, 1 - lax.rem(nused, 2)).wait()


def _experts(first, nblk, nvalid, nused, xs, wg, wu, wd):
    grid_spec = pltpu.PrefetchScalarGridSpec(
        num_scalar_prefetch=4,
        grid=(N_EXPERTS,),
        in_specs=[
            pl.BlockSpec(memory_space=pl.ANY),
            pl.BlockSpec((1, D_MODEL, D_EXPERT), lambda e, *_: (e, 0, 0)),
            pl.BlockSpec((1, D_MODEL, D_EXPERT), lambda e, *_: (e, 0, 0)),
            pl.BlockSpec((1, D_EXPERT, D_MODEL), lambda e, *_: (e, 0, 0)),
        ],
        out_specs=pl.BlockSpec(memory_space=pl.ANY),
        scratch_shapes=[
            pltpu.VMEM((D_MODEL, D_EXPERT), BF16),
            pltpu.VMEM((D_MODEL, D_EXPERT), BF16),
            pltpu.VMEM((D_EXPERT, D_MODEL), BF16),
            pltpu.VMEM((IN_SLOTS, BM) + ROW_TILE, U32),
            pltpu.VMEM((2, BM) + ROW_TILE, U32),
            pltpu.SemaphoreType.DMA((IN_SLOTS,)),
            pltpu.SemaphoreType.DMA((2,)),
        ],
    )
    return pl.pallas_call(
        _expert_kernel,
        grid_spec=grid_spec,
        out_shape=jax.ShapeDtypeStruct((N_ROWS,) + ROW_TILE, U32),
        compiler_params=pltpu.CompilerParams(
            dimension_semantics=("arbitrary",), vmem_limit_bytes=VMEM_LIMIT),
        name="experts",
    )(first, nblk, nvalid, nused, xs, wg, wu, wd)


def _combine_kernel(x1_ref, rec_ref, yg_ref, p_ref, gple_ref, wpg_ref, wple_ref, gfin_ref, *rest):
    out_ref = rest[-1]
    ple = jnp.dot(p_ref[0].reshape(TM, D_PLE).astype(BF16), wple_ref[...], preferred_element_type=F32)
    rec = rec_ref[...]
    w0 = rec[:, REC_W0:REC_W0 + 1]
    w1 = rec[:, REC_W1:REC_W1 + 1]
    lo0, hi0 = _unpack_bf16_pair(yg_ref[0].reshape(TM, HALF))
    lo1, hi1 = _unpack_bf16_pair(yg_ref[1].reshape(TM, HALF))
    moe = jnp.concatenate([lo0 * w0 + lo1 * w1, hi0 * w0 + hi1 * w1], axis=1)
    x2 = x1_ref[...].reshape(TM, D_MODEL) + moe
    gate = jax.nn.sigmoid(jnp.dot(_rms(x2, gple_ref[...]).astype(BF16), wpg_ref[...],
                                  preferred_element_type=F32))
    x3 = x2 + gate * ple
    out_ref[...] = _rms(x3, gfin_ref[...]).reshape(BATCH, TT, D_MODEL)


def _combine(s0, n_steps, x1, rec, yg, p, gple, wpg, wple, gfin, out_prev=None):
    seq_spec = pl.BlockSpec((BATCH, TT, D_MODEL), lambda i: (0, s0 + i, 0))
    in_specs = [
        seq_spec,
        pl.BlockSpec((TM, LANES), lambda i: (s0 + i, 0)),
        pl.BlockSpec((TOPK, TM) + ROW_TILE, lambda i: (0, i, 0, 0)),
        pl.BlockSpec((1, BATCH, TT, D_PLE), lambda i: (0, 0, s0 + i, 0)),
        _const_spec((1, D_MODEL)),
        _const_spec((D_MODEL, D_MODEL)),
        _const_spec((D_PLE, D_MODEL)),
        _const_spec((1, D_MODEL)),
    ]
    args = [x1, rec, yg, p, gple, wpg, wple, gfin]
    aliases = {}
    if out_prev is not None:
        in_specs.append(pl.BlockSpec(memory_space=pl.ANY))
        args.append(out_prev)
        aliases = {len(args) - 1: 0}
    return pl.pallas_call(
        _combine_kernel,
        grid=(n_steps,),
        in_specs=in_specs,
        out_specs=seq_spec,
        out_shape=jax.ShapeDtypeStruct((BATCH, SEQ, D_MODEL), F32),
        input_output_aliases=aliases,
        compiler_params=pltpu.CompilerParams(
            dimension_semantics=("arbitrary",), vmem_limit_bytes=VMEM_LIMIT),
        name="combine",
    )(*args)


def kernel(x, p, g_mix, w_in, b_gate, ssm_a_re, ssm_a_im, ssm_log_dt, ssm_b_re, ssm_b_im, ssm_c_re,
           ssm_c_im, ssm_d, w_glu, conv_dw, conv_dw_b, conv_ln_g, conv_ln_b, w_conv_out, w_out, g_moe,
           w_router_group, b_router_group, w_router_expert, b_router_expert, w_exp_gate, w_exp_up,
           w_exp_down, g_ple, w_ple_gate, w_ple, g_final):
    assert x.shape == (BATCH, SEQ, D_MODEL) and p.shape == (1, BATCH, SEQ, D_PLE)
    row = lambda v: v.reshape(1, -1)

    um, up, ur, a_re, a_im = _ssm_matrices(ssm_a_re[0], ssm_a_im[0], ssm_log_dt[0], ssm_b_re[0],
                                           ssm_b_im[0], ssm_c_re[0], ssm_c_im[0])
    wr1, wr2, br = _router_weights(w_router_group[0], b_router_group[0], w_router_expert[0],
                                   b_router_expert[0])
    x1, h2p, rec, rect, cnt = _mixer(
        x, row(g_mix[0]), w_in[0], row(b_gate[0]), um, up, ur, a_re, a_im,
        row(ssm_d[0]), w_glu[0], conv_dw[0], row(conv_dw_b[0]), row(conv_ln_g[0]),
        row(conv_ln_b[0]), w_conv_out[0], w_out[0], row(g_moe[0]), wr1, wr2, br)

    counts = cnt[0, LANE_EXP0:LANE_EXP0 + N_EXPERTS].astype(I32)
    pcounts = (counts + BM - 1) // BM * BM
    pends = jnp.cumsum(pcounts)
    pstarts = pends - pcounts
    eid = rect[REC_EID0:REC_EID1 + 1].astype(I32)
    rank = rect[REC_RANK0:REC_RANK1 + 1].astype(I32)
    dest = (jnp.sum(jnp.where(eid[..., None] == jnp.arange(N_EXPERTS, dtype=I32), pstarts, 0), axis=-1)
            + rank).reshape(TOPK, N_TOK // SC_WINDOW, SC_WINDOW)
    nused = (pends[-1] // BM).astype(I32)
    blk = jnp.arange(N_BLK, dtype=I32)[:, None] * BM
    in_expert = (pstarts[None, :] <= blk) & (blk < pends[None, :])
    nvalid = jnp.clip(jnp.sum(jnp.where(in_expert, (pstarts + counts)[None, :] - blk, 0), axis=1), 0, BM)

    xs = _dispatch(h2p, dest)
    ys = _experts(pstarts // BM, pcounts // BM, nvalid.astype(I32), nused.reshape(1), xs,
                  w_exp_gate[0], w_exp_up[0], w_exp_down[0])
    dest_tok = dest.reshape(TOPK, N_TOK)
    wpg = w_ple_gate[0].astype(BF16)
    wple = w_ple[0].astype(BF16)
    out = None
    s0 = 0
    for n_steps in PART_STEPS:
        yg = _collect(ys, dest_tok[:, s0 * TM:(s0 + n_steps) * TM])
        out = _combine(s0, n_steps, x1, rec, yg, p, row(g_ple[0]), wpg, wple, row(g_final), out)
        s0 += n_steps
    return out
```

```python
import jax
import jax.numpy as jnp
from jax import lax
from jax.experimental import pallas as pl
from jax.experimental.pallas import tpu as pltpu
from jax.experimental.pallas import tpu_sc as plsc

F32 = jnp.float32
BF16 = jnp.bfloat16
U32 = jnp.uint32
I32 = jnp.int32

D_MODEL = 1024
BATCH = 8
SEQ = 2048
N_TOK = BATCH * SEQ
D_SSM = 512
SSM_GROUP_WIDTH = 16
SSM_GROUPS = 32
SSM_STATE = 64
D_CONV = 512
CONV_WIDTH = 31
D_IN = D_SSM + 2 * D_CONV + 2 * D_MODEL
N_GROUPS_MOE = 4
EXPERTS_PER_GROUP = 8
N_EXPERTS = 32
TOPK = 2
D_EXPERT = 512
D_PLE = 256
EPS = 1e-6

SUBLANES = 8
LANES = 128
assert BATCH == SUBLANES

TT = 64
TM = TT * BATCH
N_STEP = SEQ // TT
SB = 512
NSB = TM // SB
BPS = SB // TT
Q = 2
N_SLAB = D_SSM // LANES
GROUPS_PER_SLAB = SSM_GROUPS // N_SLAB
ROWS_Z = TM // Q
STATE_LANES = 2 * GROUPS_PER_SLAB * SSM_STATE
HALO = (CONV_WIDTH - 1) * BATCH
CHUNK_ROWS = SB // (Q * SUBLANES)
W_STAGE_ROWS = 64
W_STAGE_SLOTS = 4
CONV_ROWS = 64
N_LC = D_CONV // LANES

LANE_GRP0 = 0
LANE_EXP0 = 32
REC_EID0, REC_EID1, REC_W0, REC_W1, REC_RANK0, REC_RANK1 = 0, 1, 2, 3, 4, 5
REC_ROWS = 8

BM = 256
N_BLK = (TOPK * N_TOK + N_EXPERTS * (BM - 1) + BM - 1) // BM
N_ROWS = N_BLK * BM
HALF = D_MODEL // 2
ROW_TILE = (HALF // LANES, LANES)
SC_WINDOW = 128
IN_AHEAD = 3
IN_SLOTS = IN_AHEAD + 1
RING_DMA_PRIORITY = 1
PART_STEPS = (12, 20)
assert sum(PART_STEPS) == N_STEP

VMEM_LIMIT = 56 * 1024 * 1024


def _const_spec(shape):
    n = len(shape)
    return pl.BlockSpec(shape, lambda *_: (0,) * n, pipeline_mode=pl.Buffered(1))


def _rms(x, g):
    ms = jnp.mean(x * x, axis=-1, keepdims=True)
    return x * lax.rsqrt(ms + EPS) * g


def _pack_bf16_pair(lo, hi):
    ulo = lax.bitcast_convert_type(lo.astype(BF16).astype(F32), U32)
    uhi = lax.bitcast_convert_type(hi.astype(BF16).astype(F32), U32)
    return (ulo >> 16) | (uhi & jnp.uint32(0xFFFF0000))


def _unpack_bf16_pair(w):
    lo = lax.bitcast_convert_type(w << 16, F32)
    hi = lax.bitcast_convert_type(w & jnp.uint32(0xFFFF0000), F32)
    return lo, hi


def _load_weights_bf16(pairs, stage, sem):
    chunks = [(src, dst, r0) for src, dst in pairs for r0 in range(0, src.shape[0], W_STAGE_ROWS)]

    def copy(c):
        src, _, r0 = chunks[c]
        slot = c % W_STAGE_SLOTS
        return pltpu.make_async_copy(src.at[pl.ds(r0, W_STAGE_ROWS)],
                                     stage.at[slot, :, 0:src.shape[1]], sem.at[slot])

    for c in range(W_STAGE_SLOTS):
        copy(c).start()
    for c, (src, dst, r0) in enumerate(chunks):
        copy(c).wait()
        dst[r0:r0 + W_STAGE_ROWS, :] = stage[c % W_STAGE_SLOTS, :, 0:src.shape[1]].astype(BF16)
        if c + W_STAGE_SLOTS < len(chunks):
            copy(c + W_STAGE_SLOTS).start()


def _expand_ssm(um_ref, up_ref, ur_ref, mp_ref, r_ref):
    gw, ns, half = SSM_GROUP_WIDTH, SSM_STATE, STATE_LANES // 2
    div = lambda a, n: lax.shift_right_logical(a, n.bit_length() - 1)
    mod = lambda a, n: a & (n - 1)
    iota2 = lambda shape: (lax.broadcasted_iota(I32, shape, 0), lax.broadcasted_iota(I32, shape, 1))
    one = lambda cond: jnp.where(cond, 1.0, 0.0).astype(BF16)

    r, q = iota2((Q * gw, Q * LANES))
    x_m = one((div(r, gw) == div(q, LANES)) & (mod(r, gw) == mod(q, gw)))
    r, q = iota2((2 * ns, STATE_LANES))
    x_p = one((div(r, ns) == div(q, half)) & (mod(r, ns) == mod(q, ns)))
    p, r = iota2((STATE_LANES, 2 * ns))
    x_r = one((div(p, half) == div(r, ns)) & (mod(p, ns) == mod(r, ns)))
    p, q = iota2((Q * LANES, Q * LANES))
    same_m = div(mod(p, LANES), gw) == div(mod(q, LANES), gw)
    p, q = iota2((Q * LANES, STATE_LANES))
    same_p = div(mod(p, LANES), gw) == div(mod(q, half), ns)
    p, q = iota2((STATE_LANES, Q * LANES))
    same_r = div(mod(p, half), ns) == div(mod(q, LANES), gw)
    for s in range(N_SLAB):
        m = jnp.dot(um_ref[s], x_m, preferred_element_type=F32)
        mp_ref[s, :, 0:Q * LANES] = jnp.where(same_m, m, 0.0).astype(BF16)
        pm = jnp.dot(up_ref[s], x_p, preferred_element_type=F32)
        mp_ref[s, :, Q * LANES:] = jnp.where(same_p, pm, 0.0).astype(BF16)
        rm = jnp.dot(x_r, ur_ref[s], preferred_element_type=F32)
        r_ref[s] = jnp.where(same_r, rm, 0.0).astype(BF16)


def _mixer_kernel(x_ref, gmix_ref, win_hbm, bgate_ref, um_ref, up_ref, ur_ref, are_ref,
                  aim_ref, d_ref, wglu_hbm, dw_ref, dwb_ref, lng_ref, lnb_ref, wco_hbm, wout_hbm,
                  gmoe_ref, wr1_ref, wr2_ref, br_ref,
                  x1_ref, h2p_ref, rec_ref, rect_ref, cnt_ref,
                  hb_scr, ht_scr, u_scr, y_scr, yi_scr, xs_scr, z_scr, conv_scr, act_scr, actb_scr,
                  logit_scr, s_scr, cnt_scr, win_ref, wglu_ref, wco_ref, wout_ref, wstage_scr, wstage_sem,
                  mp_ref, r_ref):
    step = pl.program_id(0)
    assert NSB == 1

    @pl.when(step == 0)
    def _init():
        logit_scr[...] = jnp.zeros(logit_scr.shape, F32)
        z_scr[:, 0:HALO, :] = jnp.zeros((N_LC, HALO, LANES), F32)
        s_scr[...] = jnp.zeros(s_scr.shape, F32)
        cnt_scr[...] = jnp.zeros(cnt_scr.shape, F32)
        _expand_ssm(um_ref, up_ref, ur_ref, mp_ref, r_ref)
        _load_weights_bf16([(win_hbm, win_ref), (wglu_hbm, wglu_ref), (wco_hbm, wco_ref),
                            (wout_hbm, wout_ref)], wstage_scr, wstage_sem)

    def sub_rows(r):
        return pl.ds(pl.multiple_of(r * SB, SB), SB)

    def phase_a(r, carry):
        xb = x_ref[pl.ds(r * BPS, BPS)].reshape(SB, D_MODEL)
        hb_scr[sub_rows(r), :] = _rms(xb, gmix_ref[...]).astype(BF16)
        return carry

    def phase_a3(r, carry):
        h = ht_scr[sub_rows(r), :]
        u = jnp.dot(h, win_ref[:, 0:D_SSM], preferred_element_type=F32)
        u_scr[pl.ds(r * CHUNK_ROWS, CHUNK_ROWS)] = u.reshape(CHUNK_ROWS, Q, SUBLANES, D_SSM)
        v = jnp.dot(h, win_ref[:, D_SSM:D_SSM + 2 * D_CONV], preferred_element_type=F32)
        zc = v[:, 0:D_CONV] * jax.nn.sigmoid(v[:, D_CONV:])
        for lc in range(N_LC):
            z_scr[lc, pl.ds(pl.multiple_of(HALO + r * SB, SUBLANES), SB), :] = zc[:, lc * LANES:(lc + 1) * LANES]
        return carry

    def phase_b():
        for s in range(N_SLAB):
            lanes = slice(s * LANES, (s + 1) * LANES)
            z = jnp.concatenate(
                [u_scr[:, i, :, lanes].reshape(ROWS_Z, LANES) for i in range(Q)], axis=1).astype(BF16)
            xp = jnp.dot(z, mp_ref[s], preferred_element_type=F32)
            yi_scr[s] = xp[:, 0:Q * LANES]
            xs_scr[s] = xp[:, Q * LANES:]

        half = STATE_LANES // 2
        for s in range(N_SLAB):
            a_re = jnp.broadcast_to(are_ref[s:s + 1, :], (SUBLANES, half))
            a_im = jnp.broadcast_to(aim_ref[s:s + 1, :], (SUBLANES, half))

            def scan_body(k, carry, s=s, a_re=a_re, a_im=a_im):
                s_re, s_im = carry
                rows = pl.ds(pl.multiple_of(k * SUBLANES, SUBLANES), SUBLANES)
                x_re = xs_scr[s, rows, 0:half]
                x_im = xs_scr[s, rows, half:]
                xs_scr[s, rows, 0:half] = s_re
                xs_scr[s, rows, half:] = s_im
                n_re = a_re * s_re - a_im * s_im + x_re
                n_im = a_re * s_im + a_im * s_re + x_im
                return n_re, n_im

            s_re, s_im = lax.fori_loop(0, ROWS_Z // SUBLANES, scan_body,
                                       (s_scr[s, :, 0:half], s_scr[s, :, half:]), unroll=True)
            s_scr[s, :, 0:half] = s_re
            s_scr[s, :, half:] = s_im

        for s in range(N_SLAB):
            lanes = slice(s * LANES, (s + 1) * LANES)
            y_tot = yi_scr[s] + jnp.dot(xs_scr[s].astype(BF16), r_ref[s], preferred_element_type=F32)
            for j in range(Q):
                y_scr[:, j, :, lanes] = y_tot[:, j * LANES:(j + 1) * LANES].reshape(
                    ROWS_Z // SUBLANES, SUBLANES, LANES)

    def phase_c1(r, carry):
        rows = sub_rows(r)
        crow = pl.ds(r * CHUNK_ROWS, CHUNK_ROWS)
        y = y_scr[crow].reshape(SB, D_SSM) + d_ref[...] * u_scr[crow].reshape(SB, D_SSM)
        act_scr[rows, 0:D_SSM] = jax.nn.gelu(y).astype(BF16)
        for lc in range(N_LC):
            lanes = slice(lc * LANES, (lc + 1) * LANES)

            def conv_piece(rc, c, lc=lc, lanes=lanes):
                r0 = r * SB + rc * CONV_ROWS
                piece = jnp.broadcast_to(dwb_ref[:, lanes], (CONV_ROWS, LANES))
                for j in range(CONV_WIDTH):
                    zrows = pl.ds(pl.multiple_of(r0 + j * BATCH, SUBLANES), CONV_ROWS)
                    piece = piece + dw_ref[j:j + 1, lanes] * z_scr[lc, zrows, :]
                conv_scr[pl.ds(pl.multiple_of(rc * CONV_ROWS, CONV_ROWS), CONV_ROWS), lanes] = piece
                return c

            lax.fori_loop(0, SB // CONV_ROWS, conv_piece, 0, unroll=4)
        acc = conv_scr[...]
        mu = jnp.mean(acc, axis=-1, keepdims=True)
        cen = acc - mu
        var = jnp.mean(cen * cen, axis=-1, keepdims=True)
        ln = cen * lax.rsqrt(var + EPS) * lng_ref[...] + lnb_ref[...]
        act_scr[rows, D_SSM:] = jax.nn.silu(ln).astype(BF16)
        return carry

    lane = lax.broadcasted_iota(I32, (1, LANES), 1).astype(F32)
    grp_mask = lane < float(N_GROUPS_MOE)
    exp_lane = (lane >= float(LANE_EXP0)) & (lane < float(LANE_EXP0 + N_EXPERTS))
    lane_grp = jnp.floor((lane - float(LANE_EXP0)) * (1.0 / EXPERTS_PER_GROUP))
    tri = (lax.broadcasted_iota(I32, (SB, SB), 0) > lax.broadcasted_iota(I32, (SB, SB), 1)).astype(BF16)
    neg_inf = float("-inf")
    big = float(4 * LANES)

    def phase_c3(r, carry):
        rows = sub_rows(r)
        h = hb_scr[rows, :]
        g0 = D_SSM + 2 * D_CONV
        gate_ssm = jnp.dot(h, win_ref[:, g0:g0 + D_MODEL], preferred_element_type=F32) \
            + bgate_ref[:, 0:D_MODEL]
        gate_conv = jnp.dot(h, win_ref[:, g0 + D_MODEL:], preferred_element_type=F32) \
            + bgate_ref[:, D_MODEL:]
        zz = jnp.dot(actb_scr[rows, 0:D_SSM], wglu_ref[...], preferred_element_type=F32)
        y_ssm = zz[:, 0:D_MODEL] * jax.nn.sigmoid(zz[:, D_MODEL:])
        y_conv = jnp.dot(actb_scr[rows, D_SSM:], wco_ref[...], preferred_element_type=F32)

        merged = jax.nn.sigmoid(gate_ssm) * y_ssm + jax.nn.sigmoid(gate_conv) * y_conv
        xb = x_ref[pl.ds(r * BPS, BPS)].reshape(SB, D_MODEL)
        x1 = xb + jnp.dot(merged.astype(BF16), wout_ref[...], preferred_element_type=F32)
        x1_ref[pl.ds(r * BPS, BPS)] = x1.reshape(BPS, TT, D_MODEL)

        h2 = _rms(x1, gmoe_ref[...])
        h2p_ref[rows] = _pack_bf16_pair(h2[:, 0:HALF], h2[:, HALF:]).reshape((SB,) + ROW_TILE)

        h2_hi = h2.astype(BF16)
        h2_lo = (h2 - h2_hi.astype(F32)).astype(BF16)
        l1 = jnp.dot(h2_hi, wr1_ref[...], preferred_element_type=F32)
        l2 = jnp.dot(h2_lo, wr2_ref[...], preferred_element_type=F32)
        logit_scr[rows, :] = l1[:, 0:LANES] + l1[:, LANES:] + l2 + br_ref[...]
        return carry

    def route_previous():
        rows = sub_rows(0)
        logits = logit_scr[...]
        counted = jnp.where(step > 0, 1.0, 0.0)

        lg = jnp.where(grp_mask, logits, neg_inf)
        g_max = jnp.max(lg, axis=-1, keepdims=True)
        g_sel = jnp.min(jnp.where(lg == g_max, lane, big), axis=-1, keepdims=True)
        p_g = 1.0 / jnp.sum(jnp.where(grp_mask, jnp.exp(logits - g_max), 0.0), axis=-1, keepdims=True)
        le = jnp.where(exp_lane & (lane_grp == g_sel), logits, neg_inf)
        m1 = jnp.max(le, axis=-1, keepdims=True)
        i1 = jnp.min(jnp.where(le == m1, lane, big), axis=-1, keepdims=True)
        le2 = jnp.where(lane == i1, neg_inf, le)
        m2 = jnp.max(le2, axis=-1, keepdims=True)
        i2 = jnp.min(jnp.where(le2 == m2, lane, big), axis=-1, keepdims=True)
        e2 = jnp.exp(m2 - m1)
        den = 1.0 + e2
        w_a = (1.0 / den) * p_g
        w_b = (e2 / den) * p_g

        sel1 = lane == i1
        sel2 = lane == i2
        onehot = jnp.where(sel1 | sel2, counted, 0.0)
        prefix = jnp.dot(tri, onehot.astype(BF16), preferred_element_type=F32) + cnt_scr[...]
        rank_a = jnp.sum(jnp.where(sel1, prefix, 0.0), axis=-1, keepdims=True)
        rank_b = jnp.sum(jnp.where(sel2, prefix, 0.0), axis=-1, keepdims=True)
        cnt_scr[...] = cnt_scr[...] + jnp.sum(onehot, axis=0, keepdims=True)

        rec = jnp.where(lane == float(REC_EID0), i1 - float(LANE_EXP0), 0.0)
        rec = jnp.where(lane == float(REC_EID1), i2 - float(LANE_EXP0), rec)
        rec = jnp.where(lane == float(REC_W0), w_a, rec)
        rec = jnp.where(lane == float(REC_W1), w_b, rec)
        rec = jnp.where(lane == float(REC_RANK0), rank_a, rec)
        rec = jnp.where(lane == float(REC_RANK1), rank_b, rec)
        rec_ref[rows, :] = rec
        rect_ref[...] = jnp.transpose(rec)[0:REC_ROWS, :]
        cnt_ref[...] = cnt_scr[...]

    @pl.when(step < N_STEP)
    def _tile():
        route_previous()
        phase_a(0, 0)
        ht_scr[...] = jnp.swapaxes(hb_scr[...].reshape(BATCH, TT, D_MODEL), 0, 1).reshape(TM, D_MODEL)
        phase_a3(0, 0)
        phase_b()
        phase_c1(0, 0)
        z_scr[:, 0:HALO, :] = z_scr[:, TM:TM + HALO, :]
        actb_scr[...] = jnp.swapaxes(act_scr[...].reshape(TT, BATCH, D_SSM + D_CONV), 0, 1).reshape(
            TM, D_SSM + D_CONV)
        phase_c3(0, 0)

    @pl.when(step == N_STEP)
    def _last():
        route_previous()


def _mixer(x, gmix, win, bgate, um, up, ur, a_re, a_im, dvec, wglu, dw, dwb, lng, lnb, wco,
           wout, gmoe, wr1, wr2, br):
    tile = lambda i: jnp.minimum(i, N_STEP - 1)
    routed = lambda i: jnp.maximum(i - 1, 0)
    seq_spec = pl.BlockSpec((BATCH, TT, D_MODEL), lambda i: (0, tile(i), 0))
    in_hbm = pl.BlockSpec(memory_space=pl.ANY)
    in_specs = [
        seq_spec,
        _const_spec((1, D_MODEL)),
        in_hbm,
        _const_spec((1, 2 * D_MODEL)),
        _const_spec(um.shape),
        _const_spec(up.shape),
        _const_spec(ur.shape),
        _const_spec(a_re.shape),
        _const_spec(a_im.shape),
        _const_spec((1, D_SSM)),
        in_hbm,
        _const_spec((CONV_WIDTH, D_CONV)),
        _const_spec((1, D_CONV)),
        _const_spec((1, D_CONV)),
        _const_spec((1, D_CONV)),
        in_hbm,
        in_hbm,
        _const_spec((1, D_MODEL)),
        _const_spec((D_MODEL, 2 * LANES)),
        _const_spec((D_MODEL, LANES)),
        _const_spec((1, LANES)),
    ]
    out_specs = [
        seq_spec,
        pl.BlockSpec((TM,) + ROW_TILE, lambda i: (tile(i), 0, 0)),
        pl.BlockSpec((TM, LANES), lambda i: (routed(i), 0)),
        pl.BlockSpec((REC_ROWS, TM), lambda i: (0, routed(i))),
        pl.BlockSpec((1, LANES), lambda i: (0, 0)),
    ]
    out_shape = [
        jax.ShapeDtypeStruct((BATCH, SEQ, D_MODEL), F32),
        jax.ShapeDtypeStruct((N_TOK,) + ROW_TILE, U32),
        jax.ShapeDtypeStruct((N_TOK, LANES), F32),
        jax.ShapeDtypeStruct((REC_ROWS, N_TOK), F32),
        jax.ShapeDtypeStruct((1, LANES), F32),
    ]
    chunk_shape = (ROWS_Z // SUBLANES, Q, SUBLANES, D_SSM)
    scratch = [
        pltpu.VMEM((TM, D_MODEL), BF16),
        pltpu.VMEM((TM, D_MODEL), BF16),
        pltpu.VMEM(chunk_shape, F32),
        pltpu.VMEM(chunk_shape, F32),
        pltpu.VMEM((N_SLAB, ROWS_Z, Q * LANES), F32),
        pltpu.VMEM((N_SLAB, ROWS_Z, STATE_LANES), F32),
        pltpu.VMEM((N_LC, HALO + TM, LANES), F32),
        pltpu.VMEM((SB, D_CONV), F32),
        pltpu.VMEM((TM, D_SSM + D_CONV), BF16),
        pltpu.VMEM((TM, D_SSM + D_CONV), BF16),
        pltpu.VMEM((TM, LANES), F32),
        pltpu.VMEM((N_SLAB, SUBLANES, STATE_LANES), F32),
        pltpu.VMEM((1, LANES), F32),
        pltpu.VMEM(win.shape, BF16),
        pltpu.VMEM(wglu.shape, BF16),
        pltpu.VMEM(wco.shape, BF16),
        pltpu.VMEM(wout.shape, BF16),
        pltpu.VMEM((W_STAGE_SLOTS, W_STAGE_ROWS, D_IN), F32),
        pltpu.SemaphoreType.DMA((W_STAGE_SLOTS,)),
        pltpu.VMEM((N_SLAB, Q * LANES, Q * LANES + STATE_LANES), BF16),
        pltpu.VMEM((N_SLAB, STATE_LANES, Q * LANES), BF16),
    ]
    assert win.shape == (D_MODEL, D_IN) and wglu.shape == (D_SSM, 2 * D_MODEL)
    assert wco.shape == (D_CONV, D_MODEL) and wout.shape == (D_MODEL, D_MODEL)
    return pl.pallas_call(
        _mixer_kernel,
        grid=(N_STEP + 1,),
        in_specs=in_specs,
        out_specs=out_specs,
        out_shape=out_shape,
        scratch_shapes=scratch,
        compiler_params=pltpu.CompilerParams(
            dimension_semantics=("arbitrary",), vmem_limit_bytes=VMEM_LIMIT),
        name="mixer",
    )(x, gmix, win, bgate, um, up, ur, a_re, a_im, dvec, wglu, dw, dwb, lng, lnb, wco, wout,
      gmoe, wr1, wr2, br)


def _cmul(a, b):
    return a[0] * b[0] - a[1] * b[1], a[0] * b[1] + a[1] * b[0]


def _ssm_matrices(a_re, a_im, log_dt, b_re, b_im, c_re, c_im):
    dt = jnp.exp(log_dt)[:, None]
    mag = jnp.exp(a_re * dt)
    lam = (mag * jnp.cos(a_im * dt), mag * jnp.sin(a_im * dt))
    den = a_re * a_re + a_im * a_im
    nr = lam[0] - 1.0
    ni = lam[1]
    z_re = (nr * a_re + ni * a_im) / den
    z_im = (ni * a_re - nr * a_im) / den
    bbar = (z_re[..., None] * b_re - z_im[..., None] * b_im,
            z_re[..., None] * b_im + z_im[..., None] * b_re)
    pw = [(jnp.ones_like(lam[0]), jnp.zeros_like(lam[0])), lam]
    for _ in range(2, Q + 1):
        pw.append(_cmul(pw[-1], lam))
    e = [(c_re * p[0][:, None, :] - c_im * p[1][:, None, :],
          c_re * p[1][:, None, :] + c_im * p[0][:, None, :]) for p in pw]
    e_cat = jnp.concatenate([jnp.concatenate([e[m][0], -e[m][1]], axis=-1) for m in range(Q)], axis=1)
    k_cat = jnp.einsum('gcn,gnd->gcd', e_cat, jnp.concatenate(bbar, axis=1), precision=lax.Precision.HIGHEST)
    k = [k_cat[:, m * SSM_GROUP_WIDTH:(m + 1) * SSM_GROUP_WIDTH, :] for m in range(Q)]
    split = lambda t: t.reshape((N_SLAB, GROUPS_PER_SLAB) + t.shape[1:])
    zero_k = jnp.zeros_like(k[0])
    kb = jnp.stack([jnp.stack([split(jnp.swapaxes(k[j - i] if j >= i else zero_k, 1, 2))
                               for j in range(Q)]) for i in range(Q)])
    um = jnp.transpose(kb, (2, 0, 3, 4, 1, 5)).reshape(N_SLAB, Q * LANES, Q * SSM_GROUP_WIDTH)
    f = [_cmul((pw[Q - 1 - i][0][..., None], pw[Q - 1 - i][1][..., None]), bbar) for i in range(Q)]
    fs = jnp.stack([jnp.stack([split(f[i][part]) for i in range(Q)]) for part in range(2)])
    up = jnp.transpose(fs, (2, 1, 3, 5, 0, 4)).reshape(N_SLAB, Q * LANES, 2 * SSM_STATE)
    es = jnp.stack([sign * jnp.stack([split(e[j + 1][part]) for j in range(Q)])
                    for part, sign in ((0, 1.0), (1, -1.0))])
    ur = jnp.transpose(es, (2, 0, 5, 1, 3, 4)).reshape(N_SLAB, 2 * SSM_STATE, Q * LANES)
    a_q = pw[Q]
    return (um.astype(BF16), up.astype(BF16), ur.astype(BF16),
            a_q[0].reshape(N_SLAB, STATE_LANES // 2), a_q[1].reshape(N_SLAB, STATE_LANES // 2))


def _router_weights(w_rg, b_rg, w_re, b_re):
    pad_g = LANE_EXP0 - LANE_GRP0 - N_GROUPS_MOE
    pad_e = LANES - LANE_EXP0 - N_EXPERTS
    w = jnp.concatenate([w_rg, jnp.zeros((D_MODEL, pad_g), F32), w_re, jnp.zeros((D_MODEL, pad_e), F32)], axis=1)
    b = jnp.concatenate([b_rg, jnp.zeros((pad_g,), F32), b_re, jnp.zeros((pad_e,), F32)]).reshape(1, LANES)
    w_hi = w.astype(BF16)
    w_lo = (w - w_hi.astype(F32)).astype(BF16)
    return jnp.concatenate([w_hi, w_lo], axis=1), w_hi, b


def _sc_mesh():
    return plsc.VectorSubcoreMesh(core_axis_name="core", subcore_axis_name="subcore")


def _sc_worker(mesh):
    return lax.axis_index("core") * mesh.num_subcores + lax.axis_index("subcore")


def _dispatch(h2p, dest):
    mesh = _sc_mesh()
    n_win = N_TOK // SC_WINDOW
    per_worker = n_win // (mesh.num_cores * mesh.num_subcores)
    assert per_worker * mesh.num_cores * mesh.num_subcores == n_win

    @pl.kernel(out_type=jax.ShapeDtypeStruct((N_ROWS,) + ROW_TILE, U32), mesh=mesh,
               scratch_types=[pltpu.VMEM((SC_WINDOW,), I32), pltpu.VMEM((SC_WINDOW,) + ROW_TILE, U32)])
    def scatter_rows(h_hbm, dest_hbm, xs_hbm, idx_v, rows_v):
        first = _sc_worker(mesh) * per_worker

        @pl.loop(0, per_worker)
        def _(w):
            win = first + w
            pltpu.sync_copy(h_hbm.at[pl.ds(win * SC_WINDOW, SC_WINDOW)], rows_v)
            for j in range(TOPK):
                pltpu.sync_copy(dest_hbm.at[j, win], idx_v)
                pltpu.sync_copy(rows_v, xs_hbm.at[idx_v])

    return scatter_rows(h2p, dest)


def _collect(ys, dest):
    mesh = _sc_mesh()
    n_tok = dest.shape[1]
    n_win = TOPK * n_tok // SC_WINDOW
    per_worker = n_win // (mesh.num_cores * mesh.num_subcores)
    assert per_worker * mesh.num_cores * mesh.num_subcores == n_win

    @pl.kernel(out_type=jax.ShapeDtypeStruct((TOPK * n_tok,) + ROW_TILE, U32), mesh=mesh,
               scratch_types=[pltpu.VMEM((SC_WINDOW,), I32), pltpu.VMEM((SC_WINDOW,) + ROW_TILE, U32)])
    def gather_rows(ys_hbm, dest_hbm, yg_hbm, idx_v, rows_v):
        first = _sc_worker(mesh) * per_worker

        @pl.loop(0, per_worker)
        def _(w):
            win = first + w
            pltpu.sync_copy(dest_hbm.at[win], idx_v)
            pltpu.sync_copy(ys_hbm.at[idx_v], rows_v)
            pltpu.sync_copy(rows_v, yg_hbm.at[pl.ds(win * SC_WINDOW, SC_WINDOW)])

    return gather_rows(ys, dest.reshape(n_win, SC_WINDOW)).reshape((TOPK, n_tok) + ROW_TILE)


def _expert_kernel(first_ref, nblk_ref, nvalid_ref, nused_ref, xs_hbm, wg_ref, wu_ref, wd_ref, ys_hbm,
                   wg_scr, wu_scr, wd_scr, x_buf, y_buf, in_sem, out_sem):
    e = pl.program_id(0)
    nused = nused_ref[0]

    def in_copy(g):
        slot = lax.rem(g, IN_SLOTS)
        return pltpu.make_async_copy(xs_hbm.at[pl.ds(g * BM, BM)], x_buf.at[slot], in_sem.at[slot])

    def out_copy(g, slot):
        return pltpu.make_async_copy(y_buf.at[slot], ys_hbm.at[pl.ds(g * BM, BM)], out_sem.at[slot])

    @pl.when(e == 0)
    def _first():
        for g in range(IN_AHEAD):
            in_copy(g).start(priority=RING_DMA_PRIORITY)

    wg_scr[...] = wg_ref[0].astype(BF16)
    wu_scr[...] = wu_ref[0].astype(BF16)
    wd_scr[...] = wd_ref[0].astype(BF16)

    def block(b, carry):
        g = first_ref[e] + b
        slot = lax.rem(g, 2)
        in_copy(g).wait()

        @pl.when(g + IN_AHEAD < nused)
        def _prefetch():
            in_copy(g + IN_AHEAD).start(priority=RING_DMA_PRIORITY)

        @pl.when(g >= 2)
        def _slot_free():
            out_copy(g - 2, slot).wait()

        valid = lax.broadcasted_iota(I32, (BM, 1), 0) < nvalid_ref[g]
        x_blk = x_buf[lax.rem(g, IN_SLOTS)].reshape(BM, HALF)
        lo, hi = _unpack_bf16_pair(jnp.where(valid, x_blk, jnp.uint32(0)))
        lo = lo.astype(BF16)
        hi = hi.astype(BF16)
        gate = jnp.dot(lo, wg_scr[0:HALF, :], preferred_element_type=F32) \
            + jnp.dot(hi, wg_scr[HALF:, :], preferred_element_type=F32)
        up = jnp.dot(lo, wu_scr[0:HALF, :], preferred_element_type=F32) \
            + jnp.dot(hi, wu_scr[HALF:, :], preferred_element_type=F32)
        act = (jax.nn.silu(gate) * up).astype(BF16)
        o = jnp.dot(act, wd_scr[...], preferred_element_type=F32)
        y_buf[slot] = _pack_bf16_pair(o[:, 0:HALF], o[:, HALF:]).reshape((BM,) + ROW_TILE)
        out_copy(g, slot).start(priority=RING_DMA_PRIORITY)
        return carry

    lax.fori_loop(0, nblk_ref[e], block, 0)

    @pl.when(e == N_EXPERTS - 1)
    def _drain():
        out_copy(nused - 2, lax.rem(nused, 2)).wait()
        out_copy(nused - 1, 1 - lax.rem(nused, 2)).wait()


def _experts(first, nblk, nvalid, nused, xs, wg, wu, wd):
    grid_spec = pltpu.PrefetchScalarGridSpec(
        num_scalar_prefetch=4,
        grid=(N_EXPERTS,),
        in_specs=[
            pl.BlockSpec(memory_space=pl.ANY),
            pl.BlockSpec((1, D_MODEL, D_EXPERT), lambda e, *_: (e, 0, 0)),
            pl.BlockSpec((1, D_MODEL, D_EXPERT), lambda e, *_: (e, 0, 0)),
            pl.BlockSpec((1, D_EXPERT, D_MODEL), lambda e, *_: (e, 0, 0)),
        ],
        out_specs=pl.BlockSpec(memory_space=pl.ANY),
        scratch_shapes=[
            pltpu.VMEM((D_MODEL, D_EXPERT), BF16),
            pltpu.VMEM((D_MODEL, D_EXPERT), BF16),
            pltpu.VMEM((D_EXPERT, D_MODEL), BF16),
            pltpu.VMEM((IN_SLOTS, BM) + ROW_TILE, U32),
            pltpu.VMEM((2, BM) + ROW_TILE, U32),
            pltpu.SemaphoreType.DMA((IN_SLOTS,)),
            pltpu.SemaphoreType.DMA((2,)),
        ],
    )
    return pl.pallas_call(
        _expert_kernel,
        grid_spec=grid_spec,
        out_shape=jax.ShapeDtypeStruct((N_ROWS,) + ROW_TILE, U32),
        compiler_params=pltpu.CompilerParams(
            dimension_semantics=("arbitrary",), vmem_limit_bytes=VMEM_LIMIT),
        name="experts",
    )(first, nblk, nvalid, nused, xs, wg, wu, wd)


def _combine_kernel(x1_ref, rec_ref, yg_ref, p_ref, gple_ref, wpg_ref, wple_ref, gfin_ref, *rest):
    out_ref = rest[-1]
    ple = jnp.dot(p_ref[0].reshape(TM, D_PLE).astype(BF16), wple_ref[...], preferred_element_type=F32)
    rec = rec_ref[...]
    w0 = rec[:, REC_W0:REC_W0 + 1]
    w1 = rec[:, REC_W1:REC_W1 + 1]
    lo0, hi0 = _unpack_bf16_pair(yg_ref[0].reshape(TM, HALF))
    lo1, hi1 = _unpack_bf16_pair(yg_ref[1].reshape(TM, HALF))
    moe = jnp.concatenate([lo0 * w0 + lo1 * w1, hi0 * w0 + hi1 * w1], axis=1)
    x2 = x1_ref[...].reshape(TM, D_MODEL) + moe
    gate = jax.nn.sigmoid(jnp.dot(_rms(x2, gple_ref[...]).astype(BF16), wpg_ref[...],
                                  preferred_element_type=F32))
    x3 = x2 + gate * ple
    out_ref[...] = _rms(x3, gfin_ref[...]).reshape(BATCH, TT, D_MODEL)


def _combine(s0, n_steps, x1, rec, yg, p, gple, wpg, wple, gfin, out_prev=None):
    seq_spec = pl.BlockSpec((BATCH, TT, D_MODEL), lambda i: (0, s0 + i, 0))
    in_specs = [
        seq_spec,
        pl.BlockSpec((TM, LANES), lambda i: (s0 + i, 0)),
        pl.BlockSpec((TOPK, TM) + ROW_TILE, lambda i: (0, i, 0, 0)),
        pl.BlockSpec((1, BATCH, TT, D_PLE), lambda i: (0, 0, s0 + i, 0)),
        _const_spec((1, D_MODEL)),
        _const_spec((D_MODEL, D_MODEL)),
        _const_spec((D_PLE, D_MODEL)),
        _const_spec((1, D_MODEL)),
    ]
    args = [x1, rec, yg, p, gple, wpg, wple, gfin]
    aliases = {}
    if out_prev is not None:
        in_specs.append(pl.BlockSpec(memory_space=pl.ANY))
        args.append(out_prev)
        aliases = {len(args) - 1: 0}
    return pl.pallas_call(
        _combine_kernel,
        grid=(n_steps,),
        in_specs=in_specs,
        out_specs=seq_spec,
        out_shape=jax.ShapeDtypeStruct((BATCH, SEQ, D_MODEL), F32),
        input_output_aliases=aliases,
        compiler_params=pltpu.CompilerParams(
            dimension_semantics=("arbitrary",), vmem_limit_bytes=VMEM_LIMIT),
        name="combine",
    )(*args)


def kernel(x, p, g_mix, w_in, b_gate, ssm_a_re, ssm_a_im, ssm_log_dt, ssm_b_re, ssm_b_im, ssm_c_re,
           ssm_c_im, ssm_d, w_glu, conv_dw, conv_dw_b, conv_ln_g, conv_ln_b, w_conv_out, w_out, g_moe,
           w_router_group, b_router_group, w_router_expert, b_router_expert, w_exp_gate, w_exp_up,
           w_exp_down, g_ple, w_ple_gate, w_ple, g_final):
    assert x.shape == (BATCH, SEQ, D_MODEL) and p.shape == (1, BATCH, SEQ, D_PLE)
    row = lambda v: v.reshape(1, -1)

    um, up, ur, a_re, a_im = _ssm_matrices(ssm_a_re[0], ssm_a_im[0], ssm_log_dt[0], ssm_b_re[0],
                                           ssm_b_im[0], ssm_c_re[0], ssm_c_im[0])
    wr1, wr2, br = _router_weights(w_router_group[0], b_router_group[0], w_router_expert[0],
                                   b_router_expert[0])
    x1, h2p, rec, rect, cnt = _mixer(
        x, row(g_mix[0]), w_in[0], row(b_gate[0]), um, up, ur, a_re, a_im,
        row(ssm_d[0]), w_glu[0], conv_dw[0], row(conv_dw_b[0]), row(conv_ln_g[0]),
        row(conv_ln_b[0]), w_conv_out[0], w_out[0], row(g_moe[0]), wr1, wr2, br)

    counts = cnt[0, LANE_EXP0:LANE_EXP0 + N_EXPERTS].astype(I32)
    pcounts = (counts + BM - 1) // BM * BM
    pends = jnp.cumsum(pcounts)
    pstarts = pends - pcounts
    eid = rect[REC_EID0:REC_EID1 + 1].astype(I32)
    rank = rect[REC_RANK0:REC_RANK1 + 1].astype(I32)
    dest = (jnp.sum(jnp.where(eid[..., None] == jnp.arange(N_EXPERTS, dtype=I32), pstarts, 0), axis=-1)
            + rank).reshape(TOPK, N_TOK // SC_WINDOW, SC_WINDOW)
    nused = (pends[-1] // BM).astype(I32)
    blk = jnp.arange(N_BLK, dtype=I32)[:, None] * BM
    in_expert = (pstarts[None, :] <= blk) & (blk < pends[None, :])
    nvalid = jnp.clip(jnp.sum(jnp.where(in_expert, (pstarts + counts)[None, :] - blk, 0), axis=1), 0, BM)

    xs = _dispatch(h2p, dest)
    ys = _experts(pstarts // BM, pcounts // BM, nvalid.astype(I32), nused.reshape(1), xs,
                  w_exp_gate[0], w_exp_up[0], w_exp_down[0])
    dest_tok = dest.reshape(TOPK, N_TOK)
    wpg = w_ple_gate[0].astype(BF16)
    wple = w_ple[0].astype(BF16)
    out = None
    s0 = 0
    for n_steps in PART_STEPS:
        yg = _collect(ys, dest_tok[:, s0 * TM:(s0 + n_steps) * TM])
        out = _combine(s0, n_steps, x1, rec, yg, p, row(g_ple[0]), wpg, wple, row(g_final), out)
        s0 += n_steps
    return out
```
